```python
import jax, jax.numpy as jnp
from jax import lax
import numpy as np

D_MODEL = 1024
BATCH = 8
SEQ = 8192
DEPTH = 1

PLE_DIM = 256
D_FF = 2816
HG_HEADS = 8
HG_DK = 128
HG_DV = 128
HG_WIDTH = HG_HEADS * HG_DK
HG_VWIDTH = HG_HEADS * HG_DV
CHUNK = 64
POOL_WINDOWS = (2, 4, 8, 16)
POOL_GROUPS = 4
POOL_CH = 128
POOL_WIDTH = POOL_GROUPS * POOL_CH
IN_SIZES = (HG_WIDTH, HG_WIDTH, HG_VWIDTH, HG_VWIDTH, POOL_WIDTH, D_MODEL, D_MODEL)
IN_COLS = HG_WIDTH * 2 + HG_VWIDTH * 2 + POOL_WIDTH + 2 * D_MODEL
EPS = 1e-6

kernel_name = "hybrid_hgrn2_pool_macaron_block"


def _rmsnorm(x, g):
    xf = x.astype(jnp.float32)
    y = xf * lax.rsqrt(jnp.mean(xf * xf, axis=-1, keepdims=True) + EPS)
    return (y * g.astype(jnp.float32)).astype(x.dtype)


def _swiglu(h, w1, w3, w2):
    return (jax.nn.silu(h @ w1) * (h @ w3)) @ w2


def _hgrn2_chunked(q, k, v, log_f):
    B, S, H, DK = q.shape
    DV = v.shape[-1]
    n_chunks = S // CHUNK

    def to_chunks(t):
        return t.reshape(B, n_chunks, CHUNK, H, t.shape[-1]).transpose(1, 0, 3, 2, 4)

    qc, kc, vc, gc = to_chunks(q), to_chunks(k), to_chunks(v), to_chunks(log_f)
    causal = jnp.tril(jnp.ones((CHUNK, CHUNK), dtype=bool))[:, :, None]

    def step(state, inp):
        qb, kb, vb, gb = inp
        G = jnp.cumsum(gb, axis=2)
        diff = G[:, :, :, None, :] - G[:, :, None, :, :]
        decay = jnp.exp(jnp.where(causal, diff, -jnp.inf))
        scores = jnp.einsum('bhtk,bhsk,bhtsk->bhts', qb, kb, decay)
        o_intra = jnp.einsum('bhts,bhsv->bhtv', scores, vb)
        o_inter = jnp.einsum('bhtk,bhkv->bhtv', qb * jnp.exp(G), state)
        G_last = G[:, :, -1:, :]
        k_dec = kb * jnp.exp(G_last - G)
        new_state = (jnp.exp(G_last[:, :, 0, :])[..., None] * state
                     + jnp.einsum('bhsk,bhsv->bhkv', k_dec, vb))
        return new_state, o_intra + o_inter

    state0 = jnp.zeros((B, H, DK, DV), jnp.float32)
    _, oc = lax.scan(step, state0, (qc, kc, vc, gc))
    return oc.transpose(1, 0, 3, 2, 4).reshape(B, S, H, DV)


def _causal_multiscale_pool(u):
    B, S, G, C = u.shape
    uf = u.astype(jnp.float32)
    cs = jnp.concatenate([jnp.zeros((B, 1, G, C), jnp.float32), jnp.cumsum(uf, axis=1)], axis=1)
    pos = jnp.arange(1, S + 1, dtype=jnp.float32)
    outs = []
    for g, w in enumerate(POOL_WINDOWS):
        csg = cs[:, :, g]
        upper = csg[:, 1:]
        lower = jnp.concatenate([jnp.zeros((B, w - 1, C), jnp.float32), csg[:, :S - w + 1]], axis=1)
        count = jnp.minimum(pos, float(w))[None, :, None]
        outs.append((upper - lower) / count - uf[:, :, g])
    return jnp.stack(outs, axis=2).astype(u.dtype)


def _normal(key, shape, fan_in):
    return jax.random.normal(key, shape, jnp.float32) * (fan_in ** -0.5)


def _gain(key, shape):
    return 1.0 + 0.02 * jax.random.normal(key, shape, jnp.float32)


def _fwd_setup_inputs(seed: int = 0) -> dict:
    key = jax.random.key(seed)
    ks = jax.random.split(key, 26)
    L = DEPTH
    return {
        "x": jax.random.normal(ks[0], (BATCH, SEQ, D_MODEL), jnp.float32),
        "p": jax.random.normal(ks[1], (DEPTH, BATCH, SEQ, PLE_DIM), jnp.float32),
        "ffn1_norm": _gain(ks[2], (L, D_MODEL)),
        "ffn1_w1": _normal(ks[3], (L, D_MODEL, D_FF), D_MODEL),
        "ffn1_w3": _normal(ks[4], (L, D_MODEL, D_FF), D_MODEL),
        "ffn1_w2": _normal(ks[5], (L, D_FF, D_MODEL), D_FF),
        "mix_norm": _gain(ks[6], (L, D_MODEL)),
        "w_in": _normal(ks[7], (L, D_MODEL, IN_COLS), D_MODEL),
        "hgrn_lb": 0.1 * jax.random.normal(ks[8], (L + 1, HG_WIDTH), jnp.float32),
        "hgrn_onorm": _gain(ks[9], (L, HG_VWIDTH)),
        "w_branch_a": _normal(ks[10], (L, HG_VWIDTH, D_MODEL), HG_VWIDTH),
        "pool_w": _normal(ks[11], (L, POOL_GROUPS, POOL_CH, POOL_CH), POOL_CH),
        "pool_scale": _gain(ks[12], (L, POOL_WIDTH)),
        "w_branch_b": _normal(ks[13], (L, POOL_WIDTH, D_MODEL), POOL_WIDTH),
        "w_out": _normal(ks[14], (L, D_MODEL, D_MODEL), D_MODEL),
        "ffn2_norm": _gain(ks[15], (L, D_MODEL)),
        "ffn2_w1": _normal(ks[16], (L, D_MODEL, D_FF), D_MODEL),
        "ffn2_w3": _normal(ks[17], (L, D_MODEL, D_FF), D_MODEL),
        "ffn2_w2": _normal(ks[18], (L, D_FF, D_MODEL), D_FF),
        "ple_norm": _gain(ks[19], (L, D_MODEL)),
        "ple_w_gate": _normal(ks[20], (L, D_MODEL, D_MODEL), D_MODEL),
        "ple_w_proj": _normal(ks[21], (L, PLE_DIM, D_MODEL), PLE_DIM),
        "ple_post_norm": _gain(ks[22], (L, D_MODEL)),
        "final_norm": _gain(ks[23], (D_MODEL,)),
    }


def _fwd_reference(x, p, ffn1_norm, ffn1_w1, ffn1_w3, ffn1_w2, mix_norm, w_in, hgrn_lb, hgrn_onorm,
              w_branch_a, pool_w, pool_scale, w_branch_b, w_out, ffn2_norm, ffn2_w1, ffn2_w3,
              ffn2_w2, ple_norm, ple_w_gate, ple_w_proj, ple_post_norm, final_norm):
    B, S, _ = x.shape
    split_points = [int(v) for v in np.cumsum(IN_SIZES)[:-1]]
    lb_all = jnp.cumsum(jax.nn.softmax(hgrn_lb.astype(jnp.float32), axis=0), axis=0)

    for i in range(DEPTH):
        h = _rmsnorm(x, ffn1_norm[i])
        x = x + 0.5 * _swiglu(h, ffn1_w1[i], ffn1_w3[i], ffn1_w2[i])

        h = _rmsnorm(x, mix_norm[i])
        proj = h @ w_in[i]
        q_r, f_r, i_r, og_r, pool_r, ga_r, gb_r = jnp.split(proj, split_points, axis=-1)

        lb = lb_all[i]
        f = lb + (1.0 - lb) * jax.nn.sigmoid(f_r.astype(jnp.float32))
        log_f = jnp.log(f).reshape(B, S, HG_HEADS, HG_DK)
        k = (1.0 - f).reshape(B, S, HG_HEADS, HG_DK)
        q = jax.nn.silu(q_r.astype(jnp.float32)).reshape(B, S, HG_HEADS, HG_DK)
        v = i_r.astype(jnp.float32).reshape(B, S, HG_HEADS, HG_DV)
        o = _hgrn2_chunked(q, k, v, log_f).astype(x.dtype)
        o = _rmsnorm(o, hgrn_onorm[i].reshape(HG_HEADS, HG_DV)) * jax.nn.silu(og_r.reshape(B, S, HG_HEADS, HG_DV))
        y_a = o.reshape(B, S, HG_VWIDTH) @ w_branch_a[i]

        u = pool_r.reshape(B, S, POOL_GROUPS, POOL_CH)
        pooled = _causal_multiscale_pool(u)
        mixed = jnp.einsum('bsgc,gcd->bsgd', pooled, pool_w[i]).reshape(B, S, POOL_WIDTH) * pool_scale[i]
        y_b = mixed @ w_branch_b[i]

        y = jax.nn.sigmoid(ga_r) * y_a + jax.nn.sigmoid(gb_r) * y_b
        x = x + y @ w_out[i]

        h = _rmsnorm(x, ffn2_norm[i])
        x = x + 0.5 * _swiglu(h, ffn2_w1[i], ffn2_w3[i], ffn2_w2[i])

        gate = jax.nn.sigmoid(_rmsnorm(x, ple_norm[i]) @ ple_w_gate[i])
        e = _rmsnorm(p[i] @ ple_w_proj[i], ple_post_norm[i])
        x = x + gate * e

    return _rmsnorm(x, final_norm)


import jax as _jax
import jax.numpy as _jnp

TWIN_FORMAT = 'train_step'
FWD_PARAMS = ['x', 'p', 'ffn1_norm', 'ffn1_w1', 'ffn1_w3', 'ffn1_w2', 'mix_norm', 'w_in', 'hgrn_lb', 'hgrn_onorm', 'w_branch_a', 'pool_w', 'pool_scale', 'w_branch_b', 'w_out', 'ffn2_norm', 'ffn2_w1', 'ffn2_w3', 'ffn2_w2', 'ple_norm', 'ple_w_gate', 'ple_w_proj', 'ple_post_norm', 'final_norm']
TWIN_WEIGHTS = ['ffn1_norm', 'ffn1_w1', 'ffn1_w3', 'ffn1_w2', 'mix_norm', 'w_in', 'hgrn_lb', 'hgrn_onorm', 'w_branch_a', 'pool_w', 'pool_scale', 'w_branch_b', 'w_out', 'ffn2_norm', 'ffn2_w1', 'ffn2_w3', 'ffn2_w2', 'ple_norm', 'ple_w_gate', 'ple_w_proj', 'ple_post_norm', 'final_norm']
TWIN_DIFF_INPUT = 'x'
TWIN_INPUTS = ['x', 'p', 'ffn1_norm', 'ffn1_w1', 'ffn1_w3', 'ffn1_w2', 'mix_norm', 'w_in', 'hgrn_lb', 'hgrn_onorm', 'w_branch_a', 'pool_w', 'pool_scale', 'w_branch_b', 'w_out', 'ffn2_norm', 'ffn2_w1', 'ffn2_w3', 'ffn2_w2', 'ple_norm', 'ple_w_gate', 'ple_w_proj', 'ple_post_norm', 'final_norm', 'loss_target', 'm_ffn1_norm', 'm_ffn1_w1', 'm_ffn1_w3', 'm_ffn1_w2', 'm_mix_norm', 'm_w_in', 'm_hgrn_lb', 'm_hgrn_onorm', 'm_w_branch_a', 'm_pool_w', 'm_pool_scale', 'm_w_branch_b', 'm_w_out', 'm_ffn2_norm', 'm_ffn2_w1', 'm_ffn2_w3', 'm_ffn2_w2', 'm_ple_norm', 'm_ple_w_gate', 'm_ple_w_proj', 'm_ple_post_norm', 'm_final_norm', 'v_ffn1_norm', 'v_ffn1_w1', 'v_ffn1_w3', 'v_ffn1_w2', 'v_mix_norm', 'v_w_in', 'v_hgrn_lb', 'v_hgrn_onorm', 'v_w_branch_a', 'v_pool_w', 'v_pool_scale', 'v_w_branch_b', 'v_w_out', 'v_ffn2_norm', 'v_ffn2_w1', 'v_ffn2_w3', 'v_ffn2_w2', 'v_ple_norm', 'v_ple_w_gate', 'v_ple_w_proj', 'v_ple_post_norm', 'v_final_norm']
TWIN_OUTPUTS = ['loss', 'grad_x', 'grad_ffn1_norm', 'grad_ffn1_w1', 'grad_ffn1_w3', 'grad_ffn1_w2', 'grad_mix_norm', 'grad_w_in', 'grad_hgrn_lb', 'grad_hgrn_onorm', 'grad_w_branch_a', 'grad_pool_w', 'grad_pool_scale', 'grad_w_branch_b', 'grad_w_out', 'grad_ffn2_norm', 'grad_ffn2_w1', 'grad_ffn2_w3', 'grad_ffn2_w2', 'grad_ple_norm', 'grad_ple_w_gate', 'grad_ple_w_proj', 'grad_ple_post_norm', 'grad_final_norm', 'delta_ffn1_norm', 'delta_ffn1_w1', 'delta_ffn1_w3', 'delta_ffn1_w2', 'delta_mix_norm', 'delta_w_in', 'delta_hgrn_lb', 'delta_hgrn_onorm', 'delta_w_branch_a', 'delta_pool_w', 'delta_pool_scale', 'delta_w_branch_b', 'delta_w_out', 'delta_ffn2_norm', 'delta_ffn2_w1', 'delta_ffn2_w3', 'delta_ffn2_w2', 'delta_ple_norm', 'delta_ple_w_gate', 'delta_ple_w_proj', 'delta_ple_post_norm', 'delta_final_norm', 'new_m_ffn1_norm', 'new_m_ffn1_w1', 'new_m_ffn1_w3', 'new_m_ffn1_w2', 'new_m_mix_norm', 'new_m_w_in', 'new_m_hgrn_lb', 'new_m_hgrn_onorm', 'new_m_w_branch_a', 'new_m_pool_w', 'new_m_pool_scale', 'new_m_w_branch_b', 'new_m_w_out', 'new_m_ffn2_norm', 'new_m_ffn2_w1', 'new_m_ffn2_w3', 'new_m_ffn2_w2', 'new_m_ple_norm', 'new_m_ple_w_gate', 'new_m_ple_w_proj', 'new_m_ple_post_norm', 'new_m_final_norm', 'new_v_ffn1_norm', 'new_v_ffn1_w1', 'new_v_ffn1_w3', 'new_v_ffn1_w2', 'new_v_mix_norm', 'new_v_w_in', 'new_v_hgrn_lb', 'new_v_hgrn_onorm', 'new_v_w_branch_a', 'new_v_pool_w', 'new_v_pool_scale', 'new_v_w_branch_b', 'new_v_w_out', 'new_v_ffn2_norm', 'new_v_ffn2_w1', 'new_v_ffn2_w3', 'new_v_ffn2_w2', 'new_v_ple_norm', 'new_v_ple_w_gate', 'new_v_ple_w_proj', 'new_v_ple_post_norm', 'new_v_final_norm']
TWIN_LEAF_KINDS = {'loss': 'loss', 'grad_x': 'grad_x', 'grad_ffn1_norm': 'grad_w', 'grad_ffn1_w1': 'grad_w', 'grad_ffn1_w3': 'grad_w', 'grad_ffn1_w2': 'grad_w', 'grad_mix_norm': 'grad_w', 'grad_w_in': 'grad_w', 'grad_hgrn_lb': 'grad_w', 'grad_hgrn_onorm': 'grad_w', 'grad_w_branch_a': 'grad_w', 'grad_pool_w': 'grad_w', 'grad_pool_scale': 'grad_w', 'grad_w_branch_b': 'grad_w', 'grad_w_out': 'grad_w', 'grad_ffn2_norm': 'grad_w', 'grad_ffn2_w1': 'grad_w', 'grad_ffn2_w3': 'grad_w', 'grad_ffn2_w2': 'grad_w', 'grad_ple_norm': 'grad_w', 'grad_ple_w_gate': 'grad_w', 'grad_ple_w_proj': 'grad_w', 'grad_ple_post_norm': 'grad_w', 'grad_final_norm': 'grad_w', 'delta_ffn1_norm': 'delta_w', 'delta_ffn1_w1': 'delta_w', 'delta_ffn1_w3': 'delta_w', 'delta_ffn1_w2': 'delta_w', 'delta_mix_norm': 'delta_w', 'delta_w_in': 'delta_w', 'delta_hgrn_lb': 'delta_w', 'delta_hgrn_onorm': 'delta_w', 'delta_w_branch_a': 'delta_w', 'delta_pool_w': 'delta_w', 'delta_pool_scale': 'delta_w', 'delta_w_branch_b': 'delta_w', 'delta_w_out': 'delta_w', 'delta_ffn2_norm': 'delta_w', 'delta_ffn2_w1': 'delta_w', 'delta_ffn2_w3': 'delta_w', 'delta_ffn2_w2': 'delta_w', 'delta_ple_norm': 'delta_w', 'delta_ple_w_gate': 'delta_w', 'delta_ple_w_proj': 'delta_w', 'delta_ple_post_norm': 'delta_w', 'delta_final_norm': 'delta_w', 'new_m_ffn1_norm': 'new_m', 'new_m_ffn1_w1': 'new_m', 'new_m_ffn1_w3': 'new_m', 'new_m_ffn1_w2': 'new_m', 'new_m_mix_norm': 'new_m', 'new_m_w_in': 'new_m', 'new_m_hgrn_lb': 'new_m', 'new_m_hgrn_onorm': 'new_m', 'new_m_w_branch_a': 'new_m', 'new_m_pool_w': 'new_m', 'new_m_pool_scale': 'new_m', 'new_m_w_branch_b': 'new_m', 'new_m_w_out': 'new_m', 'new_m_ffn2_norm': 'new_m', 'new_m_ffn2_w1': 'new_m', 'new_m_ffn2_w3': 'new_m', 'new_m_ffn2_w2': 'new_m', 'new_m_ple_norm': 'new_m', 'new_m_ple_w_gate': 'new_m', 'new_m_ple_w_proj': 'new_m', 'new_m_ple_post_norm': 'new_m', 'new_m_final_norm': 'new_m', 'new_v_ffn1_norm': 'new_v', 'new_v_ffn1_w1': 'new_v', 'new_v_ffn1_w3': 'new_v', 'new_v_ffn1_w2': 'new_v', 'new_v_mix_norm': 'new_v', 'new_v_w_in': 'new_v', 'new_v_hgrn_lb': 'new_v', 'new_v_hgrn_onorm': 'new_v', 'new_v_w_branch_a': 'new_v', 'new_v_pool_w': 'new_v', 'new_v_pool_scale': 'new_v', 'new_v_w_branch_b': 'new_v', 'new_v_w_out': 'new_v', 'new_v_ffn2_norm': 'new_v', 'new_v_ffn2_w1': 'new_v', 'new_v_ffn2_w3': 'new_v', 'new_v_ffn2_w2': 'new_v', 'new_v_ple_norm': 'new_v', 'new_v_ple_w_gate': 'new_v', 'new_v_ple_w_proj': 'new_v', 'new_v_ple_post_norm': 'new_v', 'new_v_final_norm': 'new_v'}


def _forward(args):
    return _fwd_reference(*[args[k] for k in FWD_PARAMS])


def _output_shape():
    def fwd():
        inp = _fwd_setup_inputs(0)
        return _fwd_reference(*[inp[k] for k in FWD_PARAMS])
    out = _jax.eval_shape(fwd)
    return out.shape, out.dtype

N_MICROBATCH = 1
ADAM_LR = 0.001
ADAM_B1 = 0.9
ADAM_B2 = 0.999
ADAM_EPS = 1e-08
ADAM_WD = 0.01
ADAM_STEP = 10
PER_EXAMPLE_BATCH_AXIS = {'x': 0, 'p': 1, 'loss_target': 0}
SHARED_INPUTS = []
_WEIGHT_DTYPES = {'ffn1_norm': _jnp.float32, 'ffn1_w1': _jnp.float32, 'ffn1_w3': _jnp.float32, 'ffn1_w2': _jnp.float32, 'mix_norm': _jnp.float32, 'w_in': _jnp.float32, 'hgrn_lb': _jnp.float32, 'hgrn_onorm': _jnp.float32, 'w_branch_a': _jnp.float32, 'pool_w': _jnp.float32, 'pool_scale': _jnp.float32, 'w_branch_b': _jnp.float32, 'w_out': _jnp.float32, 'ffn2_norm': _jnp.float32, 'ffn2_w1': _jnp.float32, 'ffn2_w3': _jnp.float32, 'ffn2_w2': _jnp.float32, 'ple_norm': _jnp.float32, 'ple_w_gate': _jnp.float32, 'ple_w_proj': _jnp.float32, 'ple_post_norm': _jnp.float32, 'final_norm': _jnp.float32}
MOMENT_SCALE = {'ffn1_norm': 1.076919e-01, 'ffn1_w1': 4.505063e-02, 'ffn1_w3': 4.368950e-02, 'ffn1_w2': 7.233203e-02, 'mix_norm': 1.437389e-01, 'w_in': 5.555853e-02, 'hgrn_lb': 6.635217e-03, 'hgrn_onorm': 6.751074e-02, 'w_branch_a': 6.509211e-02, 'pool_w': 1.353770e-01, 'pool_scale': 1.380025e-01, 'w_branch_b': 9.536175e-02, 'w_out': 1.161344e-01, 'ffn2_norm': 8.654267e-02, 'ffn2_w1': 3.520898e-02, 'ffn2_w3': 3.409844e-02, 'ffn2_w2': 5.650231e-02, 'ple_norm': 3.938302e-02, 'ple_w_gate': 3.947896e-02, 'ple_w_proj': 1.021029e-01, 'ple_post_norm': 2.895676e-01, 'final_norm': 6.391258e+01}


def _to_microbatches(a, axis):
    t = _jnp.moveaxis(a, axis, 0)
    t = t.reshape((N_MICROBATCH, t.shape[0] // N_MICROBATCH) + t.shape[1:])
    return _jnp.moveaxis(t, 1, axis + 1)


def setup_inputs(seed: int = 0) -> dict:
    inp = _fwd_setup_inputs(seed)
    key = _jax.random.fold_in(_jax.random.key(seed), 7919)
    shape, _ = _output_shape()
    out = dict(inp)
    out["loss_target"] = _jax.random.normal(_jax.random.fold_in(key, 0), shape, _jnp.float32)
    for i, name in enumerate(TWIN_WEIGHTS):
        w = inp[name].astype(_jnp.float32)
        if MOMENT_SCALE is None:
            s = _jnp.sqrt(_jnp.mean(_jnp.square(w)) + 1e-30)
        else:
            s = MOMENT_SCALE[name]
        km, kv = _jax.random.split(_jax.random.fold_in(key, i + 1))
        out[name] = w
        out["m_" + name] = s * _jax.random.normal(km, w.shape, _jnp.float32)
        out["v_" + name] = (s * s) * _jax.random.uniform(kv, w.shape, _jnp.float32, 0.5, 1.5)
    if N_MICROBATCH > 1:
        for name, axis in PER_EXAMPLE_BATCH_AXIS.items():
            out[name] = _to_microbatches(out[name], axis)
    return {'x': out['x'], 'p': out['p'], 'ffn1_norm': out['ffn1_norm'], 'ffn1_w1': out['ffn1_w1'], 'ffn1_w3': out['ffn1_w3'], 'ffn1_w2': out['ffn1_w2'], 'mix_norm': out['mix_norm'], 'w_in': out['w_in'], 'hgrn_lb': out['hgrn_lb'], 'hgrn_onorm': out['hgrn_onorm'], 'w_branch_a': out['w_branch_a'], 'pool_w': out['pool_w'], 'pool_scale': out['pool_scale'], 'w_branch_b': out['w_branch_b'], 'w_out': out['w_out'], 'ffn2_norm': out['ffn2_norm'], 'ffn2_w1': out['ffn2_w1'], 'ffn2_w3': out['ffn2_w3'], 'ffn2_w2': out['ffn2_w2'], 'ple_norm': out['ple_norm'], 'ple_w_gate': out['ple_w_gate'], 'ple_w_proj': out['ple_w_proj'], 'ple_post_norm': out['ple_post_norm'], 'final_norm': out['final_norm'], 'loss_target': out['loss_target'], 'm_ffn1_norm': out['m_ffn1_norm'], 'm_ffn1_w1': out['m_ffn1_w1'], 'm_ffn1_w3': out['m_ffn1_w3'], 'm_ffn1_w2': out['m_ffn1_w2'], 'm_mix_norm': out['m_mix_norm'], 'm_w_in': out['m_w_in'], 'm_hgrn_lb': out['m_hgrn_lb'], 'm_hgrn_onorm': out['m_hgrn_onorm'], 'm_w_branch_a': out['m_w_branch_a'], 'm_pool_w': out['m_pool_w'], 'm_pool_scale': out['m_pool_scale'], 'm_w_branch_b': out['m_w_branch_b'], 'm_w_out': out['m_w_out'], 'm_ffn2_norm': out['m_ffn2_norm'], 'm_ffn2_w1': out['m_ffn2_w1'], 'm_ffn2_w3': out['m_ffn2_w3'], 'm_ffn2_w2': out['m_ffn2_w2'], 'm_ple_norm': out['m_ple_norm'], 'm_ple_w_gate': out['m_ple_w_gate'], 'm_ple_w_proj': out['m_ple_w_proj'], 'm_ple_post_norm': out['m_ple_post_norm'], 'm_final_norm': out['m_final_norm'], 'v_ffn1_norm': out['v_ffn1_norm'], 'v_ffn1_w1': out['v_ffn1_w1'], 'v_ffn1_w3': out['v_ffn1_w3'], 'v_ffn1_w2': out['v_ffn1_w2'], 'v_mix_norm': out['v_mix_norm'], 'v_w_in': out['v_w_in'], 'v_hgrn_lb': out['v_hgrn_lb'], 'v_hgrn_onorm': out['v_hgrn_onorm'], 'v_w_branch_a': out['v_w_branch_a'], 'v_pool_w': out['v_pool_w'], 'v_pool_scale': out['v_pool_scale'], 'v_w_branch_b': out['v_w_branch_b'], 'v_w_out': out['v_w_out'], 'v_ffn2_norm': out['v_ffn2_norm'], 'v_ffn2_w1': out['v_ffn2_w1'], 'v_ffn2_w3': out['v_ffn2_w3'], 'v_ffn2_w2': out['v_ffn2_w2'], 'v_ple_norm': out['v_ple_norm'], 'v_ple_w_gate': out['v_ple_w_gate'], 'v_ple_w_proj': out['v_ple_w_proj'], 'v_ple_post_norm': out['v_ple_post_norm'], 'v_final_norm': out['v_final_norm']}


def _loss(weights, diff, rest, loss_target):
    with _jax.named_scope("forward"):
        args = {**rest, TWIN_DIFF_INPUT: diff, **{k: w.astype(_WEIGHT_DTYPES[k]) for k, w in weights.items()}}
        y = _forward(args)
    with _jax.named_scope("loss_head"):
        err = _jnp.square(y.astype(_jnp.float32) - loss_target)
        return 0.5 * _jnp.sum(_jnp.mean(err, axis=-1)) if err.ndim else 0.5 * err


def _adamw(w, g, m, v):
    m = ADAM_B1 * m + (1.0 - ADAM_B1) * g
    v = ADAM_B2 * v + (1.0 - ADAM_B2) * _jnp.square(g)
    m_hat = m / (1.0 - ADAM_B1 ** ADAM_STEP)
    v_hat = v / (1.0 - ADAM_B2 ** ADAM_STEP)
    delta = -ADAM_LR * (m_hat / (_jnp.sqrt(v_hat) + ADAM_EPS) + ADAM_WD * w)
    return delta, m, v


def reference(x, p, ffn1_norm, ffn1_w1, ffn1_w3, ffn1_w2, mix_norm, w_in, hgrn_lb, hgrn_onorm, w_branch_a, pool_w, pool_scale, w_branch_b, w_out, ffn2_norm, ffn2_w1, ffn2_w3, ffn2_w2, ple_norm, ple_w_gate, ple_w_proj, ple_post_norm, final_norm, loss_target, m_ffn1_norm, m_ffn1_w1, m_ffn1_w3, m_ffn1_w2, m_mix_norm, m_w_in, m_hgrn_lb, m_hgrn_onorm, m_w_branch_a, m_pool_w, m_pool_scale, m_w_branch_b, m_w_out, m_ffn2_norm, m_ffn2_w1, m_ffn2_w3, m_ffn2_w2, m_ple_norm, m_ple_w_gate, m_ple_w_proj, m_ple_post_norm, m_final_norm, v_ffn1_norm, v_ffn1_w1, v_ffn1_w3, v_ffn1_w2, v_mix_norm, v_w_in, v_hgrn_lb, v_hgrn_onorm, v_w_branch_a, v_pool_w, v_pool_scale, v_w_branch_b, v_w_out, v_ffn2_norm, v_ffn2_w1, v_ffn2_w3, v_ffn2_w2, v_ple_norm, v_ple_w_gate, v_ple_w_proj, v_ple_post_norm, v_final_norm):
    given = dict(x=x, p=p, ffn1_norm=ffn1_norm, ffn1_w1=ffn1_w1, ffn1_w3=ffn1_w3, ffn1_w2=ffn1_w2, mix_norm=mix_norm, w_in=w_in, hgrn_lb=hgrn_lb, hgrn_onorm=hgrn_onorm, w_branch_a=w_branch_a, pool_w=pool_w, pool_scale=pool_scale, w_branch_b=w_branch_b, w_out=w_out, ffn2_norm=ffn2_norm, ffn2_w1=ffn2_w1, ffn2_w3=ffn2_w3, ffn2_w2=ffn2_w2, ple_norm=ple_norm, ple_w_gate=ple_w_gate, ple_w_proj=ple_w_proj, ple_post_norm=ple_post_norm, final_norm=final_norm, loss_target=loss_target, m_ffn1_norm=m_ffn1_norm, m_ffn1_w1=m_ffn1_w1, m_ffn1_w3=m_ffn1_w3, m_ffn1_w2=m_ffn1_w2, m_mix_norm=m_mix_norm, m_w_in=m_w_in, m_hgrn_lb=m_hgrn_lb, m_hgrn_onorm=m_hgrn_onorm, m_w_branch_a=m_w_branch_a, m_pool_w=m_pool_w, m_pool_scale=m_pool_scale, m_w_branch_b=m_w_branch_b, m_w_out=m_w_out, m_ffn2_norm=m_ffn2_norm, m_ffn2_w1=m_ffn2_w1, m_ffn2_w3=m_ffn2_w3, m_ffn2_w2=m_ffn2_w2, m_ple_norm=m_ple_norm, m_ple_w_gate=m_ple_w_gate, m_ple_w_proj=m_ple_w_proj, m_ple_post_norm=m_ple_post_norm, m_final_norm=m_final_norm, v_ffn1_norm=v_ffn1_norm, v_ffn1_w1=v_ffn1_w1, v_ffn1_w3=v_ffn1_w3, v_ffn1_w2=v_ffn1_w2, v_mix_norm=v_mix_norm, v_w_in=v_w_in, v_hgrn_lb=v_hgrn_lb, v_hgrn_onorm=v_hgrn_onorm, v_w_branch_a=v_w_branch_a, v_pool_w=v_pool_w, v_pool_scale=v_pool_scale, v_w_branch_b=v_w_branch_b, v_w_out=v_w_out, v_ffn2_norm=v_ffn2_norm, v_ffn2_w1=v_ffn2_w1, v_ffn2_w3=v_ffn2_w3, v_ffn2_w2=v_ffn2_w2, v_ple_norm=v_ple_norm, v_ple_w_gate=v_ple_w_gate, v_ple_w_proj=v_ple_w_proj, v_ple_post_norm=v_ple_post_norm, v_final_norm=v_final_norm)
    weights = {n: given[n] for n in TWIN_WEIGHTS}
    shared = {n: given[n] for n in SHARED_INPUTS}
    per_example = {n: given[n] for n in ['x', 'p']}
    grad_fn = _jax.value_and_grad(_loss, argnums=(0, 1))

    def one_microbatch(ex, loss_target):
        ex = dict(ex)
        diff = ex.pop(TWIN_DIFF_INPUT)
        return grad_fn(weights, diff, {**shared, **ex}, loss_target)

    if N_MICROBATCH == 1:
        loss, (grad_w, grad_x) = one_microbatch(per_example, given["loss_target"])
    else:
        def body(carry, xs):
            loss_sum, grad_sum = carry
            l_k, (gw_k, gx_k) = one_microbatch(xs[0], xs[1])
            with _jax.named_scope("update"):
                return (loss_sum + l_k, _jax.tree.map(_jnp.add, grad_sum, gw_k)), gx_k

        init = (_jnp.zeros((), _jnp.float32), _jax.tree.map(_jnp.zeros_like, weights))
        (loss, grad_w), grad_x = _jax.lax.scan(body, init, (per_example, given["loss_target"]))
    with _jax.named_scope("update"):
        delta_w, new_m, new_v = {}, {}, {}
        for n in TWIN_WEIGHTS:
            delta_w[n], new_m[n], new_v[n] = _adamw(weights[n], grad_w[n], given["m_" + n], given["v_" + n])
    return (loss, grad_x, *[grad_w[n] for n in TWIN_WEIGHTS], *[delta_w[n] for n in TWIN_WEIGHTS],
            *[new_m[n] for n in TWIN_WEIGHTS], *[new_v[n] for n in TWIN_WEIGHTS])
```

```python
import functools

import jax
import jax.numpy as jnp
from jax import lax
from jax.experimental import pallas as pl
from jax.experimental.pallas import tpu as pltpu

F32 = jnp.float32
BF16 = jnp.bfloat16
MESH = pl.DeviceIdType.MESH

D_MODEL = 1024
D_FF = 2816
HEADS = 8
HEAD_DIM = 128
POOL_WIDTH = 512
POOL_WINDOWS = (2, 4, 8, 16)
POOL_HALO = 16
N_CHIPS = 4
EPS = 1e-6
CHUNK = 64
MAIN_COLS = 4096
GATE_COLS = 2048
SHARD_IN_COLS = 1664

ADAM_LR = 0.001
ADAM_B1 = 0.9
ADAM_B2 = 0.999
ADAM_EPS = 1e-08
ADAM_WD = 0.01
ADAM_STEP = 10

VMEM_LIMIT = 56 * 1024 * 1024
WGRAD_IN_BLOCKS = 4
WGRAD_OUT_BLOCKS = 2


def _params(semantics=None, vmem=VMEM_LIMIT):
    return pltpu.CompilerParams(dimension_semantics=semantics, vmem_limit_bytes=vmem)


def _dot(a, b):
    return jnp.dot(a, b, preferred_element_type=F32)


def _dot_nt(a, b):
    return lax.dot_general(a, b, (((1,), (1,)), ((), ())), preferred_element_type=F32)


def _dot_tn(a, b):
    return lax.dot_general(a, b, (((0,), (0,)), ((), ())), preferred_element_type=F32)


def _dot_exact(a, b):
    return jnp.dot(a, b, preferred_element_type=F32, precision=lax.Precision.HIGHEST)


def _sigmoid(x):
    return jax.nn.sigmoid(x)


def _resident(shape):
    zeros = (0,) * len(shape)
    return pl.BlockSpec(shape, lambda *_: zeros, pipeline_mode=pl.Buffered(1))


def _pick(shape, k):
    zeros = (0,) * (len(shape) - 1)
    return pl.BlockSpec((None,) + tuple(shape[1:]), lambda *_: (k,) + zeros, pipeline_mode=pl.Buffered(1))


def _rows(tm, cols, col_block=0):
    return pl.BlockSpec((tm, cols), lambda i: (i, col_block))


def _acc(shape):
    zeros = (0,) * len(shape)
    return pl.BlockSpec(shape, lambda *_: zeros)


def _rms(x):
    r = lax.rsqrt(jnp.mean(x * x, axis=-1, keepdims=True) + EPS)
    return r, x * r


def _rms_bwd(dn, n, r):
    return r * (dn - n * jnp.mean(dn * n, axis=-1, keepdims=True))


def _colsum(a):
    return jnp.sum(a, axis=0, keepdims=True)


def _ffn_fwd(x, g, ffnw, base, tm):
    t = x.shape[0]

    def body(x_ref, g_ref, w1_ref, w3_ref, w2_ref, xo_ref, a_ref, b_ref):
        xv = x_ref[...]
        _, n = _rms(xv)
        h = (n * g_ref[...]).astype(BF16)
        a = _dot_nt(h, w1_ref[...])
        b = _dot_nt(h, w3_ref[...])
        s = (a * _sigmoid(a) * b).astype(BF16)
        xo_ref[...] = xv + 0.5 * _dot(s, w2_ref[...])
        a_ref[...] = a.astype(BF16)
        b_ref[...] = b.astype(BF16)

    return pl.pallas_call(
        body, name=f"ffn_fwd_{base}", grid=(t // tm,),
        in_specs=[_rows(tm, D_MODEL), _resident((1, D_MODEL)), _pick(ffnw.shape, base), _pick(ffnw.shape, base + 1),
                  _pick(ffnw.shape, base + 2)],
        out_specs=[_rows(tm, D_MODEL), _rows(tm, D_FF), _rows(tm, D_FF)],
        out_shape=[jax.ShapeDtypeStruct((t, D_MODEL), F32), jax.ShapeDtypeStruct((t, D_FF), BF16),
                   jax.ShapeDtypeStruct((t, D_FF), BF16)],
        compiler_params=_params(("parallel",)),
    )(x, g, ffnw, ffnw, ffnw)


def _ffn_bwd(dxo, x, g, a, b, ffnw, base, tm):
    t = x.shape[0]

    def body(dxo_ref, x_ref, g_ref, a_ref, b_ref, w1_ref, w3_ref, w2_ref, dx_ref, dab_ref, s_ref, h_ref, dxh_ref, dg_ref):
        @pl.when(pl.program_id(0) == 0)
        def _():
            dg_ref[...] = jnp.zeros_like(dg_ref)

        xv = x_ref[...]
        gv = g_ref[...]
        r, n = _rms(xv)
        h_ref[...] = (n * gv).astype(BF16)
        dxo_v = dxo_ref[...]
        dxh = (0.5 * dxo_v).astype(BF16)
        dxh_ref[...] = dxh
        ds = _dot_nt(dxh, w2_ref[...])
        av = a_ref[...].astype(F32)
        bv = b_ref[...].astype(F32)
        sg = _sigmoid(av)
        silu = av * sg
        s_ref[...] = (silu * bv).astype(BF16)
        da = (ds * bv * (sg * (1.0 + av * (1.0 - sg)))).astype(BF16)
        db = (ds * silu).astype(BF16)
        dab_ref[:, :D_FF] = da
        dab_ref[:, D_FF:] = db
        dh = _dot(da, w1_ref[...]) + _dot(db, w3_ref[...])
        dg_ref[...] += _colsum(dh * n)
        dx_ref[...] = dxo_v + _rms_bwd(dh * gv, n, r)

    return pl.pallas_call(
        body, name=f"ffn_bwd_{base}", grid=(t // tm,),
        in_specs=[_rows(tm, D_MODEL), _rows(tm, D_MODEL), _resident((1, D_MODEL)), _rows(tm, D_FF), _rows(tm, D_FF),
                  _pick(ffnw.shape, base), _pick(ffnw.shape, base + 1), _pick(ffnw.shape, base + 2)],
        out_specs=[_rows(tm, D_MODEL), _rows(tm, 2 * D_FF), _rows(tm, D_FF), _rows(tm, D_MODEL), _rows(tm, D_MODEL),
                   _acc((1, D_MODEL))],
        out_shape=[jax.ShapeDtypeStruct((t, D_MODEL), F32), jax.ShapeDtypeStruct((t, 2 * D_FF), BF16),
                   jax.ShapeDtypeStruct((t, D_FF), BF16), jax.ShapeDtypeStruct((t, D_MODEL), BF16),
                   jax.ShapeDtypeStruct((t, D_MODEL), BF16), jax.ShapeDtypeStruct((1, D_MODEL), F32)],
        compiler_params=_params(("arbitrary",)),
    )(dxo, x, g, a, b, ffnw, ffnw, ffnw)


def _wgrad(xm, dy, out_blocks, name, tk):
    t, m = xm.shape
    n = dy.shape[1]
    mb = m // out_blocks

    def body(x_ref, dy_ref, o_ref):
        @pl.when(pl.program_id(1) == 0)
        def _():
            o_ref[...] = jnp.zeros_like(o_ref)

        o_ref[...] += _dot_tn(x_ref[...], dy_ref[...])

    return pl.pallas_call(
        body, name=name, grid=(out_blocks, t // tk),
        in_specs=[pl.BlockSpec((tk, mb), lambda j, k: (k, j)), pl.BlockSpec((tk, n), lambda j, k: (k, 0))],
        out_specs=pl.BlockSpec((None, mb, n), lambda j, k: (j, 0, 0)),
        out_shape=jax.ShapeDtypeStruct((out_blocks, mb, n), F32),
        compiler_params=_params(("parallel", "arbitrary")),
    )(xm, dy)


def _wgrad_cols(xm, dy, out_blocks, name, tk):
    t, m = xm.shape
    n = dy.shape[1]
    nb = n // out_blocks

    def body(x_ref, dy_ref, o_ref):
        @pl.when(pl.program_id(1) == 0)
        def _():
            o_ref[...] = jnp.zeros_like(o_ref)

        o_ref[...] += _dot_tn(x_ref[...], dy_ref[...])

    return pl.pallas_call(
        body, name=name, grid=(out_blocks, t // tk),
        in_specs=[pl.BlockSpec((tk, m), lambda j, k: (k, 0)), pl.BlockSpec((tk, nb), lambda j, k: (k, j))],
        out_specs=pl.BlockSpec((None, m, nb), lambda j, k: (j, 0, 0)),
        out_shape=jax.ShapeDtypeStruct((out_blocks, m, nb), F32),
        compiler_params=_params(("parallel", "arbitrary")),
    )(xm, dy)


def _mix_fwd(x1, g, winw, tm):
    t = x1.shape[0]

    def body(x_ref, g_ref, w_ref, main_ref, pool_ref, gate_ref):
        _, n = _rms(x_ref[...])
        h = (n * g_ref[...]).astype(BF16)
        proj = jnp.concatenate([_dot(h, w_ref[j]) for j in range(N_CHIPS)], axis=1)
        main_ref[...] = proj[:, :MAIN_COLS]
        pool_ref[...] = proj[:, MAIN_COLS:MAIN_COLS + POOL_WIDTH]
        gate_ref[...] = proj[:, MAIN_COLS + POOL_WIDTH:]

    return pl.pallas_call(
        body, name="mix_fwd", grid=(t // tm,),
        in_specs=[_rows(tm, D_MODEL), _resident((1, D_MODEL)), _resident(winw.shape)],
        out_specs=[_rows(tm, MAIN_COLS), _rows(tm, POOL_WIDTH), _rows(tm, GATE_COLS)],
        out_shape=[jax.ShapeDtypeStruct((t, MAIN_COLS), F32), jax.ShapeDtypeStruct((t, POOL_WIDTH), F32),
                   jax.ShapeDtypeStruct((t, GATE_COLS), F32)],
        compiler_params=_params(("parallel",)),
    )(x1, g, winw)


def _mix_bwd(dqfi, dog, du, dgates, dx2, x1, g, winw, tm):
    t = x1.shape[0]
    cols = N_CHIPS * SHARD_IN_COLS

    def body(dqfi_ref, dog_ref, du_ref, dgt_ref, dx2_ref, x_ref, g_ref, w_ref, dx_ref, dproj_ref, h_ref, dg_ref):
        @pl.when(pl.program_id(0) == 0)
        def _():
            dg_ref[...] = jnp.zeros_like(dg_ref)

        dproj = jnp.concatenate([dqfi_ref[...], dog_ref[...], du_ref[...], dgt_ref[...]], axis=1)
        dproj_ref[...] = dproj
        dh = _dot_nt(dproj[:, :SHARD_IN_COLS], w_ref[0])
        for j in range(1, N_CHIPS):
            dh += _dot_nt(dproj[:, j * SHARD_IN_COLS:(j + 1) * SHARD_IN_COLS], w_ref[j])
        gv = g_ref[...]
        r, n = _rms(x_ref[...])
        h_ref[...] = (n * gv).astype(BF16)
        dg_ref[...] += _colsum(dh * n)
        dx_ref[...] = dx2_ref[...] + _rms_bwd(dh * gv, n, r)

    return pl.pallas_call(
        body, name="mix_bwd", grid=(t // tm,),
        in_specs=[_rows(tm, 3 * D_MODEL), _rows(tm, D_MODEL), _rows(tm, POOL_WIDTH), _rows(tm, GATE_COLS),
                  _rows(tm, D_MODEL), _rows(tm, D_MODEL), _resident((1, D_MODEL)), _resident(winw.shape)],
        out_specs=[_rows(tm, D_MODEL), _rows(tm, cols), _rows(tm, D_MODEL), _acc((1, D_MODEL))],
        out_shape=[jax.ShapeDtypeStruct((t, D_MODEL), F32), jax.ShapeDtypeStruct((t, cols), BF16),
                   jax.ShapeDtypeStruct((t, D_MODEL), BF16), jax.ShapeDtypeStruct((1, D_MODEL), F32)],
        compiler_params=_params(("arbitrary",)),
    )(dqfi, dog, du, dgates, dx2, x1, g, winw)


def _lower_bound(lb_raw):
    l0 = lb_raw[0:1, :]
    l1 = lb_raw[1:2, :]
    m = jnp.maximum(l0, l1)
    e0 = jnp.exp(l0 - m)
    e1 = jnp.exp(l1 - m)
    return e0 / (e0 + e1)


def _chunk_gates(qr, fr, lb, g_scr, tril_f):
    sg = _sigmoid(fr)
    f = lb + (1.0 - lb) * sg
    k = 1.0 - f
    sq = _sigmoid(qr)
    q = qr * sq
    g_scr[...] = _dot_exact(tril_f, jnp.log(f))
    gc = g_scr[...]
    gm = g_scr[CHUNK // 2 - 1:CHUNK // 2, :]
    gl = g_scr[CHUNK - 1:CHUNK, :]
    e_q = jnp.exp(gc - gm)
    e_k = jnp.exp(gm - gc)
    e_in = jnp.exp(gc)
    e_out = jnp.exp(gl - gc)
    return dict(sg=sg, f=f, k=k, sq=sq, q=q, e_q=e_q, e_k=e_k, e_in=e_in, e_out=e_out, e_last=jnp.exp(gl))


def _hgrn_fwd(main, lb_raw, tt):
    t = main.shape[0]
    n_local = tt // CHUNK

    def body(q_ref, f_ref, i_ref, lb_ref, o_ref, st_ref, s_scr, g_scr):
        @pl.when(pl.program_id(0) == 0)
        def _():
            s_scr[...] = jnp.zeros_like(s_scr)

        lb = _lower_bound(lb_ref[...])
        row = lax.broadcasted_iota(jnp.int32, (CHUNK, CHUNK), 0)
        col = lax.broadcasted_iota(jnp.int32, (CHUNK, CHUNK), 1)
        tril = row >= col
        tril_f = tril.astype(F32)

        def chunk(c, carry):
            rows = pl.ds(pl.multiple_of(c * CHUNK, CHUNK), CHUNK)
            z = _chunk_gates(q_ref[rows, :], f_ref[rows, :], lb, g_scr, tril_f)
            qt = (z["q"] * z["e_q"]).astype(BF16)
            kt = (z["k"] * z["e_k"]).astype(BF16)
            qg = (z["q"] * z["e_in"]).astype(BF16)
            kg = (z["k"] * z["e_out"]).astype(BF16)
            vb = i_ref[rows, :].astype(BF16)
            outs = []
            for h in range(HEADS):
                sl = slice(h * HEAD_DIM, (h + 1) * HEAD_DIM)
                st = s_scr[h]
                st_ref[c, h] = st
                scores = jnp.where(tril, _dot_nt(qt[:, sl], kt[:, sl]), 0.0)
                outs.append(_dot(scores.astype(BF16), vb[:, sl]) + _dot_nt(qg[:, sl], st.astype(BF16)))
                s_scr[h] = st * z["e_last"][:, sl] + _dot_tn(vb[:, sl], kg[:, sl])
            o_ref[rows, :] = jnp.concatenate(outs, axis=1)
            return carry

        lax.fori_loop(0, n_local, chunk, 0)

    return pl.pallas_call(
        body, name="hgrn_fwd", grid=(t // tt,),
        in_specs=[_rows(tt, D_MODEL, 0), _rows(tt, D_MODEL, 1), _rows(tt, D_MODEL, 2), _resident((2, D_MODEL))],
        out_specs=[_rows(tt, D_MODEL),
                   pl.BlockSpec((n_local, HEADS, HEAD_DIM, HEAD_DIM), lambda i: (i, 0, 0, 0))],
        out_shape=[jax.ShapeDtypeStruct((t, D_MODEL), F32),
                   jax.ShapeDtypeStruct((t // CHUNK, HEADS, HEAD_DIM, HEAD_DIM), F32)],
        scratch_shapes=[pltpu.VMEM((HEADS, HEAD_DIM, HEAD_DIM), F32), pltpu.VMEM((CHUNK, D_MODEL), F32)],
        compiler_params=_params(("arbitrary",)),
    )(main, main, main, lb_raw)


def _hgrn_bwd(main, lb_raw, states, do, tt):
    t = main.shape[0]
    n_tiles = t // tt
    n_local = tt // CHUNK

    def rev(col_block):
        return pl.BlockSpec((tt, D_MODEL), lambda i: (n_tiles - 1 - i, col_block))

    def body(q_ref, f_ref, i_ref, lb_ref, st_ref, do_ref, dqfi_ref, dlb_ref, ds_scr, g_scr, acc_scr):
        @pl.when(pl.program_id(0) == 0)
        def _():
            ds_scr[...] = jnp.zeros_like(ds_scr)
            acc_scr[...] = jnp.zeros_like(acc_scr)

        lb = _lower_bound(lb_ref[...])
        row = lax.broadcasted_iota(jnp.int32, (CHUNK, CHUNK), 0)
        col = lax.broadcasted_iota(jnp.int32, (CHUNK, CHUNK), 1)
        tril = row >= col
        tril_f = tril.astype(F32)
        triu_f = (row <= col).astype(F32)

        def chunk(cc, carry):
            c = n_local - 1 - cc
            rows = pl.ds(pl.multiple_of(c * CHUNK, CHUNK), CHUNK)
            qr = q_ref[rows, :]
            z = _chunk_gates(qr, f_ref[rows, :], lb, g_scr, tril_f)
            qt = (z["q"] * z["e_q"]).astype(BF16)
            kt = (z["k"] * z["e_k"]).astype(BF16)
            qg = (z["q"] * z["e_in"]).astype(BF16)
            kg_f = z["k"] * z["e_out"]
            kg = kg_f.astype(BF16)
            vb = i_ref[rows, :].astype(BF16)
            dob = do_ref[rows, :].astype(BF16)
            dqt, dkt, dqg, dkg, dv, carry_in = [], [], [], [], [], []
            for h in range(HEADS):
                sl = slice(h * HEAD_DIM, (h + 1) * HEAD_DIM)
                st = st_ref[c, h]
                dst = ds_scr[h]
                dst_b = dst.astype(BF16)
                scores = jnp.where(tril, _dot_nt(qt[:, sl], kt[:, sl]), 0.0).astype(BF16)
                dscores = jnp.where(tril, _dot_nt(dob[:, sl], vb[:, sl]), 0.0).astype(BF16)
                dqt.append(_dot(dscores, kt[:, sl]))
                dkt.append(_dot_tn(dscores, qt[:, sl]))
                dv.append(_dot_tn(scores, dob[:, sl]) + _dot_nt(kg[:, sl], dst_b))
                dqg.append(_dot(dob[:, sl], st.astype(BF16)))
                dkg_h = _dot(vb[:, sl], dst_b)
                dkg.append(dkg_h)
                carry_in.append(z["e_last"][:, sl] * _colsum(dst * st) + _colsum(dkg_h * kg_f[:, sl]))
                ds_scr[h] = dst * z["e_last"][:, sl] + _dot_tn(dob[:, sl], qg[:, sl])
            cat = functools.partial(jnp.concatenate, axis=1)
            dq = cat(dqt) * z["e_q"] + cat(dqg) * z["e_in"]
            dk = cat(dkt) * z["e_k"] + cat(dkg) * z["e_out"]
            dgate = (qt.astype(F32) * cat(dqt) - kt.astype(F32) * cat(dkt)) + (z["q"] * z["e_in"] * cat(dqg) - kg_f * cat(dkg))
            dlogf = _dot_exact(triu_f, dgate) + cat(carry_in)
            df = dlogf / z["f"] - dk
            sg = z["sg"]
            sq = z["sq"]
            acc_scr[...] += _colsum(df * (1.0 - sg))
            dqfi_ref[rows, 0:D_MODEL] = (dq * (sq * (1.0 + qr * (1.0 - sq)))).astype(BF16)
            dqfi_ref[rows, D_MODEL:2 * D_MODEL] = (df * (1.0 - lb) * sg * (1.0 - sg)).astype(BF16)
            dqfi_ref[rows, 2 * D_MODEL:3 * D_MODEL] = cat(dv).astype(BF16)
            return carry

        lax.fori_loop(0, n_local, chunk, 0)
        d0 = acc_scr[...] * lb * (1.0 - lb)
        dlb_ref[0:1, :] = d0
        dlb_ref[1:2, :] = -d0

    return pl.pallas_call(
        body, name="hgrn_bwd", grid=(n_tiles,),
        in_specs=[rev(0), rev(1), rev(2), _resident((2, D_MODEL)),
                  pl.BlockSpec((n_local, HEADS, HEAD_DIM, HEAD_DIM), lambda i: (n_tiles - 1 - i, 0, 0, 0)),
                  rev(0)],
        out_specs=[pl.BlockSpec((tt, 3 * D_MODEL), lambda i: (n_tiles - 1 - i, 0)), _acc((2, D_MODEL))],
        out_shape=[jax.ShapeDtypeStruct((t, 3 * D_MODEL), BF16), jax.ShapeDtypeStruct((2, D_MODEL), F32)],
        scratch_shapes=[pltpu.VMEM((HEADS, HEAD_DIM, HEAD_DIM), F32), pltpu.VMEM((CHUNK, D_MODEL), F32),
                        pltpu.VMEM((1, D_MODEL), F32)],
        compiler_params=_params(("arbitrary",)),
    )(main, main, main, lb_raw, states, do)


def _head_norm(o):
    rs, ns = [], []
    for h in range(HEADS):
        oh = o[:, h * HEAD_DIM:(h + 1) * HEAD_DIM]
        r, n = _rms(oh)
        rs.append(jnp.broadcast_to(r, oh.shape))
        ns.append(n)
    return jnp.concatenate(rs, axis=1), jnp.concatenate(ns, axis=1)


def _head_norm_bwd(dn, n, r):
    outs = []
    for h in range(HEADS):
        sl = slice(h * HEAD_DIM, (h + 1) * HEAD_DIM)
        outs.append(_rms_bwd(dn[:, sl], n[:, sl], r[:, sl]))
    return jnp.concatenate(outs, axis=1)


def _window_counts(first_row, tm):
    pos = (first_row + 1 + lax.broadcasted_iota(jnp.int32, (tm, 1), 0)).astype(F32)
    return [jnp.minimum(pos, float(w)) for w in POOL_WINDOWS]


def _post_fwd(o, main, pool_r, gates, x1, onorm, pool_w, pool_scale, sqw, wbw, tm):
    t = o.shape[0]
    ext_rows = tm + POOL_HALO

    def body(o_ref, og_ref, u_ref, gt_ref, x1_ref, on_ref, pw_ref, ps_ref, wa_ref, wout_ref, wb_ref,
             x2_ref, ya_ref, yb_ref, pooled_ref, ext):
        i = pl.program_id(0)

        @pl.when(i == 0)
        def _():
            ext[0:POOL_HALO, :] = jnp.zeros((POOL_HALO, POOL_WIDTH), F32)

        _, n = _head_norm(o_ref[...])
        og = og_ref[...]
        oa = (n * on_ref[...] * (og * _sigmoid(og))).astype(BF16)
        ya = _dot(oa, wa_ref[...])

        u = u_ref[...]
        ext[POOL_HALO:ext_rows, :] = u
        e = ext[...]
        counts = _window_counts(i * tm, tm)
        pooled = []
        for gidx, w in enumerate(POOL_WINDOWS):
            s = e[:, gidx * HEAD_DIM:(gidx + 1) * HEAD_DIM]
            shift = 1
            while shift < w:
                s = s + pltpu.roll(s, shift, axis=0)
                shift *= 2
            pooled.append(s[POOL_HALO:, :] / counts[gidx] - u[:, gidx * HEAD_DIM:(gidx + 1) * HEAD_DIM])
        ext[0:POOL_HALO, :] = ext[tm:ext_rows, :]
        pooled_b = [pg.astype(BF16) for pg in pooled]
        pooled_ref[...] = jnp.concatenate(pooled_b, axis=1)
        mixed = jnp.concatenate([_dot(pooled_b[gidx], pw_ref[gidx].astype(BF16)) for gidx in range(len(POOL_WINDOWS))],
                                axis=1) * ps_ref[...]
        mixed_b = mixed.astype(BF16)
        yb = jnp.concatenate([_dot(mixed_b, wb_ref[j]) for j in range(N_CHIPS)], axis=1)

        gt = gt_ref[...]
        y = _sigmoid(gt[:, :D_MODEL]) * ya + _sigmoid(gt[:, D_MODEL:]) * yb
        x2_ref[...] = x1_ref[...] + _dot(y.astype(BF16), wout_ref[...])
        ya_ref[...] = ya.astype(BF16)
        yb_ref[...] = yb.astype(BF16)

    return pl.pallas_call(
        body, name="post_fwd", grid=(t // tm,),
        in_specs=[_rows(tm, D_MODEL), _rows(tm, D_MODEL, 3), _rows(tm, POOL_WIDTH), _rows(tm, GATE_COLS), _rows(tm, D_MODEL),
                  _resident((1, D_MODEL)), _resident(pool_w.shape), _resident((1, POOL_WIDTH)),
                  _pick(sqw.shape, 0), _pick(sqw.shape, 1), _resident(wbw.shape)],
        out_specs=[_rows(tm, D_MODEL), _rows(tm, D_MODEL), _rows(tm, D_MODEL), _rows(tm, POOL_WIDTH)],
        out_shape=[jax.ShapeDtypeStruct((t, D_MODEL), F32), jax.ShapeDtypeStruct((t, D_MODEL), BF16),
                   jax.ShapeDtypeStruct((t, D_MODEL), BF16), jax.ShapeDtypeStruct((t, POOL_WIDTH), BF16)],
        scratch_shapes=[pltpu.VMEM((ext_rows, POOL_WIDTH), F32)],
        compiler_params=_params(("arbitrary",)),
    )(o, main, pool_r, gates, x1, onorm, pool_w, pool_scale, sqw, sqw, wbw)


def _post_bwd(dx2, o, main, gates, ya, yb, pooled, onorm, pool_w, pool_scale, sqw, wbw, tm):
    t = o.shape[0]
    n_tiles = t // tm
    ext_rows = tm + POOL_HALO
    n_groups = len(POOL_WINDOWS)

    def rev(cols, col_block=0):
        return pl.BlockSpec((tm, cols), lambda i: (n_tiles - 1 - i, col_block))

    def body(dx2_ref, o_ref, og_ref, gt_ref, ya_ref, yb_ref, pooled_ref, on_ref, pw_ref, ps_ref, wa_ref, wout_ref, wb_ref,
             do_ref, dog_ref, du_ref, dgt_ref, dwa_ref, dwout_ref, dwb_ref, dpw_ref, dps_ref, don_ref, ext):
        i = pl.program_id(0)

        @pl.when(i == 0)
        def _():
            ext[tm:ext_rows, :] = jnp.zeros((POOL_HALO, POOL_WIDTH), F32)
            for ref in (dwa_ref, dwout_ref, dwb_ref, dpw_ref, dps_ref, don_ref):
                ref[...] = jnp.zeros_like(ref)

        dx2b = dx2_ref[...].astype(BF16)
        dy = _dot_nt(dx2b, wout_ref[...])
        gt = gt_ref[...]
        sga = _sigmoid(gt[:, :D_MODEL])
        sgb = _sigmoid(gt[:, D_MODEL:])
        ya = ya_ref[...].astype(F32)
        yb = yb_ref[...].astype(F32)
        y = (sga * ya + sgb * yb).astype(BF16)
        dwout_ref[...] += _dot_tn(y, dx2b)
        dya = (dy * sga).astype(BF16)
        dyb = (dy * sgb).astype(BF16)
        dgt_ref[:, :D_MODEL] = (dy * ya * sga * (1.0 - sga)).astype(BF16)
        dgt_ref[:, D_MODEL:] = (dy * yb * sgb * (1.0 - sgb)).astype(BF16)

        r, n = _head_norm(o_ref[...])
        onv = on_ref[...]
        og = og_ref[...]
        sog = _sigmoid(og)
        silu_og = og * sog
        normed = n * onv
        dwa_ref[...] += _dot_tn((normed * silu_og).astype(BF16), dya)
        doa = _dot_nt(dya, wa_ref[...])
        dog_ref[...] = (doa * normed * (sog * (1.0 + og * (1.0 - sog)))).astype(BF16)
        dnormed = doa * silu_og
        don_ref[...] += _colsum(dnormed * n)
        do_ref[...] = _head_norm_bwd(dnormed * onv, n, r)

        psv = ps_ref[...]
        dmixed = _dot_nt(dyb[:, :256], wb_ref[0])
        for j in range(1, N_CHIPS):
            dmixed += _dot_nt(dyb[:, j * 256:(j + 1) * 256], wb_ref[j])
        pooled_b = pooled_ref[...]
        pm = jnp.concatenate(
            [_dot(pooled_b[:, gidx * HEAD_DIM:(gidx + 1) * HEAD_DIM], pw_ref[gidx].astype(BF16)) for gidx in range(n_groups)],
            axis=1)
        mixed_b = (pm * psv).astype(BF16)
        for j in range(N_CHIPS):
            dwb_ref[j] += _dot_tn(mixed_b, dyb[:, j * 256:(j + 1) * 256])
        dps_ref[...] += _colsum(dmixed * pm)
        dpm = (dmixed * psv).astype(BF16)
        counts = _window_counts((n_tiles - 1 - i) * tm, tm)
        dpooled = []
        for gidx in range(n_groups):
            sl = slice(gidx * HEAD_DIM, (gidx + 1) * HEAD_DIM)
            dpw_ref[gidx] += _dot_tn(pooled_b[:, sl], dpm[:, sl])
            dpooled.append(_dot_nt(dpm[:, sl], pw_ref[gidx].astype(BF16)))
        ext[0:tm, :] = jnp.concatenate([dpooled[gidx] / counts[gidx] for gidx in range(n_groups)], axis=1)
        e = ext[...]
        du = []
        for gidx, w in enumerate(POOL_WINDOWS):
            s = e[:, gidx * HEAD_DIM:(gidx + 1) * HEAD_DIM]
            shift = 1
            while shift < w:
                s = s + pltpu.roll(s, ext_rows - shift, axis=0)
                shift *= 2
            du.append(s[:tm, :] - dpooled[gidx])
        ext[tm:ext_rows, :] = ext[0:POOL_HALO, :]
        du_ref[...] = jnp.concatenate(du, axis=1).astype(BF16)

    wa_shape = (D_MODEL, D_MODEL)
    return pl.pallas_call(
        body, name="post_bwd", grid=(n_tiles,),
        in_specs=[rev(D_MODEL), rev(D_MODEL), rev(D_MODEL, 3), rev(GATE_COLS), rev(D_MODEL), rev(D_MODEL), rev(POOL_WIDTH),
                  _resident((1, D_MODEL)), _resident(pool_w.shape), _resident((1, POOL_WIDTH)),
                  _pick(sqw.shape, 0), _pick(sqw.shape, 1), _resident(wbw.shape)],
        out_specs=[rev(D_MODEL), rev(D_MODEL), rev(POOL_WIDTH), rev(GATE_COLS),
                   _acc(wa_shape), _acc(wa_shape), _acc(wbw.shape), _acc(pool_w.shape), _acc((1, POOL_WIDTH)),
                   _acc((1, D_MODEL))],
        out_shape=[jax.ShapeDtypeStruct((t, D_MODEL), F32), jax.ShapeDtypeStruct((t, D_MODEL), BF16),
                   jax.ShapeDtypeStruct((t, POOL_WIDTH), BF16), jax.ShapeDtypeStruct((t, GATE_COLS), BF16),
                   jax.ShapeDtypeStruct(wa_shape, F32), jax.ShapeDtypeStruct(wa_shape, F32),
                   jax.ShapeDtypeStruct(wbw.shape, F32), jax.ShapeDtypeStruct(pool_w.shape, F32),
                   jax.ShapeDtypeStruct((1, POOL_WIDTH), F32), jax.ShapeDtypeStruct((1, D_MODEL), F32)],
        scratch_shapes=[pltpu.VMEM((ext_rows, POOL_WIDTH), F32)],
        compiler_params=_params(("arbitrary",)),
    )(dx2, o, main, gates, ya, yb, pooled, onorm, pool_w, pool_scale, sqw, sqw, wbw)


def _tail(x3, p, target, g_ple, g_post, g_final, sqw, wpw, tm):
    t = x3.shape[0]
    pd = p.shape[1]

    def body(x_ref, p_ref, tg_ref, g4_ref, g5_ref, g6_ref, wg_ref, wp_ref,
             dx_ref, loss_ref, dwg_ref, dwp_ref, dg4_ref, dg5_ref, dg6_ref):
        @pl.when(pl.program_id(0) == 0)
        def _():
            for ref in (loss_ref, dwg_ref, dwp_ref, dg4_ref, dg5_ref, dg6_ref):
                ref[...] = jnp.zeros_like(ref)

        x3v = x_ref[...]
        g4, g5, g6 = g4_ref[...], g5_ref[...], g6_ref[...]
        r4, n4 = _rms(x3v)
        h4 = (n4 * g4).astype(BF16)
        gate = _sigmoid(_dot(h4, wg_ref[...]))
        pb = p_ref[...].astype(BF16)
        r5, n5 = _rms(jnp.concatenate([_dot(pb, wp_ref[j]) for j in range(N_CHIPS)], axis=1))
        emb = n5 * g5
        r6, n6 = _rms(x3v + gate * emb)
        diff = n6 * g6 - tg_ref[...]
        loss_ref[...] += 0.5 * jnp.sum(jnp.mean(diff * diff, axis=-1, keepdims=True), axis=0, keepdims=True)
        dout = diff * (1.0 / D_MODEL)
        dg6_ref[...] += _colsum(dout * n6)
        dx4 = _rms_bwd(dout * g6, n6, r6)
        demb = dx4 * gate
        dg5_ref[...] += _colsum(demb * n5)
        dpre = _rms_bwd(demb * g5, n5, r5).astype(BF16)
        for j in range(N_CHIPS):
            dwp_ref[j] += _dot_tn(pb, dpre[:, j * pd:(j + 1) * pd])
        dz = (dx4 * emb * gate * (1.0 - gate)).astype(BF16)
        dwg_ref[...] += _dot_tn(h4, dz)
        dh4 = _dot_nt(dz, wg_ref[...])
        dg4_ref[...] += _colsum(dh4 * n4)
        dx_ref[...] = dx4 + _rms_bwd(dh4 * g4, n4, r4)

    sq_shape = (D_MODEL, D_MODEL)
    vec = (1, D_MODEL)
    return pl.pallas_call(
        body, name="tail", grid=(t // tm,),
        in_specs=[_rows(tm, D_MODEL), _rows(tm, pd), _rows(tm, D_MODEL), _resident(vec), _resident(vec), _resident(vec),
                  _pick(sqw.shape, 2), _resident(wpw.shape)],
        out_specs=[_rows(tm, D_MODEL), _acc((1, 1)), _acc(sq_shape), _acc(wpw.shape), _acc(vec), _acc(vec), _acc(vec)],
        out_shape=[jax.ShapeDtypeStruct((t, D_MODEL), F32), jax.ShapeDtypeStruct((1, 1), F32),
                   jax.ShapeDtypeStruct(sq_shape, F32), jax.ShapeDtypeStruct(wpw.shape, F32),
                   jax.ShapeDtypeStruct(vec, F32), jax.ShapeDtypeStruct(vec, F32), jax.ShapeDtypeStruct(vec, F32)],
        compiler_params=_params(("arbitrary",)),
    )(x3, p, target, g_ple, g_post, g_final, sqw, wpw)


def _local_step(x, p, target, small, pool_w, ffnw, sqw, winw, wbw, wpw):
    t = x.shape[0]
    tm = min(256, t)
    tt = min(256, t)
    tk = min(512, t)

    x1, a1, b1 = _ffn_fwd(x, small["ffn1_norm"], ffnw, 0, tm)
    main, pool_r, gates = _mix_fwd(x1, small["mix_norm"], winw, tm)
    o, states = _hgrn_fwd(main, small["hgrn_lb"], tt)
    x2, ya, yb, pooled = _post_fwd(o, main, pool_r, gates, x1, small["hgrn_onorm"], pool_w, small["pool_scale"], sqw, wbw, tm)
    x3, a2, b2 = _ffn_fwd(x2, small["ffn2_norm"], ffnw, 3, tm)

    dx3, loss, d_wg, d_wp, d_ple, d_post, d_final = _tail(
        x3, p, target, small["ple_norm"], small["ple_post_norm"], small["final_norm"], sqw, wpw, tm)

    dx2, dab2, s2, h3, dxh2, d_ffn2_norm = _ffn_bwd(dx3, x2, small["ffn2_norm"], a2, b2, ffnw, 3, tm)
    d_w13_2 = _wgrad(dab2, h3, WGRAD_IN_BLOCKS, "wgrad_ffn2_in", tk)
    d_w2_2 = _wgrad(s2, dxh2, WGRAD_OUT_BLOCKS, "wgrad_ffn2_out", tk)

    (do, dog, du, dgates, d_wa, d_wout, d_wb, d_pool_w, d_pool_scale, d_onorm) = _post_bwd(
        dx2, o, main, gates, ya, yb, pooled, small["hgrn_onorm"], pool_w, small["pool_scale"], sqw, wbw, tm)
    dqfi, d_lb = _hgrn_bwd(main, small["hgrn_lb"], states, do, tt)
    dx1, dproj, h2, d_mix_norm = _mix_bwd(dqfi, dog, du, dgates, dx2, x1, small["mix_norm"], winw, tm)
    d_win = _wgrad_cols(h2, dproj, N_CHIPS, "wgrad_in", tk)

    dx, dab1, s1, h1, dxh1, d_ffn1_norm = _ffn_bwd(dx1, x, small["ffn1_norm"], a1, b1, ffnw, 0, tm)
    d_w13_1 = _wgrad(dab1, h1, WGRAD_IN_BLOCKS, "wgrad_ffn1_in", tk)
    d_w2_1 = _wgrad(s1, dxh1, WGRAD_OUT_BLOCKS, "wgrad_ffn1_out", tk)

    big = dict(w13_1=d_w13_1, w2_1=d_w2_1, w13_2=d_w13_2, w2_2=d_w2_2, wa=d_wa, wout=d_wout, wg=d_wg, win=d_win, wb=d_wb,
               wp=d_wp)
    vecs = dict(ffn1_norm=d_ffn1_norm, mix_norm=d_mix_norm, hgrn_lb=d_lb, hgrn_onorm=d_onorm, ffn2_norm=d_ffn2_norm,
                ple_norm=d_ple, ple_post_norm=d_post, final_norm=d_final, pool_scale=d_pool_scale, pool_w=d_pool_w)
    return loss, dx, big, vecs


ANY = pl.BlockSpec(memory_space=pl.ANY)


def _position():
    return lax.axis_index("x"), lax.axis_index("y"), lax.axis_index("c")


def _other_chips(x, y):
    return [(1 - x, y), (x, 1 - y), (1 - x, 1 - y)]


def _remote(src, dst, send_sems, recv_sems, k, device):
    return pltpu.make_async_remote_copy(src_ref=src, dst_ref=dst, send_sem=send_sems.at[k], recv_sem=recv_sems.at[k],
                                        device_id=device, device_id_type=MESH)


def _gather_weights(shards):
    n = len(shards)

    def body(*refs):
        ins, outs = refs[:n], refs[n:2 * n]
        send_sems, recv_sems, local_sems = refs[2 * n:]
        x, y, c = _position()
        mine = 2 * x + y
        sibling = (x, y, 1 - c)
        chips = _other_chips(x, y)

        def half(ref, which):
            hr = ref.shape[-2] // 2
            return (slice(None), pl.ds(which * hr, hr), slice(None))

        local = [pltpu.make_async_copy(ins[a], outs[a].at[:, mine], local_sems.at[a]) for a in range(n)]
        for cp in local:
            cp.start()
        sent = []
        for k, (cx, cy) in enumerate(chips):
            for a in range(n):
                rows = half(ins[a], c)
                cp = _remote(ins[a].at[rows], outs[a].at[:, mine].at[rows], send_sems, recv_sems, k * n + a, (cx, cy, c))
                cp.start()
                sent.append(cp)
        for k, (cx, cy) in enumerate(chips):
            theirs = 2 * cx + cy
            for a in range(n):
                block = outs[a].at[:, theirs].at[half(ins[a], c)]
                _remote(block, block, send_sems, recv_sems, k * n + a, (cx, cy, c)).wait_recv()
                cp = _remote(block, block, send_sems, recv_sems, (3 + k) * n + a, sibling)
                cp.start()
                sent.append(cp)
        for k, (cx, cy) in enumerate(chips):
            theirs = 2 * cx + cy
            for a in range(n):
                block = outs[a].at[:, theirs].at[half(ins[a], 1 - c)]
                _remote(block, block, send_sems, recv_sems, (3 + k) * n + a, sibling).wait_recv()
        for cp in sent:
            cp.wait_send()
        for cp in local:
            cp.wait()

    return pl.pallas_call(
        body, name="gather_weights", in_specs=[ANY] * n, out_specs=[ANY] * n,
        out_shape=[jax.ShapeDtypeStruct((s.shape[0], N_CHIPS) + s.shape[1:], s.dtype) for s in shards],
        scratch_shapes=[pltpu.SemaphoreType.DMA((6 * n,)), pltpu.SemaphoreType.DMA((6 * n,)), pltpu.SemaphoreType.DMA((n,))],
    )(*shards)


def _pair_exchange(grads):
    n = len(grads)

    def body(*refs):
        ins, outs = refs[:n], refs[n:2 * n]
        send_sems, recv_sems = refs[2 * n:]
        x, y, c = _position()
        sent = []
        for a in range(n):
            hr = ins[a].shape[2] // 2
            cp = _remote(ins[a].at[:, :, pl.ds((1 - c) * hr, hr), :], outs[a], send_sems, recv_sems, a, (x, y, 1 - c))
            cp.start()
            sent.append(cp)
        for cp in sent:
            cp.wait_recv()
        for cp in sent:
            cp.wait_send()

    return pl.pallas_call(
        body, name="pair_exchange", in_specs=[ANY] * n, out_specs=[ANY] * n,
        out_shape=[jax.ShapeDtypeStruct(g.shape[:2] + (g.shape[2] // 2, g.shape[3]), g.dtype) for g in grads],
        scratch_shapes=[pltpu.SemaphoreType.DMA((n,)), pltpu.SemaphoreType.DMA((n,))],
    )(*grads)


def _add_pair(mine, theirs, c, tag):
    l, _, hr, cols = theirs.shape

    def body(c_ref, mine_ref, theirs_ref, out_ref):
        out_ref[...] = (mine_ref[...] + theirs_ref[...]).astype(BF16)

    block = (None, None, hr, cols)
    return pl.pallas_call(
        body, name=f"add_pair_{tag}",
        grid_spec=pltpu.PrefetchScalarGridSpec(
            num_scalar_prefetch=1, grid=(l, N_CHIPS),
            in_specs=[pl.BlockSpec(block, lambda i, j, s: (i, j, s[0], 0)), pl.BlockSpec(block, lambda i, j, s: (i, j, 0, 0))],
            out_specs=pl.BlockSpec(block, lambda i, j, s: (i, j, 0, 0))),
        out_shape=jax.ShapeDtypeStruct(theirs.shape, BF16),
        compiler_params=_params(("parallel", "parallel")),
    )(c.reshape(1), mine, theirs)


def _chip_exchange(parts):
    n = len(parts)

    def body(*refs):
        ins, outs = refs[:n], refs[n:2 * n]
        send_sems, recv_sems = refs[2 * n:]
        x, y, c = _position()
        sent = []
        for k, (cx, cy) in enumerate(_other_chips(x, y)):
            for a in range(n):
                cp = _remote(ins[a].at[:, 2 * cx + cy], outs[a].at[:, k], send_sems, recv_sems, k * n + a, (cx, cy, c))
                cp.start()
                sent.append(cp)
        for cp in sent:
            cp.wait_recv()
        for cp in sent:
            cp.wait_send()

    return pl.pallas_call(
        body, name="chip_exchange", in_specs=[ANY] * n, out_specs=[ANY] * n,
        out_shape=[jax.ShapeDtypeStruct((q.shape[0], 3) + q.shape[2:], q.dtype) for q in parts],
        scratch_shapes=[pltpu.SemaphoreType.DMA((3 * n,)), pltpu.SemaphoreType.DMA((3 * n,))],
    )(*parts)


def _add_chips(part, received, mine, tag):
    l, _, hr, cols = received.shape

    def body(j_ref, part_ref, recv_ref, out_ref):
        acc = part_ref[...].astype(F32)
        for k in range(3):
            acc += recv_ref[k].astype(F32)
        out_ref[...] = acc

    return pl.pallas_call(
        body, name=f"add_chips_{tag}",
        grid_spec=pltpu.PrefetchScalarGridSpec(
            num_scalar_prefetch=1, grid=(l,),
            in_specs=[pl.BlockSpec((None, None, hr, cols), lambda i, s: (i, s[0], 0, 0)),
                      pl.BlockSpec((None, 3, hr, cols), lambda i, s: (i, 0, 0, 0))],
            out_specs=pl.BlockSpec((None, hr, cols), lambda i, s: (i, 0, 0))),
        out_shape=jax.ShapeDtypeStruct((l, hr, cols), F32),
        compiler_params=_params(("parallel",)),
    )(mine.reshape(1), part, received)


def _pair_share(halves):
    n = len(halves)

    def body(*refs):
        ins, outs = refs[:n], refs[n:2 * n]
        send_sems, recv_sems, local_sems = refs[2 * n:]
        x, y, c = _position()
        copies = []
        for a in range(n):
            hr = ins[a].shape[1]
            rows = (slice(None), pl.ds(c * hr, hr), slice(None))
            lc = pltpu.make_async_copy(ins[a], outs[a].at[rows], local_sems.at[a])
            lc.start()
            cp = _remote(ins[a], outs[a].at[rows], send_sems, recv_sems, a, (x, y, 1 - c))
            cp.start()
            copies.append((lc, cp))
        for lc, cp in copies:
            cp.wait_recv()
        for lc, cp in copies:
            cp.wait_send()
            lc.wait()

    return pl.pallas_call(
        body, name="pair_share", in_specs=[ANY] * n, out_specs=[ANY] * n,
        out_shape=[jax.ShapeDtypeStruct((h.shape[0], 2 * h.shape[1], h.shape[2]), h.dtype) for h in halves],
        scratch_shapes=[pltpu.SemaphoreType.DMA((n,)), pltpu.SemaphoreType.DMA((n,)), pltpu.SemaphoreType.DMA((n,))],
    )(*halves)


def _all_reduce_small(pack):
    n_dev = 8

    def body(in_ref, out_ref, buf, send_sems, recv_sems):
        x, y, c = _position()
        me = 4 * x + 2 * y + c
        flips = [(fx, fy, fc) for fx in (0, 1) for fy in (0, 1) for fc in (0, 1)][1:]
        sent = []
        for k, (fx, fy, fc) in enumerate(flips):
            peer = (x + fx - 2 * x * fx, y + fy - 2 * y * fy, c + fc - 2 * c * fc)
            cp = _remote(in_ref, buf.at[me], send_sems, recv_sems, k, peer)
            cp.start()
            sent.append((cp, 4 * peer[0] + 2 * peer[1] + peer[2]))
        buf[me] = in_ref[...]
        for k, (cp, peer_index) in enumerate(sent):
            _remote(in_ref, buf.at[peer_index], send_sems, recv_sems, k, (x, y, c)).wait_recv()
        for cp, _ in sent:
            cp.wait_send()
        acc = buf[0]
        for i in range(1, n_dev):
            acc += buf[i]
        out_ref[...] = acc

    vmem = pl.BlockSpec(memory_space=pltpu.VMEM)
    return pl.pallas_call(
        body, name="all_reduce_small", in_specs=[vmem], out_specs=vmem,
        out_shape=jax.ShapeDtypeStruct(pack.shape, F32),
        scratch_shapes=[pltpu.VMEM((n_dev,) + pack.shape, F32), pltpu.SemaphoreType.DMA((n_dev - 1,)),
                        pltpu.SemaphoreType.DMA((n_dev - 1,))],
    )(pack)


def _adamw(w, g, m, v, tag):
    rows, cols = w.shape
    tr = rows
    for cand in (rows, 512, 352, 256, 128, 64, 32, 16, 8):
        if rows % cand == 0 and cand * cols * 4 <= 2 * 1024 * 1024:
            tr = cand
            break

    def body(w_ref, g_ref, m_ref, v_ref, d_ref, nm_ref, nv_ref):
        gv = g_ref[...]
        m2 = ADAM_B1 * m_ref[...] + (1.0 - ADAM_B1) * gv
        v2 = ADAM_B2 * v_ref[...] + (1.0 - ADAM_B2) * jnp.square(gv)
        m_hat = m2 / (1.0 - ADAM_B1 ** ADAM_STEP)
        v_hat = v2 / (1.0 - ADAM_B2 ** ADAM_STEP)
        d_ref[...] = -ADAM_LR * (m_hat / (jnp.sqrt(v_hat) + ADAM_EPS) + ADAM_WD * w_ref[...])
        nm_ref[...] = m2
        nv_ref[...] = v2

    spec = pl.BlockSpec((tr, cols), lambda i: (i, 0))
    shape = jax.ShapeDtypeStruct((rows, cols), F32)
    return pl.pallas_call(
        body, name=f"adamw_{tag}", grid=(rows // tr,), in_specs=[spec] * 4, out_specs=[spec] * 3, out_shape=[shape] * 3,
        compiler_params=_params(("parallel",)),
    )(w, g, m, v)


VECTOR_PARAMS = ("ffn1_norm", "mix_norm", "hgrn_lb", "hgrn_onorm", "ffn2_norm", "ple_norm", "ple_post_norm", "final_norm",
                 "pool_scale")
BIG_PARAMS = ("ffn1_w1", "ffn1_w3", "ffn1_w2", "w_in", "w_branch_a", "w_branch_b", "w_out", "ffn2_w1", "ffn2_w3", "ffn2_w2",
              "ple_w_gate", "ple_w_proj")
ALL_PARAMS = ("ffn1_norm", "ffn1_w1", "ffn1_w3", "ffn1_w2", "mix_norm", "w_in", "hgrn_lb", "hgrn_onorm", "w_branch_a",
              "pool_w", "pool_scale", "w_branch_b", "w_out", "ffn2_norm", "ffn2_w1", "ffn2_w3", "ffn2_w2", "ple_norm",
              "ple_w_gate", "ple_w_proj", "ple_post_norm", "final_norm")
TILE_ROWS = 8


def _pack_small(values):
    parts = []
    for name in VECTOR_PARAMS:
        a = values[name].reshape(-1, values[name].shape[-1])
        parts.append(jnp.pad(a, ((0, TILE_ROWS - a.shape[0]), (0, D_MODEL - a.shape[1]))))
    parts.append(values["pool_w"].reshape(-1, D_MODEL))
    return jnp.concatenate(parts, axis=0)


def _unpack_small(pack, shapes):
    out = {}
    for i, name in enumerate(VECTOR_PARAMS):
        shape = shapes[name]
        rows = 1 if len(shape) == 1 else shape[0]
        out[name] = pack[i * TILE_ROWS:i * TILE_ROWS + rows, :shape[-1]].reshape(shape)
    out["pool_w"] = pack[len(VECTOR_PARAMS) * TILE_ROWS:].reshape(shapes["pool_w"])
    return out


def kernel(x, p, ffn1_norm, ffn1_w1, ffn1_w3, ffn1_w2, mix_norm, w_in, hgrn_lb, hgrn_onorm, w_branch_a, pool_w, pool_scale, w_branch_b, w_out, ffn2_norm, ffn2_w1, ffn2_w3, ffn2_w2, ple_norm, ple_w_gate, ple_w_proj, ple_post_norm, final_norm, loss_target, m_ffn1_norm, m_ffn1_w1, m_ffn1_w3, m_ffn1_w2, m_mix_norm, m_w_in, m_hgrn_lb, m_hgrn_onorm, m_w_branch_a, m_pool_w, m_pool_scale, m_w_branch_b, m_w_out, m_ffn2_norm, m_ffn2_w1, m_ffn2_w3, m_ffn2_w2, m_ple_norm, m_ple_w_gate, m_ple_w_proj, m_ple_post_norm, m_final_norm, v_ffn1_norm, v_ffn1_w1, v_ffn1_w3, v_ffn1_w2, v_mix_norm, v_w_in, v_hgrn_lb, v_hgrn_onorm, v_w_branch_a, v_pool_w, v_pool_scale, v_w_branch_b, v_w_out, v_ffn2_norm, v_ffn2_w1, v_ffn2_w3, v_ffn2_w2, v_ple_norm, v_ple_w_gate, v_ple_w_proj, v_ple_post_norm, v_final_norm):
    args = dict(locals())
    w = {name: args[name] for name in ALL_PARAMS}
    m = {name: args["m_" + name] for name in ALL_PARAMS}
    v = {name: args["v_" + name] for name in ALL_PARAMS}
    cx, cy, cc = _position()
    chip = (2 * cx + cy).astype(jnp.int32)
    core = cc.astype(jnp.int32)

    ffn_shard = jnp.stack([w["ffn1_w1"][0].T, w["ffn1_w3"][0].T, w["ffn1_w2"][0],
                           w["ffn2_w1"][0].T, w["ffn2_w3"][0].T, w["ffn2_w2"][0]]).astype(BF16)
    sq_shard = jnp.stack([w["w_branch_a"][0], w["w_out"][0], w["ple_w_gate"][0]]).astype(BF16)
    ffnw, sqw, winw, wbw, wpw = _gather_weights(
        [ffn_shard, sq_shard, w["w_in"].astype(BF16), w["w_branch_b"].astype(BF16), w["ple_w_proj"].astype(BF16)])
    ffnw = ffnw.reshape(6, D_FF, D_MODEL)
    sqw = sqw.reshape(3, D_MODEL, D_MODEL)

    small = {name: w[name] for name in VECTOR_PARAMS}
    small["final_norm"] = w["final_norm"].reshape(1, D_MODEL)
    loss, dx, big, vecs = _local_step(x[0], p[0, 0], loss_target[0], small, w["pool_w"][0], ffnw, sqw, winw[0], wbw[0], wpw[0])
    loss = lax.psum(loss[0, 0], ("x", "y", "c"))

    as_shards = lambda a, l: a.reshape((l, N_CHIPS, -1, a.shape[-1]))
    partial = [as_shards(big["w13_1"], 2), as_shards(big["w2_1"], 1), as_shards(big["w13_2"], 2), as_shards(big["w2_2"], 1),
               as_shards(big["wa"], 1), as_shards(big["wout"], 1), as_shards(big["wg"], 1),
               big["win"][None], big["wb"][None], big["wp"][None]]
    tags = ("w13_1", "w2_1", "w13_2", "w2_2", "wa", "wout", "wg", "win", "wb", "wp")
    from_sibling = _pair_exchange(partial)
    chip_sums = [_add_pair(a, b, core, t) for a, b, t in zip(partial, from_sibling, tags)]
    from_chips = _chip_exchange(chip_sums)
    halves = [_add_chips(a, b, chip, t) for a, b, t in zip(chip_sums, from_chips, tags)]
    g13_1, g2_1, g13_2, g2_2, g_wa, g_wout, g_wg, g_win, g_wb, g_wp = _pair_share(halves)
    grads = {
        "ffn1_w1": g13_1[0].T[None], "ffn1_w3": g13_1[1].T[None], "ffn1_w2": g2_1,
        "ffn2_w1": g13_2[0].T[None], "ffn2_w3": g13_2[1].T[None], "ffn2_w2": g2_2,
        "w_branch_a": g_wa, "w_out": g_wout, "ple_w_gate": g_wg, "w_in": g_win, "w_branch_b": g_wb, "ple_w_proj": g_wp,
    }

    shapes = {name: w[name].shape for name in VECTOR_PARAMS + ("pool_w",)}
    small_sum = _all_reduce_small(_pack_small(vecs))
    grads.update(_unpack_small(small_sum, shapes))

    delta, new_m, new_v = {}, {}, {}
    for name in BIG_PARAMS:
        flat = lambda a: a.reshape(-1, a.shape[-1])
        d, nm, nv = _adamw(flat(w[name]), flat(grads[name]), flat(m[name]), flat(v[name]), name)
        delta[name], new_m[name], new_v[name] = (a.reshape(w[name].shape) for a in (d, nm, nv))
    d, nm, nv = _adamw(_pack_small(w), small_sum, _pack_small(m), _pack_small(v), "small")
    delta.update(_unpack_small(d, shapes))
    new_m.update(_unpack_small(nm, shapes))
    new_v.update(_unpack_small(nv, shapes))

    return (loss, dx[None], *[grads[n] for n in ALL_PARAMS], *[delta[n] for n in ALL_PARAMS],
            *[new_m[n] for n in ALL_PARAMS], *[new_v[n] for n in ALL_PARAMS])
```

```python
import functools

import jax
import jax.numpy as jnp
from jax import lax
from jax.experimental import pallas as pl
from jax.experimental.pallas import tpu as pltpu

F32 = jnp.float32
BF16 = jnp.bfloat16
MESH = pl.DeviceIdType.MESH

D_MODEL = 1024
D_FF = 2816
HEADS = 8
HEAD_DIM = 128
POOL_WIDTH = 512
POOL_WINDOWS = (2, 4, 8, 16)
POOL_HALO = 16
N_CHIPS = 4
EPS = 1e-6
CHUNK = 64
MAIN_COLS = 4096
GATE_COLS = 2048
SHARD_IN_COLS = 1664

ADAM_LR = 0.001
ADAM_B1 = 0.9
ADAM_B2 = 0.999
ADAM_EPS = 1e-08
ADAM_WD = 0.01
ADAM_STEP = 10

VMEM_LIMIT = 56 * 1024 * 1024
WGRAD_IN_BLOCKS = 4
WGRAD_OUT_BLOCKS = 2


def _params(semantics=None, vmem=VMEM_LIMIT):
    return pltpu.CompilerParams(dimension_semantics=semantics, vmem_limit_bytes=vmem)


def _dot(a, b):
    return jnp.dot(a, b, preferred_element_type=F32)


def _dot_nt(a, b):
    return lax.dot_general(a, b, (((1,), (1,)), ((), ())), preferred_element_type=F32)


def _dot_tn(a, b):
    return lax.dot_general(a, b, (((0,), (0,)), ((), ())), preferred_element_type=F32)


def _dot_exact(a, b):
    return jnp.dot(a, b, preferred_element_type=F32, precision=lax.Precision.HIGHEST)


def _sigmoid(x):
    return jax.nn.sigmoid(x)


def _resident(shape):
    zeros = (0,) * len(shape)
    return pl.BlockSpec(shape, lambda *_: zeros, pipeline_mode=pl.Buffered(1))


def _pick(shape, k):
    zeros = (0,) * (len(shape) - 1)
    return pl.BlockSpec((None,) + tuple(shape[1:]), lambda *_: (k,) + zeros, pipeline_mode=pl.Buffered(1))


def _rows(tm, cols, col_block=0):
    return pl.BlockSpec((tm, cols), lambda i: (i, col_block))


def _acc(shape):
    zeros = (0,) * len(shape)
    return pl.BlockSpec(shape, lambda *_: zeros)


def _rms(x):
    r = lax.rsqrt(jnp.mean(x * x, axis=-1, keepdims=True) + EPS)
    return r, x * r


def _rms_bwd(dn, n, r):
    return r * (dn - n * jnp.mean(dn * n, axis=-1, keepdims=True))


def _colsum(a):
    return jnp.sum(a, axis=0, keepdims=True)


ANY = pl.BlockSpec(memory_space=pl.ANY)


class _Rider:
    def __init__(self, inputs, out_shape, sems, phases):
        self.inputs, self.out_shape, self.sems, self.phases = list(inputs), list(out_shape), list(sems), list(phases)


def _hosted(riders, body, *, name, grid=(), in_specs, out_specs, out_shape, scratch_shapes=(), compiler_params=None):
    riders = [r for r in riders if r is not None]
    n_in, n_out, n_scr = len(in_specs), len(out_shape), len(scratch_shapes)
    n_steps = 1
    for g in grid:
        n_steps *= g

    def wrapped(*refs):
        pos = n_in
        ins = refs[:n_in]
        r_ins = []
        for r in riders:
            r_ins.append(refs[pos:pos + len(r.inputs)])
            pos += len(r.inputs)
        outs = refs[pos:pos + n_out]
        pos += n_out
        r_outs = []
        for r in riders:
            r_outs.append(refs[pos:pos + len(r.out_shape)])
            pos += len(r.out_shape)
        scr = refs[pos:pos + n_scr]
        pos += n_scr
        r_sems = []
        for r in riders:
            r_sems.append(refs[pos:pos + len(r.sems)])
            pos += len(r.sems)
        step = 0
        for axis in range(len(grid)):
            step = step * grid[axis] + pl.program_id(axis)

        def at_step(which, fn):
            if n_steps == 1:
                fn()
            else:
                pl.when(step == which)(fn)

        for r, ri, ro, rs in zip(riders, r_ins, r_outs, r_sems):
            for fraction, fn in r.phases:
                if fraction == 0:
                    at_step(0, functools.partial(fn, ri, ro, rs))
        body(*ins, *outs, *scr)
        for r, ri, ro, rs in zip(riders, r_ins, r_outs, r_sems):
            for fraction, fn in r.phases:
                if fraction > 0:
                    at_step(min(int(fraction * n_steps), n_steps - 1), functools.partial(fn, ri, ro, rs))

    call = pl.pallas_call(
        wrapped, name=name, grid=grid,
        in_specs=list(in_specs) + [ANY for r in riders for _ in r.inputs],
        out_specs=list(out_specs) + [ANY for r in riders for _ in r.out_shape],
        out_shape=list(out_shape) + [s for r in riders for s in r.out_shape],
        scratch_shapes=list(scratch_shapes) + [s for r in riders for s in r.sems],
        compiler_params=compiler_params)

    def run(*args):
        res = call(*args, *[a for r in riders for a in r.inputs])
        extras, pos = [], n_out
        for r in riders:
            extras.append(list(res[pos:pos + len(r.out_shape)]))
            pos += len(r.out_shape)
        return list(res[:n_out]), extras

    return run


def _ffn_fwd(x, g, ffnw, tag, tm, riders=()):
    t = x.shape[0]

    def body(x_ref, g_ref, w1_ref, w3_ref, w2_ref, xo_ref, a_ref, b_ref):
        xv = x_ref[...]
        _, n = _rms(xv)
        h = (n * g_ref[...]).astype(BF16)
        a = _dot_nt(h, w1_ref[...])
        b = _dot_nt(h, w3_ref[...])
        s = (a * _sigmoid(a) * b).astype(BF16)
        xo_ref[...] = xv + 0.5 * _dot(s, w2_ref[...])
        a_ref[...] = a.astype(BF16)
        b_ref[...] = b.astype(BF16)

    return _hosted(
        riders, body, name=f"ffn_fwd_{tag}", grid=(t // tm,),
        in_specs=[_rows(tm, D_MODEL), _resident((1, D_MODEL)), _pick(ffnw.shape, 0), _pick(ffnw.shape, 1),
                  _pick(ffnw.shape, 2)],
        out_specs=[_rows(tm, D_MODEL), _rows(tm, D_FF), _rows(tm, D_FF)],
        out_shape=[jax.ShapeDtypeStruct((t, D_MODEL), F32), jax.ShapeDtypeStruct((t, D_FF), BF16),
                   jax.ShapeDtypeStruct((t, D_FF), BF16)],
        compiler_params=_params(("arbitrary",)),
    )(x, g, ffnw, ffnw, ffnw)


def _ffn_bwd(dxo, x, g, a, b, ffnw, tag, tm, riders=()):
    t = x.shape[0]

    def body(dxo_ref, x_ref, g_ref, a_ref, b_ref, w1_ref, w3_ref, w2_ref, dx_ref, dab_ref, s_ref, h_ref, dxh_ref, dg_ref):
        @pl.when(pl.program_id(0) == 0)
        def _():
            dg_ref[...] = jnp.zeros_like(dg_ref)

        xv = x_ref[...]
        gv = g_ref[...]
        r, n = _rms(xv)
        h_ref[...] = (n * gv).astype(BF16)
        dxo_v = dxo_ref[...]
        dxh = (0.5 * dxo_v).astype(BF16)
        dxh_ref[...] = dxh
        ds = _dot_nt(dxh, w2_ref[...])
        av = a_ref[...].astype(F32)
        bv = b_ref[...].astype(F32)
        sg = _sigmoid(av)
        silu = av * sg
        s_ref[...] = (silu * bv).astype(BF16)
        da = (ds * bv * (sg * (1.0 + av * (1.0 - sg)))).astype(BF16)
        db = (ds * silu).astype(BF16)
        dab_ref[:, :D_FF] = da
        dab_ref[:, D_FF:] = db
        dh = _dot(da, w1_ref[...]) + _dot(db, w3_ref[...])
        dg_ref[...] += _colsum(dh * n)
        dx_ref[...] = dxo_v + _rms_bwd(dh * gv, n, r)

    return _hosted(
        riders, body, name=f"ffn_bwd_{tag}", grid=(t // tm,),
        in_specs=[_rows(tm, D_MODEL), _rows(tm, D_MODEL), _resident((1, D_MODEL)), _rows(tm, D_FF), _rows(tm, D_FF),
                  _pick(ffnw.shape, 0), _pick(ffnw.shape, 1), _pick(ffnw.shape, 2)],
        out_specs=[_rows(tm, D_MODEL), _rows(tm, 2 * D_FF), _rows(tm, D_FF), _rows(tm, D_MODEL), _rows(tm, D_MODEL),
                   _acc((1, D_MODEL))],
        out_shape=[jax.ShapeDtypeStruct((t, D_MODEL), F32), jax.ShapeDtypeStruct((t, 2 * D_FF), BF16),
                   jax.ShapeDtypeStruct((t, D_FF), BF16), jax.ShapeDtypeStruct((t, D_MODEL), BF16),
                   jax.ShapeDtypeStruct((t, D_MODEL), BF16), jax.ShapeDtypeStruct((1, D_MODEL), F32)],
        compiler_params=_params(("arbitrary",)),
    )(dxo, x, g, a, b, ffnw, ffnw, ffnw)


def _wgrad(xm, dy, out_blocks, name, tk, riders=()):
    t, m = xm.shape
    n = dy.shape[1]
    mb = m // out_blocks

    def body(x_ref, dy_ref, o_ref):
        @pl.when(pl.program_id(1) == 0)
        def _():
            o_ref[...] = jnp.zeros_like(o_ref)

        o_ref[...] += _dot_tn(x_ref[...], dy_ref[...])

    return _hosted(
        riders, body, name=name, grid=(out_blocks, t // tk),
        in_specs=[pl.BlockSpec((tk, mb), lambda j, k: (k, j)), pl.BlockSpec((tk, n), lambda j, k: (k, 0))],
        out_specs=[pl.BlockSpec((None, mb, n), lambda j, k: (j, 0, 0))],
        out_shape=[jax.ShapeDtypeStruct((out_blocks, mb, n), F32)],
        compiler_params=_params(("arbitrary", "arbitrary")),
    )(xm, dy)


def _wgrad_cols(xm, dy, out_blocks, name, tk, riders=()):
    t, m = xm.shape
    n = dy.shape[1]
    nb = n // out_blocks

    def body(x_ref, dy_ref, o_ref):
        @pl.when(pl.program_id(1) == 0)
        def _():
            o_ref[...] = jnp.zeros_like(o_ref)

        o_ref[...] += _dot_tn(x_ref[...], dy_ref[...])

    return _hosted(
        riders, body, name=name, grid=(out_blocks, t // tk),
        in_specs=[pl.BlockSpec((tk, m), lambda j, k: (k, 0)), pl.BlockSpec((tk, nb), lambda j, k: (k, j))],
        out_specs=[pl.BlockSpec((None, m, nb), lambda j, k: (j, 0, 0))],
        out_shape=[jax.ShapeDtypeStruct((out_blocks, m, nb), F32)],
        compiler_params=_params(("arbitrary", "arbitrary")),
    )(xm, dy)


def _mix_fwd(x1, g, winw, tm, riders=()):
    t = x1.shape[0]

    def body(x_ref, g_ref, w_ref, main_ref, pool_ref, gate_ref):
        _, n = _rms(x_ref[...])
        h = (n * g_ref[...]).astype(BF16)
        proj = jnp.concatenate([_dot(h, w_ref[j]) for j in range(N_CHIPS)], axis=1)
        main_ref[...] = proj[:, :MAIN_COLS]
        pool_ref[...] = proj[:, MAIN_COLS:MAIN_COLS + POOL_WIDTH]
        gate_ref[...] = proj[:, MAIN_COLS + POOL_WIDTH:]

    return _hosted(
        riders, body, name="mix_fwd", grid=(t // tm,),
        in_specs=[_rows(tm, D_MODEL), _resident((1, D_MODEL)), _resident(winw.shape)],
        out_specs=[_rows(tm, MAIN_COLS), _rows(tm, POOL_WIDTH), _rows(tm, GATE_COLS)],
        out_shape=[jax.ShapeDtypeStruct((t, MAIN_COLS), F32), jax.ShapeDtypeStruct((t, POOL_WIDTH), F32),
                   jax.ShapeDtypeStruct((t, GATE_COLS), F32)],
        compiler_params=_params(("arbitrary",)),
    )(x1, g, winw)


def _mix_bwd(dqfi, dog, du, dgates, dx2, x1, g, winw, tm, riders=()):
    t = x1.shape[0]
    cols = N_CHIPS * SHARD_IN_COLS

    def body(dqfi_ref, dog_ref, du_ref, dgt_ref, dx2_ref, x_ref, g_ref, w_ref, dx_ref, dproj_ref, h_ref, dg_ref):
        @pl.when(pl.program_id(0) == 0)
        def _():
            dg_ref[...] = jnp.zeros_like(dg_ref)

        dproj = jnp.concatenate([dqfi_ref[...], dog_ref[...], du_ref[...], dgt_ref[...]], axis=1)
        dproj_ref[...] = dproj
        dh = _dot_nt(dproj[:, :SHARD_IN_COLS], w_ref[0])
        for j in range(1, N_CHIPS):
            dh += _dot_nt(dproj[:, j * SHARD_IN_COLS:(j + 1) * SHARD_IN_COLS], w_ref[j])
        gv = g_ref[...]
        r, n = _rms(x_ref[...])
        h_ref[...] = (n * gv).astype(BF16)
        dg_ref[...] += _colsum(dh * n)
        dx_ref[...] = dx2_ref[...] + _rms_bwd(dh * gv, n, r)

    return _hosted(
        riders, body, name="mix_bwd", grid=(t // tm,),
        in_specs=[_rows(tm, 3 * D_MODEL), _rows(tm, D_MODEL), _rows(tm, POOL_WIDTH), _rows(tm, GATE_COLS),
                  _rows(tm, D_MODEL), _rows(tm, D_MODEL), _resident((1, D_MODEL)), _resident(winw.shape)],
        out_specs=[_rows(tm, D_MODEL), _rows(tm, cols), _rows(tm, D_MODEL), _acc((1, D_MODEL))],
        out_shape=[jax.ShapeDtypeStruct((t, D_MODEL), F32), jax.ShapeDtypeStruct((t, cols), BF16),
                   jax.ShapeDtypeStruct((t, D_MODEL), BF16), jax.ShapeDtypeStruct((1, D_MODEL), F32)],
        compiler_params=_params(("arbitrary",)),
    )(dqfi, dog, du, dgates, dx2, x1, g, winw)


def _lower_bound(lb_raw):
    l0 = lb_raw[0:1, :]
    l1 = lb_raw[1:2, :]
    m = jnp.maximum(l0, l1)
    e0 = jnp.exp(l0 - m)
    e1 = jnp.exp(l1 - m)
    return e0 / (e0 + e1)


def _chunk_gates(qr, fr, lb, g_scr, tril_f):
    sg = _sigmoid(fr)
    f = lb + (1.0 - lb) * sg
    k = 1.0 - f
    sq = _sigmoid(qr)
    q = qr * sq
    g_scr[...] = _dot_exact(tril_f, jnp.log(f))
    gc = g_scr[...]
    gm = g_scr[CHUNK // 2 - 1:CHUNK // 2, :]
    gl = g_scr[CHUNK - 1:CHUNK, :]
    e_q = jnp.exp(gc - gm)
    e_k = jnp.exp(gm - gc)
    e_in = jnp.exp(gc)
    e_out = jnp.exp(gl - gc)
    return dict(sg=sg, f=f, k=k, sq=sq, q=q, e_q=e_q, e_k=e_k, e_in=e_in, e_out=e_out, e_last=jnp.exp(gl))


def _hgrn_fwd(main, lb_raw, tt, riders=()):
    t = main.shape[0]
    n_local = tt // CHUNK

    def body(q_ref, f_ref, i_ref, lb_ref, o_ref, st_ref, s_scr, g_scr):
        @pl.when(pl.program_id(0) == 0)
        def _():
            s_scr[...] = jnp.zeros_like(s_scr)

        lb = _lower_bound(lb_ref[...])
        row = lax.broadcasted_iota(jnp.int32, (CHUNK, CHUNK), 0)
        col = lax.broadcasted_iota(jnp.int32, (CHUNK, CHUNK), 1)
        tril = row >= col
        tril_f = tril.astype(F32)

        def chunk(c, carry):
            rows = pl.ds(pl.multiple_of(c * CHUNK, CHUNK), CHUNK)
            z = _chunk_gates(q_ref[rows, :], f_ref[rows, :], lb, g_scr, tril_f)
            qt = (z["q"] * z["e_q"]).astype(BF16)
            kt = (z["k"] * z["e_k"]).astype(BF16)
            qg = (z["q"] * z["e_in"]).astype(BF16)
            kg = (z["k"] * z["e_out"]).astype(BF16)
            vb = i_ref[rows, :].astype(BF16)
            outs = []
            for h in range(HEADS):
                sl = slice(h * HEAD_DIM, (h + 1) * HEAD_DIM)
                st = s_scr[h]
                st_ref[c, h] = st
                scores = jnp.where(tril, _dot_nt(qt[:, sl], kt[:, sl]), 0.0)
                outs.append(_dot(scores.astype(BF16), vb[:, sl]) + _dot_nt(qg[:, sl], st.astype(BF16)))
                s_scr[h] = st * z["e_last"][:, sl] + _dot_tn(vb[:, sl], kg[:, sl])
            o_ref[rows, :] = jnp.concatenate(outs, axis=1)
            return carry

        lax.fori_loop(0, n_local, chunk, 0)

    return _hosted(
        riders, body, name="hgrn_fwd", grid=(t // tt,),
        in_specs=[_rows(tt, D_MODEL, 0), _rows(tt, D_MODEL, 1), _rows(tt, D_MODEL, 2), _resident((2, D_MODEL))],
        out_specs=[_rows(tt, D_MODEL),
                   pl.BlockSpec((n_local, HEADS, HEAD_DIM, HEAD_DIM), lambda i: (i, 0, 0, 0))],
        out_shape=[jax.ShapeDtypeStruct((t, D_MODEL), F32),
                   jax.ShapeDtypeStruct((t // CHUNK, HEADS, HEAD_DIM, HEAD_DIM), F32)],
        scratch_shapes=[pltpu.VMEM((HEADS, HEAD_DIM, HEAD_DIM), F32), pltpu.VMEM((CHUNK, D_MODEL), F32)],
        compiler_params=_params(("arbitrary",)),
    )(main, main, main, lb_raw)


def _hgrn_bwd(main, lb_raw, states, do, tt, riders=()):
    t = main.shape[0]
    n_tiles = t // tt
    n_local = tt // CHUNK

    def rev(col_block):
        return pl.BlockSpec((tt, D_MODEL), lambda i: (n_tiles - 1 - i, col_block))

    def body(q_ref, f_ref, i_ref, lb_ref, st_ref, do_ref, dqfi_ref, dlb_ref, ds_scr, g_scr, acc_scr):
        @pl.when(pl.program_id(0) == 0)
        def _():
            ds_scr[...] = jnp.zeros_like(ds_scr)
            acc_scr[...] = jnp.zeros_like(acc_scr)

        lb = _lower_bound(lb_ref[...])
        row = lax.broadcasted_iota(jnp.int32, (CHUNK, CHUNK), 0)
        col = lax.broadcasted_iota(jnp.int32, (CHUNK, CHUNK), 1)
        tril = row >= col
        tril_f = tril.astype(F32)
        triu_f = (row <= col).astype(F32)

        def chunk(cc, carry):
            c = n_local - 1 - cc
            rows = pl.ds(pl.multiple_of(c * CHUNK, CHUNK), CHUNK)
            qr = q_ref[rows, :]
            z = _chunk_gates(qr, f_ref[rows, :], lb, g_scr, tril_f)
            qt = (z["q"] * z["e_q"]).astype(BF16)
            kt = (z["k"] * z["e_k"]).astype(BF16)
            qg = (z["q"] * z["e_in"]).astype(BF16)
            kg_f = z["k"] * z["e_out"]
            kg = kg_f.astype(BF16)
            vb = i_ref[rows, :].astype(BF16)
            dob = do_ref[rows, :].astype(BF16)
            dqt, dkt, dqg, dkg, dv, carry_in = [], [], [], [], [], []
            for h in range(HEADS):
                sl = slice(h * HEAD_DIM, (h + 1) * HEAD_DIM)
                st = st_ref[c, h]
                dst = ds_scr[h]
                dst_b = dst.astype(BF16)
                scores = jnp.where(tril, _dot_nt(qt[:, sl], kt[:, sl]), 0.0).astype(BF16)
                dscores = jnp.where(tril, _dot_nt(dob[:, sl], vb[:, sl]), 0.0).astype(BF16)
                dqt.append(_dot(dscores, kt[:, sl]))
                dkt.append(_dot_tn(dscores, qt[:, sl]))
                dv.append(_dot_tn(scores, dob[:, sl]) + _dot_nt(kg[:, sl], dst_b))
                dqg.append(_dot(dob[:, sl], st.astype(BF16)))
                dkg_h = _dot(vb[:, sl], dst_b)
                dkg.append(dkg_h)
                carry_in.append(z["e_last"][:, sl] * _colsum(dst * st) + _colsum(dkg_h * kg_f[:, sl]))
                ds_scr[h] = dst * z["e_last"][:, sl] + _dot_tn(dob[:, sl], qg[:, sl])
            cat = functools.partial(jnp.concatenate, axis=1)
            dq = cat(dqt) * z["e_q"] + cat(dqg) * z["e_in"]
            dk = cat(dkt) * z["e_k"] + cat(dkg) * z["e_out"]
            dgate = (qt.astype(F32) * cat(dqt) - kt.astype(F32) * cat(dkt)) + (z["q"] * z["e_in"] * cat(dqg) - kg_f * cat(dkg))
            dlogf = _dot_exact(triu_f, dgate) + cat(carry_in)
            df = dlogf / z["f"] - dk
            sg = z["sg"]
            sq = z["sq"]
            acc_scr[...] += _colsum(df * (1.0 - sg))
            dqfi_ref[rows, 0:D_MODEL] = (dq * (sq * (1.0 + qr * (1.0 - sq)))).astype(BF16)
            dqfi_ref[rows, D_MODEL:2 * D_MODEL] = (df * (1.0 - lb) * sg * (1.0 - sg)).astype(BF16)
            dqfi_ref[rows, 2 * D_MODEL:3 * D_MODEL] = cat(dv).astype(BF16)
            return carry

        lax.fori_loop(0, n_local, chunk, 0)
        d0 = acc_scr[...] * lb * (1.0 - lb)
        dlb_ref[0:1, :] = d0
        dlb_ref[1:2, :] = -d0

    return _hosted(
        riders, body, name="hgrn_bwd", grid=(n_tiles,),
        in_specs=[rev(0), rev(1), rev(2), _resident((2, D_MODEL)),
                  pl.BlockSpec((n_local, HEADS, HEAD_DIM, HEAD_DIM), lambda i: (n_tiles - 1 - i, 0, 0, 0)),
                  rev(0)],
        out_specs=[pl.BlockSpec((tt, 3 * D_MODEL), lambda i: (n_tiles - 1 - i, 0)), _acc((2, D_MODEL))],
        out_shape=[jax.ShapeDtypeStruct((t, 3 * D_MODEL), BF16), jax.ShapeDtypeStruct((2, D_MODEL), F32)],
        scratch_shapes=[pltpu.VMEM((HEADS, HEAD_DIM, HEAD_DIM), F32), pltpu.VMEM((CHUNK, D_MODEL), F32),
                        pltpu.VMEM((1, D_MODEL), F32)],
        compiler_params=_params(("arbitrary",)),
    )(main, main, main, lb_raw, states, do)


def _head_norm(o):
    rs, ns = [], []
    for h in range(HEADS):
        oh = o[:, h * HEAD_DIM:(h + 1) * HEAD_DIM]
        r, n = _rms(oh)
        rs.append(jnp.broadcast_to(r, oh.shape))
        ns.append(n)
    return jnp.concatenate(rs, axis=1), jnp.concatenate(ns, axis=1)


def _head_norm_bwd(dn, n, r):
    outs = []
    for h in range(HEADS):
        sl = slice(h * HEAD_DIM, (h + 1) * HEAD_DIM)
        outs.append(_rms_bwd(dn[:, sl], n[:, sl], r[:, sl]))
    return jnp.concatenate(outs, axis=1)


def _window_counts(first_row, tm):
    pos = (first_row + 1 + lax.broadcasted_iota(jnp.int32, (tm, 1), 0)).astype(F32)
    return [jnp.minimum(pos, float(w)) for w in POOL_WINDOWS]


def _post_fwd(o, main, pool_r, gates, x1, onorm, pool_w, pool_scale, sqw, wbw, tm, riders=()):
    t = o.shape[0]
    ext_rows = tm + POOL_HALO

    def body(o_ref, og_ref, u_ref, gt_ref, x1_ref, on_ref, pw_ref, ps_ref, wa_ref, wout_ref, wb_ref,
             x2_ref, ya_ref, yb_ref, pooled_ref, ext):
        i = pl.program_id(0)

        @pl.when(i == 0)
        def _():
            ext[0:POOL_HALO, :] = jnp.zeros((POOL_HALO, POOL_WIDTH), F32)

        _, n = _head_norm(o_ref[...])
        og = og_ref[...]
        oa = (n * on_ref[...] * (og * _sigmoid(og))).astype(BF16)
        ya = _dot(oa, wa_ref[...])

        u = u_ref[...]
        ext[POOL_HALO:ext_rows, :] = u
        e = ext[...]
        counts = _window_counts(i * tm, tm)
        pooled = []
        for gidx, w in enumerate(POOL_WINDOWS):
            s = e[:, gidx * HEAD_DIM:(gidx + 1) * HEAD_DIM]
            shift = 1
            while shift < w:
                s = s + pltpu.roll(s, shift, axis=0)
                shift *= 2
            pooled.append(s[POOL_HALO:, :] / counts[gidx] - u[:, gidx * HEAD_DIM:(gidx + 1) * HEAD_DIM])
        ext[0:POOL_HALO, :] = ext[tm:ext_rows, :]
        pooled_b = [pg.astype(BF16) for pg in pooled]
        pooled_ref[...] = jnp.concatenate(pooled_b, axis=1)
        mixed = jnp.concatenate([_dot(pooled_b[gidx], pw_ref[gidx].astype(BF16)) for gidx in range(len(POOL_WINDOWS))],
                                axis=1) * ps_ref[...]
        mixed_b = mixed.astype(BF16)
        yb = jnp.concatenate([_dot(mixed_b, wb_ref[j]) for j in range(N_CHIPS)], axis=1)

        gt = gt_ref[...]
        y = _sigmoid(gt[:, :D_MODEL]) * ya + _sigmoid(gt[:, D_MODEL:]) * yb
        x2_ref[...] = x1_ref[...] + _dot(y.astype(BF16), wout_ref[...])
        ya_ref[...] = ya.astype(BF16)
        yb_ref[...] = yb.astype(BF16)

    return _hosted(
        riders, body, name="post_fwd", grid=(t // tm,),
        in_specs=[_rows(tm, D_MODEL), _rows(tm, D_MODEL, 3), _rows(tm, POOL_WIDTH), _rows(tm, GATE_COLS), _rows(tm, D_MODEL),
                  _resident((1, D_MODEL)), _resident(pool_w.shape), _resident((1, POOL_WIDTH)),
                  _pick(sqw.shape, 0), _pick(sqw.shape, 1), _resident(wbw.shape)],
        out_specs=[_rows(tm, D_MODEL), _rows(tm, D_MODEL), _rows(tm, D_MODEL), _rows(tm, POOL_WIDTH)],
        out_shape=[jax.ShapeDtypeStruct((t, D_MODEL), F32), jax.ShapeDtypeStruct((t, D_MODEL), BF16),
                   jax.ShapeDtypeStruct((t, D_MODEL), BF16), jax.ShapeDtypeStruct((t, POOL_WIDTH), BF16)],
        scratch_shapes=[pltpu.VMEM((ext_rows, POOL_WIDTH), F32)],
        compiler_params=_params(("arbitrary",)),
    )(o, main, pool_r, gates, x1, onorm, pool_w, pool_scale, sqw, sqw, wbw)


def _post_bwd(dx2, o, main, gates, ya, yb, pooled, onorm, pool_w, pool_scale, sqw, wbw, tm, riders=()):
    t = o.shape[0]
    n_tiles = t // tm
    ext_rows = tm + POOL_HALO
    n_groups = len(POOL_WINDOWS)

    def rev(cols, col_block=0):
        return pl.BlockSpec((tm, cols), lambda i: (n_tiles - 1 - i, col_block))

    def body(dx2_ref, o_ref, og_ref, gt_ref, ya_ref, yb_ref, pooled_ref, on_ref, pw_ref, ps_ref, wa_ref, wout_ref, wb_ref,
             do_ref, dog_ref, du_ref, dgt_ref, dwa_ref, dwout_ref, dwb_ref, dpw_ref, dps_ref, don_ref, ext):
        i = pl.program_id(0)

        @pl.when(i == 0)
        def _():
            ext[tm:ext_rows, :] = jnp.zeros((POOL_HALO, POOL_WIDTH), F32)
            for ref in (dwa_ref, dwout_ref, dwb_ref, dpw_ref, dps_ref, don_ref):
                ref[...] = jnp.zeros_like(ref)

        dx2b = dx2_ref[...].astype(BF16)
        dy = _dot_nt(dx2b, wout_ref[...])
        gt = gt_ref[...]
        sga = _sigmoid(gt[:, :D_MODEL])
        sgb = _sigmoid(gt[:, D_MODEL:])
        ya = ya_ref[...].astype(F32)
        yb = yb_ref[...].astype(F32)
        y = (sga * ya + sgb * yb).astype(BF16)
        dwout_ref[...] += _dot_tn(y, dx2b)
        dya = (dy * sga).astype(BF16)
        dyb = (dy * sgb).astype(BF16)
        dgt_ref[:, :D_MODEL] = (dy * ya * sga * (1.0 - sga)).astype(BF16)
        dgt_ref[:, D_MODEL:] = (dy * yb * sgb * (1.0 - sgb)).astype(BF16)

        r, n = _head_norm(o_ref[...])
        onv = on_ref[...]
        og = og_ref[...]
        sog = _sigmoid(og)
        silu_og = og * sog
        normed = n * onv
        dwa_ref[...] += _dot_tn((normed * silu_og).astype(BF16), dya)
        doa = _dot_nt(dya, wa_ref[...])
        dog_ref[...] = (doa * normed * (sog * (1.0 + og * (1.0 - sog)))).astype(BF16)
        dnormed = doa * silu_og
        don_ref[...] += _colsum(dnormed * n)
        do_ref[...] = _head_norm_bwd(dnormed * onv, n, r)

        psv = ps_ref[...]
        dmixed = _dot_nt(dyb[:, :256], wb_ref[0])
        for j in range(1, N_CHIPS):
            dmixed += _dot_nt(dyb[:, j * 256:(j + 1) * 256], wb_ref[j])
        pooled_b = pooled_ref[...]
        pm = jnp.concatenate(
            [_dot(pooled_b[:, gidx * HEAD_DIM:(gidx + 1) * HEAD_DIM], pw_ref[gidx].astype(BF16)) for gidx in range(n_groups)],
            axis=1)
        mixed_b = (pm * psv).astype(BF16)
        for j in range(N_CHIPS):
            dwb_ref[j] += _dot_tn(mixed_b, dyb[:, j * 256:(j + 1) * 256])
        dps_ref[...] += _colsum(dmixed * pm)
        dpm = (dmixed * psv).astype(BF16)
        counts = _window_counts((n_tiles - 1 - i) * tm, tm)
        dpooled = []
        for gidx in range(n_groups):
            sl = slice(gidx * HEAD_DIM, (gidx + 1) * HEAD_DIM)
            dpw_ref[gidx] += _dot_tn(pooled_b[:, sl], dpm[:, sl])
            dpooled.append(_dot_nt(dpm[:, sl], pw_ref[gidx].astype(BF16)))
        ext[0:tm, :] = jnp.concatenate([dpooled[gidx] / counts[gidx] for gidx in range(n_groups)], axis=1)
        e = ext[...]
        du = []
        for gidx, w in enumerate(POOL_WINDOWS):
            s = e[:, gidx * HEAD_DIM:(gidx + 1) * HEAD_DIM]
            shift = 1
            while shift < w:
                s = s + pltpu.roll(s, ext_rows - shift, axis=0)
                shift *= 2
            du.append(s[:tm, :] - dpooled[gidx])
        ext[tm:ext_rows, :] = ext[0:POOL_HALO, :]
        du_ref[...] = jnp.concatenate(du, axis=1).astype(BF16)

    wa_shape = (D_MODEL, D_MODEL)
    return _hosted(
        riders, body, name="post_bwd", grid=(n_tiles,),
        in_specs=[rev(D_MODEL), rev(D_MODEL), rev(D_MODEL, 3), rev(GATE_COLS), rev(D_MODEL), rev(D_MODEL), rev(POOL_WIDTH),
                  _resident((1, D_MODEL)), _resident(pool_w.shape), _resident((1, POOL_WIDTH)),
                  _pick(sqw.shape, 0), _pick(sqw.shape, 1), _resident(wbw.shape)],
        out_specs=[rev(D_MODEL), rev(D_MODEL), rev(POOL_WIDTH), rev(GATE_COLS),
                   _acc(wa_shape), _acc(wa_shape), _acc(wbw.shape), _acc(pool_w.shape), _acc((1, POOL_WIDTH)),
                   _acc((1, D_MODEL))],
        out_shape=[jax.ShapeDtypeStruct((t, D_MODEL), F32), jax.ShapeDtypeStruct((t, D_MODEL), BF16),
                   jax.ShapeDtypeStruct((t, POOL_WIDTH), BF16), jax.ShapeDtypeStruct((t, GATE_COLS), BF16),
                   jax.ShapeDtypeStruct(wa_shape, F32), jax.ShapeDtypeStruct(wa_shape, F32),
                   jax.ShapeDtypeStruct(wbw.shape, F32), jax.ShapeDtypeStruct(pool_w.shape, F32),
                   jax.ShapeDtypeStruct((1, POOL_WIDTH), F32), jax.ShapeDtypeStruct((1, D_MODEL), F32)],
        scratch_shapes=[pltpu.VMEM((ext_rows, POOL_WIDTH), F32)],
        compiler_params=_params(("arbitrary",)),
    )(dx2, o, main, gates, ya, yb, pooled, onorm, pool_w, pool_scale, sqw, sqw, wbw)


def _tail(x3, p, target, g_ple, g_post, g_final, sqw, wpw, tm, riders=()):
    t = x3.shape[0]
    pd = p.shape[1]

    def body(x_ref, p_ref, tg_ref, g4_ref, g5_ref, g6_ref, wg_ref, wp_ref,
             dx_ref, loss_ref, dwg_ref, dwp_ref, dg4_ref, dg5_ref, dg6_ref):
        @pl.when(pl.program_id(0) == 0)
        def _():
            for ref in (loss_ref, dwg_ref, dwp_ref, dg4_ref, dg5_ref, dg6_ref):
                ref[...] = jnp.zeros_like(ref)

        x3v = x_ref[...]
        g4, g5, g6 = g4_ref[...], g5_ref[...], g6_ref[...]
        r4, n4 = _rms(x3v)
        h4 = (n4 * g4).astype(BF16)
        gate = _sigmoid(_dot(h4, wg_ref[...]))
        pb = p_ref[...].astype(BF16)
        r5, n5 = _rms(jnp.concatenate([_dot(pb, wp_ref[j]) for j in range(N_CHIPS)], axis=1))
        emb = n5 * g5
        r6, n6 = _rms(x3v + gate * emb)
        diff = n6 * g6 - tg_ref[...]
        loss_ref[...] += 0.5 * jnp.sum(jnp.mean(diff * diff, axis=-1, keepdims=True), axis=0, keepdims=True)
        dout = diff * (1.0 / D_MODEL)
        dg6_ref[...] += _colsum(dout * n6)
        dx4 = _rms_bwd(dout * g6, n6, r6)
        demb = dx4 * gate
        dg5_ref[...] += _colsum(demb * n5)
        dpre = _rms_bwd(demb * g5, n5, r5).astype(BF16)
        for j in range(N_CHIPS):
            dwp_ref[j] += _dot_tn(pb, dpre[:, j * pd:(j + 1) * pd])
        dz = (dx4 * emb * gate * (1.0 - gate)).astype(BF16)
        dwg_ref[...] += _dot_tn(h4, dz)
        dh4 = _dot_nt(dz, wg_ref[...])
        dg4_ref[...] += _colsum(dh4 * n4)
        dx_ref[...] = dx4 + _rms_bwd(dh4 * g4, n4, r4)

    sq_shape = (D_MODEL, D_MODEL)
    vec = (1, D_MODEL)
    return _hosted(
        riders, body, name="tail", grid=(t // tm,),
        in_specs=[_rows(tm, D_MODEL), _rows(tm, pd), _rows(tm, D_MODEL), _resident(vec), _resident(vec), _resident(vec),
                  _pick(sqw.shape, 2), _resident(wpw.shape)],
        out_specs=[_rows(tm, D_MODEL), _acc((1, 1)), _acc(sq_shape), _acc(wpw.shape), _acc(vec), _acc(vec), _acc(vec)],
        out_shape=[jax.ShapeDtypeStruct((t, D_MODEL), F32), jax.ShapeDtypeStruct((1, 1), F32),
                   jax.ShapeDtypeStruct(sq_shape, F32), jax.ShapeDtypeStruct(wpw.shape, F32),
                   jax.ShapeDtypeStruct(vec, F32), jax.ShapeDtypeStruct(vec, F32), jax.ShapeDtypeStruct(vec, F32)],
        compiler_params=_params(("arbitrary",)),
    )(x3, p, target, g_ple, g_post, g_final, sqw, wpw)


def _position():
    return lax.axis_index("x"), lax.axis_index("y"), lax.axis_index("c")


def _other_chips(x, y):
    return [(1 - x, y), (x, 1 - y), (1 - x, 1 - y)]


def _remote(src, dst, send_sems, recv_sems, k, device):
    return pltpu.make_async_remote_copy(src_ref=src, dst_ref=dst, send_sem=send_sems.at[k], recv_sem=recv_sems.at[k],
                                        device_id=device, device_id_type=MESH)


def _gather_rider(shards, forward_at):
    n = len(shards)

    def copies(ins, outs, sems):
        send_sems, recv_sems, local_sems = sems
        x, y, c = _position()
        mine = 2 * x + y
        local = [pltpu.make_async_copy(ins[a], outs[a].at[:, mine], local_sems.at[a]) for a in range(n)]
        first, passed, arriving = [], [], []
        for k, (cx, cy) in enumerate(_other_chips(x, y)):
            theirs = 2 * cx + cy
            for a in range(n):
                first.append(_remote(ins[a].at[:, c], outs[a].at[:, mine, c], send_sems, recv_sems, k * n + a, (cx, cy, c)))
                block = outs[a].at[:, theirs, c]
                passed.append(_remote(block, block, send_sems, recv_sems, (3 + k) * n + a, (x, y, 1 - c)))
                other = outs[a].at[:, theirs, 1 - c]
                arriving.append(_remote(other, other, send_sems, recv_sems, (3 + k) * n + a, (x, y, 1 - c)))
        return local, first, passed, arriving

    def begin(ins, outs, sems):
        local, first, _, _ = copies(ins, outs, sems)
        for cp in local + first:
            cp.start()

    def forward(ins, outs, sems):
        _, first, passed, _ = copies(ins, outs, sems)
        for got, cp in zip(first, passed):
            got.wait_recv()
            cp.start()

    def finish(ins, outs, sems):
        local, first, passed, arriving = copies(ins, outs, sems)
        for cp in arriving:
            cp.wait_recv()
        for cp in first + passed:
            cp.wait_send()
        for cp in local:
            cp.wait()

    return _Rider(shards, [jax.ShapeDtypeStruct((s.shape[0], N_CHIPS) + s.shape[1:], s.dtype) for s in shards],
                  [pltpu.SemaphoreType.DMA((6 * n,)), pltpu.SemaphoreType.DMA((6 * n,)), pltpu.SemaphoreType.DMA((n,))],
                  [(0, begin), (forward_at, forward), (1, finish)])


def _exchange_rider(arrays, out_shape, n_copies, transfers, n_local=0):
    def copies(ins, outs, sems):
        send_sems, recv_sems, local_sems = sems
        remote, local = transfers(ins, outs)
        return ([_remote(src, dst, send_sems, recv_sems, i, dev) for i, (src, dst, dev) in enumerate(remote)],
                [pltpu.make_async_copy(src, dst, local_sems.at[i]) for i, (src, dst) in enumerate(local)])

    def begin(ins, outs, sems):
        remote, local = copies(ins, outs, sems)
        for cp in remote + local:
            cp.start()

    def finish(ins, outs, sems):
        remote, local = copies(ins, outs, sems)
        for cp in remote:
            cp.wait_recv()
        for cp in remote:
            cp.wait_send()
        for cp in local:
            cp.wait()

    return _Rider(arrays, out_shape,
                  [pltpu.SemaphoreType.DMA((n_copies,)), pltpu.SemaphoreType.DMA((n_copies,)),
                   pltpu.SemaphoreType.DMA((max(n_local, 1),))],
                  [(0, begin), (1, finish)])


def _pair_rider(partials):
    def transfers(ins, outs):
        x, y, c = _position()
        return [(ins[a].at[:, :, 1 - c], outs[a], (x, y, 1 - c)) for a in range(len(partials))], []

    shapes = [jax.ShapeDtypeStruct(g.shape[:2] + g.shape[3:], g.dtype) for g in partials]
    return _exchange_rider(partials, shapes, len(partials), transfers)


def _chips_rider(sums):
    n = len(sums)

    def transfers(ins, outs):
        x, y, c = _position()
        return [(ins[a].at[:, 2 * cx + cy], outs[a].at[:, k], (cx, cy, c))
                for k, (cx, cy) in enumerate(_other_chips(x, y)) for a in range(n)], []

    shapes = [jax.ShapeDtypeStruct((q.shape[0], 3) + q.shape[2:], q.dtype) for q in sums]
    return _exchange_rider(sums, shapes, 3 * n, transfers)


def _share_rider(halves):
    def transfers(ins, outs):
        x, y, c = _position()
        return [(ins[a], outs[a], (x, y, 1 - c)) for a in range(len(halves))], []

    return _exchange_rider(halves, [jax.ShapeDtypeStruct(h.shape, h.dtype) for h in halves], len(halves), transfers)


def _small_rider(pack):
    flips = [(fx, fy, fc) for fx in (0, 1) for fy in (0, 1) for fc in (0, 1)][1:]

    def transfers(ins, outs):
        x, y, c = _position()
        slot = outs[0].at[4 * x + 2 * y + c]
        flip = lambda v, f: v + f - 2 * v * f
        return [(ins[0], slot, (flip(x, fx), flip(y, fy), flip(c, fc))) for fx, fy, fc in flips], [(ins[0], slot)]

    return _exchange_rider([pack], [jax.ShapeDtypeStruct((8,) + pack.shape, pack.dtype)], len(flips), transfers, n_local=1)


def _alone(rider, name):
    return _hosted([rider], lambda: None, name=name, in_specs=[], out_specs=[], out_shape=[])()[1][0]


def _add_pair(mine, theirs, c, tag):
    l, _, hr, cols = theirs.shape

    def body(c_ref, mine_ref, theirs_ref, out_ref):
        out_ref[...] = (mine_ref[...] + theirs_ref[...]).astype(BF16)

    block = (None, None, hr, cols)
    return pl.pallas_call(
        body, name=f"add_pair_{tag}",
        grid_spec=pltpu.PrefetchScalarGridSpec(
            num_scalar_prefetch=1, grid=(l, N_CHIPS),
            in_specs=[pl.BlockSpec((None, None, None, hr, cols), lambda i, j, s: (i, j, s[0], 0, 0)),
                      pl.BlockSpec(block, lambda i, j, s: (i, j, 0, 0))],
            out_specs=pl.BlockSpec(block, lambda i, j, s: (i, j, 0, 0))),
        out_shape=jax.ShapeDtypeStruct(theirs.shape, BF16),
        compiler_params=_params(("parallel", "parallel")),
    )(c.reshape(1), mine, theirs)


def _add_chips(part, received, mine, tag):
    l, _, hr, cols = received.shape

    def body(j_ref, part_ref, recv_ref, out_ref):
        acc = part_ref[...].astype(F32)
        for k in range(3):
            acc += recv_ref[k].astype(F32)
        out_ref[...] = acc

    return pl.pallas_call(
        body, name=f"add_chips_{tag}",
        grid_spec=pltpu.PrefetchScalarGridSpec(
            num_scalar_prefetch=1, grid=(l,),
            in_specs=[pl.BlockSpec((None, None, hr, cols), lambda i, s: (i, s[0], 0, 0)),
                      pl.BlockSpec((None, 3, hr, cols), lambda i, s: (i, 0, 0, 0))],
            out_specs=pl.BlockSpec((None, hr, cols), lambda i, s: (i, 0, 0))),
        out_shape=jax.ShapeDtypeStruct((l, hr, cols), F32),
        compiler_params=_params(("parallel",)),
    )(mine.reshape(1), part, received)


def _adam_update(w, g, m, v):
    m2 = ADAM_B1 * m + (1.0 - ADAM_B1) * g
    v2 = ADAM_B2 * v + (1.0 - ADAM_B2) * jnp.square(g)
    m_hat = m2 / (1.0 - ADAM_B1 ** ADAM_STEP)
    v_hat = v2 / (1.0 - ADAM_B2 ** ADAM_STEP)
    return -ADAM_LR * (m_hat / (jnp.sqrt(v_hat) + ADAM_EPS) + ADAM_WD * w), m2, v2


def _row_tile(rows, cols, limit):
    for cand in (rows, 512, 352, 256, 176, 128, 64, 32, 16, 8):
        if rows % cand == 0 and cand * cols * 4 <= limit:
            return cand
    return rows


def _adamw(w, g, m, v, tag):
    rows, cols = w.shape
    tr = _row_tile(rows, cols, 2 * 1024 * 1024)

    def body(w_ref, g_ref, m_ref, v_ref, d_ref, nm_ref, nv_ref):
        d_ref[...], nm_ref[...], nv_ref[...] = _adam_update(w_ref[...], g_ref[...], m_ref[...], v_ref[...])

    spec = pl.BlockSpec((tr, cols), lambda i: (i, 0))
    shape = jax.ShapeDtypeStruct((rows, cols), F32)
    return pl.pallas_call(
        body, name=f"adamw_{tag}", grid=(rows // tr,), in_specs=[spec] * 4, out_specs=[spec] * 3, out_shape=[shape] * 3,
        compiler_params=_params(("parallel",)),
    )(w, g, m, v)


def _adamw_halves(w, own, other, m, v, c, tag):
    _, hr, cols = w.shape
    tr = _row_tile(hr, cols, 1024 * 1024)

    def body(c_ref, w_ref, own_ref, other_ref, m_ref, v_ref, g_ref, d_ref, nm_ref, nv_ref):
        gv = jnp.where(pl.program_id(0) == c_ref[0], own_ref[...], other_ref[...])
        g_ref[...] = gv
        d_ref[...], nm_ref[...], nv_ref[...] = _adam_update(w_ref[...], gv, m_ref[...], v_ref[...])

    full = pl.BlockSpec((None, tr, cols), lambda h, i, s: (h, i, 0))
    half = pl.BlockSpec((tr, cols), lambda h, i, s: (i, 0))
    shape = jax.ShapeDtypeStruct((2, hr, cols), F32)
    return pl.pallas_call(
        body, name=f"adamw_{tag}",
        grid_spec=pltpu.PrefetchScalarGridSpec(num_scalar_prefetch=1, grid=(2, hr // tr),
                                               in_specs=[full, half, half, full, full], out_specs=[full] * 4),
        out_shape=[shape] * 4,
        compiler_params=_params(("parallel", "parallel")),
    )(c.reshape(1), w, own, other, m, v)


def _adamw_small(w, gathered, m, v):
    def body(w_ref, g_ref, m_ref, v_ref, sum_ref, d_ref, nm_ref, nv_ref):
        gv = g_ref[0]
        for i in range(1, g_ref.shape[0]):
            gv += g_ref[i]
        sum_ref[...] = gv
        d_ref[...], nm_ref[...], nv_ref[...] = _adam_update(w_ref[...], gv, m_ref[...], v_ref[...])

    shape = jax.ShapeDtypeStruct(w.shape, F32)
    return pl.pallas_call(body, name="adamw_small", out_shape=[shape] * 4, compiler_params=_params())(w, gathered, m, v)


VECTOR_PARAMS = ("ffn1_norm", "mix_norm", "hgrn_lb", "hgrn_onorm", "ffn2_norm", "ple_norm", "ple_post_norm", "final_norm",
                 "pool_scale")
ALL_PARAMS = ("ffn1_norm", "ffn1_w1", "ffn1_w3", "ffn1_w2", "mix_norm", "w_in", "hgrn_lb", "hgrn_onorm", "w_branch_a",
              "pool_w", "pool_scale", "w_branch_b", "w_out", "ffn2_norm", "ffn2_w1", "ffn2_w3", "ffn2_w2", "ple_norm",
              "ple_w_gate", "ple_w_proj", "ple_post_norm", "final_norm")
TILE_ROWS = 8


def _pack_small(values):
    parts = []
    for name in VECTOR_PARAMS:
        a = values[name].reshape(-1, values[name].shape[-1])
        parts.append(jnp.pad(a, ((0, TILE_ROWS - a.shape[0]), (0, D_MODEL - a.shape[1]))))
    parts.append(values["pool_w"].reshape(-1, D_MODEL))
    return jnp.concatenate(parts, axis=0)


def _unpack_small(pack, shapes):
    out = {}
    for i, name in enumerate(VECTOR_PARAMS):
        shape = shapes[name]
        rows = 1 if len(shape) == 1 else shape[0]
        out[name] = pack[i * TILE_ROWS:i * TILE_ROWS + rows, :shape[-1]].reshape(shape)
    out["pool_w"] = pack[len(VECTOR_PARAMS) * TILE_ROWS:].reshape(shapes["pool_w"])
    return out


def _halved(a, lead):
    return a.reshape(lead, 2, -1, a.shape[-1])


def _shard_halves(a, lead):
    return a.reshape(lead, N_CHIPS, 2, -1, a.shape[-1])


def _reduce_finish(names, own, other, w, m, v, core, out):
    for name, g_own, g_other in zip(names, own, other):
        shape = w[name].shape
        g, d, nm, nv = _adamw_halves(_halved(w[name], 1)[0], g_own[0], g_other[0], _halved(m[name], 1)[0],
                                     _halved(v[name], 1)[0], core, name)
        out["grad"][name], out["delta"][name], out["new_m"][name], out["new_v"][name] = (
            a.reshape(shape) for a in (g, d, nm, nv))


def _reduce_finish_transposed(names, own, other, w, m, v, core, out):
    first = jnp.concatenate([own, other], axis=1)
    second = jnp.concatenate([other, own], axis=1)
    full = jnp.where(core == 0, first, second)
    for i, name in enumerate(names):
        shape = w[name].shape
        g = full[i].T
        d, nm, nv = _adamw(w[name][0], g, m[name][0], v[name][0], name)
        out["grad"][name] = g.reshape(shape)
        out["delta"][name], out["new_m"][name], out["new_v"][name] = (a.reshape(shape) for a in (d, nm, nv))


def kernel(x, p, ffn1_norm, ffn1_w1, ffn1_w3, ffn1_w2, mix_norm, w_in, hgrn_lb, hgrn_onorm, w_branch_a, pool_w, pool_scale, w_branch_b, w_out, ffn2_norm, ffn2_w1, ffn2_w3, ffn2_w2, ple_norm, ple_w_gate, ple_w_proj, ple_post_norm, final_norm, loss_target, m_ffn1_norm, m_ffn1_w1, m_ffn1_w3, m_ffn1_w2, m_mix_norm, m_w_in, m_hgrn_lb, m_hgrn_onorm, m_w_branch_a, m_pool_w, m_pool_scale, m_w_branch_b, m_w_out, m_ffn2_norm, m_ffn2_w1, m_ffn2_w3, m_ffn2_w2, m_ple_norm, m_ple_w_gate, m_ple_w_proj, m_ple_post_norm, m_final_norm, v_ffn1_norm, v_ffn1_w1, v_ffn1_w3, v_ffn1_w2, v_mix_norm, v_w_in, v_hgrn_lb, v_hgrn_onorm, v_w_branch_a, v_pool_w, v_pool_scale, v_w_branch_b, v_w_out, v_ffn2_norm, v_ffn2_w1, v_ffn2_w3, v_ffn2_w2, v_ple_norm, v_ple_w_gate, v_ple_w_proj, v_ple_post_norm, v_final_norm):
    args = dict(locals())
    w = {name: args[name] for name in ALL_PARAMS}
    m = {name: args["m_" + name] for name in ALL_PARAMS}
    v = {name: args["v_" + name] for name in ALL_PARAMS}
    cx, cy, cc = _position()
    chip = (2 * cx + cy).astype(jnp.int32)
    core = cc.astype(jnp.int32)
    xs, ps, target = x[0], p[0, 0], loss_target[0]
    t = xs.shape[0]
    tm = min(256, t)
    tt = min(256, t)
    tk = min(512, t)
    small = {name: w[name] for name in VECTOR_PARAMS}
    small["final_norm"] = w["final_norm"].reshape(1, D_MODEL)
    pool_w0 = w["pool_w"][0]

    ffn_shard = lambda i: _halved(jnp.stack([w[f"ffn{i}_w1"][0].T, w[f"ffn{i}_w3"][0].T, w[f"ffn{i}_w2"][0]]).astype(BF16), 3)
    sq_shard = _halved(jnp.stack([w["w_branch_a"][0], w["w_out"][0], w["ple_w_gate"][0]]).astype(BF16), 3)
    win_shard, wb_shard, wp_shard = (_halved(w[n].astype(BF16), 1) for n in ("w_in", "w_branch_b", "ple_w_proj"))

    (ffn1w,) = _alone(_gather_rider([ffn_shard(1)], 0.5), "gather_ffn1")
    ffn1w = ffn1w.reshape(3, D_FF, D_MODEL)
    (x1, a1, b1), ((winw,),) = _ffn_fwd(xs, small["ffn1_norm"], ffn1w, 1, tm, [_gather_rider([win_shard], 0.6)])
    winw = winw.reshape(N_CHIPS, D_MODEL, SHARD_IN_COLS)
    (main, pool_r, gates), ((sqw, wbw, wpw),) = _mix_fwd(x1, small["mix_norm"], winw, tm,
                                                          [_gather_rider([sq_shard, wb_shard, wp_shard], 0.5)])
    sqw = sqw.reshape(3, D_MODEL, D_MODEL)
    wbw = wbw.reshape(N_CHIPS, POOL_WIDTH, -1)
    wpw = wpw.reshape(N_CHIPS, ps.shape[1], -1)
    (o, states), ((ffn2w,),) = _hgrn_fwd(main, small["hgrn_lb"], tt, [_gather_rider([ffn_shard(2)], 0.7)])
    ffn2w = ffn2w.reshape(3, D_FF, D_MODEL)
    (x2, ya, yb, pooled), _ = _post_fwd(o, main, pool_r, gates, x1, small["hgrn_onorm"], pool_w0, small["pool_scale"], sqw,
                                       wbw, tm)
    (x3, a2, b2), _ = _ffn_fwd(x2, small["ffn2_norm"], ffn2w, 2, tm)
    (dx3, loss, d_wg, d_wp, d_ple, d_post, d_final), _ = _tail(
        x3, ps, target, small["ple_norm"], small["ple_post_norm"], small["final_norm"], sqw, wpw, tm)
    loss = lax.psum(loss[0, 0], ("x", "y", "c"))

    (dx2, dab2, s2, h3, dxh2, d_ffn2_norm), _ = _ffn_bwd(dx3, x2, small["ffn2_norm"], a2, b2, ffn2w, 2, tm)
    (d_w13_2,), _ = _wgrad(dab2, h3, WGRAD_IN_BLOCKS, "wgrad_ffn2_in", tk)
    (d_w2_2,), _ = _wgrad(s2, dxh2, WGRAD_OUT_BLOCKS, "wgrad_ffn2_out", tk)

    wave1 = [_shard_halves(d_w13_2, 2), _shard_halves(d_w2_2, 1), _shard_halves(d_wg, 1), _shard_halves(d_wp, 1)]
    tags1 = ("w13_2", "w2_2", "wg", "wp")
    (do, dog, du, dgates, d_wa, d_wout, d_wb, d_pool_w, d_pool_scale, d_onorm), (sib1,) = _post_bwd(
        dx2, o, main, gates, ya, yb, pooled, small["hgrn_onorm"], pool_w0, small["pool_scale"], sqw, wbw, tm,
        [_pair_rider(wave1)])
    sums1 = [_add_pair(a, b, core, tag) for a, b, tag in zip(wave1, sib1, tags1)]
    (dqfi, d_lb), (got1,) = _hgrn_bwd(main, small["hgrn_lb"], states, do, tt, [_chips_rider(sums1)])
    own1 = [_add_chips(a, b, chip, tag) for a, b, tag in zip(sums1, got1, tags1)]
    (dx1, dproj, h2, d_mix_norm), (other1,) = _mix_bwd(dqfi, dog, du, dgates, dx2, x1, small["mix_norm"], winw, tm,
                                                        [_share_rider(own1)])
    (d_win,), _ = _wgrad_cols(h2, dproj, N_CHIPS, "wgrad_in", tk)

    wave2 = [_shard_halves(d_wa, 1), _shard_halves(d_wout, 1), _shard_halves(d_wb, 1), _shard_halves(d_win, 1)]
    tags2 = ("wa", "wout", "wb", "win")
    (dx, dab1, s1, h1, dxh1, d_ffn1_norm), (sib2,) = _ffn_bwd(dx1, xs, small["ffn1_norm"], a1, b1, ffn1w, 1, tm,
                                                               [_pair_rider(wave2)])
    sums2 = [_add_pair(a, b, core, tag) for a, b, tag in zip(wave2, sib2, tags2)]
    vecs = dict(ffn1_norm=d_ffn1_norm, mix_norm=d_mix_norm, hgrn_lb=d_lb, hgrn_onorm=d_onorm, ffn2_norm=d_ffn2_norm,
                ple_norm=d_ple, ple_post_norm=d_post, final_norm=d_final, pool_scale=d_pool_scale, pool_w=d_pool_w)
    (d_w13_1,), (got2, (small_all,)) = _wgrad(dab1, h1, WGRAD_IN_BLOCKS, "wgrad_ffn1_in", tk,
                                              [_chips_rider(sums2), _small_rider(_pack_small(vecs))])
    own2 = [_add_chips(a, b, chip, tag) for a, b, tag in zip(sums2, got2, tags2)]
    (d_w2_1,), (other2,) = _wgrad(s1, dxh1, WGRAD_OUT_BLOCKS, "wgrad_ffn1_out", tk, [_share_rider(own2)])

    wave3 = [_shard_halves(d_w13_1, 2), _shard_halves(d_w2_1, 1)]
    tags3 = ("w13_1", "w2_1")
    sib3 = _alone(_pair_rider(wave3), "pair_last")
    sums3 = [_add_pair(a, b, core, tag) for a, b, tag in zip(wave3, sib3, tags3)]
    got3 = _alone(_chips_rider(sums3), "chips_last")
    own3 = [_add_chips(a, b, chip, tag) for a, b, tag in zip(sums3, got3, tags3)]
    other3 = _alone(_share_rider(own3), "share_last")

    out = dict(grad={}, delta={}, new_m={}, new_v={})
    _reduce_finish_transposed(("ffn2_w1", "ffn2_w3"), own1[0], other1[0], w, m, v, core, out)
    _reduce_finish(("ffn2_w2", "ple_w_gate", "ple_w_proj"), own1[1:], other1[1:], w, m, v, core, out)
    _reduce_finish(("w_branch_a", "w_out", "w_branch_b", "w_in"), own2, other2, w, m, v, core, out)
    _reduce_finish_transposed(("ffn1_w1", "ffn1_w3"), own3[0], other3[0], w, m, v, core, out)
    _reduce_finish(("ffn1_w2",), own3[1:], other3[1:], w, m, v, core, out)

    shapes = {name: w[name].shape for name in VECTOR_PARAMS + ("pool_w",)}
    results = _adamw_small(_pack_small(w), small_all, _pack_small(m), _pack_small(v))
    for key, pack in zip(("grad", "delta", "new_m", "new_v"), results):
        out[key].update(_unpack_small(pack, shapes))

    return (loss, dx[None], *[out["grad"][n] for n in ALL_PARAMS], *[out["delta"][n] for n in ALL_PARAMS],
            *[out["new_m"][n] for n in ALL_PARAMS], *[out["new_v"][n] for n in ALL_PARAMS])
```

```python
import functools

import jax
import jax.numpy as jnp
from jax import lax
from jax.experimental import pallas as pl
from jax.experimental.pallas import tpu as pltpu

F32 = jnp.float32
BF16 = jnp.bfloat16
MESH = pl.DeviceIdType.MESH

D_MODEL = 1024
D_FF = 2816
HEADS = 8
HEAD_DIM = 128
POOL_WIDTH = 512
POOL_WINDOWS = (2, 4, 8, 16)
POOL_HALO = 16
N_CHIPS = 4
EPS = 1e-6
CHUNK = 64
MAIN_COLS = 4096
GATE_COLS = 2048
SHARD_IN_COLS = 1664

ADAM_LR = 0.001
ADAM_B1 = 0.9
ADAM_B2 = 0.999
ADAM_EPS = 1e-08
ADAM_WD = 0.01
ADAM_STEP = 10

VMEM_LIMIT = 56 * 1024 * 1024
WGRAD_IN_BLOCKS = 4
WGRAD_OUT_BLOCKS = 2


def _params(semantics=None, vmem=VMEM_LIMIT):
    return pltpu.CompilerParams(dimension_semantics=semantics, vmem_limit_bytes=vmem)


def _dot(a, b):
    return jnp.dot(a, b, preferred_element_type=F32)


def _dot_nt(a, b):
    return lax.dot_general(a, b, (((1,), (1,)), ((), ())), preferred_element_type=F32)


def _dot_tn(a, b):
    return lax.dot_general(a, b, (((0,), (0,)), ((), ())), preferred_element_type=F32)


def _tri_sum(tri, x):
    hi = x.astype(BF16)
    lo = (x - hi.astype(F32)).astype(BF16)
    return _dot(tri, hi) + _dot(tri, lo)


def _sigmoid(x):
    return jax.nn.sigmoid(x)


def _resident(shape):
    zeros = (0,) * len(shape)
    return pl.BlockSpec(shape, lambda *_: zeros, pipeline_mode=pl.Buffered(1))


def _pick(shape, k):
    zeros = (0,) * (len(shape) - 1)
    return pl.BlockSpec((None,) + tuple(shape[1:]), lambda *_: (k,) + zeros, pipeline_mode=pl.Buffered(1))


def _rows(tm, cols, col_block=0):
    return pl.BlockSpec((tm, cols), lambda i: (i, col_block))


def _acc(shape):
    zeros = (0,) * len(shape)
    return pl.BlockSpec(shape, lambda *_: zeros)


def _rms(x):
    r = lax.rsqrt(jnp.mean(x * x, axis=-1, keepdims=True) + EPS)
    return r, x * r


def _rms_bwd(dn, n, r):
    return r * (dn - n * jnp.mean(dn * n, axis=-1, keepdims=True))


def _colsum(a):
    return jnp.sum(a, axis=0, keepdims=True)


ANY = pl.BlockSpec(memory_space=pl.ANY)


class _Rider:
    def __init__(self, inputs, out_shape, sems, phases):
        self.inputs, self.out_shape, self.sems, self.phases = list(inputs), list(out_shape), list(sems), list(phases)


def _hosted(riders, body, *, name, grid=(), in_specs, out_specs, out_shape, scratch_shapes=(), compiler_params=None):
    riders = [r for r in riders if r is not None]
    n_in, n_out, n_scr = len(in_specs), len(out_shape), len(scratch_shapes)
    n_steps = 1
    for g in grid:
        n_steps *= g

    def wrapped(*refs):
        pos = n_in
        ins = refs[:n_in]
        r_ins = []
        for r in riders:
            r_ins.append(refs[pos:pos + len(r.inputs)])
            pos += len(r.inputs)
        outs = refs[pos:pos + n_out]
        pos += n_out
        r_outs = []
        for r in riders:
            r_outs.append(refs[pos:pos + len(r.out_shape)])
            pos += len(r.out_shape)
        scr = refs[pos:pos + n_scr]
        pos += n_scr
        r_sems = []
        for r in riders:
            r_sems.append(refs[pos:pos + len(r.sems)])
            pos += len(r.sems)
        step = 0
        for axis in range(len(grid)):
            step = step * grid[axis] + pl.program_id(axis)

        def at_step(which, fn):
            if n_steps == 1:
                fn()
            else:
                pl.when(step == which)(fn)

        for r, ri, ro, rs in zip(riders, r_ins, r_outs, r_sems):
            for fraction, fn in r.phases:
                if fraction == 0:
                    at_step(0, functools.partial(fn, ri, ro, rs))
        body(*ins, *outs, *scr)
        for r, ri, ro, rs in zip(riders, r_ins, r_outs, r_sems):
            for fraction, fn in r.phases:
                if fraction > 0:
                    at_step(min(int(fraction * n_steps), n_steps - 1), functools.partial(fn, ri, ro, rs))

    call = pl.pallas_call(
        wrapped, name=name, grid=grid,
        in_specs=list(in_specs) + [ANY for r in riders for _ in r.inputs],
        out_specs=list(out_specs) + [ANY for r in riders for _ in r.out_shape],
        out_shape=list(out_shape) + [s for r in riders for s in r.out_shape],
        scratch_shapes=list(scratch_shapes) + [s for r in riders for s in r.sems],
        compiler_params=compiler_params)

    def run(*args):
        res = call(*args, *[a for r in riders for a in r.inputs])
        extras, pos = [], n_out
        for r in riders:
            extras.append(list(res[pos:pos + len(r.out_shape)]))
            pos += len(r.out_shape)
        return list(res[:n_out]), extras

    return run


def _ffn_fwd(x, g, ffnw, tag, tm, riders=()):
    t = x.shape[0]

    def body(x_ref, g_ref, w1_ref, w3_ref, w2_ref, xo_ref, a_ref, b_ref):
        xv = x_ref[...]
        _, n = _rms(xv)
        h = (n * g_ref[...]).astype(BF16)
        a = _dot_nt(h, w1_ref[...])
        b = _dot_nt(h, w3_ref[...])
        s = (a * _sigmoid(a) * b).astype(BF16)
        xo_ref[...] = xv + 0.5 * _dot(s, w2_ref[...])
        a_ref[...] = a.astype(BF16)
        b_ref[...] = b.astype(BF16)

    return _hosted(
        riders, body, name=f"ffn_fwd_{tag}", grid=(t // tm,),
        in_specs=[_rows(tm, D_MODEL), _resident((1, D_MODEL)), _pick(ffnw.shape, 0), _pick(ffnw.shape, 1),
                  _pick(ffnw.shape, 2)],
        out_specs=[_rows(tm, D_MODEL), _rows(tm, D_FF), _rows(tm, D_FF)],
        out_shape=[jax.ShapeDtypeStruct((t, D_MODEL), F32), jax.ShapeDtypeStruct((t, D_FF), BF16),
                   jax.ShapeDtypeStruct((t, D_FF), BF16)],
        compiler_params=_params(("arbitrary",)),
    )(x, g, ffnw, ffnw, ffnw)


def _ffn_bwd(dxo, x, g, a, b, ffnw, tag, tm, riders=()):
    t = x.shape[0]

    def body(dxo_ref, x_ref, g_ref, a_ref, b_ref, w1_ref, w3_ref, w2_ref, dx_ref, dab_ref, s_ref, h_ref, dxh_ref, dg_ref):
        @pl.when(pl.program_id(0) == 0)
        def _():
            dg_ref[...] = jnp.zeros_like(dg_ref)

        xv = x_ref[...]
        gv = g_ref[...]
        r, n = _rms(xv)
        h_ref[...] = (n * gv).astype(BF16)
        dxo_v = dxo_ref[...]
        dxh = (0.5 * dxo_v).astype(BF16)
        dxh_ref[...] = dxh
        ds = _dot_nt(dxh, w2_ref[...])
        av = a_ref[...].astype(F32)
        bv = b_ref[...].astype(F32)
        sg = _sigmoid(av)
        silu = av * sg
        s_ref[...] = (silu * bv).astype(BF16)
        da = (ds * bv * (sg * (1.0 + av * (1.0 - sg)))).astype(BF16)
        db = (ds * silu).astype(BF16)
        dab_ref[:, :D_FF] = da
        dab_ref[:, D_FF:] = db
        dh = _dot(da, w1_ref[...]) + _dot(db, w3_ref[...])
        dg_ref[...] += _colsum(dh * n)
        dx_ref[...] = dxo_v + _rms_bwd(dh * gv, n, r)

    return _hosted(
        riders, body, name=f"ffn_bwd_{tag}", grid=(t // tm,),
        in_specs=[_rows(tm, D_MODEL), _rows(tm, D_MODEL), _resident((1, D_MODEL)), _rows(tm, D_FF), _rows(tm, D_FF),
                  _pick(ffnw.shape, 0), _pick(ffnw.shape, 1), _pick(ffnw.shape, 2)],
        out_specs=[_rows(tm, D_MODEL), _rows(tm, 2 * D_FF), _rows(tm, D_FF), _rows(tm, D_MODEL), _rows(tm, D_MODEL),
                   _acc((1, D_MODEL))],
        out_shape=[jax.ShapeDtypeStruct((t, D_MODEL), F32), jax.ShapeDtypeStruct((t, 2 * D_FF), BF16),
                   jax.ShapeDtypeStruct((t, D_FF), BF16), jax.ShapeDtypeStruct((t, D_MODEL), BF16),
                   jax.ShapeDtypeStruct((t, D_MODEL), BF16), jax.ShapeDtypeStruct((1, D_MODEL), F32)],
        compiler_params=_params(("arbitrary",)),
    )(dxo, x, g, a, b, ffnw, ffnw, ffnw)


def _wgrad(xm, dy, out_blocks, name, tk, riders=()):
    t, m = xm.shape
    n = dy.shape[1]
    mb = m // out_blocks

    def body(x_ref, dy_ref, o_ref):
        @pl.when(pl.program_id(1) == 0)
        def _():
            o_ref[...] = jnp.zeros_like(o_ref)

        o_ref[...] += _dot_tn(x_ref[...], dy_ref[...])

    return _hosted(
        riders, body, name=name, grid=(out_blocks, t // tk),
        in_specs=[pl.BlockSpec((tk, mb), lambda j, k: (k, j)), pl.BlockSpec((tk, n), lambda j, k: (k, 0))],
        out_specs=[pl.BlockSpec((None, mb, n), lambda j, k: (j, 0, 0))],
        out_shape=[jax.ShapeDtypeStruct((out_blocks, mb, n), F32)],
        compiler_params=_params(("arbitrary", "arbitrary")),
    )(xm, dy)


def _wgrad_cols(xm, dy, out_blocks, name, tk, riders=()):
    t, m = xm.shape
    n = dy.shape[1]
    nb = n // out_blocks

    def body(x_ref, dy_ref, o_ref):
        @pl.when(pl.program_id(1) == 0)
        def _():
            o_ref[...] = jnp.zeros_like(o_ref)

        o_ref[...] += _dot_tn(x_ref[...], dy_ref[...])

    return _hosted(
        riders, body, name=name, grid=(out_blocks, t // tk),
        in_specs=[pl.BlockSpec((tk, m), lambda j, k: (k, 0)), pl.BlockSpec((tk, nb), lambda j, k: (k, j))],
        out_specs=[pl.BlockSpec((None, m, nb), lambda j, k: (j, 0, 0))],
        out_shape=[jax.ShapeDtypeStruct((out_blocks, m, nb), F32)],
        compiler_params=_params(("arbitrary", "arbitrary")),
    )(xm, dy)


def _mix_fwd(x1, g, winw, tm, riders=()):
    t = x1.shape[0]

    def body(x_ref, g_ref, w_ref, main_ref, pool_ref, gate_ref):
        _, n = _rms(x_ref[...])
        h = (n * g_ref[...]).astype(BF16)
        proj = jnp.concatenate([_dot(h, w_ref[j]) for j in range(N_CHIPS)], axis=1)
        main_ref[...] = proj[:, :MAIN_COLS]
        pool_ref[...] = proj[:, MAIN_COLS:MAIN_COLS + POOL_WIDTH]
        gate_ref[...] = proj[:, MAIN_COLS + POOL_WIDTH:]

    return _hosted(
        riders, body, name="mix_fwd", grid=(t // tm,),
        in_specs=[_rows(tm, D_MODEL), _resident((1, D_MODEL)), _resident(winw.shape)],
        out_specs=[_rows(tm, MAIN_COLS), _rows(tm, POOL_WIDTH), _rows(tm, GATE_COLS)],
        out_shape=[jax.ShapeDtypeStruct((t, MAIN_COLS), F32), jax.ShapeDtypeStruct((t, POOL_WIDTH), F32),
                   jax.ShapeDtypeStruct((t, GATE_COLS), F32)],
        compiler_params=_params(("arbitrary",)),
    )(x1, g, winw)


def _mix_bwd(dqfi, dog, du, dgates, dx2, x1, g, winw, tm, riders=()):
    t = x1.shape[0]
    cols = N_CHIPS * SHARD_IN_COLS

    def body(dqfi_ref, dog_ref, du_ref, dgt_ref, dx2_ref, x_ref, g_ref, w_ref, dx_ref, dproj_ref, h_ref, dg_ref):
        @pl.when(pl.program_id(0) == 0)
        def _():
            dg_ref[...] = jnp.zeros_like(dg_ref)

        dproj = jnp.concatenate([dqfi_ref[...], dog_ref[...], du_ref[...], dgt_ref[...]], axis=1)
        dproj_ref[...] = dproj
        dh = _dot_nt(dproj[:, :SHARD_IN_COLS], w_ref[0])
        for j in range(1, N_CHIPS):
            dh += _dot_nt(dproj[:, j * SHARD_IN_COLS:(j + 1) * SHARD_IN_COLS], w_ref[j])
        gv = g_ref[...]
        r, n = _rms(x_ref[...])
        h_ref[...] = (n * gv).astype(BF16)
        dg_ref[...] += _colsum(dh * n)
        dx_ref[...] = dx2_ref[...] + _rms_bwd(dh * gv, n, r)

    return _hosted(
        riders, body, name="mix_bwd", grid=(t // tm,),
        in_specs=[_rows(tm, 3 * D_MODEL), _rows(tm, D_MODEL), _rows(tm, POOL_WIDTH), _rows(tm, GATE_COLS),
                  _rows(tm, D_MODEL), _rows(tm, D_MODEL), _resident((1, D_MODEL)), _resident(winw.shape)],
        out_specs=[_rows(tm, D_MODEL), _rows(tm, cols), _rows(tm, D_MODEL), _acc((1, D_MODEL))],
        out_shape=[jax.ShapeDtypeStruct((t, D_MODEL), F32), jax.ShapeDtypeStruct((t, cols), BF16),
                   jax.ShapeDtypeStruct((t, D_MODEL), BF16), jax.ShapeDtypeStruct((1, D_MODEL), F32)],
        compiler_params=_params(("arbitrary",)),
    )(dqfi, dog, du, dgates, dx2, x1, g, winw)


def _lower_bound(lb_raw):
    l0 = lb_raw[0:1, :]
    l1 = lb_raw[1:2, :]
    m = jnp.maximum(l0, l1)
    e0 = jnp.exp(l0 - m)
    e1 = jnp.exp(l1 - m)
    return e0 / (e0 + e1)


def _head_slices():
    return [slice(h * HEAD_DIM, (h + 1) * HEAD_DIM) for h in range(HEADS)]


def _gates(qr, fr, lb, tril_b, first_half):
    sg = _sigmoid(fr)
    f = lb + (1.0 - lb) * sg
    k = 1.0 - f
    sq = _sigmoid(qr)
    q = qr * sq
    log_f = jnp.log(f)
    gc = _tri_sum(tril_b, log_f)
    gm = _colsum(jnp.where(first_half, log_f, 0.0))
    gl = _colsum(log_f)
    e_q = jnp.exp(gc - gm)
    e_k = jnp.exp(gm - gc)
    e_in = jnp.exp(gc)
    e_out = jnp.exp(gl - gc)
    return dict(sg=sg, f=f, k=k, sq=sq, q=q, e_q=e_q, e_k=e_k, e_in=e_in, e_out=e_out, e_last=jnp.exp(gl))


def _hgrn_fwd(main, lb_raw, tt, riders=()):
    t = main.shape[0]
    n_local = tt // CHUNK

    def body(q_ref, f_ref, i_ref, lb_ref, o_ref, st_ref, s_scr):
        @pl.when(pl.program_id(0) == 0)
        def _():
            s_scr[...] = jnp.zeros_like(s_scr)

        lb = _lower_bound(lb_ref[...])
        row = lax.broadcasted_iota(jnp.int32, (CHUNK, CHUNK), 0)
        col = lax.broadcasted_iota(jnp.int32, (CHUNK, CHUNK), 1)
        tril = row >= col
        tril_b = tril.astype(BF16)
        first_half = lax.broadcasted_iota(jnp.int32, (CHUNK, D_MODEL), 0) < CHUNK // 2
        heads = _head_slices()

        def chunk(c, carry):
            rows = pl.ds(pl.multiple_of(c * CHUNK, CHUNK), CHUNK)
            z = _gates(q_ref[rows, :], f_ref[rows, :], lb, tril_b, first_half)
            qt = (z["q"] * z["e_q"]).astype(BF16)
            kt = (z["k"] * z["e_k"]).astype(BF16)
            qg = (z["q"] * z["e_in"]).astype(BF16)
            kg = (z["k"] * z["e_out"]).astype(BF16)
            vb = i_ref[rows, :].astype(BF16)
            states = [s_scr[h] for h in range(HEADS)]
            for h in range(HEADS):
                st_ref[c, h] = states[h]
            raw = [_dot_nt(qt[:, sl], kt[:, sl]) for sl in heads]
            inter = [_dot_nt(qg[:, sl], states[h].astype(BF16)) for h, sl in enumerate(heads)]
            grown = [_dot_tn(vb[:, sl], kg[:, sl]) for sl in heads]
            scores = [jnp.where(tril, r, 0.0).astype(BF16) for r in raw]
            for h, sl in enumerate(heads):
                s_scr[h] = states[h] * z["e_last"][:, sl] + grown[h]
            o_ref[rows, :] = jnp.concatenate([_dot(scores[h], vb[:, sl]) + inter[h] for h, sl in enumerate(heads)], axis=1)
            return carry

        lax.fori_loop(0, n_local, chunk, 0, unroll=2)

    return _hosted(
        riders, body, name="hgrn_fwd", grid=(t // tt,),
        in_specs=[_rows(tt, D_MODEL, 0), _rows(tt, D_MODEL, 1), _rows(tt, D_MODEL, 2), _resident((2, D_MODEL))],
        out_specs=[_rows(tt, D_MODEL),
                   pl.BlockSpec((n_local, HEADS, HEAD_DIM, HEAD_DIM), lambda i: (i, 0, 0, 0))],
        out_shape=[jax.ShapeDtypeStruct((t, D_MODEL), F32),
                   jax.ShapeDtypeStruct((t // CHUNK, HEADS, HEAD_DIM, HEAD_DIM), F32)],
        scratch_shapes=[pltpu.VMEM((HEADS, HEAD_DIM, HEAD_DIM), F32)],
        compiler_params=_params(("arbitrary",)),
    )(main, main, main, lb_raw)


def _hgrn_bwd(main, lb_raw, states, do, tt, riders=()):
    t = main.shape[0]
    n_tiles = t // tt
    n_local = tt // CHUNK

    def rev(col_block):
        return pl.BlockSpec((tt, D_MODEL), lambda i: (n_tiles - 1 - i, col_block))

    def body(q_ref, f_ref, i_ref, lb_ref, st_ref, do_ref, dqfi_ref, dlb_ref, ds_scr, acc_scr):
        @pl.when(pl.program_id(0) == 0)
        def _():
            ds_scr[...] = jnp.zeros_like(ds_scr)
            acc_scr[...] = jnp.zeros_like(acc_scr)

        lb = _lower_bound(lb_ref[...])
        row = lax.broadcasted_iota(jnp.int32, (CHUNK, CHUNK), 0)
        col = lax.broadcasted_iota(jnp.int32, (CHUNK, CHUNK), 1)
        tril = row >= col
        tril_b = tril.astype(BF16)
        triu_b = (row <= col).astype(BF16)
        first_half = lax.broadcasted_iota(jnp.int32, (CHUNK, D_MODEL), 0) < CHUNK // 2
        heads = _head_slices()
        cat = functools.partial(jnp.concatenate, axis=1)

        def chunk(cc, carry):
            c = n_local - 1 - cc
            rows = pl.ds(pl.multiple_of(c * CHUNK, CHUNK), CHUNK)
            qr = q_ref[rows, :]
            z = _gates(qr, f_ref[rows, :], lb, tril_b, first_half)
            qt = (z["q"] * z["e_q"]).astype(BF16)
            kt = (z["k"] * z["e_k"]).astype(BF16)
            qg_f = z["q"] * z["e_in"]
            qg = qg_f.astype(BF16)
            kg_f = z["k"] * z["e_out"]
            kg = kg_f.astype(BF16)
            vb = i_ref[rows, :].astype(BF16)
            dob = do_ref[rows, :].astype(BF16)
            st = [st_ref[c, h] for h in range(HEADS)]
            dst = [ds_scr[h] for h in range(HEADS)]
            dst_b = [d.astype(BF16) for d in dst]
            raw = [_dot_nt(qt[:, sl], kt[:, sl]) for sl in heads]
            draw = [_dot_nt(dob[:, sl], vb[:, sl]) for sl in heads]
            dqg = [_dot(dob[:, sl], st[h].astype(BF16)) for h, sl in enumerate(heads)]
            dkg = [_dot(vb[:, sl], dst_b[h]) for h, sl in enumerate(heads)]
            dv_inter = [_dot_nt(kg[:, sl], dst_b[h]) for h, sl in enumerate(heads)]
            grown = [_dot_tn(dob[:, sl], qg[:, sl]) for sl in heads]
            scores = [jnp.where(tril, r, 0.0).astype(BF16) for r in raw]
            dscores = [jnp.where(tril, r, 0.0).astype(BF16) for r in draw]
            dqt = [_dot(dscores[h], kt[:, sl]) for h, sl in enumerate(heads)]
            dkt = [_dot_tn(dscores[h], qt[:, sl]) for h, sl in enumerate(heads)]
            dv = [_dot_tn(scores[h], dob[:, sl]) + dv_inter[h] for h, sl in enumerate(heads)]
            carry_in = cat([z["e_last"][:, sl] * _colsum(dst[h] * st[h]) for h, sl in enumerate(heads)])
            for h, sl in enumerate(heads):
                ds_scr[h] = dst[h] * z["e_last"][:, sl] + grown[h]
            dqt, dkt, dqg, dkg = cat(dqt), cat(dkt), cat(dqg), cat(dkg)
            carry_in += _colsum(dkg * kg_f)
            dq = dqt * z["e_q"] + dqg * z["e_in"]
            dk = dkt * z["e_k"] + dkg * z["e_out"]
            dgate = (qt.astype(F32) * dqt - kt.astype(F32) * dkt) + (qg_f * dqg - kg_f * dkg)
            dlogf = _tri_sum(triu_b, dgate) + carry_in
            df = dlogf / z["f"] - dk
            sg = z["sg"]
            sq = z["sq"]
            acc_scr[...] += _colsum(df * (1.0 - sg))
            dqfi_ref[rows, 0:D_MODEL] = (dq * (sq * (1.0 + qr * (1.0 - sq)))).astype(BF16)
            dqfi_ref[rows, D_MODEL:2 * D_MODEL] = (df * (1.0 - lb) * sg * (1.0 - sg)).astype(BF16)
            dqfi_ref[rows, 2 * D_MODEL:3 * D_MODEL] = cat(dv).astype(BF16)
            return carry

        lax.fori_loop(0, n_local, chunk, 0, unroll=2)
        d0 = acc_scr[...] * lb * (1.0 - lb)
        dlb_ref[0:1, :] = d0
        dlb_ref[1:2, :] = -d0

    return _hosted(
        riders, body, name="hgrn_bwd", grid=(n_tiles,),
        in_specs=[rev(0), rev(1), rev(2), _resident((2, D_MODEL)),
                  pl.BlockSpec((n_local, HEADS, HEAD_DIM, HEAD_DIM), lambda i: (n_tiles - 1 - i, 0, 0, 0)),
                  rev(0)],
        out_specs=[pl.BlockSpec((tt, 3 * D_MODEL), lambda i: (n_tiles - 1 - i, 0)), _acc((2, D_MODEL))],
        out_shape=[jax.ShapeDtypeStruct((t, 3 * D_MODEL), BF16), jax.ShapeDtypeStruct((2, D_MODEL), F32)],
        scratch_shapes=[pltpu.VMEM((HEADS, HEAD_DIM, HEAD_DIM), F32), pltpu.VMEM((1, D_MODEL), F32)],
        compiler_params=_params(("arbitrary",)),
    )(main, main, main, lb_raw, states, do)


def _head_norm(o):
    rs, ns = [], []
    for h in range(HEADS):
        oh = o[:, h * HEAD_DIM:(h + 1) * HEAD_DIM]
        r, n = _rms(oh)
        rs.append(jnp.broadcast_to(r, oh.shape))
        ns.append(n)
    return jnp.concatenate(rs, axis=1), jnp.concatenate(ns, axis=1)


def _head_norm_bwd(dn, n, r):
    outs = []
    for h in range(HEADS):
        sl = slice(h * HEAD_DIM, (h + 1) * HEAD_DIM)
        outs.append(_rms_bwd(dn[:, sl], n[:, sl], r[:, sl]))
    return jnp.concatenate(outs, axis=1)


def _window_counts(first_row, tm):
    pos = (first_row + 1 + lax.broadcasted_iota(jnp.int32, (tm, 1), 0)).astype(F32)
    return [jnp.minimum(pos, float(w)) for w in POOL_WINDOWS]


def _post_fwd(o, main, pool_r, gates, x1, onorm, pool_w, pool_scale, sqw, wbw, tm, riders=()):
    t = o.shape[0]
    ext_rows = tm + POOL_HALO

    def body(o_ref, og_ref, u_ref, gt_ref, x1_ref, on_ref, pw_ref, ps_ref, wa_ref, wout_ref, wb_ref,
             x2_ref, ya_ref, yb_ref, pooled_ref, ext):
        i = pl.program_id(0)

        @pl.when(i == 0)
        def _():
            ext[0:POOL_HALO, :] = jnp.zeros((POOL_HALO, POOL_WIDTH), F32)

        _, n = _head_norm(o_ref[...])
        og = og_ref[...]
        oa = (n * on_ref[...] * (og * _sigmoid(og))).astype(BF16)
        ya = _dot(oa, wa_ref[...])

        u = u_ref[...]
        ext[POOL_HALO:ext_rows, :] = u
        e = ext[...]
        counts = _window_counts(i * tm, tm)
        pooled = []
        for gidx, w in enumerate(POOL_WINDOWS):
            s = e[:, gidx * HEAD_DIM:(gidx + 1) * HEAD_DIM]
            shift = 1
            while shift < w:
                s = s + pltpu.roll(s, shift, axis=0)
                shift *= 2
            pooled.append(s[POOL_HALO:, :] / counts[gidx] - u[:, gidx * HEAD_DIM:(gidx + 1) * HEAD_DIM])
        ext[0:POOL_HALO, :] = ext[tm:ext_rows, :]
        pooled_b = [pg.astype(BF16) for pg in pooled]
        pooled_ref[...] = jnp.concatenate(pooled_b, axis=1)
        mixed = jnp.concatenate([_dot(pooled_b[gidx], pw_ref[gidx].astype(BF16)) for gidx in range(len(POOL_WINDOWS))],
                                axis=1) * ps_ref[...]
        mixed_b = mixed.astype(BF16)
        yb = jnp.concatenate([_dot(mixed_b, wb_ref[j]) for j in range(N_CHIPS)], axis=1)

        gt = gt_ref[...]
        y = _sigmoid(gt[:, :D_MODEL]) * ya + _sigmoid(gt[:, D_MODEL:]) * yb
        x2_ref[...] = x1_ref[...] + _dot(y.astype(BF16), wout_ref[...])
        ya_ref[...] = ya.astype(BF16)
        yb_ref[...] = yb.astype(BF16)

    return _hosted(
        riders, body, name="post_fwd", grid=(t // tm,),
        in_specs=[_rows(tm, D_MODEL), _rows(tm, D_MODEL, 3), _rows(tm, POOL_WIDTH), _rows(tm, GATE_COLS), _rows(tm, D_MODEL),
                  _resident((1, D_MODEL)), _resident(pool_w.shape), _resident((1, POOL_WIDTH)),
                  _pick(sqw.shape, 0), _pick(sqw.shape, 1), _resident(wbw.shape)],
        out_specs=[_rows(tm, D_MODEL), _rows(tm, D_MODEL), _rows(tm, D_MODEL), _rows(tm, POOL_WIDTH)],
        out_shape=[jax.ShapeDtypeStruct((t, D_MODEL), F32), jax.ShapeDtypeStruct((t, D_MODEL), BF16),
                   jax.ShapeDtypeStruct((t, D_MODEL), BF16), jax.ShapeDtypeStruct((t, POOL_WIDTH), BF16)],
        scratch_shapes=[pltpu.VMEM((ext_rows, POOL_WIDTH), F32)],
        compiler_params=_params(("arbitrary",)),
    )(o, main, pool_r, gates, x1, onorm, pool_w, pool_scale, sqw, sqw, wbw)


def _post_bwd(dx2, o, main, gates, ya, yb, pooled, onorm, pool_w, pool_scale, sqw, wbw, tm, riders=()):
    t = o.shape[0]
    n_tiles = t // tm
    ext_rows = tm + POOL_HALO
    n_groups = len(POOL_WINDOWS)

    def rev(cols, col_block=0):
        return pl.BlockSpec((tm, cols), lambda i: (n_tiles - 1 - i, col_block))

    def body(dx2_ref, o_ref, og_ref, gt_ref, ya_ref, yb_ref, pooled_ref, on_ref, pw_ref, ps_ref, wa_ref, wout_ref, wb_ref,
             do_ref, dog_ref, du_ref, dgt_ref, dwa_ref, dwout_ref, dwb_ref, dpw_ref, dps_ref, don_ref, ext):
        i = pl.program_id(0)

        @pl.when(i == 0)
        def _():
            ext[tm:ext_rows, :] = jnp.zeros((POOL_HALO, POOL_WIDTH), F32)
            for ref in (dwa_ref, dwout_ref, dwb_ref, dpw_ref, dps_ref, don_ref):
                ref[...] = jnp.zeros_like(ref)

        dx2b = dx2_ref[...].astype(BF16)
        dy = _dot_nt(dx2b, wout_ref[...])
        gt = gt_ref[...]
        sga = _sigmoid(gt[:, :D_MODEL])
        sgb = _sigmoid(gt[:, D_MODEL:])
        ya = ya_ref[...].astype(F32)
        yb = yb_ref[...].astype(F32)
        y = (sga * ya + sgb * yb).astype(BF16)
        dwout_ref[...] += _dot_tn(y, dx2b)
        dya = (dy * sga).astype(BF16)
        dyb = (dy * sgb).astype(BF16)
        dgt_ref[:, :D_MODEL] = (dy * ya * sga * (1.0 - sga)).astype(BF16)
        dgt_ref[:, D_MODEL:] = (dy * yb * sgb * (1.0 - sgb)).astype(BF16)

        r, n = _head_norm(o_ref[...])
        onv = on_ref[...]
        og = og_ref[...]
        sog = _sigmoid(og)
        silu_og = og * sog
        normed = n * onv
        dwa_ref[...] += _dot_tn((normed * silu_og).astype(BF16), dya)
        doa = _dot_nt(dya, wa_ref[...])
        dog_ref[...] = (doa * normed * (sog * (1.0 + og * (1.0 - sog)))).astype(BF16)
        dnormed = doa * silu_og
        don_ref[...] += _colsum(dnormed * n)
        do_ref[...] = _head_norm_bwd(dnormed * onv, n, r)

        psv = ps_ref[...]
        dmixed = _dot_nt(dyb[:, :256], wb_ref[0])
        for j in range(1, N_CHIPS):
            dmixed += _dot_nt(dyb[:, j * 256:(j + 1) * 256], wb_ref[j])
        pooled_b = pooled_ref[...]
        pm = jnp.concatenate(
            [_dot(pooled_b[:, gidx * HEAD_DIM:(gidx + 1) * HEAD_DIM], pw_ref[gidx].astype(BF16)) for gidx in range(n_groups)],
            axis=1)
        mixed_b = (pm * psv).astype(BF16)
        for j in range(N_CHIPS):
            dwb_ref[j] += _dot_tn(mixed_b, dyb[:, j * 256:(j + 1) * 256])
        dps_ref[...] += _colsum(dmixed * pm)
        dpm = (dmixed * psv).astype(BF16)
        counts = _window_counts((n_tiles - 1 - i) * tm, tm)
        dpooled = []
        for gidx in range(n_groups):
            sl = slice(gidx * HEAD_DIM, (gidx + 1) * HEAD_DIM)
            dpw_ref[gidx] += _dot_tn(pooled_b[:, sl], dpm[:, sl])
            dpooled.append(_dot_nt(dpm[:, sl], pw_ref[gidx].astype(BF16)))
        ext[0:tm, :] = jnp.concatenate([dpooled[gidx] / counts[gidx] for gidx in range(n_groups)], axis=1)
        e = ext[...]
        du = []
        for gidx, w in enumerate(POOL_WINDOWS):
            s = e[:, gidx * HEAD_DIM:(gidx + 1) * HEAD_DIM]
            shift = 1
            while shift < w:
                s = s + pltpu.roll(s, ext_rows - shift, axis=0)
                shift *= 2
            du.append(s[:tm, :] - dpooled[gidx])
        ext[tm:ext_rows, :] = ext[0:POOL_HALO, :]
        du_ref[...] = jnp.concatenate(du, axis=1).astype(BF16)

    wa_shape = (D_MODEL, D_MODEL)
    return _hosted(
        riders, body, name="post_bwd", grid=(n_tiles,),
        in_specs=[rev(D_MODEL), rev(D_MODEL), rev(D_MODEL, 3), rev(GATE_COLS), rev(D_MODEL), rev(D_MODEL), rev(POOL_WIDTH),
                  _resident((1, D_MODEL)), _resident(pool_w.shape), _resident((1, POOL_WIDTH)),
                  _pick(sqw.shape, 0), _pick(sqw.shape, 1), _resident(wbw.shape)],
        out_specs=[rev(D_MODEL), rev(D_MODEL), rev(POOL_WIDTH), rev(GATE_COLS),
                   _acc(wa_shape), _acc(wa_shape), _acc(wbw.shape), _acc(pool_w.shape), _acc((1, POOL_WIDTH)),
                   _acc((1, D_MODEL))],
        out_shape=[jax.ShapeDtypeStruct((t, D_MODEL), F32), jax.ShapeDtypeStruct((t, D_MODEL), BF16),
                   jax.ShapeDtypeStruct((t, POOL_WIDTH), BF16), jax.ShapeDtypeStruct((t, GATE_COLS), BF16),
                   jax.ShapeDtypeStruct(wa_shape, F32), jax.ShapeDtypeStruct(wa_shape, F32),
                   jax.ShapeDtypeStruct(wbw.shape, F32), jax.ShapeDtypeStruct(pool_w.shape, F32),
                   jax.ShapeDtypeStruct((1, POOL_WIDTH), F32), jax.ShapeDtypeStruct((1, D_MODEL), F32)],
        scratch_shapes=[pltpu.VMEM((ext_rows, POOL_WIDTH), F32)],
        compiler_params=_params(("arbitrary",)),
    )(dx2, o, main, gates, ya, yb, pooled, onorm, pool_w, pool_scale, sqw, sqw, wbw)


def _tail(x3, p, target, g_ple, g_post, g_final, sqw, wpw, tm, riders=()):
    t = x3.shape[0]
    pd = p.shape[1]

    def body(x_ref, p_ref, tg_ref, g4_ref, g5_ref, g6_ref, wg_ref, wp_ref,
             dx_ref, loss_ref, dwg_ref, dwp_ref, dg4_ref, dg5_ref, dg6_ref):
        @pl.when(pl.program_id(0) == 0)
        def _():
            for ref in (loss_ref, dwg_ref, dwp_ref, dg4_ref, dg5_ref, dg6_ref):
                ref[...] = jnp.zeros_like(ref)

        x3v = x_ref[...]
        g4, g5, g6 = g4_ref[...], g5_ref[...], g6_ref[...]
        r4, n4 = _rms(x3v)
        h4 = (n4 * g4).astype(BF16)
        gate = _sigmoid(_dot(h4, wg_ref[...]))
        pb = p_ref[...].astype(BF16)
        r5, n5 = _rms(jnp.concatenate([_dot(pb, wp_ref[j]) for j in range(N_CHIPS)], axis=1))
        emb = n5 * g5
        r6, n6 = _rms(x3v + gate * emb)
        diff = n6 * g6 - tg_ref[...]
        loss_ref[...] += 0.5 * jnp.sum(jnp.mean(diff * diff, axis=-1, keepdims=True), axis=0, keepdims=True)
        dout = diff * (1.0 / D_MODEL)
        dg6_ref[...] += _colsum(dout * n6)
        dx4 = _rms_bwd(dout * g6, n6, r6)
        demb = dx4 * gate
        dg5_ref[...] += _colsum(demb * n5)
        dpre = _rms_bwd(demb * g5, n5, r5).astype(BF16)
        for j in range(N_CHIPS):
            dwp_ref[j] += _dot_tn(pb, dpre[:, j * pd:(j + 1) * pd])
        dz = (dx4 * emb * gate * (1.0 - gate)).astype(BF16)
        dwg_ref[...] += _dot_tn(h4, dz)
        dh4 = _dot_nt(dz, wg_ref[...])
        dg4_ref[...] += _colsum(dh4 * n4)
        dx_ref[...] = dx4 + _rms_bwd(dh4 * g4, n4, r4)

    sq_shape = (D_MODEL, D_MODEL)
    vec = (1, D_MODEL)
    return _hosted(
        riders, body, name="tail", grid=(t // tm,),
        in_specs=[_rows(tm, D_MODEL), _rows(tm, pd), _rows(tm, D_MODEL), _resident(vec), _resident(vec), _resident(vec),
                  _pick(sqw.shape, 2), _resident(wpw.shape)],
        out_specs=[_rows(tm, D_MODEL), _acc((1, 1)), _acc(sq_shape), _acc(wpw.shape), _acc(vec), _acc(vec), _acc(vec)],
        out_shape=[jax.ShapeDtypeStruct((t, D_MODEL), F32), jax.ShapeDtypeStruct((1, 1), F32),
                   jax.ShapeDtypeStruct(sq_shape, F32), jax.ShapeDtypeStruct(wpw.shape, F32),
                   jax.ShapeDtypeStruct(vec, F32), jax.ShapeDtypeStruct(vec, F32), jax.ShapeDtypeStruct(vec, F32)],
        compiler_params=_params(("arbitrary",)),
    )(x3, p, target, g_ple, g_post, g_final, sqw, wpw)


def _position():
    return lax.axis_index("x"), lax.axis_index("y"), lax.axis_index("c")


def _other_chips(x, y):
    return [(1 - x, y), (x, 1 - y), (1 - x, 1 - y)]


def _remote(src, dst, send_sems, recv_sems, k, device):
    return pltpu.make_async_remote_copy(src_ref=src, dst_ref=dst, send_sem=send_sems.at[k], recv_sem=recv_sems.at[k],
                                        device_id=device, device_id_type=MESH)


def _gather_rider(shards, forward_at):
    n = len(shards)

    def copies(ins, outs, sems):
        send_sems, recv_sems, local_sems = sems
        x, y, c = _position()
        mine = 2 * x + y
        local = [pltpu.make_async_copy(ins[a], outs[a].at[:, mine], local_sems.at[a]) for a in range(n)]
        first, passed, arriving = [], [], []
        for k, (cx, cy) in enumerate(_other_chips(x, y)):
            theirs = 2 * cx + cy
            for a in range(n):
                first.append(_remote(ins[a].at[:, c], outs[a].at[:, mine, c], send_sems, recv_sems, k * n + a, (cx, cy, c)))
                block = outs[a].at[:, theirs, c]
                passed.append(_remote(block, block, send_sems, recv_sems, (3 + k) * n + a, (x, y, 1 - c)))
                other = outs[a].at[:, theirs, 1 - c]
                arriving.append(_remote(other, other, send_sems, recv_sems, (3 + k) * n + a, (x, y, 1 - c)))
        return local, first, passed, arriving

    def begin(ins, outs, sems):
        local, first, _, _ = copies(ins, outs, sems)
        for cp in local + first:
            cp.start()

    def forward(ins, outs, sems):
        _, first, passed, _ = copies(ins, outs, sems)
        for got, cp in zip(first, passed):
            got.wait_recv()
            cp.start()

    def finish(ins, outs, sems):
        local, first, passed, arriving = copies(ins, outs, sems)
        for cp in arriving:
            cp.wait_recv()
        for cp in first + passed:
            cp.wait_send()
        for cp in local:
            cp.wait()

    return _Rider(shards, [jax.ShapeDtypeStruct((s.shape[0], N_CHIPS) + s.shape[1:], s.dtype) for s in shards],
                  [pltpu.SemaphoreType.DMA((6 * n,)), pltpu.SemaphoreType.DMA((6 * n,)), pltpu.SemaphoreType.DMA((n,))],
                  [(0, begin), (forward_at, forward), (1, finish)])


def _exchange_rider(arrays, out_shape, n_copies, transfers, n_local=0):
    def copies(ins, outs, sems):
        send_sems, recv_sems, local_sems = sems
        remote, local = transfers(ins, outs)
        return ([_remote(src, dst, send_sems, recv_sems, i, dev) for i, (src, dst, dev) in enumerate(remote)],
                [pltpu.make_async_copy(src, dst, local_sems.at[i]) for i, (src, dst) in enumerate(local)])

    def begin(ins, outs, sems):
        remote, local = copies(ins, outs, sems)
        for cp in remote + local:
            cp.start()

    def finish(ins, outs, sems):
        remote, local = copies(ins, outs, sems)
        for cp in remote:
            cp.wait_recv()
        for cp in remote:
            cp.wait_send()
        for cp in local:
            cp.wait()

    return _Rider(arrays, out_shape,
                  [pltpu.SemaphoreType.DMA((n_copies,)), pltpu.SemaphoreType.DMA((n_copies,)),
                   pltpu.SemaphoreType.DMA((max(n_local, 1),))],
                  [(0, begin), (1, finish)])


def _pair_rider(partials):
    def transfers(ins, outs):
        x, y, c = _position()
        return [(ins[a].at[:, :, 1 - c], outs[a], (x, y, 1 - c)) for a in range(len(partials))], []

    shapes = [jax.ShapeDtypeStruct(g.shape[:2] + g.shape[3:], g.dtype) for g in partials]
    return _exchange_rider(partials, shapes, len(partials), transfers)


def _chips_rider(sums):
    n = len(sums)

    def transfers(ins, outs):
        x, y, c = _position()
        return [(ins[a].at[:, 2 * cx + cy], outs[a].at[:, k], (cx, cy, c))
                for k, (cx, cy) in enumerate(_other_chips(x, y)) for a in range(n)], []

    shapes = [jax.ShapeDtypeStruct((q.shape[0], 3) + q.shape[2:], q.dtype) for q in sums]
    return _exchange_rider(sums, shapes, 3 * n, transfers)


def _share_rider(halves):
    def transfers(ins, outs):
        x, y, c = _position()
        return [(ins[a], outs[a], (x, y, 1 - c)) for a in range(len(halves))], []

    return _exchange_rider(halves, [jax.ShapeDtypeStruct(h.shape, h.dtype) for h in halves], len(halves), transfers)


def _small_rider(pack):
    flips = [(fx, fy, fc) for fx in (0, 1) for fy in (0, 1) for fc in (0, 1)][1:]

    def transfers(ins, outs):
        x, y, c = _position()
        slot = outs[0].at[4 * x + 2 * y + c]
        flip = lambda v, f: v + f - 2 * v * f
        return [(ins[0], slot, (flip(x, fx), flip(y, fy), flip(c, fc))) for fx, fy, fc in flips], [(ins[0], slot)]

    return _exchange_rider([pack], [jax.ShapeDtypeStruct((8,) + pack.shape, pack.dtype)], len(flips), transfers, n_local=1)


def _alone(rider, name):
    return _hosted([rider], lambda: None, name=name, in_specs=[], out_specs=[], out_shape=[])()[1][0]


def _add_pair(mine, theirs, c, tag):
    l, _, hr, cols = theirs.shape

    def body(c_ref, mine_ref, theirs_ref, out_ref):
        out_ref[...] = (mine_ref[...] + theirs_ref[...]).astype(BF16)

    block = (None, None, hr, cols)
    return pl.pallas_call(
        body, name=f"add_pair_{tag}",
        grid_spec=pltpu.PrefetchScalarGridSpec(
            num_scalar_prefetch=1, grid=(l, N_CHIPS),
            in_specs=[pl.BlockSpec((None, None, None, hr, cols), lambda i, j, s: (i, j, s[0], 0, 0)),
                      pl.BlockSpec(block, lambda i, j, s: (i, j, 0, 0))],
            out_specs=pl.BlockSpec(block, lambda i, j, s: (i, j, 0, 0))),
        out_shape=jax.ShapeDtypeStruct(theirs.shape, BF16),
        compiler_params=_params(("parallel", "parallel")),
    )(c.reshape(1), mine, theirs)


def _add_chips(part, received, mine, tag):
    l, _, hr, cols = received.shape

    def body(j_ref, part_ref, recv_ref, out_ref):
        acc = part_ref[...].astype(F32)
        for k in range(3):
            acc += recv_ref[k].astype(F32)
        out_ref[...] = acc

    return pl.pallas_call(
        body, name=f"add_chips_{tag}",
        grid_spec=pltpu.PrefetchScalarGridSpec(
            num_scalar_prefetch=1, grid=(l,),
            in_specs=[pl.BlockSpec((None, None, hr, cols), lambda i, s: (i, s[0], 0, 0)),
                      pl.BlockSpec((None, 3, hr, cols), lambda i, s: (i, 0, 0, 0))],
            out_specs=pl.BlockSpec((None, hr, cols), lambda i, s: (i, 0, 0))),
        out_shape=jax.ShapeDtypeStruct((l, hr, cols), F32),
        compiler_params=_params(("parallel",)),
    )(mine.reshape(1), part, received)


def _adam_update(w, g, m, v):
    m2 = ADAM_B1 * m + (1.0 - ADAM_B1) * g
    v2 = ADAM_B2 * v + (1.0 - ADAM_B2) * jnp.square(g)
    m_hat = m2 / (1.0 - ADAM_B1 ** ADAM_STEP)
    v_hat = v2 / (1.0 - ADAM_B2 ** ADAM_STEP)
    return -ADAM_LR * (m_hat / (jnp.sqrt(v_hat) + ADAM_EPS) + ADAM_WD * w), m2, v2


def _row_tile(rows, cols, limit):
    for cand in (rows, 512, 352, 256, 176, 128, 64, 32, 16, 8):
        if rows % cand == 0 and cand * cols * 4 <= limit:
            return cand
    return rows


def _adamw_halves(w, own, other, m, v, c, tag):
    _, hr, cols = w.shape
    tr = _row_tile(hr, cols, 1024 * 1024)

    def body(c_ref, w_ref, own_ref, other_ref, m_ref, v_ref, g_ref, d_ref, nm_ref, nv_ref):
        gv = jnp.where(pl.program_id(0) == c_ref[0], own_ref[...], other_ref[...])
        g_ref[...] = gv
        d_ref[...], nm_ref[...], nv_ref[...] = _adam_update(w_ref[...], gv, m_ref[...], v_ref[...])

    full = pl.BlockSpec((None, tr, cols), lambda h, i, s: (h, i, 0))
    half = pl.BlockSpec((tr, cols), lambda h, i, s: (i, 0))
    shape = jax.ShapeDtypeStruct((2, hr, cols), F32)
    return pl.pallas_call(
        body, name=f"adamw_{tag}",
        grid_spec=pltpu.PrefetchScalarGridSpec(num_scalar_prefetch=1, grid=(2, hr // tr),
                                               in_specs=[full, half, half, full, full], out_specs=[full] * 4),
        out_shape=[shape] * 4,
        compiler_params=_params(("parallel", "parallel")),
    )(c.reshape(1), w, own, other, m, v)


def _adamw_small(w, gathered, m, v):
    def body(w_ref, g_ref, m_ref, v_ref, sum_ref, d_ref, nm_ref, nv_ref):
        gv = g_ref[0]
        for i in range(1, g_ref.shape[0]):
            gv += g_ref[i]
        sum_ref[...] = gv
        d_ref[...], nm_ref[...], nv_ref[...] = _adam_update(w_ref[...], gv, m_ref[...], v_ref[...])

    shape = jax.ShapeDtypeStruct(w.shape, F32)
    return pl.pallas_call(body, name="adamw_small", out_shape=[shape] * 4, compiler_params=_params())(w, gathered, m, v)


VECTOR_PARAMS = ("ffn1_norm", "mix_norm", "hgrn_lb", "hgrn_onorm", "ffn2_norm", "ple_norm", "ple_post_norm", "final_norm",
                 "pool_scale")
ALL_PARAMS = ("ffn1_norm", "ffn1_w1", "ffn1_w3", "ffn1_w2", "mix_norm", "w_in", "hgrn_lb", "hgrn_onorm", "w_branch_a",
              "pool_w", "pool_scale", "w_branch_b", "w_out", "ffn2_norm", "ffn2_w1", "ffn2_w3", "ffn2_w2", "ple_norm",
              "ple_w_gate", "ple_w_proj", "ple_post_norm", "final_norm")
TILE_ROWS = 8


def _pack_small(values):
    parts = []
    for name in VECTOR_PARAMS:
        a = values[name].reshape(-1, values[name].shape[-1])
        parts.append(jnp.pad(a, ((0, TILE_ROWS - a.shape[0]), (0, D_MODEL - a.shape[1]))))
    parts.append(values["pool_w"].reshape(-1, D_MODEL))
    return jnp.concatenate(parts, axis=0)


def _unpack_small(pack, shapes):
    out = {}
    for i, name in enumerate(VECTOR_PARAMS):
        shape = shapes[name]
        rows = 1 if len(shape) == 1 else shape[0]
        out[name] = pack[i * TILE_ROWS:i * TILE_ROWS + rows, :shape[-1]].reshape(shape)
    out["pool_w"] = pack[len(VECTOR_PARAMS) * TILE_ROWS:].reshape(shapes["pool_w"])
    return out


def _halved(a, lead):
    return a.reshape(lead, 2, -1, a.shape[-1])


def _shard_halves(a, lead):
    return a.reshape(lead, N_CHIPS, 2, -1, a.shape[-1])


def _reduce_finish(names, own, other, w, m, v, core, out, transposed=False):
    for name, g_own, g_other in zip(names, own, other):
        shape = w[name].shape
        view = (lambda a: _halved(a[0].T, 1)[0]) if transposed else (lambda a: _halved(a, 1)[0])
        back = (lambda a: a.reshape(shape[2], shape[1]).T.reshape(shape)) if transposed else (lambda a: a.reshape(shape))
        g, d, nm, nv = _adamw_halves(view(w[name]), g_own, g_other, view(m[name]), view(v[name]), core, name)
        out["grad"][name], out["delta"][name], out["new_m"][name], out["new_v"][name] = (back(a) for a in (g, d, nm, nv))


def kernel(x, p, ffn1_norm, ffn1_w1, ffn1_w3, ffn1_w2, mix_norm, w_in, hgrn_lb, hgrn_onorm, w_branch_a, pool_w, pool_scale, w_branch_b, w_out, ffn2_norm, ffn2_w1, ffn2_w3, ffn2_w2, ple_norm, ple_w_gate, ple_w_proj, ple_post_norm, final_norm, loss_target, m_ffn1_norm, m_ffn1_w1, m_ffn1_w3, m_ffn1_w2, m_mix_norm, m_w_in, m_hgrn_lb, m_hgrn_onorm, m_w_branch_a, m_pool_w, m_pool_scale, m_w_branch_b, m_w_out, m_ffn2_norm, m_ffn2_w1, m_ffn2_w3, m_ffn2_w2, m_ple_norm, m_ple_w_gate, m_ple_w_proj, m_ple_post_norm, m_final_norm, v_ffn1_norm, v_ffn1_w1, v_ffn1_w3, v_ffn1_w2, v_mix_norm, v_w_in, v_hgrn_lb, v_hgrn_onorm, v_w_branch_a, v_pool_w, v_pool_scale, v_w_branch_b, v_w_out, v_ffn2_norm, v_ffn2_w1, v_ffn2_w3, v_ffn2_w2, v_ple_norm, v_ple_w_gate, v_ple_w_proj, v_ple_post_norm, v_final_norm):
    args = dict(locals())
    w = {name: args[name] for name in ALL_PARAMS}
    m = {name: args["m_" + name] for name in ALL_PARAMS}
    v = {name: args["v_" + name] for name in ALL_PARAMS}
    cx, cy, cc = _position()
    chip = (2 * cx + cy).astype(jnp.int32)
    core = cc.astype(jnp.int32)
    xs, ps, target = x[0], p[0, 0], loss_target[0]
    t = xs.shape[0]
    tm = min(256, t)
    tt = min(512, t)
    tk = min(512, t)
    small = {name: w[name] for name in VECTOR_PARAMS}
    small["final_norm"] = w["final_norm"].reshape(1, D_MODEL)
    pool_w0 = w["pool_w"][0]

    ffn_shard = lambda i: _halved(jnp.stack([w[f"ffn{i}_w1"][0].T, w[f"ffn{i}_w3"][0].T, w[f"ffn{i}_w2"][0]]).astype(BF16), 3)
    sq_shard = _halved(jnp.stack([w["w_branch_a"][0], w["w_out"][0], w["ple_w_gate"][0]]).astype(BF16), 3)
    win_shard, wb_shard, wp_shard = (_halved(w[n].astype(BF16), 1) for n in ("w_in", "w_branch_b", "ple_w_proj"))

    (ffn1w,) = _alone(_gather_rider([ffn_shard(1)], 0.5), "gather_ffn1")
    ffn1w = ffn1w.reshape(3, D_FF, D_MODEL)
    (x1, a1, b1), ((winw,),) = _ffn_fwd(xs, small["ffn1_norm"], ffn1w, 1, tm, [_gather_rider([win_shard], 0.6)])
    winw = winw.reshape(N_CHIPS, D_MODEL, SHARD_IN_COLS)
    (main, pool_r, gates), ((sqw, wbw, wpw),) = _mix_fwd(x1, small["mix_norm"], winw, tm,
                                                          [_gather_rider([sq_shard, wb_shard, wp_shard], 0.5)])
    sqw = sqw.reshape(3, D_MODEL, D_MODEL)
    wbw = wbw.reshape(N_CHIPS, POOL_WIDTH, -1)
    wpw = wpw.reshape(N_CHIPS, ps.shape[1], -1)
    (o, states), ((ffn2w,),) = _hgrn_fwd(main, small["hgrn_lb"], tt, [_gather_rider([ffn_shard(2)], 0.7)])
    ffn2w = ffn2w.reshape(3, D_FF, D_MODEL)
    (x2, ya, yb, pooled), _ = _post_fwd(o, main, pool_r, gates, x1, small["hgrn_onorm"], pool_w0, small["pool_scale"], sqw,
                                       wbw, tm)
    (x3, a2, b2), _ = _ffn_fwd(x2, small["ffn2_norm"], ffn2w, 2, tm)
    (dx3, loss, d_wg, d_wp, d_ple, d_post, d_final), _ = _tail(
        x3, ps, target, small["ple_norm"], small["ple_post_norm"], small["final_norm"], sqw, wpw, tm)
    loss = lax.psum(loss[0, 0], ("x", "y", "c"))

    (dx2, dab2, s2, h3, dxh2, d_ffn2_norm), _ = _ffn_bwd(dx3, x2, small["ffn2_norm"], a2, b2, ffn2w, 2, tm)
    (d_w13_2,), _ = _wgrad(dab2, h3, WGRAD_IN_BLOCKS, "wgrad_ffn2_in", tk)
    (d_w2_2,), _ = _wgrad(s2, dxh2, WGRAD_OUT_BLOCKS, "wgrad_ffn2_out", tk)

    wave1 = [_shard_halves(d_w13_2, 2), _shard_halves(d_w2_2, 1), _shard_halves(d_wg, 1), _shard_halves(d_wp, 1)]
    tags1 = ("w13_2", "w2_2", "wg", "wp")
    (do, dog, du, dgates, d_wa, d_wout, d_wb, d_pool_w, d_pool_scale, d_onorm), (sib1,) = _post_bwd(
        dx2, o, main, gates, ya, yb, pooled, small["hgrn_onorm"], pool_w0, small["pool_scale"], sqw, wbw, tm,
        [_pair_rider(wave1)])
    sums1 = [_add_pair(a, b, core, tag) for a, b, tag in zip(wave1, sib1, tags1)]
    (dqfi, d_lb), (got1,) = _hgrn_bwd(main, small["hgrn_lb"], states, do, tt, [_chips_rider(sums1)])
    own1 = [_add_chips(a, b, chip, tag) for a, b, tag in zip(sums1, got1, tags1)]
    (dx1, dproj, h2, d_mix_norm), (other1,) = _mix_bwd(dqfi, dog, du, dgates, dx2, x1, small["mix_norm"], winw, tm,
                                                        [_share_rider(own1)])
    (d_win,), _ = _wgrad_cols(h2, dproj, N_CHIPS, "wgrad_in", tk)

    wave2 = [_shard_halves(d_wa, 1), _shard_halves(d_wout, 1), _shard_halves(d_wb, 1), _shard_halves(d_win, 1)]
    tags2 = ("wa", "wout", "wb", "win")
    (dx, dab1, s1, h1, dxh1, d_ffn1_norm), (sib2,) = _ffn_bwd(dx1, xs, small["ffn1_norm"], a1, b1, ffn1w, 1, tm,
                                                               [_pair_rider(wave2)])
    sums2 = [_add_pair(a, b, core, tag) for a, b, tag in zip(wave2, sib2, tags2)]
    vecs = dict(ffn1_norm=d_ffn1_norm, mix_norm=d_mix_norm, hgrn_lb=d_lb, hgrn_onorm=d_onorm, ffn2_norm=d_ffn2_norm,
                ple_norm=d_ple, ple_post_norm=d_post, final_norm=d_final, pool_scale=d_pool_scale, pool_w=d_pool_w)
    (d_w13_1,), (got2, (small_all,)) = _wgrad(dab1, h1, WGRAD_IN_BLOCKS, "wgrad_ffn1_in", tk,
                                              [_chips_rider(sums2), _small_rider(_pack_small(vecs))])
    own2 = [_add_chips(a, b, chip, tag) for a, b, tag in zip(sums2, got2, tags2)]
    (d_w2_1,), (other2,) = _wgrad(s1, dxh1, WGRAD_OUT_BLOCKS, "wgrad_ffn1_out", tk, [_share_rider(own2)])

    wave3 = [_shard_halves(d_w13_1, 2), _shard_halves(d_w2_1, 1)]
    tags3 = ("w13_1", "w2_1")
    sib3 = _alone(_pair_rider(wave3), "pair_last")
    sums3 = [_add_pair(a, b, core, tag) for a, b, tag in zip(wave3, sib3, tags3)]
    got3 = _alone(_chips_rider(sums3), "chips_last")
    own3 = [_add_chips(a, b, chip, tag) for a, b, tag in zip(sums3, got3, tags3)]
    other3 = _alone(_share_rider(own3), "share_last")

    out = dict(grad={}, delta={}, new_m={}, new_v={})
    first = lambda arrays: [a[0] for a in arrays]
    _reduce_finish(("ffn2_w1", "ffn2_w3"), own1[0], other1[0], w, m, v, core, out, transposed=True)
    _reduce_finish(("ffn2_w2", "ple_w_gate", "ple_w_proj"), first(own1[1:]), first(other1[1:]), w, m, v, core, out)
    _reduce_finish(("w_branch_a", "w_out", "w_branch_b", "w_in"), first(own2), first(other2), w, m, v, core, out)
    _reduce_finish(("ffn1_w1", "ffn1_w3"), own3[0], other3[0], w, m, v, core, out, transposed=True)
    _reduce_finish(("ffn1_w2",), first(own3[1:]), first(other3[1:]), w, m, v, core, out)

    shapes = {name: w[name].shape for name in VECTOR_PARAMS + ("pool_w",)}
    results = _adamw_small(_pack_small(w), small_all, _pack_small(m), _pack_small(v))
    for key, pack in zip(("grad", "delta", "new_m", "new_v"), results):
        out[key].update(_unpack_small(pack, shapes))

    return (loss, dx[None], *[out["grad"][n] for n in ALL_PARAMS], *[out["delta"][n] for n in ALL_PARAMS],
            *[out["new_m"][n] for n in ALL_PARAMS], *[out["new_v"][n] for n in ALL_PARAMS])
```

```python
import functools

import jax
import jax.numpy as jnp
from jax import lax
from jax.experimental import pallas as pl
from jax.experimental.pallas import tpu as pltpu

F32 = jnp.float32
BF16 = jnp.bfloat16
MESH = pl.DeviceIdType.MESH

D_MODEL = 1024
D_FF = 2816
HEADS = 8
HEAD_DIM = 128
POOL_WIDTH = 512
POOL_WINDOWS = (2, 4, 8, 16)
POOL_HALO = 16
N_CHIPS = 4
EPS = 1e-6
CHUNK = 64
MAIN_COLS = 4096
GATE_COLS = 2048
SHARD_IN_COLS = 1664

ADAM_LR = 0.001
ADAM_B1 = 0.9
ADAM_B2 = 0.999
ADAM_EPS = 1e-08
ADAM_WD = 0.01
ADAM_STEP = 10

VMEM_LIMIT = 56 * 1024 * 1024
WGRAD_IN_BLOCKS = 4
WGRAD_OUT_BLOCKS = 2


def _params(semantics=None, vmem=VMEM_LIMIT):
    return pltpu.CompilerParams(dimension_semantics=semantics, vmem_limit_bytes=vmem)


def _dot(a, b):
    return jnp.dot(a, b, preferred_element_type=F32)


def _dot_nt(a, b):
    return lax.dot_general(a, b, (((1,), (1,)), ((), ())), preferred_element_type=F32)


def _dot_tn(a, b):
    return lax.dot_general(a, b, (((0,), (0,)), ((), ())), preferred_element_type=F32)


def _tri_sum(tri, x):
    hi = x.astype(BF16)
    lo = (x - hi.astype(F32)).astype(BF16)
    return _dot(tri, hi) + _dot(tri, lo)


def _sigmoid(x):
    return jax.nn.sigmoid(x)


def _resident(shape):
    zeros = (0,) * len(shape)
    return pl.BlockSpec(shape, lambda *_: zeros, pipeline_mode=pl.Buffered(1))


def _pick(shape, k):
    zeros = (0,) * (len(shape) - 1)
    return pl.BlockSpec((None,) + tuple(shape[1:]), lambda *_: (k,) + zeros, pipeline_mode=pl.Buffered(1))


def _rows(tm, cols, col_block=0):
    return pl.BlockSpec((tm, cols), lambda i: (i, col_block))


def _acc(shape):
    zeros = (0,) * len(shape)
    return pl.BlockSpec(shape, lambda *_: zeros)


def _rms(x):
    r = lax.rsqrt(jnp.mean(x * x, axis=-1, keepdims=True) + EPS)
    return r, x * r


def _rms_bwd(dn, n, r):
    return r * (dn - n * jnp.mean(dn * n, axis=-1, keepdims=True))


def _colsum(a):
    return jnp.sum(a, axis=0, keepdims=True)


ANY = pl.BlockSpec(memory_space=pl.ANY)


class _Rider:
    def __init__(self, inputs, out_shape, sems, phases):
        self.inputs, self.out_shape, self.sems, self.phases = list(inputs), list(out_shape), list(sems), list(phases)


def _hosted(riders, body, *, name, grid=(), in_specs, out_specs, out_shape, scratch_shapes=(), compiler_params=None):
    riders = [r for r in riders if r is not None]
    n_in, n_out, n_scr = len(in_specs), len(out_shape), len(scratch_shapes)
    n_steps = 1
    for g in grid:
        n_steps *= g

    def wrapped(*refs):
        pos = n_in
        ins = refs[:n_in]
        r_ins = []
        for r in riders:
            r_ins.append(refs[pos:pos + len(r.inputs)])
            pos += len(r.inputs)
        outs = refs[pos:pos + n_out]
        pos += n_out
        r_outs = []
        for r in riders:
            r_outs.append(refs[pos:pos + len(r.out_shape)])
            pos += len(r.out_shape)
        scr = refs[pos:pos + n_scr]
        pos += n_scr
        r_sems = []
        for r in riders:
            r_sems.append(refs[pos:pos + len(r.sems)])
            pos += len(r.sems)
        step = 0
        for axis in range(len(grid)):
            step = step * grid[axis] + pl.program_id(axis)

        def at_step(which, fn):
            if n_steps == 1:
                fn()
            else:
                pl.when(step == which)(fn)

        for r, ri, ro, rs in zip(riders, r_ins, r_outs, r_sems):
            for fraction, fn in r.phases:
                if fraction == 0:
                    at_step(0, functools.partial(fn, ri, ro, rs))
        body(*ins, *outs, *scr)
        for r, ri, ro, rs in zip(riders, r_ins, r_outs, r_sems):
            for fraction, fn in r.phases:
                if fraction > 0:
                    at_step(min(int(fraction * n_steps), n_steps - 1), functools.partial(fn, ri, ro, rs))

    call = pl.pallas_call(
        wrapped, name=name, grid=grid,
        in_specs=list(in_specs) + [ANY for r in riders for _ in r.inputs],
        out_specs=list(out_specs) + [ANY for r in riders for _ in r.out_shape],
        out_shape=list(out_shape) + [s for r in riders for s in r.out_shape],
        scratch_shapes=list(scratch_shapes) + [s for r in riders for s in r.sems],
        compiler_params=compiler_params)

    def run(*args):
        res = call(*args, *[a for r in riders for a in r.inputs])
        extras, pos = [], n_out
        for r in riders:
            extras.append(list(res[pos:pos + len(r.out_shape)]))
            pos += len(r.out_shape)
        return list(res[:n_out]), extras

    return run


def _ffn_fwd(x, g, ffnw, tag, tm, riders=()):
    t = x.shape[0]

    def body(x_ref, g_ref, w1_ref, w3_ref, w2_ref, xo_ref, a_ref, b_ref):
        xv = x_ref[...]
        _, n = _rms(xv)
        h = (n * g_ref[...]).astype(BF16)
        a = _dot_nt(h, w1_ref[...])
        b = _dot_nt(h, w3_ref[...])
        s = (a * _sigmoid(a) * b).astype(BF16)
        xo_ref[...] = xv + 0.5 * _dot(s, w2_ref[...])
        a_ref[...] = a.astype(BF16)
        b_ref[...] = b.astype(BF16)

    return _hosted(
        riders, body, name=f"ffn_fwd_{tag}", grid=(t // tm,),
        in_specs=[_rows(tm, D_MODEL), _resident((1, D_MODEL)), _pick(ffnw.shape, 0), _pick(ffnw.shape, 1),
                  _pick(ffnw.shape, 2)],
        out_specs=[_rows(tm, D_MODEL), _rows(tm, D_FF), _rows(tm, D_FF)],
        out_shape=[jax.ShapeDtypeStruct((t, D_MODEL), F32), jax.ShapeDtypeStruct((t, D_FF), BF16),
                   jax.ShapeDtypeStruct((t, D_FF), BF16)],
        compiler_params=_params(("arbitrary",)),
    )(x, g, ffnw, ffnw, ffnw)


def _ffn_bwd(dxo, x, g, a, b, ffnw, tag, tm, riders=()):
    t = x.shape[0]

    def body(dxo_ref, x_ref, g_ref, a_ref, b_ref, w1_ref, w3_ref, w2_ref, dx_ref, dab_ref, s_ref, h_ref, dxh_ref, dg_ref):
        @pl.when(pl.program_id(0) == 0)
        def _():
            dg_ref[...] = jnp.zeros_like(dg_ref)

        xv = x_ref[...]
        gv = g_ref[...]
        r, n = _rms(xv)
        h_ref[...] = (n * gv).astype(BF16)
        dxo_v = dxo_ref[...]
        dxh = (0.5 * dxo_v).astype(BF16)
        dxh_ref[...] = dxh
        ds = _dot_nt(dxh, w2_ref[...])
        av = a_ref[...].astype(F32)
        bv = b_ref[...].astype(F32)
        sg = _sigmoid(av)
        silu = av * sg
        s_ref[...] = (silu * bv).astype(BF16)
        da = (ds * bv * (sg * (1.0 + av * (1.0 - sg)))).astype(BF16)
        db = (ds * silu).astype(BF16)
        dab_ref[:, :D_FF] = da
        dab_ref[:, D_FF:] = db
        dh = _dot(da, w1_ref[...]) + _dot(db, w3_ref[...])
        dg_ref[...] += _colsum(dh * n)
        dx_ref[...] = dxo_v + _rms_bwd(dh * gv, n, r)

    return _hosted(
        riders, body, name=f"ffn_bwd_{tag}", grid=(t // tm,),
        in_specs=[_rows(tm, D_MODEL), _rows(tm, D_MODEL), _resident((1, D_MODEL)), _rows(tm, D_FF), _rows(tm, D_FF),
                  _pick(ffnw.shape, 0), _pick(ffnw.shape, 1), _pick(ffnw.shape, 2)],
        out_specs=[_rows(tm, D_MODEL), _rows(tm, 2 * D_FF), _rows(tm, D_FF), _rows(tm, D_MODEL), _rows(tm, D_MODEL),
                   _acc((1, D_MODEL))],
        out_shape=[jax.ShapeDtypeStruct((t, D_MODEL), F32), jax.ShapeDtypeStruct((t, 2 * D_FF), BF16),
                   jax.ShapeDtypeStruct((t, D_FF), BF16), jax.ShapeDtypeStruct((t, D_MODEL), BF16),
                   jax.ShapeDtypeStruct((t, D_MODEL), BF16), jax.ShapeDtypeStruct((1, D_MODEL), F32)],
        compiler_params=_params(("arbitrary",)),
    )(dxo, x, g, a, b, ffnw, ffnw, ffnw)


def _wgrad(xm, dy, out_blocks, name, tk, riders=()):
    t, m = xm.shape
    n = dy.shape[1]
    mb = m // out_blocks

    def body(x_ref, dy_ref, o_ref):
        @pl.when(pl.program_id(1) == 0)
        def _():
            o_ref[...] = jnp.zeros_like(o_ref)

        o_ref[...] += _dot_tn(x_ref[...], dy_ref[...])

    return _hosted(
        riders, body, name=name, grid=(out_blocks, t // tk),
        in_specs=[pl.BlockSpec((tk, mb), lambda j, k: (k, j)), pl.BlockSpec((tk, n), lambda j, k: (k, 0))],
        out_specs=[pl.BlockSpec((None, mb, n), lambda j, k: (j, 0, 0))],
        out_shape=[jax.ShapeDtypeStruct((out_blocks, mb, n), F32)],
        compiler_params=_params(("arbitrary", "arbitrary")),
    )(xm, dy)


def _wgrad_cols(xm, dy, out_blocks, name, tk, riders=()):
    t, m = xm.shape
    n = dy.shape[1]
    nb = n // out_blocks

    def body(x_ref, dy_ref, o_ref):
        @pl.when(pl.program_id(1) == 0)
        def _():
            o_ref[...] = jnp.zeros_like(o_ref)

        o_ref[...] += _dot_tn(x_ref[...], dy_ref[...])

    return _hosted(
        riders, body, name=name, grid=(out_blocks, t // tk),
        in_specs=[pl.BlockSpec((tk, m), lambda j, k: (k, 0)), pl.BlockSpec((tk, nb), lambda j, k: (k, j))],
        out_specs=[pl.BlockSpec((None, m, nb), lambda j, k: (j, 0, 0))],
        out_shape=[jax.ShapeDtypeStruct((out_blocks, m, nb), F32)],
        compiler_params=_params(("arbitrary", "arbitrary")),
    )(xm, dy)


def _mix_fwd(x1, g, winw, tm, riders=()):
    t = x1.shape[0]

    def body(x_ref, g_ref, w_ref, main_ref, pool_ref, gate_ref):
        _, n = _rms(x_ref[...])
        h = (n * g_ref[...]).astype(BF16)
        proj = jnp.concatenate([_dot(h, w_ref[j]) for j in range(N_CHIPS)], axis=1)
        main_ref[...] = proj[:, :MAIN_COLS]
        pool_ref[...] = proj[:, MAIN_COLS:MAIN_COLS + POOL_WIDTH]
        gate_ref[...] = proj[:, MAIN_COLS + POOL_WIDTH:]

    return _hosted(
        riders, body, name="mix_fwd", grid=(t // tm,),
        in_specs=[_rows(tm, D_MODEL), _resident((1, D_MODEL)), _resident(winw.shape)],
        out_specs=[_rows(tm, MAIN_COLS), _rows(tm, POOL_WIDTH), _rows(tm, GATE_COLS)],
        out_shape=[jax.ShapeDtypeStruct((t, MAIN_COLS), F32), jax.ShapeDtypeStruct((t, POOL_WIDTH), F32),
                   jax.ShapeDtypeStruct((t, GATE_COLS), F32)],
        compiler_params=_params(("arbitrary",)),
    )(x1, g, winw)


def _mix_bwd(dqfi, dog, du, dgates, dx2, x1, g, winw, tm, riders=()):
    t = x1.shape[0]
    cols = N_CHIPS * SHARD_IN_COLS

    def body(dqfi_ref, dog_ref, du_ref, dgt_ref, dx2_ref, x_ref, g_ref, w_ref, dx_ref, dproj_ref, h_ref, dg_ref):
        @pl.when(pl.program_id(0) == 0)
        def _():
            dg_ref[...] = jnp.zeros_like(dg_ref)

        dproj = jnp.concatenate([dqfi_ref[...], dog_ref[...], du_ref[...], dgt_ref[...]], axis=1)
        dproj_ref[...] = dproj
        dh = _dot_nt(dproj[:, :SHARD_IN_COLS], w_ref[0])
        for j in range(1, N_CHIPS):
            dh += _dot_nt(dproj[:, j * SHARD_IN_COLS:(j + 1) * SHARD_IN_COLS], w_ref[j])
        gv = g_ref[...]
        r, n = _rms(x_ref[...])
        h_ref[...] = (n * gv).astype(BF16)
        dg_ref[...] += _colsum(dh * n)
        dx_ref[...] = dx2_ref[...] + _rms_bwd(dh * gv, n, r)

    return _hosted(
        riders, body, name="mix_bwd", grid=(t // tm,),
        in_specs=[_rows(tm, 3 * D_MODEL), _rows(tm, D_MODEL), _rows(tm, POOL_WIDTH), _rows(tm, GATE_COLS),
                  _rows(tm, D_MODEL), _rows(tm, D_MODEL), _resident((1, D_MODEL)), _resident(winw.shape)],
        out_specs=[_rows(tm, D_MODEL), _rows(tm, cols), _rows(tm, D_MODEL), _acc((1, D_MODEL))],
        out_shape=[jax.ShapeDtypeStruct((t, D_MODEL), F32), jax.ShapeDtypeStruct((t, cols), BF16),
                   jax.ShapeDtypeStruct((t, D_MODEL), BF16), jax.ShapeDtypeStruct((1, D_MODEL), F32)],
        compiler_params=_params(("arbitrary",)),
    )(dqfi, dog, du, dgates, dx2, x1, g, winw)


def _lower_bound(lb_raw):
    l0 = lb_raw[0:1, :]
    l1 = lb_raw[1:2, :]
    m = jnp.maximum(l0, l1)
    e0 = jnp.exp(l0 - m)
    e1 = jnp.exp(l1 - m)
    return e0 / (e0 + e1)


def _head_slices():
    return [slice(h * HEAD_DIM, (h + 1) * HEAD_DIM) for h in range(HEADS)]


def _gates(qr, fr, lb, tril_b, first_half):
    sg = _sigmoid(fr)
    f = lb + (1.0 - lb) * sg
    k = 1.0 - f
    sq = _sigmoid(qr)
    q = qr * sq
    log_f = jnp.log(f)
    gc = _tri_sum(tril_b, log_f)
    gm = _colsum(jnp.where(first_half, log_f, 0.0))
    gl = _colsum(log_f)
    e_q = jnp.exp(gc - gm)
    e_k = jnp.exp(gm - gc)
    e_in = jnp.exp(gc)
    e_out = jnp.exp(gl - gc)
    return dict(sg=sg, f=f, k=k, sq=sq, q=q, e_q=e_q, e_k=e_k, e_in=e_in, e_out=e_out, e_last=jnp.exp(gl))


def _hgrn_fwd(main, lb_raw, tt, riders=()):
    t = main.shape[0]
    n_local = tt // CHUNK

    def body(q_ref, f_ref, i_ref, lb_ref, o_ref, st_ref, s_scr):
        @pl.when(pl.program_id(0) == 0)
        def _():
            s_scr[...] = jnp.zeros_like(s_scr)

        lb = _lower_bound(lb_ref[...])
        row = lax.broadcasted_iota(jnp.int32, (CHUNK, CHUNK), 0)
        col = lax.broadcasted_iota(jnp.int32, (CHUNK, CHUNK), 1)
        tril = row >= col
        tril_b = tril.astype(BF16)
        first_half = lax.broadcasted_iota(jnp.int32, (CHUNK, D_MODEL), 0) < CHUNK // 2
        heads = _head_slices()

        def chunk(c, carry):
            rows = pl.ds(pl.multiple_of(c * CHUNK, CHUNK), CHUNK)
            z = _gates(q_ref[rows, :], f_ref[rows, :], lb, tril_b, first_half)
            qt = (z["q"] * z["e_q"]).astype(BF16)
            kt = (z["k"] * z["e_k"]).astype(BF16)
            qg = (z["q"] * z["e_in"]).astype(BF16)
            kg = (z["k"] * z["e_out"]).astype(BF16)
            vb = i_ref[rows, :].astype(BF16)
            states = [s_scr[h] for h in range(HEADS)]
            for h in range(HEADS):
                st_ref[c, h] = states[h]
            raw = [_dot_nt(qt[:, sl], kt[:, sl]) for sl in heads]
            inter = [_dot_nt(qg[:, sl], states[h].astype(BF16)) for h, sl in enumerate(heads)]
            grown = [_dot_tn(vb[:, sl], kg[:, sl]) for sl in heads]
            scores = [jnp.where(tril, r, 0.0).astype(BF16) for r in raw]
            for h, sl in enumerate(heads):
                s_scr[h] = states[h] * z["e_last"][:, sl] + grown[h]
            o_ref[rows, :] = jnp.concatenate([_dot(scores[h], vb[:, sl]) + inter[h] for h, sl in enumerate(heads)], axis=1)
            return carry

        lax.fori_loop(0, n_local, chunk, 0, unroll=2)

    return _hosted(
        riders, body, name="hgrn_fwd", grid=(t // tt,),
        in_specs=[_rows(tt, D_MODEL, 0), _rows(tt, D_MODEL, 1), _rows(tt, D_MODEL, 2), _resident((2, D_MODEL))],
        out_specs=[_rows(tt, D_MODEL),
                   pl.BlockSpec((n_local, HEADS, HEAD_DIM, HEAD_DIM), lambda i: (i, 0, 0, 0))],
        out_shape=[jax.ShapeDtypeStruct((t, D_MODEL), F32),
                   jax.ShapeDtypeStruct((t // CHUNK, HEADS, HEAD_DIM, HEAD_DIM), F32)],
        scratch_shapes=[pltpu.VMEM((HEADS, HEAD_DIM, HEAD_DIM), F32)],
        compiler_params=_params(("arbitrary",)),
    )(main, main, main, lb_raw)


def _hgrn_bwd(main, lb_raw, states, do, tt, riders=()):
    t = main.shape[0]
    n_tiles = t // tt
    n_local = tt // CHUNK

    def rev(col_block):
        return pl.BlockSpec((tt, D_MODEL), lambda i: (n_tiles - 1 - i, col_block))

    def body(q_ref, f_ref, i_ref, lb_ref, st_ref, do_ref, dqfi_ref, dlb_ref, ds_scr, acc_scr):
        @pl.when(pl.program_id(0) == 0)
        def _():
            ds_scr[...] = jnp.zeros_like(ds_scr)
            acc_scr[...] = jnp.zeros_like(acc_scr)

        lb = _lower_bound(lb_ref[...])
        row = lax.broadcasted_iota(jnp.int32, (CHUNK, CHUNK), 0)
        col = lax.broadcasted_iota(jnp.int32, (CHUNK, CHUNK), 1)
        tril = row >= col
        tril_b = tril.astype(BF16)
        triu_b = (row <= col).astype(BF16)
        first_half = lax.broadcasted_iota(jnp.int32, (CHUNK, D_MODEL), 0) < CHUNK // 2
        heads = _head_slices()
        cat = functools.partial(jnp.concatenate, axis=1)

        def chunk(cc, carry):
            c = n_local - 1 - cc
            rows = pl.ds(pl.multiple_of(c * CHUNK, CHUNK), CHUNK)
            qr = q_ref[rows, :]
            z = _gates(qr, f_ref[rows, :], lb, tril_b, first_half)
            qt = (z["q"] * z["e_q"]).astype(BF16)
            kt = (z["k"] * z["e_k"]).astype(BF16)
            qg_f = z["q"] * z["e_in"]
            qg = qg_f.astype(BF16)
            kg_f = z["k"] * z["e_out"]
            kg = kg_f.astype(BF16)
            vb = i_ref[rows, :].astype(BF16)
            dob = do_ref[rows, :].astype(BF16)
            st = [st_ref[c, h] for h in range(HEADS)]
            dst = [ds_scr[h] for h in range(HEADS)]
            dst_b = [d.astype(BF16) for d in dst]
            raw = [_dot_nt(qt[:, sl], kt[:, sl]) for sl in heads]
            draw = [_dot_nt(dob[:, sl], vb[:, sl]) for sl in heads]
            dqg = [_dot(dob[:, sl], st[h].astype(BF16)) for h, sl in enumerate(heads)]
            dkg = [_dot(vb[:, sl], dst_b[h]) for h, sl in enumerate(heads)]
            dv_inter = [_dot_nt(kg[:, sl], dst_b[h]) for h, sl in enumerate(heads)]
            grown = [_dot_tn(dob[:, sl], qg[:, sl]) for sl in heads]
            scores = [jnp.where(tril, r, 0.0).astype(BF16) for r in raw]
            dscores = [jnp.where(tril, r, 0.0).astype(BF16) for r in draw]
            dqt = [_dot(dscores[h], kt[:, sl]) for h, sl in enumerate(heads)]
            dkt = [_dot_tn(dscores[h], qt[:, sl]) for h, sl in enumerate(heads)]
            dv = [_dot_tn(scores[h], dob[:, sl]) + dv_inter[h] for h, sl in enumerate(heads)]
            carry_in = cat([z["e_last"][:, sl] * _colsum(dst[h] * st[h]) for h, sl in enumerate(heads)])
            for h, sl in enumerate(heads):
                ds_scr[h] = dst[h] * z["e_last"][:, sl] + grown[h]
            dqt, dkt, dqg, dkg = cat(dqt), cat(dkt), cat(dqg), cat(dkg)
            carry_in += _colsum(dkg * kg_f)
            dq = dqt * z["e_q"] + dqg * z["e_in"]
            dk = dkt * z["e_k"] + dkg * z["e_out"]
            dgate = (qt.astype(F32) * dqt - kt.astype(F32) * dkt) + (qg_f * dqg - kg_f * dkg)
            dlogf = _tri_sum(triu_b, dgate) + carry_in
            df = dlogf / z["f"] - dk
            sg = z["sg"]
            sq = z["sq"]
            acc_scr[...] += _colsum(df * (1.0 - sg))
            dqfi_ref[rows, 0:D_MODEL] = (dq * (sq * (1.0 + qr * (1.0 - sq)))).astype(BF16)
            dqfi_ref[rows, D_MODEL:2 * D_MODEL] = (df * (1.0 - lb) * sg * (1.0 - sg)).astype(BF16)
            dqfi_ref[rows, 2 * D_MODEL:3 * D_MODEL] = cat(dv).astype(BF16)
            return carry

        lax.fori_loop(0, n_local, chunk, 0, unroll=2)
        d0 = acc_scr[...] * lb * (1.0 - lb)
        dlb_ref[0:1, :] = d0
        dlb_ref[1:2, :] = -d0

    return _hosted(
        riders, body, name="hgrn_bwd", grid=(n_tiles,),
        in_specs=[rev(0), rev(1), rev(2), _resident((2, D_MODEL)),
                  pl.BlockSpec((n_local, HEADS, HEAD_DIM, HEAD_DIM), lambda i: (n_tiles - 1 - i, 0, 0, 0)),
                  rev(0)],
        out_specs=[pl.BlockSpec((tt, 3 * D_MODEL), lambda i: (n_tiles - 1 - i, 0)), _acc((2, D_MODEL))],
        out_shape=[jax.ShapeDtypeStruct((t, 3 * D_MODEL), BF16), jax.ShapeDtypeStruct((2, D_MODEL), F32)],
        scratch_shapes=[pltpu.VMEM((HEADS, HEAD_DIM, HEAD_DIM), F32), pltpu.VMEM((1, D_MODEL), F32)],
        compiler_params=_params(("arbitrary",)),
    )(main, main, main, lb_raw, states, do)


def _head_norm(o):
    rs, ns = [], []
    for h in range(HEADS):
        oh = o[:, h * HEAD_DIM:(h + 1) * HEAD_DIM]
        r, n = _rms(oh)
        rs.append(jnp.broadcast_to(r, oh.shape))
        ns.append(n)
    return jnp.concatenate(rs, axis=1), jnp.concatenate(ns, axis=1)


def _head_norm_bwd(dn, n, r):
    outs = []
    for h in range(HEADS):
        sl = slice(h * HEAD_DIM, (h + 1) * HEAD_DIM)
        outs.append(_rms_bwd(dn[:, sl], n[:, sl], r[:, sl]))
    return jnp.concatenate(outs, axis=1)


def _window_counts(first_row, tm):
    pos = (first_row + 1 + lax.broadcasted_iota(jnp.int32, (tm, 1), 0)).astype(F32)
    return [jnp.minimum(pos, float(w)) for w in POOL_WINDOWS]


def _post_fwd(o, main, pool_r, gates, x1, onorm, pool_w, pool_scale, sqw, wbw, tm, riders=()):
    t = o.shape[0]
    ext_rows = tm + POOL_HALO

    def body(o_ref, og_ref, u_ref, gt_ref, x1_ref, on_ref, pw_ref, ps_ref, wa_ref, wout_ref, wb_ref,
             x2_ref, ya_ref, yb_ref, pooled_ref, ext):
        i = pl.program_id(0)

        @pl.when(i == 0)
        def _():
            ext[0:POOL_HALO, :] = jnp.zeros((POOL_HALO, POOL_WIDTH), F32)

        _, n = _head_norm(o_ref[...])
        og = og_ref[...]
        oa = (n * on_ref[...] * (og * _sigmoid(og))).astype(BF16)
        ya = _dot(oa, wa_ref[...])

        u = u_ref[...]
        ext[POOL_HALO:ext_rows, :] = u
        e = ext[...]
        counts = _window_counts(i * tm, tm)
        pooled = []
        for gidx, w in enumerate(POOL_WINDOWS):
            s = e[:, gidx * HEAD_DIM:(gidx + 1) * HEAD_DIM]
            shift = 1
            while shift < w:
                s = s + pltpu.roll(s, shift, axis=0)
                shift *= 2
            pooled.append(s[POOL_HALO:, :] / counts[gidx] - u[:, gidx * HEAD_DIM:(gidx + 1) * HEAD_DIM])
        ext[0:POOL_HALO, :] = ext[tm:ext_rows, :]
        pooled_b = [pg.astype(BF16) for pg in pooled]
        pooled_ref[...] = jnp.concatenate(pooled_b, axis=1)
        mixed = jnp.concatenate([_dot(pooled_b[gidx], pw_ref[gidx].astype(BF16)) for gidx in range(len(POOL_WINDOWS))],
                                axis=1) * ps_ref[...]
        mixed_b = mixed.astype(BF16)
        yb = jnp.concatenate([_dot(mixed_b, wb_ref[j]) for j in range(N_CHIPS)], axis=1)

        gt = gt_ref[...]
        y = _sigmoid(gt[:, :D_MODEL]) * ya + _sigmoid(gt[:, D_MODEL:]) * yb
        x2_ref[...] = x1_ref[...] + _dot(y.astype(BF16), wout_ref[...])
        ya_ref[...] = ya.astype(BF16)
        yb_ref[...] = yb.astype(BF16)

    return _hosted(
        riders, body, name="post_fwd", grid=(t // tm,),
        in_specs=[_rows(tm, D_MODEL), _rows(tm, D_MODEL, 3), _rows(tm, POOL_WIDTH), _rows(tm, GATE_COLS), _rows(tm, D_MODEL),
                  _resident((1, D_MODEL)), _resident(pool_w.shape), _resident((1, POOL_WIDTH)),
                  _pick(sqw.shape, 0), _pick(sqw.shape, 1), _resident(wbw.shape)],
        out_specs=[_rows(tm, D_MODEL), _rows(tm, D_MODEL), _rows(tm, D_MODEL), _rows(tm, POOL_WIDTH)],
        out_shape=[jax.ShapeDtypeStruct((t, D_MODEL), F32), jax.ShapeDtypeStruct((t, D_MODEL), BF16),
                   jax.ShapeDtypeStruct((t, D_MODEL), BF16), jax.ShapeDtypeStruct((t, POOL_WIDTH), BF16)],
        scratch_shapes=[pltpu.VMEM((ext_rows, POOL_WIDTH), F32)],
        compiler_params=_params(("arbitrary",)),
    )(o, main, pool_r, gates, x1, onorm, pool_w, pool_scale, sqw, sqw, wbw)


def _post_bwd(dx2, o, main, gates, ya, yb, pooled, onorm, pool_w, pool_scale, sqw, wbw, tm, riders=()):
    t = o.shape[0]
    n_tiles = t // tm
    ext_rows = tm + POOL_HALO
    n_groups = len(POOL_WINDOWS)

    def rev(cols, col_block=0):
        return pl.BlockSpec((tm, cols), lambda i: (n_tiles - 1 - i, col_block))

    def body(dx2_ref, o_ref, og_ref, gt_ref, ya_ref, yb_ref, pooled_ref, on_ref, pw_ref, ps_ref, wa_ref, wout_ref, wb_ref,
             do_ref, dog_ref, du_ref, dgt_ref, dwa_ref, dwout_ref, dwb_ref, dpw_ref, dps_ref, don_ref, ext):
        i = pl.program_id(0)

        @pl.when(i == 0)
        def _():
            ext[tm:ext_rows, :] = jnp.zeros((POOL_HALO, POOL_WIDTH), F32)
            for ref in (dwa_ref, dwout_ref, dwb_ref, dpw_ref, dps_ref, don_ref):
                ref[...] = jnp.zeros_like(ref)

        dx2b = dx2_ref[...].astype(BF16)
        dy = _dot_nt(dx2b, wout_ref[...])
        gt = gt_ref[...]
        sga = _sigmoid(gt[:, :D_MODEL])
        sgb = _sigmoid(gt[:, D_MODEL:])
        ya = ya_ref[...].astype(F32)
        yb = yb_ref[...].astype(F32)
        y = (sga * ya + sgb * yb).astype(BF16)
        dwout_ref[...] += _dot_tn(y, dx2b)
        dya = (dy * sga).astype(BF16)
        dyb = (dy * sgb).astype(BF16)
        dgt_ref[:, :D_MODEL] = (dy * ya * sga * (1.0 - sga)).astype(BF16)
        dgt_ref[:, D_MODEL:] = (dy * yb * sgb * (1.0 - sgb)).astype(BF16)

        r, n = _head_norm(o_ref[...])
        onv = on_ref[...]
        og = og_ref[...]
        sog = _sigmoid(og)
        silu_og = og * sog
        normed = n * onv
        dwa_ref[...] += _dot_tn((normed * silu_og).astype(BF16), dya)
        doa = _dot_nt(dya, wa_ref[...])
        dog_ref[...] = (doa * normed * (sog * (1.0 + og * (1.0 - sog)))).astype(BF16)
        dnormed = doa * silu_og
        don_ref[...] += _colsum(dnormed * n)
        do_ref[...] = _head_norm_bwd(dnormed * onv, n, r)

        psv = ps_ref[...]
        dmixed = _dot_nt(dyb[:, :256], wb_ref[0])
        for j in range(1, N_CHIPS):
            dmixed += _dot_nt(dyb[:, j * 256:(j + 1) * 256], wb_ref[j])
        pooled_b = pooled_ref[...]
        pm = jnp.concatenate(
            [_dot(pooled_b[:, gidx * HEAD_DIM:(gidx + 1) * HEAD_DIM], pw_ref[gidx].astype(BF16)) for gidx in range(n_groups)],
            axis=1)
        mixed_b = (pm * psv).astype(BF16)
        for j in range(N_CHIPS):
            dwb_ref[j] += _dot_tn(mixed_b, dyb[:, j * 256:(j + 1) * 256])
        dps_ref[...] += _colsum(dmixed * pm)
        dpm = (dmixed * psv).astype(BF16)
        counts = _window_counts((n_tiles - 1 - i) * tm, tm)
        dpooled = []
        for gidx in range(n_groups):
            sl = slice(gidx * HEAD_DIM, (gidx + 1) * HEAD_DIM)
            dpw_ref[gidx] += _dot_tn(pooled_b[:, sl], dpm[:, sl])
            dpooled.append(_dot_nt(dpm[:, sl], pw_ref[gidx].astype(BF16)))
        ext[0:tm, :] = jnp.concatenate([dpooled[gidx] / counts[gidx] for gidx in range(n_groups)], axis=1)
        e = ext[...]
        du = []
        for gidx, w in enumerate(POOL_WINDOWS):
            s = e[:, gidx * HEAD_DIM:(gidx + 1) * HEAD_DIM]
            shift = 1
            while shift < w:
                s = s + pltpu.roll(s, ext_rows - shift, axis=0)
                shift *= 2
            du.append(s[:tm, :] - dpooled[gidx])
        ext[tm:ext_rows, :] = ext[0:POOL_HALO, :]
        du_ref[...] = jnp.concatenate(du, axis=1).astype(BF16)

    wa_shape = (D_MODEL, D_MODEL)
    return _hosted(
        riders, body, name="post_bwd", grid=(n_tiles,),
        in_specs=[rev(D_MODEL), rev(D_MODEL), rev(D_MODEL, 3), rev(GATE_COLS), rev(D_MODEL), rev(D_MODEL), rev(POOL_WIDTH),
                  _resident((1, D_MODEL)), _resident(pool_w.shape), _resident((1, POOL_WIDTH)),
                  _pick(sqw.shape, 0), _pick(sqw.shape, 1), _resident(wbw.shape)],
        out_specs=[rev(D_MODEL), rev(D_MODEL), rev(POOL_WIDTH), rev(GATE_COLS),
                   _acc(wa_shape), _acc(wa_shape), _acc(wbw.shape), _acc(pool_w.shape), _acc((1, POOL_WIDTH)),
                   _acc((1, D_MODEL))],
        out_shape=[jax.ShapeDtypeStruct((t, D_MODEL), F32), jax.ShapeDtypeStruct((t, D_MODEL), BF16),
                   jax.ShapeDtypeStruct((t, POOL_WIDTH), BF16), jax.ShapeDtypeStruct((t, GATE_COLS), BF16),
                   jax.ShapeDtypeStruct(wa_shape, F32), jax.ShapeDtypeStruct(wa_shape, F32),
                   jax.ShapeDtypeStruct(wbw.shape, F32), jax.ShapeDtypeStruct(pool_w.shape, F32),
                   jax.ShapeDtypeStruct((1, POOL_WIDTH), F32), jax.ShapeDtypeStruct((1, D_MODEL), F32)],
        scratch_shapes=[pltpu.VMEM((ext_rows, POOL_WIDTH), F32)],
        compiler_params=_params(("arbitrary",)),
    )(dx2, o, main, gates, ya, yb, pooled, onorm, pool_w, pool_scale, sqw, sqw, wbw)


def _tail(x3, p, target, g_ple, g_post, g_final, sqw, wpw, tm, riders=()):
    t = x3.shape[0]
    pd = p.shape[1]

    def body(x_ref, p_ref, tg_ref, g4_ref, g5_ref, g6_ref, wg_ref, wp_ref,
             dx_ref, loss_ref, dwg_ref, dwp_ref, dg4_ref, dg5_ref, dg6_ref):
        @pl.when(pl.program_id(0) == 0)
        def _():
            for ref in (loss_ref, dwg_ref, dwp_ref, dg4_ref, dg5_ref, dg6_ref):
                ref[...] = jnp.zeros_like(ref)

        x3v = x_ref[...]
        g4, g5, g6 = g4_ref[...], g5_ref[...], g6_ref[...]
        r4, n4 = _rms(x3v)
        h4 = (n4 * g4).astype(BF16)
        gate = _sigmoid(_dot(h4, wg_ref[...]))
        pb = p_ref[...].astype(BF16)
        r5, n5 = _rms(jnp.concatenate([_dot(pb, wp_ref[j]) for j in range(N_CHIPS)], axis=1))
        emb = n5 * g5
        r6, n6 = _rms(x3v + gate * emb)
        diff = n6 * g6 - tg_ref[...]
        loss_ref[...] += 0.5 * jnp.sum(jnp.mean(diff * diff, axis=-1, keepdims=True), axis=0, keepdims=True)
        dout = diff * (1.0 / D_MODEL)
        dg6_ref[...] += _colsum(dout * n6)
        dx4 = _rms_bwd(dout * g6, n6, r6)
        demb = dx4 * gate
        dg5_ref[...] += _colsum(demb * n5)
        dpre = _rms_bwd(demb * g5, n5, r5).astype(BF16)
        for j in range(N_CHIPS):
            dwp_ref[j] += _dot_tn(pb, dpre[:, j * pd:(j + 1) * pd])
        dz = (dx4 * emb * gate * (1.0 - gate)).astype(BF16)
        dwg_ref[...] += _dot_tn(h4, dz)
        dh4 = _dot_nt(dz, wg_ref[...])
        dg4_ref[...] += _colsum(dh4 * n4)
        dx_ref[...] = dx4 + _rms_bwd(dh4 * g4, n4, r4)

    sq_shape = (D_MODEL, D_MODEL)
    vec = (1, D_MODEL)
    return _hosted(
        riders, body, name="tail", grid=(t // tm,),
        in_specs=[_rows(tm, D_MODEL), _rows(tm, pd), _rows(tm, D_MODEL), _resident(vec), _resident(vec), _resident(vec),
                  _pick(sqw.shape, 2), _resident(wpw.shape)],
        out_specs=[_rows(tm, D_MODEL), _acc((1, 1)), _acc(sq_shape), _acc(wpw.shape), _acc(vec), _acc(vec), _acc(vec)],
        out_shape=[jax.ShapeDtypeStruct((t, D_MODEL), F32), jax.ShapeDtypeStruct((1, 1), F32),
                   jax.ShapeDtypeStruct(sq_shape, F32), jax.ShapeDtypeStruct(wpw.shape, F32),
                   jax.ShapeDtypeStruct(vec, F32), jax.ShapeDtypeStruct(vec, F32), jax.ShapeDtypeStruct(vec, F32)],
        compiler_params=_params(("arbitrary",)),
    )(x3, p, target, g_ple, g_post, g_final, sqw, wpw)


def _position():
    return lax.axis_index("x"), lax.axis_index("y"), lax.axis_index("c")


def _other_chips(x, y):
    return [(1 - x, y), (x, 1 - y), (1 - x, 1 - y)]


def _remote(src, dst, send_sems, recv_sems, k, device):
    return pltpu.make_async_remote_copy(src_ref=src, dst_ref=dst, send_sem=send_sems.at[k], recv_sem=recv_sems.at[k],
                                        device_id=device, device_id_type=MESH)


def _gather_rider(shards, forward_at):
    n = len(shards)

    def copies(ins, outs, sems):
        send_sems, recv_sems, local_sems = sems
        x, y, c = _position()
        mine = 2 * x + y
        local = [pltpu.make_async_copy(ins[a], outs[a].at[:, mine], local_sems.at[a]) for a in range(n)]
        first, passed, arriving = [], [], []
        for k, (cx, cy) in enumerate(_other_chips(x, y)):
            theirs = 2 * cx + cy
            for a in range(n):
                first.append(_remote(ins[a].at[:, c], outs[a].at[:, mine, c], send_sems, recv_sems, k * n + a, (cx, cy, c)))
                block = outs[a].at[:, theirs, c]
                passed.append(_remote(block, block, send_sems, recv_sems, (3 + k) * n + a, (x, y, 1 - c)))
                other = outs[a].at[:, theirs, 1 - c]
                arriving.append(_remote(other, other, send_sems, recv_sems, (3 + k) * n + a, (x, y, 1 - c)))
        return local, first, passed, arriving

    def begin(ins, outs, sems):
        local, first, _, _ = copies(ins, outs, sems)
        for cp in local + first:
            cp.start()

    def forward(ins, outs, sems):
        _, first, passed, _ = copies(ins, outs, sems)
        for got, cp in zip(first, passed):
            got.wait_recv()
            cp.start()

    def finish(ins, outs, sems):
        local, first, passed, arriving = copies(ins, outs, sems)
        for cp in arriving:
            cp.wait_recv()
        for cp in first + passed:
            cp.wait_send()
        for cp in local:
            cp.wait()

    return _Rider(shards, [jax.ShapeDtypeStruct((s.shape[0], N_CHIPS) + s.shape[1:], s.dtype) for s in shards],
                  [pltpu.SemaphoreType.DMA((6 * n,)), pltpu.SemaphoreType.DMA((6 * n,)), pltpu.SemaphoreType.DMA((n,))],
                  [(0, begin), (forward_at, forward), (1, finish)])


def _exchange_rider(arrays, out_shape, n_copies, transfers, n_local=0):
    def copies(ins, outs, sems):
        send_sems, recv_sems, local_sems = sems
        remote, local = transfers(ins, outs)
        return ([_remote(src, dst, send_sems, recv_sems, i, dev) for i, (src, dst, dev) in enumerate(remote)],
                [pltpu.make_async_copy(src, dst, local_sems.at[i]) for i, (src, dst) in enumerate(local)])

    def begin(ins, outs, sems):
        remote, local = copies(ins, outs, sems)
        for cp in remote + local:
            cp.start()

    def finish(ins, outs, sems):
        remote, local = copies(ins, outs, sems)
        for cp in remote:
            cp.wait_recv()
        for cp in remote:
            cp.wait_send()
        for cp in local:
            cp.wait()

    return _Rider(arrays, out_shape,
                  [pltpu.SemaphoreType.DMA((n_copies,)), pltpu.SemaphoreType.DMA((n_copies,)),
                   pltpu.SemaphoreType.DMA((max(n_local, 1),))],
                  [(0, begin), (1, finish)])


def _pair_rider(partials):
    def transfers(ins, outs):
        x, y, c = _position()
        return [(ins[a].at[:, :, 1 - c], outs[a], (x, y, 1 - c)) for a in range(len(partials))], []

    shapes = [jax.ShapeDtypeStruct(g.shape[:2] + g.shape[3:], g.dtype) for g in partials]
    return _exchange_rider(partials, shapes, len(partials), transfers)


def _chips_rider(sums):
    n = len(sums)

    def transfers(ins, outs):
        x, y, c = _position()
        return [(ins[a].at[:, 2 * cx + cy], outs[a].at[:, k], (cx, cy, c))
                for k, (cx, cy) in enumerate(_other_chips(x, y)) for a in range(n)], []

    shapes = [jax.ShapeDtypeStruct((q.shape[0], 3) + q.shape[2:], q.dtype) for q in sums]
    return _exchange_rider(sums, shapes, 3 * n, transfers)


def _share_rider(halves):
    def transfers(ins, outs):
        x, y, c = _position()
        return [(ins[a], outs[a], (x, y, 1 - c)) for a in range(len(halves))], []

    return _exchange_rider(halves, [jax.ShapeDtypeStruct(h.shape, h.dtype) for h in halves], len(halves), transfers)


def _small_rider(pack):
    flips = [(fx, fy, fc) for fx in (0, 1) for fy in (0, 1) for fc in (0, 1)][1:]

    def transfers(ins, outs):
        x, y, c = _position()
        slot = outs[0].at[4 * x + 2 * y + c]
        flip = lambda v, f: v + f - 2 * v * f
        return [(ins[0], slot, (flip(x, fx), flip(y, fy), flip(c, fc))) for fx, fy, fc in flips], [(ins[0], slot)]

    return _exchange_rider([pack], [jax.ShapeDtypeStruct((8,) + pack.shape, pack.dtype)], len(flips), transfers, n_local=1)


def _alone(rider, name):
    return _hosted([rider], lambda: None, name=name, in_specs=[], out_specs=[], out_shape=[])()[1][0]


def _add_pair(mine, theirs, c, tag):
    l, _, hr, cols = theirs.shape

    def body(c_ref, mine_ref, theirs_ref, out_ref):
        out_ref[...] = (mine_ref[...] + theirs_ref[...]).astype(BF16)

    block = (None, None, hr, cols)
    return pl.pallas_call(
        body, name=f"add_pair_{tag}",
        grid_spec=pltpu.PrefetchScalarGridSpec(
            num_scalar_prefetch=1, grid=(l, N_CHIPS),
            in_specs=[pl.BlockSpec((None, None, None, hr, cols), lambda i, j, s: (i, j, s[0], 0, 0)),
                      pl.BlockSpec(block, lambda i, j, s: (i, j, 0, 0))],
            out_specs=pl.BlockSpec(block, lambda i, j, s: (i, j, 0, 0))),
        out_shape=jax.ShapeDtypeStruct(theirs.shape, BF16),
        compiler_params=_params(("parallel", "parallel")),
    )(c.reshape(1), mine, theirs)


def _add_chips(part, received, mine, tag):
    l, _, hr, cols = received.shape

    def body(j_ref, part_ref, recv_ref, out_ref):
        acc = part_ref[...].astype(F32)
        for k in range(3):
            acc += recv_ref[k].astype(F32)
        out_ref[...] = acc

    return pl.pallas_call(
        body, name=f"add_chips_{tag}",
        grid_spec=pltpu.PrefetchScalarGridSpec(
            num_scalar_prefetch=1, grid=(l,),
            in_specs=[pl.BlockSpec((None, None, hr, cols), lambda i, s: (i, s[0], 0, 0)),
                      pl.BlockSpec((None, 3, hr, cols), lambda i, s: (i, 0, 0, 0))],
            out_specs=pl.BlockSpec((None, hr, cols), lambda i, s: (i, 0, 0))),
        out_shape=jax.ShapeDtypeStruct((l, hr, cols), F32),
        compiler_params=_params(("parallel",)),
    )(mine.reshape(1), part, received)


def _adam_update(w, g, m, v):
    m2 = ADAM_B1 * m + (1.0 - ADAM_B1) * g
    v2 = ADAM_B2 * v + (1.0 - ADAM_B2) * jnp.square(g)
    m_hat = m2 / (1.0 - ADAM_B1 ** ADAM_STEP)
    v_hat = v2 / (1.0 - ADAM_B2 ** ADAM_STEP)
    return -ADAM_LR * (m_hat / (jnp.sqrt(v_hat) + ADAM_EPS) + ADAM_WD * w), m2, v2


def _row_tile(rows, cols, limit):
    for cand in (rows, 512, 352, 256, 176, 128, 64, 32, 16, 8):
        if rows % cand == 0 and cand * cols * 4 <= limit:
            return cand
    return rows


def _adamw_halves(w, own, other, m, v, c, tag):
    _, hr, cols = w.shape
    tr = _row_tile(hr, cols, 1024 * 1024)

    def body(c_ref, w_ref, own_ref, other_ref, m_ref, v_ref, g_ref, d_ref, nm_ref, nv_ref):
        gv = jnp.where(pl.program_id(0) == c_ref[0], own_ref[...], other_ref[...])
        g_ref[...] = gv
        d_ref[...], nm_ref[...], nv_ref[...] = _adam_update(w_ref[...], gv, m_ref[...], v_ref[...])

    full = pl.BlockSpec((None, tr, cols), lambda h, i, s: (h, i, 0))
    half = pl.BlockSpec((tr, cols), lambda h, i, s: (i, 0))
    shape = jax.ShapeDtypeStruct((2, hr, cols), F32)
    return pl.pallas_call(
        body, name=f"adamw_{tag}",
        grid_spec=pltpu.PrefetchScalarGridSpec(num_scalar_prefetch=1, grid=(2, hr // tr),
                                               in_specs=[full, half, half, full, full], out_specs=[full] * 4),
        out_shape=[shape] * 4,
        compiler_params=_params(("parallel", "parallel")),
    )(c.reshape(1), w, own, other, m, v)


def _adamw_small(w, gathered, m, v):
    def body(w_ref, g_ref, m_ref, v_ref, sum_ref, d_ref, nm_ref, nv_ref):
        gv = g_ref[0]
        for i in range(1, g_ref.shape[0]):
            gv += g_ref[i]
        sum_ref[...] = gv
        d_ref[...], nm_ref[...], nv_ref[...] = _adam_update(w_ref[...], gv, m_ref[...], v_ref[...])

    shape = jax.ShapeDtypeStruct(w.shape, F32)
    return pl.pallas_call(body, name="adamw_small", out_shape=[shape] * 4, compiler_params=_params())(w, gathered, m, v)


VECTOR_PARAMS = ("ffn1_norm", "mix_norm", "hgrn_lb", "hgrn_onorm", "ffn2_norm", "ple_norm", "ple_post_norm", "final_norm",
                 "pool_scale")
ALL_PARAMS = ("ffn1_norm", "ffn1_w1", "ffn1_w3", "ffn1_w2", "mix_norm", "w_in", "hgrn_lb", "hgrn_onorm", "w_branch_a",
              "pool_w", "pool_scale", "w_branch_b", "w_out", "ffn2_norm", "ffn2_w1", "ffn2_w3", "ffn2_w2", "ple_norm",
              "ple_w_gate", "ple_w_proj", "ple_post_norm", "final_norm")
TILE_ROWS = 8


def _pack_small(values):
    parts = []
    for name in VECTOR_PARAMS:
        a = values[name].reshape(-1, values[name].shape[-1])
        parts.append(jnp.pad(a, ((0, TILE_ROWS - a.shape[0]), (0, D_MODEL - a.shape[1]))))
    parts.append(values["pool_w"].reshape(-1, D_MODEL))
    return jnp.concatenate(parts, axis=0)


def _unpack_small(pack, shapes):
    out = {}
    for i, name in enumerate(VECTOR_PARAMS):
        shape = shapes[name]
        rows = 1 if len(shape) == 1 else shape[0]
        out[name] = pack[i * TILE_ROWS:i * TILE_ROWS + rows, :shape[-1]].reshape(shape)
    out["pool_w"] = pack[len(VECTOR_PARAMS) * TILE_ROWS:].reshape(shapes["pool_w"])
    return out


def _halved(a, lead):
    return a.reshape(lead, 2, -1, a.shape[-1])


def _shard_halves(a, lead):
    return a.reshape(lead, N_CHIPS, 2, -1, a.shape[-1])


def _reduce_finish(names, own, other, w, m, v, core, out, transposed=False):
    for name, g_own, g_other in zip(names, own, other):
        shape = w[name].shape
        view = (lambda a: _halved(a[0].T, 1)[0]) if transposed else (lambda a: _halved(a, 1)[0])
        back = (lambda a: a.reshape(shape[2], shape[1]).T.reshape(shape)) if transposed else (lambda a: a.reshape(shape))
        g, d, nm, nv = _adamw_halves(view(w[name]), g_own, g_other, view(m[name]), view(v[name]), core, name)
        out["grad"][name], out["delta"][name], out["new_m"][name], out["new_v"][name] = (back(a) for a in (g, d, nm, nv))


def kernel(x, p, ffn1_norm, ffn1_w1, ffn1_w3, ffn1_w2, mix_norm, w_in, hgrn_lb, hgrn_onorm, w_branch_a, pool_w, pool_scale, w_branch_b, w_out, ffn2_norm, ffn2_w1, ffn2_w3, ffn2_w2, ple_norm, ple_w_gate, ple_w_proj, ple_post_norm, final_norm, loss_target, m_ffn1_norm, m_ffn1_w1, m_ffn1_w3, m_ffn1_w2, m_mix_norm, m_w_in, m_hgrn_lb, m_hgrn_onorm, m_w_branch_a, m_pool_w, m_pool_scale, m_w_branch_b, m_w_out, m_ffn2_norm, m_ffn2_w1, m_ffn2_w3, m_ffn2_w2, m_ple_norm, m_ple_w_gate, m_ple_w_proj, m_ple_post_norm, m_final_norm, v_ffn1_norm, v_ffn1_w1, v_ffn1_w3, v_ffn1_w2, v_mix_norm, v_w_in, v_hgrn_lb, v_hgrn_onorm, v_w_branch_a, v_pool_w, v_pool_scale, v_w_branch_b, v_w_out, v_ffn2_norm, v_ffn2_w1, v_ffn2_w3, v_ffn2_w2, v_ple_norm, v_ple_w_gate, v_ple_w_proj, v_ple_post_norm, v_final_norm):
    args = dict(locals())
    w = {name: args[name] for name in ALL_PARAMS}
    m = {name: args["m_" + name] for name in ALL_PARAMS}
    v = {name: args["v_" + name] for name in ALL_PARAMS}
    cx, cy, cc = _position()
    chip = (2 * cx + cy).astype(jnp.int32)
    core = cc.astype(jnp.int32)
    xs, ps, target = x[0], p[0, 0], loss_target[0]
    t = xs.shape[0]
    tm = min(256, t)
    tm_ffn = min(512, t)
    tt = min(512, t)
    tk = min(2048, t)
    small = {name: w[name] for name in VECTOR_PARAMS}
    small["final_norm"] = w["final_norm"].reshape(1, D_MODEL)
    pool_w0 = w["pool_w"][0]

    ffn_shard = lambda i: _halved(jnp.stack([w[f"ffn{i}_w1"][0].T, w[f"ffn{i}_w3"][0].T, w[f"ffn{i}_w2"][0]]).astype(BF16), 3)
    sq_shard = _halved(jnp.stack([w["w_branch_a"][0], w["w_out"][0], w["ple_w_gate"][0]]).astype(BF16), 3)
    win_shard, wb_shard, wp_shard = (_halved(w[n].astype(BF16), 1) for n in ("w_in", "w_branch_b", "ple_w_proj"))

    (ffn1w,) = _alone(_gather_rider([ffn_shard(1)], 0.5), "gather_ffn1")
    ffn1w = ffn1w.reshape(3, D_FF, D_MODEL)
    (x1, a1, b1), ((winw,),) = _ffn_fwd(xs, small["ffn1_norm"], ffn1w, 1, tm_ffn, [_gather_rider([win_shard], 0.6)])
    winw = winw.reshape(N_CHIPS, D_MODEL, SHARD_IN_COLS)
    (main, pool_r, gates), ((sqw, wbw, wpw),) = _mix_fwd(x1, small["mix_norm"], winw, tm,
                                                          [_gather_rider([sq_shard, wb_shard, wp_shard], 0.5)])
    sqw = sqw.reshape(3, D_MODEL, D_MODEL)
    wbw = wbw.reshape(N_CHIPS, POOL_WIDTH, -1)
    wpw = wpw.reshape(N_CHIPS, ps.shape[1], -1)
    (o, states), ((ffn2w,),) = _hgrn_fwd(main, small["hgrn_lb"], tt, [_gather_rider([ffn_shard(2)], 0.7)])
    ffn2w = ffn2w.reshape(3, D_FF, D_MODEL)
    (x2, ya, yb, pooled), _ = _post_fwd(o, main, pool_r, gates, x1, small["hgrn_onorm"], pool_w0, small["pool_scale"], sqw,
                                       wbw, tm)
    (x3, a2, b2), _ = _ffn_fwd(x2, small["ffn2_norm"], ffn2w, 2, tm_ffn)
    (dx3, loss, d_wg, d_wp, d_ple, d_post, d_final), _ = _tail(
        x3, ps, target, small["ple_norm"], small["ple_post_norm"], small["final_norm"], sqw, wpw, tm)
    loss = lax.psum(loss[0, 0], ("x", "y", "c"))

    (dx2, dab2, s2, h3, dxh2, d_ffn2_norm), _ = _ffn_bwd(dx3, x2, small["ffn2_norm"], a2, b2, ffn2w, 2, tm)
    (d_w13_2,), _ = _wgrad(dab2, h3, WGRAD_IN_BLOCKS, "wgrad_ffn2_in", tk)
    (d_w2_2,), _ = _wgrad(s2, dxh2, WGRAD_OUT_BLOCKS, "wgrad_ffn2_out", tk)

    wave1 = [_shard_halves(d_w13_2, 2), _shard_halves(d_w2_2, 1), _shard_halves(d_wg, 1), _shard_halves(d_wp, 1)]
    tags1 = ("w13_2", "w2_2", "wg", "wp")
    (do, dog, du, dgates, d_wa, d_wout, d_wb, d_pool_w, d_pool_scale, d_onorm), (sib1,) = _post_bwd(
        dx2, o, main, gates, ya, yb, pooled, small["hgrn_onorm"], pool_w0, small["pool_scale"], sqw, wbw, tm,
        [_pair_rider(wave1)])
    sums1 = [_add_pair(a, b, core, tag) for a, b, tag in zip(wave1, sib1, tags1)]
    (dqfi, d_lb), (got1,) = _hgrn_bwd(main, small["hgrn_lb"], states, do, tt, [_chips_rider(sums1)])
    own1 = [_add_chips(a, b, chip, tag) for a, b, tag in zip(sums1, got1, tags1)]
    (dx1, dproj, h2, d_mix_norm), (other1,) = _mix_bwd(dqfi, dog, du, dgates, dx2, x1, small["mix_norm"], winw, tm,
                                                        [_share_rider(own1)])
    (d_win,), _ = _wgrad_cols(h2, dproj, N_CHIPS, "wgrad_in", tk)

    wave2 = [_shard_halves(d_wa, 1), _shard_halves(d_wout, 1), _shard_halves(d_wb, 1), _shard_halves(d_win, 1)]
    tags2 = ("wa", "wout", "wb", "win")
    (dx, dab1, s1, h1, dxh1, d_ffn1_norm), (sib2,) = _ffn_bwd(dx1, xs, small["ffn1_norm"], a1, b1, ffn1w, 1, tm,
                                                               [_pair_rider(wave2)])
    sums2 = [_add_pair(a, b, core, tag) for a, b, tag in zip(wave2, sib2, tags2)]
    vecs = dict(ffn1_norm=d_ffn1_norm, mix_norm=d_mix_norm, hgrn_lb=d_lb, hgrn_onorm=d_onorm, ffn2_norm=d_ffn2_norm,
                ple_norm=d_ple, ple_post_norm=d_post, final_norm=d_final, pool_scale=d_pool_scale, pool_w=d_pool_w)
    (d_w13_1,), (got2, (small_all,)) = _wgrad(dab1, h1, WGRAD_IN_BLOCKS, "wgrad_ffn1_in", tk,
                                              [_chips_rider(sums2), _small_rider(_pack_small(vecs))])
    own2 = [_add_chips(a, b, chip, tag) for a, b, tag in zip(sums2, got2, tags2)]
    (d_w2_1,), (other2,) = _wgrad(s1, dxh1, WGRAD_OUT_BLOCKS, "wgrad_ffn1_out", tk, [_share_rider(own2)])

    wave3 = [_shard_halves(d_w13_1, 2), _shard_halves(d_w2_1, 1)]
    tags3 = ("w13_1", "w2_1")
    sib3 = _alone(_pair_rider(wave3), "pair_last")
    sums3 = [_add_pair(a, b, core, tag) for a, b, tag in zip(wave3, sib3, tags3)]
    got3 = _alone(_chips_rider(sums3), "chips_last")
    own3 = [_add_chips(a, b, chip, tag) for a, b, tag in zip(sums3, got3, tags3)]
    other3 = _alone(_share_rider(own3), "share_last")

    out = dict(grad={}, delta={}, new_m={}, new_v={})
    first = lambda arrays: [a[0] for a in arrays]
    _reduce_finish(("ffn2_w1", "ffn2_w3"), own1[0], other1[0], w, m, v, core, out, transposed=True)
    _reduce_finish(("ffn2_w2", "ple_w_gate", "ple_w_proj"), first(own1[1:]), first(other1[1:]), w, m, v, core, out)
    _reduce_finish(("w_branch_a", "w_out", "w_branch_b", "w_in"), first(own2), first(other2), w, m, v, core, out)
    _reduce_finish(("ffn1_w1", "ffn1_w3"), own3[0], other3[0], w, m, v, core, out, transposed=True)
    _reduce_finish(("ffn1_w2",), first(own3[1:]), first(other3[1:]), w, m, v, core, out)

    shapes = {name: w[name].shape for name in VECTOR_PARAMS + ("pool_w",)}
    results = _adamw_small(_pack_small(w), small_all, _pack_small(m), _pack_small(v))
    for key, pack in zip(("grad", "delta", "new_m", "new_v"), results):
        out[key].update(_unpack_small(pack, shapes))

    return (loss, dx[None], *[out["grad"][n] for n in ALL_PARAMS], *[out["delta"][n] for n in ALL_PARAMS],
            *[out["new_m"][n] for n in ALL_PARAMS], *[out["new_v"][n] for n in ALL_PARAMS])
```

```python
import functools

import jax
import jax.numpy as jnp
from jax import lax
from jax.experimental import pallas as pl
from jax.experimental.pallas import tpu as pltpu

F32 = jnp.float32
BF16 = jnp.bfloat16
MESH = pl.DeviceIdType.MESH

D_MODEL = 1024
D_FF = 2816
HEADS = 8
HEAD_DIM = 128
POOL_WIDTH = 512
POOL_WINDOWS = (2, 4, 8, 16)
POOL_HALO = 16
N_CHIPS = 4
EPS = 1e-6
CHUNK = 64
MAIN_COLS = 4096
GATE_COLS = 2048
SHARD_IN_COLS = 1664

ADAM_LR = 0.001
ADAM_B1 = 0.9
ADAM_B2 = 0.999
ADAM_EPS = 1e-08
ADAM_WD = 0.01
ADAM_STEP = 10

VMEM_LIMIT = 56 * 1024 * 1024
WGRAD_IN_BLOCKS = 4
WGRAD_OUT_BLOCKS = 2
ADAM_BLOCKS = 4


def _params(semantics=None, vmem=VMEM_LIMIT):
    return pltpu.CompilerParams(dimension_semantics=semantics, vmem_limit_bytes=vmem)


def _dot(a, b):
    return jnp.dot(a, b, preferred_element_type=F32)


def _dot_nt(a, b):
    return lax.dot_general(a, b, (((1,), (1,)), ((), ())), preferred_element_type=F32)


def _dot_tn(a, b):
    return lax.dot_general(a, b, (((0,), (0,)), ((), ())), preferred_element_type=F32)


def _tri_sum(tri, x):
    hi = x.astype(BF16)
    lo = (x - hi.astype(F32)).astype(BF16)
    return _dot(tri, hi) + _dot(tri, lo)


def _sigmoid(x):
    return jax.nn.sigmoid(x)


def _resident(shape):
    zeros = (0,) * len(shape)
    return pl.BlockSpec(shape, lambda *_: zeros, pipeline_mode=pl.Buffered(1))


def _pick(shape, k):
    zeros = (0,) * (len(shape) - 1)
    return pl.BlockSpec((None,) + tuple(shape[1:]), lambda *_: (k,) + zeros, pipeline_mode=pl.Buffered(1))


def _rows(tm, cols, col_block=0):
    return pl.BlockSpec((tm, cols), lambda i: (i, col_block))


def _acc(shape):
    zeros = (0,) * len(shape)
    return pl.BlockSpec(shape, lambda *_: zeros)


def _rms(x):
    r = lax.rsqrt(jnp.mean(x * x, axis=-1, keepdims=True) + EPS)
    return r, x * r


def _rms_bwd(dn, n, r):
    return r * (dn - n * jnp.mean(dn * n, axis=-1, keepdims=True))


def _colsum(a):
    return jnp.sum(a, axis=0, keepdims=True)


ANY = pl.BlockSpec(memory_space=pl.ANY)


class _Rider:
    def __init__(self, inputs, out_shape, sems, phases):
        self.inputs, self.out_shape, self.sems, self.phases = list(inputs), list(out_shape), list(sems), list(phases)


def _hosted(riders, body, *, name, grid=(), in_specs, out_specs, out_shape, scratch_shapes=(), compiler_params=None):
    riders = [r for r in riders if r is not None]
    n_in, n_out, n_scr = len(in_specs), len(out_shape), len(scratch_shapes)
    n_steps = 1
    for g in grid:
        n_steps *= g

    def wrapped(*refs):
        pos = n_in
        ins = refs[:n_in]
        r_ins = []
        for r in riders:
            r_ins.append(refs[pos:pos + len(r.inputs)])
            pos += len(r.inputs)
        outs = refs[pos:pos + n_out]
        pos += n_out
        r_outs = []
        for r in riders:
            r_outs.append(refs[pos:pos + len(r.out_shape)])
            pos += len(r.out_shape)
        scr = refs[pos:pos + n_scr]
        pos += n_scr
        r_sems = []
        for r in riders:
            r_sems.append(refs[pos:pos + len(r.sems)])
            pos += len(r.sems)
        step = 0
        for axis in range(len(grid)):
            step = step * grid[axis] + pl.program_id(axis)

        def at_step(which, fn):
            if n_steps == 1:
                fn()
            else:
                pl.when(step == which)(fn)

        for r, ri, ro, rs in zip(riders, r_ins, r_outs, r_sems):
            for fraction, fn in r.phases:
                if fraction == 0:
                    at_step(0, functools.partial(fn, ri, ro, rs))
        body(*ins, *outs, *scr)
        for r, ri, ro, rs in zip(riders, r_ins, r_outs, r_sems):
            for fraction, fn in r.phases:
                if fraction > 0:
                    at_step(min(int(fraction * n_steps), n_steps - 1), functools.partial(fn, ri, ro, rs))

    call = pl.pallas_call(
        wrapped, name=name, grid=grid,
        in_specs=list(in_specs) + [ANY for r in riders for _ in r.inputs],
        out_specs=list(out_specs) + [ANY for r in riders for _ in r.out_shape],
        out_shape=list(out_shape) + [s for r in riders for s in r.out_shape],
        scratch_shapes=list(scratch_shapes) + [s for r in riders for s in r.sems],
        compiler_params=compiler_params)

    def run(*args):
        res = call(*args, *[a for r in riders for a in r.inputs])
        extras, pos = [], n_out
        for r in riders:
            extras.append(list(res[pos:pos + len(r.out_shape)]))
            pos += len(r.out_shape)
        return list(res[:n_out]), extras

    return run


def _ffn_fwd(x, g, ffnw, tag, tm, riders=()):
    t = x.shape[0]

    def body(x_ref, g_ref, w1_ref, w3_ref, w2_ref, xo_ref, a_ref, b_ref):
        xv = x_ref[...]
        _, n = _rms(xv)
        h = (n * g_ref[...]).astype(BF16)
        a = _dot_nt(h, w1_ref[...])
        b = _dot_nt(h, w3_ref[...])
        s = (a * _sigmoid(a) * b).astype(BF16)
        xo_ref[...] = xv + 0.5 * _dot(s, w2_ref[...])
        a_ref[...] = a.astype(BF16)
        b_ref[...] = b.astype(BF16)

    return _hosted(
        riders, body, name=f"ffn_fwd_{tag}", grid=(t // tm,),
        in_specs=[_rows(tm, D_MODEL), _resident((1, D_MODEL)), _pick(ffnw.shape, 0), _pick(ffnw.shape, 1),
                  _pick(ffnw.shape, 2)],
        out_specs=[_rows(tm, D_MODEL), _rows(tm, D_FF), _rows(tm, D_FF)],
        out_shape=[jax.ShapeDtypeStruct((t, D_MODEL), F32), jax.ShapeDtypeStruct((t, D_FF), BF16),
                   jax.ShapeDtypeStruct((t, D_FF), BF16)],
        compiler_params=_params(("arbitrary",)),
    )(x, g, ffnw, ffnw, ffnw)


def _ffn_bwd(dxo, x, g, a, b, ffnw, tag, tm, riders=()):
    t = x.shape[0]

    def body(dxo_ref, x_ref, g_ref, a_ref, b_ref, w1_ref, w3_ref, w2_ref, dx_ref, dab_ref, s_ref, h_ref, dxh_ref, dg_ref):
        @pl.when(pl.program_id(0) == 0)
        def _():
            dg_ref[...] = jnp.zeros_like(dg_ref)

        xv = x_ref[...]
        gv = g_ref[...]
        r, n = _rms(xv)
        h_ref[...] = (n * gv).astype(BF16)
        dxo_v = dxo_ref[...]
        dxh = (0.5 * dxo_v).astype(BF16)
        dxh_ref[...] = dxh
        ds = _dot_nt(dxh, w2_ref[...])
        av = a_ref[...].astype(F32)
        bv = b_ref[...].astype(F32)
        sg = _sigmoid(av)
        silu = av * sg
        s_ref[...] = (silu * bv).astype(BF16)
        da = (ds * bv * (sg * (1.0 + av * (1.0 - sg)))).astype(BF16)
        db = (ds * silu).astype(BF16)
        dab_ref[:, :D_FF] = da
        dab_ref[:, D_FF:] = db
        dh = _dot(da, w1_ref[...]) + _dot(db, w3_ref[...])
        dg_ref[...] += _colsum(dh * n)
        dx_ref[...] = dxo_v + _rms_bwd(dh * gv, n, r)

    return _hosted(
        riders, body, name=f"ffn_bwd_{tag}", grid=(t // tm,),
        in_specs=[_rows(tm, D_MODEL), _rows(tm, D_MODEL), _resident((1, D_MODEL)), _rows(tm, D_FF), _rows(tm, D_FF),
                  _pick(ffnw.shape, 0), _pick(ffnw.shape, 1), _pick(ffnw.shape, 2)],
        out_specs=[_rows(tm, D_MODEL), _rows(tm, 2 * D_FF), _rows(tm, D_FF), _rows(tm, D_MODEL), _rows(tm, D_MODEL),
                   _acc((1, D_MODEL))],
        out_shape=[jax.ShapeDtypeStruct((t, D_MODEL), F32), jax.ShapeDtypeStruct((t, 2 * D_FF), BF16),
                   jax.ShapeDtypeStruct((t, D_FF), BF16), jax.ShapeDtypeStruct((t, D_MODEL), BF16),
                   jax.ShapeDtypeStruct((t, D_MODEL), BF16), jax.ShapeDtypeStruct((1, D_MODEL), F32)],
        compiler_params=_params(("arbitrary",)),
    )(dxo, x, g, a, b, ffnw, ffnw, ffnw)


def _wgrad(xm, dy, out_blocks, name, tk, riders=()):
    t, m = xm.shape
    n = dy.shape[1]
    mb = m // out_blocks

    def body(x_ref, dy_ref, o_ref):
        @pl.when(pl.program_id(1) == 0)
        def _():
            o_ref[...] = jnp.zeros_like(o_ref)

        o_ref[...] += _dot_tn(x_ref[...], dy_ref[...])

    return _hosted(
        riders, body, name=name, grid=(out_blocks, t // tk),
        in_specs=[pl.BlockSpec((tk, mb), lambda j, k: (k, j)), pl.BlockSpec((tk, n), lambda j, k: (k, 0))],
        out_specs=[pl.BlockSpec((None, mb, n), lambda j, k: (j, 0, 0))],
        out_shape=[jax.ShapeDtypeStruct((out_blocks, mb, n), F32)],
        compiler_params=_params(("arbitrary", "arbitrary")),
    )(xm, dy)


def _wgrad_cols(xm, dy, out_blocks, name, tk, riders=()):
    t, m = xm.shape
    n = dy.shape[1]
    nb = n // out_blocks

    def body(x_ref, dy_ref, o_ref):
        @pl.when(pl.program_id(1) == 0)
        def _():
            o_ref[...] = jnp.zeros_like(o_ref)

        o_ref[...] += _dot_tn(x_ref[...], dy_ref[...])

    return _hosted(
        riders, body, name=name, grid=(out_blocks, t // tk),
        in_specs=[pl.BlockSpec((tk, m), lambda j, k: (k, 0)), pl.BlockSpec((tk, nb), lambda j, k: (k, j))],
        out_specs=[pl.BlockSpec((None, m, nb), lambda j, k: (j, 0, 0))],
        out_shape=[jax.ShapeDtypeStruct((out_blocks, m, nb), F32)],
        compiler_params=_params(("arbitrary", "arbitrary")),
    )(xm, dy)


def _mix_fwd(x1, g, winw, tm, riders=()):
    t = x1.shape[0]

    def body(x_ref, g_ref, w_ref, main_ref, pool_ref, gate_ref):
        _, n = _rms(x_ref[...])
        h = (n * g_ref[...]).astype(BF16)
        proj = jnp.concatenate([_dot(h, w_ref[j]) for j in range(N_CHIPS)], axis=1)
        main_ref[...] = proj[:, :MAIN_COLS]
        pool_ref[...] = proj[:, MAIN_COLS:MAIN_COLS + POOL_WIDTH]
        gate_ref[...] = proj[:, MAIN_COLS + POOL_WIDTH:]

    return _hosted(
        riders, body, name="mix_fwd", grid=(t // tm,),
        in_specs=[_rows(tm, D_MODEL), _resident((1, D_MODEL)), _resident(winw.shape)],
        out_specs=[_rows(tm, MAIN_COLS), _rows(tm, POOL_WIDTH), _rows(tm, GATE_COLS)],
        out_shape=[jax.ShapeDtypeStruct((t, MAIN_COLS), F32), jax.ShapeDtypeStruct((t, POOL_WIDTH), F32),
                   jax.ShapeDtypeStruct((t, GATE_COLS), F32)],
        compiler_params=_params(("arbitrary",)),
    )(x1, g, winw)


def _mix_bwd(dqfi, dog, du, dgates, dx2, x1, g, winw, tm, riders=()):
    t = x1.shape[0]
    cols = N_CHIPS * SHARD_IN_COLS

    def body(dqfi_ref, dog_ref, du_ref, dgt_ref, dx2_ref, x_ref, g_ref, w_ref, dx_ref, dproj_ref, h_ref, dg_ref):
        @pl.when(pl.program_id(0) == 0)
        def _():
            dg_ref[...] = jnp.zeros_like(dg_ref)

        dproj = jnp.concatenate([dqfi_ref[...], dog_ref[...], du_ref[...], dgt_ref[...]], axis=1)
        dproj_ref[...] = dproj
        dh = _dot_nt(dproj[:, :SHARD_IN_COLS], w_ref[0])
        for j in range(1, N_CHIPS):
            dh += _dot_nt(dproj[:, j * SHARD_IN_COLS:(j + 1) * SHARD_IN_COLS], w_ref[j])
        gv = g_ref[...]
        r, n = _rms(x_ref[...])
        h_ref[...] = (n * gv).astype(BF16)
        dg_ref[...] += _colsum(dh * n)
        dx_ref[...] = dx2_ref[...] + _rms_bwd(dh * gv, n, r)

    return _hosted(
        riders, body, name="mix_bwd", grid=(t // tm,),
        in_specs=[_rows(tm, 3 * D_MODEL), _rows(tm, D_MODEL), _rows(tm, POOL_WIDTH), _rows(tm, GATE_COLS),
                  _rows(tm, D_MODEL), _rows(tm, D_MODEL), _resident((1, D_MODEL)), _resident(winw.shape)],
        out_specs=[_rows(tm, D_MODEL), _rows(tm, cols), _rows(tm, D_MODEL), _acc((1, D_MODEL))],
        out_shape=[jax.ShapeDtypeStruct((t, D_MODEL), F32), jax.ShapeDtypeStruct((t, cols), BF16),
                   jax.ShapeDtypeStruct((t, D_MODEL), BF16), jax.ShapeDtypeStruct((1, D_MODEL), F32)],
        compiler_params=_params(("arbitrary",)),
    )(dqfi, dog, du, dgates, dx2, x1, g, winw)


def _lower_bound(lb_raw):
    l0 = lb_raw[0:1, :]
    l1 = lb_raw[1:2, :]
    m = jnp.maximum(l0, l1)
    e0 = jnp.exp(l0 - m)
    e1 = jnp.exp(l1 - m)
    return e0 / (e0 + e1)


def _head_slices():
    return [slice(h * HEAD_DIM, (h + 1) * HEAD_DIM) for h in range(HEADS)]


def _gates(qr, fr, lb, tril_b, first_half):
    sg = _sigmoid(fr)
    f = lb + (1.0 - lb) * sg
    k = 1.0 - f
    sq = _sigmoid(qr)
    q = qr * sq
    log_f = jnp.log(f)
    gc = _tri_sum(tril_b, log_f)
    gm = _colsum(jnp.where(first_half, log_f, 0.0))
    gl = _colsum(log_f)
    e_q = jnp.exp(gc - gm)
    e_k = jnp.exp(gm - gc)
    e_in = jnp.exp(gc)
    e_out = jnp.exp(gl - gc)
    return dict(sg=sg, f=f, k=k, sq=sq, q=q, e_q=e_q, e_k=e_k, e_in=e_in, e_out=e_out, e_last=jnp.exp(gl))


def _hgrn_fwd(main, lb_raw, tt, riders=()):
    t = main.shape[0]
    n_local = tt // CHUNK

    def body(q_ref, f_ref, i_ref, lb_ref, o_ref, st_ref, s_scr):
        @pl.when(pl.program_id(0) == 0)
        def _():
            s_scr[...] = jnp.zeros_like(s_scr)

        lb = _lower_bound(lb_ref[...])
        row = lax.broadcasted_iota(jnp.int32, (CHUNK, CHUNK), 0)
        col = lax.broadcasted_iota(jnp.int32, (CHUNK, CHUNK), 1)
        tril = row >= col
        tril_b = tril.astype(BF16)
        first_half = lax.broadcasted_iota(jnp.int32, (CHUNK, D_MODEL), 0) < CHUNK // 2
        heads = _head_slices()

        def chunk(c, carry):
            rows = pl.ds(pl.multiple_of(c * CHUNK, CHUNK), CHUNK)
            z = _gates(q_ref[rows, :], f_ref[rows, :], lb, tril_b, first_half)
            qt = (z["q"] * z["e_q"]).astype(BF16)
            kt = (z["k"] * z["e_k"]).astype(BF16)
            qg = (z["q"] * z["e_in"]).astype(BF16)
            kg = (z["k"] * z["e_out"]).astype(BF16)
            vb = i_ref[rows, :].astype(BF16)
            states = [s_scr[h] for h in range(HEADS)]
            for h in range(HEADS):
                st_ref[c, h] = states[h]
            raw = [_dot_nt(qt[:, sl], kt[:, sl]) for sl in heads]
            inter = [_dot_nt(qg[:, sl], states[h].astype(BF16)) for h, sl in enumerate(heads)]
            grown = [_dot_tn(vb[:, sl], kg[:, sl]) for sl in heads]
            scores = [jnp.where(tril, r, 0.0).astype(BF16) for r in raw]
            for h, sl in enumerate(heads):
                s_scr[h] = states[h] * z["e_last"][:, sl] + grown[h]
            o_ref[rows, :] = jnp.concatenate([_dot(scores[h], vb[:, sl]) + inter[h] for h, sl in enumerate(heads)], axis=1)
            return carry

        lax.fori_loop(0, n_local, chunk, 0, unroll=2)

    return _hosted(
        riders, body, name="hgrn_fwd", grid=(t // tt,),
        in_specs=[_rows(tt, D_MODEL, 0), _rows(tt, D_MODEL, 1), _rows(tt, D_MODEL, 2), _resident((2, D_MODEL))],
        out_specs=[_rows(tt, D_MODEL),
                   pl.BlockSpec((n_local, HEADS, HEAD_DIM, HEAD_DIM), lambda i: (i, 0, 0, 0))],
        out_shape=[jax.ShapeDtypeStruct((t, D_MODEL), F32),
                   jax.ShapeDtypeStruct((t // CHUNK, HEADS, HEAD_DIM, HEAD_DIM), F32)],
        scratch_shapes=[pltpu.VMEM((HEADS, HEAD_DIM, HEAD_DIM), F32)],
        compiler_params=_params(("arbitrary",)),
    )(main, main, main, lb_raw)


def _hgrn_bwd(main, lb_raw, states, do, tt, riders=()):
    t = main.shape[0]
    n_tiles = t // tt
    n_local = tt // CHUNK

    def rev(col_block):
        return pl.BlockSpec((tt, D_MODEL), lambda i: (n_tiles - 1 - i, col_block))

    def body(q_ref, f_ref, i_ref, lb_ref, st_ref, do_ref, dqfi_ref, dlb_ref, ds_scr, acc_scr):
        @pl.when(pl.program_id(0) == 0)
        def _():
            ds_scr[...] = jnp.zeros_like(ds_scr)
            acc_scr[...] = jnp.zeros_like(acc_scr)

        lb = _lower_bound(lb_ref[...])
        row = lax.broadcasted_iota(jnp.int32, (CHUNK, CHUNK), 0)
        col = lax.broadcasted_iota(jnp.int32, (CHUNK, CHUNK), 1)
        tril = row >= col
        tril_b = tril.astype(BF16)
        triu_b = (row <= col).astype(BF16)
        first_half = lax.broadcasted_iota(jnp.int32, (CHUNK, D_MODEL), 0) < CHUNK // 2
        heads = _head_slices()
        cat = functools.partial(jnp.concatenate, axis=1)

        def chunk(cc, carry):
            c = n_local - 1 - cc
            rows = pl.ds(pl.multiple_of(c * CHUNK, CHUNK), CHUNK)
            qr = q_ref[rows, :]
            z = _gates(qr, f_ref[rows, :], lb, tril_b, first_half)
            qt = (z["q"] * z["e_q"]).astype(BF16)
            kt = (z["k"] * z["e_k"]).astype(BF16)
            qg_f = z["q"] * z["e_in"]
            qg = qg_f.astype(BF16)
            kg_f = z["k"] * z["e_out"]
            kg = kg_f.astype(BF16)
            vb = i_ref[rows, :].astype(BF16)
            dob = do_ref[rows, :].astype(BF16)
            st = [st_ref[c, h] for h in range(HEADS)]
            dst = [ds_scr[h] for h in range(HEADS)]
            dst_b = [d.astype(BF16) for d in dst]
            raw = [_dot_nt(qt[:, sl], kt[:, sl]) for sl in heads]
            draw = [_dot_nt(dob[:, sl], vb[:, sl]) for sl in heads]
            dqg = [_dot(dob[:, sl], st[h].astype(BF16)) for h, sl in enumerate(heads)]
            dkg = [_dot(vb[:, sl], dst_b[h]) for h, sl in enumerate(heads)]
            dv_inter = [_dot_nt(kg[:, sl], dst_b[h]) for h, sl in enumerate(heads)]
            grown = [_dot_tn(dob[:, sl], qg[:, sl]) for sl in heads]
            scores = [jnp.where(tril, r, 0.0).astype(BF16) for r in raw]
            dscores = [jnp.where(tril, r, 0.0).astype(BF16) for r in draw]
            dqt = [_dot(dscores[h], kt[:, sl]) for h, sl in enumerate(heads)]
            dkt = [_dot_tn(dscores[h], qt[:, sl]) for h, sl in enumerate(heads)]
            dv = [_dot_tn(scores[h], dob[:, sl]) + dv_inter[h] for h, sl in enumerate(heads)]
            carry_in = cat([z["e_last"][:, sl] * _colsum(dst[h] * st[h]) for h, sl in enumerate(heads)])
            for h, sl in enumerate(heads):
                ds_scr[h] = dst[h] * z["e_last"][:, sl] + grown[h]
            dqt, dkt, dqg, dkg = cat(dqt), cat(dkt), cat(dqg), cat(dkg)
            carry_in += _colsum(dkg * kg_f)
            dq = dqt * z["e_q"] + dqg * z["e_in"]
            dk = dkt * z["e_k"] + dkg * z["e_out"]
            dgate = (qt.astype(F32) * dqt - kt.astype(F32) * dkt) + (qg_f * dqg - kg_f * dkg)
            dlogf = _tri_sum(triu_b, dgate) + carry_in
            df = dlogf / z["f"] - dk
            sg = z["sg"]
            sq = z["sq"]
            acc_scr[...] += _colsum(df * (1.0 - sg))
            dqfi_ref[rows, 0:D_MODEL] = (dq * (sq * (1.0 + qr * (1.0 - sq)))).astype(BF16)
            dqfi_ref[rows, D_MODEL:2 * D_MODEL] = (df * (1.0 - lb) * sg * (1.0 - sg)).astype(BF16)
            dqfi_ref[rows, 2 * D_MODEL:3 * D_MODEL] = cat(dv).astype(BF16)
            return carry

        lax.fori_loop(0, n_local, chunk, 0, unroll=2)
        d0 = acc_scr[...] * lb * (1.0 - lb)
        dlb_ref[0:1, :] = d0
        dlb_ref[1:2, :] = -d0

    return _hosted(
        riders, body, name="hgrn_bwd", grid=(n_tiles,),
        in_specs=[rev(0), rev(1), rev(2), _resident((2, D_MODEL)),
                  pl.BlockSpec((n_local, HEADS, HEAD_DIM, HEAD_DIM), lambda i: (n_tiles - 1 - i, 0, 0, 0)),
                  rev(0)],
        out_specs=[pl.BlockSpec((tt, 3 * D_MODEL), lambda i: (n_tiles - 1 - i, 0)), _acc((2, D_MODEL))],
        out_shape=[jax.ShapeDtypeStruct((t, 3 * D_MODEL), BF16), jax.ShapeDtypeStruct((2, D_MODEL), F32)],
        scratch_shapes=[pltpu.VMEM((HEADS, HEAD_DIM, HEAD_DIM), F32), pltpu.VMEM((1, D_MODEL), F32)],
        compiler_params=_params(("arbitrary",)),
    )(main, main, main, lb_raw, states, do)


def _head_norm(o):
    rs, ns = [], []
    for h in range(HEADS):
        oh = o[:, h * HEAD_DIM:(h + 1) * HEAD_DIM]
        r, n = _rms(oh)
        rs.append(jnp.broadcast_to(r, oh.shape))
        ns.append(n)
    return jnp.concatenate(rs, axis=1), jnp.concatenate(ns, axis=1)


def _head_norm_bwd(dn, n, r):
    outs = []
    for h in range(HEADS):
        sl = slice(h * HEAD_DIM, (h + 1) * HEAD_DIM)
        outs.append(_rms_bwd(dn[:, sl], n[:, sl], r[:, sl]))
    return jnp.concatenate(outs, axis=1)


def _window_counts(first_row, tm):
    pos = (first_row + 1 + lax.broadcasted_iota(jnp.int32, (tm, 1), 0)).astype(F32)
    return [jnp.minimum(pos, float(w)) for w in POOL_WINDOWS]


def _post_fwd(o, main, pool_r, gates, x1, onorm, pool_w, pool_scale, sqw, wbw, tm, riders=()):
    t = o.shape[0]
    ext_rows = tm + POOL_HALO

    def body(o_ref, og_ref, u_ref, gt_ref, x1_ref, on_ref, pw_ref, ps_ref, wa_ref, wout_ref, wb_ref,
             x2_ref, ya_ref, yb_ref, pooled_ref, ext):
        i = pl.program_id(0)

        @pl.when(i == 0)
        def _():
            ext[0:POOL_HALO, :] = jnp.zeros((POOL_HALO, POOL_WIDTH), F32)

        _, n = _head_norm(o_ref[...])
        og = og_ref[...]
        oa = (n * on_ref[...] * (og * _sigmoid(og))).astype(BF16)
        ya = _dot(oa, wa_ref[...])

        u = u_ref[...]
        ext[POOL_HALO:ext_rows, :] = u
        e = ext[...]
        counts = _window_counts(i * tm, tm)
        pooled = []
        for gidx, w in enumerate(POOL_WINDOWS):
            s = e[:, gidx * HEAD_DIM:(gidx + 1) * HEAD_DIM]
            shift = 1
            while shift < w:
                s = s + pltpu.roll(s, shift, axis=0)
                shift *= 2
            pooled.append(s[POOL_HALO:, :] / counts[gidx] - u[:, gidx * HEAD_DIM:(gidx + 1) * HEAD_DIM])
        ext[0:POOL_HALO, :] = ext[tm:ext_rows, :]
        pooled_b = [pg.astype(BF16) for pg in pooled]
        pooled_ref[...] = jnp.concatenate(pooled_b, axis=1)
        mixed = jnp.concatenate([_dot(pooled_b[gidx], pw_ref[gidx].astype(BF16)) for gidx in range(len(POOL_WINDOWS))],
                                axis=1) * ps_ref[...]
        mixed_b = mixed.astype(BF16)
        yb = jnp.concatenate([_dot(mixed_b, wb_ref[j]) for j in range(N_CHIPS)], axis=1)

        gt = gt_ref[...]
        y = _sigmoid(gt[:, :D_MODEL]) * ya + _sigmoid(gt[:, D_MODEL:]) * yb
        x2_ref[...] = x1_ref[...] + _dot(y.astype(BF16), wout_ref[...])
        ya_ref[...] = ya.astype(BF16)
        yb_ref[...] = yb.astype(BF16)

    return _hosted(
        riders, body, name="post_fwd", grid=(t // tm,),
        in_specs=[_rows(tm, D_MODEL), _rows(tm, D_MODEL, 3), _rows(tm, POOL_WIDTH), _rows(tm, GATE_COLS), _rows(tm, D_MODEL),
                  _resident((1, D_MODEL)), _resident(pool_w.shape), _resident((1, POOL_WIDTH)),
                  _pick(sqw.shape, 0), _pick(sqw.shape, 1), _resident(wbw.shape)],
        out_specs=[_rows(tm, D_MODEL), _rows(tm, D_MODEL), _rows(tm, D_MODEL), _rows(tm, POOL_WIDTH)],
        out_shape=[jax.ShapeDtypeStruct((t, D_MODEL), F32), jax.ShapeDtypeStruct((t, D_MODEL), BF16),
                   jax.ShapeDtypeStruct((t, D_MODEL), BF16), jax.ShapeDtypeStruct((t, POOL_WIDTH), BF16)],
        scratch_shapes=[pltpu.VMEM((ext_rows, POOL_WIDTH), F32)],
        compiler_params=_params(("arbitrary",)),
    )(o, main, pool_r, gates, x1, onorm, pool_w, pool_scale, sqw, sqw, wbw)


def _post_bwd(dx2, o, main, gates, ya, yb, pooled, onorm, pool_w, pool_scale, sqw, wbw, tm, riders=()):
    t = o.shape[0]
    n_tiles = t // tm
    ext_rows = tm + POOL_HALO
    n_groups = len(POOL_WINDOWS)

    def rev(cols, col_block=0):
        return pl.BlockSpec((tm, cols), lambda i: (n_tiles - 1 - i, col_block))

    def body(dx2_ref, o_ref, og_ref, gt_ref, ya_ref, yb_ref, pooled_ref, on_ref, pw_ref, ps_ref, wa_ref, wout_ref, wb_ref,
             do_ref, dog_ref, du_ref, dgt_ref, dwa_ref, dwout_ref, dwb_ref, dpw_ref, dps_ref, don_ref, ext):
        i = pl.program_id(0)

        @pl.when(i == 0)
        def _():
            ext[tm:ext_rows, :] = jnp.zeros((POOL_HALO, POOL_WIDTH), F32)
            for ref in (dwa_ref, dwout_ref, dwb_ref, dpw_ref, dps_ref, don_ref):
                ref[...] = jnp.zeros_like(ref)

        dx2b = dx2_ref[...].astype(BF16)
        dy = _dot_nt(dx2b, wout_ref[...])
        gt = gt_ref[...]
        sga = _sigmoid(gt[:, :D_MODEL])
        sgb = _sigmoid(gt[:, D_MODEL:])
        ya = ya_ref[...].astype(F32)
        yb = yb_ref[...].astype(F32)
        y = (sga * ya + sgb * yb).astype(BF16)
        dwout_ref[...] += _dot_tn(y, dx2b)
        dya = (dy * sga).astype(BF16)
        dyb = (dy * sgb).astype(BF16)
        dgt_ref[:, :D_MODEL] = (dy * ya * sga * (1.0 - sga)).astype(BF16)
        dgt_ref[:, D_MODEL:] = (dy * yb * sgb * (1.0 - sgb)).astype(BF16)

        r, n = _head_norm(o_ref[...])
        onv = on_ref[...]
        og = og_ref[...]
        sog = _sigmoid(og)
        silu_og = og * sog
        normed = n * onv
        dwa_ref[...] += _dot_tn((normed * silu_og).astype(BF16), dya)
        doa = _dot_nt(dya, wa_ref[...])
        dog_ref[...] = (doa * normed * (sog * (1.0 + og * (1.0 - sog)))).astype(BF16)
        dnormed = doa * silu_og
        don_ref[...] += _colsum(dnormed * n)
        do_ref[...] = _head_norm_bwd(dnormed * onv, n, r)

        psv = ps_ref[...]
        dmixed = _dot_nt(dyb[:, :256], wb_ref[0])
        for j in range(1, N_CHIPS):
            dmixed += _dot_nt(dyb[:, j * 256:(j + 1) * 256], wb_ref[j])
        pooled_b = pooled_ref[...]
        pm = jnp.concatenate(
            [_dot(pooled_b[:, gidx * HEAD_DIM:(gidx + 1) * HEAD_DIM], pw_ref[gidx].astype(BF16)) for gidx in range(n_groups)],
            axis=1)
        mixed_b = (pm * psv).astype(BF16)
        for j in range(N_CHIPS):
            dwb_ref[j] += _dot_tn(mixed_b, dyb[:, j * 256:(j + 1) * 256])
        dps_ref[...] += _colsum(dmixed * pm)
        dpm = (dmixed * psv).astype(BF16)
        counts = _window_counts((n_tiles - 1 - i) * tm, tm)
        dpooled = []
        for gidx in range(n_groups):
            sl = slice(gidx * HEAD_DIM, (gidx + 1) * HEAD_DIM)
            dpw_ref[gidx] += _dot_tn(pooled_b[:, sl], dpm[:, sl])
            dpooled.append(_dot_nt(dpm[:, sl], pw_ref[gidx].astype(BF16)))
        ext[0:tm, :] = jnp.concatenate([dpooled[gidx] / counts[gidx] for gidx in range(n_groups)], axis=1)
        e = ext[...]
        du = []
        for gidx, w in enumerate(POOL_WINDOWS):
            s = e[:, gidx * HEAD_DIM:(gidx + 1) * HEAD_DIM]
            shift = 1
            while shift < w:
                s = s + pltpu.roll(s, ext_rows - shift, axis=0)
                shift *= 2
            du.append(s[:tm, :] - dpooled[gidx])
        ext[tm:ext_rows, :] = ext[0:POOL_HALO, :]
        du_ref[...] = jnp.concatenate(du, axis=1).astype(BF16)

    wa_shape = (D_MODEL, D_MODEL)
    return _hosted(
        riders, body, name="post_bwd", grid=(n_tiles,),
        in_specs=[rev(D_MODEL), rev(D_MODEL), rev(D_MODEL, 3), rev(GATE_COLS), rev(D_MODEL), rev(D_MODEL), rev(POOL_WIDTH),
                  _resident((1, D_MODEL)), _resident(pool_w.shape), _resident((1, POOL_WIDTH)),
                  _pick(sqw.shape, 0), _pick(sqw.shape, 1), _resident(wbw.shape)],
        out_specs=[rev(D_MODEL), rev(D_MODEL), rev(POOL_WIDTH), rev(GATE_COLS),
                   _acc(wa_shape), _acc(wa_shape), _acc(wbw.shape), _acc(pool_w.shape), _acc((1, POOL_WIDTH)),
                   _acc((1, D_MODEL))],
        out_shape=[jax.ShapeDtypeStruct((t, D_MODEL), F32), jax.ShapeDtypeStruct((t, D_MODEL), BF16),
                   jax.ShapeDtypeStruct((t, POOL_WIDTH), BF16), jax.ShapeDtypeStruct((t, GATE_COLS), BF16),
                   jax.ShapeDtypeStruct(wa_shape, F32), jax.ShapeDtypeStruct(wa_shape, F32),
                   jax.ShapeDtypeStruct(wbw.shape, F32), jax.ShapeDtypeStruct(pool_w.shape, F32),
                   jax.ShapeDtypeStruct((1, POOL_WIDTH), F32), jax.ShapeDtypeStruct((1, D_MODEL), F32)],
        scratch_shapes=[pltpu.VMEM((ext_rows, POOL_WIDTH), F32)],
        compiler_params=_params(("arbitrary",)),
    )(dx2, o, main, gates, ya, yb, pooled, onorm, pool_w, pool_scale, sqw, sqw, wbw)


def _tail(x3, p, target, g_ple, g_post, g_final, sqw, wpw, tm, riders=()):
    t = x3.shape[0]
    pd = p.shape[1]

    def body(x_ref, p_ref, tg_ref, g4_ref, g5_ref, g6_ref, wg_ref, wp_ref,
             dx_ref, loss_ref, dwg_ref, dwp_ref, dg4_ref, dg5_ref, dg6_ref):
        @pl.when(pl.program_id(0) == 0)
        def _():
            for ref in (loss_ref, dwg_ref, dwp_ref, dg4_ref, dg5_ref, dg6_ref):
                ref[...] = jnp.zeros_like(ref)

        x3v = x_ref[...]
        g4, g5, g6 = g4_ref[...], g5_ref[...], g6_ref[...]
        r4, n4 = _rms(x3v)
        h4 = (n4 * g4).astype(BF16)
        gate = _sigmoid(_dot(h4, wg_ref[...]))
        pb = p_ref[...].astype(BF16)
        r5, n5 = _rms(jnp.concatenate([_dot(pb, wp_ref[j]) for j in range(N_CHIPS)], axis=1))
        emb = n5 * g5
        r6, n6 = _rms(x3v + gate * emb)
        diff = n6 * g6 - tg_ref[...]
        loss_ref[...] += 0.5 * jnp.sum(jnp.mean(diff * diff, axis=-1, keepdims=True), axis=0, keepdims=True)
        dout = diff * (1.0 / D_MODEL)
        dg6_ref[...] += _colsum(dout * n6)
        dx4 = _rms_bwd(dout * g6, n6, r6)
        demb = dx4 * gate
        dg5_ref[...] += _colsum(demb * n5)
        dpre = _rms_bwd(demb * g5, n5, r5).astype(BF16)
        for j in range(N_CHIPS):
            dwp_ref[j] += _dot_tn(pb, dpre[:, j * pd:(j + 1) * pd])
        dz = (dx4 * emb * gate * (1.0 - gate)).astype(BF16)
        dwg_ref[...] += _dot_tn(h4, dz)
        dh4 = _dot_nt(dz, wg_ref[...])
        dg4_ref[...] += _colsum(dh4 * n4)
        dx_ref[...] = dx4 + _rms_bwd(dh4 * g4, n4, r4)

    sq_shape = (D_MODEL, D_MODEL)
    vec = (1, D_MODEL)
    return _hosted(
        riders, body, name="tail", grid=(t // tm,),
        in_specs=[_rows(tm, D_MODEL), _rows(tm, pd), _rows(tm, D_MODEL), _resident(vec), _resident(vec), _resident(vec),
                  _pick(sqw.shape, 2), _resident(wpw.shape)],
        out_specs=[_rows(tm, D_MODEL), _acc((1, 1)), _acc(sq_shape), _acc(wpw.shape), _acc(vec), _acc(vec), _acc(vec)],
        out_shape=[jax.ShapeDtypeStruct((t, D_MODEL), F32), jax.ShapeDtypeStruct((1, 1), F32),
                   jax.ShapeDtypeStruct(sq_shape, F32), jax.ShapeDtypeStruct(wpw.shape, F32),
                   jax.ShapeDtypeStruct(vec, F32), jax.ShapeDtypeStruct(vec, F32), jax.ShapeDtypeStruct(vec, F32)],
        compiler_params=_params(("arbitrary",)),
    )(x3, p, target, g_ple, g_post, g_final, sqw, wpw)


def _position():
    return lax.axis_index("x"), lax.axis_index("y"), lax.axis_index("c")


def _other_chips(x, y):
    return [(1 - x, y), (x, 1 - y), (1 - x, 1 - y)]


def _remote(src, dst, send_sems, recv_sems, k, device):
    return pltpu.make_async_remote_copy(src_ref=src, dst_ref=dst, send_sem=send_sems.at[k], recv_sem=recv_sems.at[k],
                                        device_id=device, device_id_type=MESH)


def _gather_rider(shards, forward_at):
    n = len(shards)

    def copies(ins, outs, sems):
        send_sems, recv_sems, local_sems = sems
        x, y, c = _position()
        mine = 2 * x + y
        local = [pltpu.make_async_copy(ins[a], outs[a].at[:, mine], local_sems.at[a]) for a in range(n)]
        first, passed, arriving = [], [], []
        for k, (cx, cy) in enumerate(_other_chips(x, y)):
            theirs = 2 * cx + cy
            for a in range(n):
                first.append(_remote(ins[a].at[:, c], outs[a].at[:, mine, c], send_sems, recv_sems, k * n + a, (cx, cy, c)))
                block = outs[a].at[:, theirs, c]
                passed.append(_remote(block, block, send_sems, recv_sems, (3 + k) * n + a, (x, y, 1 - c)))
                other = outs[a].at[:, theirs, 1 - c]
                arriving.append(_remote(other, other, send_sems, recv_sems, (3 + k) * n + a, (x, y, 1 - c)))
        return local, first, passed, arriving

    def begin(ins, outs, sems):
        local, first, _, _ = copies(ins, outs, sems)
        for cp in local + first:
            cp.start()

    def forward(ins, outs, sems):
        _, first, passed, _ = copies(ins, outs, sems)
        for got, cp in zip(first, passed):
            got.wait_recv()
            cp.start()

    def finish(ins, outs, sems):
        local, first, passed, arriving = copies(ins, outs, sems)
        for cp in arriving:
            cp.wait_recv()
        for cp in first + passed:
            cp.wait_send()
        for cp in local:
            cp.wait()

    return _Rider(shards, [jax.ShapeDtypeStruct((s.shape[0], N_CHIPS) + s.shape[1:], s.dtype) for s in shards],
                  [pltpu.SemaphoreType.DMA((6 * n,)), pltpu.SemaphoreType.DMA((6 * n,)), pltpu.SemaphoreType.DMA((n,))],
                  [(0, begin), (forward_at, forward), (1, finish)])


def _exchange_rider(arrays, out_shape, n_copies, transfers, n_local=0):
    def copies(ins, outs, sems):
        send_sems, recv_sems, local_sems = sems
        remote, local = transfers(ins, outs)
        return ([_remote(src, dst, send_sems, recv_sems, i, dev) for i, (src, dst, dev) in enumerate(remote)],
                [pltpu.make_async_copy(src, dst, local_sems.at[i]) for i, (src, dst) in enumerate(local)])

    def begin(ins, outs, sems):
        remote, local = copies(ins, outs, sems)
        for cp in remote + local:
            cp.start()

    def finish(ins, outs, sems):
        remote, local = copies(ins, outs, sems)
        for cp in remote:
            cp.wait_recv()
        for cp in remote:
            cp.wait_send()
        for cp in local:
            cp.wait()

    return _Rider(arrays, out_shape,
                  [pltpu.SemaphoreType.DMA((n_copies,)), pltpu.SemaphoreType.DMA((n_copies,)),
                   pltpu.SemaphoreType.DMA((max(n_local, 1),))],
                  [(0, begin), (1, finish)])


def _pair_rider(partials):
    def transfers(ins, outs):
        x, y, c = _position()
        return [(ins[a].at[:, :, 1 - c], outs[a], (x, y, 1 - c)) for a in range(len(partials))], []

    shapes = [jax.ShapeDtypeStruct(g.shape[:2] + g.shape[3:], g.dtype) for g in partials]
    return _exchange_rider(partials, shapes, len(partials), transfers)


def _chips_rider(sums):
    n = len(sums)

    def transfers(ins, outs):
        x, y, c = _position()
        return [(ins[a].at[:, 2 * cx + cy], outs[a].at[:, k], (cx, cy, c))
                for k, (cx, cy) in enumerate(_other_chips(x, y)) for a in range(n)], []

    shapes = [jax.ShapeDtypeStruct((q.shape[0], 3) + q.shape[2:], q.dtype) for q in sums]
    return _exchange_rider(sums, shapes, 3 * n, transfers)


def _share_rider(halves):
    def transfers(ins, outs):
        x, y, c = _position()
        return [(ins[a], outs[a], (x, y, 1 - c)) for a in range(len(halves))], []

    return _exchange_rider(halves, [jax.ShapeDtypeStruct(h.shape, h.dtype) for h in halves], len(halves), transfers)


def _small_rider(pack):
    flips = [(fx, fy, fc) for fx in (0, 1) for fy in (0, 1) for fc in (0, 1)][1:]

    def transfers(ins, outs):
        x, y, c = _position()
        slot = outs[0].at[4 * x + 2 * y + c]
        flip = lambda v, f: v + f - 2 * v * f
        return [(ins[0], slot, (flip(x, fx), flip(y, fy), flip(c, fc))) for fx, fy, fc in flips], [(ins[0], slot)]

    return _exchange_rider([pack], [jax.ShapeDtypeStruct((8,) + pack.shape, pack.dtype)], len(flips), transfers, n_local=1)


def _alone(rider, name):
    return _hosted([rider], lambda: None, name=name, in_specs=[], out_specs=[], out_shape=[])()[1][0]


def _add_pair(mine, theirs, c, tag):
    l, _, hr, cols = theirs.shape

    def body(c_ref, mine_ref, theirs_ref, out_ref):
        out_ref[...] = (mine_ref[...] + theirs_ref[...]).astype(BF16)

    block = (None, None, hr, cols)
    return pl.pallas_call(
        body, name=f"add_pair_{tag}",
        grid_spec=pltpu.PrefetchScalarGridSpec(
            num_scalar_prefetch=1, grid=(l, N_CHIPS),
            in_specs=[pl.BlockSpec((None, None, None, hr, cols), lambda i, j, s: (i, j, s[0], 0, 0)),
                      pl.BlockSpec(block, lambda i, j, s: (i, j, 0, 0))],
            out_specs=pl.BlockSpec(block, lambda i, j, s: (i, j, 0, 0))),
        out_shape=jax.ShapeDtypeStruct(theirs.shape, BF16),
        compiler_params=_params(("parallel", "parallel")),
    )(c.reshape(1), mine, theirs)


def _add_chips(part, received, mine, tag):
    l, _, hr, cols = received.shape

    def body(j_ref, part_ref, recv_ref, out_ref):
        acc = part_ref[...].astype(F32)
        for k in range(3):
            acc += recv_ref[k].astype(F32)
        out_ref[...] = acc

    return pl.pallas_call(
        body, name=f"add_chips_{tag}",
        grid_spec=pltpu.PrefetchScalarGridSpec(
            num_scalar_prefetch=1, grid=(l,),
            in_specs=[pl.BlockSpec((None, None, hr, cols), lambda i, s: (i, s[0], 0, 0)),
                      pl.BlockSpec((None, 3, hr, cols), lambda i, s: (i, 0, 0, 0))],
            out_specs=pl.BlockSpec((None, hr, cols), lambda i, s: (i, 0, 0))),
        out_shape=jax.ShapeDtypeStruct((l, hr, cols), F32),
        compiler_params=_params(("parallel",)),
    )(mine.reshape(1), part, received)


def _adam_update(w, g, m, v):
    m2 = ADAM_B1 * m + (1.0 - ADAM_B1) * g
    v2 = ADAM_B2 * v + (1.0 - ADAM_B2) * jnp.square(g)
    m_hat = m2 / (1.0 - ADAM_B1 ** ADAM_STEP)
    v_hat = v2 / (1.0 - ADAM_B2 ** ADAM_STEP)
    return -ADAM_LR * (m_hat / (jnp.sqrt(v_hat) + ADAM_EPS) + ADAM_WD * w), m2, v2


def _adamw_group(items, tag, riders=()):
    n = len(items)

    def body(*refs):
        ins, outs = refs[:5 * n], refs[5 * n:]
        mine = pl.program_id(0) == lax.axis_index("c")
        for a in range(n):
            w_ref, own_ref, other_ref, m_ref, v_ref = ins[5 * a:5 * a + 5]
            g_ref, d_ref, nm_ref, nv_ref = outs[4 * a:4 * a + 4]
            gv = jnp.where(mine, own_ref[...], other_ref[...])
            g_ref[...] = gv
            d_ref[...], nm_ref[...], nv_ref[...] = _adam_update(w_ref[...], gv, m_ref[...], v_ref[...])

    in_specs, out_specs, out_shape, args = [], [], [], []
    for w, own, other, m, v in items:
        _, hr, cols = w.shape
        tr = hr // ADAM_BLOCKS
        full = pl.BlockSpec((None, tr, cols), lambda h, i: (h, i, 0))
        half = pl.BlockSpec((tr, cols), lambda h, i: (i, 0))
        in_specs += [full, half, half, full, full]
        out_specs += [full] * 4
        out_shape += [jax.ShapeDtypeStruct((2, hr, cols), F32)] * 4
        args += [w, own, other, m, v]
    outs, extras = _hosted(riders, body, name=f"adamw_{tag}", grid=(2, ADAM_BLOCKS), in_specs=in_specs, out_specs=out_specs,
                           out_shape=out_shape, compiler_params=_params(("arbitrary", "arbitrary")))(*args)
    return [outs[4 * a:4 * a + 4] for a in range(n)], extras


def _adamw_small(w, gathered, m, v):
    def body(w_ref, g_ref, m_ref, v_ref, sum_ref, d_ref, nm_ref, nv_ref):
        gv = g_ref[0]
        for i in range(1, g_ref.shape[0]):
            gv += g_ref[i]
        sum_ref[...] = gv
        d_ref[...], nm_ref[...], nv_ref[...] = _adam_update(w_ref[...], gv, m_ref[...], v_ref[...])

    shape = jax.ShapeDtypeStruct(w.shape, F32)
    return pl.pallas_call(body, name="adamw_small", out_shape=[shape] * 4, compiler_params=_params())(w, gathered, m, v)


VECTOR_PARAMS = ("ffn1_norm", "mix_norm", "hgrn_lb", "hgrn_onorm", "ffn2_norm", "ple_norm", "ple_post_norm", "final_norm",
                 "pool_scale")
ALL_PARAMS = ("ffn1_norm", "ffn1_w1", "ffn1_w3", "ffn1_w2", "mix_norm", "w_in", "hgrn_lb", "hgrn_onorm", "w_branch_a",
              "pool_w", "pool_scale", "w_branch_b", "w_out", "ffn2_norm", "ffn2_w1", "ffn2_w3", "ffn2_w2", "ple_norm",
              "ple_w_gate", "ple_w_proj", "ple_post_norm", "final_norm")
TILE_ROWS = 8


def _pack_small(values):
    parts = []
    for name in VECTOR_PARAMS:
        a = values[name].reshape(-1, values[name].shape[-1])
        parts.append(jnp.pad(a, ((0, TILE_ROWS - a.shape[0]), (0, D_MODEL - a.shape[1]))))
    parts.append(values["pool_w"].reshape(-1, D_MODEL))
    return jnp.concatenate(parts, axis=0)


def _unpack_small(pack, shapes):
    out = {}
    for i, name in enumerate(VECTOR_PARAMS):
        shape = shapes[name]
        rows = 1 if len(shape) == 1 else shape[0]
        out[name] = pack[i * TILE_ROWS:i * TILE_ROWS + rows, :shape[-1]].reshape(shape)
    out["pool_w"] = pack[len(VECTOR_PARAMS) * TILE_ROWS:].reshape(shapes["pool_w"])
    return out


def _halved(a, lead):
    return a.reshape(lead, 2, -1, a.shape[-1])


def _shard_halves(a, lead):
    return a.reshape(lead, N_CHIPS, 2, -1, a.shape[-1])


REDUCED_TRANSPOSED = ("ffn1_w1", "ffn1_w3", "ffn2_w1", "ffn2_w3")


def _entries(arrays):
    return [a[i] for a in arrays for i in range(a.shape[0])]


def _adam_items(names, own, other, w, m, v):
    items = []
    for name, g_own, g_other in zip(names, _entries(own), _entries(other)):
        view = (lambda a: _halved(a[0].T, 1)[0]) if name in REDUCED_TRANSPOSED else (lambda a: _halved(a, 1)[0])
        items.append((view(w[name]), g_own, g_other, view(m[name]), view(v[name])))
    return items


def _adam_store(names, results, w, out):
    for name, res in zip(names, results):
        shape = w[name].shape
        if name in REDUCED_TRANSPOSED:
            back = [a.reshape(shape[2], shape[1]).T.reshape(shape) for a in res]
        else:
            back = [a.reshape(shape) for a in res]
        out["grad"][name], out["delta"][name], out["new_m"][name], out["new_v"][name] = back


def kernel(x, p, ffn1_norm, ffn1_w1, ffn1_w3, ffn1_w2, mix_norm, w_in, hgrn_lb, hgrn_onorm, w_branch_a, pool_w, pool_scale, w_branch_b, w_out, ffn2_norm, ffn2_w1, ffn2_w3, ffn2_w2, ple_norm, ple_w_gate, ple_w_proj, ple_post_norm, final_norm, loss_target, m_ffn1_norm, m_ffn1_w1, m_ffn1_w3, m_ffn1_w2, m_mix_norm, m_w_in, m_hgrn_lb, m_hgrn_onorm, m_w_branch_a, m_pool_w, m_pool_scale, m_w_branch_b, m_w_out, m_ffn2_norm, m_ffn2_w1, m_ffn2_w3, m_ffn2_w2, m_ple_norm, m_ple_w_gate, m_ple_w_proj, m_ple_post_norm, m_final_norm, v_ffn1_norm, v_ffn1_w1, v_ffn1_w3, v_ffn1_w2, v_mix_norm, v_w_in, v_hgrn_lb, v_hgrn_onorm, v_w_branch_a, v_pool_w, v_pool_scale, v_w_branch_b, v_w_out, v_ffn2_norm, v_ffn2_w1, v_ffn2_w3, v_ffn2_w2, v_ple_norm, v_ple_w_gate, v_ple_w_proj, v_ple_post_norm, v_final_norm):
    args = dict(locals())
    w = {name: args[name] for name in ALL_PARAMS}
    m = {name: args["m_" + name] for name in ALL_PARAMS}
    v = {name: args["v_" + name] for name in ALL_PARAMS}
    cx, cy, cc = _position()
    chip = (2 * cx + cy).astype(jnp.int32)
    core = cc.astype(jnp.int32)
    xs, ps, target = x[0], p[0, 0], loss_target[0]
    t = xs.shape[0]
    tm = min(256, t)
    tm_ffn = min(512, t)
    tt = min(512, t)
    tk = min(2048, t)
    small = {name: w[name] for name in VECTOR_PARAMS}
    small["final_norm"] = w["final_norm"].reshape(1, D_MODEL)
    pool_w0 = w["pool_w"][0]

    ffn_shard = lambda i: _halved(jnp.stack([w[f"ffn{i}_w1"][0].T, w[f"ffn{i}_w3"][0].T, w[f"ffn{i}_w2"][0]]).astype(BF16), 3)
    sq_shard = _halved(jnp.stack([w["w_branch_a"][0], w["w_out"][0], w["ple_w_gate"][0]]).astype(BF16), 3)
    win_shard, wb_shard, wp_shard = (_halved(w[n].astype(BF16), 1) for n in ("w_in", "w_branch_b", "ple_w_proj"))

    (ffn1w,) = _alone(_gather_rider([ffn_shard(1)], 0.5), "gather_ffn1")
    ffn1w = ffn1w.reshape(3, D_FF, D_MODEL)
    (x1, a1, b1), ((winw,),) = _ffn_fwd(xs, small["ffn1_norm"], ffn1w, 1, tm_ffn, [_gather_rider([win_shard], 0.6)])
    winw = winw.reshape(N_CHIPS, D_MODEL, SHARD_IN_COLS)
    (main, pool_r, gates), ((sqw, wbw, wpw),) = _mix_fwd(x1, small["mix_norm"], winw, tm,
                                                          [_gather_rider([sq_shard, wb_shard, wp_shard], 0.5)])
    sqw = sqw.reshape(3, D_MODEL, D_MODEL)
    wbw = wbw.reshape(N_CHIPS, POOL_WIDTH, -1)
    wpw = wpw.reshape(N_CHIPS, ps.shape[1], -1)
    (o, states), ((ffn2w,),) = _hgrn_fwd(main, small["hgrn_lb"], tt, [_gather_rider([ffn_shard(2)], 0.7)])
    ffn2w = ffn2w.reshape(3, D_FF, D_MODEL)
    (x2, ya, yb, pooled), _ = _post_fwd(o, main, pool_r, gates, x1, small["hgrn_onorm"], pool_w0, small["pool_scale"], sqw,
                                       wbw, tm)
    (x3, a2, b2), _ = _ffn_fwd(x2, small["ffn2_norm"], ffn2w, 2, tm_ffn)
    (dx3, loss, d_wg, d_wp, d_ple, d_post, d_final), _ = _tail(
        x3, ps, target, small["ple_norm"], small["ple_post_norm"], small["final_norm"], sqw, wpw, tm)
    loss = lax.psum(loss[0, 0], ("x", "y", "c"))

    add_pairs = lambda parts, got, tags: [_add_pair(a, b, core, tag) for a, b, tag in zip(parts, got, tags)]
    add_chips = lambda sums, got, tags: [_add_chips(a, b, chip, tag) for a, b, tag in zip(sums, got, tags)]
    names1 = ("ffn2_w1", "ffn2_w3", "ffn2_w2", "ple_w_gate", "ple_w_proj")
    names2 = ("w_branch_a", "w_out", "w_branch_b")
    names3 = ("w_in",)
    names4 = ("ffn1_w1", "ffn1_w3")
    names5 = ("ffn1_w2",)
    tags1, tags2, tags3, tags4, tags5 = ("w13_2", "w2_2", "wg", "wp"), ("wa", "wout", "wb"), ("win",), ("w13_1",), ("w2_1",)

    (dx2, dab2, s2, h3, dxh2, d_ffn2_norm), _ = _ffn_bwd(dx3, x2, small["ffn2_norm"], a2, b2, ffn2w, 2, tm)
    (d_w13_2,), _ = _wgrad(dab2, h3, WGRAD_IN_BLOCKS, "wgrad_ffn2_in", tk)
    (d_w2_2,), _ = _wgrad(s2, dxh2, WGRAD_OUT_BLOCKS, "wgrad_ffn2_out", tk)
    part1 = [_shard_halves(d_w13_2, 2), _shard_halves(d_w2_2, 1), _shard_halves(d_wg, 1), _shard_halves(d_wp, 1)]
    (do, dog, du, dgates, d_wa, d_wout, d_wb, d_pool_w, d_pool_scale, d_onorm), (sib1,) = _post_bwd(
        dx2, o, main, gates, ya, yb, pooled, small["hgrn_onorm"], pool_w0, small["pool_scale"], sqw, wbw, tm,
        [_pair_rider(part1)])
    sums1 = add_pairs(part1, sib1, tags1)
    part2 = [_shard_halves(d_wa, 1), _shard_halves(d_wout, 1), _shard_halves(d_wb, 1)]
    (dqfi, d_lb), (got1, sib2) = _hgrn_bwd(main, small["hgrn_lb"], states, do, tt, [_chips_rider(sums1), _pair_rider(part2)])
    own1 = add_chips(sums1, got1, tags1)
    sums2 = add_pairs(part2, sib2, tags2)
    (dx1, dproj, h2, d_mix_norm), (other1, got2) = _mix_bwd(dqfi, dog, du, dgates, dx2, x1, small["mix_norm"], winw, tm,
                                                            [_share_rider(own1), _chips_rider(sums2)])
    own2 = add_chips(sums2, got2, tags2)
    (d_win,), (other2,) = _wgrad_cols(h2, dproj, N_CHIPS, "wgrad_in", tk, [_share_rider(own2)])
    part3 = [_shard_halves(d_win, 1)]
    (dx, dab1, s1, h1, dxh1, d_ffn1_norm), _ = _ffn_bwd(dx1, xs, small["ffn1_norm"], a1, b1, ffn1w, 1, tm)
    vecs = dict(ffn1_norm=d_ffn1_norm, mix_norm=d_mix_norm, hgrn_lb=d_lb, hgrn_onorm=d_onorm, ffn2_norm=d_ffn2_norm,
                ple_norm=d_ple, ple_post_norm=d_post, final_norm=d_final, pool_scale=d_pool_scale, pool_w=d_pool_w)
    (d_w13_1,), (sib3, (small_all,)) = _wgrad(dab1, h1, WGRAD_IN_BLOCKS, "wgrad_ffn1_in", tk,
                                              [_pair_rider(part3), _small_rider(_pack_small(vecs))])
    sums3 = add_pairs(part3, sib3, tags3)
    part4 = [_shard_halves(d_w13_1, 2)]
    (d_w2_1,), (got3, sib4) = _wgrad(s1, dxh1, WGRAD_OUT_BLOCKS, "wgrad_ffn1_out", tk,
                                     [_chips_rider(sums3), _pair_rider(part4)])
    own3 = add_chips(sums3, got3, tags3)
    sums4 = add_pairs(part4, sib4, tags4)
    part5 = [_shard_halves(d_w2_1, 1)]

    out = dict(grad={}, delta={}, new_m={}, new_v={})
    results, (other3, got4, sib5) = _adamw_group(
        _adam_items(names1 + names2, own1 + own2, other1 + other2, w, m, v), "early",
        [_share_rider(own3), _chips_rider(sums4), _pair_rider(part5)])
    _adam_store(names1 + names2, results, w, out)
    own4 = add_chips(sums4, got4, tags4)
    sums5 = add_pairs(part5, sib5, tags5)
    results, (other4, got5) = _adamw_group(_adam_items(names3, own3, other3, w, m, v), "w_in",
                                           [_share_rider(own4), _chips_rider(sums5)])
    _adam_store(names3, results, w, out)
    own5 = add_chips(sums5, got5, tags5)
    other5 = _alone(_share_rider(own5), "share_last")
    results, _ = _adamw_group(_adam_items(names4 + names5, own4 + own5, other4 + other5, w, m, v), "ffn1")
    _adam_store(names4 + names5, results, w, out)

    shapes = {name: w[name].shape for name in VECTOR_PARAMS + ("pool_w",)}
    results = _adamw_small(_pack_small(w), small_all, _pack_small(m), _pack_small(v))
    for key, pack in zip(("grad", "delta", "new_m", "new_v"), results):
        out[key].update(_unpack_small(pack, shapes))

    return (loss, dx[None], *[out["grad"][n] for n in ALL_PARAMS], *[out["delta"][n] for n in ALL_PARAMS],
            *[out["new_m"][n] for n in ALL_PARAMS], *[out["new_v"][n] for n in ALL_PARAMS])
```

```python
import functools

import jax
import jax.numpy as jnp
from jax import lax
from jax.experimental import pallas as pl
from jax.experimental.pallas import tpu as pltpu

F32 = jnp.float32
BF16 = jnp.bfloat16
MESH = pl.DeviceIdType.MESH

D_MODEL = 1024
D_FF = 2816
HEADS = 8
HEAD_DIM = 128
POOL_WIDTH = 512
POOL_WINDOWS = (2, 4, 8, 16)
POOL_HALO = 16
N_CHIPS = 4
EPS = 1e-6
CHUNK = 64
MAIN_COLS = 4096
GATE_COLS = 2048
SHARD_IN_COLS = 1664

ADAM_LR = 0.001
ADAM_B1 = 0.9
ADAM_B2 = 0.999
ADAM_EPS = 1e-08
ADAM_WD = 0.01
ADAM_STEP = 10

VMEM_LIMIT = 56 * 1024 * 1024
WGRAD_IN_BLOCKS = 4
WGRAD_OUT_BLOCKS = 2
ADAM_BLOCKS = 4


def _params(semantics=None, vmem=VMEM_LIMIT):
    return pltpu.CompilerParams(dimension_semantics=semantics, vmem_limit_bytes=vmem)


def _dot(a, b):
    return jnp.dot(a, b, preferred_element_type=F32)


def _dot_nt(a, b):
    return lax.dot_general(a, b, (((1,), (1,)), ((), ())), preferred_element_type=F32)


def _dot_tn(a, b):
    return lax.dot_general(a, b, (((0,), (0,)), ((), ())), preferred_element_type=F32)


def _tri_sum(tri, x):
    hi = x.astype(BF16)
    lo = (x - hi.astype(F32)).astype(BF16)
    return _dot(tri, hi) + _dot(tri, lo)


def _sigmoid(x):
    return jax.nn.sigmoid(x)


def _resident(shape):
    zeros = (0,) * len(shape)
    return pl.BlockSpec(shape, lambda *_: zeros, pipeline_mode=pl.Buffered(1))


def _pick(shape, k):
    zeros = (0,) * (len(shape) - 1)
    return pl.BlockSpec((None,) + tuple(shape[1:]), lambda *_: (k,) + zeros, pipeline_mode=pl.Buffered(1))


def _rows(tm, cols, col_block=0):
    return pl.BlockSpec((tm, cols), lambda i: (i, col_block))


def _acc(shape):
    zeros = (0,) * len(shape)
    return pl.BlockSpec(shape, lambda *_: zeros)


def _rms(x):
    r = lax.rsqrt(jnp.mean(x * x, axis=-1, keepdims=True) + EPS)
    return r, x * r


def _rms_bwd(dn, n, r):
    return r * (dn - n * jnp.mean(dn * n, axis=-1, keepdims=True))


def _colsum(a):
    return jnp.sum(a, axis=0, keepdims=True)


ANY = pl.BlockSpec(memory_space=pl.ANY)


class _Rider:
    def __init__(self, inputs, out_shape, sems, phases):
        self.inputs, self.out_shape, self.sems, self.phases = list(inputs), list(out_shape), list(sems), list(phases)


def _hosted(riders, body, *, name, grid=(), in_specs, out_specs, out_shape, scratch_shapes=(), compiler_params=None):
    riders = [r for r in riders if r is not None]
    n_in, n_out, n_scr = len(in_specs), len(out_shape), len(scratch_shapes)
    n_steps = 1
    for g in grid:
        n_steps *= g

    def wrapped(*refs):
        pos = n_in
        ins = refs[:n_in]
        r_ins = []
        for r in riders:
            r_ins.append(refs[pos:pos + len(r.inputs)])
            pos += len(r.inputs)
        outs = refs[pos:pos + n_out]
        pos += n_out
        r_outs = []
        for r in riders:
            r_outs.append(refs[pos:pos + len(r.out_shape)])
            pos += len(r.out_shape)
        scr = refs[pos:pos + n_scr]
        pos += n_scr
        r_sems = []
        for r in riders:
            r_sems.append(refs[pos:pos + len(r.sems)])
            pos += len(r.sems)
        step = 0
        for axis in range(len(grid)):
            step = step * grid[axis] + pl.program_id(axis)

        def at_step(which, fn):
            if n_steps == 1:
                fn()
            else:
                pl.when(step == which)(fn)

        for r, ri, ro, rs in zip(riders, r_ins, r_outs, r_sems):
            for fraction, fn in r.phases:
                if fraction == 0:
                    at_step(0, functools.partial(fn, ri, ro, rs))
        body(*ins, *outs, *scr)
        for r, ri, ro, rs in zip(riders, r_ins, r_outs, r_sems):
            for fraction, fn in r.phases:
                if fraction > 0:
                    at_step(min(int(fraction * n_steps), n_steps - 1), functools.partial(fn, ri, ro, rs))

    call = pl.pallas_call(
        wrapped, name=name, grid=grid,
        in_specs=list(in_specs) + [ANY for r in riders for _ in r.inputs],
        out_specs=list(out_specs) + [ANY for r in riders for _ in r.out_shape],
        out_shape=list(out_shape) + [s for r in riders for s in r.out_shape],
        scratch_shapes=list(scratch_shapes) + [s for r in riders for s in r.sems],
        compiler_params=compiler_params)

    def run(*args):
        res = call(*args, *[a for r in riders for a in r.inputs])
        extras, pos = [], n_out
        for r in riders:
            extras.append(list(res[pos:pos + len(r.out_shape)]))
            pos += len(r.out_shape)
        return list(res[:n_out]), extras

    return run


def _ffn_fwd(x, g, ffnw, tag, tm, riders=()):
    t = x.shape[0]

    def body(x_ref, g_ref, w1_ref, w3_ref, w2_ref, xo_ref, a_ref, b_ref):
        xv = x_ref[...]
        _, n = _rms(xv)
        h = (n * g_ref[...]).astype(BF16)
        a = _dot_nt(h, w1_ref[...])
        b = _dot_nt(h, w3_ref[...])
        s = (a * _sigmoid(a) * b).astype(BF16)
        xo_ref[...] = xv + 0.5 * _dot(s, w2_ref[...])
        a_ref[...] = a.astype(BF16)
        b_ref[...] = b.astype(BF16)

    return _hosted(
        riders, body, name=f"ffn_fwd_{tag}", grid=(t // tm,),
        in_specs=[_rows(tm, D_MODEL), _resident((1, D_MODEL)), _pick(ffnw.shape, 0), _pick(ffnw.shape, 1),
                  _pick(ffnw.shape, 2)],
        out_specs=[_rows(tm, D_MODEL), _rows(tm, D_FF), _rows(tm, D_FF)],
        out_shape=[jax.ShapeDtypeStruct((t, D_MODEL), F32), jax.ShapeDtypeStruct((t, D_FF), BF16),
                   jax.ShapeDtypeStruct((t, D_FF), BF16)],
        compiler_params=_params(("arbitrary",)),
    )(x, g, ffnw, ffnw, ffnw)


def _ffn_bwd(dxo, x, g, a, b, ffnw, tag, tm, riders=()):
    t = x.shape[0]

    def body(dxo_ref, x_ref, g_ref, a_ref, b_ref, w1_ref, w3_ref, w2_ref, dx_ref, dab_ref, s_ref, h_ref, dxh_ref, dg_ref):
        @pl.when(pl.program_id(0) == 0)
        def _():
            dg_ref[...] = jnp.zeros_like(dg_ref)

        xv = x_ref[...]
        gv = g_ref[...]
        r, n = _rms(xv)
        h_ref[...] = (n * gv).astype(BF16)
        dxo_v = dxo_ref[...]
        dxh = (0.5 * dxo_v).astype(BF16)
        dxh_ref[...] = dxh
        ds = _dot_nt(dxh, w2_ref[...])
        av = a_ref[...].astype(F32)
        bv = b_ref[...].astype(F32)
        sg = _sigmoid(av)
        silu = av * sg
        s_ref[...] = (silu * bv).astype(BF16)
        da = (ds * bv * (sg * (1.0 + av * (1.0 - sg)))).astype(BF16)
        db = (ds * silu).astype(BF16)
        dab_ref[:, :D_FF] = da
        dab_ref[:, D_FF:] = db
        dh = _dot(da, w1_ref[...]) + _dot(db, w3_ref[...])
        dg_ref[...] += _colsum(dh * n)
        dx_ref[...] = dxo_v + _rms_bwd(dh * gv, n, r)

    return _hosted(
        riders, body, name=f"ffn_bwd_{tag}", grid=(t // tm,),
        in_specs=[_rows(tm, D_MODEL), _rows(tm, D_MODEL), _resident((1, D_MODEL)), _rows(tm, D_FF), _rows(tm, D_FF),
                  _pick(ffnw.shape, 0), _pick(ffnw.shape, 1), _pick(ffnw.shape, 2)],
        out_specs=[_rows(tm, D_MODEL), _rows(tm, 2 * D_FF), _rows(tm, D_FF), _rows(tm, D_MODEL), _rows(tm, D_MODEL),
                   _acc((1, D_MODEL))],
        out_shape=[jax.ShapeDtypeStruct((t, D_MODEL), F32), jax.ShapeDtypeStruct((t, 2 * D_FF), BF16),
                   jax.ShapeDtypeStruct((t, D_FF), BF16), jax.ShapeDtypeStruct((t, D_MODEL), BF16),
                   jax.ShapeDtypeStruct((t, D_MODEL), BF16), jax.ShapeDtypeStruct((1, D_MODEL), F32)],
        compiler_params=_params(("arbitrary",)),
    )(dxo, x, g, a, b, ffnw, ffnw, ffnw)


def _wgrad(xm, dy, out_blocks, name, tk, riders=()):
    t, m = xm.shape
    n = dy.shape[1]
    mb = m // out_blocks

    def body(x_ref, dy_ref, o_ref):
        @pl.when(pl.program_id(1) == 0)
        def _():
            o_ref[...] = jnp.zeros_like(o_ref)

        o_ref[...] += _dot_tn(x_ref[...], dy_ref[...])

    return _hosted(
        riders, body, name=name, grid=(out_blocks, t // tk),
        in_specs=[pl.BlockSpec((tk, mb), lambda j, k: (k, j)), pl.BlockSpec((tk, n), lambda j, k: (k, 0))],
        out_specs=[pl.BlockSpec((None, mb, n), lambda j, k: (j, 0, 0))],
        out_shape=[jax.ShapeDtypeStruct((out_blocks, mb, n), F32)],
        compiler_params=_params(("arbitrary", "arbitrary")),
    )(xm, dy)


def _wgrad_cols(xm, dy, out_blocks, name, tk, riders=()):
    t, m = xm.shape
    n = dy.shape[1]
    nb = n // out_blocks

    def body(x_ref, dy_ref, o_ref):
        @pl.when(pl.program_id(1) == 0)
        def _():
            o_ref[...] = jnp.zeros_like(o_ref)

        o_ref[...] += _dot_tn(x_ref[...], dy_ref[...])

    return _hosted(
        riders, body, name=name, grid=(out_blocks, t // tk),
        in_specs=[pl.BlockSpec((tk, m), lambda j, k: (k, 0)), pl.BlockSpec((tk, nb), lambda j, k: (k, j))],
        out_specs=[pl.BlockSpec((None, m, nb), lambda j, k: (j, 0, 0))],
        out_shape=[jax.ShapeDtypeStruct((out_blocks, m, nb), F32)],
        compiler_params=_params(("arbitrary", "arbitrary")),
    )(xm, dy)


def _mix_fwd(x1, g, winw, tm, riders=()):
    t = x1.shape[0]

    def body(x_ref, g_ref, w_ref, main_ref, pool_ref, gate_ref):
        _, n = _rms(x_ref[...])
        h = (n * g_ref[...]).astype(BF16)
        proj = jnp.concatenate([_dot(h, w_ref[j]) for j in range(N_CHIPS)], axis=1)
        main_ref[...] = proj[:, :MAIN_COLS]
        pool_ref[...] = proj[:, MAIN_COLS:MAIN_COLS + POOL_WIDTH]
        gate_ref[...] = proj[:, MAIN_COLS + POOL_WIDTH:]

    return _hosted(
        riders, body, name="mix_fwd", grid=(t // tm,),
        in_specs=[_rows(tm, D_MODEL), _resident((1, D_MODEL)), _resident(winw.shape)],
        out_specs=[_rows(tm, MAIN_COLS), _rows(tm, POOL_WIDTH), _rows(tm, GATE_COLS)],
        out_shape=[jax.ShapeDtypeStruct((t, MAIN_COLS), F32), jax.ShapeDtypeStruct((t, POOL_WIDTH), F32),
                   jax.ShapeDtypeStruct((t, GATE_COLS), F32)],
        compiler_params=_params(("arbitrary",)),
    )(x1, g, winw)


def _mix_bwd(dqfi, dog, du, dgates, dx2, x1, g, winw, tm, riders=()):
    t = x1.shape[0]
    cols = N_CHIPS * SHARD_IN_COLS

    def body(dqfi_ref, dog_ref, du_ref, dgt_ref, dx2_ref, x_ref, g_ref, w_ref, dx_ref, dproj_ref, h_ref, dg_ref):
        @pl.when(pl.program_id(0) == 0)
        def _():
            dg_ref[...] = jnp.zeros_like(dg_ref)

        dproj = jnp.concatenate([dqfi_ref[...], dog_ref[...], du_ref[...], dgt_ref[...]], axis=1)
        dproj_ref[...] = dproj
        dh = _dot_nt(dproj[:, :SHARD_IN_COLS], w_ref[0])
        for j in range(1, N_CHIPS):
            dh += _dot_nt(dproj[:, j * SHARD_IN_COLS:(j + 1) * SHARD_IN_COLS], w_ref[j])
        gv = g_ref[...]
        r, n = _rms(x_ref[...])
        h_ref[...] = (n * gv).astype(BF16)
        dg_ref[...] += _colsum(dh * n)
        dx_ref[...] = dx2_ref[...] + _rms_bwd(dh * gv, n, r)

    return _hosted(
        riders, body, name="mix_bwd", grid=(t // tm,),
        in_specs=[_rows(tm, 3 * D_MODEL), _rows(tm, D_MODEL), _rows(tm, POOL_WIDTH), _rows(tm, GATE_COLS),
                  _rows(tm, D_MODEL), _rows(tm, D_MODEL), _resident((1, D_MODEL)), _resident(winw.shape)],
        out_specs=[_rows(tm, D_MODEL), _rows(tm, cols), _rows(tm, D_MODEL), _acc((1, D_MODEL))],
        out_shape=[jax.ShapeDtypeStruct((t, D_MODEL), F32), jax.ShapeDtypeStruct((t, cols), BF16),
                   jax.ShapeDtypeStruct((t, D_MODEL), BF16), jax.ShapeDtypeStruct((1, D_MODEL), F32)],
        compiler_params=_params(("arbitrary",)),
    )(dqfi, dog, du, dgates, dx2, x1, g, winw)


def _lower_bound(lb_raw):
    l0 = lb_raw[0:1, :]
    l1 = lb_raw[1:2, :]
    m = jnp.maximum(l0, l1)
    e0 = jnp.exp(l0 - m)
    e1 = jnp.exp(l1 - m)
    return e0 / (e0 + e1)


def _head_slices():
    return [slice(h * HEAD_DIM, (h + 1) * HEAD_DIM) for h in range(HEADS)]


def _gates(qr, fr, lb, tril_b, first_half):
    sg = _sigmoid(fr)
    f = lb + (1.0 - lb) * sg
    k = 1.0 - f
    sq = _sigmoid(qr)
    q = qr * sq
    log_f = jnp.log(f)
    gc = _tri_sum(tril_b, log_f)
    gm = _colsum(jnp.where(first_half, log_f, 0.0))
    gl = _colsum(log_f)
    e_q = jnp.exp(gc - gm)
    e_k = jnp.exp(gm - gc)
    e_in = jnp.exp(gc)
    e_out = jnp.exp(gl - gc)
    return dict(sg=sg, f=f, k=k, sq=sq, q=q, e_q=e_q, e_k=e_k, e_in=e_in, e_out=e_out, e_last=jnp.exp(gl))


def _hgrn_fwd(main, lb_raw, tt, riders=()):
    t = main.shape[0]
    n_local = tt // CHUNK

    def body(q_ref, f_ref, i_ref, lb_ref, o_ref, st_ref, s_scr):
        @pl.when(pl.program_id(0) == 0)
        def _():
            s_scr[...] = jnp.zeros_like(s_scr)

        lb = _lower_bound(lb_ref[...])
        row = lax.broadcasted_iota(jnp.int32, (CHUNK, CHUNK), 0)
        col = lax.broadcasted_iota(jnp.int32, (CHUNK, CHUNK), 1)
        tril = row >= col
        tril_b = tril.astype(BF16)
        first_half = lax.broadcasted_iota(jnp.int32, (CHUNK, D_MODEL), 0) < CHUNK // 2
        heads = _head_slices()

        def chunk(c, carry):
            rows = pl.ds(pl.multiple_of(c * CHUNK, CHUNK), CHUNK)
            z = _gates(q_ref[rows, :], f_ref[rows, :], lb, tril_b, first_half)
            qt = (z["q"] * z["e_q"]).astype(BF16)
            kt = (z["k"] * z["e_k"]).astype(BF16)
            qg = (z["q"] * z["e_in"]).astype(BF16)
            kg = (z["k"] * z["e_out"]).astype(BF16)
            vb = i_ref[rows, :].astype(BF16)
            states = [s_scr[h] for h in range(HEADS)]
            for h in range(HEADS):
                st_ref[c, h] = states[h]
            raw = [_dot_nt(qt[:, sl], kt[:, sl]) for sl in heads]
            inter = [_dot_nt(qg[:, sl], states[h].astype(BF16)) for h, sl in enumerate(heads)]
            grown = [_dot_tn(vb[:, sl], kg[:, sl]) for sl in heads]
            scores = [jnp.where(tril, r, 0.0).astype(BF16) for r in raw]
            for h, sl in enumerate(heads):
                s_scr[h] = states[h] * z["e_last"][:, sl] + grown[h]
            o_ref[rows, :] = jnp.concatenate([_dot(scores[h], vb[:, sl]) + inter[h] for h, sl in enumerate(heads)], axis=1)
            return carry

        lax.fori_loop(0, n_local, chunk, 0, unroll=2)

    return _hosted(
        riders, body, name="hgrn_fwd", grid=(t // tt,),
        in_specs=[_rows(tt, D_MODEL, 0), _rows(tt, D_MODEL, 1), _rows(tt, D_MODEL, 2), _resident((2, D_MODEL))],
        out_specs=[_rows(tt, D_MODEL),
                   pl.BlockSpec((n_local, HEADS, HEAD_DIM, HEAD_DIM), lambda i: (i, 0, 0, 0))],
        out_shape=[jax.ShapeDtypeStruct((t, D_MODEL), F32),
                   jax.ShapeDtypeStruct((t // CHUNK, HEADS, HEAD_DIM, HEAD_DIM), F32)],
        scratch_shapes=[pltpu.VMEM((HEADS, HEAD_DIM, HEAD_DIM), F32)],
        compiler_params=_params(("arbitrary",)),
    )(main, main, main, lb_raw)


def _hgrn_bwd(main, lb_raw, states, do, tt, riders=()):
    t = main.shape[0]
    n_tiles = t // tt
    n_local = tt // CHUNK

    def rev(col_block):
        return pl.BlockSpec((tt, D_MODEL), lambda i: (n_tiles - 1 - i, col_block))

    def body(q_ref, f_ref, i_ref, lb_ref, st_ref, do_ref, dqfi_ref, dlb_ref, ds_scr, acc_scr):
        @pl.when(pl.program_id(0) == 0)
        def _():
            ds_scr[...] = jnp.zeros_like(ds_scr)
            acc_scr[...] = jnp.zeros_like(acc_scr)

        lb = _lower_bound(lb_ref[...])
        row = lax.broadcasted_iota(jnp.int32, (CHUNK, CHUNK), 0)
        col = lax.broadcasted_iota(jnp.int32, (CHUNK, CHUNK), 1)
        tril = row >= col
        tril_b = tril.astype(BF16)
        triu_b = (row <= col).astype(BF16)
        first_half = lax.broadcasted_iota(jnp.int32, (CHUNK, D_MODEL), 0) < CHUNK // 2
        heads = _head_slices()
        cat = functools.partial(jnp.concatenate, axis=1)

        def chunk(cc, carry):
            c = n_local - 1 - cc
            rows = pl.ds(pl.multiple_of(c * CHUNK, CHUNK), CHUNK)
            qr = q_ref[rows, :]
            z = _gates(qr, f_ref[rows, :], lb, tril_b, first_half)
            qt = (z["q"] * z["e_q"]).astype(BF16)
            kt = (z["k"] * z["e_k"]).astype(BF16)
            qg_f = z["q"] * z["e_in"]
            qg = qg_f.astype(BF16)
            kg_f = z["k"] * z["e_out"]
            kg = kg_f.astype(BF16)
            vb = i_ref[rows, :].astype(BF16)
            dob = do_ref[rows, :].astype(BF16)
            st = [st_ref[c, h] for h in range(HEADS)]
            dst = [ds_scr[h] for h in range(HEADS)]
            dst_b = [d.astype(BF16) for d in dst]
            raw = [_dot_nt(qt[:, sl], kt[:, sl]) for sl in heads]
            draw = [_dot_nt(dob[:, sl], vb[:, sl]) for sl in heads]
            dqg = [_dot(dob[:, sl], st[h].astype(BF16)) for h, sl in enumerate(heads)]
            dkg = [_dot(vb[:, sl], dst_b[h]) for h, sl in enumerate(heads)]
            dv_inter = [_dot_nt(kg[:, sl], dst_b[h]) for h, sl in enumerate(heads)]
            grown = [_dot_tn(dob[:, sl], qg[:, sl]) for sl in heads]
            scores = [jnp.where(tril, r, 0.0).astype(BF16) for r in raw]
            dscores = [jnp.where(tril, r, 0.0).astype(BF16) for r in draw]
            dqt = [_dot(dscores[h], kt[:, sl]) for h, sl in enumerate(heads)]
            dkt = [_dot_tn(dscores[h], qt[:, sl]) for h, sl in enumerate(heads)]
            dv = [_dot_tn(scores[h], dob[:, sl]) + dv_inter[h] for h, sl in enumerate(heads)]
            carry_in = cat([z["e_last"][:, sl] * _colsum(dst[h] * st[h]) for h, sl in enumerate(heads)])
            for h, sl in enumerate(heads):
                ds_scr[h] = dst[h] * z["e_last"][:, sl] + grown[h]
            dqt, dkt, dqg, dkg = cat(dqt), cat(dkt), cat(dqg), cat(dkg)
            carry_in += _colsum(dkg * kg_f)
            dq = dqt * z["e_q"] + dqg * z["e_in"]
            dk = dkt * z["e_k"] + dkg * z["e_out"]
            dgate = (qt.astype(F32) * dqt - kt.astype(F32) * dkt) + (qg_f * dqg - kg_f * dkg)
            dlogf = _tri_sum(triu_b, dgate) + carry_in
            df = dlogf / z["f"] - dk
            sg = z["sg"]
            sq = z["sq"]
            acc_scr[...] += _colsum(df * (1.0 - sg))
            dqfi_ref[rows, 0:D_MODEL] = (dq * (sq * (1.0 + qr * (1.0 - sq)))).astype(BF16)
            dqfi_ref[rows, D_MODEL:2 * D_MODEL] = (df * (1.0 - lb) * sg * (1.0 - sg)).astype(BF16)
            dqfi_ref[rows, 2 * D_MODEL:3 * D_MODEL] = cat(dv).astype(BF16)
            return carry

        lax.fori_loop(0, n_local, chunk, 0, unroll=2)
        d0 = acc_scr[...] * lb * (1.0 - lb)
        dlb_ref[0:1, :] = d0
        dlb_ref[1:2, :] = -d0

    return _hosted(
        riders, body, name="hgrn_bwd", grid=(n_tiles,),
        in_specs=[rev(0), rev(1), rev(2), _resident((2, D_MODEL)),
                  pl.BlockSpec((n_local, HEADS, HEAD_DIM, HEAD_DIM), lambda i: (n_tiles - 1 - i, 0, 0, 0)),
                  rev(0)],
        out_specs=[pl.BlockSpec((tt, 3 * D_MODEL), lambda i: (n_tiles - 1 - i, 0)), _acc((2, D_MODEL))],
        out_shape=[jax.ShapeDtypeStruct((t, 3 * D_MODEL), BF16), jax.ShapeDtypeStruct((2, D_MODEL), F32)],
        scratch_shapes=[pltpu.VMEM((HEADS, HEAD_DIM, HEAD_DIM), F32), pltpu.VMEM((1, D_MODEL), F32)],
        compiler_params=_params(("arbitrary",)),
    )(main, main, main, lb_raw, states, do)


def _head_norm(o):
    rs, ns = [], []
    for h in range(HEADS):
        oh = o[:, h * HEAD_DIM:(h + 1) * HEAD_DIM]
        r, n = _rms(oh)
        rs.append(jnp.broadcast_to(r, oh.shape))
        ns.append(n)
    return jnp.concatenate(rs, axis=1), jnp.concatenate(ns, axis=1)


def _head_norm_bwd(dn, n, r):
    outs = []
    for h in range(HEADS):
        sl = slice(h * HEAD_DIM, (h + 1) * HEAD_DIM)
        outs.append(_rms_bwd(dn[:, sl], n[:, sl], r[:, sl]))
    return jnp.concatenate(outs, axis=1)


def _window_counts(first_row, tm):
    pos = (first_row + 1 + lax.broadcasted_iota(jnp.int32, (tm, 1), 0)).astype(F32)
    return [jnp.minimum(pos, float(w)) for w in POOL_WINDOWS]


def _post_fwd(o, main, pool_r, gates, x1, onorm, pool_w, pool_scale, sqw, wbw, tm, riders=()):
    t = o.shape[0]
    ext_rows = tm + POOL_HALO

    def body(o_ref, og_ref, u_ref, gt_ref, x1_ref, on_ref, pw_ref, ps_ref, wa_ref, wout_ref, wb_ref,
             x2_ref, ya_ref, yb_ref, pooled_ref, ext):
        i = pl.program_id(0)

        @pl.when(i == 0)
        def _():
            ext[0:POOL_HALO, :] = jnp.zeros((POOL_HALO, POOL_WIDTH), F32)

        _, n = _head_norm(o_ref[...])
        og = og_ref[...]
        oa = (n * on_ref[...] * (og * _sigmoid(og))).astype(BF16)
        ya = _dot(oa, wa_ref[...])

        u = u_ref[...]
        ext[POOL_HALO:ext_rows, :] = u
        e = ext[...]
        counts = _window_counts(i * tm, tm)
        pooled = []
        for gidx, w in enumerate(POOL_WINDOWS):
            s = e[:, gidx * HEAD_DIM:(gidx + 1) * HEAD_DIM]
            shift = 1
            while shift < w:
                s = s + pltpu.roll(s, shift, axis=0)
                shift *= 2
            pooled.append(s[POOL_HALO:, :] / counts[gidx] - u[:, gidx * HEAD_DIM:(gidx + 1) * HEAD_DIM])
        ext[0:POOL_HALO, :] = ext[tm:ext_rows, :]
        pooled_b = [pg.astype(BF16) for pg in pooled]
        pooled_ref[...] = jnp.concatenate(pooled_b, axis=1)
        mixed = jnp.concatenate([_dot(pooled_b[gidx], pw_ref[gidx].astype(BF16)) for gidx in range(len(POOL_WINDOWS))],
                                axis=1) * ps_ref[...]
        mixed_b = mixed.astype(BF16)
        yb = jnp.concatenate([_dot(mixed_b, wb_ref[j]) for j in range(N_CHIPS)], axis=1)

        gt = gt_ref[...]
        y = _sigmoid(gt[:, :D_MODEL]) * ya + _sigmoid(gt[:, D_MODEL:]) * yb
        x2_ref[...] = x1_ref[...] + _dot(y.astype(BF16), wout_ref[...])
        ya_ref[...] = ya.astype(BF16)
        yb_ref[...] = yb.astype(BF16)

    return _hosted(
        riders, body, name="post_fwd", grid=(t // tm,),
        in_specs=[_rows(tm, D_MODEL), _rows(tm, D_MODEL, 3), _rows(tm, POOL_WIDTH), _rows(tm, GATE_COLS), _rows(tm, D_MODEL),
                  _resident((1, D_MODEL)), _resident(pool_w.shape), _resident((1, POOL_WIDTH)),
                  _pick(sqw.shape, 0), _pick(sqw.shape, 1), _resident(wbw.shape)],
        out_specs=[_rows(tm, D_MODEL), _rows(tm, D_MODEL), _rows(tm, D_MODEL), _rows(tm, POOL_WIDTH)],
        out_shape=[jax.ShapeDtypeStruct((t, D_MODEL), F32), jax.ShapeDtypeStruct((t, D_MODEL), BF16),
                   jax.ShapeDtypeStruct((t, D_MODEL), BF16), jax.ShapeDtypeStruct((t, POOL_WIDTH), BF16)],
        scratch_shapes=[pltpu.VMEM((ext_rows, POOL_WIDTH), F32)],
        compiler_params=_params(("arbitrary",)),
    )(o, main, pool_r, gates, x1, onorm, pool_w, pool_scale, sqw, sqw, wbw)


def _post_bwd(dx2, o, main, gates, ya, yb, pooled, onorm, pool_w, pool_scale, sqw, wbw, tm, riders=()):
    t = o.shape[0]
    n_tiles = t // tm
    ext_rows = tm + POOL_HALO
    n_groups = len(POOL_WINDOWS)

    def rev(cols, col_block=0):
        return pl.BlockSpec((tm, cols), lambda i: (n_tiles - 1 - i, col_block))

    def body(dx2_ref, o_ref, og_ref, gt_ref, ya_ref, yb_ref, pooled_ref, on_ref, pw_ref, ps_ref, wa_ref, wout_ref, wb_ref,
             do_ref, dog_ref, du_ref, dgt_ref, dwa_ref, dwout_ref, dwb_ref, dpw_ref, dps_ref, don_ref, ext):
        i = pl.program_id(0)

        @pl.when(i == 0)
        def _():
            ext[tm:ext_rows, :] = jnp.zeros((POOL_HALO, POOL_WIDTH), F32)
            for ref in (dwa_ref, dwout_ref, dwb_ref, dpw_ref, dps_ref, don_ref):
                ref[...] = jnp.zeros_like(ref)

        groups = [slice(gidx * HEAD_DIM, (gidx + 1) * HEAD_DIM) for gidx in range(n_groups)]
        shards = [slice(j * 256, (j + 1) * 256) for j in range(N_CHIPS)]
        dx2b = dx2_ref[...].astype(BF16)
        dy = _dot_nt(dx2b, wout_ref[...])
        pooled_b = pooled_ref[...]
        pm = jnp.concatenate([_dot(pooled_b[:, g], pw_ref[gidx].astype(BF16)) for gidx, g in enumerate(groups)], axis=1)
        gt = gt_ref[...]
        sga = _sigmoid(gt[:, :D_MODEL])
        sgb = _sigmoid(gt[:, D_MODEL:])
        ya = ya_ref[...].astype(F32)
        yb = yb_ref[...].astype(F32)
        y = (sga * ya + sgb * yb).astype(BF16)
        dya = (dy * sga).astype(BF16)
        dyb = (dy * sgb).astype(BF16)
        dgt_ref[:, :D_MODEL] = (dy * ya * sga * (1.0 - sga)).astype(BF16)
        dgt_ref[:, D_MODEL:] = (dy * yb * sgb * (1.0 - sgb)).astype(BF16)
        dwout_ref[...] += _dot_tn(y, dx2b)
        doa = _dot_nt(dya, wa_ref[...])
        dmixed = _dot_nt(dyb[:, shards[0]], wb_ref[0])
        for j in range(1, N_CHIPS):
            dmixed += _dot_nt(dyb[:, shards[j]], wb_ref[j])
        r, n = _head_norm(o_ref[...])
        onv = on_ref[...]
        og = og_ref[...]
        sog = _sigmoid(og)
        silu_og = og * sog
        normed = n * onv
        oa = (normed * silu_og).astype(BF16)
        dog_ref[...] = (doa * normed * (sog * (1.0 + og * (1.0 - sog)))).astype(BF16)
        dnormed = doa * silu_og
        don_ref[...] += _colsum(dnormed * n)
        do_ref[...] = _head_norm_bwd(dnormed * onv, n, r)
        psv = ps_ref[...]
        mixed_b = (pm * psv).astype(BF16)
        dps_ref[...] += _colsum(dmixed * pm)
        dpm = (dmixed * psv).astype(BF16)
        dwa_ref[...] += _dot_tn(oa, dya)
        for j in range(N_CHIPS):
            dwb_ref[j] += _dot_tn(mixed_b, dyb[:, shards[j]])
        counts = _window_counts((n_tiles - 1 - i) * tm, tm)
        dpooled = []
        for gidx, g in enumerate(groups):
            dpw_ref[gidx] += _dot_tn(pooled_b[:, g], dpm[:, g])
            dpooled.append(_dot_nt(dpm[:, g], pw_ref[gidx].astype(BF16)))
        ext[0:tm, :] = jnp.concatenate([dpooled[gidx] / counts[gidx] for gidx in range(n_groups)], axis=1)
        e = ext[...]
        du = []
        for gidx, w in enumerate(POOL_WINDOWS):
            s = e[:, gidx * HEAD_DIM:(gidx + 1) * HEAD_DIM]
            shift = 1
            while shift < w:
                s = s + pltpu.roll(s, ext_rows - shift, axis=0)
                shift *= 2
            du.append(s[:tm, :] - dpooled[gidx])
        ext[tm:ext_rows, :] = ext[0:POOL_HALO, :]
        du_ref[...] = jnp.concatenate(du, axis=1).astype(BF16)

    wa_shape = (D_MODEL, D_MODEL)
    return _hosted(
        riders, body, name="post_bwd", grid=(n_tiles,),
        in_specs=[rev(D_MODEL), rev(D_MODEL), rev(D_MODEL, 3), rev(GATE_COLS), rev(D_MODEL), rev(D_MODEL), rev(POOL_WIDTH),
                  _resident((1, D_MODEL)), _resident(pool_w.shape), _resident((1, POOL_WIDTH)),
                  _pick(sqw.shape, 0), _pick(sqw.shape, 1), _resident(wbw.shape)],
        out_specs=[rev(D_MODEL), rev(D_MODEL), rev(POOL_WIDTH), rev(GATE_COLS),
                   _acc(wa_shape), _acc(wa_shape), _acc(wbw.shape), _acc(pool_w.shape), _acc((1, POOL_WIDTH)),
                   _acc((1, D_MODEL))],
        out_shape=[jax.ShapeDtypeStruct((t, D_MODEL), F32), jax.ShapeDtypeStruct((t, D_MODEL), BF16),
                   jax.ShapeDtypeStruct((t, POOL_WIDTH), BF16), jax.ShapeDtypeStruct((t, GATE_COLS), BF16),
                   jax.ShapeDtypeStruct(wa_shape, F32), jax.ShapeDtypeStruct(wa_shape, F32),
                   jax.ShapeDtypeStruct(wbw.shape, F32), jax.ShapeDtypeStruct(pool_w.shape, F32),
                   jax.ShapeDtypeStruct((1, POOL_WIDTH), F32), jax.ShapeDtypeStruct((1, D_MODEL), F32)],
        scratch_shapes=[pltpu.VMEM((ext_rows, POOL_WIDTH), F32)],
        compiler_params=_params(("arbitrary",)),
    )(dx2, o, main, gates, ya, yb, pooled, onorm, pool_w, pool_scale, sqw, sqw, wbw)


def _tail(x3, p, target, g_ple, g_post, g_final, sqw, wpw, tm, riders=()):
    t = x3.shape[0]
    pd = p.shape[1]

    def body(x_ref, p_ref, tg_ref, g4_ref, g5_ref, g6_ref, wg_ref, wp_ref,
             dx_ref, loss_ref, dwg_ref, dwp_ref, dg4_ref, dg5_ref, dg6_ref):
        @pl.when(pl.program_id(0) == 0)
        def _():
            for ref in (loss_ref, dwg_ref, dwp_ref, dg4_ref, dg5_ref, dg6_ref):
                ref[...] = jnp.zeros_like(ref)

        x3v = x_ref[...]
        g4, g5, g6 = g4_ref[...], g5_ref[...], g6_ref[...]
        r4, n4 = _rms(x3v)
        h4 = (n4 * g4).astype(BF16)
        gate = _sigmoid(_dot(h4, wg_ref[...]))
        pb = p_ref[...].astype(BF16)
        r5, n5 = _rms(jnp.concatenate([_dot(pb, wp_ref[j]) for j in range(N_CHIPS)], axis=1))
        emb = n5 * g5
        r6, n6 = _rms(x3v + gate * emb)
        diff = n6 * g6 - tg_ref[...]
        loss_ref[...] += 0.5 * jnp.sum(jnp.mean(diff * diff, axis=-1, keepdims=True), axis=0, keepdims=True)
        dout = diff * (1.0 / D_MODEL)
        dg6_ref[...] += _colsum(dout * n6)
        dx4 = _rms_bwd(dout * g6, n6, r6)
        demb = dx4 * gate
        dg5_ref[...] += _colsum(demb * n5)
        dpre = _rms_bwd(demb * g5, n5, r5).astype(BF16)
        for j in range(N_CHIPS):
            dwp_ref[j] += _dot_tn(pb, dpre[:, j * pd:(j + 1) * pd])
        dz = (dx4 * emb * gate * (1.0 - gate)).astype(BF16)
        dwg_ref[...] += _dot_tn(h4, dz)
        dh4 = _dot_nt(dz, wg_ref[...])
        dg4_ref[...] += _colsum(dh4 * n4)
        dx_ref[...] = dx4 + _rms_bwd(dh4 * g4, n4, r4)

    sq_shape = (D_MODEL, D_MODEL)
    vec = (1, D_MODEL)
    return _hosted(
        riders, body, name="tail", grid=(t // tm,),
        in_specs=[_rows(tm, D_MODEL), _rows(tm, pd), _rows(tm, D_MODEL), _resident(vec), _resident(vec), _resident(vec),
                  _pick(sqw.shape, 2), _resident(wpw.shape)],
        out_specs=[_rows(tm, D_MODEL), _acc((1, 1)), _acc(sq_shape), _acc(wpw.shape), _acc(vec), _acc(vec), _acc(vec)],
        out_shape=[jax.ShapeDtypeStruct((t, D_MODEL), F32), jax.ShapeDtypeStruct((1, 1), F32),
                   jax.ShapeDtypeStruct(sq_shape, F32), jax.ShapeDtypeStruct(wpw.shape, F32),
                   jax.ShapeDtypeStruct(vec, F32), jax.ShapeDtypeStruct(vec, F32), jax.ShapeDtypeStruct(vec, F32)],
        compiler_params=_params(("arbitrary",)),
    )(x3, p, target, g_ple, g_post, g_final, sqw, wpw)


def _position():
    return lax.axis_index("x"), lax.axis_index("y"), lax.axis_index("c")


def _other_chips(x, y):
    return [(1 - x, y), (x, 1 - y), (1 - x, 1 - y)]


def _remote(src, dst, send_sems, recv_sems, k, device):
    return pltpu.make_async_remote_copy(src_ref=src, dst_ref=dst, send_sem=send_sems.at[k], recv_sem=recv_sems.at[k],
                                        device_id=device, device_id_type=MESH)


def _gather_rider(shards, forward_at):
    n = len(shards)

    def copies(ins, outs, sems):
        send_sems, recv_sems, local_sems = sems
        x, y, c = _position()
        mine = 2 * x + y
        local = [pltpu.make_async_copy(ins[a], outs[a].at[:, mine], local_sems.at[a]) for a in range(n)]
        first, passed, arriving = [], [], []
        for k, (cx, cy) in enumerate(_other_chips(x, y)):
            theirs = 2 * cx + cy
            for a in range(n):
                first.append(_remote(ins[a].at[:, c], outs[a].at[:, mine, c], send_sems, recv_sems, k * n + a, (cx, cy, c)))
                block = outs[a].at[:, theirs, c]
                passed.append(_remote(block, block, send_sems, recv_sems, (3 + k) * n + a, (x, y, 1 - c)))
                other = outs[a].at[:, theirs, 1 - c]
                arriving.append(_remote(other, other, send_sems, recv_sems, (3 + k) * n + a, (x, y, 1 - c)))
        return local, first, passed, arriving

    def begin(ins, outs, sems):
        local, first, _, _ = copies(ins, outs, sems)
        for cp in local + first:
            cp.start()

    def forward(ins, outs, sems):
        _, first, passed, _ = copies(ins, outs, sems)
        for got, cp in zip(first, passed):
            got.wait_recv()
            cp.start()

    def finish(ins, outs, sems):
        local, first, passed, arriving = copies(ins, outs, sems)
        for cp in arriving:
            cp.wait_recv()
        for cp in first + passed:
            cp.wait_send()
        for cp in local:
            cp.wait()

    return _Rider(shards, [jax.ShapeDtypeStruct((s.shape[0], N_CHIPS) + s.shape[1:], s.dtype) for s in shards],
                  [pltpu.SemaphoreType.DMA((6 * n,)), pltpu.SemaphoreType.DMA((6 * n,)), pltpu.SemaphoreType.DMA((n,))],
                  [(0, begin), (forward_at, forward), (1, finish)])


def _exchange_rider(arrays, out_shape, n_copies, transfers, n_local=0):
    def copies(ins, outs, sems):
        send_sems, recv_sems, local_sems = sems
        remote, local = transfers(ins, outs)
        return ([_remote(src, dst, send_sems, recv_sems, i, dev) for i, (src, dst, dev) in enumerate(remote)],
                [pltpu.make_async_copy(src, dst, local_sems.at[i]) for i, (src, dst) in enumerate(local)])

    def begin(ins, outs, sems):
        remote, local = copies(ins, outs, sems)
        for cp in remote + local:
            cp.start()

    def finish(ins, outs, sems):
        remote, local = copies(ins, outs, sems)
        for cp in remote:
            cp.wait_recv()
        for cp in remote:
            cp.wait_send()
        for cp in local:
            cp.wait()

    return _Rider(arrays, out_shape,
                  [pltpu.SemaphoreType.DMA((n_copies,)), pltpu.SemaphoreType.DMA((n_copies,)),
                   pltpu.SemaphoreType.DMA((max(n_local, 1),))],
                  [(0, begin), (1, finish)])


def _pair_rider(partials):
    def transfers(ins, outs):
        x, y, c = _position()
        return [(ins[a].at[:, :, 1 - c], outs[a], (x, y, 1 - c)) for a in range(len(partials))], []

    shapes = [jax.ShapeDtypeStruct(g.shape[:2] + g.shape[3:], g.dtype) for g in partials]
    return _exchange_rider(partials, shapes, len(partials), transfers)


def _chips_rider(sums):
    n = len(sums)

    def transfers(ins, outs):
        x, y, c = _position()
        return [(ins[a].at[:, 2 * cx + cy], outs[a].at[:, k], (cx, cy, c))
                for k, (cx, cy) in enumerate(_other_chips(x, y)) for a in range(n)], []

    shapes = [jax.ShapeDtypeStruct((q.shape[0], 3) + q.shape[2:], q.dtype) for q in sums]
    return _exchange_rider(sums, shapes, 3 * n, transfers)


def _share_rider(halves):
    def transfers(ins, outs):
        x, y, c = _position()
        return [(ins[a], outs[a], (x, y, 1 - c)) for a in range(len(halves))], []

    return _exchange_rider(halves, [jax.ShapeDtypeStruct(h.shape, h.dtype) for h in halves], len(halves), transfers)


def _small_rider(pack):
    flips = [(fx, fy, fc) for fx in (0, 1) for fy in (0, 1) for fc in (0, 1)][1:]

    def transfers(ins, outs):
        x, y, c = _position()
        slot = outs[0].at[4 * x + 2 * y + c]
        flip = lambda v, f: v + f - 2 * v * f
        return [(ins[0], slot, (flip(x, fx), flip(y, fy), flip(c, fc))) for fx, fy, fc in flips], [(ins[0], slot)]

    return _exchange_rider([pack], [jax.ShapeDtypeStruct((8,) + pack.shape, pack.dtype)], len(flips), transfers, n_local=1)


def _alone(rider, name):
    return _hosted([rider], lambda: None, name=name, in_specs=[], out_specs=[], out_shape=[])()[1][0]


def _add_pair(mine, theirs, c, tag):
    n = len(mine)

    def body(c_ref, *refs):
        for a in range(n):
            refs[2 * n + a][...] = (refs[2 * a][...] + refs[2 * a + 1][...]).astype(BF16)

    in_specs, out_specs = [], []
    for got in theirs:
        l, _, hr, cols = got.shape
        in_specs += [pl.BlockSpec((l, None, None, hr, cols), lambda j, s: (0, j, s[0], 0, 0)),
                     pl.BlockSpec((l, None, hr, cols), lambda j, s: (0, j, 0, 0))]
        out_specs.append(pl.BlockSpec((l, None, hr, cols), lambda j, s: (0, j, 0, 0)))
    return pl.pallas_call(
        body, name=f"add_pair_{tag}",
        grid_spec=pltpu.PrefetchScalarGridSpec(num_scalar_prefetch=1, grid=(N_CHIPS,), in_specs=in_specs, out_specs=out_specs),
        out_shape=[jax.ShapeDtypeStruct(got.shape, BF16) for got in theirs],
        compiler_params=_params(("parallel",)),
    )(c.reshape(1), *[a for pair in zip(mine, theirs) for a in pair])


def _add_chips(parts, received, mine, tag):
    n = len(parts)

    def body(j_ref, *refs):
        for a in range(n):
            acc = refs[2 * a][...].astype(F32)
            for k in range(3):
                acc += refs[2 * a + 1][:, k].astype(F32)
            refs[2 * n + a][...] = acc

    in_specs, out_specs, out_shape = [], [], []
    for got in received:
        l, _, hr, cols = got.shape
        in_specs += [pl.BlockSpec((l, None, hr // 2, cols), lambda i, s: (0, s[0], i, 0)),
                     pl.BlockSpec((l, 3, hr // 2, cols), lambda i, s: (0, 0, i, 0))]
        out_specs.append(pl.BlockSpec((l, hr // 2, cols), lambda i, s: (0, i, 0)))
        out_shape.append(jax.ShapeDtypeStruct((l, hr, cols), F32))
    return pl.pallas_call(
        body, name=f"add_chips_{tag}",
        grid_spec=pltpu.PrefetchScalarGridSpec(num_scalar_prefetch=1, grid=(2,), in_specs=in_specs, out_specs=out_specs),
        out_shape=out_shape,
        compiler_params=_params(("parallel",)),
    )(mine.reshape(1), *[a for pair in zip(parts, received) for a in pair])


def _adam_update(w, g, m, v):
    m2 = ADAM_B1 * m + (1.0 - ADAM_B1) * g
    v2 = ADAM_B2 * v + (1.0 - ADAM_B2) * jnp.square(g)
    m_hat = m2 / (1.0 - ADAM_B1 ** ADAM_STEP)
    v_hat = v2 / (1.0 - ADAM_B2 ** ADAM_STEP)
    return -ADAM_LR * (m_hat / (jnp.sqrt(v_hat) + ADAM_EPS) + ADAM_WD * w), m2, v2


def _adamw_group(items, tag, riders=()):
    n = len(items)

    def body(*refs):
        ins, outs = refs[:5 * n], refs[5 * n:]
        mine = pl.program_id(0) == lax.axis_index("c")
        for a in range(n):
            w_ref, own_ref, other_ref, m_ref, v_ref = ins[5 * a:5 * a + 5]
            g_ref, d_ref, nm_ref, nv_ref = outs[4 * a:4 * a + 4]
            gv = jnp.where(mine, own_ref[...], other_ref[...])
            g_ref[...] = gv
            d_ref[...], nm_ref[...], nv_ref[...] = _adam_update(w_ref[...], gv, m_ref[...], v_ref[...])

    in_specs, out_specs, out_shape, args = [], [], [], []
    for w, own, other, m, v in items:
        _, hr, cols = w.shape
        tr = hr // ADAM_BLOCKS
        full = pl.BlockSpec((None, tr, cols), lambda h, i: (h, i, 0))
        half = pl.BlockSpec((tr, cols), lambda h, i: (i, 0))
        in_specs += [full, half, half, full, full]
        out_specs += [full] * 4
        out_shape += [jax.ShapeDtypeStruct((2, hr, cols), F32)] * 4
        args += [w, own, other, m, v]
    outs, extras = _hosted(riders, body, name=f"adamw_{tag}", grid=(2, ADAM_BLOCKS), in_specs=in_specs, out_specs=out_specs,
                           out_shape=out_shape, compiler_params=_params(("arbitrary", "arbitrary")))(*args)
    return [outs[4 * a:4 * a + 4] for a in range(n)], extras


def _adamw_small(w, gathered, m, v):
    def body(w_ref, g_ref, m_ref, v_ref, sum_ref, d_ref, nm_ref, nv_ref):
        gv = g_ref[0]
        for i in range(1, g_ref.shape[0]):
            gv += g_ref[i]
        sum_ref[...] = gv
        d_ref[...], nm_ref[...], nv_ref[...] = _adam_update(w_ref[...], gv, m_ref[...], v_ref[...])

    shape = jax.ShapeDtypeStruct(w.shape, F32)
    return pl.pallas_call(body, name="adamw_small", out_shape=[shape] * 4, compiler_params=_params())(w, gathered, m, v)


VECTOR_PARAMS = ("ffn1_norm", "mix_norm", "hgrn_lb", "hgrn_onorm", "ffn2_norm", "ple_norm", "ple_post_norm", "final_norm",
                 "pool_scale")
ALL_PARAMS = ("ffn1_norm", "ffn1_w1", "ffn1_w3", "ffn1_w2", "mix_norm", "w_in", "hgrn_lb", "hgrn_onorm", "w_branch_a",
              "pool_w", "pool_scale", "w_branch_b", "w_out", "ffn2_norm", "ffn2_w1", "ffn2_w3", "ffn2_w2", "ple_norm",
              "ple_w_gate", "ple_w_proj", "ple_post_norm", "final_norm")
TILE_ROWS = 8


def _pack_small(values):
    parts = []
    for name in VECTOR_PARAMS:
        a = values[name].reshape(-1, values[name].shape[-1])
        parts.append(jnp.pad(a, ((0, TILE_ROWS - a.shape[0]), (0, D_MODEL - a.shape[1]))))
    parts.append(values["pool_w"].reshape(-1, D_MODEL))
    return jnp.concatenate(parts, axis=0)


def _unpack_small(pack, shapes):
    out = {}
    for i, name in enumerate(VECTOR_PARAMS):
        shape = shapes[name]
        rows = 1 if len(shape) == 1 else shape[0]
        out[name] = pack[i * TILE_ROWS:i * TILE_ROWS + rows, :shape[-1]].reshape(shape)
    out["pool_w"] = pack[len(VECTOR_PARAMS) * TILE_ROWS:].reshape(shapes["pool_w"])
    return out


def _halved(a, lead):
    return a.reshape(lead, 2, -1, a.shape[-1])


def _shard_halves(a, lead):
    return a.reshape(lead, N_CHIPS, 2, -1, a.shape[-1])


REDUCED_TRANSPOSED = ("ffn1_w1", "ffn1_w3", "ffn2_w1", "ffn2_w3")


def _entries(arrays):
    return [a[i] for a in arrays for i in range(a.shape[0])]


def _adam_items(names, own, other, w, m, v):
    items = []
    for name, g_own, g_other in zip(names, _entries(own), _entries(other)):
        view = (lambda a: _halved(a[0].T, 1)[0]) if name in REDUCED_TRANSPOSED else (lambda a: _halved(a, 1)[0])
        items.append((view(w[name]), g_own, g_other, view(m[name]), view(v[name])))
    return items


def _adam_store(names, results, w, out):
    for name, res in zip(names, results):
        shape = w[name].shape
        if name in REDUCED_TRANSPOSED:
            back = [a.reshape(shape[2], shape[1]).T.reshape(shape) for a in res]
        else:
            back = [a.reshape(shape) for a in res]
        out["grad"][name], out["delta"][name], out["new_m"][name], out["new_v"][name] = back


def kernel(x, p, ffn1_norm, ffn1_w1, ffn1_w3, ffn1_w2, mix_norm, w_in, hgrn_lb, hgrn_onorm, w_branch_a, pool_w, pool_scale, w_branch_b, w_out, ffn2_norm, ffn2_w1, ffn2_w3, ffn2_w2, ple_norm, ple_w_gate, ple_w_proj, ple_post_norm, final_norm, loss_target, m_ffn1_norm, m_ffn1_w1, m_ffn1_w3, m_ffn1_w2, m_mix_norm, m_w_in, m_hgrn_lb, m_hgrn_onorm, m_w_branch_a, m_pool_w, m_pool_scale, m_w_branch_b, m_w_out, m_ffn2_norm, m_ffn2_w1, m_ffn2_w3, m_ffn2_w2, m_ple_norm, m_ple_w_gate, m_ple_w_proj, m_ple_post_norm, m_final_norm, v_ffn1_norm, v_ffn1_w1, v_ffn1_w3, v_ffn1_w2, v_mix_norm, v_w_in, v_hgrn_lb, v_hgrn_onorm, v_w_branch_a, v_pool_w, v_pool_scale, v_w_branch_b, v_w_out, v_ffn2_norm, v_ffn2_w1, v_ffn2_w3, v_ffn2_w2, v_ple_norm, v_ple_w_gate, v_ple_w_proj, v_ple_post_norm, v_final_norm):
    args = dict(locals())
    w = {name: args[name] for name in ALL_PARAMS}
    m = {name: args["m_" + name] for name in ALL_PARAMS}
    v = {name: args["v_" + name] for name in ALL_PARAMS}
    cx, cy, cc = _position()
    chip = (2 * cx + cy).astype(jnp.int32)
    core = cc.astype(jnp.int32)
    xs, ps, target = x[0], p[0, 0], loss_target[0]
    t = xs.shape[0]
    tm = min(256, t)
    tm_ffn = min(512, t)
    tt = min(512, t)
    tk = min(2048, t)
    small = {name: w[name] for name in VECTOR_PARAMS}
    small["final_norm"] = w["final_norm"].reshape(1, D_MODEL)
    pool_w0 = w["pool_w"][0]

    ffn_shard = lambda i: _halved(jnp.stack([w[f"ffn{i}_w1"][0].T, w[f"ffn{i}_w3"][0].T, w[f"ffn{i}_w2"][0]]).astype(BF16), 3)
    sq_shard = _halved(jnp.stack([w["w_branch_a"][0], w["w_out"][0], w["ple_w_gate"][0]]).astype(BF16), 3)
    win_shard, wb_shard, wp_shard = (_halved(w[n].astype(BF16), 1) for n in ("w_in", "w_branch_b", "ple_w_proj"))

    (ffn1w,) = _alone(_gather_rider([ffn_shard(1)], 0.5), "gather_ffn1")
    ffn1w = ffn1w.reshape(3, D_FF, D_MODEL)
    (x1, a1, b1), ((winw,),) = _ffn_fwd(xs, small["ffn1_norm"], ffn1w, 1, tm_ffn, [_gather_rider([win_shard], 0.6)])
    winw = winw.reshape(N_CHIPS, D_MODEL, SHARD_IN_COLS)
    (main, pool_r, gates), ((sqw, wbw, wpw),) = _mix_fwd(x1, small["mix_norm"], winw, tm,
                                                          [_gather_rider([sq_shard, wb_shard, wp_shard], 0.5)])
    sqw = sqw.reshape(3, D_MODEL, D_MODEL)
    wbw = wbw.reshape(N_CHIPS, POOL_WIDTH, -1)
    wpw = wpw.reshape(N_CHIPS, ps.shape[1], -1)
    (o, states), ((ffn2w,),) = _hgrn_fwd(main, small["hgrn_lb"], tt, [_gather_rider([ffn_shard(2)], 0.7)])
    ffn2w = ffn2w.reshape(3, D_FF, D_MODEL)
    (x2, ya, yb, pooled), _ = _post_fwd(o, main, pool_r, gates, x1, small["hgrn_onorm"], pool_w0, small["pool_scale"], sqw,
                                       wbw, tm)
    (x3, a2, b2), _ = _ffn_fwd(x2, small["ffn2_norm"], ffn2w, 2, tm_ffn)
    (dx3, loss, d_wg, d_wp, d_ple, d_post, d_final), _ = _tail(
        x3, ps, target, small["ple_norm"], small["ple_post_norm"], small["final_norm"], sqw, wpw, tm_ffn)
    loss = lax.psum(loss[0, 0], ("x", "y", "c"))

    add_pairs = lambda parts, got, group: _add_pair(parts, got, core, group)
    add_chips = lambda sums, got, group: _add_chips(sums, got, chip, group)
    names1 = ("ffn2_w1", "ffn2_w3", "ffn2_w2", "ple_w_gate", "ple_w_proj")
    names2 = ("w_branch_a", "w_out", "w_branch_b")
    names3 = ("w_in",)
    names4 = ("ffn1_w1", "ffn1_w3")
    names5 = ("ffn1_w2",)
    tags1, tags2, tags3, tags4, tags5 = "ffn2", "branches", "w_in", "ffn1_in", "ffn1_out"

    (dx2, dab2, s2, h3, dxh2, d_ffn2_norm), _ = _ffn_bwd(dx3, x2, small["ffn2_norm"], a2, b2, ffn2w, 2, tm)
    (d_w13_2,), _ = _wgrad(dab2, h3, WGRAD_IN_BLOCKS, "wgrad_ffn2_in", tk)
    (d_w2_2,), _ = _wgrad(s2, dxh2, WGRAD_OUT_BLOCKS, "wgrad_ffn2_out", tk)
    part1 = [_shard_halves(d_w13_2, 2), _shard_halves(d_w2_2, 1), _shard_halves(d_wg, 1), _shard_halves(d_wp, 1)]
    (do, dog, du, dgates, d_wa, d_wout, d_wb, d_pool_w, d_pool_scale, d_onorm), (sib1,) = _post_bwd(
        dx2, o, main, gates, ya, yb, pooled, small["hgrn_onorm"], pool_w0, small["pool_scale"], sqw, wbw, tm,
        [_pair_rider(part1)])
    sums1 = add_pairs(part1, sib1, tags1)
    part2 = [_shard_halves(d_wa, 1), _shard_halves(d_wout, 1), _shard_halves(d_wb, 1)]
    (dqfi, d_lb), (got1, sib2) = _hgrn_bwd(main, small["hgrn_lb"], states, do, tt, [_chips_rider(sums1), _pair_rider(part2)])
    own1 = add_chips(sums1, got1, tags1)
    sums2 = add_pairs(part2, sib2, tags2)
    (dx1, dproj, h2, d_mix_norm), (other1, got2) = _mix_bwd(dqfi, dog, du, dgates, dx2, x1, small["mix_norm"], winw, tm,
                                                            [_share_rider(own1), _chips_rider(sums2)])
    own2 = add_chips(sums2, got2, tags2)
    (d_win,), (other2,) = _wgrad_cols(h2, dproj, N_CHIPS, "wgrad_in", tk, [_share_rider(own2)])
    part3 = [_shard_halves(d_win, 1)]
    (dx, dab1, s1, h1, dxh1, d_ffn1_norm), _ = _ffn_bwd(dx1, xs, small["ffn1_norm"], a1, b1, ffn1w, 1, tm)
    vecs = dict(ffn1_norm=d_ffn1_norm, mix_norm=d_mix_norm, hgrn_lb=d_lb, hgrn_onorm=d_onorm, ffn2_norm=d_ffn2_norm,
                ple_norm=d_ple, ple_post_norm=d_post, final_norm=d_final, pool_scale=d_pool_scale, pool_w=d_pool_w)
    (d_w13_1,), (sib3, (small_all,)) = _wgrad(dab1, h1, WGRAD_IN_BLOCKS, "wgrad_ffn1_in", tk,
                                              [_pair_rider(part3), _small_rider(_pack_small(vecs))])
    sums3 = add_pairs(part3, sib3, tags3)
    part4 = [_shard_halves(d_w13_1, 2)]
    (d_w2_1,), (got3, sib4) = _wgrad(s1, dxh1, WGRAD_OUT_BLOCKS, "wgrad_ffn1_out", tk,
                                     [_chips_rider(sums3), _pair_rider(part4)])
    own3 = add_chips(sums3, got3, tags3)
    sums4 = add_pairs(part4, sib4, tags4)
    part5 = [_shard_halves(d_w2_1, 1)]

    out = dict(grad={}, delta={}, new_m={}, new_v={})
    results, (other3, got4, sib5) = _adamw_group(
        _adam_items(names1 + names2, own1 + own2, other1 + other2, w, m, v), "early",
        [_share_rider(own3), _chips_rider(sums4), _pair_rider(part5)])
    _adam_store(names1 + names2, results, w, out)
    own4 = add_chips(sums4, got4, tags4)
    sums5 = add_pairs(part5, sib5, tags5)
    results, (other4, got5) = _adamw_group(_adam_items(names3, own3, other3, w, m, v), "w_in",
                                           [_share_rider(own4), _chips_rider(sums5)])
    _adam_store(names3, results, w, out)
    own5 = add_chips(sums5, got5, tags5)
    other5 = _alone(_share_rider(own5), "share_last")
    results, _ = _adamw_group(_adam_items(names4 + names5, own4 + own5, other4 + other5, w, m, v), "ffn1")
    _adam_store(names4 + names5, results, w, out)

    shapes = {name: w[name].shape for name in VECTOR_PARAMS + ("pool_w",)}
    results = _adamw_small(_pack_small(w), small_all, _pack_small(m), _pack_small(v))
    for key, pack in zip(("grad", "delta", "new_m", "new_v"), results):
        out[key].update(_unpack_small(pack, shapes))

    return (loss, dx[None], *[out["grad"][n] for n in ALL_PARAMS], *[out["delta"][n] for n in ALL_PARAMS],
            *[out["new_m"][n] for n in ALL_PARAMS], *[out["new_v"][n] for n in ALL_PARAMS])
```

```python
import functools

import jax
import jax.numpy as jnp
from jax import lax
from jax.experimental import pallas as pl
from jax.experimental.pallas import tpu as pltpu

F32 = jnp.float32
BF16 = jnp.bfloat16
MESH = pl.DeviceIdType.MESH

D_MODEL = 1024
D_FF = 2816
HEADS = 8
HEAD_DIM = 128
POOL_WIDTH = 512
POOL_WINDOWS = (2, 4, 8, 16)
POOL_HALO = 16
N_CHIPS = 4
EPS = 1e-6
CHUNK = 64
MAIN_COLS = 4096
GATE_COLS = 2048
SHARD_IN_COLS = 1664

ADAM_LR = 0.001
ADAM_B1 = 0.9
ADAM_B2 = 0.999
ADAM_EPS = 1e-08
ADAM_WD = 0.01
ADAM_STEP = 10

VMEM_LIMIT = 56 * 1024 * 1024
WGRAD_IN_BLOCKS = 4
WGRAD_OUT_BLOCKS = 2
ADAM_BLOCKS = 4


def _params(semantics=None, vmem=VMEM_LIMIT):
    return pltpu.CompilerParams(dimension_semantics=semantics, vmem_limit_bytes=vmem)


def _dot(a, b):
    return jnp.dot(a, b, preferred_element_type=F32)


def _dot_nt(a, b):
    return lax.dot_general(a, b, (((1,), (1,)), ((), ())), preferred_element_type=F32)


def _dot_tn(a, b):
    return lax.dot_general(a, b, (((0,), (0,)), ((), ())), preferred_element_type=F32)


def _tri_sum(tri, x):
    hi = x.astype(BF16)
    lo = (x - hi.astype(F32)).astype(BF16)
    return _dot(tri, hi) + _dot(tri, lo)


def _sigmoid(x):
    return jax.nn.sigmoid(x)


def _resident(shape):
    zeros = (0,) * len(shape)
    return pl.BlockSpec(shape, lambda *_: zeros, pipeline_mode=pl.Buffered(1))


def _pick(shape, k):
    zeros = (0,) * (len(shape) - 1)
    return pl.BlockSpec((None,) + tuple(shape[1:]), lambda *_: (k,) + zeros, pipeline_mode=pl.Buffered(1))


def _rows(tm, cols, col_block=0):
    return pl.BlockSpec((tm, cols), lambda i: (i, col_block))


def _acc(shape):
    zeros = (0,) * len(shape)
    return pl.BlockSpec(shape, lambda *_: zeros)


def _rms(x):
    r = lax.rsqrt(jnp.mean(x * x, axis=-1, keepdims=True) + EPS)
    return r, x * r


def _rms_bwd(dn, n, r):
    return r * (dn - n * jnp.mean(dn * n, axis=-1, keepdims=True))


def _colsum(a):
    return jnp.sum(a, axis=0, keepdims=True)


ANY = pl.BlockSpec(memory_space=pl.ANY)


class _Rider:
    def __init__(self, inputs, out_shape, sems, phases):
        self.inputs, self.out_shape, self.sems, self.phases = list(inputs), list(out_shape), list(sems), list(phases)


def _hosted(riders, body, *, name, grid=(), in_specs, out_specs, out_shape, scratch_shapes=(), compiler_params=None):
    riders = [r for r in riders if r is not None]
    n_in, n_out, n_scr = len(in_specs), len(out_shape), len(scratch_shapes)
    n_steps = 1
    for g in grid:
        n_steps *= g

    def wrapped(*refs):
        pos = n_in
        ins = refs[:n_in]
        r_ins = []
        for r in riders:
            r_ins.append(refs[pos:pos + len(r.inputs)])
            pos += len(r.inputs)
        outs = refs[pos:pos + n_out]
        pos += n_out
        r_outs = []
        for r in riders:
            r_outs.append(refs[pos:pos + len(r.out_shape)])
            pos += len(r.out_shape)
        scr = refs[pos:pos + n_scr]
        pos += n_scr
        r_sems = []
        for r in riders:
            r_sems.append(refs[pos:pos + len(r.sems)])
            pos += len(r.sems)
        step = 0
        for axis in range(len(grid)):
            step = step * grid[axis] + pl.program_id(axis)

        def at_step(which, fn):
            if n_steps == 1:
                fn()
            else:
                pl.when(step == which)(fn)

        for r, ri, ro, rs in zip(riders, r_ins, r_outs, r_sems):
            for fraction, fn in r.phases:
                if fraction == 0:
                    at_step(0, functools.partial(fn, ri, ro, rs))
        body(*ins, *outs, *scr)
        for r, ri, ro, rs in zip(riders, r_ins, r_outs, r_sems):
            for fraction, fn in r.phases:
                if fraction > 0:
                    at_step(min(int(fraction * n_steps), n_steps - 1), functools.partial(fn, ri, ro, rs))

    call = pl.pallas_call(
        wrapped, name=name, grid=grid,
        in_specs=list(in_specs) + [ANY for r in riders for _ in r.inputs],
        out_specs=list(out_specs) + [ANY for r in riders for _ in r.out_shape],
        out_shape=list(out_shape) + [s for r in riders for s in r.out_shape],
        scratch_shapes=list(scratch_shapes) + [s for r in riders for s in r.sems],
        compiler_params=compiler_params)

    def run(*args):
        res = call(*args, *[a for r in riders for a in r.inputs])
        extras, pos = [], n_out
        for r in riders:
            extras.append(list(res[pos:pos + len(r.out_shape)]))
            pos += len(r.out_shape)
        return list(res[:n_out]), extras

    return run


def _ffn_fwd(x, g, ffnw, tag, tm, riders=(), base=None):
    t = x.shape[0]
    width = ffnw.shape[1]

    def body(*refs):
        x_ref, base_ref = refs[0], refs[0 if base is None else 1]
        g_ref, w1_ref, w3_ref, w2_ref, xo_ref, a_ref, b_ref = refs[1 if base is None else 2:]
        _, n = _rms(x_ref[...])
        h = (n * g_ref[...]).astype(BF16)
        a = _dot_nt(h, w1_ref[...])
        b = _dot_nt(h, w3_ref[...])
        s = (a * _sigmoid(a) * b).astype(BF16)
        xo_ref[...] = base_ref[...] + 0.5 * _dot(s, w2_ref[...])
        a_ref[...] = a.astype(BF16)
        b_ref[...] = b.astype(BF16)

    acts = [x] if base is None else [x, base]
    return _hosted(
        riders, body, name=f"ffn_fwd_{tag}", grid=(t // tm,),
        in_specs=[_rows(tm, D_MODEL)] * len(acts) + [_resident((1, D_MODEL)), _pick(ffnw.shape, 0), _pick(ffnw.shape, 1),
                                                     _pick(ffnw.shape, 2)],
        out_specs=[_rows(tm, D_MODEL), _rows(tm, width), _rows(tm, width)],
        out_shape=[jax.ShapeDtypeStruct((t, D_MODEL), F32), jax.ShapeDtypeStruct((t, width), BF16),
                   jax.ShapeDtypeStruct((t, width), BF16)],
        compiler_params=_params(("arbitrary",)),
    )(*acts, g, ffnw, ffnw, ffnw)


def _ffn_bwd(dxo, x, g, a, b, ffnw, tag, tm, riders=()):
    t = x.shape[0]
    n_pieces = len(a)

    def body(dxo_ref, x_ref, g_ref, *refs):
        a_refs, b_refs = refs[:n_pieces], refs[n_pieces:2 * n_pieces]
        w1_ref, w3_ref, w2_ref, dx_ref, dab_ref, s_ref, h_ref, dxh_ref, dg_ref = refs[2 * n_pieces:]
        whole = lambda pieces: jnp.concatenate([r[...] for r in pieces], axis=1).astype(F32)

        @pl.when(pl.program_id(0) == 0)
        def _():
            dg_ref[...] = jnp.zeros_like(dg_ref)

        xv = x_ref[...]
        gv = g_ref[...]
        r, n = _rms(xv)
        h_ref[...] = (n * gv).astype(BF16)
        dxo_v = dxo_ref[...]
        dxh = (0.5 * dxo_v).astype(BF16)
        dxh_ref[...] = dxh
        ds = _dot_nt(dxh, w2_ref[...])
        av = whole(a_refs)
        bv = whole(b_refs)
        sg = _sigmoid(av)
        silu = av * sg
        s_ref[...] = (silu * bv).astype(BF16)
        da = (ds * bv * (sg * (1.0 + av * (1.0 - sg)))).astype(BF16)
        db = (ds * silu).astype(BF16)
        dab_ref[:, :D_FF] = da
        dab_ref[:, D_FF:] = db
        dh = _dot(da, w1_ref[...]) + _dot(db, w3_ref[...])
        dg_ref[...] += _colsum(dh * n)
        dx_ref[...] = dxo_v + _rms_bwd(dh * gv, n, r)

    return _hosted(
        riders, body, name=f"ffn_bwd_{tag}", grid=(t // tm,),
        in_specs=[_rows(tm, D_MODEL), _rows(tm, D_MODEL), _resident((1, D_MODEL))] + [_rows(tm, p.shape[1]) for p in a + b]
        + [_pick(ffnw.shape, 0), _pick(ffnw.shape, 1), _pick(ffnw.shape, 2)],
        out_specs=[_rows(tm, D_MODEL), _rows(tm, 2 * D_FF), _rows(tm, D_FF), _rows(tm, D_MODEL), _rows(tm, D_MODEL),
                   _acc((1, D_MODEL))],
        out_shape=[jax.ShapeDtypeStruct((t, D_MODEL), F32), jax.ShapeDtypeStruct((t, 2 * D_FF), BF16),
                   jax.ShapeDtypeStruct((t, D_FF), BF16), jax.ShapeDtypeStruct((t, D_MODEL), BF16),
                   jax.ShapeDtypeStruct((t, D_MODEL), BF16), jax.ShapeDtypeStruct((1, D_MODEL), F32)],
        compiler_params=_params(("arbitrary",)),
    )(dxo, x, g, *a, *b, ffnw, ffnw, ffnw)


def _wgrad(xm, dy, out_blocks, name, tk, riders=()):
    t, m = xm.shape
    n = dy.shape[1]
    mb = m // out_blocks

    def body(x_ref, dy_ref, o_ref):
        @pl.when(pl.program_id(1) == 0)
        def _():
            o_ref[...] = jnp.zeros_like(o_ref)

        o_ref[...] += _dot_tn(x_ref[...], dy_ref[...])

    return _hosted(
        riders, body, name=name, grid=(out_blocks, t // tk),
        in_specs=[pl.BlockSpec((tk, mb), lambda j, k: (k, j)), pl.BlockSpec((tk, n), lambda j, k: (k, 0))],
        out_specs=[pl.BlockSpec((None, mb, n), lambda j, k: (j, 0, 0))],
        out_shape=[jax.ShapeDtypeStruct((out_blocks, mb, n), F32)],
        compiler_params=_params(("arbitrary", "arbitrary")),
    )(xm, dy)


def _wgrad_cols(xm, dy, out_blocks, name, tk, riders=()):
    t, m = xm.shape
    n = dy.shape[1]
    nb = n // out_blocks

    def body(x_ref, dy_ref, o_ref):
        @pl.when(pl.program_id(1) == 0)
        def _():
            o_ref[...] = jnp.zeros_like(o_ref)

        o_ref[...] += _dot_tn(x_ref[...], dy_ref[...])

    return _hosted(
        riders, body, name=name, grid=(out_blocks, t // tk),
        in_specs=[pl.BlockSpec((tk, m), lambda j, k: (k, 0)), pl.BlockSpec((tk, nb), lambda j, k: (k, j))],
        out_specs=[pl.BlockSpec((None, m, nb), lambda j, k: (j, 0, 0))],
        out_shape=[jax.ShapeDtypeStruct((out_blocks, m, nb), F32)],
        compiler_params=_params(("arbitrary", "arbitrary")),
    )(xm, dy)


def _mix_fwd(x1, g, winw, tm, riders=()):
    t = x1.shape[0]

    def body(x_ref, g_ref, w_ref, main_ref, pool_ref, gate_ref):
        _, n = _rms(x_ref[...])
        h = (n * g_ref[...]).astype(BF16)
        proj = jnp.concatenate([_dot(h, w_ref[j]) for j in range(N_CHIPS)], axis=1)
        main_ref[...] = proj[:, :MAIN_COLS]
        pool_ref[...] = proj[:, MAIN_COLS:MAIN_COLS + POOL_WIDTH]
        gate_ref[...] = proj[:, MAIN_COLS + POOL_WIDTH:]

    return _hosted(
        riders, body, name="mix_fwd", grid=(t // tm,),
        in_specs=[_rows(tm, D_MODEL), _resident((1, D_MODEL)), _resident(winw.shape)],
        out_specs=[_rows(tm, MAIN_COLS), _rows(tm, POOL_WIDTH), _rows(tm, GATE_COLS)],
        out_shape=[jax.ShapeDtypeStruct((t, MAIN_COLS), F32), jax.ShapeDtypeStruct((t, POOL_WIDTH), F32),
                   jax.ShapeDtypeStruct((t, GATE_COLS), F32)],
        compiler_params=_params(("arbitrary",)),
    )(x1, g, winw)


def _mix_bwd(dqfi, dog, du, dgates, dx2, x1, g, winw, tm, riders=()):
    t = x1.shape[0]
    cols = N_CHIPS * SHARD_IN_COLS

    def body(dqfi_ref, dog_ref, du_ref, dgt_ref, dx2_ref, x_ref, g_ref, w_ref, dx_ref, dproj_ref, h_ref, dg_ref):
        @pl.when(pl.program_id(0) == 0)
        def _():
            dg_ref[...] = jnp.zeros_like(dg_ref)

        dproj = jnp.concatenate([dqfi_ref[...], dog_ref[...], du_ref[...], dgt_ref[...]], axis=1)
        dproj_ref[...] = dproj
        dh = _dot_nt(dproj[:, :SHARD_IN_COLS], w_ref[0])
        for j in range(1, N_CHIPS):
            dh += _dot_nt(dproj[:, j * SHARD_IN_COLS:(j + 1) * SHARD_IN_COLS], w_ref[j])
        gv = g_ref[...]
        r, n = _rms(x_ref[...])
        h_ref[...] = (n * gv).astype(BF16)
        dg_ref[...] += _colsum(dh * n)
        dx_ref[...] = dx2_ref[...] + _rms_bwd(dh * gv, n, r)

    return _hosted(
        riders, body, name="mix_bwd", grid=(t // tm,),
        in_specs=[_rows(tm, 3 * D_MODEL), _rows(tm, D_MODEL), _rows(tm, POOL_WIDTH), _rows(tm, GATE_COLS),
                  _rows(tm, D_MODEL), _rows(tm, D_MODEL), _resident((1, D_MODEL)), _resident(winw.shape)],
        out_specs=[_rows(tm, D_MODEL), _rows(tm, cols), _rows(tm, D_MODEL), _acc((1, D_MODEL))],
        out_shape=[jax.ShapeDtypeStruct((t, D_MODEL), F32), jax.ShapeDtypeStruct((t, cols), BF16),
                   jax.ShapeDtypeStruct((t, D_MODEL), BF16), jax.ShapeDtypeStruct((1, D_MODEL), F32)],
        compiler_params=_params(("arbitrary",)),
    )(dqfi, dog, du, dgates, dx2, x1, g, winw)


def _lower_bound(lb_raw):
    l0 = lb_raw[0:1, :]
    l1 = lb_raw[1:2, :]
    m = jnp.maximum(l0, l1)
    e0 = jnp.exp(l0 - m)
    e1 = jnp.exp(l1 - m)
    return e0 / (e0 + e1)


def _head_slices():
    return [slice(h * HEAD_DIM, (h + 1) * HEAD_DIM) for h in range(HEADS)]


def _gates(qr, fr, lb, tril_b, first_half):
    sg = _sigmoid(fr)
    f = lb + (1.0 - lb) * sg
    k = 1.0 - f
    sq = _sigmoid(qr)
    q = qr * sq
    log_f = jnp.log(f)
    gc = _tri_sum(tril_b, log_f)
    gm = _colsum(jnp.where(first_half, log_f, 0.0))
    gl = _colsum(log_f)
    e_q = jnp.exp(gc - gm)
    e_k = jnp.exp(gm - gc)
    e_in = jnp.exp(gc)
    e_out = jnp.exp(gl - gc)
    return dict(sg=sg, f=f, k=k, sq=sq, q=q, e_q=e_q, e_k=e_k, e_in=e_in, e_out=e_out, e_last=jnp.exp(gl))


def _hgrn_fwd(main, lb_raw, tt, riders=()):
    t = main.shape[0]
    n_local = tt // CHUNK

    def body(q_ref, f_ref, i_ref, lb_ref, o_ref, st_ref, s_scr):
        @pl.when(pl.program_id(0) == 0)
        def _():
            s_scr[...] = jnp.zeros_like(s_scr)

        lb = _lower_bound(lb_ref[...])
        row = lax.broadcasted_iota(jnp.int32, (CHUNK, CHUNK), 0)
        col = lax.broadcasted_iota(jnp.int32, (CHUNK, CHUNK), 1)
        tril = row >= col
        tril_b = tril.astype(BF16)
        first_half = lax.broadcasted_iota(jnp.int32, (CHUNK, D_MODEL), 0) < CHUNK // 2
        heads = _head_slices()

        def chunk(c, carry):
            rows = pl.ds(pl.multiple_of(c * CHUNK, CHUNK), CHUNK)
            z = _gates(q_ref[rows, :], f_ref[rows, :], lb, tril_b, first_half)
            qt = (z["q"] * z["e_q"]).astype(BF16)
            kt = (z["k"] * z["e_k"]).astype(BF16)
            qg = (z["q"] * z["e_in"]).astype(BF16)
            kg = (z["k"] * z["e_out"]).astype(BF16)
            vb = i_ref[rows, :].astype(BF16)
            states = [s_scr[h] for h in range(HEADS)]
            for h in range(HEADS):
                st_ref[c, h] = states[h]
            raw = [_dot_nt(qt[:, sl], kt[:, sl]) for sl in heads]
            inter = [_dot_nt(qg[:, sl], states[h].astype(BF16)) for h, sl in enumerate(heads)]
            grown = [_dot_tn(vb[:, sl], kg[:, sl]) for sl in heads]
            scores = [jnp.where(tril, r, 0.0).astype(BF16) for r in raw]
            for h, sl in enumerate(heads):
                s_scr[h] = states[h] * z["e_last"][:, sl] + grown[h]
            o_ref[rows, :] = jnp.concatenate([_dot(scores[h], vb[:, sl]) + inter[h] for h, sl in enumerate(heads)], axis=1)
            return carry

        lax.fori_loop(0, n_local, chunk, 0, unroll=2)

    return _hosted(
        riders, body, name="hgrn_fwd", grid=(t // tt,),
        in_specs=[_rows(tt, D_MODEL, 0), _rows(tt, D_MODEL, 1), _rows(tt, D_MODEL, 2), _resident((2, D_MODEL))],
        out_specs=[_rows(tt, D_MODEL),
                   pl.BlockSpec((n_local, HEADS, HEAD_DIM, HEAD_DIM), lambda i: (i, 0, 0, 0))],
        out_shape=[jax.ShapeDtypeStruct((t, D_MODEL), F32),
                   jax.ShapeDtypeStruct((t // CHUNK, HEADS, HEAD_DIM, HEAD_DIM), F32)],
        scratch_shapes=[pltpu.VMEM((HEADS, HEAD_DIM, HEAD_DIM), F32)],
        compiler_params=_params(("arbitrary",)),
    )(main, main, main, lb_raw)


def _hgrn_bwd(main, lb_raw, states, do, tt, riders=()):
    t = main.shape[0]
    n_tiles = t // tt
    n_local = tt // CHUNK

    def rev(col_block):
        return pl.BlockSpec((tt, D_MODEL), lambda i: (n_tiles - 1 - i, col_block))

    def body(q_ref, f_ref, i_ref, lb_ref, st_ref, do_ref, dqfi_ref, dlb_ref, ds_scr, acc_scr):
        @pl.when(pl.program_id(0) == 0)
        def _():
            ds_scr[...] = jnp.zeros_like(ds_scr)
            acc_scr[...] = jnp.zeros_like(acc_scr)

        lb = _lower_bound(lb_ref[...])
        row = lax.broadcasted_iota(jnp.int32, (CHUNK, CHUNK), 0)
        col = lax.broadcasted_iota(jnp.int32, (CHUNK, CHUNK), 1)
        tril = row >= col
        tril_b = tril.astype(BF16)
        triu_b = (row <= col).astype(BF16)
        first_half = lax.broadcasted_iota(jnp.int32, (CHUNK, D_MODEL), 0) < CHUNK // 2
        heads = _head_slices()
        cat = functools.partial(jnp.concatenate, axis=1)

        def chunk(cc, carry):
            c = n_local - 1 - cc
            rows = pl.ds(pl.multiple_of(c * CHUNK, CHUNK), CHUNK)
            qr = q_ref[rows, :]
            z = _gates(qr, f_ref[rows, :], lb, tril_b, first_half)
            qt = (z["q"] * z["e_q"]).astype(BF16)
            kt = (z["k"] * z["e_k"]).astype(BF16)
            qg_f = z["q"] * z["e_in"]
            qg = qg_f.astype(BF16)
            kg_f = z["k"] * z["e_out"]
            kg = kg_f.astype(BF16)
            vb = i_ref[rows, :].astype(BF16)
            dob = do_ref[rows, :].astype(BF16)
            st = [st_ref[c, h] for h in range(HEADS)]
            dst = [ds_scr[h] for h in range(HEADS)]
            dst_b = [d.astype(BF16) for d in dst]
            raw = [_dot_nt(qt[:, sl], kt[:, sl]) for sl in heads]
            draw = [_dot_nt(dob[:, sl], vb[:, sl]) for sl in heads]
            dqg = [_dot(dob[:, sl], st[h].astype(BF16)) for h, sl in enumerate(heads)]
            dkg = [_dot(vb[:, sl], dst_b[h]) for h, sl in enumerate(heads)]
            dv_inter = [_dot_nt(kg[:, sl], dst_b[h]) for h, sl in enumerate(heads)]
            grown = [_dot_tn(dob[:, sl], qg[:, sl]) for sl in heads]
            scores = [jnp.where(tril, r, 0.0).astype(BF16) for r in raw]
            dscores = [jnp.where(tril, r, 0.0).astype(BF16) for r in draw]
            dqt = [_dot(dscores[h], kt[:, sl]) for h, sl in enumerate(heads)]
            dkt = [_dot_tn(dscores[h], qt[:, sl]) for h, sl in enumerate(heads)]
            dv = [_dot_tn(scores[h], dob[:, sl]) + dv_inter[h] for h, sl in enumerate(heads)]
            carry_in = cat([z["e_last"][:, sl] * _colsum(dst[h] * st[h]) for h, sl in enumerate(heads)])
            for h, sl in enumerate(heads):
                ds_scr[h] = dst[h] * z["e_last"][:, sl] + grown[h]
            dqt, dkt, dqg, dkg = cat(dqt), cat(dkt), cat(dqg), cat(dkg)
            carry_in += _colsum(dkg * kg_f)
            dq = dqt * z["e_q"] + dqg * z["e_in"]
            dk = dkt * z["e_k"] + dkg * z["e_out"]
            dgate = (qt.astype(F32) * dqt - kt.astype(F32) * dkt) + (qg_f * dqg - kg_f * dkg)
            dlogf = _tri_sum(triu_b, dgate) + carry_in
            df = dlogf / z["f"] - dk
            sg = z["sg"]
            sq = z["sq"]
            acc_scr[...] += _colsum(df * (1.0 - sg))
            dqfi_ref[rows, 0:D_MODEL] = (dq * (sq * (1.0 + qr * (1.0 - sq)))).astype(BF16)
            dqfi_ref[rows, D_MODEL:2 * D_MODEL] = (df * (1.0 - lb) * sg * (1.0 - sg)).astype(BF16)
            dqfi_ref[rows, 2 * D_MODEL:3 * D_MODEL] = cat(dv).astype(BF16)
            return carry

        lax.fori_loop(0, n_local, chunk, 0, unroll=2)
        d0 = acc_scr[...] * lb * (1.0 - lb)
        dlb_ref[0:1, :] = d0
        dlb_ref[1:2, :] = -d0

    return _hosted(
        riders, body, name="hgrn_bwd", grid=(n_tiles,),
        in_specs=[rev(0), rev(1), rev(2), _resident((2, D_MODEL)),
                  pl.BlockSpec((n_local, HEADS, HEAD_DIM, HEAD_DIM), lambda i: (n_tiles - 1 - i, 0, 0, 0)),
                  rev(0)],
        out_specs=[pl.BlockSpec((tt, 3 * D_MODEL), lambda i: (n_tiles - 1 - i, 0)), _acc((2, D_MODEL))],
        out_shape=[jax.ShapeDtypeStruct((t, 3 * D_MODEL), BF16), jax.ShapeDtypeStruct((2, D_MODEL), F32)],
        scratch_shapes=[pltpu.VMEM((HEADS, HEAD_DIM, HEAD_DIM), F32), pltpu.VMEM((1, D_MODEL), F32)],
        compiler_params=_params(("arbitrary",)),
    )(main, main, main, lb_raw, states, do)


def _head_norm(o):
    rs, ns = [], []
    for h in range(HEADS):
        oh = o[:, h * HEAD_DIM:(h + 1) * HEAD_DIM]
        r, n = _rms(oh)
        rs.append(jnp.broadcast_to(r, oh.shape))
        ns.append(n)
    return jnp.concatenate(rs, axis=1), jnp.concatenate(ns, axis=1)


def _head_norm_bwd(dn, n, r):
    outs = []
    for h in range(HEADS):
        sl = slice(h * HEAD_DIM, (h + 1) * HEAD_DIM)
        outs.append(_rms_bwd(dn[:, sl], n[:, sl], r[:, sl]))
    return jnp.concatenate(outs, axis=1)


def _window_counts(first_row, tm):
    pos = (first_row + 1 + lax.broadcasted_iota(jnp.int32, (tm, 1), 0)).astype(F32)
    return [jnp.minimum(pos, float(w)) for w in POOL_WINDOWS]


def _post_fwd(o, main, pool_r, gates, x1, onorm, pool_w, pool_scale, sqw, wbw, tm, riders=()):
    t = o.shape[0]
    ext_rows = tm + POOL_HALO

    def body(o_ref, og_ref, u_ref, gt_ref, x1_ref, on_ref, pw_ref, ps_ref, wa_ref, wout_ref, wb_ref,
             x2_ref, ya_ref, yb_ref, pooled_ref, ext):
        i = pl.program_id(0)

        @pl.when(i == 0)
        def _():
            ext[0:POOL_HALO, :] = jnp.zeros((POOL_HALO, POOL_WIDTH), F32)

        _, n = _head_norm(o_ref[...])
        og = og_ref[...]
        oa = (n * on_ref[...] * (og * _sigmoid(og))).astype(BF16)
        ya = _dot(oa, wa_ref[...])

        u = u_ref[...]
        ext[POOL_HALO:ext_rows, :] = u
        e = ext[...]
        counts = _window_counts(i * tm, tm)
        pooled = []
        for gidx, w in enumerate(POOL_WINDOWS):
            s = e[:, gidx * HEAD_DIM:(gidx + 1) * HEAD_DIM]
            shift = 1
            while shift < w:
                s = s + pltpu.roll(s, shift, axis=0)
                shift *= 2
            pooled.append(s[POOL_HALO:, :] / counts[gidx] - u[:, gidx * HEAD_DIM:(gidx + 1) * HEAD_DIM])
        ext[0:POOL_HALO, :] = ext[tm:ext_rows, :]
        pooled_b = [pg.astype(BF16) for pg in pooled]
        pooled_ref[...] = jnp.concatenate(pooled_b, axis=1)
        mixed = jnp.concatenate([_dot(pooled_b[gidx], pw_ref[gidx].astype(BF16)) for gidx in range(len(POOL_WINDOWS))],
                                axis=1) * ps_ref[...]
        mixed_b = mixed.astype(BF16)
        yb = jnp.concatenate([_dot(mixed_b, wb_ref[j]) for j in range(N_CHIPS)], axis=1)

        gt = gt_ref[...]
        y = _sigmoid(gt[:, :D_MODEL]) * ya + _sigmoid(gt[:, D_MODEL:]) * yb
        x2_ref[...] = x1_ref[...] + _dot(y.astype(BF16), wout_ref[...])
        ya_ref[...] = ya.astype(BF16)
        yb_ref[...] = yb.astype(BF16)

    return _hosted(
        riders, body, name="post_fwd", grid=(t // tm,),
        in_specs=[_rows(tm, D_MODEL), _rows(tm, D_MODEL, 3), _rows(tm, POOL_WIDTH), _rows(tm, GATE_COLS), _rows(tm, D_MODEL),
                  _resident((1, D_MODEL)), _resident(pool_w.shape), _resident((1, POOL_WIDTH)),
                  _pick(sqw.shape, 0), _pick(sqw.shape, 1), _resident(wbw.shape)],
        out_specs=[_rows(tm, D_MODEL), _rows(tm, D_MODEL), _rows(tm, D_MODEL), _rows(tm, POOL_WIDTH)],
        out_shape=[jax.ShapeDtypeStruct((t, D_MODEL), F32), jax.ShapeDtypeStruct((t, D_MODEL), BF16),
                   jax.ShapeDtypeStruct((t, D_MODEL), BF16), jax.ShapeDtypeStruct((t, POOL_WIDTH), BF16)],
        scratch_shapes=[pltpu.VMEM((ext_rows, POOL_WIDTH), F32)],
        compiler_params=_params(("arbitrary",)),
    )(o, main, pool_r, gates, x1, onorm, pool_w, pool_scale, sqw, sqw, wbw)


def _post_bwd(dx2, o, main, gates, ya, yb, pooled, onorm, pool_w, pool_scale, sqw, wbw, tm, riders=()):
    t = o.shape[0]
    n_tiles = t // tm
    ext_rows = tm + POOL_HALO
    n_groups = len(POOL_WINDOWS)

    def rev(cols, col_block=0):
        return pl.BlockSpec((tm, cols), lambda i: (n_tiles - 1 - i, col_block))

    def body(dx2_ref, o_ref, og_ref, gt_ref, ya_ref, yb_ref, pooled_ref, on_ref, pw_ref, ps_ref, wa_ref, wout_ref, wb_ref,
             do_ref, dog_ref, du_ref, dgt_ref, dwa_ref, dwout_ref, dwb_ref, dpw_ref, dps_ref, don_ref, ext):
        i = pl.program_id(0)

        @pl.when(i == 0)
        def _():
            ext[tm:ext_rows, :] = jnp.zeros((POOL_HALO, POOL_WIDTH), F32)
            for ref in (dwa_ref, dwout_ref, dwb_ref, dpw_ref, dps_ref, don_ref):
                ref[...] = jnp.zeros_like(ref)

        groups = [slice(gidx * HEAD_DIM, (gidx + 1) * HEAD_DIM) for gidx in range(n_groups)]
        shards = [slice(j * 256, (j + 1) * 256) for j in range(N_CHIPS)]
        dx2b = dx2_ref[...].astype(BF16)
        dy = _dot_nt(dx2b, wout_ref[...])
        pooled_b = pooled_ref[...]
        pm = jnp.concatenate([_dot(pooled_b[:, g], pw_ref[gidx].astype(BF16)) for gidx, g in enumerate(groups)], axis=1)
        gt = gt_ref[...]
        sga = _sigmoid(gt[:, :D_MODEL])
        sgb = _sigmoid(gt[:, D_MODEL:])
        ya = ya_ref[...].astype(F32)
        yb = yb_ref[...].astype(F32)
        y = (sga * ya + sgb * yb).astype(BF16)
        dya = (dy * sga).astype(BF16)
        dyb = (dy * sgb).astype(BF16)
        dgt_ref[:, :D_MODEL] = (dy * ya * sga * (1.0 - sga)).astype(BF16)
        dgt_ref[:, D_MODEL:] = (dy * yb * sgb * (1.0 - sgb)).astype(BF16)
        dwout_ref[...] += _dot_tn(y, dx2b)
        doa = _dot_nt(dya, wa_ref[...])
        dmixed = _dot_nt(dyb[:, shards[0]], wb_ref[0])
        for j in range(1, N_CHIPS):
            dmixed += _dot_nt(dyb[:, shards[j]], wb_ref[j])
        r, n = _head_norm(o_ref[...])
        onv = on_ref[...]
        og = og_ref[...]
        sog = _sigmoid(og)
        silu_og = og * sog
        normed = n * onv
        oa = (normed * silu_og).astype(BF16)
        dog_ref[...] = (doa * normed * (sog * (1.0 + og * (1.0 - sog)))).astype(BF16)
        dnormed = doa * silu_og
        don_ref[...] += _colsum(dnormed * n)
        do_ref[...] = _head_norm_bwd(dnormed * onv, n, r)
        psv = ps_ref[...]
        mixed_b = (pm * psv).astype(BF16)
        dps_ref[...] += _colsum(dmixed * pm)
        dpm = (dmixed * psv).astype(BF16)
        dwa_ref[...] += _dot_tn(oa, dya)
        for j in range(N_CHIPS):
            dwb_ref[j] += _dot_tn(mixed_b, dyb[:, shards[j]])
        counts = _window_counts((n_tiles - 1 - i) * tm, tm)
        dpooled = []
        for gidx, g in enumerate(groups):
            dpw_ref[gidx] += _dot_tn(pooled_b[:, g], dpm[:, g])
            dpooled.append(_dot_nt(dpm[:, g], pw_ref[gidx].astype(BF16)))
        ext[0:tm, :] = jnp.concatenate([dpooled[gidx] / counts[gidx] for gidx in range(n_groups)], axis=1)
        e = ext[...]
        du = []
        for gidx, w in enumerate(POOL_WINDOWS):
            s = e[:, gidx * HEAD_DIM:(gidx + 1) * HEAD_DIM]
            shift = 1
            while shift < w:
                s = s + pltpu.roll(s, ext_rows - shift, axis=0)
                shift *= 2
            du.append(s[:tm, :] - dpooled[gidx])
        ext[tm:ext_rows, :] = ext[0:POOL_HALO, :]
        du_ref[...] = jnp.concatenate(du, axis=1).astype(BF16)

    wa_shape = (D_MODEL, D_MODEL)
    return _hosted(
        riders, body, name="post_bwd", grid=(n_tiles,),
        in_specs=[rev(D_MODEL), rev(D_MODEL), rev(D_MODEL, 3), rev(GATE_COLS), rev(D_MODEL), rev(D_MODEL), rev(POOL_WIDTH),
                  _resident((1, D_MODEL)), _resident(pool_w.shape), _resident((1, POOL_WIDTH)),
                  _pick(sqw.shape, 0), _pick(sqw.shape, 1), _resident(wbw.shape)],
        out_specs=[rev(D_MODEL), rev(D_MODEL), rev(POOL_WIDTH), rev(GATE_COLS),
                   _acc(wa_shape), _acc(wa_shape), _acc(wbw.shape), _acc(pool_w.shape), _acc((1, POOL_WIDTH)),
                   _acc((1, D_MODEL))],
        out_shape=[jax.ShapeDtypeStruct((t, D_MODEL), F32), jax.ShapeDtypeStruct((t, D_MODEL), BF16),
                   jax.ShapeDtypeStruct((t, POOL_WIDTH), BF16), jax.ShapeDtypeStruct((t, GATE_COLS), BF16),
                   jax.ShapeDtypeStruct(wa_shape, F32), jax.ShapeDtypeStruct(wa_shape, F32),
                   jax.ShapeDtypeStruct(wbw.shape, F32), jax.ShapeDtypeStruct(pool_w.shape, F32),
                   jax.ShapeDtypeStruct((1, POOL_WIDTH), F32), jax.ShapeDtypeStruct((1, D_MODEL), F32)],
        scratch_shapes=[pltpu.VMEM((ext_rows, POOL_WIDTH), F32)],
        compiler_params=_params(("arbitrary",)),
    )(dx2, o, main, gates, ya, yb, pooled, onorm, pool_w, pool_scale, sqw, sqw, wbw)


def _tail(x3, p, target, g_ple, g_post, g_final, sqw, wpw, tm, riders=()):
    t = x3.shape[0]
    pd = p.shape[1]

    def body(x_ref, p_ref, tg_ref, g4_ref, g5_ref, g6_ref, wg_ref, wp_ref,
             dx_ref, loss_ref, dwg_ref, dwp_ref, dg4_ref, dg5_ref, dg6_ref):
        @pl.when(pl.program_id(0) == 0)
        def _():
            for ref in (loss_ref, dwg_ref, dwp_ref, dg4_ref, dg5_ref, dg6_ref):
                ref[...] = jnp.zeros_like(ref)

        x3v = x_ref[...]
        g4, g5, g6 = g4_ref[...], g5_ref[...], g6_ref[...]
        r4, n4 = _rms(x3v)
        h4 = (n4 * g4).astype(BF16)
        gate = _sigmoid(_dot(h4, wg_ref[...]))
        pb = p_ref[...].astype(BF16)
        r5, n5 = _rms(jnp.concatenate([_dot(pb, wp_ref[j]) for j in range(N_CHIPS)], axis=1))
        emb = n5 * g5
        r6, n6 = _rms(x3v + gate * emb)
        diff = n6 * g6 - tg_ref[...]
        loss_ref[...] += 0.5 * jnp.sum(jnp.mean(diff * diff, axis=-1, keepdims=True), axis=0, keepdims=True)
        dout = diff * (1.0 / D_MODEL)
        dg6_ref[...] += _colsum(dout * n6)
        dx4 = _rms_bwd(dout * g6, n6, r6)
        demb = dx4 * gate
        dg5_ref[...] += _colsum(demb * n5)
        dpre = _rms_bwd(demb * g5, n5, r5).astype(BF16)
        for j in range(N_CHIPS):
            dwp_ref[j] += _dot_tn(pb, dpre[:, j * pd:(j + 1) * pd])
        dz = (dx4 * emb * gate * (1.0 - gate)).astype(BF16)
        dwg_ref[...] += _dot_tn(h4, dz)
        dh4 = _dot_nt(dz, wg_ref[...])
        dg4_ref[...] += _colsum(dh4 * n4)
        dx_ref[...] = dx4 + _rms_bwd(dh4 * g4, n4, r4)

    sq_shape = (D_MODEL, D_MODEL)
    vec = (1, D_MODEL)
    return _hosted(
        riders, body, name="tail", grid=(t // tm,),
        in_specs=[_rows(tm, D_MODEL), _rows(tm, pd), _rows(tm, D_MODEL), _resident(vec), _resident(vec), _resident(vec),
                  _pick(sqw.shape, 2), _resident(wpw.shape)],
        out_specs=[_rows(tm, D_MODEL), _acc((1, 1)), _acc(sq_shape), _acc(wpw.shape), _acc(vec), _acc(vec), _acc(vec)],
        out_shape=[jax.ShapeDtypeStruct((t, D_MODEL), F32), jax.ShapeDtypeStruct((1, 1), F32),
                   jax.ShapeDtypeStruct(sq_shape, F32), jax.ShapeDtypeStruct(wpw.shape, F32),
                   jax.ShapeDtypeStruct(vec, F32), jax.ShapeDtypeStruct(vec, F32), jax.ShapeDtypeStruct(vec, F32)],
        compiler_params=_params(("arbitrary",)),
    )(x3, p, target, g_ple, g_post, g_final, sqw, wpw)


def _position():
    return lax.axis_index("x"), lax.axis_index("y"), lax.axis_index("c")


def _other_chips(x, y):
    return [(1 - x, y), (x, 1 - y), (1 - x, 1 - y)]


def _remote(src, dst, send_sems, recv_sems, k, device):
    return pltpu.make_async_remote_copy(src_ref=src, dst_ref=dst, send_sem=send_sems.at[k], recv_sem=recv_sems.at[k],
                                        device_id=device, device_id_type=MESH)


def _gather_rider(shards, forward_at):
    n = len(shards)

    def copies(ins, outs, sems):
        send_sems, recv_sems, local_sems = sems
        x, y, c = _position()
        mine = 2 * x + y
        local = [pltpu.make_async_copy(ins[a], outs[a].at[:, mine], local_sems.at[a]) for a in range(n)]
        first, passed, arriving = [], [], []
        for k, (cx, cy) in enumerate(_other_chips(x, y)):
            theirs = 2 * cx + cy
            for a in range(n):
                first.append(_remote(ins[a].at[:, c], outs[a].at[:, mine, c], send_sems, recv_sems, k * n + a, (cx, cy, c)))
                block = outs[a].at[:, theirs, c]
                passed.append(_remote(block, block, send_sems, recv_sems, (3 + k) * n + a, (x, y, 1 - c)))
                other = outs[a].at[:, theirs, 1 - c]
                arriving.append(_remote(other, other, send_sems, recv_sems, (3 + k) * n + a, (x, y, 1 - c)))
        return local, first, passed, arriving

    def begin(ins, outs, sems):
        local, first, _, _ = copies(ins, outs, sems)
        for cp in local + first:
            cp.start()

    def forward(ins, outs, sems):
        _, first, passed, _ = copies(ins, outs, sems)
        for got, cp in zip(first, passed):
            got.wait_recv()
            cp.start()

    def finish(ins, outs, sems):
        local, first, passed, arriving = copies(ins, outs, sems)
        for cp in arriving:
            cp.wait_recv()
        for cp in first + passed:
            cp.wait_send()
        for cp in local:
            cp.wait()

    return _Rider(shards, [jax.ShapeDtypeStruct((s.shape[0], N_CHIPS) + s.shape[1:], s.dtype) for s in shards],
                  [pltpu.SemaphoreType.DMA((6 * n,)), pltpu.SemaphoreType.DMA((6 * n,)), pltpu.SemaphoreType.DMA((n,))],
                  [(0, begin), (forward_at, forward), (1, finish)])


def _gather_from(shards, peers, n_peers, with_own, forward_at):
    n = len(shards)

    def copies(ins, outs, sems):
        send_sems, recv_sems, local_sems = sems
        x, y, c = _position()
        local = [pltpu.make_async_copy(ins[a], outs[a].at[:, 0], local_sems.at[a]) for a in range(n)] if with_own else []
        first, passed, arriving = [], [], []
        for k, (cx, cy) in enumerate(peers(x, y)):
            slot = k + int(with_own)
            for a in range(n):
                first.append(_remote(ins[a].at[:, c], outs[a].at[:, slot, c], send_sems, recv_sems, k * n + a, (cx, cy, c)))
                block = outs[a].at[:, slot, c]
                passed.append(_remote(block, block, send_sems, recv_sems, (n_peers + k) * n + a, (x, y, 1 - c)))
                other = outs[a].at[:, slot, 1 - c]
                arriving.append(_remote(other, other, send_sems, recv_sems, (n_peers + k) * n + a, (x, y, 1 - c)))
        return local, first, passed, arriving

    def begin(ins, outs, sems):
        local, first, _, _ = copies(ins, outs, sems)
        for cp in local + first:
            cp.start()

    def forward(ins, outs, sems):
        _, first, passed, _ = copies(ins, outs, sems)
        for got, cp in zip(first, passed):
            got.wait_recv()
            cp.start()

    def finish(ins, outs, sems):
        local, first, passed, arriving = copies(ins, outs, sems)
        for cp in arriving:
            cp.wait_recv()
        for cp in first + passed:
            cp.wait_send()
        for cp in local:
            cp.wait()

    slots = n_peers + int(with_own)
    return _Rider(shards, [jax.ShapeDtypeStruct((s.shape[0], slots) + s.shape[1:], s.dtype) for s in shards],
                  [pltpu.SemaphoreType.DMA((2 * n_peers * n,)), pltpu.SemaphoreType.DMA((2 * n_peers * n,)),
                   pltpu.SemaphoreType.DMA((n,))],
                  [(0, begin), (forward_at, forward), (1, finish)])


def _exchange_rider(arrays, out_shape, n_copies, transfers, n_local=0):
    def copies(ins, outs, sems):
        send_sems, recv_sems, local_sems = sems
        remote, local = transfers(ins, outs)
        return ([_remote(src, dst, send_sems, recv_sems, i, dev) for i, (src, dst, dev) in enumerate(remote)],
                [pltpu.make_async_copy(src, dst, local_sems.at[i]) for i, (src, dst) in enumerate(local)])

    def begin(ins, outs, sems):
        remote, local = copies(ins, outs, sems)
        for cp in remote + local:
            cp.start()

    def finish(ins, outs, sems):
        remote, local = copies(ins, outs, sems)
        for cp in remote:
            cp.wait_recv()
        for cp in remote:
            cp.wait_send()
        for cp in local:
            cp.wait()

    return _Rider(arrays, out_shape,
                  [pltpu.SemaphoreType.DMA((max(n_copies, 1),)), pltpu.SemaphoreType.DMA((max(n_copies, 1),)),
                   pltpu.SemaphoreType.DMA((max(n_local, 1),))],
                  [(0, begin), (1, finish)])


def _pair_rider(partials):
    def transfers(ins, outs):
        x, y, c = _position()
        return [(ins[a].at[:, :, 1 - c], outs[a], (x, y, 1 - c)) for a in range(len(partials))], []

    shapes = [jax.ShapeDtypeStruct(g.shape[:2] + g.shape[3:], g.dtype) for g in partials]
    return _exchange_rider(partials, shapes, len(partials), transfers)


RELATIVE_SLOT = (2, 1, 3)


def _near_chip(x, y):
    return [(x, 1 - y)]


def _far_chips(x, y):
    return [(1 - x, y), (1 - x, 1 - y)]


def _chips_rider(sums, relative=False):
    n = len(sums)

    def transfers(ins, outs):
        x, y, c = _position()
        return [(ins[a].at[:, RELATIVE_SLOT[k] if relative else 2 * cx + cy], outs[a].at[:, k], (cx, cy, c))
                for k, (cx, cy) in enumerate(_other_chips(x, y)) for a in range(n)], []

    shapes = [jax.ShapeDtypeStruct((q.shape[0], 3) + q.shape[2:], q.dtype) for q in sums]
    return _exchange_rider(sums, shapes, 3 * n, transfers)


def _share_rider(halves):
    def transfers(ins, outs):
        x, y, c = _position()
        return [(ins[a], outs[a], (x, y, 1 - c)) for a in range(len(halves))], []

    return _exchange_rider(halves, [jax.ShapeDtypeStruct(h.shape, h.dtype) for h in halves], len(halves), transfers)


def _assemble_rider(near, far):
    def transfers(ins, outs):
        return [], [(ins[0], outs[0].at[:, 0:2]), (ins[1], outs[0].at[:, 2:4])]

    shape = jax.ShapeDtypeStruct((near.shape[0], N_CHIPS) + near.shape[2:], near.dtype)
    return _exchange_rider([near, far], [shape], 0, transfers, n_local=2)


def _small_rider(pack):
    flips = [(fx, fy, fc) for fx in (0, 1) for fy in (0, 1) for fc in (0, 1)][1:]

    def transfers(ins, outs):
        x, y, c = _position()
        slot = outs[0].at[4 * x + 2 * y + c]
        flip = lambda v, f: v + f - 2 * v * f
        return [(ins[0], slot, (flip(x, fx), flip(y, fy), flip(c, fc))) for fx, fy, fc in flips], [(ins[0], slot)]

    return _exchange_rider([pack], [jax.ShapeDtypeStruct((8,) + pack.shape, pack.dtype)], len(flips), transfers, n_local=1)


def _alone(rider, name):
    return _hosted([rider], lambda: None, name=name, in_specs=[], out_specs=[], out_shape=[])()[1][0]


def _add_pair(mine, theirs, c, tag):
    n = len(mine)

    def body(c_ref, *refs):
        for a in range(n):
            refs[2 * n + a][...] = (refs[2 * a][...] + refs[2 * a + 1][...]).astype(BF16)

    in_specs, out_specs = [], []
    for got in theirs:
        l, _, hr, cols = got.shape
        in_specs += [pl.BlockSpec((l, None, None, hr, cols), lambda j, s: (0, j, s[0], 0, 0)),
                     pl.BlockSpec((l, None, hr, cols), lambda j, s: (0, j, 0, 0))]
        out_specs.append(pl.BlockSpec((l, None, hr, cols), lambda j, s: (0, j, 0, 0)))
    return pl.pallas_call(
        body, name=f"add_pair_{tag}",
        grid_spec=pltpu.PrefetchScalarGridSpec(num_scalar_prefetch=1, grid=(N_CHIPS,), in_specs=in_specs, out_specs=out_specs),
        out_shape=[jax.ShapeDtypeStruct(got.shape, BF16) for got in theirs],
        compiler_params=_params(("parallel",)),
    )(c.reshape(1), *[a for pair in zip(mine, theirs) for a in pair])


def _add_chips(parts, received, mine, tag):
    n = len(parts)

    def body(j_ref, *refs):
        for a in range(n):
            acc = refs[2 * a][...].astype(F32)
            for k in range(3):
                acc += refs[2 * a + 1][:, k].astype(F32)
            refs[2 * n + a][...] = acc

    in_specs, out_specs, out_shape = [], [], []
    for got in received:
        l, _, hr, cols = got.shape
        in_specs += [pl.BlockSpec((l, None, hr // 2, cols), lambda i, s: (0, s[0], i, 0)),
                     pl.BlockSpec((l, 3, hr // 2, cols), lambda i, s: (0, 0, i, 0))]
        out_specs.append(pl.BlockSpec((l, hr // 2, cols), lambda i, s: (0, i, 0)))
        out_shape.append(jax.ShapeDtypeStruct((l, hr, cols), F32))
    return pl.pallas_call(
        body, name=f"add_chips_{tag}",
        grid_spec=pltpu.PrefetchScalarGridSpec(num_scalar_prefetch=1, grid=(2,), in_specs=in_specs, out_specs=out_specs),
        out_shape=out_shape,
        compiler_params=_params(("parallel",)),
    )(mine.reshape(1), *[a for pair in zip(parts, received) for a in pair])


def _adam_update(w, g, m, v):
    m2 = ADAM_B1 * m + (1.0 - ADAM_B1) * g
    v2 = ADAM_B2 * v + (1.0 - ADAM_B2) * jnp.square(g)
    m_hat = m2 / (1.0 - ADAM_B1 ** ADAM_STEP)
    v_hat = v2 / (1.0 - ADAM_B2 ** ADAM_STEP)
    return -ADAM_LR * (m_hat / (jnp.sqrt(v_hat) + ADAM_EPS) + ADAM_WD * w), m2, v2


def _adamw_group(items, tag, riders=()):
    n = len(items)

    def body(*refs):
        ins, outs = refs[:5 * n], refs[5 * n:]
        mine = pl.program_id(0) == lax.axis_index("c")
        for a in range(n):
            w_ref, own_ref, other_ref, m_ref, v_ref = ins[5 * a:5 * a + 5]
            g_ref, d_ref, nm_ref, nv_ref = outs[4 * a:4 * a + 4]
            gv = jnp.where(mine, own_ref[...], other_ref[...])
            g_ref[...] = gv
            d_ref[...], nm_ref[...], nv_ref[...] = _adam_update(w_ref[...], gv, m_ref[...], v_ref[...])

    in_specs, out_specs, out_shape, args = [], [], [], []
    for w, own, other, m, v in items:
        _, hr, cols = w.shape
        tr = hr // ADAM_BLOCKS
        full = pl.BlockSpec((None, tr, cols), lambda h, i: (h, i, 0))
        half = pl.BlockSpec((tr, cols), lambda h, i: (i, 0))
        in_specs += [full, half, half, full, full]
        out_specs += [full] * 4
        out_shape += [jax.ShapeDtypeStruct((2, hr, cols), F32)] * 4
        args += [w, own, other, m, v]
    outs, extras = _hosted(riders, body, name=f"adamw_{tag}", grid=(2, ADAM_BLOCKS), in_specs=in_specs, out_specs=out_specs,
                           out_shape=out_shape, compiler_params=_params(("arbitrary", "arbitrary")))(*args)
    return [outs[4 * a:4 * a + 4] for a in range(n)], extras


def _adamw_small(w, gathered, m, v, shapes):
    n_rows = w.shape[0]
    places = []
    for i, name in enumerate(VECTOR_PARAMS):
        places.append((name, i * TILE_ROWS, 1 if len(shapes[name]) == 1 else shapes[name][0], shapes[name][-1]))
    places.append(("pool_w", len(VECTOR_PARAMS) * TILE_ROWS, n_rows - len(VECTOR_PARAMS) * TILE_ROWS, D_MODEL))

    def body(w_ref, g_ref, m_ref, v_ref, loss_ref, *rest):
        outs, (sum_scr, d_scr, nm_scr, nv_scr) = rest[:-4], rest[-4:]
        total = g_ref[0]
        for i in range(1, g_ref.shape[0]):
            total += g_ref[i]
        sum_scr[...] = total
        gv = sum_scr[0:n_rows, :]
        d_scr[...], nm_scr[...], nv_scr[...] = _adam_update(w_ref[...], gv, m_ref[...], v_ref[...])
        loss_ref[...] = sum_scr[n_rows:n_rows + 1, 0:1]
        for k, (_, first, rows, cols) in enumerate(places):
            for j, scr in enumerate((sum_scr, d_scr, nm_scr, nv_scr)):
                outs[4 * k + j][...] = scr[first:first + rows, 0:cols]

    out_shape = [jax.ShapeDtypeStruct((1, 1), F32)]
    for _, _, rows, cols in places:
        out_shape += [jax.ShapeDtypeStruct((rows, cols), F32)] * 4
    res = pl.pallas_call(
        body, name="adamw_small", out_shape=out_shape,
        scratch_shapes=[pltpu.VMEM(gathered.shape[1:], F32)] + [pltpu.VMEM(w.shape, F32)] * 3,
        compiler_params=_params())(w, gathered, m, v)
    return res[0], {name: res[1 + 4 * k:5 + 4 * k] for k, (name, _, _, _) in enumerate(places)}


VECTOR_PARAMS = ("ffn1_norm", "mix_norm", "hgrn_lb", "hgrn_onorm", "ffn2_norm", "ple_norm", "ple_post_norm", "final_norm",
                 "pool_scale")
ALL_PARAMS = ("ffn1_norm", "ffn1_w1", "ffn1_w3", "ffn1_w2", "mix_norm", "w_in", "hgrn_lb", "hgrn_onorm", "w_branch_a",
              "pool_w", "pool_scale", "w_branch_b", "w_out", "ffn2_norm", "ffn2_w1", "ffn2_w3", "ffn2_w2", "ple_norm",
              "ple_w_gate", "ple_w_proj", "ple_post_norm", "final_norm")
TILE_ROWS = 8


def _pack_small(values, loss=None):
    tile = lambda a: jnp.pad(a, ((0, TILE_ROWS - a.shape[0]), (0, D_MODEL - a.shape[1])))
    parts = [tile(values[name].reshape(-1, values[name].shape[-1])) for name in VECTOR_PARAMS]
    parts.append(values["pool_w"].reshape(-1, D_MODEL))
    if loss is not None:
        parts.append(tile(loss))
    return jnp.concatenate(parts, axis=0)


def _halved(a, lead):
    return a.reshape(lead, 2, -1, a.shape[-1])


def _shard_halves(a, lead):
    return a.reshape(lead, N_CHIPS, 2, -1, a.shape[-1])


REDUCED_TRANSPOSED = ("ffn1_w1", "ffn1_w3", "ffn2_w1", "ffn2_w3")


def _entries(arrays):
    return [a[i] for a in arrays for i in range(a.shape[0])]


def _adam_items(names, own, other, w, m, v):
    items = []
    for name, g_own, g_other in zip(names, _entries(own), _entries(other)):
        view = (lambda a: _halved(a[0].T, 1)[0]) if name in REDUCED_TRANSPOSED else (lambda a: _halved(a, 1)[0])
        items.append((view(w[name]), g_own, g_other, view(m[name]), view(v[name])))
    return items


def _adam_store(names, results, w, out):
    for name, res in zip(names, results):
        shape = w[name].shape
        if name in REDUCED_TRANSPOSED:
            back = [a.reshape(shape[2], shape[1]).T.reshape(shape) for a in res]
        else:
            back = [a.reshape(shape) for a in res]
        out["grad"][name], out["delta"][name], out["new_m"][name], out["new_v"][name] = back


def kernel(x, p, ffn1_norm, ffn1_w1, ffn1_w3, ffn1_w2, mix_norm, w_in, hgrn_lb, hgrn_onorm, w_branch_a, pool_w, pool_scale, w_branch_b, w_out, ffn2_norm, ffn2_w1, ffn2_w3, ffn2_w2, ple_norm, ple_w_gate, ple_w_proj, ple_post_norm, final_norm, loss_target, m_ffn1_norm, m_ffn1_w1, m_ffn1_w3, m_ffn1_w2, m_mix_norm, m_w_in, m_hgrn_lb, m_hgrn_onorm, m_w_branch_a, m_pool_w, m_pool_scale, m_w_branch_b, m_w_out, m_ffn2_norm, m_ffn2_w1, m_ffn2_w3, m_ffn2_w2, m_ple_norm, m_ple_w_gate, m_ple_w_proj, m_ple_post_norm, m_final_norm, v_ffn1_norm, v_ffn1_w1, v_ffn1_w3, v_ffn1_w2, v_mix_norm, v_w_in, v_hgrn_lb, v_hgrn_onorm, v_w_branch_a, v_pool_w, v_pool_scale, v_w_branch_b, v_w_out, v_ffn2_norm, v_ffn2_w1, v_ffn2_w3, v_ffn2_w2, v_ple_norm, v_ple_w_gate, v_ple_w_proj, v_ple_post_norm, v_final_norm):
    args = dict(locals())
    w = {name: args[name] for name in ALL_PARAMS}
    m = {name: args["m_" + name] for name in ALL_PARAMS}
    v = {name: args["v_" + name] for name in ALL_PARAMS}
    cx, cy, cc = _position()
    chip = (2 * cx + cy).astype(jnp.int32)
    core = cc.astype(jnp.int32)
    xs, ps, target = x[0], p[0, 0], loss_target[0]
    t = xs.shape[0]
    tm = min(256, t)
    tm_ffn = min(512, t)
    tt = min(512, t)
    tk = min(2048, t)
    small = {name: w[name] for name in VECTOR_PARAMS}
    small["final_norm"] = w["final_norm"].reshape(1, D_MODEL)
    pool_w0 = w["pool_w"][0]

    ffn_shard = lambda i: _halved(jnp.stack([w[f"ffn{i}_w1"][0].T, w[f"ffn{i}_w3"][0].T, w[f"ffn{i}_w2"][0]]).astype(BF16), 3)
    sq_shard = _halved(jnp.stack([w["w_branch_a"][0], w["w_out"][0], w["ple_w_gate"][0]]).astype(BF16), 3)
    win_shard, wb_shard, wp_shard = (_halved(w[n].astype(BF16), 1) for n in ("w_in", "w_branch_b", "ple_w_proj"))

    ffn1_shard = ffn_shard(1)
    (near,) = _alone(_gather_from([ffn1_shard], _near_chip, 1, True, 0.5), "gather_ffn1")
    half_ffn = lambda a: a.reshape(3, D_FF // 2, D_MODEL)
    (x1_near, a1_near, b1_near), ((far,),) = _ffn_fwd(xs, small["ffn1_norm"], half_ffn(near), "1_near", tm_ffn,
                                                       [_gather_from([ffn1_shard], _far_chips, 2, False, 0.6)])
    (x1, a1_far, b1_far), ((winw,), (ffn1w,)) = _ffn_fwd(xs, small["ffn1_norm"], half_ffn(far), "1_far", tm_ffn,
                                                          [_gather_rider([win_shard], 0.6), _assemble_rider(near, far)],
                                                          base=x1_near)
    ffn1w = ffn1w.reshape(3, D_FF, D_MODEL)
    winw = winw.reshape(N_CHIPS, D_MODEL, SHARD_IN_COLS)
    (main, pool_r, gates), ((sqw, wbw, wpw),) = _mix_fwd(x1, small["mix_norm"], winw, tm,
                                                          [_gather_rider([sq_shard, wb_shard, wp_shard], 0.5)])
    sqw = sqw.reshape(3, D_MODEL, D_MODEL)
    wbw = wbw.reshape(N_CHIPS, POOL_WIDTH, -1)
    wpw = wpw.reshape(N_CHIPS, ps.shape[1], -1)
    (o, states), ((ffn2w,),) = _hgrn_fwd(main, small["hgrn_lb"], tt, [_gather_rider([ffn_shard(2)], 0.7)])
    ffn2w = ffn2w.reshape(3, D_FF, D_MODEL)
    (x2, ya, yb, pooled), _ = _post_fwd(o, main, pool_r, gates, x1, small["hgrn_onorm"], pool_w0, small["pool_scale"], sqw,
                                       wbw, tm)
    (x3, a2, b2), _ = _ffn_fwd(x2, small["ffn2_norm"], ffn2w, "2", tm_ffn)
    (dx3, loss, d_wg, d_wp, d_ple, d_post, d_final), _ = _tail(
        x3, ps, target, small["ple_norm"], small["ple_post_norm"], small["final_norm"], sqw, wpw, tm_ffn)

    add_pairs = lambda parts, got, group: _add_pair(parts, got, core, group)
    add_chips = lambda sums, got, group, mine=chip: _add_chips(sums, got, mine, group)
    own_slot = jnp.zeros((), jnp.int32)
    names1 = ("ffn2_w1", "ffn2_w3", "ffn2_w2", "ple_w_gate", "ple_w_proj")
    names2 = ("w_branch_a", "w_out", "w_branch_b")
    names3 = ("w_in",)
    names4 = ("ffn1_w1", "ffn1_w3")
    names5 = ("ffn1_w2",)
    tags1, tags2, tags3, tags4, tags5 = "ffn2", "branches", "w_in", "ffn1_in", "ffn1_out"

    (dx2, dab2, s2, h3, dxh2, d_ffn2_norm), _ = _ffn_bwd(dx3, x2, small["ffn2_norm"], [a2], [b2], ffn2w, 2, tm)
    (d_w13_2,), _ = _wgrad(dab2, h3, WGRAD_IN_BLOCKS, "wgrad_ffn2_in", tk)
    (d_w2_2,), _ = _wgrad(s2, dxh2, WGRAD_OUT_BLOCKS, "wgrad_ffn2_out", tk)
    part1 = [_shard_halves(d_w13_2, 2), _shard_halves(d_w2_2, 1), _shard_halves(d_wg, 1), _shard_halves(d_wp, 1)]
    (do, dog, du, dgates, d_wa, d_wout, d_wb, d_pool_w, d_pool_scale, d_onorm), (sib1,) = _post_bwd(
        dx2, o, main, gates, ya, yb, pooled, small["hgrn_onorm"], pool_w0, small["pool_scale"], sqw, wbw, tm,
        [_pair_rider(part1)])
    sums1 = add_pairs(part1, sib1, tags1)
    part2 = [_shard_halves(d_wa, 1), _shard_halves(d_wout, 1), _shard_halves(d_wb, 1)]
    (dqfi, d_lb), (got1, sib2) = _hgrn_bwd(main, small["hgrn_lb"], states, do, tt, [_chips_rider(sums1), _pair_rider(part2)])
    own1 = add_chips(sums1, got1, tags1)
    sums2 = add_pairs(part2, sib2, tags2)
    (dx1, dproj, h2, d_mix_norm), (other1, got2) = _mix_bwd(dqfi, dog, du, dgates, dx2, x1, small["mix_norm"], winw, tm,
                                                            [_share_rider(own1), _chips_rider(sums2)])
    own2 = add_chips(sums2, got2, tags2)
    (d_win,), (other2,) = _wgrad_cols(h2, dproj, N_CHIPS, "wgrad_in", tk, [_share_rider(own2)])
    part3 = [_shard_halves(d_win, 1)]
    (dx, dab1, s1, h1, dxh1, d_ffn1_norm), _ = _ffn_bwd(dx1, xs, small["ffn1_norm"], [a1_near, a1_far], [b1_near, b1_far], ffn1w, 1, tm)
    vecs = dict(ffn1_norm=d_ffn1_norm, mix_norm=d_mix_norm, hgrn_lb=d_lb, hgrn_onorm=d_onorm, ffn2_norm=d_ffn2_norm,
                ple_norm=d_ple, ple_post_norm=d_post, final_norm=d_final, pool_scale=d_pool_scale, pool_w=d_pool_w)
    (d_w13_1,), (sib3, (small_all,)) = _wgrad(dab1, h1, WGRAD_IN_BLOCKS, "wgrad_ffn1_in", tk,
                                              [_pair_rider(part3), _small_rider(_pack_small(vecs, loss))])
    sums3 = add_pairs(part3, sib3, tags3)
    part4 = [_shard_halves(d_w13_1, 2)]
    (d_w2_1,), (got3, sib4) = _wgrad(s1, dxh1, WGRAD_OUT_BLOCKS, "wgrad_ffn1_out", tk,
                                     [_chips_rider(sums3), _pair_rider(part4)])
    own3 = add_chips(sums3, got3, tags3)
    sums4 = add_pairs(part4, sib4, tags4)
    part5 = [_shard_halves(d_w2_1, 1)]

    out = dict(grad={}, delta={}, new_m={}, new_v={})
    results, (other3, got4, sib5) = _adamw_group(
        _adam_items(names1 + names2, own1 + own2, other1 + other2, w, m, v), "early",
        [_share_rider(own3), _chips_rider(sums4, relative=True), _pair_rider(part5)])
    _adam_store(names1 + names2, results, w, out)
    own4 = add_chips(sums4, got4, tags4, own_slot)
    sums5 = add_pairs(part5, sib5, tags5)
    results, (other4, got5) = _adamw_group(_adam_items(names3, own3, other3, w, m, v), "w_in",
                                           [_share_rider(own4), _chips_rider(sums5, relative=True)])
    _adam_store(names3, results, w, out)
    own5 = add_chips(sums5, got5, tags5, own_slot)
    other5 = _alone(_share_rider(own5), "share_last")
    results, _ = _adamw_group(_adam_items(names4 + names5, own4 + own5, other4 + other5, w, m, v), "ffn1")
    _adam_store(names4 + names5, results, w, out)

    shapes = {name: w[name].shape for name in VECTOR_PARAMS + ("pool_w",)}
    loss, results = _adamw_small(_pack_small(w), small_all, _pack_small(m), _pack_small(v), shapes)
    for name, res in results.items():
        out["grad"][name], out["delta"][name], out["new_m"][name], out["new_v"][name] = (a.reshape(shapes[name]) for a in res)

    return (loss[0, 0], dx[None], *[out["grad"][n] for n in ALL_PARAMS], *[out["delta"][n] for n in ALL_PARAMS],
            *[out["new_m"][n] for n in ALL_PARAMS], *[out["new_v"][n] for n in ALL_PARAMS])
```

```python
import functools

import jax
import jax.numpy as jnp
from jax import lax
from jax.experimental import pallas as pl
from jax.experimental.pallas import tpu as pltpu

F32 = jnp.float32
BF16 = jnp.bfloat16
MESH = pl.DeviceIdType.MESH

D_MODEL = 1024
D_FF = 2816
HEADS = 8
HEAD_DIM = 128
POOL_WIDTH = 512
POOL_WINDOWS = (2, 4, 8, 16)
POOL_HALO = 16
N_CHIPS = 4
EPS = 1e-6
CHUNK = 64
MAIN_COLS = 4096
GATE_COLS = 2048
SHARD_IN_COLS = 1664

ADAM_LR = 0.001
ADAM_B1 = 0.9
ADAM_B2 = 0.999
ADAM_EPS = 1e-08
ADAM_WD = 0.01
ADAM_STEP = 10

VMEM_LIMIT = 56 * 1024 * 1024
WGRAD_IN_BLOCKS = 4
WGRAD_OUT_BLOCKS = 2
ADAM_BLOCKS = 4


def _params(semantics=None, vmem=VMEM_LIMIT):
    return pltpu.CompilerParams(dimension_semantics=semantics, vmem_limit_bytes=vmem)


def _dot(a, b):
    return jnp.dot(a, b, preferred_element_type=F32)


def _dot_nt(a, b):
    return lax.dot_general(a, b, (((1,), (1,)), ((), ())), preferred_element_type=F32)


def _dot_tn(a, b):
    return lax.dot_general(a, b, (((0,), (0,)), ((), ())), preferred_element_type=F32)


def _tri_sum(tri, x):
    hi = x.astype(BF16)
    lo = (x - hi.astype(F32)).astype(BF16)
    return _dot(tri, hi) + _dot(tri, lo)


def _sigmoid(x):
    return jax.nn.sigmoid(x)


def _resident(shape):
    zeros = (0,) * len(shape)
    return pl.BlockSpec(shape, lambda *_: zeros, pipeline_mode=pl.Buffered(1))


def _pick(shape, k):
    zeros = (0,) * (len(shape) - 1)
    return pl.BlockSpec((None,) + tuple(shape[1:]), lambda *_: (k,) + zeros, pipeline_mode=pl.Buffered(1))


def _rows(tm, cols, col_block=0):
    return pl.BlockSpec((tm, cols), lambda i: (i, col_block))


def _acc(shape):
    zeros = (0,) * len(shape)
    return pl.BlockSpec(shape, lambda *_: zeros)


def _rms(x):
    r = lax.rsqrt(jnp.mean(x * x, axis=-1, keepdims=True) + EPS)
    return r, x * r


def _rms_bwd(dn, n, r):
    return r * (dn - n * jnp.mean(dn * n, axis=-1, keepdims=True))


def _colsum(a):
    return jnp.sum(a, axis=0, keepdims=True)


ANY = pl.BlockSpec(memory_space=pl.ANY)


class _Rider:
    def __init__(self, inputs, out_shape, sems, phases):
        self.inputs, self.out_shape, self.sems, self.phases = list(inputs), list(out_shape), list(sems), list(phases)


def _hosted(riders, body, *, name, grid=(), in_specs, out_specs, out_shape, scratch_shapes=(), compiler_params=None):
    riders = [r for r in riders if r is not None]
    n_in, n_out, n_scr = len(in_specs), len(out_shape), len(scratch_shapes)
    n_steps = 1
    for g in grid:
        n_steps *= g

    def wrapped(*refs):
        pos = n_in
        ins = refs[:n_in]
        r_ins = []
        for r in riders:
            r_ins.append(refs[pos:pos + len(r.inputs)])
            pos += len(r.inputs)
        outs = refs[pos:pos + n_out]
        pos += n_out
        r_outs = []
        for r in riders:
            r_outs.append(refs[pos:pos + len(r.out_shape)])
            pos += len(r.out_shape)
        scr = refs[pos:pos + n_scr]
        pos += n_scr
        r_sems = []
        for r in riders:
            r_sems.append(refs[pos:pos + len(r.sems)])
            pos += len(r.sems)
        step = 0
        for axis in range(len(grid)):
            step = step * grid[axis] + pl.program_id(axis)

        def at_step(which, fn):
            if n_steps == 1:
                fn()
            else:
                pl.when(step == which)(fn)

        for r, ri, ro, rs in zip(riders, r_ins, r_outs, r_sems):
            for fraction, fn in r.phases:
                if fraction == 0:
                    at_step(0, functools.partial(fn, ri, ro, rs))
        body(*ins, *outs, *scr)
        for r, ri, ro, rs in zip(riders, r_ins, r_outs, r_sems):
            for fraction, fn in r.phases:
                if fraction > 0:
                    at_step(min(int(fraction * n_steps), n_steps - 1), functools.partial(fn, ri, ro, rs))

    call = pl.pallas_call(
        wrapped, name=name, grid=grid,
        in_specs=list(in_specs) + [ANY for r in riders for _ in r.inputs],
        out_specs=list(out_specs) + [ANY for r in riders for _ in r.out_shape],
        out_shape=list(out_shape) + [s for r in riders for s in r.out_shape],
        scratch_shapes=list(scratch_shapes) + [s for r in riders for s in r.sems],
        compiler_params=compiler_params)

    def run(*args):
        res = call(*args, *[a for r in riders for a in r.inputs])
        extras, pos = [], n_out
        for r in riders:
            extras.append(list(res[pos:pos + len(r.out_shape)]))
            pos += len(r.out_shape)
        return list(res[:n_out]), extras

    return run


def _ffn_fwd(x, g, ffnw, tag, tm, riders=(), base=None):
    t = x.shape[0]
    width = ffnw.shape[1]

    def body(*refs):
        x_ref, base_ref = refs[0], refs[0 if base is None else 1]
        g_ref, w1_ref, w3_ref, w2_ref, xo_ref, a_ref, b_ref = refs[1 if base is None else 2:]
        _, n = _rms(x_ref[...])
        h = (n * g_ref[...]).astype(BF16)
        a = _dot_nt(h, w1_ref[...])
        b = _dot_nt(h, w3_ref[...])
        s = (a * _sigmoid(a) * b).astype(BF16)
        xo_ref[...] = base_ref[...] + 0.5 * _dot(s, w2_ref[...])
        a_ref[...] = a.astype(BF16)
        b_ref[...] = b.astype(BF16)

    acts = [x] if base is None else [x, base]
    return _hosted(
        riders, body, name=f"ffn_fwd_{tag}", grid=(t // tm,),
        in_specs=[_rows(tm, D_MODEL)] * len(acts) + [_resident((1, D_MODEL)), _pick(ffnw.shape, 0), _pick(ffnw.shape, 1),
                                                     _pick(ffnw.shape, 2)],
        out_specs=[_rows(tm, D_MODEL), _rows(tm, width), _rows(tm, width)],
        out_shape=[jax.ShapeDtypeStruct((t, D_MODEL), F32), jax.ShapeDtypeStruct((t, width), BF16),
                   jax.ShapeDtypeStruct((t, width), BF16)],
        compiler_params=_params(("arbitrary",)),
    )(*acts, g, ffnw, ffnw, ffnw)


def _ffn_bwd(dxo, x, g, a, b, ffnw, tag, tm, riders=()):
    t = x.shape[0]
    n_pieces = len(a)

    def body(dxo_ref, x_ref, g_ref, *refs):
        a_refs, b_refs = refs[:n_pieces], refs[n_pieces:2 * n_pieces]
        w1_ref, w3_ref, w2_ref, dx_ref, dab_ref, s_ref, h_ref, dxh_ref, dg_ref = refs[2 * n_pieces:]
        whole = lambda pieces: jnp.concatenate([r[...] for r in pieces], axis=1).astype(F32)

        @pl.when(pl.program_id(0) == 0)
        def _():
            dg_ref[...] = jnp.zeros_like(dg_ref)

        xv = x_ref[...]
        gv = g_ref[...]
        r, n = _rms(xv)
        h_ref[...] = (n * gv).astype(BF16)
        dxo_v = dxo_ref[...]
        dxh = (0.5 * dxo_v).astype(BF16)
        dxh_ref[...] = dxh
        ds = _dot_nt(dxh, w2_ref[...])
        av = whole(a_refs)
        bv = whole(b_refs)
        sg = _sigmoid(av)
        silu = av * sg
        s_ref[...] = (silu * bv).astype(BF16)
        da = (ds * bv * (sg * (1.0 + av * (1.0 - sg)))).astype(BF16)
        db = (ds * silu).astype(BF16)
        dab_ref[:, :D_FF] = da
        dab_ref[:, D_FF:] = db
        dh = _dot(da, w1_ref[...]) + _dot(db, w3_ref[...])
        dg_ref[...] += _colsum(dh * n)
        dx_ref[...] = dxo_v + _rms_bwd(dh * gv, n, r)

    return _hosted(
        riders, body, name=f"ffn_bwd_{tag}", grid=(t // tm,),
        in_specs=[_rows(tm, D_MODEL), _rows(tm, D_MODEL), _resident((1, D_MODEL))] + [_rows(tm, p.shape[1]) for p in a + b]
        + [_pick(ffnw.shape, 0), _pick(ffnw.shape, 1), _pick(ffnw.shape, 2)],
        out_specs=[_rows(tm, D_MODEL), _rows(tm, 2 * D_FF), _rows(tm, D_FF), _rows(tm, D_MODEL), _rows(tm, D_MODEL),
                   _acc((1, D_MODEL))],
        out_shape=[jax.ShapeDtypeStruct((t, D_MODEL), F32), jax.ShapeDtypeStruct((t, 2 * D_FF), BF16),
                   jax.ShapeDtypeStruct((t, D_FF), BF16), jax.ShapeDtypeStruct((t, D_MODEL), BF16),
                   jax.ShapeDtypeStruct((t, D_MODEL), BF16), jax.ShapeDtypeStruct((1, D_MODEL), F32)],
        compiler_params=_params(("arbitrary",)),
    )(dxo, x, g, *a, *b, ffnw, ffnw, ffnw)


def _wgrad(xm, dy, out_blocks, name, tk, riders=()):
    t, m = xm.shape
    n = dy.shape[1]
    mb = m // out_blocks

    def body(x_ref, dy_ref, o_ref):
        @pl.when(pl.program_id(1) == 0)
        def _():
            o_ref[...] = jnp.zeros_like(o_ref)

        o_ref[...] += _dot_tn(x_ref[...], dy_ref[...])

    return _hosted(
        riders, body, name=name, grid=(out_blocks, t // tk),
        in_specs=[pl.BlockSpec((tk, mb), lambda j, k: (k, j)), pl.BlockSpec((tk, n), lambda j, k: (k, 0))],
        out_specs=[pl.BlockSpec((None, mb, n), lambda j, k: (j, 0, 0))],
        out_shape=[jax.ShapeDtypeStruct((out_blocks, mb, n), F32)],
        compiler_params=_params(("arbitrary", "arbitrary")),
    )(xm, dy)


def _wgrad_cols(xm, dy, out_blocks, name, tk, riders=()):
    t, m = xm.shape
    n = dy.shape[1]
    nb = n // out_blocks

    def body(x_ref, dy_ref, o_ref):
        @pl.when(pl.program_id(1) == 0)
        def _():
            o_ref[...] = jnp.zeros_like(o_ref)

        o_ref[...] += _dot_tn(x_ref[...], dy_ref[...])

    return _hosted(
        riders, body, name=name, grid=(out_blocks, t // tk),
        in_specs=[pl.BlockSpec((tk, m), lambda j, k: (k, 0)), pl.BlockSpec((tk, nb), lambda j, k: (k, j))],
        out_specs=[pl.BlockSpec((None, m, nb), lambda j, k: (j, 0, 0))],
        out_shape=[jax.ShapeDtypeStruct((out_blocks, m, nb), F32)],
        compiler_params=_params(("arbitrary", "arbitrary")),
    )(xm, dy)


def _mix_fwd(x1, g, winw, tm, riders=()):
    t = x1.shape[0]

    def body(x_ref, g_ref, w_ref, main_ref, pool_ref, gate_ref):
        _, n = _rms(x_ref[...])
        h = (n * g_ref[...]).astype(BF16)
        proj = jnp.concatenate([_dot(h, w_ref[j]) for j in range(N_CHIPS)], axis=1)
        main_ref[...] = proj[:, :MAIN_COLS]
        pool_ref[...] = proj[:, MAIN_COLS:MAIN_COLS + POOL_WIDTH]
        gate_ref[...] = proj[:, MAIN_COLS + POOL_WIDTH:]

    return _hosted(
        riders, body, name="mix_fwd", grid=(t // tm,),
        in_specs=[_rows(tm, D_MODEL), _resident((1, D_MODEL)), _resident(winw.shape)],
        out_specs=[_rows(tm, MAIN_COLS), _rows(tm, POOL_WIDTH), _rows(tm, GATE_COLS)],
        out_shape=[jax.ShapeDtypeStruct((t, MAIN_COLS), F32), jax.ShapeDtypeStruct((t, POOL_WIDTH), F32),
                   jax.ShapeDtypeStruct((t, GATE_COLS), F32)],
        compiler_params=_params(("arbitrary",)),
    )(x1, g, winw)


def _mix_bwd(dqfi, dog, du, dgates, dx2, x1, g, winw, tm, riders=()):
    t = x1.shape[0]
    cols = N_CHIPS * SHARD_IN_COLS

    def body(dqfi_ref, dog_ref, du_ref, dgt_ref, dx2_ref, x_ref, g_ref, w_ref, dx_ref, dproj_ref, h_ref, dg_ref):
        @pl.when(pl.program_id(0) == 0)
        def _():
            dg_ref[...] = jnp.zeros_like(dg_ref)

        dproj = jnp.concatenate([dqfi_ref[...], dog_ref[...], du_ref[...], dgt_ref[...]], axis=1)
        dproj_ref[...] = dproj
        dh = _dot_nt(dproj[:, :SHARD_IN_COLS], w_ref[0])
        for j in range(1, N_CHIPS):
            dh += _dot_nt(dproj[:, j * SHARD_IN_COLS:(j + 1) * SHARD_IN_COLS], w_ref[j])
        gv = g_ref[...]
        r, n = _rms(x_ref[...])
        h_ref[...] = (n * gv).astype(BF16)
        dg_ref[...] += _colsum(dh * n)
        dx_ref[...] = dx2_ref[...] + _rms_bwd(dh * gv, n, r)

    return _hosted(
        riders, body, name="mix_bwd", grid=(t // tm,),
        in_specs=[_rows(tm, 3 * D_MODEL), _rows(tm, D_MODEL), _rows(tm, POOL_WIDTH), _rows(tm, GATE_COLS),
                  _rows(tm, D_MODEL), _rows(tm, D_MODEL), _resident((1, D_MODEL)), _resident(winw.shape)],
        out_specs=[_rows(tm, D_MODEL), _rows(tm, cols), _rows(tm, D_MODEL), _acc((1, D_MODEL))],
        out_shape=[jax.ShapeDtypeStruct((t, D_MODEL), F32), jax.ShapeDtypeStruct((t, cols), BF16),
                   jax.ShapeDtypeStruct((t, D_MODEL), BF16), jax.ShapeDtypeStruct((1, D_MODEL), F32)],
        compiler_params=_params(("arbitrary",)),
    )(dqfi, dog, du, dgates, dx2, x1, g, winw)


def _lower_bound(lb_raw):
    l0 = lb_raw[0:1, :]
    l1 = lb_raw[1:2, :]
    m = jnp.maximum(l0, l1)
    e0 = jnp.exp(l0 - m)
    e1 = jnp.exp(l1 - m)
    return e0 / (e0 + e1)


def _head_slices():
    return [slice(h * HEAD_DIM, (h + 1) * HEAD_DIM) for h in range(HEADS)]


def _gates(qr, fr, lb, tril_b, first_half):
    sg = _sigmoid(fr)
    f = lb + (1.0 - lb) * sg
    k = 1.0 - f
    sq = _sigmoid(qr)
    q = qr * sq
    log_f = jnp.log(f)
    gc = _tri_sum(tril_b, log_f)
    gm = _colsum(jnp.where(first_half, log_f, 0.0))
    gl = _colsum(log_f)
    e_q = jnp.exp(gc - gm)
    e_k = jnp.exp(gm - gc)
    e_in = jnp.exp(gc)
    e_out = jnp.exp(gl - gc)
    return dict(sg=sg, f=f, k=k, sq=sq, q=q, e_q=e_q, e_k=e_k, e_in=e_in, e_out=e_out, e_last=jnp.exp(gl))


def _hgrn_fwd(main, lb_raw, tt, riders=()):
    t = main.shape[0]
    n_local = tt // CHUNK

    def body(q_ref, f_ref, i_ref, lb_ref, o_ref, st_ref, s_scr):
        @pl.when(pl.program_id(0) == 0)
        def _():
            s_scr[...] = jnp.zeros_like(s_scr)

        lb = _lower_bound(lb_ref[...])
        row = lax.broadcasted_iota(jnp.int32, (CHUNK, CHUNK), 0)
        col = lax.broadcasted_iota(jnp.int32, (CHUNK, CHUNK), 1)
        tril = row >= col
        tril_b = tril.astype(BF16)
        first_half = lax.broadcasted_iota(jnp.int32, (CHUNK, D_MODEL), 0) < CHUNK // 2
        heads = _head_slices()

        def chunk(c, carry):
            rows = pl.ds(pl.multiple_of(c * CHUNK, CHUNK), CHUNK)
            z = _gates(q_ref[rows, :], f_ref[rows, :], lb, tril_b, first_half)
            qt = (z["q"] * z["e_q"]).astype(BF16)
            kt = (z["k"] * z["e_k"]).astype(BF16)
            qg = (z["q"] * z["e_in"]).astype(BF16)
            kg = (z["k"] * z["e_out"]).astype(BF16)
            vb = i_ref[rows, :].astype(BF16)
            states = [s_scr[h] for h in range(HEADS)]
            for h in range(HEADS):
                st_ref[c, h] = states[h]
            raw = [_dot_nt(qt[:, sl], kt[:, sl]) for sl in heads]
            inter = [_dot_nt(qg[:, sl], states[h].astype(BF16)) for h, sl in enumerate(heads)]
            grown = [_dot_tn(vb[:, sl], kg[:, sl]) for sl in heads]
            scores = [jnp.where(tril, r, 0.0).astype(BF16) for r in raw]
            for h, sl in enumerate(heads):
                s_scr[h] = states[h] * z["e_last"][:, sl] + grown[h]
            o_ref[rows, :] = jnp.concatenate([_dot(scores[h], vb[:, sl]) + inter[h] for h, sl in enumerate(heads)], axis=1)
            return carry

        lax.fori_loop(0, n_local, chunk, 0, unroll=2)

    return _hosted(
        riders, body, name="hgrn_fwd", grid=(t // tt,),
        in_specs=[_rows(tt, D_MODEL, 0), _rows(tt, D_MODEL, 1), _rows(tt, D_MODEL, 2), _resident((2, D_MODEL))],
        out_specs=[_rows(tt, D_MODEL),
                   pl.BlockSpec((n_local, HEADS, HEAD_DIM, HEAD_DIM), lambda i: (i, 0, 0, 0))],
        out_shape=[jax.ShapeDtypeStruct((t, D_MODEL), F32),
                   jax.ShapeDtypeStruct((t // CHUNK, HEADS, HEAD_DIM, HEAD_DIM), F32)],
        scratch_shapes=[pltpu.VMEM((HEADS, HEAD_DIM, HEAD_DIM), F32)],
        compiler_params=_params(("arbitrary",)),
    )(main, main, main, lb_raw)


def _hgrn_bwd(main, lb_raw, states, do, tt, riders=()):
    t = main.shape[0]
    n_tiles = t // tt
    n_local = tt // CHUNK

    def rev(col_block):
        return pl.BlockSpec((tt, D_MODEL), lambda i: (n_tiles - 1 - i, col_block))

    def body(q_ref, f_ref, i_ref, lb_ref, st_ref, do_ref, dqfi_ref, dlb_ref, ds_scr, acc_scr):
        @pl.when(pl.program_id(0) == 0)
        def _():
            ds_scr[...] = jnp.zeros_like(ds_scr)
            acc_scr[...] = jnp.zeros_like(acc_scr)

        lb = _lower_bound(lb_ref[...])
        row = lax.broadcasted_iota(jnp.int32, (CHUNK, CHUNK), 0)
        col = lax.broadcasted_iota(jnp.int32, (CHUNK, CHUNK), 1)
        tril = row >= col
        tril_b = tril.astype(BF16)
        triu_b = (row <= col).astype(BF16)
        first_half = lax.broadcasted_iota(jnp.int32, (CHUNK, D_MODEL), 0) < CHUNK // 2
        heads = _head_slices()
        cat = functools.partial(jnp.concatenate, axis=1)

        def chunk(cc, carry):
            c = n_local - 1 - cc
            rows = pl.ds(pl.multiple_of(c * CHUNK, CHUNK), CHUNK)
            qr = q_ref[rows, :]
            z = _gates(qr, f_ref[rows, :], lb, tril_b, first_half)
            qt = (z["q"] * z["e_q"]).astype(BF16)
            kt = (z["k"] * z["e_k"]).astype(BF16)
            qg_f = z["q"] * z["e_in"]
            qg = qg_f.astype(BF16)
            kg_f = z["k"] * z["e_out"]
            kg = kg_f.astype(BF16)
            vb = i_ref[rows, :].astype(BF16)
            dob = do_ref[rows, :].astype(BF16)
            st = [st_ref[c, h] for h in range(HEADS)]
            dst = [ds_scr[h] for h in range(HEADS)]
            dst_b = [d.astype(BF16) for d in dst]
            raw = [_dot_nt(qt[:, sl], kt[:, sl]) for sl in heads]
            draw = [_dot_nt(dob[:, sl], vb[:, sl]) for sl in heads]
            dqg = [_dot(dob[:, sl], st[h].astype(BF16)) for h, sl in enumerate(heads)]
            dkg = [_dot(vb[:, sl], dst_b[h]) for h, sl in enumerate(heads)]
            dv_inter = [_dot_nt(kg[:, sl], dst_b[h]) for h, sl in enumerate(heads)]
            grown = [_dot_tn(dob[:, sl], qg[:, sl]) for sl in heads]
            scores = [jnp.where(tril, r, 0.0).astype(BF16) for r in raw]
            dscores = [jnp.where(tril, r, 0.0).astype(BF16) for r in draw]
            dqt = [_dot(dscores[h], kt[:, sl]) for h, sl in enumerate(heads)]
            dkt = [_dot_tn(dscores[h], qt[:, sl]) for h, sl in enumerate(heads)]
            dv = [_dot_tn(scores[h], dob[:, sl]) + dv_inter[h] for h, sl in enumerate(heads)]
            carry_in = cat([z["e_last"][:, sl] * _colsum(dst[h] * st[h]) for h, sl in enumerate(heads)])
            for h, sl in enumerate(heads):
                ds_scr[h] = dst[h] * z["e_last"][:, sl] + grown[h]
            dqt, dkt, dqg, dkg = cat(dqt), cat(dkt), cat(dqg), cat(dkg)
            carry_in += _colsum(dkg * kg_f)
            dq = dqt * z["e_q"] + dqg * z["e_in"]
            dk = dkt * z["e_k"] + dkg * z["e_out"]
            dgate = (qt.astype(F32) * dqt - kt.astype(F32) * dkt) + (qg_f * dqg - kg_f * dkg)
            dlogf = _tri_sum(triu_b, dgate) + carry_in
            df = dlogf / z["f"] - dk
            sg = z["sg"]
            sq = z["sq"]
            acc_scr[...] += _colsum(df * (1.0 - sg))
            dqfi_ref[rows, 0:D_MODEL] = (dq * (sq * (1.0 + qr * (1.0 - sq)))).astype(BF16)
            dqfi_ref[rows, D_MODEL:2 * D_MODEL] = (df * (1.0 - lb) * sg * (1.0 - sg)).astype(BF16)
            dqfi_ref[rows, 2 * D_MODEL:3 * D_MODEL] = cat(dv).astype(BF16)
            return carry

        lax.fori_loop(0, n_local, chunk, 0, unroll=2)
        d0 = acc_scr[...] * lb * (1.0 - lb)
        dlb_ref[0:1, :] = d0
        dlb_ref[1:2, :] = -d0

    return _hosted(
        riders, body, name="hgrn_bwd", grid=(n_tiles,),
        in_specs=[rev(0), rev(1), rev(2), _resident((2, D_MODEL)),
                  pl.BlockSpec((n_local, HEADS, HEAD_DIM, HEAD_DIM), lambda i: (n_tiles - 1 - i, 0, 0, 0)),
                  rev(0)],
        out_specs=[pl.BlockSpec((tt, 3 * D_MODEL), lambda i: (n_tiles - 1 - i, 0)), _acc((2, D_MODEL))],
        out_shape=[jax.ShapeDtypeStruct((t, 3 * D_MODEL), BF16), jax.ShapeDtypeStruct((2, D_MODEL), F32)],
        scratch_shapes=[pltpu.VMEM((HEADS, HEAD_DIM, HEAD_DIM), F32), pltpu.VMEM((1, D_MODEL), F32)],
        compiler_params=_params(("arbitrary",)),
    )(main, main, main, lb_raw, states, do)


def _head_norm(o):
    rs, ns = [], []
    for h in range(HEADS):
        oh = o[:, h * HEAD_DIM:(h + 1) * HEAD_DIM]
        r, n = _rms(oh)
        rs.append(jnp.broadcast_to(r, oh.shape))
        ns.append(n)
    return jnp.concatenate(rs, axis=1), jnp.concatenate(ns, axis=1)


def _head_norm_bwd(dn, n, r):
    outs = []
    for h in range(HEADS):
        sl = slice(h * HEAD_DIM, (h + 1) * HEAD_DIM)
        outs.append(_rms_bwd(dn[:, sl], n[:, sl], r[:, sl]))
    return jnp.concatenate(outs, axis=1)


def _window_counts(first_row, tm):
    pos = (first_row + 1 + lax.broadcasted_iota(jnp.int32, (tm, 1), 0)).astype(F32)
    return [jnp.minimum(pos, float(w)) for w in POOL_WINDOWS]


def _post_fwd(o, main, pool_r, gates, x1, onorm, pool_w, pool_scale, sqw, wbw, tm, riders=()):
    t = o.shape[0]
    ext_rows = tm + POOL_HALO

    def body(o_ref, og_ref, u_ref, gt_ref, x1_ref, on_ref, pw_ref, ps_ref, wa_ref, wout_ref, wb_ref,
             x2_ref, ya_ref, yb_ref, pooled_ref, ext):
        i = pl.program_id(0)

        @pl.when(i == 0)
        def _():
            ext[0:POOL_HALO, :] = jnp.zeros((POOL_HALO, POOL_WIDTH), F32)

        _, n = _head_norm(o_ref[...])
        og = og_ref[...]
        oa = (n * on_ref[...] * (og * _sigmoid(og))).astype(BF16)
        ya = _dot(oa, wa_ref[...])

        u = u_ref[...]
        ext[POOL_HALO:ext_rows, :] = u
        e = ext[...]
        counts = _window_counts(i * tm, tm)
        pooled = []
        for gidx, w in enumerate(POOL_WINDOWS):
            s = e[:, gidx * HEAD_DIM:(gidx + 1) * HEAD_DIM]
            shift = 1
            while shift < w:
                s = s + pltpu.roll(s, shift, axis=0)
                shift *= 2
            pooled.append(s[POOL_HALO:, :] / counts[gidx] - u[:, gidx * HEAD_DIM:(gidx + 1) * HEAD_DIM])
        ext[0:POOL_HALO, :] = ext[tm:ext_rows, :]
        pooled_b = [pg.astype(BF16) for pg in pooled]
        pooled_ref[...] = jnp.concatenate(pooled_b, axis=1)
        mixed = jnp.concatenate([_dot(pooled_b[gidx], pw_ref[gidx].astype(BF16)) for gidx in range(len(POOL_WINDOWS))],
                                axis=1) * ps_ref[...]
        mixed_b = mixed.astype(BF16)
        yb = jnp.concatenate([_dot(mixed_b, wb_ref[j]) for j in range(N_CHIPS)], axis=1)

        gt = gt_ref[...]
        y = _sigmoid(gt[:, :D_MODEL]) * ya + _sigmoid(gt[:, D_MODEL:]) * yb
        x2_ref[...] = x1_ref[...] + _dot(y.astype(BF16), wout_ref[...])
        ya_ref[...] = ya.astype(BF16)
        yb_ref[...] = yb.astype(BF16)

    return _hosted(
        riders, body, name="post_fwd", grid=(t // tm,),
        in_specs=[_rows(tm, D_MODEL), _rows(tm, D_MODEL, 3), _rows(tm, POOL_WIDTH), _rows(tm, GATE_COLS), _rows(tm, D_MODEL),
                  _resident((1, D_MODEL)), _resident(pool_w.shape), _resident((1, POOL_WIDTH)),
                  _pick(sqw.shape, 0), _pick(sqw.shape, 1), _resident(wbw.shape)],
        out_specs=[_rows(tm, D_MODEL), _rows(tm, D_MODEL), _rows(tm, D_MODEL), _rows(tm, POOL_WIDTH)],
        out_shape=[jax.ShapeDtypeStruct((t, D_MODEL), F32), jax.ShapeDtypeStruct((t, D_MODEL), BF16),
                   jax.ShapeDtypeStruct((t, D_MODEL), BF16), jax.ShapeDtypeStruct((t, POOL_WIDTH), BF16)],
        scratch_shapes=[pltpu.VMEM((ext_rows, POOL_WIDTH), F32)],
        compiler_params=_params(("arbitrary",)),
    )(o, main, pool_r, gates, x1, onorm, pool_w, pool_scale, sqw, sqw, wbw)


def _post_bwd(dx2, o, main, gates, ya, yb, pooled, onorm, pool_w, pool_scale, sqw, wbw, tm, riders=()):
    t = o.shape[0]
    n_tiles = t // tm
    ext_rows = tm + POOL_HALO
    n_groups = len(POOL_WINDOWS)

    def rev(cols, col_block=0):
        return pl.BlockSpec((tm, cols), lambda i: (n_tiles - 1 - i, col_block))

    def body(dx2_ref, o_ref, og_ref, gt_ref, ya_ref, yb_ref, pooled_ref, on_ref, pw_ref, ps_ref, wa_ref, wout_ref, wb_ref,
             do_ref, dog_ref, du_ref, dgt_ref, dwa_ref, dwout_ref, dwb_ref, dpw_ref, dps_ref, don_ref, ext):
        i = pl.program_id(0)

        @pl.when(i == 0)
        def _():
            ext[tm:ext_rows, :] = jnp.zeros((POOL_HALO, POOL_WIDTH), F32)
            for ref in (dwa_ref, dwout_ref, dwb_ref, dpw_ref, dps_ref, don_ref):
                ref[...] = jnp.zeros_like(ref)

        groups = [slice(gidx * HEAD_DIM, (gidx + 1) * HEAD_DIM) for gidx in range(n_groups)]
        shards = [slice(j * 256, (j + 1) * 256) for j in range(N_CHIPS)]
        dx2b = dx2_ref[...].astype(BF16)
        dy = _dot_nt(dx2b, wout_ref[...])
        pooled_b = pooled_ref[...]
        pm = jnp.concatenate([_dot(pooled_b[:, g], pw_ref[gidx].astype(BF16)) for gidx, g in enumerate(groups)], axis=1)
        gt = gt_ref[...]
        sga = _sigmoid(gt[:, :D_MODEL])
        sgb = _sigmoid(gt[:, D_MODEL:])
        ya = ya_ref[...].astype(F32)
        yb = yb_ref[...].astype(F32)
        y = (sga * ya + sgb * yb).astype(BF16)
        dya = (dy * sga).astype(BF16)
        dyb = (dy * sgb).astype(BF16)
        dgt_ref[:, :D_MODEL] = (dy * ya * sga * (1.0 - sga)).astype(BF16)
        dgt_ref[:, D_MODEL:] = (dy * yb * sgb * (1.0 - sgb)).astype(BF16)
        dwout_ref[...] += _dot_tn(y, dx2b)
        doa = _dot_nt(dya, wa_ref[...])
        dmixed = _dot_nt(dyb[:, shards[0]], wb_ref[0])
        for j in range(1, N_CHIPS):
            dmixed += _dot_nt(dyb[:, shards[j]], wb_ref[j])
        r, n = _head_norm(o_ref[...])
        onv = on_ref[...]
        og = og_ref[...]
        sog = _sigmoid(og)
        silu_og = og * sog
        normed = n * onv
        oa = (normed * silu_og).astype(BF16)
        dog_ref[...] = (doa * normed * (sog * (1.0 + og * (1.0 - sog)))).astype(BF16)
        dnormed = doa * silu_og
        don_ref[...] += _colsum(dnormed * n)
        do_ref[...] = _head_norm_bwd(dnormed * onv, n, r)
        psv = ps_ref[...]
        mixed_b = (pm * psv).astype(BF16)
        dps_ref[...] += _colsum(dmixed * pm)
        dpm = (dmixed * psv).astype(BF16)
        dwa_ref[...] += _dot_tn(oa, dya)
        for j in range(N_CHIPS):
            dwb_ref[j] += _dot_tn(mixed_b, dyb[:, shards[j]])
        counts = _window_counts((n_tiles - 1 - i) * tm, tm)
        dpooled = []
        for gidx, g in enumerate(groups):
            dpw_ref[gidx] += _dot_tn(pooled_b[:, g], dpm[:, g])
            dpooled.append(_dot_nt(dpm[:, g], pw_ref[gidx].astype(BF16)))
        ext[0:tm, :] = jnp.concatenate([dpooled[gidx] / counts[gidx] for gidx in range(n_groups)], axis=1)
        e = ext[...]
        du = []
        for gidx, w in enumerate(POOL_WINDOWS):
            s = e[:, gidx * HEAD_DIM:(gidx + 1) * HEAD_DIM]
            shift = 1
            while shift < w:
                s = s + pltpu.roll(s, ext_rows - shift, axis=0)
                shift *= 2
            du.append(s[:tm, :] - dpooled[gidx])
        ext[tm:ext_rows, :] = ext[0:POOL_HALO, :]
        du_ref[...] = jnp.concatenate(du, axis=1).astype(BF16)

    wa_shape = (D_MODEL, D_MODEL)
    return _hosted(
        riders, body, name="post_bwd", grid=(n_tiles,),
        in_specs=[rev(D_MODEL), rev(D_MODEL), rev(D_MODEL, 3), rev(GATE_COLS), rev(D_MODEL), rev(D_MODEL), rev(POOL_WIDTH),
                  _resident((1, D_MODEL)), _resident(pool_w.shape), _resident((1, POOL_WIDTH)),
                  _pick(sqw.shape, 0), _pick(sqw.shape, 1), _resident(wbw.shape)],
        out_specs=[rev(D_MODEL), rev(D_MODEL), rev(POOL_WIDTH), rev(GATE_COLS),
                   _acc(wa_shape), _acc(wa_shape), _acc(wbw.shape), _acc(pool_w.shape), _acc((1, POOL_WIDTH)),
                   _acc((1, D_MODEL))],
        out_shape=[jax.ShapeDtypeStruct((t, D_MODEL), F32), jax.ShapeDtypeStruct((t, D_MODEL), BF16),
                   jax.ShapeDtypeStruct((t, POOL_WIDTH), BF16), jax.ShapeDtypeStruct((t, GATE_COLS), BF16),
                   jax.ShapeDtypeStruct(wa_shape, F32), jax.ShapeDtypeStruct(wa_shape, F32),
                   jax.ShapeDtypeStruct(wbw.shape, F32), jax.ShapeDtypeStruct(pool_w.shape, F32),
                   jax.ShapeDtypeStruct((1, POOL_WIDTH), F32), jax.ShapeDtypeStruct((1, D_MODEL), F32)],
        scratch_shapes=[pltpu.VMEM((ext_rows, POOL_WIDTH), F32)],
        compiler_params=_params(("arbitrary",)),
    )(dx2, o, main, gates, ya, yb, pooled, onorm, pool_w, pool_scale, sqw, sqw, wbw)


def _tail(x3, p, target, g_ple, g_post, g_final, sqw, wpw, tm, riders=()):
    t = x3.shape[0]
    pd = p.shape[1]

    def body(x_ref, p_ref, tg_ref, g4_ref, g5_ref, g6_ref, wg_ref, wp_ref,
             dx_ref, loss_ref, dwg_ref, dwp_ref, dg4_ref, dg5_ref, dg6_ref):
        @pl.when(pl.program_id(0) == 0)
        def _():
            for ref in (loss_ref, dwg_ref, dwp_ref, dg4_ref, dg5_ref, dg6_ref):
                ref[...] = jnp.zeros_like(ref)

        x3v = x_ref[...]
        g4, g5, g6 = g4_ref[...], g5_ref[...], g6_ref[...]
        r4, n4 = _rms(x3v)
        h4 = (n4 * g4).astype(BF16)
        gate = _sigmoid(_dot(h4, wg_ref[...]))
        pb = p_ref[...].astype(BF16)
        r5, n5 = _rms(jnp.concatenate([_dot(pb, wp_ref[j]) for j in range(N_CHIPS)], axis=1))
        emb = n5 * g5
        r6, n6 = _rms(x3v + gate * emb)
        diff = n6 * g6 - tg_ref[...]
        loss_ref[...] += 0.5 * jnp.sum(jnp.mean(diff * diff, axis=-1, keepdims=True), axis=0, keepdims=True)
        dout = diff * (1.0 / D_MODEL)
        dg6_ref[...] += _colsum(dout * n6)
        dx4 = _rms_bwd(dout * g6, n6, r6)
        demb = dx4 * gate
        dg5_ref[...] += _colsum(demb * n5)
        dpre = _rms_bwd(demb * g5, n5, r5).astype(BF16)
        for j in range(N_CHIPS):
            dwp_ref[j] += _dot_tn(pb, dpre[:, j * pd:(j + 1) * pd])
        dz = (dx4 * emb * gate * (1.0 - gate)).astype(BF16)
        dwg_ref[...] += _dot_tn(h4, dz)
        dh4 = _dot_nt(dz, wg_ref[...])
        dg4_ref[...] += _colsum(dh4 * n4)
        dx_ref[...] = dx4 + _rms_bwd(dh4 * g4, n4, r4)

    sq_shape = (D_MODEL, D_MODEL)
    vec = (1, D_MODEL)
    return _hosted(
        riders, body, name="tail", grid=(t // tm,),
        in_specs=[_rows(tm, D_MODEL), _rows(tm, pd), _rows(tm, D_MODEL), _resident(vec), _resident(vec), _resident(vec),
                  _pick(sqw.shape, 2), _resident(wpw.shape)],
        out_specs=[_rows(tm, D_MODEL), _acc((1, 1)), _acc(sq_shape), _acc(wpw.shape), _acc(vec), _acc(vec), _acc(vec)],
        out_shape=[jax.ShapeDtypeStruct((t, D_MODEL), F32), jax.ShapeDtypeStruct((1, 1), F32),
                   jax.ShapeDtypeStruct(sq_shape, F32), jax.ShapeDtypeStruct(wpw.shape, F32),
                   jax.ShapeDtypeStruct(vec, F32), jax.ShapeDtypeStruct(vec, F32), jax.ShapeDtypeStruct(vec, F32)],
        compiler_params=_params(("arbitrary",)),
    )(x3, p, target, g_ple, g_post, g_final, sqw, wpw)


def _position():
    return lax.axis_index("x"), lax.axis_index("y"), lax.axis_index("c")


def _other_chips(x, y):
    return [(1 - x, y), (x, 1 - y), (1 - x, 1 - y)]


def _remote(src, dst, send_sems, recv_sems, k, device):
    return pltpu.make_async_remote_copy(src_ref=src, dst_ref=dst, send_sem=send_sems.at[k], recv_sem=recv_sems.at[k],
                                        device_id=device, device_id_type=MESH)


def _gather_rider(shards, forward_at):
    n = len(shards)

    def copies(ins, outs, sems):
        send_sems, recv_sems = sems
        x, y, c = _position()
        mine = 2 * x + y
        first, passed, arriving = [], [], []
        for k, (cx, cy) in enumerate(_other_chips(x, y)):
            theirs = 2 * cx + cy
            for a in range(n):
                first.append(_remote(ins[a].at[:, c], outs[a].at[:, mine, c], send_sems, recv_sems, k * n + a, (cx, cy, c)))
                block = outs[a].at[:, theirs, c]
                passed.append(_remote(block, block, send_sems, recv_sems, (3 + k) * n + a, (x, y, 1 - c)))
                other = outs[a].at[:, theirs, 1 - c]
                arriving.append(_remote(other, other, send_sems, recv_sems, (3 + k) * n + a, (x, y, 1 - c)))
        return first, passed, arriving

    return _Rider(shards, [jax.ShapeDtypeStruct((s.shape[0], N_CHIPS) + s.shape[1:], s.dtype) for s in shards],
                  [pltpu.SemaphoreType.DMA((6 * n,)), pltpu.SemaphoreType.DMA((6 * n,))], _gather_phases(copies, forward_at))


def _gather_phases(copies, forward_at):
    def begin(ins, outs, sems):
        for cp in copies(ins, outs, sems)[0]:
            cp.start()

    def forward(ins, outs, sems):
        first, passed, _ = copies(ins, outs, sems)
        for got, cp in zip(first, passed):
            got.wait_recv()
            cp.start()

    def finish(ins, outs, sems):
        first, passed, arriving = copies(ins, outs, sems)
        for cp in arriving:
            cp.wait_recv()
        for cp in first + passed:
            cp.wait_send()

    return [(0, begin), (forward_at, forward), (1, finish)]


def _with_own(gathered, shard, slot):
    return lax.dynamic_update_slice(gathered, shard[:, None], (0, slot, 0, 0, 0))


def _gather_from(shards, peers, n_peers, first_slot, forward_at):
    n = len(shards)

    def copies(ins, outs, sems):
        send_sems, recv_sems = sems
        x, y, c = _position()
        first, passed, arriving = [], [], []
        for k, (cx, cy) in enumerate(peers(x, y)):
            slot = first_slot + k
            for a in range(n):
                first.append(_remote(ins[a].at[:, c], outs[a].at[:, slot, c], send_sems, recv_sems, k * n + a, (cx, cy, c)))
                block = outs[a].at[:, slot, c]
                passed.append(_remote(block, block, send_sems, recv_sems, (n_peers + k) * n + a, (x, y, 1 - c)))
                other = outs[a].at[:, slot, 1 - c]
                arriving.append(_remote(other, other, send_sems, recv_sems, (n_peers + k) * n + a, (x, y, 1 - c)))
        return first, passed, arriving

    return _Rider(shards, [jax.ShapeDtypeStruct((s.shape[0], first_slot + n_peers) + s.shape[1:], s.dtype) for s in shards],
                  [pltpu.SemaphoreType.DMA((2 * n_peers * n,)), pltpu.SemaphoreType.DMA((2 * n_peers * n,))],
                  _gather_phases(copies, forward_at))


def _exchange_rider(arrays, out_shape, n_copies, transfers, n_local=0):
    def copies(ins, outs, sems):
        send_sems, recv_sems, local_sems = sems
        remote, local = transfers(ins, outs)
        return ([_remote(src, dst, send_sems, recv_sems, i, dev) for i, (src, dst, dev) in enumerate(remote)],
                [pltpu.make_async_copy(src, dst, local_sems.at[i]) for i, (src, dst) in enumerate(local)])

    def begin(ins, outs, sems):
        remote, local = copies(ins, outs, sems)
        for cp in remote + local:
            cp.start()

    def finish(ins, outs, sems):
        remote, local = copies(ins, outs, sems)
        for cp in remote:
            cp.wait_recv()
        for cp in remote:
            cp.wait_send()
        for cp in local:
            cp.wait()

    return _Rider(arrays, out_shape,
                  [pltpu.SemaphoreType.DMA((max(n_copies, 1),)), pltpu.SemaphoreType.DMA((max(n_copies, 1),)),
                   pltpu.SemaphoreType.DMA((max(n_local, 1),))],
                  [(0, begin), (1, finish)])


def _pair_rider(partials):
    def transfers(ins, outs):
        x, y, c = _position()
        return [(ins[a].at[:, :, 1 - c], outs[a], (x, y, 1 - c)) for a in range(len(partials))], []

    shapes = [jax.ShapeDtypeStruct(g.shape[:2] + g.shape[3:], g.dtype) for g in partials]
    return _exchange_rider(partials, shapes, len(partials), transfers)


RELATIVE_SLOT = (2, 1, 3)


def _near_chip(x, y):
    return [(x, 1 - y)]


def _far_chips(x, y):
    return [(1 - x, y), (1 - x, 1 - y)]


def _chips_rider(sums, relative=False):
    n = len(sums)

    def transfers(ins, outs):
        x, y, c = _position()
        return [(ins[a].at[:, RELATIVE_SLOT[k] if relative else 2 * cx + cy], outs[a].at[:, k], (cx, cy, c))
                for k, (cx, cy) in enumerate(_other_chips(x, y)) for a in range(n)], []

    shapes = [jax.ShapeDtypeStruct((q.shape[0], 3) + q.shape[2:], q.dtype) for q in sums]
    return _exchange_rider(sums, shapes, 3 * n, transfers)


def _share_rider(halves):
    def transfers(ins, outs):
        x, y, c = _position()
        return [(ins[a], outs[a], (x, y, 1 - c)) for a in range(len(halves))], []

    return _exchange_rider(halves, [jax.ShapeDtypeStruct(h.shape, h.dtype) for h in halves], len(halves), transfers)


def _small_rider(pack):
    flips = [(fx, fy, fc) for fx in (0, 1) for fy in (0, 1) for fc in (0, 1)][1:]

    def transfers(ins, outs):
        x, y, c = _position()
        slot = outs[0].at[4 * x + 2 * y + c]
        flip = lambda v, f: v + f - 2 * v * f
        return [(ins[0], slot, (flip(x, fx), flip(y, fy), flip(c, fc))) for fx, fy, fc in flips], [(ins[0], slot)]

    return _exchange_rider([pack], [jax.ShapeDtypeStruct((8,) + pack.shape, pack.dtype)], len(flips), transfers, n_local=1)


def _alone(rider, name):
    return _hosted([rider], lambda: None, name=name, in_specs=[], out_specs=[], out_shape=[])()[1][0]


def _add_pair(mine, theirs, c, tag):
    n = len(mine)

    def body(c_ref, *refs):
        for a in range(n):
            refs[2 * n + a][...] = (refs[2 * a][...] + refs[2 * a + 1][...]).astype(BF16)

    in_specs, out_specs = [], []
    for got in theirs:
        l, _, hr, cols = got.shape
        in_specs += [pl.BlockSpec((l, None, None, hr, cols), lambda j, s: (0, j, s[0], 0, 0)),
                     pl.BlockSpec((l, None, hr, cols), lambda j, s: (0, j, 0, 0))]
        out_specs.append(pl.BlockSpec((l, None, hr, cols), lambda j, s: (0, j, 0, 0)))
    return pl.pallas_call(
        body, name=f"add_pair_{tag}",
        grid_spec=pltpu.PrefetchScalarGridSpec(num_scalar_prefetch=1, grid=(N_CHIPS,), in_specs=in_specs, out_specs=out_specs),
        out_shape=[jax.ShapeDtypeStruct(got.shape, BF16) for got in theirs],
        compiler_params=_params(("parallel",)),
    )(c.reshape(1), *[a for pair in zip(mine, theirs) for a in pair])


def _add_chips(parts, received, mine, tag):
    n = len(parts)

    def body(j_ref, *refs):
        for a in range(n):
            acc = refs[2 * a][...].astype(F32)
            for k in range(3):
                acc += refs[2 * a + 1][:, k].astype(F32)
            refs[2 * n + a][...] = acc

    in_specs, out_specs, out_shape = [], [], []
    for got in received:
        l, _, hr, cols = got.shape
        in_specs += [pl.BlockSpec((l, None, hr // 2, cols), lambda i, s: (0, s[0], i, 0)),
                     pl.BlockSpec((l, 3, hr // 2, cols), lambda i, s: (0, 0, i, 0))]
        out_specs.append(pl.BlockSpec((l, hr // 2, cols), lambda i, s: (0, i, 0)))
        out_shape.append(jax.ShapeDtypeStruct((l, hr, cols), F32))
    return pl.pallas_call(
        body, name=f"add_chips_{tag}",
        grid_spec=pltpu.PrefetchScalarGridSpec(num_scalar_prefetch=1, grid=(2,), in_specs=in_specs, out_specs=out_specs),
        out_shape=out_shape,
        compiler_params=_params(("parallel",)),
    )(mine.reshape(1), *[a for pair in zip(parts, received) for a in pair])


def _adam_update(w, g, m, v):
    m2 = ADAM_B1 * m + (1.0 - ADAM_B1) * g
    v2 = ADAM_B2 * v + (1.0 - ADAM_B2) * jnp.square(g)
    m_hat = m2 / (1.0 - ADAM_B1 ** ADAM_STEP)
    v_hat = v2 / (1.0 - ADAM_B2 ** ADAM_STEP)
    return -ADAM_LR * (m_hat / (jnp.sqrt(v_hat) + ADAM_EPS) + ADAM_WD * w), m2, v2


def _adamw_group(items, tag, riders=()):
    n = len(items)

    def body(*refs):
        ins, outs = refs[:5 * n], refs[5 * n:]
        mine = pl.program_id(0) == lax.axis_index("c")
        for a in range(n):
            w_ref, own_ref, other_ref, m_ref, v_ref = ins[5 * a:5 * a + 5]
            g_ref, d_ref, nm_ref, nv_ref = outs[4 * a:4 * a + 4]
            gv = jnp.where(mine, own_ref[...], other_ref[...])
            g_ref[...] = gv
            d_ref[...], nm_ref[...], nv_ref[...] = _adam_update(w_ref[...], gv, m_ref[...], v_ref[...])

    in_specs, out_specs, out_shape, args = [], [], [], []
    for w, own, other, m, v in items:
        _, hr, cols = w.shape
        tr = hr // ADAM_BLOCKS
        full = pl.BlockSpec((None, tr, cols), lambda h, i: (h, i, 0))
        half = pl.BlockSpec((tr, cols), lambda h, i: (i, 0))
        in_specs += [full, half, half, full, full]
        out_specs += [full] * 4
        out_shape += [jax.ShapeDtypeStruct((2, hr, cols), F32)] * 4
        args += [w, own, other, m, v]
    outs, extras = _hosted(riders, body, name=f"adamw_{tag}", grid=(2, ADAM_BLOCKS), in_specs=in_specs, out_specs=out_specs,
                           out_shape=out_shape, compiler_params=_params(("arbitrary", "arbitrary")))(*args)
    return [outs[4 * a:4 * a + 4] for a in range(n)], extras


def _adamw_small(w, gathered, m, v, shapes):
    n_rows = w.shape[0]
    places = []
    for i, name in enumerate(VECTOR_PARAMS):
        places.append((name, i * TILE_ROWS, 1 if len(shapes[name]) == 1 else shapes[name][0], shapes[name][-1]))
    places.append(("pool_w", len(VECTOR_PARAMS) * TILE_ROWS, n_rows - len(VECTOR_PARAMS) * TILE_ROWS, D_MODEL))

    def body(w_ref, g_ref, m_ref, v_ref, loss_ref, *rest):
        outs, (sum_scr, d_scr, nm_scr, nv_scr) = rest[:-4], rest[-4:]
        total = g_ref[0]
        for i in range(1, g_ref.shape[0]):
            total += g_ref[i]
        sum_scr[...] = total
        gv = sum_scr[0:n_rows, :]
        d_scr[...], nm_scr[...], nv_scr[...] = _adam_update(w_ref[...], gv, m_ref[...], v_ref[...])
        loss_ref[...] = sum_scr[n_rows:n_rows + 1, 0:1]
        for k, (_, first, rows, cols) in enumerate(places):
            for j, scr in enumerate((sum_scr, d_scr, nm_scr, nv_scr)):
                outs[4 * k + j][...] = scr[first:first + rows, 0:cols]

    out_shape = [jax.ShapeDtypeStruct((1, 1), F32)]
    for _, _, rows, cols in places:
        out_shape += [jax.ShapeDtypeStruct((rows, cols), F32)] * 4
    res = pl.pallas_call(
        body, name="adamw_small", out_shape=out_shape,
        scratch_shapes=[pltpu.VMEM(gathered.shape[1:], F32)] + [pltpu.VMEM(w.shape, F32)] * 3,
        compiler_params=_params())(w, gathered, m, v)
    return res[0], {name: res[1 + 4 * k:5 + 4 * k] for k, (name, _, _, _) in enumerate(places)}


VECTOR_PARAMS = ("ffn1_norm", "mix_norm", "hgrn_lb", "hgrn_onorm", "ffn2_norm", "ple_norm", "ple_post_norm", "final_norm",
                 "pool_scale")
ALL_PARAMS = ("ffn1_norm", "ffn1_w1", "ffn1_w3", "ffn1_w2", "mix_norm", "w_in", "hgrn_lb", "hgrn_onorm", "w_branch_a",
              "pool_w", "pool_scale", "w_branch_b", "w_out", "ffn2_norm", "ffn2_w1", "ffn2_w3", "ffn2_w2", "ple_norm",
              "ple_w_gate", "ple_w_proj", "ple_post_norm", "final_norm")
TILE_ROWS = 8


def _pack_small(values, loss=None):
    tile = lambda a: jnp.pad(a, ((0, TILE_ROWS - a.shape[0]), (0, D_MODEL - a.shape[1])))
    parts = [tile(values[name].reshape(-1, values[name].shape[-1])) for name in VECTOR_PARAMS]
    parts.append(values["pool_w"].reshape(-1, D_MODEL))
    if loss is not None:
        parts.append(tile(loss))
    return jnp.concatenate(parts, axis=0)


def _halved(a, lead):
    return a.reshape(lead, 2, -1, a.shape[-1])


def _shard_halves(a, lead):
    return a.reshape(lead, N_CHIPS, 2, -1, a.shape[-1])


REDUCED_TRANSPOSED = ("ffn1_w1", "ffn1_w3", "ffn2_w1", "ffn2_w3")


def _entries(arrays):
    return [a[i] for a in arrays for i in range(a.shape[0])]


def _adam_items(names, own, other, w, m, v):
    items = []
    for name, g_own, g_other in zip(names, _entries(own), _entries(other)):
        view = (lambda a: _halved(a[0].T, 1)[0]) if name in REDUCED_TRANSPOSED else (lambda a: _halved(a, 1)[0])
        items.append((view(w[name]), g_own, g_other, view(m[name]), view(v[name])))
    return items


def _adam_store(names, results, w, out):
    for name, res in zip(names, results):
        shape = w[name].shape
        if name in REDUCED_TRANSPOSED:
            back = [a.reshape(shape[2], shape[1]).T.reshape(shape) for a in res]
        else:
            back = [a.reshape(shape) for a in res]
        out["grad"][name], out["delta"][name], out["new_m"][name], out["new_v"][name] = back


def kernel(x, p, ffn1_norm, ffn1_w1, ffn1_w3, ffn1_w2, mix_norm, w_in, hgrn_lb, hgrn_onorm, w_branch_a, pool_w, pool_scale, w_branch_b, w_out, ffn2_norm, ffn2_w1, ffn2_w3, ffn2_w2, ple_norm, ple_w_gate, ple_w_proj, ple_post_norm, final_norm, loss_target, m_ffn1_norm, m_ffn1_w1, m_ffn1_w3, m_ffn1_w2, m_mix_norm, m_w_in, m_hgrn_lb, m_hgrn_onorm, m_w_branch_a, m_pool_w, m_pool_scale, m_w_branch_b, m_w_out, m_ffn2_norm, m_ffn2_w1, m_ffn2_w3, m_ffn2_w2, m_ple_norm, m_ple_w_gate, m_ple_w_proj, m_ple_post_norm, m_final_norm, v_ffn1_norm, v_ffn1_w1, v_ffn1_w3, v_ffn1_w2, v_mix_norm, v_w_in, v_hgrn_lb, v_hgrn_onorm, v_w_branch_a, v_pool_w, v_pool_scale, v_w_branch_b, v_w_out, v_ffn2_norm, v_ffn2_w1, v_ffn2_w3, v_ffn2_w2, v_ple_norm, v_ple_w_gate, v_ple_w_proj, v_ple_post_norm, v_final_norm):
    args = dict(locals())
    w = {name: args[name] for name in ALL_PARAMS}
    m = {name: args["m_" + name] for name in ALL_PARAMS}
    v = {name: args["v_" + name] for name in ALL_PARAMS}
    cx, cy, cc = _position()
    chip = (2 * cx + cy).astype(jnp.int32)
    core = cc.astype(jnp.int32)
    xs, ps, target = x[0], p[0, 0], loss_target[0]
    t = xs.shape[0]
    tm = min(256, t)
    tm_ffn = min(512, t)
    tt = min(512, t)
    tk = min(2048, t)
    small = {name: w[name] for name in VECTOR_PARAMS}
    small["final_norm"] = w["final_norm"].reshape(1, D_MODEL)
    pool_w0 = w["pool_w"][0]

    ffn_shard = lambda i: _halved(jnp.stack([w[f"ffn{i}_w1"][0].T, w[f"ffn{i}_w3"][0].T, w[f"ffn{i}_w2"][0]]).astype(BF16), 3)
    sq_shard = _halved(jnp.stack([w["w_branch_a"][0], w["w_out"][0], w["ple_w_gate"][0]]).astype(BF16), 3)
    win_shard, wb_shard, wp_shard = (_halved(w[n].astype(BF16), 1) for n in ("w_in", "w_branch_b", "ple_w_proj"))

    ffn1_shard = ffn_shard(1)
    (near,) = _alone(_gather_from([ffn1_shard], _near_chip, 1, 1, 0.5), "gather_ffn1")
    near = _with_own(near, ffn1_shard, 0)
    half_ffn = lambda a: a.reshape(3, D_FF // 2, D_MODEL)
    (x1_near, a1_near, b1_near), ((far,),) = _ffn_fwd(xs, small["ffn1_norm"], half_ffn(near), "1_near", tm_ffn,
                                                       [_gather_from([ffn1_shard], _far_chips, 2, 0, 0.6)])
    (x1, a1_far, b1_far), ((winw,),) = _ffn_fwd(xs, small["ffn1_norm"], half_ffn(far), "1_far", tm_ffn,
                                                 [_gather_rider([win_shard], 0.6)], base=x1_near)
    ffn1w = jnp.concatenate([near, far], axis=1).reshape(3, D_FF, D_MODEL)
    winw = _with_own(winw, win_shard, chip).reshape(N_CHIPS, D_MODEL, SHARD_IN_COLS)
    (main, pool_r, gates), ((sqw, wbw, wpw),) = _mix_fwd(x1, small["mix_norm"], winw, tm,
                                                          [_gather_rider([sq_shard, wb_shard, wp_shard], 0.5)])
    sqw = _with_own(sqw, sq_shard, chip).reshape(3, D_MODEL, D_MODEL)
    wbw = _with_own(wbw, wb_shard, chip).reshape(N_CHIPS, POOL_WIDTH, -1)
    wpw = _with_own(wpw, wp_shard, chip).reshape(N_CHIPS, ps.shape[1], -1)
    ffn2_shard = ffn_shard(2)
    (o, states), ((ffn2w,),) = _hgrn_fwd(main, small["hgrn_lb"], tt, [_gather_rider([ffn2_shard], 0.7)])
    ffn2w = _with_own(ffn2w, ffn2_shard, chip).reshape(3, D_FF, D_MODEL)
    (x2, ya, yb, pooled), _ = _post_fwd(o, main, pool_r, gates, x1, small["hgrn_onorm"], pool_w0, small["pool_scale"], sqw,
                                       wbw, tm)
    (x3, a2, b2), _ = _ffn_fwd(x2, small["ffn2_norm"], ffn2w, "2", tm_ffn)
    (dx3, loss, d_wg, d_wp, d_ple, d_post, d_final), _ = _tail(
        x3, ps, target, small["ple_norm"], small["ple_post_norm"], small["final_norm"], sqw, wpw, tm_ffn)

    add_pairs = lambda parts, got, group: _add_pair(parts, got, core, group)
    add_chips = lambda sums, got, group, mine=chip: _add_chips(sums, got, mine, group)
    own_slot = jnp.zeros((), jnp.int32)
    names1 = ("ffn2_w1", "ffn2_w3", "ffn2_w2", "ple_w_gate", "ple_w_proj")
    names2 = ("w_branch_a", "w_out", "w_branch_b")
    names3 = ("w_in",)
    names4 = ("ffn1_w1", "ffn1_w3")
    names5 = ("ffn1_w2",)
    tags1, tags2, tags3, tags4, tags5 = "ffn2", "branches", "w_in", "ffn1_in", "ffn1_out"

    (dx2, dab2, s2, h3, dxh2, d_ffn2_norm), _ = _ffn_bwd(dx3, x2, small["ffn2_norm"], [a2], [b2], ffn2w, 2, tm)
    (d_w13_2,), _ = _wgrad(dab2, h3, WGRAD_IN_BLOCKS, "wgrad_ffn2_in", tk)
    (d_w2_2,), _ = _wgrad(s2, dxh2, WGRAD_OUT_BLOCKS, "wgrad_ffn2_out", tk)
    part1 = [_shard_halves(d_w13_2, 2), _shard_halves(d_w2_2, 1), _shard_halves(d_wg, 1), _shard_halves(d_wp, 1)]
    (do, dog, du, dgates, d_wa, d_wout, d_wb, d_pool_w, d_pool_scale, d_onorm), (sib1,) = _post_bwd(
        dx2, o, main, gates, ya, yb, pooled, small["hgrn_onorm"], pool_w0, small["pool_scale"], sqw, wbw, tm,
        [_pair_rider(part1)])
    sums1 = add_pairs(part1, sib1, tags1)
    part2 = [_shard_halves(d_wa, 1), _shard_halves(d_wout, 1), _shard_halves(d_wb, 1)]
    (dqfi, d_lb), (got1, sib2) = _hgrn_bwd(main, small["hgrn_lb"], states, do, tt, [_chips_rider(sums1), _pair_rider(part2)])
    own1 = add_chips(sums1, got1, tags1)
    sums2 = add_pairs(part2, sib2, tags2)
    (dx1, dproj, h2, d_mix_norm), (other1, got2) = _mix_bwd(dqfi, dog, du, dgates, dx2, x1, small["mix_norm"], winw, tm,
                                                            [_share_rider(own1), _chips_rider(sums2)])
    own2 = add_chips(sums2, got2, tags2)
    (d_win,), (other2,) = _wgrad_cols(h2, dproj, N_CHIPS, "wgrad_in", tk, [_share_rider(own2)])
    part3 = [_shard_halves(d_win, 1)]
    (dx, dab1, s1, h1, dxh1, d_ffn1_norm), _ = _ffn_bwd(dx1, xs, small["ffn1_norm"], [a1_near, a1_far], [b1_near, b1_far], ffn1w, 1, tm)
    vecs = dict(ffn1_norm=d_ffn1_norm, mix_norm=d_mix_norm, hgrn_lb=d_lb, hgrn_onorm=d_onorm, ffn2_norm=d_ffn2_norm,
                ple_norm=d_ple, ple_post_norm=d_post, final_norm=d_final, pool_scale=d_pool_scale, pool_w=d_pool_w)
    (d_w13_1,), (sib3, (small_all,)) = _wgrad(dab1, h1, WGRAD_IN_BLOCKS, "wgrad_ffn1_in", tk,
                                              [_pair_rider(part3), _small_rider(_pack_small(vecs, loss))])
    sums3 = add_pairs(part3, sib3, tags3)
    part4 = [_shard_halves(d_w13_1, 2)]
    (d_w2_1,), (got3, sib4) = _wgrad(s1, dxh1, WGRAD_OUT_BLOCKS, "wgrad_ffn1_out", tk,
                                     [_chips_rider(sums3), _pair_rider(part4)])
    own3 = add_chips(sums3, got3, tags3)
    sums4 = add_pairs(part4, sib4, tags4)
    part5 = [_shard_halves(d_w2_1, 1)]

    out = dict(grad={}, delta={}, new_m={}, new_v={})
    results, (other3, got4, sib5) = _adamw_group(
        _adam_items(names1 + names2, own1 + own2, other1 + other2, w, m, v), "early",
        [_share_rider(own3), _chips_rider(sums4, relative=True), _pair_rider(part5)])
    _adam_store(names1 + names2, results, w, out)
    own4 = add_chips(sums4, got4, tags4, own_slot)
    sums5 = add_pairs(part5, sib5, tags5)
    results, (other4, got5) = _adamw_group(_adam_items(names3, own3, other3, w, m, v), "w_in",
                                           [_share_rider(own4), _chips_rider(sums5, relative=True)])
    _adam_store(names3, results, w, out)
    own5 = add_chips(sums5, got5, tags5, own_slot)
    other5 = _alone(_share_rider(own5), "share_last")
    results, _ = _adamw_group(_adam_items(names4 + names5, own4 + own5, other4 + other5, w, m, v), "ffn1")
    _adam_store(names4 + names5, results, w, out)

    shapes = {name: w[name].shape for name in VECTOR_PARAMS + ("pool_w",)}
    loss, results = _adamw_small(_pack_small(w), small_all, _pack_small(m), _pack_small(v), shapes)
    for name, res in results.items():
        out["grad"][name], out["delta"][name], out["new_m"][name], out["new_v"][name] = (a.reshape(shapes[name]) for a in res)

    return (loss[0, 0], dx[None], *[out["grad"][n] for n in ALL_PARAMS], *[out["delta"][n] for n in ALL_PARAMS],
            *[out["new_m"][n] for n in ALL_PARAMS], *[out["new_v"][n] for n in ALL_PARAMS])
```

```python
import functools

import jax
import jax.numpy as jnp
from jax import lax
from jax.experimental import pallas as pl
from jax.experimental.pallas import tpu as pltpu

F32 = jnp.float32
BF16 = jnp.bfloat16
MESH = pl.DeviceIdType.MESH

D_MODEL = 1024
D_FF = 2816
HEADS = 8
HEAD_DIM = 128
POOL_WIDTH = 512
POOL_WINDOWS = (2, 4, 8, 16)
POOL_HALO = 16
N_CHIPS = 4
EPS = 1e-6
CHUNK = 64
MAIN_COLS = 4096
GATE_COLS = 2048
SHARD_IN_COLS = 1664

ADAM_LR = 0.001
ADAM_B1 = 0.9
ADAM_B2 = 0.999
ADAM_EPS = 1e-08
ADAM_WD = 0.01
ADAM_STEP = 10

VMEM_LIMIT = 56 * 1024 * 1024
WGRAD_IN_BLOCKS = 4
WGRAD_OUT_BLOCKS = 2
ADAM_BLOCKS = 4


def _params(semantics=None, vmem=VMEM_LIMIT):
    return pltpu.CompilerParams(dimension_semantics=semantics, vmem_limit_bytes=vmem)


def _dot(a, b):
    return jnp.dot(a, b, preferred_element_type=F32)


def _dot_nt(a, b):
    return lax.dot_general(a, b, (((1,), (1,)), ((), ())), preferred_element_type=F32)


def _dot_tn(a, b):
    return lax.dot_general(a, b, (((0,), (0,)), ((), ())), preferred_element_type=F32)


def _tri_sum(tri, x):
    hi = x.astype(BF16)
    lo = (x - hi.astype(F32)).astype(BF16)
    return _dot(tri, hi) + _dot(tri, lo)


def _sigmoid(x):
    return jax.nn.sigmoid(x)


def _resident(shape):
    zeros = (0,) * len(shape)
    return pl.BlockSpec(shape, lambda *_: zeros, pipeline_mode=pl.Buffered(1))


def _pick(shape, k):
    zeros = (0,) * (len(shape) - 1)
    return pl.BlockSpec((None,) + tuple(shape[1:]), lambda *_: (k,) + zeros, pipeline_mode=pl.Buffered(1))


def _rows(tm, cols, col_block=0):
    return pl.BlockSpec((tm, cols), lambda i: (i, col_block))


def _acc(shape):
    zeros = (0,) * len(shape)
    return pl.BlockSpec(shape, lambda *_: zeros)


def _rms(x):
    r = lax.rsqrt(jnp.mean(x * x, axis=-1, keepdims=True) + EPS)
    return r, x * r


def _rms_bwd(dn, n, r):
    return r * (dn - n * jnp.mean(dn * n, axis=-1, keepdims=True))


def _colsum(a):
    return jnp.sum(a, axis=0, keepdims=True)


ANY = pl.BlockSpec(memory_space=pl.ANY)


class _Rider:
    def __init__(self, inputs, out_shape, sems, phases):
        self.inputs, self.out_shape, self.sems, self.phases = list(inputs), list(out_shape), list(sems), list(phases)


def _hosted(riders, body, *, name, grid=(), in_specs, out_specs, out_shape, scratch_shapes=(), compiler_params=None):
    riders = [r for r in riders if r is not None]
    n_in, n_out, n_scr = len(in_specs), len(out_shape), len(scratch_shapes)
    n_steps = 1
    for g in grid:
        n_steps *= g

    def wrapped(*refs):
        pos = n_in
        ins = refs[:n_in]
        r_ins = []
        for r in riders:
            r_ins.append(refs[pos:pos + len(r.inputs)])
            pos += len(r.inputs)
        outs = refs[pos:pos + n_out]
        pos += n_out
        r_outs = []
        for r in riders:
            r_outs.append(refs[pos:pos + len(r.out_shape)])
            pos += len(r.out_shape)
        scr = refs[pos:pos + n_scr]
        pos += n_scr
        r_sems = []
        for r in riders:
            r_sems.append(refs[pos:pos + len(r.sems)])
            pos += len(r.sems)
        step = 0
        for axis in range(len(grid)):
            step = step * grid[axis] + pl.program_id(axis)

        def at_step(which, fn):
            if n_steps == 1:
                fn()
            else:
                pl.when(step == which)(fn)

        for r, ri, ro, rs in zip(riders, r_ins, r_outs, r_sems):
            for fraction, fn in r.phases:
                if fraction == 0:
                    at_step(0, functools.partial(fn, ri, ro, rs))
        body(*ins, *outs, *scr)
        for r, ri, ro, rs in zip(riders, r_ins, r_outs, r_sems):
            for fraction, fn in r.phases:
                if fraction > 0:
                    at_step(min(int(fraction * n_steps), n_steps - 1), functools.partial(fn, ri, ro, rs))

    call = pl.pallas_call(
        wrapped, name=name, grid=grid,
        in_specs=list(in_specs) + [ANY for r in riders for _ in r.inputs],
        out_specs=list(out_specs) + [ANY for r in riders for _ in r.out_shape],
        out_shape=list(out_shape) + [s for r in riders for s in r.out_shape],
        scratch_shapes=list(scratch_shapes) + [s for r in riders for s in r.sems],
        compiler_params=compiler_params)

    def run(*args):
        res = call(*args, *[a for r in riders for a in r.inputs])
        extras, pos = [], n_out
        for r in riders:
            extras.append(list(res[pos:pos + len(r.out_shape)]))
            pos += len(r.out_shape)
        return list(res[:n_out]), extras

    return run


def _ffn_fwd(x, g, ffnw, tag, tm, riders=()):
    t = x.shape[0]

    def body(x_ref, g_ref, w1_ref, w3_ref, w2_ref, xo_ref, a_ref, b_ref):
        xv = x_ref[...]
        _, n = _rms(xv)
        h = (n * g_ref[...]).astype(BF16)
        a = _dot_nt(h, w1_ref[...])
        b = _dot_nt(h, w3_ref[...])
        s = (a * _sigmoid(a) * b).astype(BF16)
        xo_ref[...] = xv + 0.5 * _dot(s, w2_ref[...])
        a_ref[...] = a.astype(BF16)
        b_ref[...] = b.astype(BF16)

    return _hosted(
        riders, body, name=f"ffn_fwd_{tag}", grid=(t // tm,),
        in_specs=[_rows(tm, D_MODEL), _resident((1, D_MODEL)), _pick(ffnw.shape, 0), _pick(ffnw.shape, 1),
                  _pick(ffnw.shape, 2)],
        out_specs=[_rows(tm, D_MODEL), _rows(tm, D_FF), _rows(tm, D_FF)],
        out_shape=[jax.ShapeDtypeStruct((t, D_MODEL), F32), jax.ShapeDtypeStruct((t, D_FF), BF16),
                   jax.ShapeDtypeStruct((t, D_FF), BF16)],
        compiler_params=_params(("arbitrary",)),
    )(x, g, ffnw, ffnw, ffnw)


def _ffn_bwd(dxo, x, g, a, b, ffnw, tag, tm, riders=()):
    t = x.shape[0]

    def body(dxo_ref, x_ref, g_ref, a_ref, b_ref, w1_ref, w3_ref, w2_ref, dx_ref, dab_ref, s_ref, h_ref, dxh_ref, dg_ref):
        @pl.when(pl.program_id(0) == 0)
        def _():
            dg_ref[...] = jnp.zeros_like(dg_ref)

        xv = x_ref[...]
        gv = g_ref[...]
        r, n = _rms(xv)
        h_ref[...] = (n * gv).astype(BF16)
        dxo_v = dxo_ref[...]
        dxh = (0.5 * dxo_v).astype(BF16)
        dxh_ref[...] = dxh
        ds = _dot_nt(dxh, w2_ref[...])
        av = a_ref[...].astype(F32)
        bv = b_ref[...].astype(F32)
        sg = _sigmoid(av)
        silu = av * sg
        s_ref[...] = (silu * bv).astype(BF16)
        da = (ds * bv * (sg * (1.0 + av * (1.0 - sg)))).astype(BF16)
        db = (ds * silu).astype(BF16)
        dab_ref[:, :D_FF] = da
        dab_ref[:, D_FF:] = db
        dh = _dot(da, w1_ref[...]) + _dot(db, w3_ref[...])
        dg_ref[...] += _colsum(dh * n)
        dx_ref[...] = dxo_v + _rms_bwd(dh * gv, n, r)

    return _hosted(
        riders, body, name=f"ffn_bwd_{tag}", grid=(t // tm,),
        in_specs=[_rows(tm, D_MODEL), _rows(tm, D_MODEL), _resident((1, D_MODEL)), _rows(tm, D_FF), _rows(tm, D_FF),
                  _pick(ffnw.shape, 0), _pick(ffnw.shape, 1), _pick(ffnw.shape, 2)],
        out_specs=[_rows(tm, D_MODEL), _rows(tm, 2 * D_FF), _rows(tm, D_FF), _rows(tm, D_MODEL), _rows(tm, D_MODEL),
                   _acc((1, D_MODEL))],
        out_shape=[jax.ShapeDtypeStruct((t, D_MODEL), F32), jax.ShapeDtypeStruct((t, 2 * D_FF), BF16),
                   jax.ShapeDtypeStruct((t, D_FF), BF16), jax.ShapeDtypeStruct((t, D_MODEL), BF16),
                   jax.ShapeDtypeStruct((t, D_MODEL), BF16), jax.ShapeDtypeStruct((1, D_MODEL), F32)],
        compiler_params=_params(("arbitrary",)),
    )(dxo, x, g, a, b, ffnw, ffnw, ffnw)


def _wgrad(xm, dy, out_blocks, name, tk, riders=()):
    t, m = xm.shape
    n = dy.shape[1]
    mb = m // out_blocks

    def body(x_ref, dy_ref, o_ref):
        @pl.when(pl.program_id(1) == 0)
        def _():
            o_ref[...] = jnp.zeros_like(o_ref)

        o_ref[...] += _dot_tn(x_ref[...], dy_ref[...])

    return _hosted(
        riders, body, name=name, grid=(out_blocks, t // tk),
        in_specs=[pl.BlockSpec((tk, mb), lambda j, k: (k, j)), pl.BlockSpec((tk, n), lambda j, k: (k, 0))],
        out_specs=[pl.BlockSpec((None, mb, n), lambda j, k: (j, 0, 0))],
        out_shape=[jax.ShapeDtypeStruct((out_blocks, mb, n), F32)],
        compiler_params=_params(("arbitrary", "arbitrary")),
    )(xm, dy)


def _wgrad_cols(xm, dy, out_blocks, name, tk, riders=()):
    t, m = xm.shape
    n = dy.shape[1]
    nb = n // out_blocks

    def body(x_ref, dy_ref, o_ref):
        @pl.when(pl.program_id(1) == 0)
        def _():
            o_ref[...] = jnp.zeros_like(o_ref)

        o_ref[...] += _dot_tn(x_ref[...], dy_ref[...])

    return _hosted(
        riders, body, name=name, grid=(out_blocks, t // tk),
        in_specs=[pl.BlockSpec((tk, m), lambda j, k: (k, 0)), pl.BlockSpec((tk, nb), lambda j, k: (k, j))],
        out_specs=[pl.BlockSpec((None, m, nb), lambda j, k: (j, 0, 0))],
        out_shape=[jax.ShapeDtypeStruct((out_blocks, m, nb), F32)],
        compiler_params=_params(("arbitrary", "arbitrary")),
    )(xm, dy)


def _mix_fwd(x1, g, winw, tm, riders=()):
    t = x1.shape[0]

    def body(x_ref, g_ref, w_ref, main_ref, pool_ref, gate_ref):
        _, n = _rms(x_ref[...])
        h = (n * g_ref[...]).astype(BF16)
        proj = jnp.concatenate([_dot(h, w_ref[j]) for j in range(N_CHIPS)], axis=1)
        main_ref[...] = proj[:, :MAIN_COLS]
        pool_ref[...] = proj[:, MAIN_COLS:MAIN_COLS + POOL_WIDTH]
        gate_ref[...] = proj[:, MAIN_COLS + POOL_WIDTH:]

    return _hosted(
        riders, body, name="mix_fwd", grid=(t // tm,),
        in_specs=[_rows(tm, D_MODEL), _resident((1, D_MODEL)), _resident(winw.shape)],
        out_specs=[_rows(tm, MAIN_COLS), _rows(tm, POOL_WIDTH), _rows(tm, GATE_COLS)],
        out_shape=[jax.ShapeDtypeStruct((t, MAIN_COLS), F32), jax.ShapeDtypeStruct((t, POOL_WIDTH), F32),
                   jax.ShapeDtypeStruct((t, GATE_COLS), F32)],
        compiler_params=_params(("arbitrary",)),
    )(x1, g, winw)


def _mix_bwd(dqfi, dog, du, dgates, dx2, x1, g, winw, tm, riders=()):
    t = x1.shape[0]
    cols = N_CHIPS * SHARD_IN_COLS

    def body(dqfi_ref, dog_ref, du_ref, dgt_ref, dx2_ref, x_ref, g_ref, w_ref, dx_ref, dproj_ref, h_ref, dg_ref):
        @pl.when(pl.program_id(0) == 0)
        def _():
            dg_ref[...] = jnp.zeros_like(dg_ref)

        dproj = jnp.concatenate([dqfi_ref[...], dog_ref[...], du_ref[...], dgt_ref[...]], axis=1)
        dproj_ref[...] = dproj
        dh = _dot_nt(dproj[:, :SHARD_IN_COLS], w_ref[0])
        for j in range(1, N_CHIPS):
            dh += _dot_nt(dproj[:, j * SHARD_IN_COLS:(j + 1) * SHARD_IN_COLS], w_ref[j])
        gv = g_ref[...]
        r, n = _rms(x_ref[...])
        h_ref[...] = (n * gv).astype(BF16)
        dg_ref[...] += _colsum(dh * n)
        dx_ref[...] = dx2_ref[...] + _rms_bwd(dh * gv, n, r)

    return _hosted(
        riders, body, name="mix_bwd", grid=(t // tm,),
        in_specs=[_rows(tm, 3 * D_MODEL), _rows(tm, D_MODEL), _rows(tm, POOL_WIDTH), _rows(tm, GATE_COLS),
                  _rows(tm, D_MODEL), _rows(tm, D_MODEL), _resident((1, D_MODEL)), _resident(winw.shape)],
        out_specs=[_rows(tm, D_MODEL), _rows(tm, cols), _rows(tm, D_MODEL), _acc((1, D_MODEL))],
        out_shape=[jax.ShapeDtypeStruct((t, D_MODEL), F32), jax.ShapeDtypeStruct((t, cols), BF16),
                   jax.ShapeDtypeStruct((t, D_MODEL), BF16), jax.ShapeDtypeStruct((1, D_MODEL), F32)],
        compiler_params=_params(("arbitrary",)),
    )(dqfi, dog, du, dgates, dx2, x1, g, winw)


def _lower_bound(lb_raw):
    l0 = lb_raw[0:1, :]
    l1 = lb_raw[1:2, :]
    m = jnp.maximum(l0, l1)
    e0 = jnp.exp(l0 - m)
    e1 = jnp.exp(l1 - m)
    return e0 / (e0 + e1)


def _head_slices():
    return [slice(h * HEAD_DIM, (h + 1) * HEAD_DIM) for h in range(HEADS)]


def _gates(qr, fr, lb, tril_b, first_half):
    sg = _sigmoid(fr)
    f = lb + (1.0 - lb) * sg
    k = 1.0 - f
    sq = _sigmoid(qr)
    q = qr * sq
    log_f = jnp.log(f)
    gc = _tri_sum(tril_b, log_f)
    gm = _colsum(jnp.where(first_half, log_f, 0.0))
    gl = _colsum(log_f)
    e_q = jnp.exp(gc - gm)
    e_k = jnp.exp(gm - gc)
    e_in = jnp.exp(gc)
    e_out = jnp.exp(gl - gc)
    return dict(sg=sg, f=f, k=k, sq=sq, q=q, e_q=e_q, e_k=e_k, e_in=e_in, e_out=e_out, e_last=jnp.exp(gl))


def _hgrn_fwd(main, lb_raw, tt, riders=()):
    t = main.shape[0]
    n_local = tt // CHUNK

    def body(q_ref, f_ref, i_ref, lb_ref, o_ref, st_ref, s_scr):
        @pl.when(pl.program_id(0) == 0)
        def _():
            s_scr[...] = jnp.zeros_like(s_scr)

        lb = _lower_bound(lb_ref[...])
        row = lax.broadcasted_iota(jnp.int32, (CHUNK, CHUNK), 0)
        col = lax.broadcasted_iota(jnp.int32, (CHUNK, CHUNK), 1)
        tril = row >= col
        tril_b = tril.astype(BF16)
        first_half = lax.broadcasted_iota(jnp.int32, (CHUNK, D_MODEL), 0) < CHUNK // 2
        heads = _head_slices()

        def chunk(c, carry):
            rows = pl.ds(pl.multiple_of(c * CHUNK, CHUNK), CHUNK)
            z = _gates(q_ref[rows, :], f_ref[rows, :], lb, tril_b, first_half)
            qt = (z["q"] * z["e_q"]).astype(BF16)
            kt = (z["k"] * z["e_k"]).astype(BF16)
            qg = (z["q"] * z["e_in"]).astype(BF16)
            kg = (z["k"] * z["e_out"]).astype(BF16)
            vb = i_ref[rows, :].astype(BF16)
            states = [s_scr[h] for h in range(HEADS)]
            for h in range(HEADS):
                st_ref[c, h] = states[h]
            raw = [_dot_nt(qt[:, sl], kt[:, sl]) for sl in heads]
            inter = [_dot_nt(qg[:, sl], states[h].astype(BF16)) for h, sl in enumerate(heads)]
            grown = [_dot_tn(vb[:, sl], kg[:, sl]) for sl in heads]
            scores = [jnp.where(tril, r, 0.0).astype(BF16) for r in raw]
            for h, sl in enumerate(heads):
                s_scr[h] = states[h] * z["e_last"][:, sl] + grown[h]
            o_ref[rows, :] = jnp.concatenate([_dot(scores[h], vb[:, sl]) + inter[h] for h, sl in enumerate(heads)], axis=1)
            return carry

        lax.fori_loop(0, n_local, chunk, 0, unroll=2)

    return _hosted(
        riders, body, name="hgrn_fwd", grid=(t // tt,),
        in_specs=[_rows(tt, D_MODEL, 0), _rows(tt, D_MODEL, 1), _rows(tt, D_MODEL, 2), _resident((2, D_MODEL))],
        out_specs=[_rows(tt, D_MODEL),
                   pl.BlockSpec((n_local, HEADS, HEAD_DIM, HEAD_DIM), lambda i: (i, 0, 0, 0))],
        out_shape=[jax.ShapeDtypeStruct((t, D_MODEL), F32),
                   jax.ShapeDtypeStruct((t // CHUNK, HEADS, HEAD_DIM, HEAD_DIM), F32)],
        scratch_shapes=[pltpu.VMEM((HEADS, HEAD_DIM, HEAD_DIM), F32)],
        compiler_params=_params(("arbitrary",)),
    )(main, main, main, lb_raw)


def _hgrn_bwd(main, lb_raw, states, do, tt, riders=()):
    t = main.shape[0]
    n_tiles = t // tt
    n_local = tt // CHUNK

    def rev(col_block):
        return pl.BlockSpec((tt, D_MODEL), lambda i: (n_tiles - 1 - i, col_block))

    def body(q_ref, f_ref, i_ref, lb_ref, st_ref, do_ref, dqfi_ref, dlb_ref, ds_scr, acc_scr):
        @pl.when(pl.program_id(0) == 0)
        def _():
            ds_scr[...] = jnp.zeros_like(ds_scr)
            acc_scr[...] = jnp.zeros_like(acc_scr)

        lb = _lower_bound(lb_ref[...])
        row = lax.broadcasted_iota(jnp.int32, (CHUNK, CHUNK), 0)
        col = lax.broadcasted_iota(jnp.int32, (CHUNK, CHUNK), 1)
        tril = row >= col
        tril_b = tril.astype(BF16)
        triu_b = (row <= col).astype(BF16)
        first_half = lax.broadcasted_iota(jnp.int32, (CHUNK, D_MODEL), 0) < CHUNK // 2
        heads = _head_slices()
        cat = functools.partial(jnp.concatenate, axis=1)

        def chunk(cc, carry):
            c = n_local - 1 - cc
            rows = pl.ds(pl.multiple_of(c * CHUNK, CHUNK), CHUNK)
            qr = q_ref[rows, :]
            z = _gates(qr, f_ref[rows, :], lb, tril_b, first_half)
            qt = (z["q"] * z["e_q"]).astype(BF16)
            kt = (z["k"] * z["e_k"]).astype(BF16)
            qg_f = z["q"] * z["e_in"]
            qg = qg_f.astype(BF16)
            kg_f = z["k"] * z["e_out"]
            kg = kg_f.astype(BF16)
            vb = i_ref[rows, :].astype(BF16)
            dob = do_ref[rows, :].astype(BF16)
            st = [st_ref[c, h] for h in range(HEADS)]
            dst = [ds_scr[h] for h in range(HEADS)]
            dst_b = [d.astype(BF16) for d in dst]
            raw = [_dot_nt(qt[:, sl], kt[:, sl]) for sl in heads]
            draw = [_dot_nt(dob[:, sl], vb[:, sl]) for sl in heads]
            dqg = [_dot(dob[:, sl], st[h].astype(BF16)) for h, sl in enumerate(heads)]
            dkg = [_dot(vb[:, sl], dst_b[h]) for h, sl in enumerate(heads)]
            dv_inter = [_dot_nt(kg[:, sl], dst_b[h]) for h, sl in enumerate(heads)]
            grown = [_dot_tn(dob[:, sl], qg[:, sl]) for sl in heads]
            scores = [jnp.where(tril, r, 0.0).astype(BF16) for r in raw]
            dscores = [jnp.where(tril, r, 0.0).astype(BF16) for r in draw]
            dqt = [_dot(dscores[h], kt[:, sl]) for h, sl in enumerate(heads)]
            dkt = [_dot_tn(dscores[h], qt[:, sl]) for h, sl in enumerate(heads)]
            dv = [_dot_tn(scores[h], dob[:, sl]) + dv_inter[h] for h, sl in enumerate(heads)]
            carry_in = cat([z["e_last"][:, sl] * _colsum(dst[h] * st[h]) for h, sl in enumerate(heads)])
            for h, sl in enumerate(heads):
                ds_scr[h] = dst[h] * z["e_last"][:, sl] + grown[h]
            dqt, dkt, dqg, dkg = cat(dqt), cat(dkt), cat(dqg), cat(dkg)
            carry_in += _colsum(dkg * kg_f)
            dq = dqt * z["e_q"] + dqg * z["e_in"]
            dk = dkt * z["e_k"] + dkg * z["e_out"]
            dgate = (qt.astype(F32) * dqt - kt.astype(F32) * dkt) + (qg_f * dqg - kg_f * dkg)
            dlogf = _tri_sum(triu_b, dgate) + carry_in
            df = dlogf / z["f"] - dk
            sg = z["sg"]
            sq = z["sq"]
            acc_scr[...] += _colsum(df * (1.0 - sg))
            dqfi_ref[rows, 0:D_MODEL] = (dq * (sq * (1.0 + qr * (1.0 - sq)))).astype(BF16)
            dqfi_ref[rows, D_MODEL:2 * D_MODEL] = (df * (1.0 - lb) * sg * (1.0 - sg)).astype(BF16)
            dqfi_ref[rows, 2 * D_MODEL:3 * D_MODEL] = cat(dv).astype(BF16)
            return carry

        lax.fori_loop(0, n_local, chunk, 0, unroll=2)
        d0 = acc_scr[...] * lb * (1.0 - lb)
        dlb_ref[0:1, :] = d0
        dlb_ref[1:2, :] = -d0

    return _hosted(
        riders, body, name="hgrn_bwd", grid=(n_tiles,),
        in_specs=[rev(0), rev(1), rev(2), _resident((2, D_MODEL)),
                  pl.BlockSpec((n_local, HEADS, HEAD_DIM, HEAD_DIM), lambda i: (n_tiles - 1 - i, 0, 0, 0)),
                  rev(0)],
        out_specs=[pl.BlockSpec((tt, 3 * D_MODEL), lambda i: (n_tiles - 1 - i, 0)), _acc((2, D_MODEL))],
        out_shape=[jax.ShapeDtypeStruct((t, 3 * D_MODEL), BF16), jax.ShapeDtypeStruct((2, D_MODEL), F32)],
        scratch_shapes=[pltpu.VMEM((HEADS, HEAD_DIM, HEAD_DIM), F32), pltpu.VMEM((1, D_MODEL), F32)],
        compiler_params=_params(("arbitrary",)),
    )(main, main, main, lb_raw, states, do)


def _head_norm(o):
    rs, ns = [], []
    for h in range(HEADS):
        oh = o[:, h * HEAD_DIM:(h + 1) * HEAD_DIM]
        r, n = _rms(oh)
        rs.append(jnp.broadcast_to(r, oh.shape))
        ns.append(n)
    return jnp.concatenate(rs, axis=1), jnp.concatenate(ns, axis=1)


def _head_norm_bwd(dn, n, r):
    outs = []
    for h in range(HEADS):
        sl = slice(h * HEAD_DIM, (h + 1) * HEAD_DIM)
        outs.append(_rms_bwd(dn[:, sl], n[:, sl], r[:, sl]))
    return jnp.concatenate(outs, axis=1)


def _window_counts(first_row, tm):
    pos = (first_row + 1 + lax.broadcasted_iota(jnp.int32, (tm, 1), 0)).astype(F32)
    return [jnp.minimum(pos, float(w)) for w in POOL_WINDOWS]


def _post_fwd(o, main, pool_r, gates, x1, onorm, pool_w, pool_scale, sqw, wbw, tm, riders=()):
    t = o.shape[0]
    ext_rows = tm + POOL_HALO

    def body(o_ref, og_ref, u_ref, gt_ref, x1_ref, on_ref, pw_ref, ps_ref, wa_ref, wout_ref, wb_ref,
             x2_ref, ya_ref, yb_ref, pooled_ref, ext):
        i = pl.program_id(0)

        @pl.when(i == 0)
        def _():
            ext[0:POOL_HALO, :] = jnp.zeros((POOL_HALO, POOL_WIDTH), F32)

        _, n = _head_norm(o_ref[...])
        og = og_ref[...]
        oa = (n * on_ref[...] * (og * _sigmoid(og))).astype(BF16)
        ya = _dot(oa, wa_ref[...])

        u = u_ref[...]
        ext[POOL_HALO:ext_rows, :] = u
        e = ext[...]
        counts = _window_counts(i * tm, tm)
        pooled = []
        for gidx, w in enumerate(POOL_WINDOWS):
            s = e[:, gidx * HEAD_DIM:(gidx + 1) * HEAD_DIM]
            shift = 1
            while shift < w:
                s = s + pltpu.roll(s, shift, axis=0)
                shift *= 2
            pooled.append(s[POOL_HALO:, :] / counts[gidx] - u[:, gidx * HEAD_DIM:(gidx + 1) * HEAD_DIM])
        ext[0:POOL_HALO, :] = ext[tm:ext_rows, :]
        pooled_b = [pg.astype(BF16) for pg in pooled]
        pooled_ref[...] = jnp.concatenate(pooled_b, axis=1)
        mixed = jnp.concatenate([_dot(pooled_b[gidx], pw_ref[gidx].astype(BF16)) for gidx in range(len(POOL_WINDOWS))],
                                axis=1) * ps_ref[...]
        mixed_b = mixed.astype(BF16)
        yb = jnp.concatenate([_dot(mixed_b, wb_ref[j]) for j in range(N_CHIPS)], axis=1)

        gt = gt_ref[...]
        y = _sigmoid(gt[:, :D_MODEL]) * ya + _sigmoid(gt[:, D_MODEL:]) * yb
        x2_ref[...] = x1_ref[...] + _dot(y.astype(BF16), wout_ref[...])
        ya_ref[...] = ya.astype(BF16)
        yb_ref[...] = yb.astype(BF16)

    return _hosted(
        riders, body, name="post_fwd", grid=(t // tm,),
        in_specs=[_rows(tm, D_MODEL), _rows(tm, D_MODEL, 3), _rows(tm, POOL_WIDTH), _rows(tm, GATE_COLS), _rows(tm, D_MODEL),
                  _resident((1, D_MODEL)), _resident(pool_w.shape), _resident((1, POOL_WIDTH)),
                  _pick(sqw.shape, 0), _pick(sqw.shape, 1), _resident(wbw.shape)],
        out_specs=[_rows(tm, D_MODEL), _rows(tm, D_MODEL), _rows(tm, D_MODEL), _rows(tm, POOL_WIDTH)],
        out_shape=[jax.ShapeDtypeStruct((t, D_MODEL), F32), jax.ShapeDtypeStruct((t, D_MODEL), BF16),
                   jax.ShapeDtypeStruct((t, D_MODEL), BF16), jax.ShapeDtypeStruct((t, POOL_WIDTH), BF16)],
        scratch_shapes=[pltpu.VMEM((ext_rows, POOL_WIDTH), F32)],
        compiler_params=_params(("arbitrary",)),
    )(o, main, pool_r, gates, x1, onorm, pool_w, pool_scale, sqw, sqw, wbw)


def _post_bwd(dx2, o, main, gates, ya, yb, pooled, onorm, pool_w, pool_scale, sqw, wbw, tm, riders=()):
    t = o.shape[0]
    n_tiles = t // tm
    ext_rows = tm + POOL_HALO
    n_groups = len(POOL_WINDOWS)

    def rev(cols, col_block=0):
        return pl.BlockSpec((tm, cols), lambda i: (n_tiles - 1 - i, col_block))

    def body(dx2_ref, o_ref, og_ref, gt_ref, ya_ref, yb_ref, pooled_ref, on_ref, pw_ref, ps_ref, wa_ref, wout_ref, wb_ref,
             do_ref, dog_ref, du_ref, dgt_ref, dwa_ref, dwout_ref, dwb_ref, dpw_ref, dps_ref, don_ref, ext):
        i = pl.program_id(0)

        @pl.when(i == 0)
        def _():
            ext[tm:ext_rows, :] = jnp.zeros((POOL_HALO, POOL_WIDTH), F32)
            for ref in (dwa_ref, dwout_ref, dwb_ref, dpw_ref, dps_ref, don_ref):
                ref[...] = jnp.zeros_like(ref)

        groups = [slice(gidx * HEAD_DIM, (gidx + 1) * HEAD_DIM) for gidx in range(n_groups)]
        shards = [slice(j * 256, (j + 1) * 256) for j in range(N_CHIPS)]
        dx2b = dx2_ref[...].astype(BF16)
        dy = _dot_nt(dx2b, wout_ref[...])
        pooled_b = pooled_ref[...]
        pm = jnp.concatenate([_dot(pooled_b[:, g], pw_ref[gidx].astype(BF16)) for gidx, g in enumerate(groups)], axis=1)
        gt = gt_ref[...]
        sga = _sigmoid(gt[:, :D_MODEL])
        sgb = _sigmoid(gt[:, D_MODEL:])
        ya = ya_ref[...].astype(F32)
        yb = yb_ref[...].astype(F32)
        y = (sga * ya + sgb * yb).astype(BF16)
        dya = (dy * sga).astype(BF16)
        dyb = (dy * sgb).astype(BF16)
        dgt_ref[:, :D_MODEL] = (dy * ya * sga * (1.0 - sga)).astype(BF16)
        dgt_ref[:, D_MODEL:] = (dy * yb * sgb * (1.0 - sgb)).astype(BF16)
        dwout_ref[...] += _dot_tn(y, dx2b)
        doa = _dot_nt(dya, wa_ref[...])
        dmixed = _dot_nt(dyb[:, shards[0]], wb_ref[0])
        for j in range(1, N_CHIPS):
            dmixed += _dot_nt(dyb[:, shards[j]], wb_ref[j])
        r, n = _head_norm(o_ref[...])
        onv = on_ref[...]
        og = og_ref[...]
        sog = _sigmoid(og)
        silu_og = og * sog
        normed = n * onv
        oa = (normed * silu_og).astype(BF16)
        dog_ref[...] = (doa * normed * (sog * (1.0 + og * (1.0 - sog)))).astype(BF16)
        dnormed = doa * silu_og
        don_ref[...] += _colsum(dnormed * n)
        do_ref[...] = _head_norm_bwd(dnormed * onv, n, r)
        psv = ps_ref[...]
        mixed_b = (pm * psv).astype(BF16)
        dps_ref[...] += _colsum(dmixed * pm)
        dpm = (dmixed * psv).astype(BF16)
        dwa_ref[...] += _dot_tn(oa, dya)
        for j in range(N_CHIPS):
            dwb_ref[j] += _dot_tn(mixed_b, dyb[:, shards[j]])
        counts = _window_counts((n_tiles - 1 - i) * tm, tm)
        dpooled = []
        for gidx, g in enumerate(groups):
            dpw_ref[gidx] += _dot_tn(pooled_b[:, g], dpm[:, g])
            dpooled.append(_dot_nt(dpm[:, g], pw_ref[gidx].astype(BF16)))
        ext[0:tm, :] = jnp.concatenate([dpooled[gidx] / counts[gidx] for gidx in range(n_groups)], axis=1)
        e = ext[...]
        du = []
        for gidx, w in enumerate(POOL_WINDOWS):
            s = e[:, gidx * HEAD_DIM:(gidx + 1) * HEAD_DIM]
            shift = 1
            while shift < w:
                s = s + pltpu.roll(s, ext_rows - shift, axis=0)
                shift *= 2
            du.append(s[:tm, :] - dpooled[gidx])
        ext[tm:ext_rows, :] = ext[0:POOL_HALO, :]
        du_ref[...] = jnp.concatenate(du, axis=1).astype(BF16)

    wa_shape = (D_MODEL, D_MODEL)
    return _hosted(
        riders, body, name="post_bwd", grid=(n_tiles,),
        in_specs=[rev(D_MODEL), rev(D_MODEL), rev(D_MODEL, 3), rev(GATE_COLS), rev(D_MODEL), rev(D_MODEL), rev(POOL_WIDTH),
                  _resident((1, D_MODEL)), _resident(pool_w.shape), _resident((1, POOL_WIDTH)),
                  _pick(sqw.shape, 0), _pick(sqw.shape, 1), _resident(wbw.shape)],
        out_specs=[rev(D_MODEL), rev(D_MODEL), rev(POOL_WIDTH), rev(GATE_COLS),
                   _acc(wa_shape), _acc(wa_shape), _acc(wbw.shape), _acc(pool_w.shape), _acc((1, POOL_WIDTH)),
                   _acc((1, D_MODEL))],
        out_shape=[jax.ShapeDtypeStruct((t, D_MODEL), F32), jax.ShapeDtypeStruct((t, D_MODEL), BF16),
                   jax.ShapeDtypeStruct((t, POOL_WIDTH), BF16), jax.ShapeDtypeStruct((t, GATE_COLS), BF16),
                   jax.ShapeDtypeStruct(wa_shape, F32), jax.ShapeDtypeStruct(wa_shape, F32),
                   jax.ShapeDtypeStruct(wbw.shape, F32), jax.ShapeDtypeStruct(pool_w.shape, F32),
                   jax.ShapeDtypeStruct((1, POOL_WIDTH), F32), jax.ShapeDtypeStruct((1, D_MODEL), F32)],
        scratch_shapes=[pltpu.VMEM((ext_rows, POOL_WIDTH), F32)],
        compiler_params=_params(("arbitrary",)),
    )(dx2, o, main, gates, ya, yb, pooled, onorm, pool_w, pool_scale, sqw, sqw, wbw)


def _tail(x3, p, target, g_ple, g_post, g_final, sqw, wpw, tm, riders=()):
    t = x3.shape[0]
    pd = p.shape[1]

    def body(x_ref, p_ref, tg_ref, g4_ref, g5_ref, g6_ref, wg_ref, wp_ref,
             dx_ref, loss_ref, dwg_ref, dwp_ref, dg4_ref, dg5_ref, dg6_ref):
        @pl.when(pl.program_id(0) == 0)
        def _():
            for ref in (loss_ref, dwg_ref, dwp_ref, dg4_ref, dg5_ref, dg6_ref):
                ref[...] = jnp.zeros_like(ref)

        x3v = x_ref[...]
        g4, g5, g6 = g4_ref[...], g5_ref[...], g6_ref[...]
        r4, n4 = _rms(x3v)
        h4 = (n4 * g4).astype(BF16)
        gate = _sigmoid(_dot(h4, wg_ref[...]))
        pb = p_ref[...].astype(BF16)
        r5, n5 = _rms(jnp.concatenate([_dot(pb, wp_ref[j]) for j in range(N_CHIPS)], axis=1))
        emb = n5 * g5
        r6, n6 = _rms(x3v + gate * emb)
        diff = n6 * g6 - tg_ref[...]
        loss_ref[...] += 0.5 * jnp.sum(jnp.mean(diff * diff, axis=-1, keepdims=True), axis=0, keepdims=True)
        dout = diff * (1.0 / D_MODEL)
        dg6_ref[...] += _colsum(dout * n6)
        dx4 = _rms_bwd(dout * g6, n6, r6)
        demb = dx4 * gate
        dg5_ref[...] += _colsum(demb * n5)
        dpre = _rms_bwd(demb * g5, n5, r5).astype(BF16)
        for j in range(N_CHIPS):
            dwp_ref[j] += _dot_tn(pb, dpre[:, j * pd:(j + 1) * pd])
        dz = (dx4 * emb * gate * (1.0 - gate)).astype(BF16)
        dwg_ref[...] += _dot_tn(h4, dz)
        dh4 = _dot_nt(dz, wg_ref[...])
        dg4_ref[...] += _colsum(dh4 * n4)
        dx_ref[...] = dx4 + _rms_bwd(dh4 * g4, n4, r4)

    sq_shape = (D_MODEL, D_MODEL)
    vec = (1, D_MODEL)
    return _hosted(
        riders, body, name="tail", grid=(t // tm,),
        in_specs=[_rows(tm, D_MODEL), _rows(tm, pd), _rows(tm, D_MODEL), _resident(vec), _resident(vec), _resident(vec),
                  _pick(sqw.shape, 2), _resident(wpw.shape)],
        out_specs=[_rows(tm, D_MODEL), _acc((1, 1)), _acc(sq_shape), _acc(wpw.shape), _acc(vec), _acc(vec), _acc(vec)],
        out_shape=[jax.ShapeDtypeStruct((t, D_MODEL), F32), jax.ShapeDtypeStruct((1, 1), F32),
                   jax.ShapeDtypeStruct(sq_shape, F32), jax.ShapeDtypeStruct(wpw.shape, F32),
                   jax.ShapeDtypeStruct(vec, F32), jax.ShapeDtypeStruct(vec, F32), jax.ShapeDtypeStruct(vec, F32)],
        compiler_params=_params(("arbitrary",)),
    )(x3, p, target, g_ple, g_post, g_final, sqw, wpw)


def _position():
    return lax.axis_index("x"), lax.axis_index("y"), lax.axis_index("c")


def _other_chips(x, y):
    return [(1 - x, y), (x, 1 - y), (1 - x, 1 - y)]


def _remote(src, dst, send_sems, recv_sems, k, device):
    return pltpu.make_async_remote_copy(src_ref=src, dst_ref=dst, send_sem=send_sems.at[k], recv_sem=recv_sems.at[k],
                                        device_id=device, device_id_type=MESH)


def _gather_rider(shards, forward_at):
    n = len(shards)

    def copies(ins, outs, sems):
        send_sems, recv_sems = sems
        x, y, c = _position()
        mine = 2 * x + y
        first, passed, arriving = [], [], []
        for k, (cx, cy) in enumerate(_other_chips(x, y)):
            theirs = 2 * cx + cy
            for a in range(n):
                first.append(_remote(ins[a].at[:, c], outs[a].at[:, mine, c], send_sems, recv_sems, k * n + a, (cx, cy, c)))
                block = outs[a].at[:, theirs, c]
                passed.append(_remote(block, block, send_sems, recv_sems, (3 + k) * n + a, (x, y, 1 - c)))
                other = outs[a].at[:, theirs, 1 - c]
                arriving.append(_remote(other, other, send_sems, recv_sems, (3 + k) * n + a, (x, y, 1 - c)))
        return first, passed, arriving

    return _Rider(shards, [jax.ShapeDtypeStruct((s.shape[0], N_CHIPS) + s.shape[1:], s.dtype) for s in shards],
                  [pltpu.SemaphoreType.DMA((6 * n,)), pltpu.SemaphoreType.DMA((6 * n,))], _gather_phases(copies, forward_at))


def _gather_phases(copies, forward_at):
    def begin(ins, outs, sems):
        for cp in copies(ins, outs, sems)[0]:
            cp.start()

    def forward(ins, outs, sems):
        first, passed, _ = copies(ins, outs, sems)
        for got, cp in zip(first, passed):
            got.wait_recv()
            cp.start()

    def finish(ins, outs, sems):
        first, passed, arriving = copies(ins, outs, sems)
        for cp in arriving:
            cp.wait_recv()
        for cp in first + passed:
            cp.wait_send()

    return [(0, begin), (forward_at, forward), (1, finish)]


def _with_own(gathered, shard, slot):
    return lax.dynamic_update_slice(gathered, shard[:, None], (0, slot, 0, 0, 0))


def _exchange_rider(arrays, out_shape, n_copies, transfers, n_local=0):
    def copies(ins, outs, sems):
        send_sems, recv_sems, local_sems = sems
        remote, local = transfers(ins, outs)
        return ([_remote(src, dst, send_sems, recv_sems, i, dev) for i, (src, dst, dev) in enumerate(remote)],
                [pltpu.make_async_copy(src, dst, local_sems.at[i]) for i, (src, dst) in enumerate(local)])

    def begin(ins, outs, sems):
        remote, local = copies(ins, outs, sems)
        for cp in remote + local:
            cp.start()

    def finish(ins, outs, sems):
        remote, local = copies(ins, outs, sems)
        for cp in remote:
            cp.wait_recv()
        for cp in remote:
            cp.wait_send()
        for cp in local:
            cp.wait()

    return _Rider(arrays, out_shape,
                  [pltpu.SemaphoreType.DMA((n_copies,)), pltpu.SemaphoreType.DMA((n_copies,)),
                   pltpu.SemaphoreType.DMA((max(n_local, 1),))],
                  [(0, begin), (1, finish)])


def _pair_rider(partials):
    def transfers(ins, outs):
        x, y, c = _position()
        return [(ins[a].at[:, :, 1 - c], outs[a], (x, y, 1 - c)) for a in range(len(partials))], []

    shapes = [jax.ShapeDtypeStruct(g.shape[:2] + g.shape[3:], g.dtype) for g in partials]
    return _exchange_rider(partials, shapes, len(partials), transfers)


def _chips_rider(sums):
    n = len(sums)

    def transfers(ins, outs):
        x, y, c = _position()
        return [(ins[a].at[:, 2 * cx + cy], outs[a].at[:, k], (cx, cy, c))
                for k, (cx, cy) in enumerate(_other_chips(x, y)) for a in range(n)], []

    shapes = [jax.ShapeDtypeStruct((q.shape[0], 3) + q.shape[2:], q.dtype) for q in sums]
    return _exchange_rider(sums, shapes, 3 * n, transfers)


def _share_rider(halves):
    def transfers(ins, outs):
        x, y, c = _position()
        return [(ins[a], outs[a], (x, y, 1 - c)) for a in range(len(halves))], []

    return _exchange_rider(halves, [jax.ShapeDtypeStruct(h.shape, h.dtype) for h in halves], len(halves), transfers)


def _small_rider(pack):
    flips = [(fx, fy, fc) for fx in (0, 1) for fy in (0, 1) for fc in (0, 1)][1:]

    def transfers(ins, outs):
        x, y, c = _position()
        slot = outs[0].at[4 * x + 2 * y + c]
        flip = lambda v, f: v + f - 2 * v * f
        return [(ins[0], slot, (flip(x, fx), flip(y, fy), flip(c, fc))) for fx, fy, fc in flips], [(ins[0], slot)]

    return _exchange_rider([pack], [jax.ShapeDtypeStruct((8,) + pack.shape, pack.dtype)], len(flips), transfers, n_local=1)


def _alone(rider, name):
    return _hosted([rider], lambda: None, name=name, in_specs=[], out_specs=[], out_shape=[])()[1][0]


def _add_pair(mine, theirs, c, tag):
    n = len(mine)

    def body(c_ref, *refs):
        for a in range(n):
            refs[2 * n + a][...] = (refs[2 * a][...] + refs[2 * a + 1][...]).astype(BF16)

    in_specs, out_specs = [], []
    for got in theirs:
        l, _, hr, cols = got.shape
        in_specs += [pl.BlockSpec((l, None, None, hr, cols), lambda j, s: (0, j, s[0], 0, 0)),
                     pl.BlockSpec((l, None, hr, cols), lambda j, s: (0, j, 0, 0))]
        out_specs.append(pl.BlockSpec((l, None, hr, cols), lambda j, s: (0, j, 0, 0)))
    return pl.pallas_call(
        body, name=f"add_pair_{tag}",
        grid_spec=pltpu.PrefetchScalarGridSpec(num_scalar_prefetch=1, grid=(N_CHIPS,), in_specs=in_specs, out_specs=out_specs),
        out_shape=[jax.ShapeDtypeStruct(got.shape, BF16) for got in theirs],
        compiler_params=_params(("parallel",)),
    )(c.reshape(1), *[a for pair in zip(mine, theirs) for a in pair])


def _add_chips(parts, received, mine, tag):
    n = len(parts)

    def body(j_ref, *refs):
        for a in range(n):
            acc = refs[2 * a][...].astype(F32)
            for k in range(3):
                acc += refs[2 * a + 1][:, k].astype(F32)
            refs[2 * n + a][...] = acc

    in_specs, out_specs, out_shape = [], [], []
    for got in received:
        l, _, hr, cols = got.shape
        in_specs += [pl.BlockSpec((l, None, hr // 2, cols), lambda i, s: (0, s[0], i, 0)),
                     pl.BlockSpec((l, 3, hr // 2, cols), lambda i, s: (0, 0, i, 0))]
        out_specs.append(pl.BlockSpec((l, hr // 2, cols), lambda i, s: (0, i, 0)))
        out_shape.append(jax.ShapeDtypeStruct((l, hr, cols), F32))
    return pl.pallas_call(
        body, name=f"add_chips_{tag}",
        grid_spec=pltpu.PrefetchScalarGridSpec(num_scalar_prefetch=1, grid=(2,), in_specs=in_specs, out_specs=out_specs),
        out_shape=out_shape,
        compiler_params=_params(("parallel",)),
    )(mine.reshape(1), *[a for pair in zip(parts, received) for a in pair])


def _adam_update(w, g, m, v):
    m2 = ADAM_B1 * m + (1.0 - ADAM_B1) * g
    v2 = ADAM_B2 * v + (1.0 - ADAM_B2) * jnp.square(g)
    m_hat = m2 / (1.0 - ADAM_B1 ** ADAM_STEP)
    v_hat = v2 / (1.0 - ADAM_B2 ** ADAM_STEP)
    return -ADAM_LR * (m_hat / (jnp.sqrt(v_hat) + ADAM_EPS) + ADAM_WD * w), m2, v2


def _adamw_group(items, tag, riders=()):
    n = len(items)

    def body(*refs):
        ins, outs = refs[:5 * n], refs[5 * n:]
        mine = pl.program_id(0) == lax.axis_index("c")
        for a in range(n):
            w_ref, own_ref, other_ref, m_ref, v_ref = ins[5 * a:5 * a + 5]
            g_ref, d_ref, nm_ref, nv_ref = outs[4 * a:4 * a + 4]
            gv = jnp.where(mine, own_ref[...], other_ref[...])
            g_ref[...] = gv
            d_ref[...], nm_ref[...], nv_ref[...] = _adam_update(w_ref[...], gv, m_ref[...], v_ref[...])

    in_specs, out_specs, out_shape, args = [], [], [], []
    for w, own, other, m, v in items:
        _, hr, cols = w.shape
        tr = hr // ADAM_BLOCKS
        full = pl.BlockSpec((None, tr, cols), lambda h, i: (h, i, 0))
        half = pl.BlockSpec((tr, cols), lambda h, i: (i, 0))
        in_specs += [full, half, half, full, full]
        out_specs += [full] * 4
        out_shape += [jax.ShapeDtypeStruct((2, hr, cols), F32)] * 4
        args += [w, own, other, m, v]
    outs, extras = _hosted(riders, body, name=f"adamw_{tag}", grid=(2, ADAM_BLOCKS), in_specs=in_specs, out_specs=out_specs,
                           out_shape=out_shape, compiler_params=_params(("arbitrary", "arbitrary")))(*args)
    return [outs[4 * a:4 * a + 4] for a in range(n)], extras


def _adamw_small(w, gathered, m, v, shapes):
    n_rows = w.shape[0]
    places = []
    for i, name in enumerate(VECTOR_PARAMS):
        places.append((name, i * TILE_ROWS, 1 if len(shapes[name]) == 1 else shapes[name][0], shapes[name][-1]))
    places.append(("pool_w", len(VECTOR_PARAMS) * TILE_ROWS, n_rows - len(VECTOR_PARAMS) * TILE_ROWS, D_MODEL))

    def body(w_ref, g_ref, m_ref, v_ref, loss_ref, *rest):
        outs, (sum_scr, d_scr, nm_scr, nv_scr) = rest[:-4], rest[-4:]
        total = g_ref[0]
        for i in range(1, g_ref.shape[0]):
            total += g_ref[i]
        sum_scr[...] = total
        gv = sum_scr[0:n_rows, :]
        d_scr[...], nm_scr[...], nv_scr[...] = _adam_update(w_ref[...], gv, m_ref[...], v_ref[...])
        loss_ref[...] = sum_scr[n_rows:n_rows + 1, 0:1]
        for k, (_, first, rows, cols) in enumerate(places):
            for j, scr in enumerate((sum_scr, d_scr, nm_scr, nv_scr)):
                outs[4 * k + j][...] = scr[first:first + rows, 0:cols]

    out_shape = [jax.ShapeDtypeStruct((1, 1), F32)]
    for _, _, rows, cols in places:
        out_shape += [jax.ShapeDtypeStruct((rows, cols), F32)] * 4
    res = pl.pallas_call(
        body, name="adamw_small", out_shape=out_shape,
        scratch_shapes=[pltpu.VMEM(gathered.shape[1:], F32)] + [pltpu.VMEM(w.shape, F32)] * 3,
        compiler_params=_params())(w, gathered, m, v)
    return res[0], {name: res[1 + 4 * k:5 + 4 * k] for k, (name, _, _, _) in enumerate(places)}


VECTOR_PARAMS = ("ffn1_norm", "mix_norm", "hgrn_lb", "hgrn_onorm", "ffn2_norm", "ple_norm", "ple_post_norm", "final_norm",
                 "pool_scale")
ALL_PARAMS = ("ffn1_norm", "ffn1_w1", "ffn1_w3", "ffn1_w2", "mix_norm", "w_in", "hgrn_lb", "hgrn_onorm", "w_branch_a",
              "pool_w", "pool_scale", "w_branch_b", "w_out", "ffn2_norm", "ffn2_w1", "ffn2_w3", "ffn2_w2", "ple_norm",
              "ple_w_gate", "ple_w_proj", "ple_post_norm", "final_norm")
TILE_ROWS = 8


def _pack_small(values, loss=None):
    tile = lambda a: jnp.pad(a, ((0, TILE_ROWS - a.shape[0]), (0, D_MODEL - a.shape[1])))
    parts = [tile(values[name].reshape(-1, values[name].shape[-1])) for name in VECTOR_PARAMS]
    parts.append(values["pool_w"].reshape(-1, D_MODEL))
    if loss is not None:
        parts.append(tile(loss))
    return jnp.concatenate(parts, axis=0)


def _halved(a, lead):
    return a.reshape(lead, 2, -1, a.shape[-1])


def _shard_halves(a, lead):
    return a.reshape(lead, N_CHIPS, 2, -1, a.shape[-1])


REDUCED_TRANSPOSED = ("ffn1_w1", "ffn1_w3", "ffn2_w1", "ffn2_w3")


def _entries(arrays):
    return [a[i] for a in arrays for i in range(a.shape[0])]


def _adam_items(names, own, other, w, m, v):
    items = []
    for name, g_own, g_other in zip(names, _entries(own), _entries(other)):
        view = (lambda a: _halved(a[0].T, 1)[0]) if name in REDUCED_TRANSPOSED else (lambda a: _halved(a, 1)[0])
        items.append((view(w[name]), g_own, g_other, view(m[name]), view(v[name])))
    return items


def _adam_store(names, results, w, out):
    for name, res in zip(names, results):
        shape = w[name].shape
        if name in REDUCED_TRANSPOSED:
            back = [a.reshape(shape[2], shape[1]).T.reshape(shape) for a in res]
        else:
            back = [a.reshape(shape) for a in res]
        out["grad"][name], out["delta"][name], out["new_m"][name], out["new_v"][name] = back


def kernel(x, p, ffn1_norm, ffn1_w1, ffn1_w3, ffn1_w2, mix_norm, w_in, hgrn_lb, hgrn_onorm, w_branch_a, pool_w, pool_scale, w_branch_b, w_out, ffn2_norm, ffn2_w1, ffn2_w3, ffn2_w2, ple_norm, ple_w_gate, ple_w_proj, ple_post_norm, final_norm, loss_target, m_ffn1_norm, m_ffn1_w1, m_ffn1_w3, m_ffn1_w2, m_mix_norm, m_w_in, m_hgrn_lb, m_hgrn_onorm, m_w_branch_a, m_pool_w, m_pool_scale, m_w_branch_b, m_w_out, m_ffn2_norm, m_ffn2_w1, m_ffn2_w3, m_ffn2_w2, m_ple_norm, m_ple_w_gate, m_ple_w_proj, m_ple_post_norm, m_final_norm, v_ffn1_norm, v_ffn1_w1, v_ffn1_w3, v_ffn1_w2, v_mix_norm, v_w_in, v_hgrn_lb, v_hgrn_onorm, v_w_branch_a, v_pool_w, v_pool_scale, v_w_branch_b, v_w_out, v_ffn2_norm, v_ffn2_w1, v_ffn2_w3, v_ffn2_w2, v_ple_norm, v_ple_w_gate, v_ple_w_proj, v_ple_post_norm, v_final_norm):
    args = dict(locals())
    w = {name: args[name] for name in ALL_PARAMS}
    m = {name: args["m_" + name] for name in ALL_PARAMS}
    v = {name: args["v_" + name] for name in ALL_PARAMS}
    cx, cy, cc = _position()
    chip = (2 * cx + cy).astype(jnp.int32)
    core = cc.astype(jnp.int32)
    xs, ps, target = x[0], p[0, 0], loss_target[0]
    t = xs.shape[0]
    tm = min(256, t)
    tm_ffn = min(512, t)
    tt = min(512, t)
    tk = min(2048, t)
    small = {name: w[name] for name in VECTOR_PARAMS}
    small["final_norm"] = w["final_norm"].reshape(1, D_MODEL)
    pool_w0 = w["pool_w"][0]

    ffn_shard = lambda i: _halved(jnp.stack([w[f"ffn{i}_w1"][0].T, w[f"ffn{i}_w3"][0].T, w[f"ffn{i}_w2"][0]]).astype(BF16), 3)
    sq_shard = _halved(jnp.stack([w["w_branch_a"][0], w["w_out"][0], w["ple_w_gate"][0]]).astype(BF16), 3)
    win_shard, wb_shard, wp_shard = (_halved(w[n].astype(BF16), 1) for n in ("w_in", "w_branch_b", "ple_w_proj"))

    ffn1_shard, ffn2_shard = ffn_shard(1), ffn_shard(2)
    (ffn1w,) = _alone(_gather_rider([ffn1_shard], 0.5), "gather_ffn1")
    ffn1w = _with_own(ffn1w, ffn1_shard, chip).reshape(3, D_FF, D_MODEL)
    (x1, a1, b1), ((winw,),) = _ffn_fwd(xs, small["ffn1_norm"], ffn1w, 1, tm_ffn, [_gather_rider([win_shard], 0.6)])
    winw = _with_own(winw, win_shard, chip).reshape(N_CHIPS, D_MODEL, SHARD_IN_COLS)
    (main, pool_r, gates), ((sqw, wbw, wpw),) = _mix_fwd(x1, small["mix_norm"], winw, tm,
                                                          [_gather_rider([sq_shard, wb_shard, wp_shard], 0.5)])
    sqw = _with_own(sqw, sq_shard, chip).reshape(3, D_MODEL, D_MODEL)
    wbw = _with_own(wbw, wb_shard, chip).reshape(N_CHIPS, POOL_WIDTH, -1)
    wpw = _with_own(wpw, wp_shard, chip).reshape(N_CHIPS, ps.shape[1], -1)
    (o, states), ((ffn2w,),) = _hgrn_fwd(main, small["hgrn_lb"], tt, [_gather_rider([ffn2_shard], 0.7)])
    ffn2w = _with_own(ffn2w, ffn2_shard, chip).reshape(3, D_FF, D_MODEL)
    (x2, ya, yb, pooled), _ = _post_fwd(o, main, pool_r, gates, x1, small["hgrn_onorm"], pool_w0, small["pool_scale"], sqw,
                                       wbw, tm)
    (x3, a2, b2), _ = _ffn_fwd(x2, small["ffn2_norm"], ffn2w, 2, tm_ffn)
    (dx3, loss, d_wg, d_wp, d_ple, d_post, d_final), _ = _tail(
        x3, ps, target, small["ple_norm"], small["ple_post_norm"], small["final_norm"], sqw, wpw, tm_ffn)

    add_pairs = lambda parts, got, group: _add_pair(parts, got, core, group)
    add_chips = lambda sums, got, group: _add_chips(sums, got, chip, group)
    names1 = ("ffn2_w1", "ffn2_w3", "ffn2_w2", "ple_w_gate", "ple_w_proj")
    names2 = ("w_branch_a", "w_out", "w_branch_b")
    names3 = ("w_in",)
    names4 = ("ffn1_w1", "ffn1_w3")
    names5 = ("ffn1_w2",)
    tags1, tags2, tags3, tags4, tags5 = "ffn2", "branches", "w_in", "ffn1_in", "ffn1_out"

    (dx2, dab2, s2, h3, dxh2, d_ffn2_norm), _ = _ffn_bwd(dx3, x2, small["ffn2_norm"], a2, b2, ffn2w, 2, tm)
    (d_w13_2,), _ = _wgrad(dab2, h3, WGRAD_IN_BLOCKS, "wgrad_ffn2_in", tk)
    (d_w2_2,), _ = _wgrad(s2, dxh2, WGRAD_OUT_BLOCKS, "wgrad_ffn2_out", tk)
    part1 = [_shard_halves(d_w13_2, 2), _shard_halves(d_w2_2, 1), _shard_halves(d_wg, 1), _shard_halves(d_wp, 1)]
    (do, dog, du, dgates, d_wa, d_wout, d_wb, d_pool_w, d_pool_scale, d_onorm), (sib1,) = _post_bwd(
        dx2, o, main, gates, ya, yb, pooled, small["hgrn_onorm"], pool_w0, small["pool_scale"], sqw, wbw, tm,
        [_pair_rider(part1)])
    sums1 = add_pairs(part1, sib1, tags1)
    part2 = [_shard_halves(d_wa, 1), _shard_halves(d_wout, 1), _shard_halves(d_wb, 1)]
    (dqfi, d_lb), (got1, sib2) = _hgrn_bwd(main, small["hgrn_lb"], states, do, tt, [_chips_rider(sums1), _pair_rider(part2)])
    own1 = add_chips(sums1, got1, tags1)
    sums2 = add_pairs(part2, sib2, tags2)
    (dx1, dproj, h2, d_mix_norm), (other1, got2) = _mix_bwd(dqfi, dog, du, dgates, dx2, x1, small["mix_norm"], winw, tm,
                                                            [_share_rider(own1), _chips_rider(sums2)])
    own2 = add_chips(sums2, got2, tags2)
    (d_win,), (other2,) = _wgrad_cols(h2, dproj, N_CHIPS, "wgrad_in", tk, [_share_rider(own2)])
    part3 = [_shard_halves(d_win, 1)]
    (dx, dab1, s1, h1, dxh1, d_ffn1_norm), _ = _ffn_bwd(dx1, xs, small["ffn1_norm"], a1, b1, ffn1w, 1, tm)
    vecs = dict(ffn1_norm=d_ffn1_norm, mix_norm=d_mix_norm, hgrn_lb=d_lb, hgrn_onorm=d_onorm, ffn2_norm=d_ffn2_norm,
                ple_norm=d_ple, ple_post_norm=d_post, final_norm=d_final, pool_scale=d_pool_scale, pool_w=d_pool_w)
    (d_w13_1,), (sib3, (small_all,)) = _wgrad(dab1, h1, WGRAD_IN_BLOCKS, "wgrad_ffn1_in", tk,
                                              [_pair_rider(part3), _small_rider(_pack_small(vecs, loss))])
    sums3 = add_pairs(part3, sib3, tags3)
    part4 = [_shard_halves(d_w13_1, 2)]
    (d_w2_1,), (got3, sib4) = _wgrad(s1, dxh1, WGRAD_OUT_BLOCKS, "wgrad_ffn1_out", tk,
                                     [_chips_rider(sums3), _pair_rider(part4)])
    own3 = add_chips(sums3, got3, tags3)
    sums4 = add_pairs(part4, sib4, tags4)
    part5 = [_shard_halves(d_w2_1, 1)]

    out = dict(grad={}, delta={}, new_m={}, new_v={})
    results, (other3, got4, sib5) = _adamw_group(
        _adam_items(names1 + names2, own1 + own2, other1 + other2, w, m, v), "early",
        [_share_rider(own3), _chips_rider(sums4), _pair_rider(part5)])
    _adam_store(names1 + names2, results, w, out)
    own4 = add_chips(sums4, got4, tags4)
    sums5 = add_pairs(part5, sib5, tags5)
    results, (other4, got5) = _adamw_group(_adam_items(names3, own3, other3, w, m, v), "w_in",
                                           [_share_rider(own4), _chips_rider(sums5)])
    _adam_store(names3, results, w, out)
    own5 = add_chips(sums5, got5, tags5)
    other5 = _alone(_share_rider(own5), "share_last")
    results, _ = _adamw_group(_adam_items(names4 + names5, own4 + own5, other4 + other5, w, m, v), "ffn1")
    _adam_store(names4 + names5, results, w, out)

    shapes = {name: w[name].shape for name in VECTOR_PARAMS + ("pool_w",)}
    loss, results = _adamw_small(_pack_small(w), small_all, _pack_small(m), _pack_small(v), shapes)
    for name, res in results.items():
        out["grad"][name], out["delta"][name], out["new_m"][name], out["new_v"][name] = (a.reshape(shapes[name]) for a in res)

    return (loss[0, 0], dx[None], *[out["grad"][n] for n in ALL_PARAMS], *[out["delta"][n] for n in ALL_PARAMS],
            *[out["new_m"][n] for n in ALL_PARAMS], *[out["new_v"][n] for n in ALL_PARAMS])
```

```python
import functools

import jax
import jax.numpy as jnp
from jax import lax
from jax.experimental import pallas as pl
from jax.experimental.pallas import tpu as pltpu

F32 = jnp.float32
BF16 = jnp.bfloat16
MESH = pl.DeviceIdType.MESH

D_MODEL = 1024
D_FF = 2816
HEADS = 8
HEAD_DIM = 128
POOL_WIDTH = 512
POOL_WINDOWS = (2, 4, 8, 16)
POOL_HALO = 16
N_CHIPS = 4
EPS = 1e-6
CHUNK = 64
MAIN_COLS = 4096
GATE_COLS = 2048
SHARD_IN_COLS = 1664

ADAM_LR = 0.001
ADAM_B1 = 0.9
ADAM_B2 = 0.999
ADAM_EPS = 1e-08
ADAM_WD = 0.01
ADAM_STEP = 10

VMEM_LIMIT = 56 * 1024 * 1024
WGRAD_IN_BLOCKS = 4
WGRAD_OUT_BLOCKS = 2
ADAM_BLOCKS = 4


def _params(semantics=None, vmem=VMEM_LIMIT):
    return pltpu.CompilerParams(dimension_semantics=semantics, vmem_limit_bytes=vmem)


def _dot(a, b):
    return jnp.dot(a, b, preferred_element_type=F32)


def _dot_nt(a, b):
    return lax.dot_general(a, b, (((1,), (1,)), ((), ())), preferred_element_type=F32)


def _dot_tn(a, b):
    return lax.dot_general(a, b, (((0,), (0,)), ((), ())), preferred_element_type=F32)


def _tri_sum(tri, x):
    hi = x.astype(BF16)
    lo = (x - hi.astype(F32)).astype(BF16)
    return _dot(tri, hi) + _dot(tri, lo)


def _sigmoid(x):
    return jax.nn.sigmoid(x)


def _resident(shape):
    zeros = (0,) * len(shape)
    return pl.BlockSpec(shape, lambda *_: zeros, pipeline_mode=pl.Buffered(1))


def _pick(shape, k):
    zeros = (0,) * (len(shape) - 1)
    return pl.BlockSpec((None,) + tuple(shape[1:]), lambda *_: (k,) + zeros, pipeline_mode=pl.Buffered(1))


def _rows(tm, cols, col_block=0):
    return pl.BlockSpec((tm, cols), lambda i: (i, col_block))


def _acc(shape):
    zeros = (0,) * len(shape)
    return pl.BlockSpec(shape, lambda *_: zeros)


def _rms(x):
    r = lax.rsqrt(jnp.mean(x * x, axis=-1, keepdims=True) + EPS)
    return r, x * r


def _rms_bwd(dn, n, r):
    return r * (dn - n * jnp.mean(dn * n, axis=-1, keepdims=True))


def _colsum(a):
    return jnp.sum(a, axis=0, keepdims=True)


ANY = pl.BlockSpec(memory_space=pl.ANY)


class _Rider:
    def __init__(self, inputs, out_shape, sems, phases):
        self.inputs, self.out_shape, self.sems, self.phases = list(inputs), list(out_shape), list(sems), list(phases)


def _hosted(riders, body, *, name, grid=(), in_specs, out_specs, out_shape, scratch_shapes=(), compiler_params=None):
    riders = [r for r in riders if r is not None]
    n_in, n_out, n_scr = len(in_specs), len(out_shape), len(scratch_shapes)
    n_steps = 1
    for g in grid:
        n_steps *= g

    def wrapped(*refs):
        pos = n_in
        ins = refs[:n_in]
        r_ins = []
        for r in riders:
            r_ins.append(refs[pos:pos + len(r.inputs)])
            pos += len(r.inputs)
        outs = refs[pos:pos + n_out]
        pos += n_out
        r_outs = []
        for r in riders:
            r_outs.append(refs[pos:pos + len(r.out_shape)])
            pos += len(r.out_shape)
        scr = refs[pos:pos + n_scr]
        pos += n_scr
        r_sems = []
        for r in riders:
            r_sems.append(refs[pos:pos + len(r.sems)])
            pos += len(r.sems)
        step = 0
        for axis in range(len(grid)):
            step = step * grid[axis] + pl.program_id(axis)

        def at_step(which, fn):
            if n_steps == 1:
                fn()
            else:
                pl.when(step == which)(fn)

        for r, ri, ro, rs in zip(riders, r_ins, r_outs, r_sems):
            for fraction, fn in r.phases:
                if fraction == 0:
                    at_step(0, functools.partial(fn, ri, ro, rs))
        body(*ins, *outs, *scr)
        for r, ri, ro, rs in zip(riders, r_ins, r_outs, r_sems):
            for fraction, fn in r.phases:
                if fraction > 0:
                    at_step(min(int(fraction * n_steps), n_steps - 1), functools.partial(fn, ri, ro, rs))

    call = pl.pallas_call(
        wrapped, name=name, grid=grid,
        in_specs=list(in_specs) + [ANY for r in riders for _ in r.inputs],
        out_specs=list(out_specs) + [ANY for r in riders for _ in r.out_shape],
        out_shape=list(out_shape) + [s for r in riders for s in r.out_shape],
        scratch_shapes=list(scratch_shapes) + [s for r in riders for s in r.sems],
        compiler_params=compiler_params)

    def run(*args):
        res = call(*args, *[a for r in riders for a in r.inputs])
        extras, pos = [], n_out
        for r in riders:
            extras.append(list(res[pos:pos + len(r.out_shape)]))
            pos += len(r.out_shape)
        return list(res[:n_out]), extras

    return run


def _ffn_fwd(x, g, ffnw, tag, tm, riders=()):
    t = x.shape[0]

    def body(x_ref, g_ref, w1_ref, w3_ref, w2_ref, xo_ref, a_ref, b_ref):
        xv = x_ref[...]
        _, n = _rms(xv)
        h = (n * g_ref[...]).astype(BF16)
        a = _dot_nt(h, w1_ref[...])
        b = _dot_nt(h, w3_ref[...])
        s = (a * _sigmoid(a) * b).astype(BF16)
        xo_ref[...] = xv + 0.5 * _dot(s, w2_ref[...])
        a_ref[...] = a.astype(BF16)
        b_ref[...] = b.astype(BF16)

    return _hosted(
        riders, body, name=f"ffn_fwd_{tag}", grid=(t // tm,),
        in_specs=[_rows(tm, D_MODEL), _resident((1, D_MODEL)), _pick(ffnw.shape, 0), _pick(ffnw.shape, 1),
                  _pick(ffnw.shape, 2)],
        out_specs=[_rows(tm, D_MODEL), _rows(tm, D_FF), _rows(tm, D_FF)],
        out_shape=[jax.ShapeDtypeStruct((t, D_MODEL), F32), jax.ShapeDtypeStruct((t, D_FF), BF16),
                   jax.ShapeDtypeStruct((t, D_FF), BF16)],
        compiler_params=_params(("arbitrary",)),
    )(x, g, ffnw, ffnw, ffnw)


def _ffn_bwd(dxo, x, g, a, b, ffnw, tag, tm, riders=()):
    t = x.shape[0]

    def body(dxo_ref, x_ref, g_ref, a_ref, b_ref, w1_ref, w3_ref, w2_ref, dx_ref, dab_ref, s_ref, h_ref, dxh_ref, dg_ref):
        @pl.when(pl.program_id(0) == 0)
        def _():
            dg_ref[...] = jnp.zeros_like(dg_ref)

        xv = x_ref[...]
        gv = g_ref[...]
        r, n = _rms(xv)
        h_ref[...] = (n * gv).astype(BF16)
        dxo_v = dxo_ref[...]
        dxh = (0.5 * dxo_v).astype(BF16)
        dxh_ref[...] = dxh
        ds = _dot_nt(dxh, w2_ref[...])
        av = a_ref[...].astype(F32)
        bv = b_ref[...].astype(F32)
        sg = _sigmoid(av)
        silu = av * sg
        s_ref[...] = (silu * bv).astype(BF16)
        da = (ds * bv * (sg * (1.0 + av * (1.0 - sg)))).astype(BF16)
        db = (ds * silu).astype(BF16)
        dab_ref[:, :D_FF] = da
        dab_ref[:, D_FF:] = db
        dh = _dot(da, w1_ref[...]) + _dot(db, w3_ref[...])
        dg_ref[...] += _colsum(dh * n)
        dx_ref[...] = dxo_v + _rms_bwd(dh * gv, n, r)

    return _hosted(
        riders, body, name=f"ffn_bwd_{tag}", grid=(t // tm,),
        in_specs=[_rows(tm, D_MODEL), _rows(tm, D_MODEL), _resident((1, D_MODEL)), _rows(tm, D_FF), _rows(tm, D_FF),
                  _pick(ffnw.shape, 0), _pick(ffnw.shape, 1), _pick(ffnw.shape, 2)],
        out_specs=[_rows(tm, D_MODEL), _rows(tm, 2 * D_FF), _rows(tm, D_FF), _rows(tm, D_MODEL), _rows(tm, D_MODEL),
                   _acc((1, D_MODEL))],
        out_shape=[jax.ShapeDtypeStruct((t, D_MODEL), F32), jax.ShapeDtypeStruct((t, 2 * D_FF), BF16),
                   jax.ShapeDtypeStruct((t, D_FF), BF16), jax.ShapeDtypeStruct((t, D_MODEL), BF16),
                   jax.ShapeDtypeStruct((t, D_MODEL), BF16), jax.ShapeDtypeStruct((1, D_MODEL), F32)],
        compiler_params=_params(("arbitrary",)),
    )(dxo, x, g, a, b, ffnw, ffnw, ffnw)


def _wgrad_body(n_token_tiles):
    def body(x_ref, dy_ref, o_ref, acc):
        k = pl.program_id(1)

        @pl.when(k == 0)
        def _():
            acc[...] = jnp.zeros_like(acc)

        acc[...] += _dot_tn(x_ref[...], dy_ref[...])

        @pl.when(k == n_token_tiles - 1)
        def _():
            o_ref[...] = acc[...].astype(BF16)

    return body


def _wgrad(xm, dy, out_blocks, name, tk, riders=()):
    t, m = xm.shape
    n = dy.shape[1]
    mb = m // out_blocks

    return _hosted(
        riders, _wgrad_body(t // tk), name=name, grid=(out_blocks, t // tk),
        in_specs=[pl.BlockSpec((tk, mb), lambda j, k: (k, j)), pl.BlockSpec((tk, n), lambda j, k: (k, 0))],
        out_specs=[pl.BlockSpec((None, mb, n), lambda j, k: (j, 0, 0))],
        out_shape=[jax.ShapeDtypeStruct((out_blocks, mb, n), BF16)],
        scratch_shapes=[pltpu.VMEM((mb, n), F32)],
        compiler_params=_params(("arbitrary", "arbitrary")),
    )(xm, dy)


def _wgrad_cols(xm, dy, out_blocks, name, tk, riders=()):
    t, m = xm.shape
    n = dy.shape[1]
    nb = n // out_blocks

    return _hosted(
        riders, _wgrad_body(t // tk), name=name, grid=(out_blocks, t // tk),
        in_specs=[pl.BlockSpec((tk, m), lambda j, k: (k, 0)), pl.BlockSpec((tk, nb), lambda j, k: (k, j))],
        out_specs=[pl.BlockSpec((None, m, nb), lambda j, k: (j, 0, 0))],
        out_shape=[jax.ShapeDtypeStruct((out_blocks, m, nb), BF16)],
        scratch_shapes=[pltpu.VMEM((m, nb), F32)],
        compiler_params=_params(("arbitrary", "arbitrary")),
    )(xm, dy)


def _mix_fwd(x1, g, winw, tm, riders=()):
    t = x1.shape[0]

    def body(x_ref, g_ref, w_ref, main_ref, pool_ref, gate_ref):
        _, n = _rms(x_ref[...])
        h = (n * g_ref[...]).astype(BF16)
        proj = jnp.concatenate([_dot(h, w_ref[j]) for j in range(N_CHIPS)], axis=1)
        main_ref[...] = proj[:, :MAIN_COLS]
        pool_ref[...] = proj[:, MAIN_COLS:MAIN_COLS + POOL_WIDTH]
        gate_ref[...] = proj[:, MAIN_COLS + POOL_WIDTH:]

    return _hosted(
        riders, body, name="mix_fwd", grid=(t // tm,),
        in_specs=[_rows(tm, D_MODEL), _resident((1, D_MODEL)), _resident(winw.shape)],
        out_specs=[_rows(tm, MAIN_COLS), _rows(tm, POOL_WIDTH), _rows(tm, GATE_COLS)],
        out_shape=[jax.ShapeDtypeStruct((t, MAIN_COLS), F32), jax.ShapeDtypeStruct((t, POOL_WIDTH), F32),
                   jax.ShapeDtypeStruct((t, GATE_COLS), F32)],
        compiler_params=_params(("arbitrary",)),
    )(x1, g, winw)


def _mix_bwd(dqfi, dog, du, dgates, dx2, x1, g, winw, tm, riders=()):
    t = x1.shape[0]
    cols = N_CHIPS * SHARD_IN_COLS

    def body(dqfi_ref, dog_ref, du_ref, dgt_ref, dx2_ref, x_ref, g_ref, w_ref, dx_ref, dproj_ref, h_ref, dg_ref):
        @pl.when(pl.program_id(0) == 0)
        def _():
            dg_ref[...] = jnp.zeros_like(dg_ref)

        dproj = jnp.concatenate([dqfi_ref[...], dog_ref[...], du_ref[...], dgt_ref[...]], axis=1)
        dproj_ref[...] = dproj
        dh = _dot_nt(dproj[:, :SHARD_IN_COLS], w_ref[0])
        for j in range(1, N_CHIPS):
            dh += _dot_nt(dproj[:, j * SHARD_IN_COLS:(j + 1) * SHARD_IN_COLS], w_ref[j])
        gv = g_ref[...]
        r, n = _rms(x_ref[...])
        h_ref[...] = (n * gv).astype(BF16)
        dg_ref[...] += _colsum(dh * n)
        dx_ref[...] = dx2_ref[...] + _rms_bwd(dh * gv, n, r)

    return _hosted(
        riders, body, name="mix_bwd", grid=(t // tm,),
        in_specs=[_rows(tm, 3 * D_MODEL), _rows(tm, D_MODEL), _rows(tm, POOL_WIDTH), _rows(tm, GATE_COLS),
                  _rows(tm, D_MODEL), _rows(tm, D_MODEL), _resident((1, D_MODEL)), _resident(winw.shape)],
        out_specs=[_rows(tm, D_MODEL), _rows(tm, cols), _rows(tm, D_MODEL), _acc((1, D_MODEL))],
        out_shape=[jax.ShapeDtypeStruct((t, D_MODEL), F32), jax.ShapeDtypeStruct((t, cols), BF16),
                   jax.ShapeDtypeStruct((t, D_MODEL), BF16), jax.ShapeDtypeStruct((1, D_MODEL), F32)],
        compiler_params=_params(("arbitrary",)),
    )(dqfi, dog, du, dgates, dx2, x1, g, winw)


def _lower_bound(lb_raw):
    l0 = lb_raw[0:1, :]
    l1 = lb_raw[1:2, :]
    m = jnp.maximum(l0, l1)
    e0 = jnp.exp(l0 - m)
    e1 = jnp.exp(l1 - m)
    return e0 / (e0 + e1)


def _head_slices():
    return [slice(h * HEAD_DIM, (h + 1) * HEAD_DIM) for h in range(HEADS)]


def _gates(qr, fr, lb, tril_b, first_half):
    sg = _sigmoid(fr)
    f = lb + (1.0 - lb) * sg
    k = 1.0 - f
    sq = _sigmoid(qr)
    q = qr * sq
    log_f = jnp.log(f)
    gc = _tri_sum(tril_b, log_f)
    gm = _colsum(jnp.where(first_half, log_f, 0.0))
    gl = _colsum(log_f)
    e_q = jnp.exp(gc - gm)
    e_k = jnp.exp(gm - gc)
    e_in = jnp.exp(gc)
    e_out = jnp.exp(gl - gc)
    return dict(sg=sg, f=f, k=k, sq=sq, q=q, e_q=e_q, e_k=e_k, e_in=e_in, e_out=e_out, e_last=jnp.exp(gl))


def _hgrn_fwd(main, lb_raw, tt, riders=()):
    t = main.shape[0]
    n_local = tt // CHUNK

    def body(q_ref, f_ref, i_ref, lb_ref, o_ref, st_ref, s_scr):
        @pl.when(pl.program_id(0) == 0)
        def _():
            s_scr[...] = jnp.zeros_like(s_scr)

        lb = _lower_bound(lb_ref[...])
        row = lax.broadcasted_iota(jnp.int32, (CHUNK, CHUNK), 0)
        col = lax.broadcasted_iota(jnp.int32, (CHUNK, CHUNK), 1)
        tril = row >= col
        tril_b = tril.astype(BF16)
        first_half = lax.broadcasted_iota(jnp.int32, (CHUNK, D_MODEL), 0) < CHUNK // 2
        heads = _head_slices()

        def chunk(c, carry):
            rows = pl.ds(pl.multiple_of(c * CHUNK, CHUNK), CHUNK)
            z = _gates(q_ref[rows, :], f_ref[rows, :], lb, tril_b, first_half)
            qt = (z["q"] * z["e_q"]).astype(BF16)
            kt = (z["k"] * z["e_k"]).astype(BF16)
            qg = (z["q"] * z["e_in"]).astype(BF16)
            kg = (z["k"] * z["e_out"]).astype(BF16)
            vb = i_ref[rows, :].astype(BF16)
            states = [s_scr[h] for h in range(HEADS)]
            for h in range(HEADS):
                st_ref[c, h] = states[h]
            raw = [_dot_nt(qt[:, sl], kt[:, sl]) for sl in heads]
            inter = [_dot_nt(qg[:, sl], states[h].astype(BF16)) for h, sl in enumerate(heads)]
            grown = [_dot_tn(vb[:, sl], kg[:, sl]) for sl in heads]
            scores = [jnp.where(tril, r, 0.0).astype(BF16) for r in raw]
            for h, sl in enumerate(heads):
                s_scr[h] = states[h] * z["e_last"][:, sl] + grown[h]
            o_ref[rows, :] = jnp.concatenate([_dot(scores[h], vb[:, sl]) + inter[h] for h, sl in enumerate(heads)], axis=1)
            return carry

        lax.fori_loop(0, n_local, chunk, 0, unroll=2)

    return _hosted(
        riders, body, name="hgrn_fwd", grid=(t // tt,),
        in_specs=[_rows(tt, D_MODEL, 0), _rows(tt, D_MODEL, 1), _rows(tt, D_MODEL, 2), _resident((2, D_MODEL))],
        out_specs=[_rows(tt, D_MODEL),
                   pl.BlockSpec((n_local, HEADS, HEAD_DIM, HEAD_DIM), lambda i: (i, 0, 0, 0))],
        out_shape=[jax.ShapeDtypeStruct((t, D_MODEL), F32),
                   jax.ShapeDtypeStruct((t // CHUNK, HEADS, HEAD_DIM, HEAD_DIM), F32)],
        scratch_shapes=[pltpu.VMEM((HEADS, HEAD_DIM, HEAD_DIM), F32)],
        compiler_params=_params(("arbitrary",)),
    )(main, main, main, lb_raw)


def _hgrn_bwd(main, lb_raw, states, do, tt, riders=()):
    t = main.shape[0]
    n_tiles = t // tt
    n_local = tt // CHUNK

    def rev(col_block):
        return pl.BlockSpec((tt, D_MODEL), lambda i: (n_tiles - 1 - i, col_block))

    def body(q_ref, f_ref, i_ref, lb_ref, st_ref, do_ref, dqfi_ref, dlb_ref, ds_scr, acc_scr):
        @pl.when(pl.program_id(0) == 0)
        def _():
            ds_scr[...] = jnp.zeros_like(ds_scr)
            acc_scr[...] = jnp.zeros_like(acc_scr)

        lb = _lower_bound(lb_ref[...])
        row = lax.broadcasted_iota(jnp.int32, (CHUNK, CHUNK), 0)
        col = lax.broadcasted_iota(jnp.int32, (CHUNK, CHUNK), 1)
        tril = row >= col
        tril_b = tril.astype(BF16)
        triu_b = (row <= col).astype(BF16)
        first_half = lax.broadcasted_iota(jnp.int32, (CHUNK, D_MODEL), 0) < CHUNK // 2
        heads = _head_slices()
        cat = functools.partial(jnp.concatenate, axis=1)

        def chunk(cc, carry):
            c = n_local - 1 - cc
            rows = pl.ds(pl.multiple_of(c * CHUNK, CHUNK), CHUNK)
            qr = q_ref[rows, :]
            z = _gates(qr, f_ref[rows, :], lb, tril_b, first_half)
            qt = (z["q"] * z["e_q"]).astype(BF16)
            kt = (z["k"] * z["e_k"]).astype(BF16)
            qg_f = z["q"] * z["e_in"]
            qg = qg_f.astype(BF16)
            kg_f = z["k"] * z["e_out"]
            kg = kg_f.astype(BF16)
            vb = i_ref[rows, :].astype(BF16)
            dob = do_ref[rows, :].astype(BF16)
            st = [st_ref[c, h] for h in range(HEADS)]
            dst = [ds_scr[h] for h in range(HEADS)]
            dst_b = [d.astype(BF16) for d in dst]
            raw = [_dot_nt(qt[:, sl], kt[:, sl]) for sl in heads]
            draw = [_dot_nt(dob[:, sl], vb[:, sl]) for sl in heads]
            dqg = [_dot(dob[:, sl], st[h].astype(BF16)) for h, sl in enumerate(heads)]
            dkg = [_dot(vb[:, sl], dst_b[h]) for h, sl in enumerate(heads)]
            dv_inter = [_dot_nt(kg[:, sl], dst_b[h]) for h, sl in enumerate(heads)]
            grown = [_dot_tn(dob[:, sl], qg[:, sl]) for sl in heads]
            scores = [jnp.where(tril, r, 0.0).astype(BF16) for r in raw]
            dscores = [jnp.where(tril, r, 0.0).astype(BF16) for r in draw]
            dqt = [_dot(dscores[h], kt[:, sl]) for h, sl in enumerate(heads)]
            dkt = [_dot_tn(dscores[h], qt[:, sl]) for h, sl in enumerate(heads)]
            dv = [_dot_tn(scores[h], dob[:, sl]) + dv_inter[h] for h, sl in enumerate(heads)]
            carry_in = cat([z["e_last"][:, sl] * _colsum(dst[h] * st[h]) for h, sl in enumerate(heads)])
            for h, sl in enumerate(heads):
                ds_scr[h] = dst[h] * z["e_last"][:, sl] + grown[h]
            dqt, dkt, dqg, dkg = cat(dqt), cat(dkt), cat(dqg), cat(dkg)
            carry_in += _colsum(dkg * kg_f)
            dq = dqt * z["e_q"] + dqg * z["e_in"]
            dk = dkt * z["e_k"] + dkg * z["e_out"]
            dgate = (qt.astype(F32) * dqt - kt.astype(F32) * dkt) + (qg_f * dqg - kg_f * dkg)
            dlogf = _tri_sum(triu_b, dgate) + carry_in
            df = dlogf / z["f"] - dk
            sg = z["sg"]
            sq = z["sq"]
            acc_scr[...] += _colsum(df * (1.0 - sg))
            dqfi_ref[rows, 0:D_MODEL] = (dq * (sq * (1.0 + qr * (1.0 - sq)))).astype(BF16)
            dqfi_ref[rows, D_MODEL:2 * D_MODEL] = (df * (1.0 - lb) * sg * (1.0 - sg)).astype(BF16)
            dqfi_ref[rows, 2 * D_MODEL:3 * D_MODEL] = cat(dv).astype(BF16)
            return carry

        lax.fori_loop(0, n_local, chunk, 0, unroll=2)
        d0 = acc_scr[...] * lb * (1.0 - lb)
        dlb_ref[0:1, :] = d0
        dlb_ref[1:2, :] = -d0

    return _hosted(
        riders, body, name="hgrn_bwd", grid=(n_tiles,),
        in_specs=[rev(0), rev(1), rev(2), _resident((2, D_MODEL)),
                  pl.BlockSpec((n_local, HEADS, HEAD_DIM, HEAD_DIM), lambda i: (n_tiles - 1 - i, 0, 0, 0)),
                  rev(0)],
        out_specs=[pl.BlockSpec((tt, 3 * D_MODEL), lambda i: (n_tiles - 1 - i, 0)), _acc((2, D_MODEL))],
        out_shape=[jax.ShapeDtypeStruct((t, 3 * D_MODEL), BF16), jax.ShapeDtypeStruct((2, D_MODEL), F32)],
        scratch_shapes=[pltpu.VMEM((HEADS, HEAD_DIM, HEAD_DIM), F32), pltpu.VMEM((1, D_MODEL), F32)],
        compiler_params=_params(("arbitrary",)),
    )(main, main, main, lb_raw, states, do)


def _head_norm(o):
    rs, ns = [], []
    for h in range(HEADS):
        oh = o[:, h * HEAD_DIM:(h + 1) * HEAD_DIM]
        r, n = _rms(oh)
        rs.append(jnp.broadcast_to(r, oh.shape))
        ns.append(n)
    return jnp.concatenate(rs, axis=1), jnp.concatenate(ns, axis=1)


def _head_norm_bwd(dn, n, r):
    outs = []
    for h in range(HEADS):
        sl = slice(h * HEAD_DIM, (h + 1) * HEAD_DIM)
        outs.append(_rms_bwd(dn[:, sl], n[:, sl], r[:, sl]))
    return jnp.concatenate(outs, axis=1)


def _window_counts(first_row, tm):
    pos = (first_row + 1 + lax.broadcasted_iota(jnp.int32, (tm, 1), 0)).astype(F32)
    return [jnp.minimum(pos, float(w)) for w in POOL_WINDOWS]


def _post_fwd(o, main, pool_r, gates, x1, onorm, pool_w, pool_scale, sqw, wbw, tm, riders=()):
    t = o.shape[0]
    ext_rows = tm + POOL_HALO

    def body(o_ref, og_ref, u_ref, gt_ref, x1_ref, on_ref, pw_ref, ps_ref, wa_ref, wout_ref, wb_ref,
             x2_ref, ya_ref, yb_ref, pooled_ref, ext):
        i = pl.program_id(0)

        @pl.when(i == 0)
        def _():
            ext[0:POOL_HALO, :] = jnp.zeros((POOL_HALO, POOL_WIDTH), F32)

        _, n = _head_norm(o_ref[...])
        og = og_ref[...]
        oa = (n * on_ref[...] * (og * _sigmoid(og))).astype(BF16)
        ya = _dot(oa, wa_ref[...])

        u = u_ref[...]
        ext[POOL_HALO:ext_rows, :] = u
        e = ext[...]
        counts = _window_counts(i * tm, tm)
        pooled = []
        for gidx, w in enumerate(POOL_WINDOWS):
            s = e[:, gidx * HEAD_DIM:(gidx + 1) * HEAD_DIM]
            shift = 1
            while shift < w:
                s = s + pltpu.roll(s, shift, axis=0)
                shift *= 2
            pooled.append(s[POOL_HALO:, :] / counts[gidx] - u[:, gidx * HEAD_DIM:(gidx + 1) * HEAD_DIM])
        ext[0:POOL_HALO, :] = ext[tm:ext_rows, :]
        pooled_b = [pg.astype(BF16) for pg in pooled]
        pooled_ref[...] = jnp.concatenate(pooled_b, axis=1)
        mixed = jnp.concatenate([_dot(pooled_b[gidx], pw_ref[gidx].astype(BF16)) for gidx in range(len(POOL_WINDOWS))],
                                axis=1) * ps_ref[...]
        mixed_b = mixed.astype(BF16)
        yb = jnp.concatenate([_dot(mixed_b, wb_ref[j]) for j in range(N_CHIPS)], axis=1)

        gt = gt_ref[...]
        y = _sigmoid(gt[:, :D_MODEL]) * ya + _sigmoid(gt[:, D_MODEL:]) * yb
        x2_ref[...] = x1_ref[...] + _dot(y.astype(BF16), wout_ref[...])
        ya_ref[...] = ya.astype(BF16)
        yb_ref[...] = yb.astype(BF16)

    return _hosted(
        riders, body, name="post_fwd", grid=(t // tm,),
        in_specs=[_rows(tm, D_MODEL), _rows(tm, D_MODEL, 3), _rows(tm, POOL_WIDTH), _rows(tm, GATE_COLS), _rows(tm, D_MODEL),
                  _resident((1, D_MODEL)), _resident(pool_w.shape), _resident((1, POOL_WIDTH)),
                  _pick(sqw.shape, 0), _pick(sqw.shape, 1), _resident(wbw.shape)],
        out_specs=[_rows(tm, D_MODEL), _rows(tm, D_MODEL), _rows(tm, D_MODEL), _rows(tm, POOL_WIDTH)],
        out_shape=[jax.ShapeDtypeStruct((t, D_MODEL), F32), jax.ShapeDtypeStruct((t, D_MODEL), BF16),
                   jax.ShapeDtypeStruct((t, D_MODEL), BF16), jax.ShapeDtypeStruct((t, POOL_WIDTH), BF16)],
        scratch_shapes=[pltpu.VMEM((ext_rows, POOL_WIDTH), F32)],
        compiler_params=_params(("arbitrary",)),
    )(o, main, pool_r, gates, x1, onorm, pool_w, pool_scale, sqw, sqw, wbw)


def _post_bwd(dx2, o, main, gates, ya, yb, pooled, onorm, pool_w, pool_scale, sqw, wbw, tm, riders=()):
    t = o.shape[0]
    n_tiles = t // tm
    ext_rows = tm + POOL_HALO
    n_groups = len(POOL_WINDOWS)

    def rev(cols, col_block=0):
        return pl.BlockSpec((tm, cols), lambda i: (n_tiles - 1 - i, col_block))

    def body(dx2_ref, o_ref, og_ref, gt_ref, ya_ref, yb_ref, pooled_ref, on_ref, pw_ref, ps_ref, wa_ref, wout_ref, wb_ref,
             do_ref, dog_ref, du_ref, dgt_ref, dwa_ref, dwout_ref, dwb_ref, dpw_ref, dps_ref, don_ref, ext):
        i = pl.program_id(0)

        @pl.when(i == 0)
        def _():
            ext[tm:ext_rows, :] = jnp.zeros((POOL_HALO, POOL_WIDTH), F32)
            for ref in (dwa_ref, dwout_ref, dwb_ref, dpw_ref, dps_ref, don_ref):
                ref[...] = jnp.zeros_like(ref)

        groups = [slice(gidx * HEAD_DIM, (gidx + 1) * HEAD_DIM) for gidx in range(n_groups)]
        shards = [slice(j * 256, (j + 1) * 256) for j in range(N_CHIPS)]
        dx2b = dx2_ref[...].astype(BF16)
        dy = _dot_nt(dx2b, wout_ref[...])
        pooled_b = pooled_ref[...]
        pm = jnp.concatenate([_dot(pooled_b[:, g], pw_ref[gidx].astype(BF16)) for gidx, g in enumerate(groups)], axis=1)
        gt = gt_ref[...]
        sga = _sigmoid(gt[:, :D_MODEL])
        sgb = _sigmoid(gt[:, D_MODEL:])
        ya = ya_ref[...].astype(F32)
        yb = yb_ref[...].astype(F32)
        y = (sga * ya + sgb * yb).astype(BF16)
        dya = (dy * sga).astype(BF16)
        dyb = (dy * sgb).astype(BF16)
        dgt_ref[:, :D_MODEL] = (dy * ya * sga * (1.0 - sga)).astype(BF16)
        dgt_ref[:, D_MODEL:] = (dy * yb * sgb * (1.0 - sgb)).astype(BF16)
        dwout_ref[...] += _dot_tn(y, dx2b)
        doa = _dot_nt(dya, wa_ref[...])
        dmixed = _dot_nt(dyb[:, shards[0]], wb_ref[0])
        for j in range(1, N_CHIPS):
            dmixed += _dot_nt(dyb[:, shards[j]], wb_ref[j])
        r, n = _head_norm(o_ref[...])
        onv = on_ref[...]
        og = og_ref[...]
        sog = _sigmoid(og)
        silu_og = og * sog
        normed = n * onv
        oa = (normed * silu_og).astype(BF16)
        dog_ref[...] = (doa * normed * (sog * (1.0 + og * (1.0 - sog)))).astype(BF16)
        dnormed = doa * silu_og
        don_ref[...] += _colsum(dnormed * n)
        do_ref[...] = _head_norm_bwd(dnormed * onv, n, r)
        psv = ps_ref[...]
        mixed_b = (pm * psv).astype(BF16)
        dps_ref[...] += _colsum(dmixed * pm)
        dpm = (dmixed * psv).astype(BF16)
        dwa_ref[...] += _dot_tn(oa, dya)
        for j in range(N_CHIPS):
            dwb_ref[j] += _dot_tn(mixed_b, dyb[:, shards[j]])
        counts = _window_counts((n_tiles - 1 - i) * tm, tm)
        dpooled = []
        for gidx, g in enumerate(groups):
            dpw_ref[gidx] += _dot_tn(pooled_b[:, g], dpm[:, g])
            dpooled.append(_dot_nt(dpm[:, g], pw_ref[gidx].astype(BF16)))
        ext[0:tm, :] = jnp.concatenate([dpooled[gidx] / counts[gidx] for gidx in range(n_groups)], axis=1)
        e = ext[...]
        du = []
        for gidx, w in enumerate(POOL_WINDOWS):
            s = e[:, gidx * HEAD_DIM:(gidx + 1) * HEAD_DIM]
            shift = 1
            while shift < w:
                s = s + pltpu.roll(s, ext_rows - shift, axis=0)
                shift *= 2
            du.append(s[:tm, :] - dpooled[gidx])
        ext[tm:ext_rows, :] = ext[0:POOL_HALO, :]
        du_ref[...] = jnp.concatenate(du, axis=1).astype(BF16)

    wa_shape = (D_MODEL, D_MODEL)
    return _hosted(
        riders, body, name="post_bwd", grid=(n_tiles,),
        in_specs=[rev(D_MODEL), rev(D_MODEL), rev(D_MODEL, 3), rev(GATE_COLS), rev(D_MODEL), rev(D_MODEL), rev(POOL_WIDTH),
                  _resident((1, D_MODEL)), _resident(pool_w.shape), _resident((1, POOL_WIDTH)),
                  _pick(sqw.shape, 0), _pick(sqw.shape, 1), _resident(wbw.shape)],
        out_specs=[rev(D_MODEL), rev(D_MODEL), rev(POOL_WIDTH), rev(GATE_COLS),
                   _acc(wa_shape), _acc(wa_shape), _acc(wbw.shape), _acc(pool_w.shape), _acc((1, POOL_WIDTH)),
                   _acc((1, D_MODEL))],
        out_shape=[jax.ShapeDtypeStruct((t, D_MODEL), F32), jax.ShapeDtypeStruct((t, D_MODEL), BF16),
                   jax.ShapeDtypeStruct((t, POOL_WIDTH), BF16), jax.ShapeDtypeStruct((t, GATE_COLS), BF16),
                   jax.ShapeDtypeStruct(wa_shape, F32), jax.ShapeDtypeStruct(wa_shape, F32),
                   jax.ShapeDtypeStruct(wbw.shape, F32), jax.ShapeDtypeStruct(pool_w.shape, F32),
                   jax.ShapeDtypeStruct((1, POOL_WIDTH), F32), jax.ShapeDtypeStruct((1, D_MODEL), F32)],
        scratch_shapes=[pltpu.VMEM((ext_rows, POOL_WIDTH), F32)],
        compiler_params=_params(("arbitrary",)),
    )(dx2, o, main, gates, ya, yb, pooled, onorm, pool_w, pool_scale, sqw, sqw, wbw)


def _tail(x3, p, target, g_ple, g_post, g_final, sqw, wpw, tm, riders=()):
    t = x3.shape[0]
    pd = p.shape[1]

    def body(x_ref, p_ref, tg_ref, g4_ref, g5_ref, g6_ref, wg_ref, wp_ref,
             dx_ref, loss_ref, dwg_ref, dwp_ref, dg4_ref, dg5_ref, dg6_ref):
        @pl.when(pl.program_id(0) == 0)
        def _():
            for ref in (loss_ref, dwg_ref, dwp_ref, dg4_ref, dg5_ref, dg6_ref):
                ref[...] = jnp.zeros_like(ref)

        x3v = x_ref[...]
        g4, g5, g6 = g4_ref[...], g5_ref[...], g6_ref[...]
        r4, n4 = _rms(x3v)
        h4 = (n4 * g4).astype(BF16)
        gate = _sigmoid(_dot(h4, wg_ref[...]))
        pb = p_ref[...].astype(BF16)
        r5, n5 = _rms(jnp.concatenate([_dot(pb, wp_ref[j]) for j in range(N_CHIPS)], axis=1))
        emb = n5 * g5
        r6, n6 = _rms(x3v + gate * emb)
        diff = n6 * g6 - tg_ref[...]
        loss_ref[...] += 0.5 * jnp.sum(jnp.mean(diff * diff, axis=-1, keepdims=True), axis=0, keepdims=True)
        dout = diff * (1.0 / D_MODEL)
        dg6_ref[...] += _colsum(dout * n6)
        dx4 = _rms_bwd(dout * g6, n6, r6)
        demb = dx4 * gate
        dg5_ref[...] += _colsum(demb * n5)
        dpre = _rms_bwd(demb * g5, n5, r5).astype(BF16)
        for j in range(N_CHIPS):
            dwp_ref[j] += _dot_tn(pb, dpre[:, j * pd:(j + 1) * pd])
        dz = (dx4 * emb * gate * (1.0 - gate)).astype(BF16)
        dwg_ref[...] += _dot_tn(h4, dz)
        dh4 = _dot_nt(dz, wg_ref[...])
        dg4_ref[...] += _colsum(dh4 * n4)
        dx_ref[...] = dx4 + _rms_bwd(dh4 * g4, n4, r4)

    sq_shape = (D_MODEL, D_MODEL)
    vec = (1, D_MODEL)
    return _hosted(
        riders, body, name="tail", grid=(t // tm,),
        in_specs=[_rows(tm, D_MODEL), _rows(tm, pd), _rows(tm, D_MODEL), _resident(vec), _resident(vec), _resident(vec),
                  _pick(sqw.shape, 2), _resident(wpw.shape)],
        out_specs=[_rows(tm, D_MODEL), _acc((1, 1)), _acc(sq_shape), _acc(wpw.shape), _acc(vec), _acc(vec), _acc(vec)],
        out_shape=[jax.ShapeDtypeStruct((t, D_MODEL), F32), jax.ShapeDtypeStruct((1, 1), F32),
                   jax.ShapeDtypeStruct(sq_shape, F32), jax.ShapeDtypeStruct(wpw.shape, F32),
                   jax.ShapeDtypeStruct(vec, F32), jax.ShapeDtypeStruct(vec, F32), jax.ShapeDtypeStruct(vec, F32)],
        compiler_params=_params(("arbitrary",)),
    )(x3, p, target, g_ple, g_post, g_final, sqw, wpw)


def _position():
    return lax.axis_index("x"), lax.axis_index("y"), lax.axis_index("c")


def _other_chips(x, y):
    return [(1 - x, y), (x, 1 - y), (1 - x, 1 - y)]


def _remote(src, dst, send_sems, recv_sems, k, device):
    return pltpu.make_async_remote_copy(src_ref=src, dst_ref=dst, send_sem=send_sems.at[k], recv_sem=recv_sems.at[k],
                                        device_id=device, device_id_type=MESH)


def _gather_rider(shards, forward_at):
    n = len(shards)

    def copies(ins, outs, sems):
        send_sems, recv_sems = sems
        x, y, c = _position()
        mine = 2 * x + y
        first, passed, arriving = [], [], []
        for k, (cx, cy) in enumerate(_other_chips(x, y)):
            theirs = 2 * cx + cy
            for a in range(n):
                first.append(_remote(ins[a].at[:, c], outs[a].at[:, mine, c], send_sems, recv_sems, k * n + a, (cx, cy, c)))
                block = outs[a].at[:, theirs, c]
                passed.append(_remote(block, block, send_sems, recv_sems, (3 + k) * n + a, (x, y, 1 - c)))
                other = outs[a].at[:, theirs, 1 - c]
                arriving.append(_remote(other, other, send_sems, recv_sems, (3 + k) * n + a, (x, y, 1 - c)))
        return first, passed, arriving

    return _Rider(shards, [jax.ShapeDtypeStruct((s.shape[0], N_CHIPS) + s.shape[1:], s.dtype) for s in shards],
                  [pltpu.SemaphoreType.DMA((6 * n,)), pltpu.SemaphoreType.DMA((6 * n,))], _gather_phases(copies, forward_at))


def _gather_phases(copies, forward_at):
    def begin(ins, outs, sems):
        for cp in copies(ins, outs, sems)[0]:
            cp.start()

    def forward(ins, outs, sems):
        first, passed, _ = copies(ins, outs, sems)
        for got, cp in zip(first, passed):
            got.wait_recv()
            cp.start()

    def finish(ins, outs, sems):
        first, passed, arriving = copies(ins, outs, sems)
        for cp in arriving:
            cp.wait_recv()
        for cp in first + passed:
            cp.wait_send()

    return [(0, begin), (forward_at, forward), (1, finish)]


def _with_own(gathered, shard, slot):
    return lax.dynamic_update_slice(gathered, shard[:, None], (0, slot, 0, 0, 0))


def _exchange_rider(arrays, out_shape, n_copies, transfers, n_local=0):
    def copies(ins, outs, sems):
        send_sems, recv_sems, local_sems = sems
        remote, local = transfers(ins, outs)
        return ([_remote(src, dst, send_sems, recv_sems, i, dev) for i, (src, dst, dev) in enumerate(remote)],
                [pltpu.make_async_copy(src, dst, local_sems.at[i]) for i, (src, dst) in enumerate(local)])

    def begin(ins, outs, sems):
        remote, local = copies(ins, outs, sems)
        for cp in remote + local:
            cp.start()

    def finish(ins, outs, sems):
        remote, local = copies(ins, outs, sems)
        for cp in remote:
            cp.wait_recv()
        for cp in remote:
            cp.wait_send()
        for cp in local:
            cp.wait()

    return _Rider(arrays, out_shape,
                  [pltpu.SemaphoreType.DMA((n_copies,)), pltpu.SemaphoreType.DMA((n_copies,)),
                   pltpu.SemaphoreType.DMA((max(n_local, 1),))],
                  [(0, begin), (1, finish)])


def _pair_rider(partials):
    def transfers(ins, outs):
        x, y, c = _position()
        return [(ins[a].at[:, :, 1 - c], outs[a], (x, y, 1 - c)) for a in range(len(partials))], []

    shapes = [jax.ShapeDtypeStruct(g.shape[:2] + g.shape[3:], g.dtype) for g in partials]
    return _exchange_rider(partials, shapes, len(partials), transfers)


def _chips_rider(sums):
    n = len(sums)

    def transfers(ins, outs):
        x, y, c = _position()
        return [(ins[a].at[:, 2 * cx + cy], outs[a].at[:, k], (cx, cy, c))
                for k, (cx, cy) in enumerate(_other_chips(x, y)) for a in range(n)], []

    shapes = [jax.ShapeDtypeStruct((q.shape[0], 3) + q.shape[2:], q.dtype) for q in sums]
    return _exchange_rider(sums, shapes, 3 * n, transfers)


def _share_rider(halves):
    def transfers(ins, outs):
        x, y, c = _position()
        return [(ins[a], outs[a], (x, y, 1 - c)) for a in range(len(halves))], []

    return _exchange_rider(halves, [jax.ShapeDtypeStruct(h.shape, h.dtype) for h in halves], len(halves), transfers)


def _small_rider(pack):
    flips = [(fx, fy, fc) for fx in (0, 1) for fy in (0, 1) for fc in (0, 1)][1:]

    def transfers(ins, outs):
        x, y, c = _position()
        slot = outs[0].at[4 * x + 2 * y + c]
        flip = lambda v, f: v + f - 2 * v * f
        return [(ins[0], slot, (flip(x, fx), flip(y, fy), flip(c, fc))) for fx, fy, fc in flips], [(ins[0], slot)]

    return _exchange_rider([pack], [jax.ShapeDtypeStruct((8,) + pack.shape, pack.dtype)], len(flips), transfers, n_local=1)


def _alone(rider, name):
    return _hosted([rider], lambda: None, name=name, in_specs=[], out_specs=[], out_shape=[])()[1][0]


def _add_pair(mine, theirs, c, tag):
    n = len(mine)

    def body(c_ref, *refs):
        for a in range(n):
            refs[2 * n + a][...] = (refs[2 * a][...].astype(F32) + refs[2 * a + 1][...].astype(F32)).astype(BF16)

    in_specs, out_specs = [], []
    for got in theirs:
        l, _, hr, cols = got.shape
        in_specs += [pl.BlockSpec((l, None, None, hr, cols), lambda j, s: (0, j, s[0], 0, 0)),
                     pl.BlockSpec((l, None, hr, cols), lambda j, s: (0, j, 0, 0))]
        out_specs.append(pl.BlockSpec((l, None, hr, cols), lambda j, s: (0, j, 0, 0)))
    return pl.pallas_call(
        body, name=f"add_pair_{tag}",
        grid_spec=pltpu.PrefetchScalarGridSpec(num_scalar_prefetch=1, grid=(N_CHIPS,), in_specs=in_specs, out_specs=out_specs),
        out_shape=[jax.ShapeDtypeStruct(got.shape, BF16) for got in theirs],
        compiler_params=_params(("parallel",)),
    )(c.reshape(1), *[a for pair in zip(mine, theirs) for a in pair])


def _add_chips(parts, received, mine, tag):
    n = len(parts)

    def body(j_ref, *refs):
        for a in range(n):
            acc = refs[2 * a][...].astype(F32)
            for k in range(3):
                acc += refs[2 * a + 1][:, k].astype(F32)
            refs[2 * n + a][...] = acc

    in_specs, out_specs, out_shape = [], [], []
    for got in received:
        l, _, hr, cols = got.shape
        in_specs += [pl.BlockSpec((l, None, hr // 2, cols), lambda i, s: (0, s[0], i, 0)),
                     pl.BlockSpec((l, 3, hr // 2, cols), lambda i, s: (0, 0, i, 0))]
        out_specs.append(pl.BlockSpec((l, hr // 2, cols), lambda i, s: (0, i, 0)))
        out_shape.append(jax.ShapeDtypeStruct((l, hr, cols), F32))
    return pl.pallas_call(
        body, name=f"add_chips_{tag}",
        grid_spec=pltpu.PrefetchScalarGridSpec(num_scalar_prefetch=1, grid=(2,), in_specs=in_specs, out_specs=out_specs),
        out_shape=out_shape,
        compiler_params=_params(("parallel",)),
    )(mine.reshape(1), *[a for pair in zip(parts, received) for a in pair])


def _adam_update(w, g, m, v):
    m2 = ADAM_B1 * m + (1.0 - ADAM_B1) * g
    v2 = ADAM_B2 * v + (1.0 - ADAM_B2) * jnp.square(g)
    m_hat = m2 / (1.0 - ADAM_B1 ** ADAM_STEP)
    v_hat = v2 / (1.0 - ADAM_B2 ** ADAM_STEP)
    return -ADAM_LR * (m_hat / (jnp.sqrt(v_hat) + ADAM_EPS) + ADAM_WD * w), m2, v2


def _adamw_group(items, tag, riders=()):
    n = len(items)

    def body(*refs):
        ins, outs = refs[:5 * n], refs[5 * n:]
        mine = pl.program_id(0) == lax.axis_index("c")
        for a in range(n):
            w_ref, own_ref, other_ref, m_ref, v_ref = ins[5 * a:5 * a + 5]
            g_ref, d_ref, nm_ref, nv_ref = outs[4 * a:4 * a + 4]
            gv = jnp.where(mine, own_ref[...], other_ref[...])
            g_ref[...] = gv
            d_ref[...], nm_ref[...], nv_ref[...] = _adam_update(w_ref[...], gv, m_ref[...], v_ref[...])

    in_specs, out_specs, out_shape, args = [], [], [], []
    for w, own, other, m, v in items:
        _, hr, cols = w.shape
        tr = hr // ADAM_BLOCKS
        full = pl.BlockSpec((None, tr, cols), lambda h, i: (h, i, 0))
        half = pl.BlockSpec((tr, cols), lambda h, i: (i, 0))
        in_specs += [full, half, half, full, full]
        out_specs += [full] * 4
        out_shape += [jax.ShapeDtypeStruct((2, hr, cols), F32)] * 4
        args += [w, own, other, m, v]
    outs, extras = _hosted(riders, body, name=f"adamw_{tag}", grid=(2, ADAM_BLOCKS), in_specs=in_specs, out_specs=out_specs,
                           out_shape=out_shape, compiler_params=_params(("arbitrary", "arbitrary")))(*args)
    return [outs[4 * a:4 * a + 4] for a in range(n)], extras


def _adamw_small(w, gathered, m, v, shapes):
    n_rows = w.shape[0]
    places = []
    for i, name in enumerate(VECTOR_PARAMS):
        places.append((name, i * TILE_ROWS, 1 if len(shapes[name]) == 1 else shapes[name][0], shapes[name][-1]))
    places.append(("pool_w", len(VECTOR_PARAMS) * TILE_ROWS, n_rows - len(VECTOR_PARAMS) * TILE_ROWS, D_MODEL))

    def body(w_ref, g_ref, m_ref, v_ref, loss_ref, *rest):
        outs, (sum_scr, d_scr, nm_scr, nv_scr) = rest[:-4], rest[-4:]
        total = g_ref[0]
        for i in range(1, g_ref.shape[0]):
            total += g_ref[i]
        sum_scr[...] = total
        gv = sum_scr[0:n_rows, :]
        d_scr[...], nm_scr[...], nv_scr[...] = _adam_update(w_ref[...], gv, m_ref[...], v_ref[...])
        loss_ref[...] = sum_scr[n_rows:n_rows + 1, 0:1]
        for k, (_, first, rows, cols) in enumerate(places):
            for j, scr in enumerate((sum_scr, d_scr, nm_scr, nv_scr)):
                outs[4 * k + j][...] = scr[first:first + rows, 0:cols]

    out_shape = [jax.ShapeDtypeStruct((1, 1), F32)]
    for _, _, rows, cols in places:
        out_shape += [jax.ShapeDtypeStruct((rows, cols), F32)] * 4
    res = pl.pallas_call(
        body, name="adamw_small", out_shape=out_shape,
        scratch_shapes=[pltpu.VMEM(gathered.shape[1:], F32)] + [pltpu.VMEM(w.shape, F32)] * 3,
        compiler_params=_params())(w, gathered, m, v)
    return res[0], {name: res[1 + 4 * k:5 + 4 * k] for k, (name, _, _, _) in enumerate(places)}


VECTOR_PARAMS = ("ffn1_norm", "mix_norm", "hgrn_lb", "hgrn_onorm", "ffn2_norm", "ple_norm", "ple_post_norm", "final_norm",
                 "pool_scale")
ALL_PARAMS = ("ffn1_norm", "ffn1_w1", "ffn1_w3", "ffn1_w2", "mix_norm", "w_in", "hgrn_lb", "hgrn_onorm", "w_branch_a",
              "pool_w", "pool_scale", "w_branch_b", "w_out", "ffn2_norm", "ffn2_w1", "ffn2_w3", "ffn2_w2", "ple_norm",
              "ple_w_gate", "ple_w_proj", "ple_post_norm", "final_norm")
TILE_ROWS = 8


def _pack_small(values, loss=None):
    tile = lambda a: jnp.pad(a, ((0, TILE_ROWS - a.shape[0]), (0, D_MODEL - a.shape[1])))
    parts = [tile(values[name].reshape(-1, values[name].shape[-1])) for name in VECTOR_PARAMS]
    parts.append(values["pool_w"].reshape(-1, D_MODEL))
    if loss is not None:
        parts.append(tile(loss))
    return jnp.concatenate(parts, axis=0)


def _halved(a, lead):
    return a.reshape(lead, 2, -1, a.shape[-1])


def _shard_halves(a, lead):
    return a.reshape(lead, N_CHIPS, 2, -1, a.shape[-1])


REDUCED_TRANSPOSED = ("ffn1_w1", "ffn1_w3", "ffn2_w1", "ffn2_w3")


def _entries(arrays):
    return [a[i] for a in arrays for i in range(a.shape[0])]


def _adam_items(names, own, other, w, m, v):
    items = []
    for name, g_own, g_other in zip(names, _entries(own), _entries(other)):
        view = (lambda a: _halved(a[0].T, 1)[0]) if name in REDUCED_TRANSPOSED else (lambda a: _halved(a, 1)[0])
        items.append((view(w[name]), g_own, g_other, view(m[name]), view(v[name])))
    return items


def _adam_store(names, results, w, out):
    for name, res in zip(names, results):
        shape = w[name].shape
        if name in REDUCED_TRANSPOSED:
            back = [a.reshape(shape[2], shape[1]).T.reshape(shape) for a in res]
        else:
            back = [a.reshape(shape) for a in res]
        out["grad"][name], out["delta"][name], out["new_m"][name], out["new_v"][name] = back


def kernel(x, p, ffn1_norm, ffn1_w1, ffn1_w3, ffn1_w2, mix_norm, w_in, hgrn_lb, hgrn_onorm, w_branch_a, pool_w, pool_scale, w_branch_b, w_out, ffn2_norm, ffn2_w1, ffn2_w3, ffn2_w2, ple_norm, ple_w_gate, ple_w_proj, ple_post_norm, final_norm, loss_target, m_ffn1_norm, m_ffn1_w1, m_ffn1_w3, m_ffn1_w2, m_mix_norm, m_w_in, m_hgrn_lb, m_hgrn_onorm, m_w_branch_a, m_pool_w, m_pool_scale, m_w_branch_b, m_w_out, m_ffn2_norm, m_ffn2_w1, m_ffn2_w3, m_ffn2_w2, m_ple_norm, m_ple_w_gate, m_ple_w_proj, m_ple_post_norm, m_final_norm, v_ffn1_norm, v_ffn1_w1, v_ffn1_w3, v_ffn1_w2, v_mix_norm, v_w_in, v_hgrn_lb, v_hgrn_onorm, v_w_branch_a, v_pool_w, v_pool_scale, v_w_branch_b, v_w_out, v_ffn2_norm, v_ffn2_w1, v_ffn2_w3, v_ffn2_w2, v_ple_norm, v_ple_w_gate, v_ple_w_proj, v_ple_post_norm, v_final_norm):
    args = dict(locals())
    w = {name: args[name] for name in ALL_PARAMS}
    m = {name: args["m_" + name] for name in ALL_PARAMS}
    v = {name: args["v_" + name] for name in ALL_PARAMS}
    cx, cy, cc = _position()
    chip = (2 * cx + cy).astype(jnp.int32)
    core = cc.astype(jnp.int32)
    xs, ps, target = x[0], p[0, 0], loss_target[0]
    t = xs.shape[0]
    tm = min(256, t)
    tm_ffn = min(512, t)
    tt = min(512, t)
    tk = min(2048, t)
    small = {name: w[name] for name in VECTOR_PARAMS}
    small["final_norm"] = w["final_norm"].reshape(1, D_MODEL)
    pool_w0 = w["pool_w"][0]

    ffn_shard = lambda i: _halved(jnp.stack([w[f"ffn{i}_w1"][0].T, w[f"ffn{i}_w3"][0].T, w[f"ffn{i}_w2"][0]]).astype(BF16), 3)
    sq_shard = _halved(jnp.stack([w["w_branch_a"][0], w["w_out"][0], w["ple_w_gate"][0]]).astype(BF16), 3)
    win_shard, wb_shard, wp_shard = (_halved(w[n].astype(BF16), 1) for n in ("w_in", "w_branch_b", "ple_w_proj"))

    ffn1_shard, ffn2_shard = ffn_shard(1), ffn_shard(2)
    (ffn1w,) = _alone(_gather_rider([ffn1_shard], 0.5), "gather_ffn1")
    ffn1w = _with_own(ffn1w, ffn1_shard, chip).reshape(3, D_FF, D_MODEL)
    (x1, a1, b1), ((winw,),) = _ffn_fwd(xs, small["ffn1_norm"], ffn1w, 1, tm_ffn, [_gather_rider([win_shard], 0.6)])
    winw = _with_own(winw, win_shard, chip).reshape(N_CHIPS, D_MODEL, SHARD_IN_COLS)
    (main, pool_r, gates), ((sqw, wbw, wpw),) = _mix_fwd(x1, small["mix_norm"], winw, tm,
                                                          [_gather_rider([sq_shard, wb_shard, wp_shard], 0.5)])
    sqw = _with_own(sqw, sq_shard, chip).reshape(3, D_MODEL, D_MODEL)
    wbw = _with_own(wbw, wb_shard, chip).reshape(N_CHIPS, POOL_WIDTH, -1)
    wpw = _with_own(wpw, wp_shard, chip).reshape(N_CHIPS, ps.shape[1], -1)
    (o, states), ((ffn2w,),) = _hgrn_fwd(main, small["hgrn_lb"], tt, [_gather_rider([ffn2_shard], 0.7)])
    ffn2w = _with_own(ffn2w, ffn2_shard, chip).reshape(3, D_FF, D_MODEL)
    (x2, ya, yb, pooled), _ = _post_fwd(o, main, pool_r, gates, x1, small["hgrn_onorm"], pool_w0, small["pool_scale"], sqw,
                                       wbw, tm)
    (x3, a2, b2), _ = _ffn_fwd(x2, small["ffn2_norm"], ffn2w, 2, tm_ffn)
    (dx3, loss, d_wg, d_wp, d_ple, d_post, d_final), _ = _tail(
        x3, ps, target, small["ple_norm"], small["ple_post_norm"], small["final_norm"], sqw, wpw, tm_ffn)

    add_pairs = lambda parts, got, group: _add_pair(parts, got, core, group)
    add_chips = lambda sums, got, group: _add_chips(sums, got, chip, group)
    names1 = ("ffn2_w1", "ffn2_w3", "ffn2_w2", "ple_w_gate", "ple_w_proj")
    names2 = ("w_branch_a", "w_out", "w_branch_b")
    names3 = ("w_in",)
    names4 = ("ffn1_w1", "ffn1_w3")
    names5 = ("ffn1_w2",)
    tags1, tags2, tags3, tags4, tags5 = "ffn2", "branches", "w_in", "ffn1_in", "ffn1_out"

    (dx2, dab2, s2, h3, dxh2, d_ffn2_norm), _ = _ffn_bwd(dx3, x2, small["ffn2_norm"], a2, b2, ffn2w, 2, tm)
    (d_w13_2,), _ = _wgrad(dab2, h3, WGRAD_IN_BLOCKS, "wgrad_ffn2_in", tk)
    (d_w2_2,), _ = _wgrad(s2, dxh2, WGRAD_OUT_BLOCKS, "wgrad_ffn2_out", tk)
    part1 = [_shard_halves(d_w13_2, 2), _shard_halves(d_w2_2, 1), _shard_halves(d_wg, 1), _shard_halves(d_wp, 1)]
    (do, dog, du, dgates, d_wa, d_wout, d_wb, d_pool_w, d_pool_scale, d_onorm), (sib1,) = _post_bwd(
        dx2, o, main, gates, ya, yb, pooled, small["hgrn_onorm"], pool_w0, small["pool_scale"], sqw, wbw, tm,
        [_pair_rider(part1)])
    sums1 = add_pairs(part1, sib1, tags1)
    part2 = [_shard_halves(d_wa, 1), _shard_halves(d_wout, 1), _shard_halves(d_wb, 1)]
    (dqfi, d_lb), (got1, sib2) = _hgrn_bwd(main, small["hgrn_lb"], states, do, tt, [_chips_rider(sums1), _pair_rider(part2)])
    own1 = add_chips(sums1, got1, tags1)
    sums2 = add_pairs(part2, sib2, tags2)
    (dx1, dproj, h2, d_mix_norm), (other1, got2) = _mix_bwd(dqfi, dog, du, dgates, dx2, x1, small["mix_norm"], winw, tm,
                                                            [_share_rider(own1), _chips_rider(sums2)])
    own2 = add_chips(sums2, got2, tags2)
    (d_win,), (other2,) = _wgrad_cols(h2, dproj, N_CHIPS, "wgrad_in", tk, [_share_rider(own2)])
    part3 = [_shard_halves(d_win, 1)]
    (dx, dab1, s1, h1, dxh1, d_ffn1_norm), _ = _ffn_bwd(dx1, xs, small["ffn1_norm"], a1, b1, ffn1w, 1, tm)
    vecs = dict(ffn1_norm=d_ffn1_norm, mix_norm=d_mix_norm, hgrn_lb=d_lb, hgrn_onorm=d_onorm, ffn2_norm=d_ffn2_norm,
                ple_norm=d_ple, ple_post_norm=d_post, final_norm=d_final, pool_scale=d_pool_scale, pool_w=d_pool_w)
    (d_w2_1,), (sib3, (small_all,)) = _wgrad(s1, dxh1, WGRAD_OUT_BLOCKS, "wgrad_ffn1_out", tk,
                                             [_pair_rider(part3), _small_rider(_pack_small(vecs, loss))])
    sums3 = add_pairs(part3, sib3, tags3)
    part5 = [_shard_halves(d_w2_1, 1)]
    (d_w13_1,), (got3, sib5) = _wgrad(dab1, h1, WGRAD_IN_BLOCKS, "wgrad_ffn1_in", tk,
                                      [_chips_rider(sums3), _pair_rider(part5)])
    own3 = add_chips(sums3, got3, tags3)
    sums5 = add_pairs(part5, sib5, tags5)
    part4 = [_shard_halves(d_w13_1, 2)]
    sib4 = _alone(_pair_rider(part4), "pair_last")
    sums4 = add_pairs(part4, sib4, tags4)

    out = dict(grad={}, delta={}, new_m={}, new_v={})
    results, (other3, got4) = _adamw_group(_adam_items(names1 + names2, own1 + own2, other1 + other2, w, m, v), "early",
                                           [_share_rider(own3), _chips_rider(sums4)])
    _adam_store(names1 + names2, results, w, out)
    own4 = add_chips(sums4, got4, tags4)
    results, (other4, got5) = _adamw_group(_adam_items(names3, own3, other3, w, m, v), "w_in",
                                           [_share_rider(own4), _chips_rider(sums5)])
    _adam_store(names3, results, w, out)
    own5 = add_chips(sums5, got5, tags5)
    other5 = _alone(_share_rider(own5), "share_last")
    results, _ = _adamw_group(_adam_items(names4 + names5, own4 + own5, other4 + other5, w, m, v), "ffn1")
    _adam_store(names4 + names5, results, w, out)

    shapes = {name: w[name].shape for name in VECTOR_PARAMS + ("pool_w",)}
    loss, results = _adamw_small(_pack_small(w), small_all, _pack_small(m), _pack_small(v), shapes)
    for name, res in results.items():
        out["grad"][name], out["delta"][name], out["new_m"][name], out["new_v"][name] = (a.reshape(shapes[name]) for a in res)

    return (loss[0, 0], dx[None], *[out["grad"][n] for n in ALL_PARAMS], *[out["delta"][n] for n in ALL_PARAMS],
            *[out["new_m"][n] for n in ALL_PARAMS], *[out["new_v"][n] for n in ALL_PARAMS])
```

```python
import functools

import jax
import jax.numpy as jnp
from jax import lax
from jax.experimental import pallas as pl
from jax.experimental.pallas import tpu as pltpu

F32 = jnp.float32
BF16 = jnp.bfloat16
MESH = pl.DeviceIdType.MESH

D_MODEL = 1024
D_FF = 2816
HEADS = 8
HEAD_DIM = 128
POOL_WIDTH = 512
POOL_WINDOWS = (2, 4, 8, 16)
POOL_HALO = 16
N_CHIPS = 4
EPS = 1e-6
CHUNK = 64
MAIN_COLS = 4096
GATE_COLS = 2048
SHARD_IN_COLS = 1664

ADAM_LR = 0.001
ADAM_B1 = 0.9
ADAM_B2 = 0.999
ADAM_EPS = 1e-08
ADAM_WD = 0.01
ADAM_STEP = 10

VMEM_LIMIT = 56 * 1024 * 1024
WGRAD_IN_BLOCKS = 4
WGRAD_OUT_BLOCKS = 2
ADAM_BLOCKS = 4


def _params(semantics=None, vmem=VMEM_LIMIT):
    return pltpu.CompilerParams(dimension_semantics=semantics, vmem_limit_bytes=vmem)


def _dot(a, b):
    return jnp.dot(a, b, preferred_element_type=F32)


def _dot_nt(a, b):
    return lax.dot_general(a, b, (((1,), (1,)), ((), ())), preferred_element_type=F32)


def _dot_tn(a, b):
    return lax.dot_general(a, b, (((0,), (0,)), ((), ())), preferred_element_type=F32)


def _tri_sum(tri, x):
    hi = x.astype(BF16)
    lo = (x - hi.astype(F32)).astype(BF16)
    return _dot(tri, hi) + _dot(tri, lo)


def _sigmoid(x):
    return jax.nn.sigmoid(x)


def _resident(shape):
    zeros = (0,) * len(shape)
    return pl.BlockSpec(shape, lambda *_: zeros, pipeline_mode=pl.Buffered(1))


def _pick(shape, k):
    zeros = (0,) * (len(shape) - 1)
    return pl.BlockSpec((None,) + tuple(shape[1:]), lambda *_: (k,) + zeros, pipeline_mode=pl.Buffered(1))


def _rows(tm, cols, col_block=0):
    return pl.BlockSpec((tm, cols), lambda i: (i, col_block))


def _acc(shape):
    zeros = (0,) * len(shape)
    return pl.BlockSpec(shape, lambda *_: zeros)


def _rms(x):
    r = lax.rsqrt(jnp.mean(x * x, axis=-1, keepdims=True) + EPS)
    return r, x * r


def _rms_bwd(dn, n, r):
    return r * (dn - n * jnp.mean(dn * n, axis=-1, keepdims=True))


def _colsum(a):
    return jnp.sum(a, axis=0, keepdims=True)


ANY = pl.BlockSpec(memory_space=pl.ANY)


class _Rider:
    def __init__(self, inputs, out_shape, sems, phases):
        self.inputs, self.out_shape, self.sems, self.phases = list(inputs), list(out_shape), list(sems), list(phases)


def _hosted(riders, body, *, name, grid=(), in_specs, out_specs, out_shape, scratch_shapes=(), compiler_params=None):
    riders = [r for r in riders if r is not None]
    n_in, n_out, n_scr = len(in_specs), len(out_shape), len(scratch_shapes)
    n_steps = 1
    for g in grid:
        n_steps *= g

    def wrapped(*refs):
        pos = n_in
        ins = refs[:n_in]
        r_ins = []
        for r in riders:
            r_ins.append(refs[pos:pos + len(r.inputs)])
            pos += len(r.inputs)
        outs = refs[pos:pos + n_out]
        pos += n_out
        r_outs = []
        for r in riders:
            r_outs.append(refs[pos:pos + len(r.out_shape)])
            pos += len(r.out_shape)
        scr = refs[pos:pos + n_scr]
        pos += n_scr
        r_sems = []
        for r in riders:
            r_sems.append(refs[pos:pos + len(r.sems)])
            pos += len(r.sems)
        step = 0
        for axis in range(len(grid)):
            step = step * grid[axis] + pl.program_id(axis)

        def at_step(which, fn):
            if n_steps == 1:
                fn()
            else:
                pl.when(step == which)(fn)

        for r, ri, ro, rs in zip(riders, r_ins, r_outs, r_sems):
            for fraction, fn in r.phases:
                if fraction == 0:
                    at_step(0, functools.partial(fn, ri, ro, rs))
        body(*ins, *outs, *scr)
        for r, ri, ro, rs in zip(riders, r_ins, r_outs, r_sems):
            for fraction, fn in r.phases:
                if fraction > 0:
                    at_step(min(int(fraction * n_steps), n_steps - 1), functools.partial(fn, ri, ro, rs))

    call = pl.pallas_call(
        wrapped, name=name, grid=grid,
        in_specs=list(in_specs) + [ANY for r in riders for _ in r.inputs],
        out_specs=list(out_specs) + [ANY for r in riders for _ in r.out_shape],
        out_shape=list(out_shape) + [s for r in riders for s in r.out_shape],
        scratch_shapes=list(scratch_shapes) + [s for r in riders for s in r.sems],
        compiler_params=compiler_params)

    def run(*args):
        res = call(*args, *[a for r in riders for a in r.inputs])
        extras, pos = [], n_out
        for r in riders:
            extras.append(list(res[pos:pos + len(r.out_shape)]))
            pos += len(r.out_shape)
        return list(res[:n_out]), extras

    return run


def _ffn_fwd(x, g, ffnw, tag, tm, riders=()):
    t = x.shape[0]

    def body(x_ref, g_ref, w1_ref, w3_ref, w2_ref, xo_ref, a_ref, b_ref):
        xv = x_ref[...]
        _, n = _rms(xv)
        h = (n * g_ref[...]).astype(BF16)
        a = _dot_nt(h, w1_ref[...])
        b = _dot_nt(h, w3_ref[...])
        s = (a * _sigmoid(a) * b).astype(BF16)
        xo_ref[...] = xv + 0.5 * _dot(s, w2_ref[...])
        a_ref[...] = a.astype(BF16)
        b_ref[...] = b.astype(BF16)

    return _hosted(
        riders, body, name=f"ffn_fwd_{tag}", grid=(t // tm,),
        in_specs=[_rows(tm, D_MODEL), _resident((1, D_MODEL)), _pick(ffnw.shape, 0), _pick(ffnw.shape, 1),
                  _pick(ffnw.shape, 2)],
        out_specs=[_rows(tm, D_MODEL), _rows(tm, D_FF), _rows(tm, D_FF)],
        out_shape=[jax.ShapeDtypeStruct((t, D_MODEL), F32), jax.ShapeDtypeStruct((t, D_FF), BF16),
                   jax.ShapeDtypeStruct((t, D_FF), BF16)],
        compiler_params=_params(("arbitrary",)),
    )(x, g, ffnw, ffnw, ffnw)


def _ffn_bwd(dxo, x, g, a, b, ffnw, tag, tm, riders=()):
    t = x.shape[0]

    def body(dxo_ref, x_ref, g_ref, a_ref, b_ref, w1_ref, w3_ref, w2_ref, dx_ref, dab_ref, s_ref, h_ref, dxh_ref, dg_ref):
        @pl.when(pl.program_id(0) == 0)
        def _():
            dg_ref[...] = jnp.zeros_like(dg_ref)

        xv = x_ref[...]
        gv = g_ref[...]
        r, n = _rms(xv)
        h_ref[...] = (n * gv).astype(BF16)
        dxo_v = dxo_ref[...]
        dxh = (0.5 * dxo_v).astype(BF16)
        dxh_ref[...] = dxh
        ds = _dot_nt(dxh, w2_ref[...])
        av = a_ref[...].astype(F32)
        bv = b_ref[...].astype(F32)
        sg = _sigmoid(av)
        silu = av * sg
        s_ref[...] = (silu * bv).astype(BF16)
        da = (ds * bv * (sg * (1.0 + av * (1.0 - sg)))).astype(BF16)
        db = (ds * silu).astype(BF16)
        dab_ref[:, :D_FF] = da
        dab_ref[:, D_FF:] = db
        dh = _dot(da, w1_ref[...]) + _dot(db, w3_ref[...])
        dg_ref[...] += _colsum(dh * n)
        dx_ref[...] = dxo_v + _rms_bwd(dh * gv, n, r)

    return _hosted(
        riders, body, name=f"ffn_bwd_{tag}", grid=(t // tm,),
        in_specs=[_rows(tm, D_MODEL), _rows(tm, D_MODEL), _resident((1, D_MODEL)), _rows(tm, D_FF), _rows(tm, D_FF),
                  _pick(ffnw.shape, 0), _pick(ffnw.shape, 1), _pick(ffnw.shape, 2)],
        out_specs=[_rows(tm, D_MODEL), _rows(tm, 2 * D_FF), _rows(tm, D_FF), _rows(tm, D_MODEL), _rows(tm, D_MODEL),
                   _acc((1, D_MODEL))],
        out_shape=[jax.ShapeDtypeStruct((t, D_MODEL), F32), jax.ShapeDtypeStruct((t, 2 * D_FF), BF16),
                   jax.ShapeDtypeStruct((t, D_FF), BF16), jax.ShapeDtypeStruct((t, D_MODEL), BF16),
                   jax.ShapeDtypeStruct((t, D_MODEL), BF16), jax.ShapeDtypeStruct((1, D_MODEL), F32)],
        compiler_params=_params(("arbitrary",)),
    )(dxo, x, g, a, b, ffnw, ffnw, ffnw)


def _wgrad_body(n_token_tiles):
    def body(x_ref, dy_ref, o_ref, acc):
        k = pl.program_id(1)

        @pl.when(k == 0)
        def _():
            acc[...] = jnp.zeros_like(acc)

        acc[...] += _dot_tn(x_ref[...], dy_ref[...])

        @pl.when(k == n_token_tiles - 1)
        def _():
            o_ref[...] = acc[...].astype(BF16)

    return body


def _wgrad(xm, dy, out_blocks, name, tk, riders=()):
    t, m = xm.shape
    n = dy.shape[1]
    mb = m // out_blocks

    return _hosted(
        riders, _wgrad_body(t // tk), name=name, grid=(out_blocks, t // tk),
        in_specs=[pl.BlockSpec((tk, mb), lambda j, k: (k, j)), pl.BlockSpec((tk, n), lambda j, k: (k, 0))],
        out_specs=[pl.BlockSpec((None, mb, n), lambda j, k: (j, 0, 0))],
        out_shape=[jax.ShapeDtypeStruct((out_blocks, mb, n), BF16)],
        scratch_shapes=[pltpu.VMEM((mb, n), F32)],
        compiler_params=_params(("arbitrary", "arbitrary")),
    )(xm, dy)


def _wgrad_cols(xm, dy, out_blocks, name, tk, riders=()):
    t, m = xm.shape
    n = dy.shape[1]
    nb = n // out_blocks

    return _hosted(
        riders, _wgrad_body(t // tk), name=name, grid=(out_blocks, t // tk),
        in_specs=[pl.BlockSpec((tk, m), lambda j, k: (k, 0)), pl.BlockSpec((tk, nb), lambda j, k: (k, j))],
        out_specs=[pl.BlockSpec((None, m, nb), lambda j, k: (j, 0, 0))],
        out_shape=[jax.ShapeDtypeStruct((out_blocks, m, nb), BF16)],
        scratch_shapes=[pltpu.VMEM((m, nb), F32)],
        compiler_params=_params(("arbitrary", "arbitrary")),
    )(xm, dy)


def _mix_fwd(x1, g, winw, tm, riders=()):
    t = x1.shape[0]

    def body(x_ref, g_ref, w_ref, main_ref, pool_ref, gate_ref):
        _, n = _rms(x_ref[...])
        h = (n * g_ref[...]).astype(BF16)
        proj = jnp.concatenate([_dot(h, w_ref[j]) for j in range(N_CHIPS)], axis=1)
        main_ref[...] = proj[:, :MAIN_COLS]
        pool_ref[...] = proj[:, MAIN_COLS:MAIN_COLS + POOL_WIDTH]
        gate_ref[...] = proj[:, MAIN_COLS + POOL_WIDTH:]

    return _hosted(
        riders, body, name="mix_fwd", grid=(t // tm,),
        in_specs=[_rows(tm, D_MODEL), _resident((1, D_MODEL)), _resident(winw.shape)],
        out_specs=[_rows(tm, MAIN_COLS), _rows(tm, POOL_WIDTH), _rows(tm, GATE_COLS)],
        out_shape=[jax.ShapeDtypeStruct((t, MAIN_COLS), F32), jax.ShapeDtypeStruct((t, POOL_WIDTH), F32),
                   jax.ShapeDtypeStruct((t, GATE_COLS), F32)],
        compiler_params=_params(("arbitrary",)),
    )(x1, g, winw)


def _mix_bwd(dqfi, dog, du, dgates, dx2, x1, g, winw, tm, riders=()):
    t = x1.shape[0]
    cols = N_CHIPS * SHARD_IN_COLS

    def body(dqfi_ref, dog_ref, du_ref, dgt_ref, dx2_ref, x_ref, g_ref, w_ref, dx_ref, dproj_ref, h_ref, dg_ref):
        @pl.when(pl.program_id(0) == 0)
        def _():
            dg_ref[...] = jnp.zeros_like(dg_ref)

        dproj = jnp.concatenate([dqfi_ref[...], dog_ref[...], du_ref[...], dgt_ref[...]], axis=1)
        dproj_ref[...] = dproj
        dh = _dot_nt(dproj[:, :SHARD_IN_COLS], w_ref[0])
        for j in range(1, N_CHIPS):
            dh += _dot_nt(dproj[:, j * SHARD_IN_COLS:(j + 1) * SHARD_IN_COLS], w_ref[j])
        gv = g_ref[...]
        r, n = _rms(x_ref[...])
        h_ref[...] = (n * gv).astype(BF16)
        dg_ref[...] += _colsum(dh * n)
        dx_ref[...] = dx2_ref[...] + _rms_bwd(dh * gv, n, r)

    return _hosted(
        riders, body, name="mix_bwd", grid=(t // tm,),
        in_specs=[_rows(tm, 3 * D_MODEL), _rows(tm, D_MODEL), _rows(tm, POOL_WIDTH), _rows(tm, GATE_COLS),
                  _rows(tm, D_MODEL), _rows(tm, D_MODEL), _resident((1, D_MODEL)), _resident(winw.shape)],
        out_specs=[_rows(tm, D_MODEL), _rows(tm, cols), _rows(tm, D_MODEL), _acc((1, D_MODEL))],
        out_shape=[jax.ShapeDtypeStruct((t, D_MODEL), F32), jax.ShapeDtypeStruct((t, cols), BF16),
                   jax.ShapeDtypeStruct((t, D_MODEL), BF16), jax.ShapeDtypeStruct((1, D_MODEL), F32)],
        compiler_params=_params(("arbitrary",)),
    )(dqfi, dog, du, dgates, dx2, x1, g, winw)


def _lower_bound(lb_raw):
    l0 = lb_raw[0:1, :]
    l1 = lb_raw[1:2, :]
    m = jnp.maximum(l0, l1)
    e0 = jnp.exp(l0 - m)
    e1 = jnp.exp(l1 - m)
    return e0 / (e0 + e1)


def _head_slices():
    return [slice(h * HEAD_DIM, (h + 1) * HEAD_DIM) for h in range(HEADS)]


def _gates(qr, fr, lb, tril_b, first_half):
    sg = _sigmoid(fr)
    f = lb + (1.0 - lb) * sg
    k = 1.0 - f
    sq = _sigmoid(qr)
    q = qr * sq
    log_f = jnp.log(f)
    gc = _tri_sum(tril_b, log_f)
    gm = _colsum(jnp.where(first_half, log_f, 0.0))
    gl = _colsum(log_f)
    e_q = jnp.exp(gc - gm)
    e_k = jnp.exp(gm - gc)
    e_in = jnp.exp(gc)
    e_out = jnp.exp(gl - gc)
    return dict(sg=sg, f=f, k=k, sq=sq, q=q, e_q=e_q, e_k=e_k, e_in=e_in, e_out=e_out, e_last=jnp.exp(gl))


def _hgrn_fwd(main, lb_raw, tt, riders=()):
    t = main.shape[0]
    n_local = tt // CHUNK

    def body(q_ref, f_ref, i_ref, lb_ref, o_ref, st_ref, s_scr):
        @pl.when(pl.program_id(0) == 0)
        def _():
            s_scr[...] = jnp.zeros_like(s_scr)

        lb = _lower_bound(lb_ref[...])
        row = lax.broadcasted_iota(jnp.int32, (CHUNK, CHUNK), 0)
        col = lax.broadcasted_iota(jnp.int32, (CHUNK, CHUNK), 1)
        tril = row >= col
        tril_b = tril.astype(BF16)
        first_half = lax.broadcasted_iota(jnp.int32, (CHUNK, D_MODEL), 0) < CHUNK // 2
        heads = _head_slices()

        def chunk(c, carry):
            rows = pl.ds(pl.multiple_of(c * CHUNK, CHUNK), CHUNK)
            z = _gates(q_ref[rows, :], f_ref[rows, :], lb, tril_b, first_half)
            qt = (z["q"] * z["e_q"]).astype(BF16)
            kt = (z["k"] * z["e_k"]).astype(BF16)
            qg = (z["q"] * z["e_in"]).astype(BF16)
            kg = (z["k"] * z["e_out"]).astype(BF16)
            vb = i_ref[rows, :].astype(BF16)
            states = [s_scr[h] for h in range(HEADS)]
            for h in range(HEADS):
                st_ref[c, h] = states[h]
            raw = [_dot_nt(qt[:, sl], kt[:, sl]) for sl in heads]
            inter = [_dot_nt(qg[:, sl], states[h].astype(BF16)) for h, sl in enumerate(heads)]
            grown = [_dot_tn(vb[:, sl], kg[:, sl]) for sl in heads]
            scores = [jnp.where(tril, r, 0.0).astype(BF16) for r in raw]
            for h, sl in enumerate(heads):
                s_scr[h] = states[h] * z["e_last"][:, sl] + grown[h]
            o_ref[rows, :] = jnp.concatenate([_dot(scores[h], vb[:, sl]) + inter[h] for h, sl in enumerate(heads)], axis=1)
            return carry

        lax.fori_loop(0, n_local, chunk, 0, unroll=True)

    return _hosted(
        riders, body, name="hgrn_fwd", grid=(t // tt,),
        in_specs=[_rows(tt, D_MODEL, 0), _rows(tt, D_MODEL, 1), _rows(tt, D_MODEL, 2), _resident((2, D_MODEL))],
        out_specs=[_rows(tt, D_MODEL),
                   pl.BlockSpec((n_local, HEADS, HEAD_DIM, HEAD_DIM), lambda i: (i, 0, 0, 0))],
        out_shape=[jax.ShapeDtypeStruct((t, D_MODEL), F32),
                   jax.ShapeDtypeStruct((t // CHUNK, HEADS, HEAD_DIM, HEAD_DIM), F32)],
        scratch_shapes=[pltpu.VMEM((HEADS, HEAD_DIM, HEAD_DIM), F32)],
        compiler_params=_params(("arbitrary",)),
    )(main, main, main, lb_raw)


def _hgrn_bwd(main, lb_raw, states, do, tt, riders=()):
    t = main.shape[0]
    n_tiles = t // tt
    n_local = tt // CHUNK

    def rev(col_block):
        return pl.BlockSpec((tt, D_MODEL), lambda i: (n_tiles - 1 - i, col_block))

    def body(q_ref, f_ref, i_ref, lb_ref, st_ref, do_ref, dqfi_ref, dlb_ref, ds_scr, acc_scr):
        @pl.when(pl.program_id(0) == 0)
        def _():
            ds_scr[...] = jnp.zeros_like(ds_scr)
            acc_scr[...] = jnp.zeros_like(acc_scr)

        lb = _lower_bound(lb_ref[...])
        row = lax.broadcasted_iota(jnp.int32, (CHUNK, CHUNK), 0)
        col = lax.broadcasted_iota(jnp.int32, (CHUNK, CHUNK), 1)
        tril = row >= col
        tril_b = tril.astype(BF16)
        triu_b = (row <= col).astype(BF16)
        first_half = lax.broadcasted_iota(jnp.int32, (CHUNK, D_MODEL), 0) < CHUNK // 2
        heads = _head_slices()
        cat = functools.partial(jnp.concatenate, axis=1)

        def chunk(cc, carry):
            c = n_local - 1 - cc
            rows = pl.ds(pl.multiple_of(c * CHUNK, CHUNK), CHUNK)
            qr = q_ref[rows, :]
            z = _gates(qr, f_ref[rows, :], lb, tril_b, first_half)
            qt = (z["q"] * z["e_q"]).astype(BF16)
            kt = (z["k"] * z["e_k"]).astype(BF16)
            qg_f = z["q"] * z["e_in"]
            qg = qg_f.astype(BF16)
            kg_f = z["k"] * z["e_out"]
            kg = kg_f.astype(BF16)
            vb = i_ref[rows, :].astype(BF16)
            dob = do_ref[rows, :].astype(BF16)
            st = [st_ref[c, h] for h in range(HEADS)]
            dst = [ds_scr[h] for h in range(HEADS)]
            dst_b = [d.astype(BF16) for d in dst]
            raw = [_dot_nt(qt[:, sl], kt[:, sl]) for sl in heads]
            draw = [_dot_nt(dob[:, sl], vb[:, sl]) for sl in heads]
            dqg = [_dot(dob[:, sl], st[h].astype(BF16)) for h, sl in enumerate(heads)]
            dkg = [_dot(vb[:, sl], dst_b[h]) for h, sl in enumerate(heads)]
            dv_inter = [_dot_nt(kg[:, sl], dst_b[h]) for h, sl in enumerate(heads)]
            grown = [_dot_tn(dob[:, sl], qg[:, sl]) for sl in heads]
            scores = [jnp.where(tril, r, 0.0).astype(BF16) for r in raw]
            dscores = [jnp.where(tril, r, 0.0).astype(BF16) for r in draw]
            dqt = [_dot(dscores[h], kt[:, sl]) for h, sl in enumerate(heads)]
            dkt = [_dot_tn(dscores[h], qt[:, sl]) for h, sl in enumerate(heads)]
            dv = [_dot_tn(scores[h], dob[:, sl]) + dv_inter[h] for h, sl in enumerate(heads)]
            carry_in = cat([z["e_last"][:, sl] * _colsum(dst[h] * st[h]) for h, sl in enumerate(heads)])
            for h, sl in enumerate(heads):
                ds_scr[h] = dst[h] * z["e_last"][:, sl] + grown[h]
            dqt, dkt, dqg, dkg = cat(dqt), cat(dkt), cat(dqg), cat(dkg)
            carry_in += _colsum(dkg * kg_f)
            dq = dqt * z["e_q"] + dqg * z["e_in"]
            dk = dkt * z["e_k"] + dkg * z["e_out"]
            dgate = (qt.astype(F32) * dqt - kt.astype(F32) * dkt) + (qg_f * dqg - kg_f * dkg)
            dlogf = _tri_sum(triu_b, dgate) + carry_in
            df = dlogf / z["f"] - dk
            sg = z["sg"]
            sq = z["sq"]
            acc_scr[...] += _colsum(df * (1.0 - sg))
            dqfi_ref[rows, 0:D_MODEL] = (dq * (sq * (1.0 + qr * (1.0 - sq)))).astype(BF16)
            dqfi_ref[rows, D_MODEL:2 * D_MODEL] = (df * (1.0 - lb) * sg * (1.0 - sg)).astype(BF16)
            dqfi_ref[rows, 2 * D_MODEL:3 * D_MODEL] = cat(dv).astype(BF16)
            return carry

        lax.fori_loop(0, n_local, chunk, 0, unroll=True)
        d0 = acc_scr[...] * lb * (1.0 - lb)
        dlb_ref[0:1, :] = d0
        dlb_ref[1:2, :] = -d0

    return _hosted(
        riders, body, name="hgrn_bwd", grid=(n_tiles,),
        in_specs=[rev(0), rev(1), rev(2), _resident((2, D_MODEL)),
                  pl.BlockSpec((n_local, HEADS, HEAD_DIM, HEAD_DIM), lambda i: (n_tiles - 1 - i, 0, 0, 0)),
                  rev(0)],
        out_specs=[pl.BlockSpec((tt, 3 * D_MODEL), lambda i: (n_tiles - 1 - i, 0)), _acc((2, D_MODEL))],
        out_shape=[jax.ShapeDtypeStruct((t, 3 * D_MODEL), BF16), jax.ShapeDtypeStruct((2, D_MODEL), F32)],
        scratch_shapes=[pltpu.VMEM((HEADS, HEAD_DIM, HEAD_DIM), F32), pltpu.VMEM((1, D_MODEL), F32)],
        compiler_params=_params(("arbitrary",)),
    )(main, main, main, lb_raw, states, do)


def _head_norm(o):
    rs, ns = [], []
    for h in range(HEADS):
        oh = o[:, h * HEAD_DIM:(h + 1) * HEAD_DIM]
        r, n = _rms(oh)
        rs.append(jnp.broadcast_to(r, oh.shape))
        ns.append(n)
    return jnp.concatenate(rs, axis=1), jnp.concatenate(ns, axis=1)


def _head_norm_bwd(dn, n, r):
    outs = []
    for h in range(HEADS):
        sl = slice(h * HEAD_DIM, (h + 1) * HEAD_DIM)
        outs.append(_rms_bwd(dn[:, sl], n[:, sl], r[:, sl]))
    return jnp.concatenate(outs, axis=1)


def _window_counts(first_row, tm):
    pos = (first_row + 1 + lax.broadcasted_iota(jnp.int32, (tm, 1), 0)).astype(F32)
    return [jnp.minimum(pos, float(w)) for w in POOL_WINDOWS]


def _post_fwd(o, main, pool_r, gates, x1, onorm, pool_w, pool_scale, sqw, wbw, tm, riders=()):
    t = o.shape[0]
    ext_rows = tm + POOL_HALO

    def body(o_ref, og_ref, u_ref, gt_ref, x1_ref, on_ref, pw_ref, ps_ref, wa_ref, wout_ref, wb_ref,
             x2_ref, ya_ref, yb_ref, pooled_ref, ext):
        i = pl.program_id(0)

        @pl.when(i == 0)
        def _():
            ext[0:POOL_HALO, :] = jnp.zeros((POOL_HALO, POOL_WIDTH), F32)

        _, n = _head_norm(o_ref[...])
        og = og_ref[...]
        oa = (n * on_ref[...] * (og * _sigmoid(og))).astype(BF16)
        ya = _dot(oa, wa_ref[...])

        u = u_ref[...]
        ext[POOL_HALO:ext_rows, :] = u
        e = ext[...]
        counts = _window_counts(i * tm, tm)
        pooled = []
        for gidx, w in enumerate(POOL_WINDOWS):
            s = e[:, gidx * HEAD_DIM:(gidx + 1) * HEAD_DIM]
            shift = 1
            while shift < w:
                s = s + pltpu.roll(s, shift, axis=0)
                shift *= 2
            pooled.append(s[POOL_HALO:, :] / counts[gidx] - u[:, gidx * HEAD_DIM:(gidx + 1) * HEAD_DIM])
        ext[0:POOL_HALO, :] = ext[tm:ext_rows, :]
        pooled_b = [pg.astype(BF16) for pg in pooled]
        pooled_ref[...] = jnp.concatenate(pooled_b, axis=1)
        mixed = jnp.concatenate([_dot(pooled_b[gidx], pw_ref[gidx].astype(BF16)) for gidx in range(len(POOL_WINDOWS))],
                                axis=1) * ps_ref[...]
        mixed_b = mixed.astype(BF16)
        yb = jnp.concatenate([_dot(mixed_b, wb_ref[j]) for j in range(N_CHIPS)], axis=1)

        gt = gt_ref[...]
        y = _sigmoid(gt[:, :D_MODEL]) * ya + _sigmoid(gt[:, D_MODEL:]) * yb
        x2_ref[...] = x1_ref[...] + _dot(y.astype(BF16), wout_ref[...])
        ya_ref[...] = ya.astype(BF16)
        yb_ref[...] = yb.astype(BF16)

    return _hosted(
        riders, body, name="post_fwd", grid=(t // tm,),
        in_specs=[_rows(tm, D_MODEL), _rows(tm, D_MODEL, 3), _rows(tm, POOL_WIDTH), _rows(tm, GATE_COLS), _rows(tm, D_MODEL),
                  _resident((1, D_MODEL)), _resident(pool_w.shape), _resident((1, POOL_WIDTH)),
                  _pick(sqw.shape, 0), _pick(sqw.shape, 1), _resident(wbw.shape)],
        out_specs=[_rows(tm, D_MODEL), _rows(tm, D_MODEL), _rows(tm, D_MODEL), _rows(tm, POOL_WIDTH)],
        out_shape=[jax.ShapeDtypeStruct((t, D_MODEL), F32), jax.ShapeDtypeStruct((t, D_MODEL), BF16),
                   jax.ShapeDtypeStruct((t, D_MODEL), BF16), jax.ShapeDtypeStruct((t, POOL_WIDTH), BF16)],
        scratch_shapes=[pltpu.VMEM((ext_rows, POOL_WIDTH), F32)],
        compiler_params=_params(("arbitrary",)),
    )(o, main, pool_r, gates, x1, onorm, pool_w, pool_scale, sqw, sqw, wbw)


def _post_bwd(dx2, o, main, gates, ya, yb, pooled, onorm, pool_w, pool_scale, sqw, wbw, tm, riders=()):
    t = o.shape[0]
    n_tiles = t // tm
    ext_rows = tm + POOL_HALO
    n_groups = len(POOL_WINDOWS)

    def rev(cols, col_block=0):
        return pl.BlockSpec((tm, cols), lambda i: (n_tiles - 1 - i, col_block))

    def body(dx2_ref, o_ref, og_ref, gt_ref, ya_ref, yb_ref, pooled_ref, on_ref, pw_ref, ps_ref, wa_ref, wout_ref, wb_ref,
             do_ref, dog_ref, du_ref, dgt_ref, dwa_ref, dwout_ref, dwb_ref, dpw_ref, dps_ref, don_ref, ext):
        i = pl.program_id(0)

        @pl.when(i == 0)
        def _():
            ext[tm:ext_rows, :] = jnp.zeros((POOL_HALO, POOL_WIDTH), F32)
            for ref in (dwa_ref, dwout_ref, dwb_ref, dpw_ref, dps_ref, don_ref):
                ref[...] = jnp.zeros_like(ref)

        groups = [slice(gidx * HEAD_DIM, (gidx + 1) * HEAD_DIM) for gidx in range(n_groups)]
        shards = [slice(j * 256, (j + 1) * 256) for j in range(N_CHIPS)]
        dx2b = dx2_ref[...].astype(BF16)
        dy = _dot_nt(dx2b, wout_ref[...])
        pooled_b = pooled_ref[...]
        pm = jnp.concatenate([_dot(pooled_b[:, g], pw_ref[gidx].astype(BF16)) for gidx, g in enumerate(groups)], axis=1)
        gt = gt_ref[...]
        sga = _sigmoid(gt[:, :D_MODEL])
        sgb = _sigmoid(gt[:, D_MODEL:])
        ya = ya_ref[...].astype(F32)
        yb = yb_ref[...].astype(F32)
        y = (sga * ya + sgb * yb).astype(BF16)
        dya = (dy * sga).astype(BF16)
        dyb = (dy * sgb).astype(BF16)
        dgt_ref[:, :D_MODEL] = (dy * ya * sga * (1.0 - sga)).astype(BF16)
        dgt_ref[:, D_MODEL:] = (dy * yb * sgb * (1.0 - sgb)).astype(BF16)
        dwout_ref[...] += _dot_tn(y, dx2b)
        doa = _dot_nt(dya, wa_ref[...])
        dmixed = _dot_nt(dyb[:, shards[0]], wb_ref[0])
        for j in range(1, N_CHIPS):
            dmixed += _dot_nt(dyb[:, shards[j]], wb_ref[j])
        r, n = _head_norm(o_ref[...])
        onv = on_ref[...]
        og = og_ref[...]
        sog = _sigmoid(og)
        silu_og = og * sog
        normed = n * onv
        oa = (normed * silu_og).astype(BF16)
        dog_ref[...] = (doa * normed * (sog * (1.0 + og * (1.0 - sog)))).astype(BF16)
        dnormed = doa * silu_og
        don_ref[...] += _colsum(dnormed * n)
        do_ref[...] = _head_norm_bwd(dnormed * onv, n, r)
        psv = ps_ref[...]
        mixed_b = (pm * psv).astype(BF16)
        dps_ref[...] += _colsum(dmixed * pm)
        dpm = (dmixed * psv).astype(BF16)
        dwa_ref[...] += _dot_tn(oa, dya)
        for j in range(N_CHIPS):
            dwb_ref[j] += _dot_tn(mixed_b, dyb[:, shards[j]])
        counts = _window_counts((n_tiles - 1 - i) * tm, tm)
        dpooled = []
        for gidx, g in enumerate(groups):
            dpw_ref[gidx] += _dot_tn(pooled_b[:, g], dpm[:, g])
            dpooled.append(_dot_nt(dpm[:, g], pw_ref[gidx].astype(BF16)))
        ext[0:tm, :] = jnp.concatenate([dpooled[gidx] / counts[gidx] for gidx in range(n_groups)], axis=1)
        e = ext[...]
        du = []
        for gidx, w in enumerate(POOL_WINDOWS):
            s = e[:, gidx * HEAD_DIM:(gidx + 1) * HEAD_DIM]
            shift = 1
            while shift < w:
                s = s + pltpu.roll(s, ext_rows - shift, axis=0)
                shift *= 2
            du.append(s[:tm, :] - dpooled[gidx])
        ext[tm:ext_rows, :] = ext[0:POOL_HALO, :]
        du_ref[...] = jnp.concatenate(du, axis=1).astype(BF16)

    wa_shape = (D_MODEL, D_MODEL)
    return _hosted(
        riders, body, name="post_bwd", grid=(n_tiles,),
        in_specs=[rev(D_MODEL), rev(D_MODEL), rev(D_MODEL, 3), rev(GATE_COLS), rev(D_MODEL), rev(D_MODEL), rev(POOL_WIDTH),
                  _resident((1, D_MODEL)), _resident(pool_w.shape), _resident((1, POOL_WIDTH)),
                  _pick(sqw.shape, 0), _pick(sqw.shape, 1), _resident(wbw.shape)],
        out_specs=[rev(D_MODEL), rev(D_MODEL), rev(POOL_WIDTH), rev(GATE_COLS),
                   _acc(wa_shape), _acc(wa_shape), _acc(wbw.shape), _acc(pool_w.shape), _acc((1, POOL_WIDTH)),
                   _acc((1, D_MODEL))],
        out_shape=[jax.ShapeDtypeStruct((t, D_MODEL), F32), jax.ShapeDtypeStruct((t, D_MODEL), BF16),
                   jax.ShapeDtypeStruct((t, POOL_WIDTH), BF16), jax.ShapeDtypeStruct((t, GATE_COLS), BF16),
                   jax.ShapeDtypeStruct(wa_shape, F32), jax.ShapeDtypeStruct(wa_shape, F32),
                   jax.ShapeDtypeStruct(wbw.shape, F32), jax.ShapeDtypeStruct(pool_w.shape, F32),
                   jax.ShapeDtypeStruct((1, POOL_WIDTH), F32), jax.ShapeDtypeStruct((1, D_MODEL), F32)],
        scratch_shapes=[pltpu.VMEM((ext_rows, POOL_WIDTH), F32)],
        compiler_params=_params(("arbitrary",)),
    )(dx2, o, main, gates, ya, yb, pooled, onorm, pool_w, pool_scale, sqw, sqw, wbw)


def _tail(x3, p, target, g_ple, g_post, g_final, sqw, wpw, tm, riders=()):
    t = x3.shape[0]
    pd = p.shape[1]

    def body(x_ref, p_ref, tg_ref, g4_ref, g5_ref, g6_ref, wg_ref, wp_ref,
             dx_ref, loss_ref, dwg_ref, dwp_ref, dg4_ref, dg5_ref, dg6_ref):
        @pl.when(pl.program_id(0) == 0)
        def _():
            for ref in (loss_ref, dwg_ref, dwp_ref, dg4_ref, dg5_ref, dg6_ref):
                ref[...] = jnp.zeros_like(ref)

        x3v = x_ref[...]
        g4, g5, g6 = g4_ref[...], g5_ref[...], g6_ref[...]
        r4, n4 = _rms(x3v)
        h4 = (n4 * g4).astype(BF16)
        gate = _sigmoid(_dot(h4, wg_ref[...]))
        pb = p_ref[...].astype(BF16)
        r5, n5 = _rms(jnp.concatenate([_dot(pb, wp_ref[j]) for j in range(N_CHIPS)], axis=1))
        emb = n5 * g5
        r6, n6 = _rms(x3v + gate * emb)
        diff = n6 * g6 - tg_ref[...]
        loss_ref[...] += 0.5 * jnp.sum(jnp.mean(diff * diff, axis=-1, keepdims=True), axis=0, keepdims=True)
        dout = diff * (1.0 / D_MODEL)
        dg6_ref[...] += _colsum(dout * n6)
        dx4 = _rms_bwd(dout * g6, n6, r6)
        demb = dx4 * gate
        dg5_ref[...] += _colsum(demb * n5)
        dpre = _rms_bwd(demb * g5, n5, r5).astype(BF16)
        for j in range(N_CHIPS):
            dwp_ref[j] += _dot_tn(pb, dpre[:, j * pd:(j + 1) * pd])
        dz = (dx4 * emb * gate * (1.0 - gate)).astype(BF16)
        dwg_ref[...] += _dot_tn(h4, dz)
        dh4 = _dot_nt(dz, wg_ref[...])
        dg4_ref[...] += _colsum(dh4 * n4)
        dx_ref[...] = dx4 + _rms_bwd(dh4 * g4, n4, r4)

    sq_shape = (D_MODEL, D_MODEL)
    vec = (1, D_MODEL)
    return _hosted(
        riders, body, name="tail", grid=(t // tm,),
        in_specs=[_rows(tm, D_MODEL), _rows(tm, pd), _rows(tm, D_MODEL), _resident(vec), _resident(vec), _resident(vec),
                  _pick(sqw.shape, 2), _resident(wpw.shape)],
        out_specs=[_rows(tm, D_MODEL), _acc((1, 1)), _acc(sq_shape), _acc(wpw.shape), _acc(vec), _acc(vec), _acc(vec)],
        out_shape=[jax.ShapeDtypeStruct((t, D_MODEL), F32), jax.ShapeDtypeStruct((1, 1), F32),
                   jax.ShapeDtypeStruct(sq_shape, F32), jax.ShapeDtypeStruct(wpw.shape, F32),
                   jax.ShapeDtypeStruct(vec, F32), jax.ShapeDtypeStruct(vec, F32), jax.ShapeDtypeStruct(vec, F32)],
        compiler_params=_params(("arbitrary",)),
    )(x3, p, target, g_ple, g_post, g_final, sqw, wpw)


def _position():
    return lax.axis_index("x"), lax.axis_index("y"), lax.axis_index("c")


def _other_chips(x, y):
    return [(1 - x, y), (x, 1 - y), (1 - x, 1 - y)]


def _remote(src, dst, send_sems, recv_sems, k, device):
    return pltpu.make_async_remote_copy(src_ref=src, dst_ref=dst, send_sem=send_sems.at[k], recv_sem=recv_sems.at[k],
                                        device_id=device, device_id_type=MESH)


def _gather_rider(shards, forward_at):
    n = len(shards)

    def copies(ins, outs, sems):
        send_sems, recv_sems = sems
        x, y, c = _position()
        mine = 2 * x + y
        first, passed, arriving = [], [], []
        for k, (cx, cy) in enumerate(_other_chips(x, y)):
            theirs = 2 * cx + cy
            for a in range(n):
                first.append(_remote(ins[a].at[:, c], outs[a].at[:, mine, c], send_sems, recv_sems, k * n + a, (cx, cy, c)))
                block = outs[a].at[:, theirs, c]
                passed.append(_remote(block, block, send_sems, recv_sems, (3 + k) * n + a, (x, y, 1 - c)))
                other = outs[a].at[:, theirs, 1 - c]
                arriving.append(_remote(other, other, send_sems, recv_sems, (3 + k) * n + a, (x, y, 1 - c)))
        return first, passed, arriving

    return _Rider(shards, [jax.ShapeDtypeStruct((s.shape[0], N_CHIPS) + s.shape[1:], s.dtype) for s in shards],
                  [pltpu.SemaphoreType.DMA((6 * n,)), pltpu.SemaphoreType.DMA((6 * n,))], _gather_phases(copies, forward_at))


def _gather_phases(copies, forward_at):
    def begin(ins, outs, sems):
        for cp in copies(ins, outs, sems)[0]:
            cp.start()

    def forward(ins, outs, sems):
        first, passed, _ = copies(ins, outs, sems)
        for got, cp in zip(first, passed):
            got.wait_recv()
            cp.start()

    def finish(ins, outs, sems):
        first, passed, arriving = copies(ins, outs, sems)
        for cp in arriving:
            cp.wait_recv()
        for cp in first + passed:
            cp.wait_send()

    return [(0, begin), (forward_at, forward), (1, finish)]


def _with_own(gathered, shard, slot):
    return lax.dynamic_update_slice(gathered, shard[:, None], (0, slot, 0, 0, 0))


def _exchange_rider(arrays, out_shape, n_copies, transfers, n_local=0):
    def copies(ins, outs, sems):
        send_sems, recv_sems, local_sems = sems
        remote, local = transfers(ins, outs)
        return ([_remote(src, dst, send_sems, recv_sems, i, dev) for i, (src, dst, dev) in enumerate(remote)],
                [pltpu.make_async_copy(src, dst, local_sems.at[i]) for i, (src, dst) in enumerate(local)])

    def begin(ins, outs, sems):
        remote, local = copies(ins, outs, sems)
        for cp in remote + local:
            cp.start()

    def finish(ins, outs, sems):
        remote, local = copies(ins, outs, sems)
        for cp in remote:
            cp.wait_recv()
        for cp in remote:
            cp.wait_send()
        for cp in local:
            cp.wait()

    return _Rider(arrays, out_shape,
                  [pltpu.SemaphoreType.DMA((n_copies,)), pltpu.SemaphoreType.DMA((n_copies,)),
                   pltpu.SemaphoreType.DMA((max(n_local, 1),))],
                  [(0, begin), (1, finish)])


def _pair_rider(partials):
    def transfers(ins, outs):
        x, y, c = _position()
        return [(ins[a].at[:, :, 1 - c], outs[a], (x, y, 1 - c)) for a in range(len(partials))], []

    shapes = [jax.ShapeDtypeStruct(g.shape[:2] + g.shape[3:], g.dtype) for g in partials]
    return _exchange_rider(partials, shapes, len(partials), transfers)


def _chips_rider(sums):
    n = len(sums)

    def transfers(ins, outs):
        x, y, c = _position()
        return [(ins[a].at[:, 2 * cx + cy], outs[a].at[:, k], (cx, cy, c))
                for k, (cx, cy) in enumerate(_other_chips(x, y)) for a in range(n)], []

    shapes = [jax.ShapeDtypeStruct((q.shape[0], 3) + q.shape[2:], q.dtype) for q in sums]
    return _exchange_rider(sums, shapes, 3 * n, transfers)


def _share_rider(halves):
    def transfers(ins, outs):
        x, y, c = _position()
        return [(ins[a], outs[a], (x, y, 1 - c)) for a in range(len(halves))], []

    return _exchange_rider(halves, [jax.ShapeDtypeStruct(h.shape, h.dtype) for h in halves], len(halves), transfers)


def _small_rider(pack):
    flips = [(fx, fy, fc) for fx in (0, 1) for fy in (0, 1) for fc in (0, 1)][1:]

    def transfers(ins, outs):
        x, y, c = _position()
        slot = outs[0].at[4 * x + 2 * y + c]
        flip = lambda v, f: v + f - 2 * v * f
        return [(ins[0], slot, (flip(x, fx), flip(y, fy), flip(c, fc))) for fx, fy, fc in flips], [(ins[0], slot)]

    return _exchange_rider([pack], [jax.ShapeDtypeStruct((8,) + pack.shape, pack.dtype)], len(flips), transfers, n_local=1)


def _alone(rider, name):
    return _hosted([rider], lambda: None, name=name, in_specs=[], out_specs=[], out_shape=[])()[1][0]


def _add_pair(mine, theirs, c, tag):
    n = len(mine)

    def body(c_ref, *refs):
        for a in range(n):
            refs[2 * n + a][...] = (refs[2 * a][...].astype(F32) + refs[2 * a + 1][...].astype(F32)).astype(BF16)

    in_specs, out_specs = [], []
    for got in theirs:
        l, _, hr, cols = got.shape
        in_specs += [pl.BlockSpec((l, None, None, hr, cols), lambda j, s: (0, j, s[0], 0, 0)),
                     pl.BlockSpec((l, None, hr, cols), lambda j, s: (0, j, 0, 0))]
        out_specs.append(pl.BlockSpec((l, None, hr, cols), lambda j, s: (0, j, 0, 0)))
    return pl.pallas_call(
        body, name=f"add_pair_{tag}",
        grid_spec=pltpu.PrefetchScalarGridSpec(num_scalar_prefetch=1, grid=(N_CHIPS,), in_specs=in_specs, out_specs=out_specs),
        out_shape=[jax.ShapeDtypeStruct(got.shape, BF16) for got in theirs],
        compiler_params=_params(("parallel",)),
    )(c.reshape(1), *[a for pair in zip(mine, theirs) for a in pair])


def _add_chips(parts, received, mine, tag):
    n = len(parts)

    def body(j_ref, *refs):
        for a in range(n):
            acc = refs[2 * a][...].astype(F32)
            for k in range(3):
                acc += refs[2 * a + 1][:, k].astype(F32)
            refs[2 * n + a][...] = acc

    in_specs, out_specs, out_shape = [], [], []
    for got in received:
        l, _, hr, cols = got.shape
        in_specs += [pl.BlockSpec((l, None, hr // 2, cols), lambda i, s: (0, s[0], i, 0)),
                     pl.BlockSpec((l, 3, hr // 2, cols), lambda i, s: (0, 0, i, 0))]
        out_specs.append(pl.BlockSpec((l, hr // 2, cols), lambda i, s: (0, i, 0)))
        out_shape.append(jax.ShapeDtypeStruct((l, hr, cols), F32))
    return pl.pallas_call(
        body, name=f"add_chips_{tag}",
        grid_spec=pltpu.PrefetchScalarGridSpec(num_scalar_prefetch=1, grid=(2,), in_specs=in_specs, out_specs=out_specs),
        out_shape=out_shape,
        compiler_params=_params(("parallel",)),
    )(mine.reshape(1), *[a for pair in zip(parts, received) for a in pair])


def _adam_update(w, g, m, v):
    m2 = ADAM_B1 * m + (1.0 - ADAM_B1) * g
    v2 = ADAM_B2 * v + (1.0 - ADAM_B2) * jnp.square(g)
    m_hat = m2 / (1.0 - ADAM_B1 ** ADAM_STEP)
    v_hat = v2 / (1.0 - ADAM_B2 ** ADAM_STEP)
    return -ADAM_LR * (m_hat / (jnp.sqrt(v_hat) + ADAM_EPS) + ADAM_WD * w), m2, v2


def _adamw_group(items, tag, riders=()):
    n = len(items)

    def body(*refs):
        ins, outs = refs[:5 * n], refs[5 * n:]
        mine = pl.program_id(0) == lax.axis_index("c")
        for a in range(n):
            w_ref, own_ref, other_ref, m_ref, v_ref = ins[5 * a:5 * a + 5]
            g_ref, d_ref, nm_ref, nv_ref = outs[4 * a:4 * a + 4]
            gv = jnp.where(mine, own_ref[...], other_ref[...])
            g_ref[...] = gv
            d_ref[...], nm_ref[...], nv_ref[...] = _adam_update(w_ref[...], gv, m_ref[...], v_ref[...])

    in_specs, out_specs, out_shape, args = [], [], [], []
    for w, own, other, m, v in items:
        _, hr, cols = w.shape
        tr = hr // ADAM_BLOCKS
        full = pl.BlockSpec((None, tr, cols), lambda h, i: (h, i, 0))
        half = pl.BlockSpec((tr, cols), lambda h, i: (i, 0))
        in_specs += [full, half, half, full, full]
        out_specs += [full] * 4
        out_shape += [jax.ShapeDtypeStruct((2, hr, cols), F32)] * 4
        args += [w, own, other, m, v]
    outs, extras = _hosted(riders, body, name=f"adamw_{tag}", grid=(2, ADAM_BLOCKS), in_specs=in_specs, out_specs=out_specs,
                           out_shape=out_shape, compiler_params=_params(("arbitrary", "arbitrary")))(*args)
    return [outs[4 * a:4 * a + 4] for a in range(n)], extras


def _adamw_small(w, gathered, m, v, shapes):
    n_rows = w.shape[0]
    places = []
    for i, name in enumerate(VECTOR_PARAMS):
        places.append((name, i * TILE_ROWS, 1 if len(shapes[name]) == 1 else shapes[name][0], shapes[name][-1]))
    places.append(("pool_w", len(VECTOR_PARAMS) * TILE_ROWS, n_rows - len(VECTOR_PARAMS) * TILE_ROWS, D_MODEL))

    def body(w_ref, g_ref, m_ref, v_ref, loss_ref, *rest):
        outs, (sum_scr, d_scr, nm_scr, nv_scr) = rest[:-4], rest[-4:]
        total = g_ref[0]
        for i in range(1, g_ref.shape[0]):
            total += g_ref[i]
        sum_scr[...] = total
        gv = sum_scr[0:n_rows, :]
        d_scr[...], nm_scr[...], nv_scr[...] = _adam_update(w_ref[...], gv, m_ref[...], v_ref[...])
        loss_ref[...] = sum_scr[n_rows:n_rows + 1, 0:1]
        for k, (_, first, rows, cols) in enumerate(places):
            for j, scr in enumerate((sum_scr, d_scr, nm_scr, nv_scr)):
                outs[4 * k + j][...] = scr[first:first + rows, 0:cols]

    out_shape = [jax.ShapeDtypeStruct((1, 1), F32)]
    for _, _, rows, cols in places:
        out_shape += [jax.ShapeDtypeStruct((rows, cols), F32)] * 4
    res = pl.pallas_call(
        body, name="adamw_small", out_shape=out_shape,
        scratch_shapes=[pltpu.VMEM(gathered.shape[1:], F32)] + [pltpu.VMEM(w.shape, F32)] * 3,
        compiler_params=_params())(w, gathered, m, v)
    return res[0], {name: res[1 + 4 * k:5 + 4 * k] for k, (name, _, _, _) in enumerate(places)}


VECTOR_PARAMS = ("ffn1_norm", "mix_norm", "hgrn_lb", "hgrn_onorm", "ffn2_norm", "ple_norm", "ple_post_norm", "final_norm",
                 "pool_scale")
ALL_PARAMS = ("ffn1_norm", "ffn1_w1", "ffn1_w3", "ffn1_w2", "mix_norm", "w_in", "hgrn_lb", "hgrn_onorm", "w_branch_a",
              "pool_w", "pool_scale", "w_branch_b", "w_out", "ffn2_norm", "ffn2_w1", "ffn2_w3", "ffn2_w2", "ple_norm",
              "ple_w_gate", "ple_w_proj", "ple_post_norm", "final_norm")
TILE_ROWS = 8


def _pack_small(values, loss=None):
    tile = lambda a: jnp.pad(a, ((0, TILE_ROWS - a.shape[0]), (0, D_MODEL - a.shape[1])))
    parts = [tile(values[name].reshape(-1, values[name].shape[-1])) for name in VECTOR_PARAMS]
    parts.append(values["pool_w"].reshape(-1, D_MODEL))
    if loss is not None:
        parts.append(tile(loss))
    return jnp.concatenate(parts, axis=0)


def _halved(a, lead):
    return a.reshape(lead, 2, -1, a.shape[-1])


def _shard_halves(a, lead):
    return a.reshape(lead, N_CHIPS, 2, -1, a.shape[-1])


REDUCED_TRANSPOSED = ("ffn1_w1", "ffn1_w3", "ffn2_w1", "ffn2_w3")


def _entries(arrays):
    return [a[i] for a in arrays for i in range(a.shape[0])]


def _adam_items(names, own, other, w, m, v):
    items = []
    for name, g_own, g_other in zip(names, _entries(own), _entries(other)):
        view = (lambda a: _halved(a[0].T, 1)[0]) if name in REDUCED_TRANSPOSED else (lambda a: _halved(a, 1)[0])
        items.append((view(w[name]), g_own, g_other, view(m[name]), view(v[name])))
    return items


def _adam_store(names, results, w, out):
    for name, res in zip(names, results):
        shape = w[name].shape
        if name in REDUCED_TRANSPOSED:
            back = [a.reshape(shape[2], shape[1]).T.reshape(shape) for a in res]
        else:
            back = [a.reshape(shape) for a in res]
        out["grad"][name], out["delta"][name], out["new_m"][name], out["new_v"][name] = back


def kernel(x, p, ffn1_norm, ffn1_w1, ffn1_w3, ffn1_w2, mix_norm, w_in, hgrn_lb, hgrn_onorm, w_branch_a, pool_w, pool_scale, w_branch_b, w_out, ffn2_norm, ffn2_w1, ffn2_w3, ffn2_w2, ple_norm, ple_w_gate, ple_w_proj, ple_post_norm, final_norm, loss_target, m_ffn1_norm, m_ffn1_w1, m_ffn1_w3, m_ffn1_w2, m_mix_norm, m_w_in, m_hgrn_lb, m_hgrn_onorm, m_w_branch_a, m_pool_w, m_pool_scale, m_w_branch_b, m_w_out, m_ffn2_norm, m_ffn2_w1, m_ffn2_w3, m_ffn2_w2, m_ple_norm, m_ple_w_gate, m_ple_w_proj, m_ple_post_norm, m_final_norm, v_ffn1_norm, v_ffn1_w1, v_ffn1_w3, v_ffn1_w2, v_mix_norm, v_w_in, v_hgrn_lb, v_hgrn_onorm, v_w_branch_a, v_pool_w, v_pool_scale, v_w_branch_b, v_w_out, v_ffn2_norm, v_ffn2_w1, v_ffn2_w3, v_ffn2_w2, v_ple_norm, v_ple_w_gate, v_ple_w_proj, v_ple_post_norm, v_final_norm):
    args = dict(locals())
    w = {name: args[name] for name in ALL_PARAMS}
    m = {name: args["m_" + name] for name in ALL_PARAMS}
    v = {name: args["v_" + name] for name in ALL_PARAMS}
    cx, cy, cc = _position()
    chip = (2 * cx + cy).astype(jnp.int32)
    core = cc.astype(jnp.int32)
    xs, ps, target = x[0], p[0, 0], loss_target[0]
    t = xs.shape[0]
    tm = min(256, t)
    tm_ffn = min(512, t)
    tt = min(512, t)
    tk = min(2048, t)
    small = {name: w[name] for name in VECTOR_PARAMS}
    small["final_norm"] = w["final_norm"].reshape(1, D_MODEL)
    pool_w0 = w["pool_w"][0]

    ffn_shard = lambda i: _halved(jnp.stack([w[f"ffn{i}_w1"][0].T, w[f"ffn{i}_w3"][0].T, w[f"ffn{i}_w2"][0]]).astype(BF16), 3)
    sq_shard = _halved(jnp.stack([w["w_branch_a"][0], w["w_out"][0], w["ple_w_gate"][0]]).astype(BF16), 3)
    win_shard, wb_shard, wp_shard = (_halved(w[n].astype(BF16), 1) for n in ("w_in", "w_branch_b", "ple_w_proj"))

    ffn1_shard, ffn2_shard = ffn_shard(1), ffn_shard(2)
    (ffn1w,) = _alone(_gather_rider([ffn1_shard], 0.5), "gather_ffn1")
    ffn1w = _with_own(ffn1w, ffn1_shard, chip).reshape(3, D_FF, D_MODEL)
    (x1, a1, b1), ((winw,),) = _ffn_fwd(xs, small["ffn1_norm"], ffn1w, 1, tm_ffn, [_gather_rider([win_shard], 0.6)])
    winw = _with_own(winw, win_shard, chip).reshape(N_CHIPS, D_MODEL, SHARD_IN_COLS)
    (main, pool_r, gates), ((sqw, wbw, wpw),) = _mix_fwd(x1, small["mix_norm"], winw, tm,
                                                          [_gather_rider([sq_shard, wb_shard, wp_shard], 0.5)])
    sqw = _with_own(sqw, sq_shard, chip).reshape(3, D_MODEL, D_MODEL)
    wbw = _with_own(wbw, wb_shard, chip).reshape(N_CHIPS, POOL_WIDTH, -1)
    wpw = _with_own(wpw, wp_shard, chip).reshape(N_CHIPS, ps.shape[1], -1)
    (o, states), ((ffn2w,),) = _hgrn_fwd(main, small["hgrn_lb"], tt, [_gather_rider([ffn2_shard], 0.7)])
    ffn2w = _with_own(ffn2w, ffn2_shard, chip).reshape(3, D_FF, D_MODEL)
    (x2, ya, yb, pooled), _ = _post_fwd(o, main, pool_r, gates, x1, small["hgrn_onorm"], pool_w0, small["pool_scale"], sqw,
                                       wbw, tm)
    (x3, a2, b2), _ = _ffn_fwd(x2, small["ffn2_norm"], ffn2w, 2, tm_ffn)
    (dx3, loss, d_wg, d_wp, d_ple, d_post, d_final), _ = _tail(
        x3, ps, target, small["ple_norm"], small["ple_post_norm"], small["final_norm"], sqw, wpw, tm_ffn)

    add_pairs = lambda parts, got, group: _add_pair(parts, got, core, group)
    add_chips = lambda sums, got, group: _add_chips(sums, got, chip, group)
    names1 = ("ffn2_w1", "ffn2_w3", "ffn2_w2", "ple_w_gate", "ple_w_proj")
    names2 = ("w_branch_a", "w_out", "w_branch_b")
    names3 = ("w_in",)
    names4 = ("ffn1_w1", "ffn1_w3")
    names5 = ("ffn1_w2",)
    tags1, tags2, tags3, tags4, tags5 = "ffn2", "branches", "w_in", "ffn1_in", "ffn1_out"

    (dx2, dab2, s2, h3, dxh2, d_ffn2_norm), _ = _ffn_bwd(dx3, x2, small["ffn2_norm"], a2, b2, ffn2w, 2, tm)
    (d_w13_2,), _ = _wgrad(dab2, h3, WGRAD_IN_BLOCKS, "wgrad_ffn2_in", tk)
    (d_w2_2,), _ = _wgrad(s2, dxh2, WGRAD_OUT_BLOCKS, "wgrad_ffn2_out", tk)
    part1 = [_shard_halves(d_w13_2, 2), _shard_halves(d_w2_2, 1), _shard_halves(d_wg, 1), _shard_halves(d_wp, 1)]
    (do, dog, du, dgates, d_wa, d_wout, d_wb, d_pool_w, d_pool_scale, d_onorm), (sib1,) = _post_bwd(
        dx2, o, main, gates, ya, yb, pooled, small["hgrn_onorm"], pool_w0, small["pool_scale"], sqw, wbw, tm,
        [_pair_rider(part1)])
    sums1 = add_pairs(part1, sib1, tags1)
    part2 = [_shard_halves(d_wa, 1), _shard_halves(d_wout, 1), _shard_halves(d_wb, 1)]
    (dqfi, d_lb), (got1, sib2) = _hgrn_bwd(main, small["hgrn_lb"], states, do, tt, [_chips_rider(sums1), _pair_rider(part2)])
    own1 = add_chips(sums1, got1, tags1)
    sums2 = add_pairs(part2, sib2, tags2)
    (dx1, dproj, h2, d_mix_norm), (other1, got2) = _mix_bwd(dqfi, dog, du, dgates, dx2, x1, small["mix_norm"], winw, tm,
                                                            [_share_rider(own1), _chips_rider(sums2)])
    own2 = add_chips(sums2, got2, tags2)
    (d_win,), (other2,) = _wgrad_cols(h2, dproj, N_CHIPS, "wgrad_in", tk, [_share_rider(own2)])
    part3 = [_shard_halves(d_win, 1)]
    (dx, dab1, s1, h1, dxh1, d_ffn1_norm), _ = _ffn_bwd(dx1, xs, small["ffn1_norm"], a1, b1, ffn1w, 1, tm)
    vecs = dict(ffn1_norm=d_ffn1_norm, mix_norm=d_mix_norm, hgrn_lb=d_lb, hgrn_onorm=d_onorm, ffn2_norm=d_ffn2_norm,
                ple_norm=d_ple, ple_post_norm=d_post, final_norm=d_final, pool_scale=d_pool_scale, pool_w=d_pool_w)
    (d_w2_1,), (sib3, (small_all,)) = _wgrad(s1, dxh1, WGRAD_OUT_BLOCKS, "wgrad_ffn1_out", tk,
                                             [_pair_rider(part3), _small_rider(_pack_small(vecs, loss))])
    sums3 = add_pairs(part3, sib3, tags3)
    part5 = [_shard_halves(d_w2_1, 1)]
    (d_w13_1,), (got3, sib5) = _wgrad(dab1, h1, WGRAD_IN_BLOCKS, "wgrad_ffn1_in", tk,
                                      [_chips_rider(sums3), _pair_rider(part5)])
    own3 = add_chips(sums3, got3, tags3)
    sums5 = add_pairs(part5, sib5, tags5)
    part4 = [_shard_halves(d_w13_1, 2)]
    sib4 = _alone(_pair_rider(part4), "pair_last")
    sums4 = add_pairs(part4, sib4, tags4)

    out = dict(grad={}, delta={}, new_m={}, new_v={})
    results, (other3, got4) = _adamw_group(_adam_items(names1 + names2, own1 + own2, other1 + other2, w, m, v), "early",
                                           [_share_rider(own3), _chips_rider(sums4)])
    _adam_store(names1 + names2, results, w, out)
    own4 = add_chips(sums4, got4, tags4)
    results, (other4, got5) = _adamw_group(_adam_items(names3, own3, other3, w, m, v), "w_in",
                                           [_share_rider(own4), _chips_rider(sums5)])
    _adam_store(names3, results, w, out)
    own5 = add_chips(sums5, got5, tags5)
    other5 = _alone(_share_rider(own5), "share_last")
    results, _ = _adamw_group(_adam_items(names4 + names5, own4 + own5, other4 + other5, w, m, v), "ffn1")
    _adam_store(names4 + names5, results, w, out)

    shapes = {name: w[name].shape for name in VECTOR_PARAMS + ("pool_w",)}
    loss, results = _adamw_small(_pack_small(w), small_all, _pack_small(m), _pack_small(v), shapes)
    for name, res in results.items():
        out["grad"][name], out["delta"][name], out["new_m"][name], out["new_v"][name] = (a.reshape(shapes[name]) for a in res)

    return (loss[0, 0], dx[None], *[out["grad"][n] for n in ALL_PARAMS], *[out["delta"][n] for n in ALL_PARAMS],
            *[out["new_m"][n] for n in ALL_PARAMS], *[out["new_v"][n] for n in ALL_PARAMS])
```

```python
import functools

import jax
import jax.numpy as jnp
from jax import lax
from jax.experimental import pallas as pl
from jax.experimental.pallas import tpu as pltpu

F32 = jnp.float32
BF16 = jnp.bfloat16
MESH = pl.DeviceIdType.MESH

D_MODEL = 1024
D_FF = 2816
HEADS = 8
HEAD_DIM = 128
POOL_WIDTH = 512
POOL_WINDOWS = (2, 4, 8, 16)
POOL_HALO = 16
N_CHIPS = 4
EPS = 1e-6
CHUNK = 64
MAIN_COLS = 4096
GATE_COLS = 2048
SHARD_IN_COLS = 1664

ADAM_LR = 0.001
ADAM_B1 = 0.9
ADAM_B2 = 0.999
ADAM_EPS = 1e-08
ADAM_WD = 0.01
ADAM_STEP = 10

VMEM_LIMIT = 56 * 1024 * 1024
WGRAD_IN_BLOCKS = 4
WGRAD_OUT_BLOCKS = 2
ADAM_BLOCKS = 4


def _params(semantics=None, vmem=VMEM_LIMIT):
    return pltpu.CompilerParams(dimension_semantics=semantics, vmem_limit_bytes=vmem)


def _dot(a, b):
    return jnp.dot(a, b, preferred_element_type=F32)


def _dot_nt(a, b):
    return lax.dot_general(a, b, (((1,), (1,)), ((), ())), preferred_element_type=F32)


def _dot_tn(a, b):
    return lax.dot_general(a, b, (((0,), (0,)), ((), ())), preferred_element_type=F32)


def _tri_sum(tri, x):
    hi = x.astype(BF16)
    lo = (x - hi.astype(F32)).astype(BF16)
    return _dot(tri, hi) + _dot(tri, lo)


def _sigmoid(x):
    return jax.nn.sigmoid(x)


def _resident(shape):
    zeros = (0,) * len(shape)
    return pl.BlockSpec(shape, lambda *_: zeros, pipeline_mode=pl.Buffered(1))


def _pick(shape, k):
    zeros = (0,) * (len(shape) - 1)
    return pl.BlockSpec((None,) + tuple(shape[1:]), lambda *_: (k,) + zeros, pipeline_mode=pl.Buffered(1))


def _rows(tm, cols, col_block=0):
    return pl.BlockSpec((tm, cols), lambda i: (i, col_block))


def _acc(shape):
    zeros = (0,) * len(shape)
    return pl.BlockSpec(shape, lambda *_: zeros)


def _rms(x):
    r = lax.rsqrt(jnp.mean(x * x, axis=-1, keepdims=True) + EPS)
    return r, x * r


def _rms_bwd(dn, n, r):
    return r * (dn - n * jnp.mean(dn * n, axis=-1, keepdims=True))


def _colsum(a):
    return jnp.sum(a, axis=0, keepdims=True)


ANY = pl.BlockSpec(memory_space=pl.ANY)


class _Rider:
    def __init__(self, inputs, out_shape, sems, phases):
        self.inputs, self.out_shape, self.sems, self.phases = list(inputs), list(out_shape), list(sems), list(phases)


def _hosted(riders, body, *, name, grid=(), in_specs, out_specs, out_shape, scratch_shapes=(), compiler_params=None):
    riders = [r for r in riders if r is not None]
    n_in, n_out, n_scr = len(in_specs), len(out_shape), len(scratch_shapes)
    n_steps = 1
    for g in grid:
        n_steps *= g

    def wrapped(*refs):
        pos = n_in
        ins = refs[:n_in]
        r_ins = []
        for r in riders:
            r_ins.append(refs[pos:pos + len(r.inputs)])
            pos += len(r.inputs)
        outs = refs[pos:pos + n_out]
        pos += n_out
        r_outs = []
        for r in riders:
            r_outs.append(refs[pos:pos + len(r.out_shape)])
            pos += len(r.out_shape)
        scr = refs[pos:pos + n_scr]
        pos += n_scr
        r_sems = []
        for r in riders:
            r_sems.append(refs[pos:pos + len(r.sems)])
            pos += len(r.sems)
        step = 0
        for axis in range(len(grid)):
            step = step * grid[axis] + pl.program_id(axis)

        def at_step(which, fn):
            if n_steps == 1:
                fn()
            else:
                pl.when(step == which)(fn)

        for r, ri, ro, rs in zip(riders, r_ins, r_outs, r_sems):
            for fraction, fn in r.phases:
                if fraction == 0:
                    at_step(0, functools.partial(fn, ri, ro, rs))
        body(*ins, *outs, *scr)
        for r, ri, ro, rs in zip(riders, r_ins, r_outs, r_sems):
            for fraction, fn in r.phases:
                if fraction > 0:
                    at_step(min(int(fraction * n_steps), n_steps - 1), functools.partial(fn, ri, ro, rs))

    call = pl.pallas_call(
        wrapped, name=name, grid=grid,
        in_specs=list(in_specs) + [ANY for r in riders for _ in r.inputs],
        out_specs=list(out_specs) + [ANY for r in riders for _ in r.out_shape],
        out_shape=list(out_shape) + [s for r in riders for s in r.out_shape],
        scratch_shapes=list(scratch_shapes) + [s for r in riders for s in r.sems],
        compiler_params=compiler_params)

    def run(*args):
        res = call(*args, *[a for r in riders for a in r.inputs])
        extras, pos = [], n_out
        for r in riders:
            extras.append(list(res[pos:pos + len(r.out_shape)]))
            pos += len(r.out_shape)
        return list(res[:n_out]), extras

    return run


def _ffn_fwd(x, g, ffnw, tag, tm, riders=()):
    t = x.shape[0]

    def body(x_ref, g_ref, w1_ref, w3_ref, w2_ref, xo_ref, a_ref, b_ref):
        xv = x_ref[...]
        _, n = _rms(xv)
        h = (n * g_ref[...]).astype(BF16)
        a = _dot_nt(h, w1_ref[...])
        b = _dot_nt(h, w3_ref[...])
        s = (a * _sigmoid(a) * b).astype(BF16)
        xo_ref[...] = xv + 0.5 * _dot(s, w2_ref[...])
        a_ref[...] = a.astype(BF16)
        b_ref[...] = b.astype(BF16)

    return _hosted(
        riders, body, name=f"ffn_fwd_{tag}", grid=(t // tm,),
        in_specs=[_rows(tm, D_MODEL), _resident((1, D_MODEL)), _pick(ffnw.shape, 0), _pick(ffnw.shape, 1),
                  _pick(ffnw.shape, 2)],
        out_specs=[_rows(tm, D_MODEL), _rows(tm, D_FF), _rows(tm, D_FF)],
        out_shape=[jax.ShapeDtypeStruct((t, D_MODEL), F32), jax.ShapeDtypeStruct((t, D_FF), BF16),
                   jax.ShapeDtypeStruct((t, D_FF), BF16)],
        compiler_params=_params(("arbitrary",)),
    )(x, g, ffnw, ffnw, ffnw)


def _ffn_bwd(dxo, x, g, a, b, ffnw, tag, tm, riders=()):
    t = x.shape[0]

    def body(dxo_ref, x_ref, g_ref, a_ref, b_ref, w1_ref, w3_ref, w2_ref, dx_ref, dab_ref, s_ref, h_ref, dxh_ref, dg_ref):
        @pl.when(pl.program_id(0) == 0)
        def _():
            dg_ref[...] = jnp.zeros_like(dg_ref)

        xv = x_ref[...]
        gv = g_ref[...]
        r, n = _rms(xv)
        h_ref[...] = (n * gv).astype(BF16)
        dxo_v = dxo_ref[...]
        dxh = (0.5 * dxo_v).astype(BF16)
        dxh_ref[...] = dxh
        ds = _dot_nt(dxh, w2_ref[...])
        av = a_ref[...].astype(F32)
        bv = b_ref[...].astype(F32)
        sg = _sigmoid(av)
        silu = av * sg
        s_ref[...] = (silu * bv).astype(BF16)
        da = (ds * bv * (sg * (1.0 + av * (1.0 - sg)))).astype(BF16)
        db = (ds * silu).astype(BF16)
        dab_ref[:, :D_FF] = da
        dab_ref[:, D_FF:] = db
        dh = _dot(da, w1_ref[...]) + _dot(db, w3_ref[...])
        dg_ref[...] += _colsum(dh * n)
        dx_ref[...] = dxo_v + _rms_bwd(dh * gv, n, r)

    return _hosted(
        riders, body, name=f"ffn_bwd_{tag}", grid=(t // tm,),
        in_specs=[_rows(tm, D_MODEL), _rows(tm, D_MODEL), _resident((1, D_MODEL)), _rows(tm, D_FF), _rows(tm, D_FF),
                  _pick(ffnw.shape, 0), _pick(ffnw.shape, 1), _pick(ffnw.shape, 2)],
        out_specs=[_rows(tm, D_MODEL), _rows(tm, 2 * D_FF), _rows(tm, D_FF), _rows(tm, D_MODEL), _rows(tm, D_MODEL),
                   _acc((1, D_MODEL))],
        out_shape=[jax.ShapeDtypeStruct((t, D_MODEL), F32), jax.ShapeDtypeStruct((t, 2 * D_FF), BF16),
                   jax.ShapeDtypeStruct((t, D_FF), BF16), jax.ShapeDtypeStruct((t, D_MODEL), BF16),
                   jax.ShapeDtypeStruct((t, D_MODEL), BF16), jax.ShapeDtypeStruct((1, D_MODEL), F32)],
        compiler_params=_params(("arbitrary",)),
    )(dxo, x, g, a, b, ffnw, ffnw, ffnw)


def _wgrad_body(n_token_tiles):
    def body(x_ref, dy_ref, o_ref, acc):
        k = pl.program_id(1)

        @pl.when(k == 0)
        def _():
            acc[...] = jnp.zeros_like(acc)

        acc[...] += _dot_tn(x_ref[...], dy_ref[...])

        @pl.when(k == n_token_tiles - 1)
        def _():
            o_ref[...] = acc[...].astype(BF16)

    return body


def _wgrad(xm, dy, out_blocks, name, tk, riders=()):
    t, m = xm.shape
    n = dy.shape[1]
    mb = m // out_blocks

    return _hosted(
        riders, _wgrad_body(t // tk), name=name, grid=(out_blocks, t // tk),
        in_specs=[pl.BlockSpec((tk, mb), lambda j, k: (k, j)), pl.BlockSpec((tk, n), lambda j, k: (k, 0))],
        out_specs=[pl.BlockSpec((None, mb, n), lambda j, k: (j, 0, 0))],
        out_shape=[jax.ShapeDtypeStruct((out_blocks, mb, n), BF16)],
        scratch_shapes=[pltpu.VMEM((mb, n), F32)],
        compiler_params=_params(("arbitrary", "arbitrary")),
    )(xm, dy)


def _wgrad_cols(xm, dy, out_blocks, name, tk, riders=()):
    t, m = xm.shape
    n = dy.shape[1]
    nb = n // out_blocks

    return _hosted(
        riders, _wgrad_body(t // tk), name=name, grid=(out_blocks, t // tk),
        in_specs=[pl.BlockSpec((tk, m), lambda j, k: (k, 0)), pl.BlockSpec((tk, nb), lambda j, k: (k, j))],
        out_specs=[pl.BlockSpec((None, m, nb), lambda j, k: (j, 0, 0))],
        out_shape=[jax.ShapeDtypeStruct((out_blocks, m, nb), BF16)],
        scratch_shapes=[pltpu.VMEM((m, nb), F32)],
        compiler_params=_params(("arbitrary", "arbitrary")),
    )(xm, dy)


def _mix_fwd(x1, g, winw, tm, riders=()):
    t = x1.shape[0]

    def body(x_ref, g_ref, w_ref, main_ref, pool_ref, gate_ref):
        _, n = _rms(x_ref[...])
        h = (n * g_ref[...]).astype(BF16)
        proj = jnp.concatenate([_dot(h, w_ref[j]) for j in range(N_CHIPS)], axis=1)
        main_ref[...] = proj[:, :MAIN_COLS]
        pool_ref[...] = proj[:, MAIN_COLS:MAIN_COLS + POOL_WIDTH]
        gate_ref[...] = proj[:, MAIN_COLS + POOL_WIDTH:]

    return _hosted(
        riders, body, name="mix_fwd", grid=(t // tm,),
        in_specs=[_rows(tm, D_MODEL), _resident((1, D_MODEL)), _resident(winw.shape)],
        out_specs=[_rows(tm, MAIN_COLS), _rows(tm, POOL_WIDTH), _rows(tm, GATE_COLS)],
        out_shape=[jax.ShapeDtypeStruct((t, MAIN_COLS), F32), jax.ShapeDtypeStruct((t, POOL_WIDTH), F32),
                   jax.ShapeDtypeStruct((t, GATE_COLS), F32)],
        compiler_params=_params(("arbitrary",)),
    )(x1, g, winw)


def _mix_bwd(dqfi, dog, du, dgates, dx2, x1, g, winw, tm, riders=()):
    t = x1.shape[0]
    cols = N_CHIPS * SHARD_IN_COLS

    def body(dqfi_ref, dog_ref, du_ref, dgt_ref, dx2_ref, x_ref, g_ref, w_ref, dx_ref, dproj_ref, h_ref, dg_ref):
        @pl.when(pl.program_id(0) == 0)
        def _():
            dg_ref[...] = jnp.zeros_like(dg_ref)

        dproj = jnp.concatenate([dqfi_ref[...], dog_ref[...], du_ref[...], dgt_ref[...]], axis=1)
        dproj_ref[...] = dproj
        dh = _dot_nt(dproj[:, :SHARD_IN_COLS], w_ref[0])
        for j in range(1, N_CHIPS):
            dh += _dot_nt(dproj[:, j * SHARD_IN_COLS:(j + 1) * SHARD_IN_COLS], w_ref[j])
        gv = g_ref[...]
        r, n = _rms(x_ref[...])
        h_ref[...] = (n * gv).astype(BF16)
        dg_ref[...] += _colsum(dh * n)
        dx_ref[...] = dx2_ref[...] + _rms_bwd(dh * gv, n, r)

    return _hosted(
        riders, body, name="mix_bwd", grid=(t // tm,),
        in_specs=[_rows(tm, 3 * D_MODEL), _rows(tm, D_MODEL), _rows(tm, POOL_WIDTH), _rows(tm, GATE_COLS),
                  _rows(tm, D_MODEL), _rows(tm, D_MODEL), _resident((1, D_MODEL)), _resident(winw.shape)],
        out_specs=[_rows(tm, D_MODEL), _rows(tm, cols), _rows(tm, D_MODEL), _acc((1, D_MODEL))],
        out_shape=[jax.ShapeDtypeStruct((t, D_MODEL), F32), jax.ShapeDtypeStruct((t, cols), BF16),
                   jax.ShapeDtypeStruct((t, D_MODEL), BF16), jax.ShapeDtypeStruct((1, D_MODEL), F32)],
        compiler_params=_params(("arbitrary",)),
    )(dqfi, dog, du, dgates, dx2, x1, g, winw)


def _lower_bound(lb_raw):
    l0 = lb_raw[0:1, :]
    l1 = lb_raw[1:2, :]
    m = jnp.maximum(l0, l1)
    e0 = jnp.exp(l0 - m)
    e1 = jnp.exp(l1 - m)
    return e0 / (e0 + e1)


def _head_slices():
    return [slice(h * HEAD_DIM, (h + 1) * HEAD_DIM) for h in range(HEADS)]


def _gates(qr, fr, lb, tril_b, first_half):
    sg = _sigmoid(fr)
    f = lb + (1.0 - lb) * sg
    k = 1.0 - f
    sq = _sigmoid(qr)
    q = qr * sq
    log_f = jnp.log(f)
    gc = _tri_sum(tril_b, log_f)
    gm = _colsum(jnp.where(first_half, log_f, 0.0))
    gl = _colsum(log_f)
    e_q = jnp.exp(gc - gm)
    e_k = jnp.exp(gm - gc)
    e_in = jnp.exp(gc)
    e_out = jnp.exp(gl - gc)
    return dict(sg=sg, f=f, k=k, sq=sq, q=q, e_q=e_q, e_k=e_k, e_in=e_in, e_out=e_out, e_last=jnp.exp(gl))


def _hgrn_fwd(main, lb_raw, tt, riders=()):
    t = main.shape[0]
    n_local = tt // CHUNK

    def body(q_ref, f_ref, i_ref, lb_ref, o_ref, st_ref, s_scr):
        @pl.when(pl.program_id(0) == 0)
        def _():
            s_scr[...] = jnp.zeros_like(s_scr)

        lb = _lower_bound(lb_ref[...])
        row = lax.broadcasted_iota(jnp.int32, (CHUNK, CHUNK), 0)
        col = lax.broadcasted_iota(jnp.int32, (CHUNK, CHUNK), 1)
        tril = row >= col
        tril_b = tril.astype(BF16)
        first_half = lax.broadcasted_iota(jnp.int32, (CHUNK, D_MODEL), 0) < CHUNK // 2
        heads = _head_slices()

        def chunk(c, carry):
            rows = pl.ds(pl.multiple_of(c * CHUNK, CHUNK), CHUNK)
            z = _gates(q_ref[rows, :], f_ref[rows, :], lb, tril_b, first_half)
            qt = (z["q"] * z["e_q"]).astype(BF16)
            kt = (z["k"] * z["e_k"]).astype(BF16)
            qg = (z["q"] * z["e_in"]).astype(BF16)
            kg = (z["k"] * z["e_out"]).astype(BF16)
            vb = i_ref[rows, :].astype(BF16)
            states = [s_scr[h] for h in range(HEADS)]
            for h in range(HEADS):
                st_ref[c, h] = states[h]
            raw = [_dot_nt(qt[:, sl], kt[:, sl]) for sl in heads]
            inter = [_dot_nt(qg[:, sl], states[h].astype(BF16)) for h, sl in enumerate(heads)]
            grown = [_dot_tn(vb[:, sl], kg[:, sl]) for sl in heads]
            scores = [jnp.where(tril, r, 0.0).astype(BF16) for r in raw]
            for h, sl in enumerate(heads):
                s_scr[h] = states[h] * z["e_last"][:, sl] + grown[h]
            o_ref[rows, :] = jnp.concatenate([_dot(scores[h], vb[:, sl]) + inter[h] for h, sl in enumerate(heads)], axis=1)
            return carry

        lax.fori_loop(0, n_local, chunk, 0, unroll=True)

    return _hosted(
        riders, body, name="hgrn_fwd", grid=(t // tt,),
        in_specs=[_rows(tt, D_MODEL, 0), _rows(tt, D_MODEL, 1), _rows(tt, D_MODEL, 2), _resident((2, D_MODEL))],
        out_specs=[_rows(tt, D_MODEL),
                   pl.BlockSpec((n_local, HEADS, HEAD_DIM, HEAD_DIM), lambda i: (i, 0, 0, 0))],
        out_shape=[jax.ShapeDtypeStruct((t, D_MODEL), F32),
                   jax.ShapeDtypeStruct((t // CHUNK, HEADS, HEAD_DIM, HEAD_DIM), F32)],
        scratch_shapes=[pltpu.VMEM((HEADS, HEAD_DIM, HEAD_DIM), F32)],
        compiler_params=_params(("arbitrary",)),
    )(main, main, main, lb_raw)


def _hgrn_bwd(main, lb_raw, states, do, tt, riders=()):
    t = main.shape[0]
    n_tiles = t // tt
    n_local = tt // CHUNK

    def rev(col_block):
        return pl.BlockSpec((tt, D_MODEL), lambda i: (n_tiles - 1 - i, col_block))

    def body(q_ref, f_ref, i_ref, lb_ref, st_ref, do_ref, dqfi_ref, dlb_ref, ds_scr, acc_scr):
        @pl.when(pl.program_id(0) == 0)
        def _():
            ds_scr[...] = jnp.zeros_like(ds_scr)
            acc_scr[...] = jnp.zeros_like(acc_scr)

        lb = _lower_bound(lb_ref[...])
        row = lax.broadcasted_iota(jnp.int32, (CHUNK, CHUNK), 0)
        col = lax.broadcasted_iota(jnp.int32, (CHUNK, CHUNK), 1)
        tril = row >= col
        tril_b = tril.astype(BF16)
        triu_b = (row <= col).astype(BF16)
        first_half = lax.broadcasted_iota(jnp.int32, (CHUNK, D_MODEL), 0) < CHUNK // 2
        heads = _head_slices()
        cat = functools.partial(jnp.concatenate, axis=1)

        def chunk(cc, carry):
            c = n_local - 1 - cc
            rows = pl.ds(pl.multiple_of(c * CHUNK, CHUNK), CHUNK)
            qr = q_ref[rows, :]
            z = _gates(qr, f_ref[rows, :], lb, tril_b, first_half)
            qt = (z["q"] * z["e_q"]).astype(BF16)
            kt = (z["k"] * z["e_k"]).astype(BF16)
            qg_f = z["q"] * z["e_in"]
            qg = qg_f.astype(BF16)
            kg_f = z["k"] * z["e_out"]
            kg = kg_f.astype(BF16)
            vb = i_ref[rows, :].astype(BF16)
            dob = do_ref[rows, :].astype(BF16)
            st = [st_ref[c, h] for h in range(HEADS)]
            dst = [ds_scr[h] for h in range(HEADS)]
            dst_b = [d.astype(BF16) for d in dst]
            raw = [_dot_nt(qt[:, sl], kt[:, sl]) for sl in heads]
            draw = [_dot_nt(dob[:, sl], vb[:, sl]) for sl in heads]
            dqg = [_dot(dob[:, sl], st[h].astype(BF16)) for h, sl in enumerate(heads)]
            dkg = [_dot(vb[:, sl], dst_b[h]) for h, sl in enumerate(heads)]
            dv_inter = [_dot_nt(kg[:, sl], dst_b[h]) for h, sl in enumerate(heads)]
            grown = [_dot_tn(dob[:, sl], qg[:, sl]) for sl in heads]
            scores = [jnp.where(tril, r, 0.0).astype(BF16) for r in raw]
            dscores = [jnp.where(tril, r, 0.0).astype(BF16) for r in draw]
            dqt = [_dot(dscores[h], kt[:, sl]) for h, sl in enumerate(heads)]
            dkt = [_dot_tn(dscores[h], qt[:, sl]) for h, sl in enumerate(heads)]
            dv = [_dot_tn(scores[h], dob[:, sl]) + dv_inter[h] for h, sl in enumerate(heads)]
            carry_in = cat([z["e_last"][:, sl] * _colsum(dst[h] * st[h]) for h, sl in enumerate(heads)])
            for h, sl in enumerate(heads):
                ds_scr[h] = dst[h] * z["e_last"][:, sl] + grown[h]
            dqt, dkt, dqg, dkg = cat(dqt), cat(dkt), cat(dqg), cat(dkg)
            carry_in += _colsum(dkg * kg_f)
            dq = dqt * z["e_q"] + dqg * z["e_in"]
            dk = dkt * z["e_k"] + dkg * z["e_out"]
            dgate = (qt.astype(F32) * dqt - kt.astype(F32) * dkt) + (qg_f * dqg - kg_f * dkg)
            dlogf = _tri_sum(triu_b, dgate) + carry_in
            df = dlogf / z["f"] - dk
            sg = z["sg"]
            sq = z["sq"]
            acc_scr[...] += _colsum(df * (1.0 - sg))
            dqfi_ref[rows, 0:D_MODEL] = (dq * (sq * (1.0 + qr * (1.0 - sq)))).astype(BF16)
            dqfi_ref[rows, D_MODEL:2 * D_MODEL] = (df * (1.0 - lb) * sg * (1.0 - sg)).astype(BF16)
            dqfi_ref[rows, 2 * D_MODEL:3 * D_MODEL] = cat(dv).astype(BF16)
            return carry

        lax.fori_loop(0, n_local, chunk, 0, unroll=True)
        d0 = acc_scr[...] * lb * (1.0 - lb)
        dlb_ref[0:1, :] = d0
        dlb_ref[1:2, :] = -d0

    return _hosted(
        riders, body, name="hgrn_bwd", grid=(n_tiles,),
        in_specs=[rev(0), rev(1), rev(2), _resident((2, D_MODEL)),
                  pl.BlockSpec((n_local, HEADS, HEAD_DIM, HEAD_DIM), lambda i: (n_tiles - 1 - i, 0, 0, 0)),
                  rev(0)],
        out_specs=[pl.BlockSpec((tt, 3 * D_MODEL), lambda i: (n_tiles - 1 - i, 0)), _acc((2, D_MODEL))],
        out_shape=[jax.ShapeDtypeStruct((t, 3 * D_MODEL), BF16), jax.ShapeDtypeStruct((2, D_MODEL), F32)],
        scratch_shapes=[pltpu.VMEM((HEADS, HEAD_DIM, HEAD_DIM), F32), pltpu.VMEM((1, D_MODEL), F32)],
        compiler_params=_params(("arbitrary",)),
    )(main, main, main, lb_raw, states, do)


def _head_norm(o):
    rs, ns = [], []
    for h in range(HEADS):
        oh = o[:, h * HEAD_DIM:(h + 1) * HEAD_DIM]
        r, n = _rms(oh)
        rs.append(jnp.broadcast_to(r, oh.shape))
        ns.append(n)
    return jnp.concatenate(rs, axis=1), jnp.concatenate(ns, axis=1)


def _head_norm_bwd(dn, n, r):
    outs = []
    for h in range(HEADS):
        sl = slice(h * HEAD_DIM, (h + 1) * HEAD_DIM)
        outs.append(_rms_bwd(dn[:, sl], n[:, sl], r[:, sl]))
    return jnp.concatenate(outs, axis=1)


def _window_counts(first_row, tm):
    pos = (first_row + 1 + lax.broadcasted_iota(jnp.int32, (tm, 1), 0)).astype(F32)
    return [jnp.minimum(pos, float(w)) for w in POOL_WINDOWS]


def _post_fwd(o, main, pool_r, gates, x1, onorm, pool_w, pool_scale, sqw, wbw, tm, riders=()):
    t = o.shape[0]
    ext_rows = tm + POOL_HALO

    def body(o_ref, og_ref, u_ref, gt_ref, x1_ref, on_ref, pw_ref, ps_ref, wa_ref, wout_ref, wb_ref,
             x2_ref, ya_ref, yb_ref, pooled_ref, ext):
        i = pl.program_id(0)

        @pl.when(i == 0)
        def _():
            ext[0:POOL_HALO, :] = jnp.zeros((POOL_HALO, POOL_WIDTH), F32)

        _, n = _head_norm(o_ref[...])
        og = og_ref[...]
        oa = (n * on_ref[...] * (og * _sigmoid(og))).astype(BF16)
        ya = _dot(oa, wa_ref[...])

        u = u_ref[...]
        ext[POOL_HALO:ext_rows, :] = u
        e = ext[...]
        counts = _window_counts(i * tm, tm)
        pooled = []
        for gidx, w in enumerate(POOL_WINDOWS):
            s = e[:, gidx * HEAD_DIM:(gidx + 1) * HEAD_DIM]
            shift = 1
            while shift < w:
                s = s + pltpu.roll(s, shift, axis=0)
                shift *= 2
            pooled.append(s[POOL_HALO:, :] / counts[gidx] - u[:, gidx * HEAD_DIM:(gidx + 1) * HEAD_DIM])
        ext[0:POOL_HALO, :] = ext[tm:ext_rows, :]
        pooled_b = [pg.astype(BF16) for pg in pooled]
        pooled_ref[...] = jnp.concatenate(pooled_b, axis=1)
        mixed = jnp.concatenate([_dot(pooled_b[gidx], pw_ref[gidx].astype(BF16)) for gidx in range(len(POOL_WINDOWS))],
                                axis=1) * ps_ref[...]
        mixed_b = mixed.astype(BF16)
        yb = jnp.concatenate([_dot(mixed_b, wb_ref[j]) for j in range(N_CHIPS)], axis=1)

        gt = gt_ref[...]
        y = _sigmoid(gt[:, :D_MODEL]) * ya + _sigmoid(gt[:, D_MODEL:]) * yb
        x2_ref[...] = x1_ref[...] + _dot(y.astype(BF16), wout_ref[...])
        ya_ref[...] = ya.astype(BF16)
        yb_ref[...] = yb.astype(BF16)

    return _hosted(
        riders, body, name="post_fwd", grid=(t // tm,),
        in_specs=[_rows(tm, D_MODEL), _rows(tm, D_MODEL, 3), _rows(tm, POOL_WIDTH), _rows(tm, GATE_COLS), _rows(tm, D_MODEL),
                  _resident((1, D_MODEL)), _resident(pool_w.shape), _resident((1, POOL_WIDTH)),
                  _pick(sqw.shape, 0), _pick(sqw.shape, 1), _resident(wbw.shape)],
        out_specs=[_rows(tm, D_MODEL), _rows(tm, D_MODEL), _rows(tm, D_MODEL), _rows(tm, POOL_WIDTH)],
        out_shape=[jax.ShapeDtypeStruct((t, D_MODEL), F32), jax.ShapeDtypeStruct((t, D_MODEL), BF16),
                   jax.ShapeDtypeStruct((t, D_MODEL), BF16), jax.ShapeDtypeStruct((t, POOL_WIDTH), BF16)],
        scratch_shapes=[pltpu.VMEM((ext_rows, POOL_WIDTH), F32)],
        compiler_params=_params(("arbitrary",)),
    )(o, main, pool_r, gates, x1, onorm, pool_w, pool_scale, sqw, sqw, wbw)


def _post_bwd(dx2, o, main, gates, ya, yb, pooled, onorm, pool_w, pool_scale, sqw, wbw, tm, riders=()):
    t = o.shape[0]
    n_tiles = t // tm
    ext_rows = tm + POOL_HALO
    n_groups = len(POOL_WINDOWS)

    def rev(cols, col_block=0):
        return pl.BlockSpec((tm, cols), lambda i: (n_tiles - 1 - i, col_block))

    def body(dx2_ref, o_ref, og_ref, gt_ref, ya_ref, yb_ref, pooled_ref, on_ref, pw_ref, ps_ref, wa_ref, wout_ref, wb_ref,
             do_ref, dog_ref, du_ref, dgt_ref, dwa_ref, dwout_ref, dwb_ref, dpw_ref, dps_ref, don_ref, ext):
        i = pl.program_id(0)

        @pl.when(i == 0)
        def _():
            ext[tm:ext_rows, :] = jnp.zeros((POOL_HALO, POOL_WIDTH), F32)
            for ref in (dwa_ref, dwout_ref, dwb_ref, dpw_ref, dps_ref, don_ref):
                ref[...] = jnp.zeros_like(ref)

        groups = [slice(gidx * HEAD_DIM, (gidx + 1) * HEAD_DIM) for gidx in range(n_groups)]
        shards = [slice(j * 256, (j + 1) * 256) for j in range(N_CHIPS)]
        dx2b = dx2_ref[...].astype(BF16)
        dy = _dot_nt(dx2b, wout_ref[...])
        pooled_b = pooled_ref[...]
        pm = jnp.concatenate([_dot(pooled_b[:, g], pw_ref[gidx].astype(BF16)) for gidx, g in enumerate(groups)], axis=1)
        gt = gt_ref[...]
        sga = _sigmoid(gt[:, :D_MODEL])
        sgb = _sigmoid(gt[:, D_MODEL:])
        ya = ya_ref[...].astype(F32)
        yb = yb_ref[...].astype(F32)
        y = (sga * ya + sgb * yb).astype(BF16)
        dya = (dy * sga).astype(BF16)
        dyb = (dy * sgb).astype(BF16)
        dgt_ref[:, :D_MODEL] = (dy * ya * sga * (1.0 - sga)).astype(BF16)
        dgt_ref[:, D_MODEL:] = (dy * yb * sgb * (1.0 - sgb)).astype(BF16)
        dwout_ref[...] += _dot_tn(y, dx2b)
        doa = _dot_nt(dya, wa_ref[...])
        dmixed = _dot_nt(dyb[:, shards[0]], wb_ref[0])
        for j in range(1, N_CHIPS):
            dmixed += _dot_nt(dyb[:, shards[j]], wb_ref[j])
        r, n = _head_norm(o_ref[...])
        onv = on_ref[...]
        og = og_ref[...]
        sog = _sigmoid(og)
        silu_og = og * sog
        normed = n * onv
        oa = (normed * silu_og).astype(BF16)
        dog_ref[...] = (doa * normed * (sog * (1.0 + og * (1.0 - sog)))).astype(BF16)
        dnormed = doa * silu_og
        don_ref[...] += _colsum(dnormed * n)
        do_ref[...] = _head_norm_bwd(dnormed * onv, n, r)
        psv = ps_ref[...]
        mixed_b = (pm * psv).astype(BF16)
        dps_ref[...] += _colsum(dmixed * pm)
        dpm = (dmixed * psv).astype(BF16)
        dwa_ref[...] += _dot_tn(oa, dya)
        for j in range(N_CHIPS):
            dwb_ref[j] += _dot_tn(mixed_b, dyb[:, shards[j]])
        counts = _window_counts((n_tiles - 1 - i) * tm, tm)
        dpooled = []
        for gidx, g in enumerate(groups):
            dpw_ref[gidx] += _dot_tn(pooled_b[:, g], dpm[:, g])
            dpooled.append(_dot_nt(dpm[:, g], pw_ref[gidx].astype(BF16)))
        ext[0:tm, :] = jnp.concatenate([dpooled[gidx] / counts[gidx] for gidx in range(n_groups)], axis=1)
        e = ext[...]
        du = []
        for gidx, w in enumerate(POOL_WINDOWS):
            s = e[:, gidx * HEAD_DIM:(gidx + 1) * HEAD_DIM]
            shift = 1
            while shift < w:
                s = s + pltpu.roll(s, ext_rows - shift, axis=0)
                shift *= 2
            du.append(s[:tm, :] - dpooled[gidx])
        ext[tm:ext_rows, :] = ext[0:POOL_HALO, :]
        du_ref[...] = jnp.concatenate(du, axis=1).astype(BF16)

    wa_shape = (D_MODEL, D_MODEL)
    return _hosted(
        riders, body, name="post_bwd", grid=(n_tiles,),
        in_specs=[rev(D_MODEL), rev(D_MODEL), rev(D_MODEL, 3), rev(GATE_COLS), rev(D_MODEL), rev(D_MODEL), rev(POOL_WIDTH),
                  _resident((1, D_MODEL)), _resident(pool_w.shape), _resident((1, POOL_WIDTH)),
                  _pick(sqw.shape, 0), _pick(sqw.shape, 1), _resident(wbw.shape)],
        out_specs=[rev(D_MODEL), rev(D_MODEL), rev(POOL_WIDTH), rev(GATE_COLS),
                   _acc(wa_shape), _acc(wa_shape), _acc(wbw.shape), _acc(pool_w.shape), _acc((1, POOL_WIDTH)),
                   _acc((1, D_MODEL))],
        out_shape=[jax.ShapeDtypeStruct((t, D_MODEL), F32), jax.ShapeDtypeStruct((t, D_MODEL), BF16),
                   jax.ShapeDtypeStruct((t, POOL_WIDTH), BF16), jax.ShapeDtypeStruct((t, GATE_COLS), BF16),
                   jax.ShapeDtypeStruct(wa_shape, F32), jax.ShapeDtypeStruct(wa_shape, F32),
                   jax.ShapeDtypeStruct(wbw.shape, F32), jax.ShapeDtypeStruct(pool_w.shape, F32),
                   jax.ShapeDtypeStruct((1, POOL_WIDTH), F32), jax.ShapeDtypeStruct((1, D_MODEL), F32)],
        scratch_shapes=[pltpu.VMEM((ext_rows, POOL_WIDTH), F32)],
        compiler_params=_params(("arbitrary",)),
    )(dx2, o, main, gates, ya, yb, pooled, onorm, pool_w, pool_scale, sqw, sqw, wbw)


def _tail(x3, p, target, g_ple, g_post, g_final, sqw, wpw, tm, riders=()):
    t = x3.shape[0]
    pd = p.shape[1]

    def body(x_ref, p_ref, tg_ref, g4_ref, g5_ref, g6_ref, wg_ref, wp_ref,
             dx_ref, loss_ref, dwg_ref, dwp_ref, dg4_ref, dg5_ref, dg6_ref):
        @pl.when(pl.program_id(0) == 0)
        def _():
            for ref in (loss_ref, dwg_ref, dwp_ref, dg4_ref, dg5_ref, dg6_ref):
                ref[...] = jnp.zeros_like(ref)

        x3v = x_ref[...]
        g4, g5, g6 = g4_ref[...], g5_ref[...], g6_ref[...]
        r4, n4 = _rms(x3v)
        h4 = (n4 * g4).astype(BF16)
        gate = _sigmoid(_dot(h4, wg_ref[...]))
        pb = p_ref[...].astype(BF16)
        r5, n5 = _rms(jnp.concatenate([_dot(pb, wp_ref[j]) for j in range(N_CHIPS)], axis=1))
        emb = n5 * g5
        r6, n6 = _rms(x3v + gate * emb)
        diff = n6 * g6 - tg_ref[...]
        loss_ref[...] += 0.5 * jnp.sum(jnp.mean(diff * diff, axis=-1, keepdims=True), axis=0, keepdims=True)
        dout = diff * (1.0 / D_MODEL)
        dg6_ref[...] += _colsum(dout * n6)
        dx4 = _rms_bwd(dout * g6, n6, r6)
        demb = dx4 * gate
        dg5_ref[...] += _colsum(demb * n5)
        dpre = _rms_bwd(demb * g5, n5, r5).astype(BF16)
        for j in range(N_CHIPS):
            dwp_ref[j] += _dot_tn(pb, dpre[:, j * pd:(j + 1) * pd])
        dz = (dx4 * emb * gate * (1.0 - gate)).astype(BF16)
        dwg_ref[...] += _dot_tn(h4, dz)
        dh4 = _dot_nt(dz, wg_ref[...])
        dg4_ref[...] += _colsum(dh4 * n4)
        dx_ref[...] = dx4 + _rms_bwd(dh4 * g4, n4, r4)

    sq_shape = (D_MODEL, D_MODEL)
    vec = (1, D_MODEL)
    return _hosted(
        riders, body, name="tail", grid=(t // tm,),
        in_specs=[_rows(tm, D_MODEL), _rows(tm, pd), _rows(tm, D_MODEL), _resident(vec), _resident(vec), _resident(vec),
                  _pick(sqw.shape, 2), _resident(wpw.shape)],
        out_specs=[_rows(tm, D_MODEL), _acc((1, 1)), _acc(sq_shape), _acc(wpw.shape), _acc(vec), _acc(vec), _acc(vec)],
        out_shape=[jax.ShapeDtypeStruct((t, D_MODEL), F32), jax.ShapeDtypeStruct((1, 1), F32),
                   jax.ShapeDtypeStruct(sq_shape, F32), jax.ShapeDtypeStruct(wpw.shape, F32),
                   jax.ShapeDtypeStruct(vec, F32), jax.ShapeDtypeStruct(vec, F32), jax.ShapeDtypeStruct(vec, F32)],
        compiler_params=_params(("arbitrary",)),
    )(x3, p, target, g_ple, g_post, g_final, sqw, wpw)


def _position():
    return lax.axis_index("x"), lax.axis_index("y"), lax.axis_index("c")


def _other_chips(x, y):
    return [(1 - x, y), (x, 1 - y), (1 - x, 1 - y)]


def _remote(src, dst, send_sems, recv_sems, k, device):
    return pltpu.make_async_remote_copy(src_ref=src, dst_ref=dst, send_sem=send_sems.at[k], recv_sem=recv_sems.at[k],
                                        device_id=device, device_id_type=MESH)


def _gather_rider(shards, forward_at):
    n = len(shards)

    def copies(ins, outs, sems):
        send_sems, recv_sems = sems
        x, y, c = _position()
        mine = 2 * x + y
        first, passed, arriving = [], [], []
        for k, (cx, cy) in enumerate(_other_chips(x, y)):
            theirs = 2 * cx + cy
            for a in range(n):
                first.append(_remote(ins[a].at[:, c], outs[a].at[:, mine, c], send_sems, recv_sems, k * n + a, (cx, cy, c)))
                block = outs[a].at[:, theirs, c]
                passed.append(_remote(block, block, send_sems, recv_sems, (3 + k) * n + a, (x, y, 1 - c)))
                other = outs[a].at[:, theirs, 1 - c]
                arriving.append(_remote(other, other, send_sems, recv_sems, (3 + k) * n + a, (x, y, 1 - c)))
        return first, passed, arriving

    return _Rider(shards, [jax.ShapeDtypeStruct((s.shape[0], N_CHIPS) + s.shape[1:], s.dtype) for s in shards],
                  [pltpu.SemaphoreType.DMA((6 * n,)), pltpu.SemaphoreType.DMA((6 * n,))], _gather_phases(copies, forward_at))


def _gather_phases(copies, forward_at):
    def begin(ins, outs, sems):
        for cp in copies(ins, outs, sems)[0]:
            cp.start()

    def forward(ins, outs, sems):
        first, passed, _ = copies(ins, outs, sems)
        for got, cp in zip(first, passed):
            got.wait_recv()
            cp.start()

    def finish(ins, outs, sems):
        first, passed, arriving = copies(ins, outs, sems)
        for cp in arriving:
            cp.wait_recv()
        for cp in first + passed:
            cp.wait_send()

    return [(0, begin), (forward_at, forward), (1, finish)]


def _with_own(gathered, shard, slot):
    return lax.dynamic_update_slice(gathered, shard[:, None], (0, slot, 0, 0, 0))


def _exchange_rider(arrays, out_shape, n_copies, transfers, n_local=0):
    def copies(ins, outs, sems):
        send_sems, recv_sems, local_sems = sems
        remote, local = transfers(ins, outs)
        return ([_remote(src, dst, send_sems, recv_sems, i, dev) for i, (src, dst, dev) in enumerate(remote)],
                [pltpu.make_async_copy(src, dst, local_sems.at[i]) for i, (src, dst) in enumerate(local)])

    def begin(ins, outs, sems):
        remote, local = copies(ins, outs, sems)
        for cp in remote + local:
            cp.start()

    def finish(ins, outs, sems):
        remote, local = copies(ins, outs, sems)
        for cp in remote:
            cp.wait_recv()
        for cp in remote:
            cp.wait_send()
        for cp in local:
            cp.wait()

    return _Rider(arrays, out_shape,
                  [pltpu.SemaphoreType.DMA((n_copies,)), pltpu.SemaphoreType.DMA((n_copies,)),
                   pltpu.SemaphoreType.DMA((max(n_local, 1),))],
                  [(0, begin), (1, finish)])


def _pair_rider(partials):
    def transfers(ins, outs):
        x, y, c = _position()
        return [(ins[a].at[:, :, 1 - c], outs[a], (x, y, 1 - c)) for a in range(len(partials))], []

    shapes = [jax.ShapeDtypeStruct(g.shape[:2] + g.shape[3:], g.dtype) for g in partials]
    return _exchange_rider(partials, shapes, len(partials), transfers)


def _chips_rider(sums):
    n = len(sums)

    def transfers(ins, outs):
        x, y, c = _position()
        return [(ins[a].at[:, 2 * cx + cy], outs[a].at[:, k], (cx, cy, c))
                for k, (cx, cy) in enumerate(_other_chips(x, y)) for a in range(n)], []

    shapes = [jax.ShapeDtypeStruct((q.shape[0], 3) + q.shape[2:], q.dtype) for q in sums]
    return _exchange_rider(sums, shapes, 3 * n, transfers)


def _share_rider(halves):
    def transfers(ins, outs):
        x, y, c = _position()
        return [(ins[a], outs[a], (x, y, 1 - c)) for a in range(len(halves))], []

    return _exchange_rider(halves, [jax.ShapeDtypeStruct(h.shape, h.dtype) for h in halves], len(halves), transfers)


def _small_rider(pack):
    flips = [(fx, fy, fc) for fx in (0, 1) for fy in (0, 1) for fc in (0, 1)][1:]

    def transfers(ins, outs):
        x, y, c = _position()
        slot = outs[0].at[4 * x + 2 * y + c]
        flip = lambda v, f: v + f - 2 * v * f
        return [(ins[0], slot, (flip(x, fx), flip(y, fy), flip(c, fc))) for fx, fy, fc in flips], [(ins[0], slot)]

    return _exchange_rider([pack], [jax.ShapeDtypeStruct((8,) + pack.shape, pack.dtype)], len(flips), transfers, n_local=1)


IN_HBM = pl.BlockSpec(memory_space=pltpu.HBM)
IN_SEM = pl.BlockSpec(memory_space=pltpu.SEMAPHORE)
SPLIT_COPY = pltpu.CompilerParams(has_side_effects=pltpu.SideEffectType.DATAFLOW_SIDE_EFFECTING)


def _start_copies(sources, landing_shapes, n_copies, plan, name):
    n_src, n_land = len(sources), len(landing_shapes)

    def body(*refs):
        srcs, lands, outs = refs[:n_src], refs[n_src:n_src + n_land], refs[n_src + n_land:]
        send, recv, token = outs[:n_copies], outs[n_copies:2 * n_copies], outs[-1]
        for i, (src, dst, device) in enumerate(plan(srcs, lands)):
            pltpu.make_async_remote_copy(src_ref=src, dst_ref=dst, send_sem=send[i], recv_sem=recv[i], device_id=device,
                                         device_id_type=MESH).start()
        token[...] = jnp.zeros_like(token)

    arrays = [pltpu.with_memory_space_constraint(a, pltpu.HBM) for a in sources]
    arrays += [pltpu.with_memory_space_constraint(lax.empty(s.shape, s.dtype), pltpu.HBM) for s in landing_shapes]
    res = pl.pallas_call(
        body, name=name,
        out_shape=[pltpu.SemaphoreType.DMA(())] * (2 * n_copies) + [pltpu.HBM(a.shape, a.dtype) for a in arrays]
        + [jax.ShapeDtypeStruct((8, 128), F32)],
        in_specs=[IN_HBM] * len(arrays),
        out_specs=[IN_SEM] * (2 * n_copies) + [IN_HBM] * len(arrays) + [pl.BlockSpec(memory_space=pltpu.VMEM)],
        input_output_aliases={i: 2 * n_copies + i for i in range(len(arrays))},
        compiler_params=SPLIT_COPY,
    )(*arrays)
    sems, rest = res[:2 * n_copies], res[2 * n_copies:]
    return sems, rest[:n_src], rest[n_src:n_src + n_land], rest[-1]


def _wait_copies(started, n_copies, plan, after, name):
    sems, sources, landings, _ = started
    n_src, n_land = len(sources), len(landings)

    def body(*refs):
        srcs, lands = refs[:n_src], refs[n_src:n_src + n_land]
        sem_refs = refs[n_src + n_land:n_src + n_land + 2 * n_copies]
        send, recv = sem_refs[:n_copies], sem_refs[n_copies:]
        for i, (src, dst, device) in enumerate(plan(srcs, lands)):
            cp = pltpu.make_async_remote_copy(src_ref=src, dst_ref=dst, send_sem=send[i], recv_sem=recv[i], device_id=device,
                                              device_id_type=MESH)
            cp.wait_send()
            cp.wait_recv()

    arrays = list(sources) + list(landings)
    res = pl.pallas_call(
        body, name=name, out_shape=[pltpu.HBM(a.shape, a.dtype) for a in arrays],
        in_specs=[IN_HBM] * len(arrays) + [IN_SEM] * (2 * n_copies) + [ANY], out_specs=[IN_HBM] * len(arrays),
        input_output_aliases={i: i for i in range(len(arrays))},
        compiler_params=SPLIT_COPY,
    )(*arrays, *sems, after)
    return res[n_src:]


def _chips_plan(n):
    def plan(srcs, lands):
        x, y, c = _position()
        return [(srcs[a].at[:, 2 * cx + cy], lands[a].at[:, k], (cx, cy, c))
                for k, (cx, cy) in enumerate(_other_chips(x, y)) for a in range(n)]
    return plan


def _alone(rider, name, after=()):
    return _hosted([rider], lambda *refs: None, name=name, in_specs=[ANY] * len(after), out_specs=[], out_shape=[])(*after)[1][0]


def _add_pair(mine, theirs, c, tag):
    n = len(mine)

    def body(c_ref, *refs):
        for a in range(n):
            refs[2 * n + a][...] = (refs[2 * a][...].astype(F32) + refs[2 * a + 1][...].astype(F32)).astype(BF16)

    in_specs, out_specs = [], []
    for got in theirs:
        l, _, hr, cols = got.shape
        in_specs += [pl.BlockSpec((l, None, None, hr, cols), lambda j, s: (0, j, s[0], 0, 0)),
                     pl.BlockSpec((l, None, hr, cols), lambda j, s: (0, j, 0, 0))]
        out_specs.append(pl.BlockSpec((l, None, hr, cols), lambda j, s: (0, j, 0, 0)))
    return pl.pallas_call(
        body, name=f"add_pair_{tag}",
        grid_spec=pltpu.PrefetchScalarGridSpec(num_scalar_prefetch=1, grid=(N_CHIPS,), in_specs=in_specs, out_specs=out_specs),
        out_shape=[jax.ShapeDtypeStruct(got.shape, BF16) for got in theirs],
        compiler_params=_params(("parallel",)),
    )(c.reshape(1), *[a for pair in zip(mine, theirs) for a in pair])


def _add_chips(parts, received, mine, tag):
    n = len(parts)

    def body(j_ref, *refs):
        for a in range(n):
            acc = refs[2 * a][...].astype(F32)
            for k in range(3):
                acc += refs[2 * a + 1][:, k].astype(F32)
            refs[2 * n + a][...] = acc

    in_specs, out_specs, out_shape = [], [], []
    for got in received:
        l, _, hr, cols = got.shape
        in_specs += [pl.BlockSpec((l, None, hr // 2, cols), lambda i, s: (0, s[0], i, 0)),
                     pl.BlockSpec((l, 3, hr // 2, cols), lambda i, s: (0, 0, i, 0))]
        out_specs.append(pl.BlockSpec((l, hr // 2, cols), lambda i, s: (0, i, 0)))
        out_shape.append(jax.ShapeDtypeStruct((l, hr, cols), F32))
    return pl.pallas_call(
        body, name=f"add_chips_{tag}",
        grid_spec=pltpu.PrefetchScalarGridSpec(num_scalar_prefetch=1, grid=(2,), in_specs=in_specs, out_specs=out_specs),
        out_shape=out_shape,
        compiler_params=_params(("parallel",)),
    )(mine.reshape(1), *[a for pair in zip(parts, received) for a in pair])


def _adam_update(w, g, m, v):
    m2 = ADAM_B1 * m + (1.0 - ADAM_B1) * g
    v2 = ADAM_B2 * v + (1.0 - ADAM_B2) * jnp.square(g)
    m_hat = m2 / (1.0 - ADAM_B1 ** ADAM_STEP)
    v_hat = v2 / (1.0 - ADAM_B2 ** ADAM_STEP)
    return -ADAM_LR * (m_hat / (jnp.sqrt(v_hat) + ADAM_EPS) + ADAM_WD * w), m2, v2


def _adamw_group(items, tag, after=()):
    n = len(items)

    def body(*refs):
        ins, outs = refs[:5 * n], refs[5 * n + len(after):]
        mine = pl.program_id(0) == lax.axis_index("c")
        for a in range(n):
            w_ref, own_ref, other_ref, m_ref, v_ref = ins[5 * a:5 * a + 5]
            g_ref, d_ref, nm_ref, nv_ref = outs[4 * a:4 * a + 4]
            gv = jnp.where(mine, own_ref[...], other_ref[...])
            g_ref[...] = gv
            d_ref[...], nm_ref[...], nv_ref[...] = _adam_update(w_ref[...], gv, m_ref[...], v_ref[...])

    in_specs, out_specs, out_shape, args = [], [], [], []
    for w, own, other, m, v in items:
        _, hr, cols = w.shape
        tr = hr // ADAM_BLOCKS
        full = pl.BlockSpec((None, tr, cols), lambda h, i: (h, i, 0))
        half = pl.BlockSpec((tr, cols), lambda h, i: (i, 0))
        in_specs += [full, half, half, full, full]
        out_specs += [full] * 4
        out_shape += [jax.ShapeDtypeStruct((2, hr, cols), F32)] * 4
        args += [w, own, other, m, v]
    outs = pl.pallas_call(body, name=f"adamw_{tag}", grid=(2, ADAM_BLOCKS), in_specs=in_specs + [ANY] * len(after),
                          out_specs=out_specs, out_shape=out_shape,
                          compiler_params=_params(("parallel", "parallel")))(*args, *after)
    return [outs[4 * a:4 * a + 4] for a in range(n)]


def _adamw_small(w, gathered, m, v, shapes):
    n_rows = w.shape[0]
    places = []
    for i, name in enumerate(VECTOR_PARAMS):
        places.append((name, i * TILE_ROWS, 1 if len(shapes[name]) == 1 else shapes[name][0], shapes[name][-1]))
    places.append(("pool_w", len(VECTOR_PARAMS) * TILE_ROWS, n_rows - len(VECTOR_PARAMS) * TILE_ROWS, D_MODEL))

    def body(w_ref, g_ref, m_ref, v_ref, loss_ref, *rest):
        outs, (sum_scr, d_scr, nm_scr, nv_scr) = rest[:-4], rest[-4:]
        total = g_ref[0]
        for i in range(1, g_ref.shape[0]):
            total += g_ref[i]
        sum_scr[...] = total
        gv = sum_scr[0:n_rows, :]
        d_scr[...], nm_scr[...], nv_scr[...] = _adam_update(w_ref[...], gv, m_ref[...], v_ref[...])
        loss_ref[...] = sum_scr[n_rows:n_rows + 1, 0:1]
        for k, (_, first, rows, cols) in enumerate(places):
            for j, scr in enumerate((sum_scr, d_scr, nm_scr, nv_scr)):
                outs[4 * k + j][...] = scr[first:first + rows, 0:cols]

    out_shape = [jax.ShapeDtypeStruct((1, 1), F32)]
    for _, _, rows, cols in places:
        out_shape += [jax.ShapeDtypeStruct((rows, cols), F32)] * 4
    res = pl.pallas_call(
        body, name="adamw_small", out_shape=out_shape,
        scratch_shapes=[pltpu.VMEM(gathered.shape[1:], F32)] + [pltpu.VMEM(w.shape, F32)] * 3,
        compiler_params=_params())(w, gathered, m, v)
    return res[0], {name: res[1 + 4 * k:5 + 4 * k] for k, (name, _, _, _) in enumerate(places)}


VECTOR_PARAMS = ("ffn1_norm", "mix_norm", "hgrn_lb", "hgrn_onorm", "ffn2_norm", "ple_norm", "ple_post_norm", "final_norm",
                 "pool_scale")
ALL_PARAMS = ("ffn1_norm", "ffn1_w1", "ffn1_w3", "ffn1_w2", "mix_norm", "w_in", "hgrn_lb", "hgrn_onorm", "w_branch_a",
              "pool_w", "pool_scale", "w_branch_b", "w_out", "ffn2_norm", "ffn2_w1", "ffn2_w3", "ffn2_w2", "ple_norm",
              "ple_w_gate", "ple_w_proj", "ple_post_norm", "final_norm")
TILE_ROWS = 8


def _pack_small(values, loss=None):
    tile = lambda a: jnp.pad(a, ((0, TILE_ROWS - a.shape[0]), (0, D_MODEL - a.shape[1])))
    parts = [tile(values[name].reshape(-1, values[name].shape[-1])) for name in VECTOR_PARAMS]
    parts.append(values["pool_w"].reshape(-1, D_MODEL))
    if loss is not None:
        parts.append(tile(loss))
    return jnp.concatenate(parts, axis=0)


def _halved(a, lead):
    return a.reshape(lead, 2, -1, a.shape[-1])


def _shard_halves(a, lead):
    return a.reshape(lead, N_CHIPS, 2, -1, a.shape[-1])


REDUCED_TRANSPOSED = ("ffn1_w1", "ffn1_w3", "ffn2_w1", "ffn2_w3")


def _entries(arrays):
    return [a[i] for a in arrays for i in range(a.shape[0])]


def _adam_items(names, own, other, w, m, v):
    items = []
    for name, g_own, g_other in zip(names, _entries(own), _entries(other)):
        view = (lambda a: _halved(a[0].T, 1)[0]) if name in REDUCED_TRANSPOSED else (lambda a: _halved(a, 1)[0])
        items.append((view(w[name]), g_own, g_other, view(m[name]), view(v[name])))
    return items


def _adam_store(names, results, w, out):
    for name, res in zip(names, results):
        shape = w[name].shape
        if name in REDUCED_TRANSPOSED:
            back = [a.reshape(shape[2], shape[1]).T.reshape(shape) for a in res]
        else:
            back = [a.reshape(shape) for a in res]
        out["grad"][name], out["delta"][name], out["new_m"][name], out["new_v"][name] = back


def kernel(x, p, ffn1_norm, ffn1_w1, ffn1_w3, ffn1_w2, mix_norm, w_in, hgrn_lb, hgrn_onorm, w_branch_a, pool_w, pool_scale, w_branch_b, w_out, ffn2_norm, ffn2_w1, ffn2_w3, ffn2_w2, ple_norm, ple_w_gate, ple_w_proj, ple_post_norm, final_norm, loss_target, m_ffn1_norm, m_ffn1_w1, m_ffn1_w3, m_ffn1_w2, m_mix_norm, m_w_in, m_hgrn_lb, m_hgrn_onorm, m_w_branch_a, m_pool_w, m_pool_scale, m_w_branch_b, m_w_out, m_ffn2_norm, m_ffn2_w1, m_ffn2_w3, m_ffn2_w2, m_ple_norm, m_ple_w_gate, m_ple_w_proj, m_ple_post_norm, m_final_norm, v_ffn1_norm, v_ffn1_w1, v_ffn1_w3, v_ffn1_w2, v_mix_norm, v_w_in, v_hgrn_lb, v_hgrn_onorm, v_w_branch_a, v_pool_w, v_pool_scale, v_w_branch_b, v_w_out, v_ffn2_norm, v_ffn2_w1, v_ffn2_w3, v_ffn2_w2, v_ple_norm, v_ple_w_gate, v_ple_w_proj, v_ple_post_norm, v_final_norm):
    args = dict(locals())
    w = {name: args[name] for name in ALL_PARAMS}
    m = {name: args["m_" + name] for name in ALL_PARAMS}
    v = {name: args["v_" + name] for name in ALL_PARAMS}
    cx, cy, cc = _position()
    chip = (2 * cx + cy).astype(jnp.int32)
    core = cc.astype(jnp.int32)
    xs, ps, target = x[0], p[0, 0], loss_target[0]
    t = xs.shape[0]
    tm = min(256, t)
    tm_ffn = min(512, t)
    tt = min(512, t)
    tk = min(2048, t)
    small = {name: w[name] for name in VECTOR_PARAMS}
    small["final_norm"] = w["final_norm"].reshape(1, D_MODEL)
    pool_w0 = w["pool_w"][0]

    ffn_shard = lambda i: _halved(jnp.stack([w[f"ffn{i}_w1"][0].T, w[f"ffn{i}_w3"][0].T, w[f"ffn{i}_w2"][0]]).astype(BF16), 3)
    sq_shard = _halved(jnp.stack([w["w_branch_a"][0], w["w_out"][0], w["ple_w_gate"][0]]).astype(BF16), 3)
    win_shard, wb_shard, wp_shard = (_halved(w[n].astype(BF16), 1) for n in ("w_in", "w_branch_b", "ple_w_proj"))

    ffn1_shard, ffn2_shard = ffn_shard(1), ffn_shard(2)
    (ffn1w,) = _alone(_gather_rider([ffn1_shard], 0.5), "gather_ffn1")
    ffn1w = _with_own(ffn1w, ffn1_shard, chip).reshape(3, D_FF, D_MODEL)
    (x1, a1, b1), ((winw,),) = _ffn_fwd(xs, small["ffn1_norm"], ffn1w, 1, tm_ffn, [_gather_rider([win_shard], 0.6)])
    winw = _with_own(winw, win_shard, chip).reshape(N_CHIPS, D_MODEL, SHARD_IN_COLS)
    (main, pool_r, gates), ((sqw, wbw, wpw),) = _mix_fwd(x1, small["mix_norm"], winw, tm,
                                                          [_gather_rider([sq_shard, wb_shard, wp_shard], 0.5)])
    sqw = _with_own(sqw, sq_shard, chip).reshape(3, D_MODEL, D_MODEL)
    wbw = _with_own(wbw, wb_shard, chip).reshape(N_CHIPS, POOL_WIDTH, -1)
    wpw = _with_own(wpw, wp_shard, chip).reshape(N_CHIPS, ps.shape[1], -1)
    (o, states), ((ffn2w,),) = _hgrn_fwd(main, small["hgrn_lb"], tt, [_gather_rider([ffn2_shard], 0.7)])
    ffn2w = _with_own(ffn2w, ffn2_shard, chip).reshape(3, D_FF, D_MODEL)
    (x2, ya, yb, pooled), _ = _post_fwd(o, main, pool_r, gates, x1, small["hgrn_onorm"], pool_w0, small["pool_scale"], sqw,
                                       wbw, tm)
    (x3, a2, b2), _ = _ffn_fwd(x2, small["ffn2_norm"], ffn2w, 2, tm_ffn)
    (dx3, loss, d_wg, d_wp, d_ple, d_post, d_final), _ = _tail(
        x3, ps, target, small["ple_norm"], small["ple_post_norm"], small["final_norm"], sqw, wpw, tm_ffn)

    add_pairs = lambda parts, got, group: _add_pair(parts, got, core, group)
    add_chips = lambda sums, got, group: _add_chips(sums, got, chip, group)
    names1 = ("ffn2_w1", "ffn2_w3", "ffn2_w2", "ple_w_gate", "ple_w_proj")
    names2 = ("w_branch_a", "w_out", "w_branch_b")
    names3 = ("w_in",)
    names4 = ("ffn1_w1", "ffn1_w3")
    names5 = ("ffn1_w2",)
    tags1, tags2, tags3, tags4, tags5 = "ffn2", "branches", "w_in", "ffn1_in", "ffn1_out"

    (dx2, dab2, s2, h3, dxh2, d_ffn2_norm), _ = _ffn_bwd(dx3, x2, small["ffn2_norm"], a2, b2, ffn2w, 2, tm)
    (d_w13_2,), _ = _wgrad(dab2, h3, WGRAD_IN_BLOCKS, "wgrad_ffn2_in", tk)
    (d_w2_2,), _ = _wgrad(s2, dxh2, WGRAD_OUT_BLOCKS, "wgrad_ffn2_out", tk)
    part1 = [_shard_halves(d_w13_2, 2), _shard_halves(d_w2_2, 1), _shard_halves(d_wg, 1), _shard_halves(d_wp, 1)]
    (do, dog, du, dgates, d_wa, d_wout, d_wb, d_pool_w, d_pool_scale, d_onorm), (sib1,) = _post_bwd(
        dx2, o, main, gates, ya, yb, pooled, small["hgrn_onorm"], pool_w0, small["pool_scale"], sqw, wbw, tm,
        [_pair_rider(part1)])
    sums1 = add_pairs(part1, sib1, tags1)
    part2 = [_shard_halves(d_wa, 1), _shard_halves(d_wout, 1), _shard_halves(d_wb, 1)]
    (dqfi, d_lb), (got1, sib2) = _hgrn_bwd(main, small["hgrn_lb"], states, do, tt, [_chips_rider(sums1), _pair_rider(part2)])
    own1 = add_chips(sums1, got1, tags1)
    sums2 = add_pairs(part2, sib2, tags2)
    (dx1, dproj, h2, d_mix_norm), (other1, got2) = _mix_bwd(dqfi, dog, du, dgates, dx2, x1, small["mix_norm"], winw, tm,
                                                            [_share_rider(own1), _chips_rider(sums2)])
    own2 = add_chips(sums2, got2, tags2)
    (d_win,), (other2,) = _wgrad_cols(h2, dproj, N_CHIPS, "wgrad_in", tk, [_share_rider(own2)])
    part3 = [_shard_halves(d_win, 1)]
    (dx, dab1, s1, h1, dxh1, d_ffn1_norm), _ = _ffn_bwd(dx1, xs, small["ffn1_norm"], a1, b1, ffn1w, 1, tm)
    vecs = dict(ffn1_norm=d_ffn1_norm, mix_norm=d_mix_norm, hgrn_lb=d_lb, hgrn_onorm=d_onorm, ffn2_norm=d_ffn2_norm,
                ple_norm=d_ple, ple_post_norm=d_post, final_norm=d_final, pool_scale=d_pool_scale, pool_w=d_pool_w)
    (d_w2_1,), (sib3, (small_all,)) = _wgrad(s1, dxh1, WGRAD_OUT_BLOCKS, "wgrad_ffn1_out", tk,
                                             [_pair_rider(part3), _small_rider(_pack_small(vecs, loss))])
    sums3 = add_pairs(part3, sib3, tags3)
    part5 = [_shard_halves(d_w2_1, 1)]
    (d_w13_1,), (got3, sib5) = _wgrad(dab1, h1, WGRAD_IN_BLOCKS, "wgrad_ffn1_in", tk,
                                      [_chips_rider(sums3), _pair_rider(part5)])
    own3 = add_chips(sums3, got3, tags3)
    sums5 = add_pairs(part5, sib5, tags5)
    part4 = [_shard_halves(d_w13_1, 2)]
    landing = lambda a: jax.ShapeDtypeStruct((a.shape[0], 3) + a.shape[2:], a.dtype)

    def sibling_plan(srcs, lands):
        x, y, c = _position()
        return [(srcs[-1], lands[-1], (x, y, 1 - c))]

    plan_a = lambda srcs, lands: _chips_plan(1)(srcs[:1], lands[:1]) + sibling_plan(srcs, lands)
    started_a = _start_copies([sums5[0], own3[0]], [landing(sums5[0]), own3[0]], 4, plan_a, "start_a")
    sib4 = _alone(_pair_rider(part4), "pair_last", [started_a[3]])
    sums4 = add_pairs(part4, sib4, tags4)
    started_b = _start_copies([sums4[0]], [landing(sums4[0])], 3, _chips_plan(1), "start_b")

    out = dict(grad={}, delta={}, new_m={}, new_v={})
    results = _adamw_group(_adam_items(names1 + names2, own1 + own2, other1 + other2, w, m, v), "early",
                           [started_a[3], started_b[3]])
    _adam_store(names1 + names2, results, w, out)
    got5, other3 = _wait_copies(started_a, 4, plan_a, results[0][1], "wait_a")
    results = _adamw_group(_adam_items(names3, own3, [other3], w, m, v), "w_in")
    _adam_store(names3, results, w, out)
    (got4,) = _wait_copies(started_b, 3, _chips_plan(1), results[0][1], "wait_b")
    own4 = add_chips(sums4, [got4], tags4)
    own5 = add_chips(sums5, [got5], tags5)
    other4, other5 = _alone(_share_rider(own4 + own5), "share_last")
    results = _adamw_group(_adam_items(names4 + names5, own4 + own5, [other4, other5], w, m, v), "ffn1")
    _adam_store(names4 + names5, results, w, out)

    shapes = {name: w[name].shape for name in VECTOR_PARAMS + ("pool_w",)}
    loss, results = _adamw_small(_pack_small(w), small_all, _pack_small(m), _pack_small(v), shapes)
    for name, res in results.items():
        out["grad"][name], out["delta"][name], out["new_m"][name], out["new_v"][name] = (a.reshape(shapes[name]) for a in res)

    return (loss[0, 0], dx[None], *[out["grad"][n] for n in ALL_PARAMS], *[out["delta"][n] for n in ALL_PARAMS],
            *[out["new_m"][n] for n in ALL_PARAMS], *[out["new_v"][n] for n in ALL_PARAMS])
```

```python
import functools

import jax
import jax.numpy as jnp
from jax import lax
from jax.experimental import pallas as pl
from jax.experimental.pallas import tpu as pltpu

F32 = jnp.float32
BF16 = jnp.bfloat16
MESH = pl.DeviceIdType.MESH

D_MODEL = 1024
D_FF = 2816
HEADS = 8
HEAD_DIM = 128
POOL_WIDTH = 512
POOL_WINDOWS = (2, 4, 8, 16)
POOL_HALO = 16
N_CHIPS = 4
EPS = 1e-6
CHUNK = 64
MAIN_COLS = 4096
GATE_COLS = 2048
SHARD_IN_COLS = 1664

ADAM_LR = 0.001
ADAM_B1 = 0.9
ADAM_B2 = 0.999
ADAM_EPS = 1e-08
ADAM_WD = 0.01
ADAM_STEP = 10

VMEM_LIMIT = 56 * 1024 * 1024
WGRAD_IN_BLOCKS = 4
WGRAD_OUT_BLOCKS = 2
ADAM_BLOCKS = 4


def _params(semantics=None, vmem=VMEM_LIMIT):
    return pltpu.CompilerParams(dimension_semantics=semantics, vmem_limit_bytes=vmem)


def _dot(a, b):
    return jnp.dot(a, b, preferred_element_type=F32)


def _dot_nt(a, b):
    return lax.dot_general(a, b, (((1,), (1,)), ((), ())), preferred_element_type=F32)


def _dot_tn(a, b):
    return lax.dot_general(a, b, (((0,), (0,)), ((), ())), preferred_element_type=F32)


def _tri_sum(tri, x):
    hi = x.astype(BF16)
    lo = (x - hi.astype(F32)).astype(BF16)
    return _dot(tri, hi) + _dot(tri, lo)


def _sigmoid(x):
    return jax.nn.sigmoid(x)


def _resident(shape):
    zeros = (0,) * len(shape)
    return pl.BlockSpec(shape, lambda *_: zeros, pipeline_mode=pl.Buffered(1))


def _pick(shape, k):
    zeros = (0,) * (len(shape) - 1)
    return pl.BlockSpec((None,) + tuple(shape[1:]), lambda *_: (k,) + zeros, pipeline_mode=pl.Buffered(1))


def _rows(tm, cols, col_block=0):
    return pl.BlockSpec((tm, cols), lambda i: (i, col_block))


def _acc(shape):
    zeros = (0,) * len(shape)
    return pl.BlockSpec(shape, lambda *_: zeros)


def _rms(x):
    r = lax.rsqrt(jnp.mean(x * x, axis=-1, keepdims=True) + EPS)
    return r, x * r


def _rms_bwd(dn, n, r):
    return r * (dn - n * jnp.mean(dn * n, axis=-1, keepdims=True))


def _colsum(a):
    return jnp.sum(a, axis=0, keepdims=True)


ANY = pl.BlockSpec(memory_space=pl.ANY)


class _Rider:
    def __init__(self, inputs, out_shape, sems, phases):
        self.inputs, self.out_shape, self.sems, self.phases = list(inputs), list(out_shape), list(sems), list(phases)


def _hosted(riders, body, *, name, grid=(), in_specs, out_specs, out_shape, scratch_shapes=(), compiler_params=None):
    riders = [r for r in riders if r is not None]
    n_in, n_out, n_scr = len(in_specs), len(out_shape), len(scratch_shapes)
    n_steps = 1
    for g in grid:
        n_steps *= g

    def wrapped(*refs):
        pos = n_in
        ins = refs[:n_in]
        r_ins = []
        for r in riders:
            r_ins.append(refs[pos:pos + len(r.inputs)])
            pos += len(r.inputs)
        outs = refs[pos:pos + n_out]
        pos += n_out
        r_outs = []
        for r in riders:
            r_outs.append(refs[pos:pos + len(r.out_shape)])
            pos += len(r.out_shape)
        scr = refs[pos:pos + n_scr]
        pos += n_scr
        r_sems = []
        for r in riders:
            r_sems.append(refs[pos:pos + len(r.sems)])
            pos += len(r.sems)
        step = 0
        for axis in range(len(grid)):
            step = step * grid[axis] + pl.program_id(axis)

        def at_step(which, fn):
            if n_steps == 1:
                fn()
            else:
                pl.when(step == which)(fn)

        for r, ri, ro, rs in zip(riders, r_ins, r_outs, r_sems):
            for fraction, fn in r.phases:
                if fraction == 0:
                    at_step(0, functools.partial(fn, ri, ro, rs))
        body(*ins, *outs, *scr)
        for r, ri, ro, rs in zip(riders, r_ins, r_outs, r_sems):
            for fraction, fn in r.phases:
                if fraction > 0:
                    at_step(min(int(fraction * n_steps), n_steps - 1), functools.partial(fn, ri, ro, rs))

    call = pl.pallas_call(
        wrapped, name=name, grid=grid,
        in_specs=list(in_specs) + [ANY for r in riders for _ in r.inputs],
        out_specs=list(out_specs) + [ANY for r in riders for _ in r.out_shape],
        out_shape=list(out_shape) + [s for r in riders for s in r.out_shape],
        scratch_shapes=list(scratch_shapes) + [s for r in riders for s in r.sems],
        compiler_params=compiler_params)

    def run(*args):
        res = call(*args, *[a for r in riders for a in r.inputs])
        extras, pos = [], n_out
        for r in riders:
            extras.append(list(res[pos:pos + len(r.out_shape)]))
            pos += len(r.out_shape)
        return list(res[:n_out]), extras

    return run


def _ffn_fwd(x, g, ffnw, tag, tm, riders=()):
    t = x.shape[0]

    def body(x_ref, g_ref, w1_ref, w3_ref, w2_ref, xo_ref, a_ref, b_ref):
        xv = x_ref[...]
        _, n = _rms(xv)
        h = (n * g_ref[...]).astype(BF16)
        a = _dot_nt(h, w1_ref[...])
        b = _dot_nt(h, w3_ref[...])
        s = (a * _sigmoid(a) * b).astype(BF16)
        xo_ref[...] = xv + 0.5 * _dot(s, w2_ref[...])
        a_ref[...] = a.astype(BF16)
        b_ref[...] = b.astype(BF16)

    return _hosted(
        riders, body, name=f"ffn_fwd_{tag}", grid=(t // tm,),
        in_specs=[_rows(tm, D_MODEL), _resident((1, D_MODEL)), _pick(ffnw.shape, 0), _pick(ffnw.shape, 1),
                  _pick(ffnw.shape, 2)],
        out_specs=[_rows(tm, D_MODEL), _rows(tm, D_FF), _rows(tm, D_FF)],
        out_shape=[jax.ShapeDtypeStruct((t, D_MODEL), F32), jax.ShapeDtypeStruct((t, D_FF), BF16),
                   jax.ShapeDtypeStruct((t, D_FF), BF16)],
        compiler_params=_params(("arbitrary",)),
    )(x, g, ffnw, ffnw, ffnw)


def _ffn_bwd(dxo, x, g, a, b, ffnw, tag, tm, riders=()):
    t = x.shape[0]

    def body(dxo_ref, x_ref, g_ref, a_ref, b_ref, w1_ref, w3_ref, w2_ref, dx_ref, dab_ref, s_ref, h_ref, dxh_ref, dg_ref):
        @pl.when(pl.program_id(0) == 0)
        def _():
            dg_ref[...] = jnp.zeros_like(dg_ref)

        xv = x_ref[...]
        gv = g_ref[...]
        r, n = _rms(xv)
        h_ref[...] = (n * gv).astype(BF16)
        dxo_v = dxo_ref[...]
        dxh = (0.5 * dxo_v).astype(BF16)
        dxh_ref[...] = dxh
        ds = _dot_nt(dxh, w2_ref[...])
        av = a_ref[...].astype(F32)
        bv = b_ref[...].astype(F32)
        sg = _sigmoid(av)
        silu = av * sg
        s_ref[...] = (silu * bv).astype(BF16)
        da = (ds * bv * (sg * (1.0 + av * (1.0 - sg)))).astype(BF16)
        db = (ds * silu).astype(BF16)
        dab_ref[:, :D_FF] = da
        dab_ref[:, D_FF:] = db
        dh = _dot(da, w1_ref[...]) + _dot(db, w3_ref[...])
        dg_ref[...] += _colsum(dh * n)
        dx_ref[...] = dxo_v + _rms_bwd(dh * gv, n, r)

    return _hosted(
        riders, body, name=f"ffn_bwd_{tag}", grid=(t // tm,),
        in_specs=[_rows(tm, D_MODEL), _rows(tm, D_MODEL), _resident((1, D_MODEL)), _rows(tm, D_FF), _rows(tm, D_FF),
                  _pick(ffnw.shape, 0), _pick(ffnw.shape, 1), _pick(ffnw.shape, 2)],
        out_specs=[_rows(tm, D_MODEL), _rows(tm, 2 * D_FF), _rows(tm, D_FF), _rows(tm, D_MODEL), _rows(tm, D_MODEL),
                   _acc((1, D_MODEL))],
        out_shape=[jax.ShapeDtypeStruct((t, D_MODEL), F32), jax.ShapeDtypeStruct((t, 2 * D_FF), BF16),
                   jax.ShapeDtypeStruct((t, D_FF), BF16), jax.ShapeDtypeStruct((t, D_MODEL), BF16),
                   jax.ShapeDtypeStruct((t, D_MODEL), BF16), jax.ShapeDtypeStruct((1, D_MODEL), F32)],
        compiler_params=_params(("arbitrary",)),
    )(dxo, x, g, a, b, ffnw, ffnw, ffnw)


def _wgrad_body(n_token_tiles):
    def body(x_ref, dy_ref, o_ref, acc):
        k = pl.program_id(1)

        @pl.when(k == 0)
        def _():
            acc[...] = jnp.zeros_like(acc)

        acc[...] += _dot_tn(x_ref[...], dy_ref[...])

        @pl.when(k == n_token_tiles - 1)
        def _():
            o_ref[...] = acc[...].astype(BF16)

    return body


def _wgrad(xm, dy, out_blocks, name, tk, riders=()):
    t, m = xm.shape
    n = dy.shape[1]
    mb = m // out_blocks

    return _hosted(
        riders, _wgrad_body(t // tk), name=name, grid=(out_blocks, t // tk),
        in_specs=[pl.BlockSpec((tk, mb), lambda j, k: (k, j)), pl.BlockSpec((tk, n), lambda j, k: (k, 0))],
        out_specs=[pl.BlockSpec((None, mb, n), lambda j, k: (j, 0, 0))],
        out_shape=[jax.ShapeDtypeStruct((out_blocks, mb, n), BF16)],
        scratch_shapes=[pltpu.VMEM((mb, n), F32)],
        compiler_params=_params(("arbitrary", "arbitrary")),
    )(xm, dy)


def _wgrad_cols(xm, dy, out_blocks, name, tk, riders=()):
    t, m = xm.shape
    n = dy.shape[1]
    nb = n // out_blocks

    return _hosted(
        riders, _wgrad_body(t // tk), name=name, grid=(out_blocks, t // tk),
        in_specs=[pl.BlockSpec((tk, m), lambda j, k: (k, 0)), pl.BlockSpec((tk, nb), lambda j, k: (k, j))],
        out_specs=[pl.BlockSpec((None, m, nb), lambda j, k: (j, 0, 0))],
        out_shape=[jax.ShapeDtypeStruct((out_blocks, m, nb), BF16)],
        scratch_shapes=[pltpu.VMEM((m, nb), F32)],
        compiler_params=_params(("arbitrary", "arbitrary")),
    )(xm, dy)


def _mix_fwd(x1, g, winw, tm, riders=()):
    t = x1.shape[0]

    def body(x_ref, g_ref, w_ref, main_ref, pool_ref, gate_ref):
        _, n = _rms(x_ref[...])
        h = (n * g_ref[...]).astype(BF16)
        proj = jnp.concatenate([_dot(h, w_ref[j]) for j in range(N_CHIPS)], axis=1)
        main_ref[...] = proj[:, :MAIN_COLS]
        pool_ref[...] = proj[:, MAIN_COLS:MAIN_COLS + POOL_WIDTH]
        gate_ref[...] = proj[:, MAIN_COLS + POOL_WIDTH:]

    return _hosted(
        riders, body, name="mix_fwd", grid=(t // tm,),
        in_specs=[_rows(tm, D_MODEL), _resident((1, D_MODEL)), _resident(winw.shape)],
        out_specs=[_rows(tm, MAIN_COLS), _rows(tm, POOL_WIDTH), _rows(tm, GATE_COLS)],
        out_shape=[jax.ShapeDtypeStruct((t, MAIN_COLS), F32), jax.ShapeDtypeStruct((t, POOL_WIDTH), F32),
                   jax.ShapeDtypeStruct((t, GATE_COLS), F32)],
        compiler_params=_params(("arbitrary",)),
    )(x1, g, winw)


def _mix_bwd(dqfi, dog, du, dgates, dx2, x1, g, winw, tm, riders=()):
    t = x1.shape[0]
    cols = N_CHIPS * SHARD_IN_COLS

    def body(dqfi_ref, dog_ref, du_ref, dgt_ref, dx2_ref, x_ref, g_ref, w_ref, dx_ref, dproj_ref, h_ref, dg_ref):
        @pl.when(pl.program_id(0) == 0)
        def _():
            dg_ref[...] = jnp.zeros_like(dg_ref)

        dproj = jnp.concatenate([dqfi_ref[...], dog_ref[...], du_ref[...], dgt_ref[...]], axis=1)
        dproj_ref[...] = dproj
        dh = _dot_nt(dproj[:, :SHARD_IN_COLS], w_ref[0])
        for j in range(1, N_CHIPS):
            dh += _dot_nt(dproj[:, j * SHARD_IN_COLS:(j + 1) * SHARD_IN_COLS], w_ref[j])
        gv = g_ref[...]
        r, n = _rms(x_ref[...])
        h_ref[...] = (n * gv).astype(BF16)
        dg_ref[...] += _colsum(dh * n)
        dx_ref[...] = dx2_ref[...] + _rms_bwd(dh * gv, n, r)

    return _hosted(
        riders, body, name="mix_bwd", grid=(t // tm,),
        in_specs=[_rows(tm, 3 * D_MODEL), _rows(tm, D_MODEL), _rows(tm, POOL_WIDTH), _rows(tm, GATE_COLS),
                  _rows(tm, D_MODEL), _rows(tm, D_MODEL), _resident((1, D_MODEL)), _resident(winw.shape)],
        out_specs=[_rows(tm, D_MODEL), _rows(tm, cols), _rows(tm, D_MODEL), _acc((1, D_MODEL))],
        out_shape=[jax.ShapeDtypeStruct((t, D_MODEL), F32), jax.ShapeDtypeStruct((t, cols), BF16),
                   jax.ShapeDtypeStruct((t, D_MODEL), BF16), jax.ShapeDtypeStruct((1, D_MODEL), F32)],
        compiler_params=_params(("arbitrary",)),
    )(dqfi, dog, du, dgates, dx2, x1, g, winw)


def _lower_bound(lb_raw):
    l0 = lb_raw[0:1, :]
    l1 = lb_raw[1:2, :]
    m = jnp.maximum(l0, l1)
    e0 = jnp.exp(l0 - m)
    e1 = jnp.exp(l1 - m)
    return e0 / (e0 + e1)


def _head_slices():
    return [slice(h * HEAD_DIM, (h + 1) * HEAD_DIM) for h in range(HEADS)]


def _gates(qr, fr, lb, tril_b, first_half):
    sg = _sigmoid(fr)
    f = lb + (1.0 - lb) * sg
    k = 1.0 - f
    sq = _sigmoid(qr)
    q = qr * sq
    log_f = jnp.log(f)
    gc = _tri_sum(tril_b, log_f)
    gm = _colsum(jnp.where(first_half, log_f, 0.0))
    gl = _colsum(log_f)
    e_q = jnp.exp(gc - gm)
    e_k = jnp.exp(gm - gc)
    e_in = jnp.exp(gc)
    e_out = jnp.exp(gl - gc)
    return dict(sg=sg, f=f, k=k, sq=sq, q=q, e_q=e_q, e_k=e_k, e_in=e_in, e_out=e_out, e_last=jnp.exp(gl))


def _hgrn_fwd(main, lb_raw, tt, riders=()):
    t = main.shape[0]
    n_local = tt // CHUNK

    def body(q_ref, f_ref, i_ref, lb_ref, o_ref, st_ref, s_scr):
        @pl.when(pl.program_id(0) == 0)
        def _():
            s_scr[...] = jnp.zeros_like(s_scr)

        lb = _lower_bound(lb_ref[...])
        row = lax.broadcasted_iota(jnp.int32, (CHUNK, CHUNK), 0)
        col = lax.broadcasted_iota(jnp.int32, (CHUNK, CHUNK), 1)
        tril = row >= col
        tril_b = tril.astype(BF16)
        first_half = lax.broadcasted_iota(jnp.int32, (CHUNK, D_MODEL), 0) < CHUNK // 2
        heads = _head_slices()

        def chunk(c, carry):
            rows = pl.ds(pl.multiple_of(c * CHUNK, CHUNK), CHUNK)
            z = _gates(q_ref[rows, :], f_ref[rows, :], lb, tril_b, first_half)
            qt = (z["q"] * z["e_q"]).astype(BF16)
            kt = (z["k"] * z["e_k"]).astype(BF16)
            qg = (z["q"] * z["e_in"]).astype(BF16)
            kg = (z["k"] * z["e_out"]).astype(BF16)
            vb = i_ref[rows, :].astype(BF16)
            states = [s_scr[h] for h in range(HEADS)]
            for h in range(HEADS):
                st_ref[c, h] = states[h]
            raw = [_dot_nt(qt[:, sl], kt[:, sl]) for sl in heads]
            inter = [_dot_nt(qg[:, sl], states[h].astype(BF16)) for h, sl in enumerate(heads)]
            grown = [_dot_tn(vb[:, sl], kg[:, sl]) for sl in heads]
            scores = [jnp.where(tril, r, 0.0).astype(BF16) for r in raw]
            for h, sl in enumerate(heads):
                s_scr[h] = states[h] * z["e_last"][:, sl] + grown[h]
            o_ref[rows, :] = jnp.concatenate([_dot(scores[h], vb[:, sl]) + inter[h] for h, sl in enumerate(heads)], axis=1)
            return carry

        lax.fori_loop(0, n_local, chunk, 0, unroll=True)

    return _hosted(
        riders, body, name="hgrn_fwd", grid=(t // tt,),
        in_specs=[_rows(tt, D_MODEL, 0), _rows(tt, D_MODEL, 1), _rows(tt, D_MODEL, 2), _resident((2, D_MODEL))],
        out_specs=[_rows(tt, D_MODEL),
                   pl.BlockSpec((n_local, HEADS, HEAD_DIM, HEAD_DIM), lambda i: (i, 0, 0, 0))],
        out_shape=[jax.ShapeDtypeStruct((t, D_MODEL), F32),
                   jax.ShapeDtypeStruct((t // CHUNK, HEADS, HEAD_DIM, HEAD_DIM), F32)],
        scratch_shapes=[pltpu.VMEM((HEADS, HEAD_DIM, HEAD_DIM), F32)],
        compiler_params=_params(("arbitrary",)),
    )(main, main, main, lb_raw)


def _hgrn_bwd(main, lb_raw, states, do, tt, riders=()):
    t = main.shape[0]
    n_tiles = t // tt
    n_local = tt // CHUNK

    def rev(col_block):
        return pl.BlockSpec((tt, D_MODEL), lambda i: (n_tiles - 1 - i, col_block))

    def body(q_ref, f_ref, i_ref, lb_ref, st_ref, do_ref, dqfi_ref, dlb_ref, ds_scr, acc_scr):
        @pl.when(pl.program_id(0) == 0)
        def _():
            ds_scr[...] = jnp.zeros_like(ds_scr)
            acc_scr[...] = jnp.zeros_like(acc_scr)

        lb = _lower_bound(lb_ref[...])
        row = lax.broadcasted_iota(jnp.int32, (CHUNK, CHUNK), 0)
        col = lax.broadcasted_iota(jnp.int32, (CHUNK, CHUNK), 1)
        tril = row >= col
        tril_b = tril.astype(BF16)
        triu_b = (row <= col).astype(BF16)
        first_half = lax.broadcasted_iota(jnp.int32, (CHUNK, D_MODEL), 0) < CHUNK // 2
        heads = _head_slices()
        cat = functools.partial(jnp.concatenate, axis=1)

        def chunk(cc, carry):
            c = n_local - 1 - cc
            rows = pl.ds(pl.multiple_of(c * CHUNK, CHUNK), CHUNK)
            qr = q_ref[rows, :]
            z = _gates(qr, f_ref[rows, :], lb, tril_b, first_half)
            qt = (z["q"] * z["e_q"]).astype(BF16)
            kt = (z["k"] * z["e_k"]).astype(BF16)
            qg_f = z["q"] * z["e_in"]
            qg = qg_f.astype(BF16)
            kg_f = z["k"] * z["e_out"]
            kg = kg_f.astype(BF16)
            vb = i_ref[rows, :].astype(BF16)
            dob = do_ref[rows, :].astype(BF16)
            st = [st_ref[c, h] for h in range(HEADS)]
            dst = [ds_scr[h] for h in range(HEADS)]
            dst_b = [d.astype(BF16) for d in dst]
            raw = [_dot_nt(qt[:, sl], kt[:, sl]) for sl in heads]
            draw = [_dot_nt(dob[:, sl], vb[:, sl]) for sl in heads]
            dqg = [_dot(dob[:, sl], st[h].astype(BF16)) for h, sl in enumerate(heads)]
            dkg = [_dot(vb[:, sl], dst_b[h]) for h, sl in enumerate(heads)]
            dv_inter = [_dot_nt(kg[:, sl], dst_b[h]) for h, sl in enumerate(heads)]
            grown = [_dot_tn(dob[:, sl], qg[:, sl]) for sl in heads]
            scores = [jnp.where(tril, r, 0.0).astype(BF16) for r in raw]
            dscores = [jnp.where(tril, r, 0.0).astype(BF16) for r in draw]
            dqt = [_dot(dscores[h], kt[:, sl]) for h, sl in enumerate(heads)]
            dkt = [_dot_tn(dscores[h], qt[:, sl]) for h, sl in enumerate(heads)]
            dv = [_dot_tn(scores[h], dob[:, sl]) + dv_inter[h] for h, sl in enumerate(heads)]
            carry_in = cat([z["e_last"][:, sl] * _colsum(dst[h] * st[h]) for h, sl in enumerate(heads)])
            for h, sl in enumerate(heads):
                ds_scr[h] = dst[h] * z["e_last"][:, sl] + grown[h]
            dqt, dkt, dqg, dkg = cat(dqt), cat(dkt), cat(dqg), cat(dkg)
            carry_in += _colsum(dkg * kg_f)
            dq = dqt * z["e_q"] + dqg * z["e_in"]
            dk = dkt * z["e_k"] + dkg * z["e_out"]
            dgate = (qt.astype(F32) * dqt - kt.astype(F32) * dkt) + (qg_f * dqg - kg_f * dkg)
            dlogf = _tri_sum(triu_b, dgate) + carry_in
            df = dlogf / z["f"] - dk
            sg = z["sg"]
            sq = z["sq"]
            acc_scr[...] += _colsum(df * (1.0 - sg))
            dqfi_ref[rows, 0:D_MODEL] = (dq * (sq * (1.0 + qr * (1.0 - sq)))).astype(BF16)
            dqfi_ref[rows, D_MODEL:2 * D_MODEL] = (df * (1.0 - lb) * sg * (1.0 - sg)).astype(BF16)
            dqfi_ref[rows, 2 * D_MODEL:3 * D_MODEL] = cat(dv).astype(BF16)
            return carry

        lax.fori_loop(0, n_local, chunk, 0, unroll=True)
        d0 = acc_scr[...] * lb * (1.0 - lb)
        dlb_ref[0:1, :] = d0
        dlb_ref[1:2, :] = -d0

    return _hosted(
        riders, body, name="hgrn_bwd", grid=(n_tiles,),
        in_specs=[rev(0), rev(1), rev(2), _resident((2, D_MODEL)),
                  pl.BlockSpec((n_local, HEADS, HEAD_DIM, HEAD_DIM), lambda i: (n_tiles - 1 - i, 0, 0, 0)),
                  rev(0)],
        out_specs=[pl.BlockSpec((tt, 3 * D_MODEL), lambda i: (n_tiles - 1 - i, 0)), _acc((2, D_MODEL))],
        out_shape=[jax.ShapeDtypeStruct((t, 3 * D_MODEL), BF16), jax.ShapeDtypeStruct((2, D_MODEL), F32)],
        scratch_shapes=[pltpu.VMEM((HEADS, HEAD_DIM, HEAD_DIM), F32), pltpu.VMEM((1, D_MODEL), F32)],
        compiler_params=_params(("arbitrary",)),
    )(main, main, main, lb_raw, states, do)


def _head_norm(o):
    rs, ns = [], []
    for h in range(HEADS):
        oh = o[:, h * HEAD_DIM:(h + 1) * HEAD_DIM]
        r, n = _rms(oh)
        rs.append(jnp.broadcast_to(r, oh.shape))
        ns.append(n)
    return jnp.concatenate(rs, axis=1), jnp.concatenate(ns, axis=1)


def _head_norm_bwd(dn, n, r):
    outs = []
    for h in range(HEADS):
        sl = slice(h * HEAD_DIM, (h + 1) * HEAD_DIM)
        outs.append(_rms_bwd(dn[:, sl], n[:, sl], r[:, sl]))
    return jnp.concatenate(outs, axis=1)


def _window_counts(first_row, tm):
    pos = (first_row + 1 + lax.broadcasted_iota(jnp.int32, (tm, 1), 0)).astype(F32)
    return [jnp.minimum(pos, float(w)) for w in POOL_WINDOWS]


def _post_fwd(o, main, pool_r, gates, x1, onorm, pool_w, pool_scale, sqw, wbw, tm, riders=()):
    t = o.shape[0]
    ext_rows = tm + POOL_HALO

    def body(o_ref, og_ref, u_ref, gt_ref, x1_ref, on_ref, pw_ref, ps_ref, wa_ref, wout_ref, wb_ref,
             x2_ref, ya_ref, yb_ref, pooled_ref, ext):
        i = pl.program_id(0)

        @pl.when(i == 0)
        def _():
            ext[0:POOL_HALO, :] = jnp.zeros((POOL_HALO, POOL_WIDTH), F32)

        _, n = _head_norm(o_ref[...])
        og = og_ref[...]
        oa = (n * on_ref[...] * (og * _sigmoid(og))).astype(BF16)
        ya = _dot(oa, wa_ref[...])

        u = u_ref[...]
        ext[POOL_HALO:ext_rows, :] = u
        e = ext[...]
        counts = _window_counts(i * tm, tm)
        pooled = []
        for gidx, w in enumerate(POOL_WINDOWS):
            s = e[:, gidx * HEAD_DIM:(gidx + 1) * HEAD_DIM]
            shift = 1
            while shift < w:
                s = s + pltpu.roll(s, shift, axis=0)
                shift *= 2
            pooled.append(s[POOL_HALO:, :] / counts[gidx] - u[:, gidx * HEAD_DIM:(gidx + 1) * HEAD_DIM])
        ext[0:POOL_HALO, :] = ext[tm:ext_rows, :]
        pooled_b = [pg.astype(BF16) for pg in pooled]
        pooled_ref[...] = jnp.concatenate(pooled_b, axis=1)
        mixed = jnp.concatenate([_dot(pooled_b[gidx], pw_ref[gidx].astype(BF16)) for gidx in range(len(POOL_WINDOWS))],
                                axis=1) * ps_ref[...]
        mixed_b = mixed.astype(BF16)
        yb = jnp.concatenate([_dot(mixed_b, wb_ref[j]) for j in range(N_CHIPS)], axis=1)

        gt = gt_ref[...]
        y = _sigmoid(gt[:, :D_MODEL]) * ya + _sigmoid(gt[:, D_MODEL:]) * yb
        x2_ref[...] = x1_ref[...] + _dot(y.astype(BF16), wout_ref[...])
        ya_ref[...] = ya.astype(BF16)
        yb_ref[...] = yb.astype(BF16)

    return _hosted(
        riders, body, name="post_fwd", grid=(t // tm,),
        in_specs=[_rows(tm, D_MODEL), _rows(tm, D_MODEL, 3), _rows(tm, POOL_WIDTH), _rows(tm, GATE_COLS), _rows(tm, D_MODEL),
                  _resident((1, D_MODEL)), _resident(pool_w.shape), _resident((1, POOL_WIDTH)),
                  _pick(sqw.shape, 0), _pick(sqw.shape, 1), _resident(wbw.shape)],
        out_specs=[_rows(tm, D_MODEL), _rows(tm, D_MODEL), _rows(tm, D_MODEL), _rows(tm, POOL_WIDTH)],
        out_shape=[jax.ShapeDtypeStruct((t, D_MODEL), F32), jax.ShapeDtypeStruct((t, D_MODEL), BF16),
                   jax.ShapeDtypeStruct((t, D_MODEL), BF16), jax.ShapeDtypeStruct((t, POOL_WIDTH), BF16)],
        scratch_shapes=[pltpu.VMEM((ext_rows, POOL_WIDTH), F32)],
        compiler_params=_params(("arbitrary",)),
    )(o, main, pool_r, gates, x1, onorm, pool_w, pool_scale, sqw, sqw, wbw)


def _post_bwd(dx2, o, main, gates, ya, yb, pooled, onorm, pool_w, pool_scale, sqw, wbw, tm, riders=()):
    t = o.shape[0]
    n_tiles = t // tm
    ext_rows = tm + POOL_HALO
    n_groups = len(POOL_WINDOWS)

    def rev(cols, col_block=0):
        return pl.BlockSpec((tm, cols), lambda i: (n_tiles - 1 - i, col_block))

    def body(dx2_ref, o_ref, og_ref, gt_ref, ya_ref, yb_ref, pooled_ref, on_ref, pw_ref, ps_ref, wa_ref, wout_ref, wb_ref,
             do_ref, dog_ref, du_ref, dgt_ref, dwa_ref, dwout_ref, dwb_ref, dpw_ref, dps_ref, don_ref, ext):
        i = pl.program_id(0)

        @pl.when(i == 0)
        def _():
            ext[tm:ext_rows, :] = jnp.zeros((POOL_HALO, POOL_WIDTH), F32)
            for ref in (dwa_ref, dwout_ref, dwb_ref, dpw_ref, dps_ref, don_ref):
                ref[...] = jnp.zeros_like(ref)

        groups = [slice(gidx * HEAD_DIM, (gidx + 1) * HEAD_DIM) for gidx in range(n_groups)]
        shards = [slice(j * 256, (j + 1) * 256) for j in range(N_CHIPS)]
        dx2b = dx2_ref[...].astype(BF16)
        dy = _dot_nt(dx2b, wout_ref[...])
        pooled_b = pooled_ref[...]
        pm = jnp.concatenate([_dot(pooled_b[:, g], pw_ref[gidx].astype(BF16)) for gidx, g in enumerate(groups)], axis=1)
        gt = gt_ref[...]
        sga = _sigmoid(gt[:, :D_MODEL])
        sgb = _sigmoid(gt[:, D_MODEL:])
        ya = ya_ref[...].astype(F32)
        yb = yb_ref[...].astype(F32)
        y = (sga * ya + sgb * yb).astype(BF16)
        dya = (dy * sga).astype(BF16)
        dyb = (dy * sgb).astype(BF16)
        dgt_ref[:, :D_MODEL] = (dy * ya * sga * (1.0 - sga)).astype(BF16)
        dgt_ref[:, D_MODEL:] = (dy * yb * sgb * (1.0 - sgb)).astype(BF16)
        dwout_ref[...] += _dot_tn(y, dx2b)
        doa = _dot_nt(dya, wa_ref[...])
        dmixed = _dot_nt(dyb[:, shards[0]], wb_ref[0])
        for j in range(1, N_CHIPS):
            dmixed += _dot_nt(dyb[:, shards[j]], wb_ref[j])
        r, n = _head_norm(o_ref[...])
        onv = on_ref[...]
        og = og_ref[...]
        sog = _sigmoid(og)
        silu_og = og * sog
        normed = n * onv
        oa = (normed * silu_og).astype(BF16)
        dog_ref[...] = (doa * normed * (sog * (1.0 + og * (1.0 - sog)))).astype(BF16)
        dnormed = doa * silu_og
        don_ref[...] += _colsum(dnormed * n)
        do_ref[...] = _head_norm_bwd(dnormed * onv, n, r)
        psv = ps_ref[...]
        mixed_b = (pm * psv).astype(BF16)
        dps_ref[...] += _colsum(dmixed * pm)
        dpm = (dmixed * psv).astype(BF16)
        dwa_ref[...] += _dot_tn(oa, dya)
        for j in range(N_CHIPS):
            dwb_ref[j] += _dot_tn(mixed_b, dyb[:, shards[j]])
        counts = _window_counts((n_tiles - 1 - i) * tm, tm)
        dpooled = []
        for gidx, g in enumerate(groups):
            dpw_ref[gidx] += _dot_tn(pooled_b[:, g], dpm[:, g])
            dpooled.append(_dot_nt(dpm[:, g], pw_ref[gidx].astype(BF16)))
        ext[0:tm, :] = jnp.concatenate([dpooled[gidx] / counts[gidx] for gidx in range(n_groups)], axis=1)
        e = ext[...]
        du = []
        for gidx, w in enumerate(POOL_WINDOWS):
            s = e[:, gidx * HEAD_DIM:(gidx + 1) * HEAD_DIM]
            shift = 1
            while shift < w:
                s = s + pltpu.roll(s, ext_rows - shift, axis=0)
                shift *= 2
            du.append(s[:tm, :] - dpooled[gidx])
        ext[tm:ext_rows, :] = ext[0:POOL_HALO, :]
        du_ref[...] = jnp.concatenate(du, axis=1).astype(BF16)

    wa_shape = (D_MODEL, D_MODEL)
    return _hosted(
        riders, body, name="post_bwd", grid=(n_tiles,),
        in_specs=[rev(D_MODEL), rev(D_MODEL), rev(D_MODEL, 3), rev(GATE_COLS), rev(D_MODEL), rev(D_MODEL), rev(POOL_WIDTH),
                  _resident((1, D_MODEL)), _resident(pool_w.shape), _resident((1, POOL_WIDTH)),
                  _pick(sqw.shape, 0), _pick(sqw.shape, 1), _resident(wbw.shape)],
        out_specs=[rev(D_MODEL), rev(D_MODEL), rev(POOL_WIDTH), rev(GATE_COLS),
                   _acc(wa_shape), _acc(wa_shape), _acc(wbw.shape), _acc(pool_w.shape), _acc((1, POOL_WIDTH)),
                   _acc((1, D_MODEL))],
        out_shape=[jax.ShapeDtypeStruct((t, D_MODEL), F32), jax.ShapeDtypeStruct((t, D_MODEL), BF16),
                   jax.ShapeDtypeStruct((t, POOL_WIDTH), BF16), jax.ShapeDtypeStruct((t, GATE_COLS), BF16),
                   jax.ShapeDtypeStruct(wa_shape, F32), jax.ShapeDtypeStruct(wa_shape, F32),
                   jax.ShapeDtypeStruct(wbw.shape, F32), jax.ShapeDtypeStruct(pool_w.shape, F32),
                   jax.ShapeDtypeStruct((1, POOL_WIDTH), F32), jax.ShapeDtypeStruct((1, D_MODEL), F32)],
        scratch_shapes=[pltpu.VMEM((ext_rows, POOL_WIDTH), F32)],
        compiler_params=_params(("arbitrary",)),
    )(dx2, o, main, gates, ya, yb, pooled, onorm, pool_w, pool_scale, sqw, sqw, wbw)


def _tail(x3, p, target, g_ple, g_post, g_final, sqw, wpw, tm, riders=()):
    t = x3.shape[0]
    pd = p.shape[1]

    def body(x_ref, p_ref, tg_ref, g4_ref, g5_ref, g6_ref, wg_ref, wp_ref,
             dx_ref, loss_ref, dwg_ref, dwp_ref, dg4_ref, dg5_ref, dg6_ref):
        @pl.when(pl.program_id(0) == 0)
        def _():
            for ref in (loss_ref, dwg_ref, dwp_ref, dg4_ref, dg5_ref, dg6_ref):
                ref[...] = jnp.zeros_like(ref)

        x3v = x_ref[...]
        g4, g5, g6 = g4_ref[...], g5_ref[...], g6_ref[...]
        r4, n4 = _rms(x3v)
        h4 = (n4 * g4).astype(BF16)
        gate = _sigmoid(_dot(h4, wg_ref[...]))
        pb = p_ref[...].astype(BF16)
        r5, n5 = _rms(jnp.concatenate([_dot(pb, wp_ref[j]) for j in range(N_CHIPS)], axis=1))
        emb = n5 * g5
        r6, n6 = _rms(x3v + gate * emb)
        diff = n6 * g6 - tg_ref[...]
        loss_ref[...] += 0.5 * jnp.sum(jnp.mean(diff * diff, axis=-1, keepdims=True), axis=0, keepdims=True)
        dout = diff * (1.0 / D_MODEL)
        dg6_ref[...] += _colsum(dout * n6)
        dx4 = _rms_bwd(dout * g6, n6, r6)
        demb = dx4 * gate
        dg5_ref[...] += _colsum(demb * n5)
        dpre = _rms_bwd(demb * g5, n5, r5).astype(BF16)
        for j in range(N_CHIPS):
            dwp_ref[j] += _dot_tn(pb, dpre[:, j * pd:(j + 1) * pd])
        dz = (dx4 * emb * gate * (1.0 - gate)).astype(BF16)
        dwg_ref[...] += _dot_tn(h4, dz)
        dh4 = _dot_nt(dz, wg_ref[...])
        dg4_ref[...] += _colsum(dh4 * n4)
        dx_ref[...] = dx4 + _rms_bwd(dh4 * g4, n4, r4)

    sq_shape = (D_MODEL, D_MODEL)
    vec = (1, D_MODEL)
    return _hosted(
        riders, body, name="tail", grid=(t // tm,),
        in_specs=[_rows(tm, D_MODEL), _rows(tm, pd), _rows(tm, D_MODEL), _resident(vec), _resident(vec), _resident(vec),
                  _pick(sqw.shape, 2), _resident(wpw.shape)],
        out_specs=[_rows(tm, D_MODEL), _acc((1, 1)), _acc(sq_shape), _acc(wpw.shape), _acc(vec), _acc(vec), _acc(vec)],
        out_shape=[jax.ShapeDtypeStruct((t, D_MODEL), F32), jax.ShapeDtypeStruct((1, 1), F32),
                   jax.ShapeDtypeStruct(sq_shape, F32), jax.ShapeDtypeStruct(wpw.shape, F32),
                   jax.ShapeDtypeStruct(vec, F32), jax.ShapeDtypeStruct(vec, F32), jax.ShapeDtypeStruct(vec, F32)],
        compiler_params=_params(("arbitrary",)),
    )(x3, p, target, g_ple, g_post, g_final, sqw, wpw)


def _position():
    return lax.axis_index("x"), lax.axis_index("y"), lax.axis_index("c")


def _other_chips(x, y):
    return [(1 - x, y), (x, 1 - y), (1 - x, 1 - y)]


def _remote(src, dst, send_sems, recv_sems, k, device):
    return pltpu.make_async_remote_copy(src_ref=src, dst_ref=dst, send_sem=send_sems.at[k], recv_sem=recv_sems.at[k],
                                        device_id=device, device_id_type=MESH)


def _gather_rider(shards, forward_at):
    n = len(shards)

    def copies(ins, outs, sems):
        send_sems, recv_sems = sems
        x, y, c = _position()
        mine = 2 * x + y
        first, passed, arriving = [], [], []
        for k, (cx, cy) in enumerate(_other_chips(x, y)):
            theirs = 2 * cx + cy
            for a in range(n):
                first.append(_remote(ins[a].at[:, c], outs[a].at[:, mine, c], send_sems, recv_sems, k * n + a, (cx, cy, c)))
                block = outs[a].at[:, theirs, c]
                passed.append(_remote(block, block, send_sems, recv_sems, (3 + k) * n + a, (x, y, 1 - c)))
                other = outs[a].at[:, theirs, 1 - c]
                arriving.append(_remote(other, other, send_sems, recv_sems, (3 + k) * n + a, (x, y, 1 - c)))
        return first, passed, arriving

    return _Rider(shards, [jax.ShapeDtypeStruct((s.shape[0], N_CHIPS) + s.shape[1:], s.dtype) for s in shards],
                  [pltpu.SemaphoreType.DMA((6 * n,)), pltpu.SemaphoreType.DMA((6 * n,))], _gather_phases(copies, forward_at))


def _gather_phases(copies, forward_at):
    def begin(ins, outs, sems):
        for cp in copies(ins, outs, sems)[0]:
            cp.start()

    def forward(ins, outs, sems):
        first, passed, _ = copies(ins, outs, sems)
        for got, cp in zip(first, passed):
            got.wait_recv()
            cp.start()

    def finish(ins, outs, sems):
        first, passed, arriving = copies(ins, outs, sems)
        for cp in arriving:
            cp.wait_recv()
        for cp in first + passed:
            cp.wait_send()

    return [(0, begin), (forward_at, forward), (1, finish)]


def _relay_gather_rider(shards):
    n = len(shards)
    kinds = 8

    def copies(ins, outs, sems):
        send_sems, recv_sems = sems
        x, y, c = _position()
        mine, nx, ny, diag = 2 * x + y, 2 * (1 - x) + y, 2 * x + 1 - y, 2 * (1 - x) + 1 - y
        to_x, to_y, sibling = (1 - x, y, c), (x, 1 - y, c), (x, y, 1 - c)
        sent, relayed, passed, arriving = [], [], [], []
        for a in range(n):
            hq = ins[a].shape[2] // 2
            quarter = lambda chip, half, q: outs[a].at[:, chip, half, pl.ds(q * hq, hq)]
            rc = functools.partial(_remote, send_sems=send_sems, recv_sems=recv_sems)
            sent += [rc(ins[a].at[:, c], outs[a].at[:, mine, c], k=kinds * a, device=to_x),
                     rc(ins[a].at[:, c], outs[a].at[:, mine, c], k=kinds * a + 1, device=to_y)]
            got = [rc(outs[a].at[:, nx, c], outs[a].at[:, nx, c], k=kinds * a, device=to_x),
                   rc(outs[a].at[:, ny, c], outs[a].at[:, ny, c], k=kinds * a + 1, device=to_y),
                   rc(quarter(diag, c, 0), quarter(diag, c, 0), k=kinds * a + 2, device=to_y),
                   rc(quarter(diag, c, 1), quarter(diag, c, 1), k=kinds * a + 3, device=to_x)]
            relayed += [rc(quarter(nx, c, 0), quarter(nx, c, 0), k=kinds * a + 2, device=to_y),
                        rc(quarter(ny, c, 1), quarter(ny, c, 1), k=kinds * a + 3, device=to_x)]
            blocks = [lambda h: outs[a].at[:, nx, h], lambda h: outs[a].at[:, ny, h],
                      lambda h: quarter(diag, h, 0), lambda h: quarter(diag, h, 1)]
            passed += [(got[i], rc(b(c), b(c), k=kinds * a + 4 + i, device=sibling)) for i, b in enumerate(blocks)]
            arriving += [rc(b(1 - c), b(1 - c), k=kinds * a + 4 + i, device=sibling) for i, b in enumerate(blocks)]
        return sent, relayed, passed, arriving

    def begin(ins, outs, sems):
        for cp in copies(ins, outs, sems)[0]:
            cp.start()

    def forward(ins, outs, sems):
        _, relayed, passed, _ = copies(ins, outs, sems)
        for i, (got, onward) in enumerate(passed):
            got.wait_recv()
            if i % 4 < 2:
                relayed[2 * (i // 4) + i % 4].start()
            onward.start()

    def finish(ins, outs, sems):
        sent, relayed, passed, arriving = copies(ins, outs, sems)
        for cp in arriving:
            cp.wait_recv()
        for cp in sent + relayed + [onward for _, onward in passed]:
            cp.wait_send()

    return _Rider(shards, [jax.ShapeDtypeStruct((s.shape[0], N_CHIPS) + s.shape[1:], s.dtype) for s in shards],
                  [pltpu.SemaphoreType.DMA((kinds * n,)), pltpu.SemaphoreType.DMA((kinds * n,))],
                  [(0, begin), (0.5, forward), (1, finish)])


def _with_own(gathered, shard, slot):
    return lax.dynamic_update_slice(gathered, shard[:, None], (0, slot, 0, 0, 0))


def _exchange_rider(arrays, out_shape, n_copies, transfers, n_local=0):
    def copies(ins, outs, sems):
        send_sems, recv_sems, local_sems = sems
        remote, local = transfers(ins, outs)
        return ([_remote(src, dst, send_sems, recv_sems, i, dev) for i, (src, dst, dev) in enumerate(remote)],
                [pltpu.make_async_copy(src, dst, local_sems.at[i]) for i, (src, dst) in enumerate(local)])

    def begin(ins, outs, sems):
        remote, local = copies(ins, outs, sems)
        for cp in remote + local:
            cp.start()

    def finish(ins, outs, sems):
        remote, local = copies(ins, outs, sems)
        for cp in remote:
            cp.wait_recv()
        for cp in remote:
            cp.wait_send()
        for cp in local:
            cp.wait()

    return _Rider(arrays, out_shape,
                  [pltpu.SemaphoreType.DMA((n_copies,)), pltpu.SemaphoreType.DMA((n_copies,)),
                   pltpu.SemaphoreType.DMA((max(n_local, 1),))],
                  [(0, begin), (1, finish)])


def _pair_rider(partials):
    def transfers(ins, outs):
        x, y, c = _position()
        return [(ins[a].at[:, :, 1 - c], outs[a], (x, y, 1 - c)) for a in range(len(partials))], []

    shapes = [jax.ShapeDtypeStruct(g.shape[:2] + g.shape[3:], g.dtype) for g in partials]
    return _exchange_rider(partials, shapes, len(partials), transfers)


def _chips_rider(sums):
    n = len(sums)

    def transfers(ins, outs):
        x, y, c = _position()
        return [(ins[a].at[:, 2 * cx + cy], outs[a].at[:, k], (cx, cy, c))
                for k, (cx, cy) in enumerate(_other_chips(x, y)) for a in range(n)], []

    shapes = [jax.ShapeDtypeStruct((q.shape[0], 3) + q.shape[2:], q.dtype) for q in sums]
    return _exchange_rider(sums, shapes, 3 * n, transfers)


def _share_rider(halves):
    def transfers(ins, outs):
        x, y, c = _position()
        return [(ins[a], outs[a], (x, y, 1 - c)) for a in range(len(halves))], []

    return _exchange_rider(halves, [jax.ShapeDtypeStruct(h.shape, h.dtype) for h in halves], len(halves), transfers)


def _small_rider(pack):
    flips = [(fx, fy, fc) for fx in (0, 1) for fy in (0, 1) for fc in (0, 1)][1:]

    def transfers(ins, outs):
        x, y, c = _position()
        slot = outs[0].at[4 * x + 2 * y + c]
        flip = lambda v, f: v + f - 2 * v * f
        return [(ins[0], slot, (flip(x, fx), flip(y, fy), flip(c, fc))) for fx, fy, fc in flips], [(ins[0], slot)]

    return _exchange_rider([pack], [jax.ShapeDtypeStruct((8,) + pack.shape, pack.dtype)], len(flips), transfers, n_local=1)


IN_HBM = pl.BlockSpec(memory_space=pltpu.HBM)
IN_SEM = pl.BlockSpec(memory_space=pltpu.SEMAPHORE)
SPLIT_COPY = pltpu.CompilerParams(has_side_effects=pltpu.SideEffectType.DATAFLOW_SIDE_EFFECTING)


def _start_copies(sources, landing_shapes, n_copies, plan, name):
    n_src, n_land = len(sources), len(landing_shapes)

    def body(*refs):
        srcs, lands, outs = refs[:n_src], refs[n_src:n_src + n_land], refs[n_src + n_land:]
        send, recv, token = outs[:n_copies], outs[n_copies:2 * n_copies], outs[-1]
        for i, (src, dst, device) in enumerate(plan(srcs, lands)):
            pltpu.make_async_remote_copy(src_ref=src, dst_ref=dst, send_sem=send[i], recv_sem=recv[i], device_id=device,
                                         device_id_type=MESH).start()
        token[...] = jnp.zeros_like(token)

    arrays = [pltpu.with_memory_space_constraint(a, pltpu.HBM) for a in sources]
    arrays += [pltpu.with_memory_space_constraint(lax.empty(s.shape, s.dtype), pltpu.HBM) for s in landing_shapes]
    res = pl.pallas_call(
        body, name=name,
        out_shape=[pltpu.SemaphoreType.DMA(())] * (2 * n_copies) + [pltpu.HBM(a.shape, a.dtype) for a in arrays]
        + [jax.ShapeDtypeStruct((8, 128), F32)],
        in_specs=[IN_HBM] * len(arrays),
        out_specs=[IN_SEM] * (2 * n_copies) + [IN_HBM] * len(arrays) + [pl.BlockSpec(memory_space=pltpu.VMEM)],
        input_output_aliases={i: 2 * n_copies + i for i in range(len(arrays))},
        compiler_params=SPLIT_COPY,
    )(*arrays)
    sems, rest = res[:2 * n_copies], res[2 * n_copies:]
    return sems, rest[:n_src], rest[n_src:n_src + n_land], rest[-1]


def _wait_copies(started, n_copies, plan, after, name):
    sems, sources, landings, _ = started
    n_src, n_land = len(sources), len(landings)

    def body(*refs):
        srcs, lands = refs[:n_src], refs[n_src:n_src + n_land]
        sem_refs = refs[n_src + n_land:n_src + n_land + 2 * n_copies]
        send, recv = sem_refs[:n_copies], sem_refs[n_copies:]
        for i, (src, dst, device) in enumerate(plan(srcs, lands)):
            cp = pltpu.make_async_remote_copy(src_ref=src, dst_ref=dst, send_sem=send[i], recv_sem=recv[i], device_id=device,
                                              device_id_type=MESH)
            cp.wait_send()
            cp.wait_recv()

    arrays = list(sources) + list(landings)
    res = pl.pallas_call(
        body, name=name, out_shape=[pltpu.HBM(a.shape, a.dtype) for a in arrays],
        in_specs=[IN_HBM] * len(arrays) + [IN_SEM] * (2 * n_copies) + [ANY], out_specs=[IN_HBM] * len(arrays),
        input_output_aliases={i: i for i in range(len(arrays))},
        compiler_params=SPLIT_COPY,
    )(*arrays, *sems, after)
    return res[n_src:]


def _chips_plan(n):
    def plan(srcs, lands):
        x, y, c = _position()
        return [(srcs[a].at[:, 2 * cx + cy], lands[a].at[:, k], (cx, cy, c))
                for k, (cx, cy) in enumerate(_other_chips(x, y)) for a in range(n)]
    return plan


def _alone(rider, name, after=()):
    return _hosted([rider], lambda *refs: None, name=name, in_specs=[ANY] * len(after), out_specs=[], out_shape=[])(*after)[1][0]


def _add_pair(mine, theirs, c, tag):
    n = len(mine)

    def body(c_ref, *refs):
        for a in range(n):
            refs[2 * n + a][...] = (refs[2 * a][...].astype(F32) + refs[2 * a + 1][...].astype(F32)).astype(BF16)

    in_specs, out_specs = [], []
    for got in theirs:
        l, _, hr, cols = got.shape
        in_specs += [pl.BlockSpec((l, None, None, hr, cols), lambda j, s: (0, j, s[0], 0, 0)),
                     pl.BlockSpec((l, None, hr, cols), lambda j, s: (0, j, 0, 0))]
        out_specs.append(pl.BlockSpec((l, None, hr, cols), lambda j, s: (0, j, 0, 0)))
    return pl.pallas_call(
        body, name=f"add_pair_{tag}",
        grid_spec=pltpu.PrefetchScalarGridSpec(num_scalar_prefetch=1, grid=(N_CHIPS,), in_specs=in_specs, out_specs=out_specs),
        out_shape=[jax.ShapeDtypeStruct(got.shape, BF16) for got in theirs],
        compiler_params=_params(("parallel",)),
    )(c.reshape(1), *[a for pair in zip(mine, theirs) for a in pair])


def _add_chips(parts, received, mine, tag):
    n = len(parts)

    def body(j_ref, *refs):
        for a in range(n):
            acc = refs[2 * a][...].astype(F32)
            for k in range(3):
                acc += refs[2 * a + 1][:, k].astype(F32)
            refs[2 * n + a][...] = acc

    in_specs, out_specs, out_shape = [], [], []
    for got in received:
        l, _, hr, cols = got.shape
        in_specs += [pl.BlockSpec((l, None, hr // 2, cols), lambda i, s: (0, s[0], i, 0)),
                     pl.BlockSpec((l, 3, hr // 2, cols), lambda i, s: (0, 0, i, 0))]
        out_specs.append(pl.BlockSpec((l, hr // 2, cols), lambda i, s: (0, i, 0)))
        out_shape.append(jax.ShapeDtypeStruct((l, hr, cols), F32))
    return pl.pallas_call(
        body, name=f"add_chips_{tag}",
        grid_spec=pltpu.PrefetchScalarGridSpec(num_scalar_prefetch=1, grid=(2,), in_specs=in_specs, out_specs=out_specs),
        out_shape=out_shape,
        compiler_params=_params(("parallel",)),
    )(mine.reshape(1), *[a for pair in zip(parts, received) for a in pair])


def _adam_update(w, g, m, v):
    m2 = ADAM_B1 * m + (1.0 - ADAM_B1) * g
    v2 = ADAM_B2 * v + (1.0 - ADAM_B2) * jnp.square(g)
    m_hat = m2 / (1.0 - ADAM_B1 ** ADAM_STEP)
    v_hat = v2 / (1.0 - ADAM_B2 ** ADAM_STEP)
    return -ADAM_LR * (m_hat / (jnp.sqrt(v_hat) + ADAM_EPS) + ADAM_WD * w), m2, v2


def _adamw_group(items, tag, after=()):
    n = len(items)

    def body(*refs):
        ins, outs = refs[:5 * n], refs[5 * n + len(after):]
        mine = pl.program_id(0) == lax.axis_index("c")
        for a in range(n):
            w_ref, own_ref, other_ref, m_ref, v_ref = ins[5 * a:5 * a + 5]
            g_ref, d_ref, nm_ref, nv_ref = outs[4 * a:4 * a + 4]
            gv = jnp.where(mine, own_ref[...], other_ref[...])
            g_ref[...] = gv
            d_ref[...], nm_ref[...], nv_ref[...] = _adam_update(w_ref[...], gv, m_ref[...], v_ref[...])

    in_specs, out_specs, out_shape, args = [], [], [], []
    for w, own, other, m, v in items:
        _, hr, cols = w.shape
        tr = hr // ADAM_BLOCKS
        full = pl.BlockSpec((None, tr, cols), lambda h, i: (h, i, 0))
        half = pl.BlockSpec((tr, cols), lambda h, i: (i, 0))
        in_specs += [full, half, half, full, full]
        out_specs += [full] * 4
        out_shape += [jax.ShapeDtypeStruct((2, hr, cols), F32)] * 4
        args += [w, own, other, m, v]
    outs = pl.pallas_call(body, name=f"adamw_{tag}", grid=(2, ADAM_BLOCKS), in_specs=in_specs + [ANY] * len(after),
                          out_specs=out_specs, out_shape=out_shape,
                          compiler_params=_params(("parallel", "parallel")))(*args, *after)
    return [outs[4 * a:4 * a + 4] for a in range(n)]


def _adamw_small(w, gathered, m, v, shapes):
    n_rows = w.shape[0]
    places = []
    for i, name in enumerate(VECTOR_PARAMS):
        places.append((name, i * TILE_ROWS, 1 if len(shapes[name]) == 1 else shapes[name][0], shapes[name][-1]))
    places.append(("pool_w", len(VECTOR_PARAMS) * TILE_ROWS, n_rows - len(VECTOR_PARAMS) * TILE_ROWS, D_MODEL))

    def body(w_ref, g_ref, m_ref, v_ref, loss_ref, *rest):
        outs, (sum_scr, d_scr, nm_scr, nv_scr) = rest[:-4], rest[-4:]
        total = g_ref[0]
        for i in range(1, g_ref.shape[0]):
            total += g_ref[i]
        sum_scr[...] = total
        gv = sum_scr[0:n_rows, :]
        d_scr[...], nm_scr[...], nv_scr[...] = _adam_update(w_ref[...], gv, m_ref[...], v_ref[...])
        loss_ref[...] = sum_scr[n_rows:n_rows + 1, 0:1]
        for k, (_, first, rows, cols) in enumerate(places):
            for j, scr in enumerate((sum_scr, d_scr, nm_scr, nv_scr)):
                outs[4 * k + j][...] = scr[first:first + rows, 0:cols]

    out_shape = [jax.ShapeDtypeStruct((1, 1), F32)]
    for _, _, rows, cols in places:
        out_shape += [jax.ShapeDtypeStruct((rows, cols), F32)] * 4
    res = pl.pallas_call(
        body, name="adamw_small", out_shape=out_shape,
        scratch_shapes=[pltpu.VMEM(gathered.shape[1:], F32)] + [pltpu.VMEM(w.shape, F32)] * 3,
        compiler_params=_params())(w, gathered, m, v)
    return res[0], {name: res[1 + 4 * k:5 + 4 * k] for k, (name, _, _, _) in enumerate(places)}


VECTOR_PARAMS = ("ffn1_norm", "mix_norm", "hgrn_lb", "hgrn_onorm", "ffn2_norm", "ple_norm", "ple_post_norm", "final_norm",
                 "pool_scale")
ALL_PARAMS = ("ffn1_norm", "ffn1_w1", "ffn1_w3", "ffn1_w2", "mix_norm", "w_in", "hgrn_lb", "hgrn_onorm", "w_branch_a",
              "pool_w", "pool_scale", "w_branch_b", "w_out", "ffn2_norm", "ffn2_w1", "ffn2_w3", "ffn2_w2", "ple_norm",
              "ple_w_gate", "ple_w_proj", "ple_post_norm", "final_norm")
TILE_ROWS = 8


def _pack_small(values, loss=None):
    tile = lambda a: jnp.pad(a, ((0, TILE_ROWS - a.shape[0]), (0, D_MODEL - a.shape[1])))
    parts = [tile(values[name].reshape(-1, values[name].shape[-1])) for name in VECTOR_PARAMS]
    parts.append(values["pool_w"].reshape(-1, D_MODEL))
    if loss is not None:
        parts.append(tile(loss))
    return jnp.concatenate(parts, axis=0)


def _halved(a, lead):
    return a.reshape(lead, 2, -1, a.shape[-1])


def _shard_halves(a, lead):
    return a.reshape(lead, N_CHIPS, 2, -1, a.shape[-1])


REDUCED_TRANSPOSED = ("ffn1_w1", "ffn1_w3", "ffn2_w1", "ffn2_w3")


def _entries(arrays):
    return [a[i] for a in arrays for i in range(a.shape[0])]


def _adam_items(names, own, other, w, m, v):
    items = []
    for name, g_own, g_other in zip(names, _entries(own), _entries(other)):
        view = (lambda a: _halved(a[0].T, 1)[0]) if name in REDUCED_TRANSPOSED else (lambda a: _halved(a, 1)[0])
        items.append((view(w[name]), g_own, g_other, view(m[name]), view(v[name])))
    return items


def _adam_store(names, results, w, out):
    for name, res in zip(names, results):
        shape = w[name].shape
        if name in REDUCED_TRANSPOSED:
            back = [a.reshape(shape[2], shape[1]).T.reshape(shape) for a in res]
        else:
            back = [a.reshape(shape) for a in res]
        out["grad"][name], out["delta"][name], out["new_m"][name], out["new_v"][name] = back


def kernel(x, p, ffn1_norm, ffn1_w1, ffn1_w3, ffn1_w2, mix_norm, w_in, hgrn_lb, hgrn_onorm, w_branch_a, pool_w, pool_scale, w_branch_b, w_out, ffn2_norm, ffn2_w1, ffn2_w3, ffn2_w2, ple_norm, ple_w_gate, ple_w_proj, ple_post_norm, final_norm, loss_target, m_ffn1_norm, m_ffn1_w1, m_ffn1_w3, m_ffn1_w2, m_mix_norm, m_w_in, m_hgrn_lb, m_hgrn_onorm, m_w_branch_a, m_pool_w, m_pool_scale, m_w_branch_b, m_w_out, m_ffn2_norm, m_ffn2_w1, m_ffn2_w3, m_ffn2_w2, m_ple_norm, m_ple_w_gate, m_ple_w_proj, m_ple_post_norm, m_final_norm, v_ffn1_norm, v_ffn1_w1, v_ffn1_w3, v_ffn1_w2, v_mix_norm, v_w_in, v_hgrn_lb, v_hgrn_onorm, v_w_branch_a, v_pool_w, v_pool_scale, v_w_branch_b, v_w_out, v_ffn2_norm, v_ffn2_w1, v_ffn2_w3, v_ffn2_w2, v_ple_norm, v_ple_w_gate, v_ple_w_proj, v_ple_post_norm, v_final_norm):
    args = dict(locals())
    w = {name: args[name] for name in ALL_PARAMS}
    m = {name: args["m_" + name] for name in ALL_PARAMS}
    v = {name: args["v_" + name] for name in ALL_PARAMS}
    cx, cy, cc = _position()
    chip = (2 * cx + cy).astype(jnp.int32)
    core = cc.astype(jnp.int32)
    xs, ps, target = x[0], p[0, 0], loss_target[0]
    t = xs.shape[0]
    tm = min(256, t)
    tm_ffn = min(512, t)
    tt = min(512, t)
    tk = min(2048, t)
    small = {name: w[name] for name in VECTOR_PARAMS}
    small["final_norm"] = w["final_norm"].reshape(1, D_MODEL)
    pool_w0 = w["pool_w"][0]

    ffn_shard = lambda i: _halved(jnp.stack([w[f"ffn{i}_w1"][0].T, w[f"ffn{i}_w3"][0].T, w[f"ffn{i}_w2"][0]]).astype(BF16), 3)
    sq_shard = _halved(jnp.stack([w["w_branch_a"][0], w["w_out"][0], w["ple_w_gate"][0]]).astype(BF16), 3)
    win_shard, wb_shard, wp_shard = (_halved(w[n].astype(BF16), 1) for n in ("w_in", "w_branch_b", "ple_w_proj"))

    ffn1_shard, ffn2_shard = ffn_shard(1), ffn_shard(2)
    (ffn1w,) = _alone(_relay_gather_rider([ffn1_shard]), "gather_ffn1")
    ffn1w = _with_own(ffn1w, ffn1_shard, chip).reshape(3, D_FF, D_MODEL)
    (x1, a1, b1), ((winw,),) = _ffn_fwd(xs, small["ffn1_norm"], ffn1w, 1, tm_ffn, [_gather_rider([win_shard], 0.6)])
    winw = _with_own(winw, win_shard, chip).reshape(N_CHIPS, D_MODEL, SHARD_IN_COLS)
    (main, pool_r, gates), ((sqw, wbw, wpw),) = _mix_fwd(x1, small["mix_norm"], winw, tm,
                                                          [_gather_rider([sq_shard, wb_shard, wp_shard], 0.5)])
    sqw = _with_own(sqw, sq_shard, chip).reshape(3, D_MODEL, D_MODEL)
    wbw = _with_own(wbw, wb_shard, chip).reshape(N_CHIPS, POOL_WIDTH, -1)
    wpw = _with_own(wpw, wp_shard, chip).reshape(N_CHIPS, ps.shape[1], -1)
    (o, states), ((ffn2w,),) = _hgrn_fwd(main, small["hgrn_lb"], tt, [_gather_rider([ffn2_shard], 0.7)])
    ffn2w = _with_own(ffn2w, ffn2_shard, chip).reshape(3, D_FF, D_MODEL)
    (x2, ya, yb, pooled), _ = _post_fwd(o, main, pool_r, gates, x1, small["hgrn_onorm"], pool_w0, small["pool_scale"], sqw,
                                       wbw, tm)
    (x3, a2, b2), _ = _ffn_fwd(x2, small["ffn2_norm"], ffn2w, 2, tm_ffn)
    (dx3, loss, d_wg, d_wp, d_ple, d_post, d_final), _ = _tail(
        x3, ps, target, small["ple_norm"], small["ple_post_norm"], small["final_norm"], sqw, wpw, tm_ffn)

    add_pairs = lambda parts, got, group: _add_pair(parts, got, core, group)
    add_chips = lambda sums, got, group: _add_chips(sums, got, chip, group)
    names1 = ("ffn2_w1", "ffn2_w3", "ffn2_w2", "ple_w_gate", "ple_w_proj")
    names2 = ("w_branch_a", "w_out", "w_branch_b")
    names3 = ("w_in",)
    names4 = ("ffn1_w1", "ffn1_w3")
    names5 = ("ffn1_w2",)
    tags1, tags2, tags3, tags4, tags5 = "ffn2", "branches", "w_in", "ffn1_in", "ffn1_out"

    (dx2, dab2, s2, h3, dxh2, d_ffn2_norm), _ = _ffn_bwd(dx3, x2, small["ffn2_norm"], a2, b2, ffn2w, 2, tm)
    (d_w13_2,), _ = _wgrad(dab2, h3, WGRAD_IN_BLOCKS, "wgrad_ffn2_in", tk)
    (d_w2_2,), _ = _wgrad(s2, dxh2, WGRAD_OUT_BLOCKS, "wgrad_ffn2_out", tk)
    part1 = [_shard_halves(d_w13_2, 2), _shard_halves(d_w2_2, 1), _shard_halves(d_wg, 1), _shard_halves(d_wp, 1)]
    (do, dog, du, dgates, d_wa, d_wout, d_wb, d_pool_w, d_pool_scale, d_onorm), (sib1,) = _post_bwd(
        dx2, o, main, gates, ya, yb, pooled, small["hgrn_onorm"], pool_w0, small["pool_scale"], sqw, wbw, tm,
        [_pair_rider(part1)])
    sums1 = add_pairs(part1, sib1, tags1)
    part2 = [_shard_halves(d_wa, 1), _shard_halves(d_wout, 1), _shard_halves(d_wb, 1)]
    (dqfi, d_lb), (got1, sib2) = _hgrn_bwd(main, small["hgrn_lb"], states, do, tt, [_chips_rider(sums1), _pair_rider(part2)])
    own1 = add_chips(sums1, got1, tags1)
    sums2 = add_pairs(part2, sib2, tags2)
    (dx1, dproj, h2, d_mix_norm), (other1, got2) = _mix_bwd(dqfi, dog, du, dgates, dx2, x1, small["mix_norm"], winw, tm,
                                                            [_share_rider(own1), _chips_rider(sums2)])
    own2 = add_chips(sums2, got2, tags2)
    (d_win,), (other2,) = _wgrad_cols(h2, dproj, N_CHIPS, "wgrad_in", tk, [_share_rider(own2)])
    part3 = [_shard_halves(d_win, 1)]
    (dx, dab1, s1, h1, dxh1, d_ffn1_norm), _ = _ffn_bwd(dx1, xs, small["ffn1_norm"], a1, b1, ffn1w, 1, tm)
    vecs = dict(ffn1_norm=d_ffn1_norm, mix_norm=d_mix_norm, hgrn_lb=d_lb, hgrn_onorm=d_onorm, ffn2_norm=d_ffn2_norm,
                ple_norm=d_ple, ple_post_norm=d_post, final_norm=d_final, pool_scale=d_pool_scale, pool_w=d_pool_w)
    (d_w2_1,), (sib3, (small_all,)) = _wgrad(s1, dxh1, WGRAD_OUT_BLOCKS, "wgrad_ffn1_out", tk,
                                             [_pair_rider(part3), _small_rider(_pack_small(vecs, loss))])
    sums3 = add_pairs(part3, sib3, tags3)
    part5 = [_shard_halves(d_w2_1, 1)]
    (d_w13_1,), (got3, sib5) = _wgrad(dab1, h1, WGRAD_IN_BLOCKS, "wgrad_ffn1_in", tk,
                                      [_chips_rider(sums3), _pair_rider(part5)])
    own3 = add_chips(sums3, got3, tags3)
    sums5 = add_pairs(part5, sib5, tags5)
    part4 = [_shard_halves(d_w13_1, 2)]
    landing = lambda a: jax.ShapeDtypeStruct((a.shape[0], 3) + a.shape[2:], a.dtype)

    def sibling_plan(srcs, lands):
        x, y, c = _position()
        return [(srcs[-1], lands[-1], (x, y, 1 - c))]

    plan_a = lambda srcs, lands: _chips_plan(1)(srcs[:1], lands[:1]) + sibling_plan(srcs, lands)
    started_a = _start_copies([sums5[0], own3[0]], [landing(sums5[0]), own3[0]], 4, plan_a, "start_a")
    sib4 = _alone(_pair_rider(part4), "pair_last", [started_a[3]])
    sums4 = add_pairs(part4, sib4, tags4)
    started_b = _start_copies([sums4[0]], [landing(sums4[0])], 3, _chips_plan(1), "start_b")

    out = dict(grad={}, delta={}, new_m={}, new_v={})
    results = _adamw_group(_adam_items(names1 + names2, own1 + own2, other1 + other2, w, m, v), "early",
                           [started_a[3], started_b[3]])
    _adam_store(names1 + names2, results, w, out)
    got5, other3 = _wait_copies(started_a, 4, plan_a, results[0][1], "wait_a")
    results = _adamw_group(_adam_items(names3, own3, [other3], w, m, v), "w_in")
    _adam_store(names3, results, w, out)
    (got4,) = _wait_copies(started_b, 3, _chips_plan(1), results[0][1], "wait_b")
    own4 = add_chips(sums4, [got4], tags4)
    own5 = add_chips(sums5, [got5], tags5)
    other4, other5 = _alone(_share_rider(own4 + own5), "share_last")
    results = _adamw_group(_adam_items(names4 + names5, own4 + own5, [other4, other5], w, m, v), "ffn1")
    _adam_store(names4 + names5, results, w, out)

    shapes = {name: w[name].shape for name in VECTOR_PARAMS + ("pool_w",)}
    loss, results = _adamw_small(_pack_small(w), small_all, _pack_small(m), _pack_small(v), shapes)
    for name, res in results.items():
        out["grad"][name], out["delta"][name], out["new_m"][name], out["new_v"][name] = (a.reshape(shapes[name]) for a in res)

    return (loss[0, 0], dx[None], *[out["grad"][n] for n in ALL_PARAMS], *[out["delta"][n] for n in ALL_PARAMS],
            *[out["new_m"][n] for n in ALL_PARAMS], *[out["new_v"][n] for n in ALL_PARAMS])
```

```python
import functools

import jax
import jax.numpy as jnp
from jax import lax
from jax.experimental import pallas as pl
from jax.experimental.pallas import tpu as pltpu

F32 = jnp.float32
BF16 = jnp.bfloat16
MESH = pl.DeviceIdType.MESH

D_MODEL = 1024
D_FF = 2816
HEADS = 8
HEAD_DIM = 128
POOL_WIDTH = 512
POOL_WINDOWS = (2, 4, 8, 16)
POOL_HALO = 16
N_CHIPS = 4
EPS = 1e-6
CHUNK = 64
MAIN_COLS = 4096
GATE_COLS = 2048
SHARD_IN_COLS = 1664

ADAM_LR = 0.001
ADAM_B1 = 0.9
ADAM_B2 = 0.999
ADAM_EPS = 1e-08
ADAM_WD = 0.01
ADAM_STEP = 10

VMEM_LIMIT = 56 * 1024 * 1024
WGRAD_IN_BLOCKS = 4
WGRAD_OUT_BLOCKS = 2
ADAM_BLOCKS = 4


def _params(semantics=None, vmem=VMEM_LIMIT):
    return pltpu.CompilerParams(dimension_semantics=semantics, vmem_limit_bytes=vmem)


def _dot(a, b):
    return jnp.dot(a, b, preferred_element_type=F32)


def _dot_nt(a, b):
    return lax.dot_general(a, b, (((1,), (1,)), ((), ())), preferred_element_type=F32)


def _dot_tn(a, b):
    return lax.dot_general(a, b, (((0,), (0,)), ((), ())), preferred_element_type=F32)


def _tri_sum(tri, x):
    hi = x.astype(BF16)
    lo = (x - hi.astype(F32)).astype(BF16)
    return _dot(tri, hi) + _dot(tri, lo)


def _sigmoid(x):
    return jax.nn.sigmoid(x)


def _resident(shape):
    zeros = (0,) * len(shape)
    return pl.BlockSpec(shape, lambda *_: zeros, pipeline_mode=pl.Buffered(1))


def _pick(shape, k):
    zeros = (0,) * (len(shape) - 1)
    return pl.BlockSpec((None,) + tuple(shape[1:]), lambda *_: (k,) + zeros, pipeline_mode=pl.Buffered(1))


def _rows(tm, cols, col_block=0):
    return pl.BlockSpec((tm, cols), lambda i: (i, col_block))


def _acc(shape):
    zeros = (0,) * len(shape)
    return pl.BlockSpec(shape, lambda *_: zeros)


def _rms(x):
    r = lax.rsqrt(jnp.mean(x * x, axis=-1, keepdims=True) + EPS)
    return r, x * r


def _rms_bwd(dn, n, r):
    return r * (dn - n * jnp.mean(dn * n, axis=-1, keepdims=True))


def _colsum(a):
    return jnp.sum(a, axis=0, keepdims=True)


ANY = pl.BlockSpec(memory_space=pl.ANY)


class _Rider:
    def __init__(self, inputs, out_shape, sems, phases):
        self.inputs, self.out_shape, self.sems, self.phases = list(inputs), list(out_shape), list(sems), list(phases)


def _hosted(riders, body, *, name, grid=(), in_specs, out_specs, out_shape, scratch_shapes=(), compiler_params=None):
    riders = [r for r in riders if r is not None]
    n_in, n_out, n_scr = len(in_specs), len(out_shape), len(scratch_shapes)
    n_steps = 1
    for g in grid:
        n_steps *= g

    def wrapped(*refs):
        pos = n_in
        ins = refs[:n_in]
        r_ins = []
        for r in riders:
            r_ins.append(refs[pos:pos + len(r.inputs)])
            pos += len(r.inputs)
        outs = refs[pos:pos + n_out]
        pos += n_out
        r_outs = []
        for r in riders:
            r_outs.append(refs[pos:pos + len(r.out_shape)])
            pos += len(r.out_shape)
        scr = refs[pos:pos + n_scr]
        pos += n_scr
        r_sems = []
        for r in riders:
            r_sems.append(refs[pos:pos + len(r.sems)])
            pos += len(r.sems)
        step = 0
        for axis in range(len(grid)):
            step = step * grid[axis] + pl.program_id(axis)

        def at_step(which, fn):
            if n_steps == 1:
                fn()
            else:
                pl.when(step == which)(fn)

        for r, ri, ro, rs in zip(riders, r_ins, r_outs, r_sems):
            for fraction, fn in r.phases:
                if fraction == 0:
                    at_step(0, functools.partial(fn, ri, ro, rs))
        body(*ins, *outs, *scr)
        for r, ri, ro, rs in zip(riders, r_ins, r_outs, r_sems):
            for fraction, fn in r.phases:
                if fraction > 0:
                    at_step(min(int(fraction * n_steps), n_steps - 1), functools.partial(fn, ri, ro, rs))

    call = pl.pallas_call(
        wrapped, name=name, grid=grid,
        in_specs=list(in_specs) + [ANY for r in riders for _ in r.inputs],
        out_specs=list(out_specs) + [ANY for r in riders for _ in r.out_shape],
        out_shape=list(out_shape) + [s for r in riders for s in r.out_shape],
        scratch_shapes=list(scratch_shapes) + [s for r in riders for s in r.sems],
        compiler_params=compiler_params)

    def run(*args):
        res = call(*args, *[a for r in riders for a in r.inputs])
        extras, pos = [], n_out
        for r in riders:
            extras.append(list(res[pos:pos + len(r.out_shape)]))
            pos += len(r.out_shape)
        return list(res[:n_out]), extras

    return run


def _ffn_fwd(x, g, ffnw, tag, tm, riders=()):
    t = x.shape[0]

    def body(x_ref, g_ref, w1_ref, w3_ref, w2_ref, xo_ref, a_ref, b_ref):
        xv = x_ref[...]
        _, n = _rms(xv)
        h = (n * g_ref[...]).astype(BF16)
        a = _dot_nt(h, w1_ref[...])
        b = _dot_nt(h, w3_ref[...])
        s = (a * _sigmoid(a) * b).astype(BF16)
        xo_ref[...] = xv + 0.5 * _dot(s, w2_ref[...])
        a_ref[...] = a.astype(BF16)
        b_ref[...] = b.astype(BF16)

    return _hosted(
        riders, body, name=f"ffn_fwd_{tag}", grid=(t // tm,),
        in_specs=[_rows(tm, D_MODEL), _resident((1, D_MODEL)), _pick(ffnw.shape, 0), _pick(ffnw.shape, 1),
                  _pick(ffnw.shape, 2)],
        out_specs=[_rows(tm, D_MODEL), _rows(tm, D_FF), _rows(tm, D_FF)],
        out_shape=[jax.ShapeDtypeStruct((t, D_MODEL), F32), jax.ShapeDtypeStruct((t, D_FF), BF16),
                   jax.ShapeDtypeStruct((t, D_FF), BF16)],
        compiler_params=_params(("arbitrary",)),
    )(x, g, ffnw, ffnw, ffnw)


def _ffn_bwd(dxo, x, g, a, b, ffnw, tag, tm, riders=()):
    t = x.shape[0]

    def body(dxo_ref, x_ref, g_ref, a_ref, b_ref, w1_ref, w3_ref, w2_ref, dx_ref, dab_ref, s_ref, h_ref, dxh_ref, dg_ref):
        @pl.when(pl.program_id(0) == 0)
        def _():
            dg_ref[...] = jnp.zeros_like(dg_ref)

        xv = x_ref[...]
        gv = g_ref[...]
        r, n = _rms(xv)
        h_ref[...] = (n * gv).astype(BF16)
        dxo_v = dxo_ref[...]
        dxh = (0.5 * dxo_v).astype(BF16)
        dxh_ref[...] = dxh
        ds = _dot_nt(dxh, w2_ref[...])
        av = a_ref[...].astype(F32)
        bv = b_ref[...].astype(F32)
        sg = _sigmoid(av)
        silu = av * sg
        s_ref[...] = (silu * bv).astype(BF16)
        da = (ds * bv * (sg * (1.0 + av * (1.0 - sg)))).astype(BF16)
        db = (ds * silu).astype(BF16)
        dab_ref[:, :D_FF] = da
        dab_ref[:, D_FF:] = db
        dh = _dot(da, w1_ref[...]) + _dot(db, w3_ref[...])
        dg_ref[...] += _colsum(dh * n)
        dx_ref[...] = dxo_v + _rms_bwd(dh * gv, n, r)

    return _hosted(
        riders, body, name=f"ffn_bwd_{tag}", grid=(t // tm,),
        in_specs=[_rows(tm, D_MODEL), _rows(tm, D_MODEL), _resident((1, D_MODEL)), _rows(tm, D_FF), _rows(tm, D_FF),
                  _pick(ffnw.shape, 0), _pick(ffnw.shape, 1), _pick(ffnw.shape, 2)],
        out_specs=[_rows(tm, D_MODEL), _rows(tm, 2 * D_FF), _rows(tm, D_FF), _rows(tm, D_MODEL), _rows(tm, D_MODEL),
                   _acc((1, D_MODEL))],
        out_shape=[jax.ShapeDtypeStruct((t, D_MODEL), F32), jax.ShapeDtypeStruct((t, 2 * D_FF), BF16),
                   jax.ShapeDtypeStruct((t, D_FF), BF16), jax.ShapeDtypeStruct((t, D_MODEL), BF16),
                   jax.ShapeDtypeStruct((t, D_MODEL), BF16), jax.ShapeDtypeStruct((1, D_MODEL), F32)],
        compiler_params=_params(("arbitrary",)),
    )(dxo, x, g, a, b, ffnw, ffnw, ffnw)


def _wgrad_body(n_token_tiles):
    def body(x_ref, dy_ref, o_ref, acc):
        k = pl.program_id(1)

        @pl.when(k == 0)
        def _():
            acc[...] = jnp.zeros_like(acc)

        acc[...] += _dot_tn(x_ref[...], dy_ref[...])

        @pl.when(k == n_token_tiles - 1)
        def _():
            o_ref[...] = acc[...].astype(BF16)

    return body


def _wgrad(xm, dy, out_blocks, name, tk, riders=()):
    t, m = xm.shape
    n = dy.shape[1]
    mb = m // out_blocks

    return _hosted(
        riders, _wgrad_body(t // tk), name=name, grid=(out_blocks, t // tk),
        in_specs=[pl.BlockSpec((tk, mb), lambda j, k: (k, j)), pl.BlockSpec((tk, n), lambda j, k: (k, 0))],
        out_specs=[pl.BlockSpec((None, mb, n), lambda j, k: (j, 0, 0))],
        out_shape=[jax.ShapeDtypeStruct((out_blocks, mb, n), BF16)],
        scratch_shapes=[pltpu.VMEM((mb, n), F32)],
        compiler_params=_params(("arbitrary", "arbitrary")),
    )(xm, dy)


def _wgrad_cols(xm, dy, out_blocks, name, tk, riders=()):
    t, m = xm.shape
    n = dy.shape[1]
    nb = n // out_blocks

    return _hosted(
        riders, _wgrad_body(t // tk), name=name, grid=(out_blocks, t // tk),
        in_specs=[pl.BlockSpec((tk, m), lambda j, k: (k, 0)), pl.BlockSpec((tk, nb), lambda j, k: (k, j))],
        out_specs=[pl.BlockSpec((None, m, nb), lambda j, k: (j, 0, 0))],
        out_shape=[jax.ShapeDtypeStruct((out_blocks, m, nb), BF16)],
        scratch_shapes=[pltpu.VMEM((m, nb), F32)],
        compiler_params=_params(("arbitrary", "arbitrary")),
    )(xm, dy)


def _mix_fwd(x1, g, winw, tm, riders=()):
    t = x1.shape[0]

    def body(x_ref, g_ref, w_ref, main_ref, pool_ref, gate_ref):
        _, n = _rms(x_ref[...])
        h = (n * g_ref[...]).astype(BF16)
        proj = jnp.concatenate([_dot(h, w_ref[j]) for j in range(N_CHIPS)], axis=1)
        main_ref[...] = proj[:, :MAIN_COLS]
        pool_ref[...] = proj[:, MAIN_COLS:MAIN_COLS + POOL_WIDTH]
        gate_ref[...] = proj[:, MAIN_COLS + POOL_WIDTH:]

    return _hosted(
        riders, body, name="mix_fwd", grid=(t // tm,),
        in_specs=[_rows(tm, D_MODEL), _resident((1, D_MODEL)), _resident(winw.shape)],
        out_specs=[_rows(tm, MAIN_COLS), _rows(tm, POOL_WIDTH), _rows(tm, GATE_COLS)],
        out_shape=[jax.ShapeDtypeStruct((t, MAIN_COLS), F32), jax.ShapeDtypeStruct((t, POOL_WIDTH), F32),
                   jax.ShapeDtypeStruct((t, GATE_COLS), F32)],
        compiler_params=_params(("arbitrary",)),
    )(x1, g, winw)


def _mix_bwd(dqfi, dog, du, dgates, dx2, x1, g, winw, tm, riders=()):
    t = x1.shape[0]
    cols = N_CHIPS * SHARD_IN_COLS

    def body(dqfi_ref, dog_ref, du_ref, dgt_ref, dx2_ref, x_ref, g_ref, w_ref, dx_ref, dproj_ref, h_ref, dg_ref):
        @pl.when(pl.program_id(0) == 0)
        def _():
            dg_ref[...] = jnp.zeros_like(dg_ref)

        dproj = jnp.concatenate([dqfi_ref[...], dog_ref[...], du_ref[...], dgt_ref[...]], axis=1)
        dproj_ref[...] = dproj
        dh = _dot_nt(dproj[:, :SHARD_IN_COLS], w_ref[0])
        for j in range(1, N_CHIPS):
            dh += _dot_nt(dproj[:, j * SHARD_IN_COLS:(j + 1) * SHARD_IN_COLS], w_ref[j])
        gv = g_ref[...]
        r, n = _rms(x_ref[...])
        h_ref[...] = (n * gv).astype(BF16)
        dg_ref[...] += _colsum(dh * n)
        dx_ref[...] = dx2_ref[...] + _rms_bwd(dh * gv, n, r)

    return _hosted(
        riders, body, name="mix_bwd", grid=(t // tm,),
        in_specs=[_rows(tm, 3 * D_MODEL), _rows(tm, D_MODEL), _rows(tm, POOL_WIDTH), _rows(tm, GATE_COLS),
                  _rows(tm, D_MODEL), _rows(tm, D_MODEL), _resident((1, D_MODEL)), _resident(winw.shape)],
        out_specs=[_rows(tm, D_MODEL), _rows(tm, cols), _rows(tm, D_MODEL), _acc((1, D_MODEL))],
        out_shape=[jax.ShapeDtypeStruct((t, D_MODEL), F32), jax.ShapeDtypeStruct((t, cols), BF16),
                   jax.ShapeDtypeStruct((t, D_MODEL), BF16), jax.ShapeDtypeStruct((1, D_MODEL), F32)],
        compiler_params=_params(("arbitrary",)),
    )(dqfi, dog, du, dgates, dx2, x1, g, winw)


def _lower_bound(lb_raw):
    l0 = lb_raw[0:1, :]
    l1 = lb_raw[1:2, :]
    m = jnp.maximum(l0, l1)
    e0 = jnp.exp(l0 - m)
    e1 = jnp.exp(l1 - m)
    return e0 / (e0 + e1)


def _head_slices():
    return [slice(h * HEAD_DIM, (h + 1) * HEAD_DIM) for h in range(HEADS)]


def _gates(qr, fr, lb, tril_b, first_half):
    sg = _sigmoid(fr)
    f = lb + (1.0 - lb) * sg
    k = 1.0 - f
    sq = _sigmoid(qr)
    q = qr * sq
    log_f = jnp.log(f)
    gc = _tri_sum(tril_b, log_f)
    gm = _colsum(jnp.where(first_half, log_f, 0.0))
    gl = _colsum(log_f)
    e_q = jnp.exp(gc - gm)
    e_k = jnp.exp(gm - gc)
    e_in = jnp.exp(gc)
    e_out = jnp.exp(gl - gc)
    return dict(sg=sg, f=f, k=k, sq=sq, q=q, e_q=e_q, e_k=e_k, e_in=e_in, e_out=e_out, e_last=jnp.exp(gl))


def _hgrn_fwd(main, lb_raw, tt, riders=()):
    t = main.shape[0]
    n_local = tt // CHUNK

    def body(q_ref, f_ref, i_ref, lb_ref, o_ref, st_ref, s_scr):
        @pl.when(pl.program_id(0) == 0)
        def _():
            s_scr[...] = jnp.zeros_like(s_scr)

        lb = _lower_bound(lb_ref[...])
        row = lax.broadcasted_iota(jnp.int32, (CHUNK, CHUNK), 0)
        col = lax.broadcasted_iota(jnp.int32, (CHUNK, CHUNK), 1)
        tril = row >= col
        tril_b = tril.astype(BF16)
        first_half = lax.broadcasted_iota(jnp.int32, (CHUNK, D_MODEL), 0) < CHUNK // 2
        heads = _head_slices()

        def chunk(c, carry):
            rows = pl.ds(pl.multiple_of(c * CHUNK, CHUNK), CHUNK)
            z = _gates(q_ref[rows, :], f_ref[rows, :], lb, tril_b, first_half)
            qt = (z["q"] * z["e_q"]).astype(BF16)
            kt = (z["k"] * z["e_k"]).astype(BF16)
            qg = (z["q"] * z["e_in"]).astype(BF16)
            kg = (z["k"] * z["e_out"]).astype(BF16)
            vb = i_ref[rows, :].astype(BF16)
            states = [s_scr[h] for h in range(HEADS)]
            for h in range(HEADS):
                st_ref[c, h] = states[h]
            raw = [_dot_nt(qt[:, sl], kt[:, sl]) for sl in heads]
            inter = [_dot_nt(qg[:, sl], states[h].astype(BF16)) for h, sl in enumerate(heads)]
            grown = [_dot_tn(vb[:, sl], kg[:, sl]) for sl in heads]
            scores = [jnp.where(tril, r, 0.0).astype(BF16) for r in raw]
            for h, sl in enumerate(heads):
                s_scr[h] = states[h] * z["e_last"][:, sl] + grown[h]
            o_ref[rows, :] = jnp.concatenate([_dot(scores[h], vb[:, sl]) + inter[h] for h, sl in enumerate(heads)], axis=1)
            return carry

        lax.fori_loop(0, n_local, chunk, 0, unroll=True)

    return _hosted(
        riders, body, name="hgrn_fwd", grid=(t // tt,),
        in_specs=[_rows(tt, D_MODEL, 0), _rows(tt, D_MODEL, 1), _rows(tt, D_MODEL, 2), _resident((2, D_MODEL))],
        out_specs=[_rows(tt, D_MODEL),
                   pl.BlockSpec((n_local, HEADS, HEAD_DIM, HEAD_DIM), lambda i: (i, 0, 0, 0))],
        out_shape=[jax.ShapeDtypeStruct((t, D_MODEL), F32),
                   jax.ShapeDtypeStruct((t // CHUNK, HEADS, HEAD_DIM, HEAD_DIM), F32)],
        scratch_shapes=[pltpu.VMEM((HEADS, HEAD_DIM, HEAD_DIM), F32)],
        compiler_params=_params(("arbitrary",)),
    )(main, main, main, lb_raw)


def _hgrn_bwd(main, lb_raw, states, do, tt, riders=()):
    t = main.shape[0]
    n_tiles = t // tt
    n_local = tt // CHUNK

    def rev(col_block):
        return pl.BlockSpec((tt, D_MODEL), lambda i: (n_tiles - 1 - i, col_block))

    def body(q_ref, f_ref, i_ref, lb_ref, st_ref, do_ref, dqfi_ref, dlb_ref, ds_scr, acc_scr):
        @pl.when(pl.program_id(0) == 0)
        def _():
            ds_scr[...] = jnp.zeros_like(ds_scr)
            acc_scr[...] = jnp.zeros_like(acc_scr)

        lb = _lower_bound(lb_ref[...])
        row = lax.broadcasted_iota(jnp.int32, (CHUNK, CHUNK), 0)
        col = lax.broadcasted_iota(jnp.int32, (CHUNK, CHUNK), 1)
        tril = row >= col
        tril_b = tril.astype(BF16)
        triu_b = (row <= col).astype(BF16)
        first_half = lax.broadcasted_iota(jnp.int32, (CHUNK, D_MODEL), 0) < CHUNK // 2
        heads = _head_slices()
        cat = functools.partial(jnp.concatenate, axis=1)

        def chunk(cc, carry):
            c = n_local - 1 - cc
            rows = pl.ds(pl.multiple_of(c * CHUNK, CHUNK), CHUNK)
            qr = q_ref[rows, :]
            z = _gates(qr, f_ref[rows, :], lb, tril_b, first_half)
            qt = (z["q"] * z["e_q"]).astype(BF16)
            kt = (z["k"] * z["e_k"]).astype(BF16)
            qg_f = z["q"] * z["e_in"]
            qg = qg_f.astype(BF16)
            kg_f = z["k"] * z["e_out"]
            kg = kg_f.astype(BF16)
            vb = i_ref[rows, :].astype(BF16)
            dob = do_ref[rows, :].astype(BF16)
            st = [st_ref[c, h] for h in range(HEADS)]
            dst = [ds_scr[h] for h in range(HEADS)]
            dst_b = [d.astype(BF16) for d in dst]
            raw = [_dot_nt(qt[:, sl], kt[:, sl]) for sl in heads]
            draw = [_dot_nt(dob[:, sl], vb[:, sl]) for sl in heads]
            dqg = [_dot(dob[:, sl], st[h].astype(BF16)) for h, sl in enumerate(heads)]
            dkg = [_dot(vb[:, sl], dst_b[h]) for h, sl in enumerate(heads)]
            dv_inter = [_dot_nt(kg[:, sl], dst_b[h]) for h, sl in enumerate(heads)]
            grown = [_dot_tn(dob[:, sl], qg[:, sl]) for sl in heads]
            scores = [jnp.where(tril, r, 0.0).astype(BF16) for r in raw]
            dscores = [jnp.where(tril, r, 0.0).astype(BF16) for r in draw]
            dqt = [_dot(dscores[h], kt[:, sl]) for h, sl in enumerate(heads)]
            dkt = [_dot_tn(dscores[h], qt[:, sl]) for h, sl in enumerate(heads)]
            dv = [_dot_tn(scores[h], dob[:, sl]) + dv_inter[h] for h, sl in enumerate(heads)]
            carry_in = cat([z["e_last"][:, sl] * _colsum(dst[h] * st[h]) for h, sl in enumerate(heads)])
            for h, sl in enumerate(heads):
                ds_scr[h] = dst[h] * z["e_last"][:, sl] + grown[h]
            dqt, dkt, dqg, dkg = cat(dqt), cat(dkt), cat(dqg), cat(dkg)
            carry_in += _colsum(dkg * kg_f)
            dq = dqt * z["e_q"] + dqg * z["e_in"]
            dk = dkt * z["e_k"] + dkg * z["e_out"]
            dgate = (qt.astype(F32) * dqt - kt.astype(F32) * dkt) + (qg_f * dqg - kg_f * dkg)
            dlogf = _tri_sum(triu_b, dgate) + carry_in
            df = dlogf / z["f"] - dk
            sg = z["sg"]
            sq = z["sq"]
            acc_scr[...] += _colsum(df * (1.0 - sg))
            dqfi_ref[rows, 0:D_MODEL] = (dq * (sq * (1.0 + qr * (1.0 - sq)))).astype(BF16)
            dqfi_ref[rows, D_MODEL:2 * D_MODEL] = (df * (1.0 - lb) * sg * (1.0 - sg)).astype(BF16)
            dqfi_ref[rows, 2 * D_MODEL:3 * D_MODEL] = cat(dv).astype(BF16)
            return carry

        lax.fori_loop(0, n_local, chunk, 0, unroll=True)
        d0 = acc_scr[...] * lb * (1.0 - lb)
        dlb_ref[0:1, :] = d0
        dlb_ref[1:2, :] = -d0

    return _hosted(
        riders, body, name="hgrn_bwd", grid=(n_tiles,),
        in_specs=[rev(0), rev(1), rev(2), _resident((2, D_MODEL)),
                  pl.BlockSpec((n_local, HEADS, HEAD_DIM, HEAD_DIM), lambda i: (n_tiles - 1 - i, 0, 0, 0)),
                  rev(0)],
        out_specs=[pl.BlockSpec((tt, 3 * D_MODEL), lambda i: (n_tiles - 1 - i, 0)), _acc((2, D_MODEL))],
        out_shape=[jax.ShapeDtypeStruct((t, 3 * D_MODEL), BF16), jax.ShapeDtypeStruct((2, D_MODEL), F32)],
        scratch_shapes=[pltpu.VMEM((HEADS, HEAD_DIM, HEAD_DIM), F32), pltpu.VMEM((1, D_MODEL), F32)],
        compiler_params=_params(("arbitrary",)),
    )(main, main, main, lb_raw, states, do)


def _head_norm(o):
    rs, ns = [], []
    for h in range(HEADS):
        oh = o[:, h * HEAD_DIM:(h + 1) * HEAD_DIM]
        r, n = _rms(oh)
        rs.append(jnp.broadcast_to(r, oh.shape))
        ns.append(n)
    return jnp.concatenate(rs, axis=1), jnp.concatenate(ns, axis=1)


def _head_norm_bwd(dn, n, r):
    outs = []
    for h in range(HEADS):
        sl = slice(h * HEAD_DIM, (h + 1) * HEAD_DIM)
        outs.append(_rms_bwd(dn[:, sl], n[:, sl], r[:, sl]))
    return jnp.concatenate(outs, axis=1)


def _window_counts(first_row, tm):
    pos = (first_row + 1 + lax.broadcasted_iota(jnp.int32, (tm, 1), 0)).astype(F32)
    return [jnp.minimum(pos, float(w)) for w in POOL_WINDOWS]


def _post_fwd(o, main, pool_r, gates, x1, onorm, pool_w, pool_scale, sqw, wbw, tm, riders=()):
    t = o.shape[0]
    ext_rows = tm + POOL_HALO

    def body(o_ref, og_ref, u_ref, gt_ref, x1_ref, on_ref, pw_ref, ps_ref, wa_ref, wout_ref, wb_ref,
             x2_ref, ya_ref, yb_ref, pooled_ref, ext):
        i = pl.program_id(0)

        @pl.when(i == 0)
        def _():
            ext[0:POOL_HALO, :] = jnp.zeros((POOL_HALO, POOL_WIDTH), F32)

        _, n = _head_norm(o_ref[...])
        og = og_ref[...]
        oa = (n * on_ref[...] * (og * _sigmoid(og))).astype(BF16)
        ya = _dot(oa, wa_ref[...])

        u = u_ref[...]
        ext[POOL_HALO:ext_rows, :] = u
        e = ext[...]
        counts = _window_counts(i * tm, tm)
        pooled = []
        for gidx, w in enumerate(POOL_WINDOWS):
            s = e[:, gidx * HEAD_DIM:(gidx + 1) * HEAD_DIM]
            shift = 1
            while shift < w:
                s = s + pltpu.roll(s, shift, axis=0)
                shift *= 2
            pooled.append(s[POOL_HALO:, :] / counts[gidx] - u[:, gidx * HEAD_DIM:(gidx + 1) * HEAD_DIM])
        ext[0:POOL_HALO, :] = ext[tm:ext_rows, :]
        pooled_b = [pg.astype(BF16) for pg in pooled]
        pooled_ref[...] = jnp.concatenate(pooled_b, axis=1)
        mixed = jnp.concatenate([_dot(pooled_b[gidx], pw_ref[gidx].astype(BF16)) for gidx in range(len(POOL_WINDOWS))],
                                axis=1) * ps_ref[...]
        mixed_b = mixed.astype(BF16)
        yb = jnp.concatenate([_dot(mixed_b, wb_ref[j]) for j in range(N_CHIPS)], axis=1)

        gt = gt_ref[...]
        y = _sigmoid(gt[:, :D_MODEL]) * ya + _sigmoid(gt[:, D_MODEL:]) * yb
        x2_ref[...] = x1_ref[...] + _dot(y.astype(BF16), wout_ref[...])
        ya_ref[...] = ya.astype(BF16)
        yb_ref[...] = yb.astype(BF16)

    return _hosted(
        riders, body, name="post_fwd", grid=(t // tm,),
        in_specs=[_rows(tm, D_MODEL), _rows(tm, D_MODEL, 3), _rows(tm, POOL_WIDTH), _rows(tm, GATE_COLS), _rows(tm, D_MODEL),
                  _resident((1, D_MODEL)), _resident(pool_w.shape), _resident((1, POOL_WIDTH)),
                  _pick(sqw.shape, 0), _pick(sqw.shape, 1), _resident(wbw.shape)],
        out_specs=[_rows(tm, D_MODEL), _rows(tm, D_MODEL), _rows(tm, D_MODEL), _rows(tm, POOL_WIDTH)],
        out_shape=[jax.ShapeDtypeStruct((t, D_MODEL), F32), jax.ShapeDtypeStruct((t, D_MODEL), BF16),
                   jax.ShapeDtypeStruct((t, D_MODEL), BF16), jax.ShapeDtypeStruct((t, POOL_WIDTH), BF16)],
        scratch_shapes=[pltpu.VMEM((ext_rows, POOL_WIDTH), F32)],
        compiler_params=_params(("arbitrary",)),
    )(o, main, pool_r, gates, x1, onorm, pool_w, pool_scale, sqw, sqw, wbw)


def _post_bwd(dx2, o, main, gates, ya, yb, pooled, onorm, pool_w, pool_scale, sqw, wbw, tm, riders=()):
    t = o.shape[0]
    n_tiles = t // tm
    ext_rows = tm + POOL_HALO
    n_groups = len(POOL_WINDOWS)

    def rev(cols, col_block=0):
        return pl.BlockSpec((tm, cols), lambda i: (n_tiles - 1 - i, col_block))

    def body(dx2_ref, o_ref, og_ref, gt_ref, ya_ref, yb_ref, pooled_ref, on_ref, pw_ref, ps_ref, wa_ref, wout_ref, wb_ref,
             do_ref, dog_ref, du_ref, dgt_ref, dwa_ref, dwout_ref, dwb_ref, dpw_ref, dps_ref, don_ref, ext):
        i = pl.program_id(0)

        @pl.when(i == 0)
        def _():
            ext[tm:ext_rows, :] = jnp.zeros((POOL_HALO, POOL_WIDTH), F32)
            for ref in (dwa_ref, dwout_ref, dwb_ref, dpw_ref, dps_ref, don_ref):
                ref[...] = jnp.zeros_like(ref)

        groups = [slice(gidx * HEAD_DIM, (gidx + 1) * HEAD_DIM) for gidx in range(n_groups)]
        shards = [slice(j * 256, (j + 1) * 256) for j in range(N_CHIPS)]
        dx2b = dx2_ref[...].astype(BF16)
        dy = _dot_nt(dx2b, wout_ref[...])
        pooled_b = pooled_ref[...]
        pm = jnp.concatenate([_dot(pooled_b[:, g], pw_ref[gidx].astype(BF16)) for gidx, g in enumerate(groups)], axis=1)
        gt = gt_ref[...]
        sga = _sigmoid(gt[:, :D_MODEL])
        sgb = _sigmoid(gt[:, D_MODEL:])
        ya = ya_ref[...].astype(F32)
        yb = yb_ref[...].astype(F32)
        y = (sga * ya + sgb * yb).astype(BF16)
        dya = (dy * sga).astype(BF16)
        dyb = (dy * sgb).astype(BF16)
        dgt_ref[:, :D_MODEL] = (dy * ya * sga * (1.0 - sga)).astype(BF16)
        dgt_ref[:, D_MODEL:] = (dy * yb * sgb * (1.0 - sgb)).astype(BF16)
        dwout_ref[...] += _dot_tn(y, dx2b)
        doa = _dot_nt(dya, wa_ref[...])
        dmixed = _dot_nt(dyb[:, shards[0]], wb_ref[0])
        for j in range(1, N_CHIPS):
            dmixed += _dot_nt(dyb[:, shards[j]], wb_ref[j])
        r, n = _head_norm(o_ref[...])
        onv = on_ref[...]
        og = og_ref[...]
        sog = _sigmoid(og)
        silu_og = og * sog
        normed = n * onv
        oa = (normed * silu_og).astype(BF16)
        dog_ref[...] = (doa * normed * (sog * (1.0 + og * (1.0 - sog)))).astype(BF16)
        dnormed = doa * silu_og
        don_ref[...] += _colsum(dnormed * n)
        do_ref[...] = _head_norm_bwd(dnormed * onv, n, r)
        psv = ps_ref[...]
        mixed_b = (pm * psv).astype(BF16)
        dps_ref[...] += _colsum(dmixed * pm)
        dpm = (dmixed * psv).astype(BF16)
        dwa_ref[...] += _dot_tn(oa, dya)
        for j in range(N_CHIPS):
            dwb_ref[j] += _dot_tn(mixed_b, dyb[:, shards[j]])
        counts = _window_counts((n_tiles - 1 - i) * tm, tm)
        dpooled = []
        for gidx, g in enumerate(groups):
            dpw_ref[gidx] += _dot_tn(pooled_b[:, g], dpm[:, g])
            dpooled.append(_dot_nt(dpm[:, g], pw_ref[gidx].astype(BF16)))
        ext[0:tm, :] = jnp.concatenate([dpooled[gidx] / counts[gidx] for gidx in range(n_groups)], axis=1)
        e = ext[...]
        du = []
        for gidx, w in enumerate(POOL_WINDOWS):
            s = e[:, gidx * HEAD_DIM:(gidx + 1) * HEAD_DIM]
            shift = 1
            while shift < w:
                s = s + pltpu.roll(s, ext_rows - shift, axis=0)
                shift *= 2
            du.append(s[:tm, :] - dpooled[gidx])
        ext[tm:ext_rows, :] = ext[0:POOL_HALO, :]
        du_ref[...] = jnp.concatenate(du, axis=1).astype(BF16)

    wa_shape = (D_MODEL, D_MODEL)
    return _hosted(
        riders, body, name="post_bwd", grid=(n_tiles,),
        in_specs=[rev(D_MODEL), rev(D_MODEL), rev(D_MODEL, 3), rev(GATE_COLS), rev(D_MODEL), rev(D_MODEL), rev(POOL_WIDTH),
                  _resident((1, D_MODEL)), _resident(pool_w.shape), _resident((1, POOL_WIDTH)),
                  _pick(sqw.shape, 0), _pick(sqw.shape, 1), _resident(wbw.shape)],
        out_specs=[rev(D_MODEL), rev(D_MODEL), rev(POOL_WIDTH), rev(GATE_COLS),
                   _acc(wa_shape), _acc(wa_shape), _acc(wbw.shape), _acc(pool_w.shape), _acc((1, POOL_WIDTH)),
                   _acc((1, D_MODEL))],
        out_shape=[jax.ShapeDtypeStruct((t, D_MODEL), F32), jax.ShapeDtypeStruct((t, D_MODEL), BF16),
                   jax.ShapeDtypeStruct((t, POOL_WIDTH), BF16), jax.ShapeDtypeStruct((t, GATE_COLS), BF16),
                   jax.ShapeDtypeStruct(wa_shape, F32), jax.ShapeDtypeStruct(wa_shape, F32),
                   jax.ShapeDtypeStruct(wbw.shape, F32), jax.ShapeDtypeStruct(pool_w.shape, F32),
                   jax.ShapeDtypeStruct((1, POOL_WIDTH), F32), jax.ShapeDtypeStruct((1, D_MODEL), F32)],
        scratch_shapes=[pltpu.VMEM((ext_rows, POOL_WIDTH), F32)],
        compiler_params=_params(("arbitrary",)),
    )(dx2, o, main, gates, ya, yb, pooled, onorm, pool_w, pool_scale, sqw, sqw, wbw)


def _tail(x3, p, target, g_ple, g_post, g_final, sqw, wpw, tm, riders=()):
    t = x3.shape[0]
    pd = p.shape[1]

    def body(x_ref, p_ref, tg_ref, g4_ref, g5_ref, g6_ref, wg_ref, wp_ref,
             dx_ref, loss_ref, dwg_ref, dwp_ref, dg4_ref, dg5_ref, dg6_ref):
        @pl.when(pl.program_id(0) == 0)
        def _():
            for ref in (loss_ref, dwg_ref, dwp_ref, dg4_ref, dg5_ref, dg6_ref):
                ref[...] = jnp.zeros_like(ref)

        x3v = x_ref[...]
        g4, g5, g6 = g4_ref[...], g5_ref[...], g6_ref[...]
        r4, n4 = _rms(x3v)
        h4 = (n4 * g4).astype(BF16)
        gate = _sigmoid(_dot(h4, wg_ref[...]))
        pb = p_ref[...].astype(BF16)
        r5, n5 = _rms(jnp.concatenate([_dot(pb, wp_ref[j]) for j in range(N_CHIPS)], axis=1))
        emb = n5 * g5
        r6, n6 = _rms(x3v + gate * emb)
        diff = n6 * g6 - tg_ref[...]
        loss_ref[...] += 0.5 * jnp.sum(jnp.mean(diff * diff, axis=-1, keepdims=True), axis=0, keepdims=True)
        dout = diff * (1.0 / D_MODEL)
        dg6_ref[...] += _colsum(dout * n6)
        dx4 = _rms_bwd(dout * g6, n6, r6)
        demb = dx4 * gate
        dg5_ref[...] += _colsum(demb * n5)
        dpre = _rms_bwd(demb * g5, n5, r5).astype(BF16)
        for j in range(N_CHIPS):
            dwp_ref[j] += _dot_tn(pb, dpre[:, j * pd:(j + 1) * pd])
        dz = (dx4 * emb * gate * (1.0 - gate)).astype(BF16)
        dwg_ref[...] += _dot_tn(h4, dz)
        dh4 = _dot_nt(dz, wg_ref[...])
        dg4_ref[...] += _colsum(dh4 * n4)
        dx_ref[...] = dx4 + _rms_bwd(dh4 * g4, n4, r4)

    sq_shape = (D_MODEL, D_MODEL)
    vec = (1, D_MODEL)
    return _hosted(
        riders, body, name="tail", grid=(t // tm,),
        in_specs=[_rows(tm, D_MODEL), _rows(tm, pd), _rows(tm, D_MODEL), _resident(vec), _resident(vec), _resident(vec),
                  _pick(sqw.shape, 2), _resident(wpw.shape)],
        out_specs=[_rows(tm, D_MODEL), _acc((1, 1)), _acc(sq_shape), _acc(wpw.shape), _acc(vec), _acc(vec), _acc(vec)],
        out_shape=[jax.ShapeDtypeStruct((t, D_MODEL), F32), jax.ShapeDtypeStruct((1, 1), F32),
                   jax.ShapeDtypeStruct(sq_shape, F32), jax.ShapeDtypeStruct(wpw.shape, F32),
                   jax.ShapeDtypeStruct(vec, F32), jax.ShapeDtypeStruct(vec, F32), jax.ShapeDtypeStruct(vec, F32)],
        compiler_params=_params(("arbitrary",)),
    )(x3, p, target, g_ple, g_post, g_final, sqw, wpw)


def _position():
    return lax.axis_index("x"), lax.axis_index("y"), lax.axis_index("c")


def _other_chips(x, y):
    return [(1 - x, y), (x, 1 - y), (1 - x, 1 - y)]


def _remote(src, dst, send_sems, recv_sems, k, device):
    return pltpu.make_async_remote_copy(src_ref=src, dst_ref=dst, send_sem=send_sems.at[k], recv_sem=recv_sems.at[k],
                                        device_id=device, device_id_type=MESH)


def _relay_gather_rider(shards):
    n = len(shards)
    kinds = 8

    def copies(ins, outs, sems):
        send_sems, recv_sems = sems
        x, y, c = _position()
        mine, nx, ny, diag = 2 * x + y, 2 * (1 - x) + y, 2 * x + 1 - y, 2 * (1 - x) + 1 - y
        to_x, to_y, sibling = (1 - x, y, c), (x, 1 - y, c), (x, y, 1 - c)
        sent, relayed, passed, arriving = [], [], [], []
        for a in range(n):
            hq = ins[a].shape[2] // 2
            quarter = lambda chip, half, q: outs[a].at[:, chip, half, pl.ds(q * hq, hq)]
            rc = functools.partial(_remote, send_sems=send_sems, recv_sems=recv_sems)
            sent += [rc(ins[a].at[:, c], outs[a].at[:, mine, c], k=kinds * a, device=to_x),
                     rc(ins[a].at[:, c], outs[a].at[:, mine, c], k=kinds * a + 1, device=to_y)]
            got = [rc(outs[a].at[:, nx, c], outs[a].at[:, nx, c], k=kinds * a, device=to_x),
                   rc(outs[a].at[:, ny, c], outs[a].at[:, ny, c], k=kinds * a + 1, device=to_y),
                   rc(quarter(diag, c, 0), quarter(diag, c, 0), k=kinds * a + 2, device=to_y),
                   rc(quarter(diag, c, 1), quarter(diag, c, 1), k=kinds * a + 3, device=to_x)]
            relayed += [rc(quarter(nx, c, 0), quarter(nx, c, 0), k=kinds * a + 2, device=to_y),
                        rc(quarter(ny, c, 1), quarter(ny, c, 1), k=kinds * a + 3, device=to_x)]
            blocks = [lambda h: outs[a].at[:, nx, h], lambda h: outs[a].at[:, ny, h],
                      lambda h: quarter(diag, h, 0), lambda h: quarter(diag, h, 1)]
            passed += [(got[i], rc(b(c), b(c), k=kinds * a + 4 + i, device=sibling)) for i, b in enumerate(blocks)]
            arriving += [rc(b(1 - c), b(1 - c), k=kinds * a + 4 + i, device=sibling) for i, b in enumerate(blocks)]
        return sent, relayed, passed, arriving

    def begin(ins, outs, sems):
        for cp in copies(ins, outs, sems)[0]:
            cp.start()

    def forward(ins, outs, sems):
        _, relayed, passed, _ = copies(ins, outs, sems)
        for i, (got, onward) in enumerate(passed):
            got.wait_recv()
            if i % 4 < 2:
                relayed[2 * (i // 4) + i % 4].start()
            onward.start()

    def finish(ins, outs, sems):
        sent, relayed, passed, arriving = copies(ins, outs, sems)
        for cp in arriving:
            cp.wait_recv()
        for cp in sent + relayed + [onward for _, onward in passed]:
            cp.wait_send()

    return _Rider(shards, [jax.ShapeDtypeStruct((s.shape[0], N_CHIPS) + s.shape[1:], s.dtype) for s in shards],
                  [pltpu.SemaphoreType.DMA((kinds * n,)), pltpu.SemaphoreType.DMA((kinds * n,))],
                  [(0, begin), (0.5, forward), (1, finish)])


def _with_own(gathered, shard, slot):
    return lax.dynamic_update_slice(gathered, shard[:, None], (0, slot, 0, 0, 0))


def _exchange_rider(arrays, out_shape, n_copies, transfers, n_local=0):
    def copies(ins, outs, sems):
        send_sems, recv_sems, local_sems = sems
        remote, local = transfers(ins, outs)
        return ([_remote(src, dst, send_sems, recv_sems, i, dev) for i, (src, dst, dev) in enumerate(remote)],
                [pltpu.make_async_copy(src, dst, local_sems.at[i]) for i, (src, dst) in enumerate(local)])

    def begin(ins, outs, sems):
        remote, local = copies(ins, outs, sems)
        for cp in remote + local:
            cp.start()

    def finish(ins, outs, sems):
        remote, local = copies(ins, outs, sems)
        for cp in remote:
            cp.wait_recv()
        for cp in remote:
            cp.wait_send()
        for cp in local:
            cp.wait()

    return _Rider(arrays, out_shape,
                  [pltpu.SemaphoreType.DMA((n_copies,)), pltpu.SemaphoreType.DMA((n_copies,)),
                   pltpu.SemaphoreType.DMA((max(n_local, 1),))],
                  [(0, begin), (1, finish)])


def _pair_rider(partials):
    def transfers(ins, outs):
        x, y, c = _position()
        return [(ins[a].at[:, :, 1 - c], outs[a], (x, y, 1 - c)) for a in range(len(partials))], []

    shapes = [jax.ShapeDtypeStruct(g.shape[:2] + g.shape[3:], g.dtype) for g in partials]
    return _exchange_rider(partials, shapes, len(partials), transfers)


def _chips_rider(sums):
    n = len(sums)

    def transfers(ins, outs):
        x, y, c = _position()
        return [(ins[a].at[:, 2 * cx + cy], outs[a].at[:, k], (cx, cy, c))
                for k, (cx, cy) in enumerate(_other_chips(x, y)) for a in range(n)], []

    shapes = [jax.ShapeDtypeStruct((q.shape[0], 3) + q.shape[2:], q.dtype) for q in sums]
    return _exchange_rider(sums, shapes, 3 * n, transfers)


def _share_rider(halves):
    def transfers(ins, outs):
        x, y, c = _position()
        return [(ins[a], outs[a], (x, y, 1 - c)) for a in range(len(halves))], []

    return _exchange_rider(halves, [jax.ShapeDtypeStruct(h.shape, h.dtype) for h in halves], len(halves), transfers)


def _small_rider(pack):
    flips = [(fx, fy, fc) for fx in (0, 1) for fy in (0, 1) for fc in (0, 1)][1:]

    def transfers(ins, outs):
        x, y, c = _position()
        slot = outs[0].at[4 * x + 2 * y + c]
        flip = lambda v, f: v + f - 2 * v * f
        return [(ins[0], slot, (flip(x, fx), flip(y, fy), flip(c, fc))) for fx, fy, fc in flips], [(ins[0], slot)]

    return _exchange_rider([pack], [jax.ShapeDtypeStruct((8,) + pack.shape, pack.dtype)], len(flips), transfers, n_local=1)


IN_HBM = pl.BlockSpec(memory_space=pltpu.HBM)
IN_SEM = pl.BlockSpec(memory_space=pltpu.SEMAPHORE)
SPLIT_COPY = pltpu.CompilerParams(has_side_effects=pltpu.SideEffectType.DATAFLOW_SIDE_EFFECTING)


def _start_copies(sources, landing_shapes, n_copies, plan, name):
    n_src, n_land = len(sources), len(landing_shapes)

    def body(*refs):
        srcs, lands, outs = refs[:n_src], refs[n_src:n_src + n_land], refs[n_src + n_land:]
        send, recv, token = outs[:n_copies], outs[n_copies:2 * n_copies], outs[-1]
        for i, (src, dst, device) in enumerate(plan(srcs, lands)):
            pltpu.make_async_remote_copy(src_ref=src, dst_ref=dst, send_sem=send[i], recv_sem=recv[i], device_id=device,
                                         device_id_type=MESH).start()
        token[...] = jnp.zeros_like(token)

    arrays = [pltpu.with_memory_space_constraint(a, pltpu.HBM) for a in sources]
    arrays += [pltpu.with_memory_space_constraint(lax.empty(s.shape, s.dtype), pltpu.HBM) for s in landing_shapes]
    res = pl.pallas_call(
        body, name=name,
        out_shape=[pltpu.SemaphoreType.DMA(())] * (2 * n_copies) + [pltpu.HBM(a.shape, a.dtype) for a in arrays]
        + [jax.ShapeDtypeStruct((8, 128), F32)],
        in_specs=[IN_HBM] * len(arrays),
        out_specs=[IN_SEM] * (2 * n_copies) + [IN_HBM] * len(arrays) + [pl.BlockSpec(memory_space=pltpu.VMEM)],
        input_output_aliases={i: 2 * n_copies + i for i in range(len(arrays))},
        compiler_params=SPLIT_COPY,
    )(*arrays)
    sems, rest = res[:2 * n_copies], res[2 * n_copies:]
    return sems, rest[:n_src], rest[n_src:n_src + n_land], rest[-1]


def _wait_copies(started, n_copies, plan, after, name):
    sems, sources, landings, _ = started
    n_src, n_land = len(sources), len(landings)

    def body(*refs):
        srcs, lands = refs[:n_src], refs[n_src:n_src + n_land]
        sem_refs = refs[n_src + n_land:n_src + n_land + 2 * n_copies]
        send, recv = sem_refs[:n_copies], sem_refs[n_copies:]
        for i, (src, dst, device) in enumerate(plan(srcs, lands)):
            cp = pltpu.make_async_remote_copy(src_ref=src, dst_ref=dst, send_sem=send[i], recv_sem=recv[i], device_id=device,
                                              device_id_type=MESH)
            cp.wait_send()
            cp.wait_recv()

    arrays = list(sources) + list(landings)
    res = pl.pallas_call(
        body, name=name, out_shape=[pltpu.HBM(a.shape, a.dtype) for a in arrays],
        in_specs=[IN_HBM] * len(arrays) + [IN_SEM] * (2 * n_copies) + [ANY], out_specs=[IN_HBM] * len(arrays),
        input_output_aliases={i: i for i in range(len(arrays))},
        compiler_params=SPLIT_COPY,
    )(*arrays, *sems, after)
    return res[n_src:]


def _chips_plan(n):
    def plan(srcs, lands):
        x, y, c = _position()
        return [(srcs[a].at[:, 2 * cx + cy], lands[a].at[:, k], (cx, cy, c))
                for k, (cx, cy) in enumerate(_other_chips(x, y)) for a in range(n)]
    return plan


def _alone(rider, name, after=()):
    return _hosted([rider], lambda *refs: None, name=name, in_specs=[ANY] * len(after), out_specs=[], out_shape=[])(*after)[1][0]


def _add_pair(mine, theirs, c, tag):
    n = len(mine)

    def body(c_ref, *refs):
        for a in range(n):
            refs[2 * n + a][...] = (refs[2 * a][...].astype(F32) + refs[2 * a + 1][...].astype(F32)).astype(BF16)

    in_specs, out_specs = [], []
    for got in theirs:
        l, _, hr, cols = got.shape
        in_specs += [pl.BlockSpec((l, None, None, hr, cols), lambda j, s: (0, j, s[0], 0, 0)),
                     pl.BlockSpec((l, None, hr, cols), lambda j, s: (0, j, 0, 0))]
        out_specs.append(pl.BlockSpec((l, None, hr, cols), lambda j, s: (0, j, 0, 0)))
    return pl.pallas_call(
        body, name=f"add_pair_{tag}",
        grid_spec=pltpu.PrefetchScalarGridSpec(num_scalar_prefetch=1, grid=(N_CHIPS,), in_specs=in_specs, out_specs=out_specs),
        out_shape=[jax.ShapeDtypeStruct(got.shape, BF16) for got in theirs],
        compiler_params=_params(("parallel",)),
    )(c.reshape(1), *[a for pair in zip(mine, theirs) for a in pair])


def _add_chips(parts, received, mine, tag):
    n = len(parts)

    def body(j_ref, *refs):
        for a in range(n):
            acc = refs[2 * a][...].astype(F32)
            for k in range(3):
                acc += refs[2 * a + 1][:, k].astype(F32)
            refs[2 * n + a][...] = acc

    in_specs, out_specs, out_shape = [], [], []
    for got in received:
        l, _, hr, cols = got.shape
        in_specs += [pl.BlockSpec((l, None, hr // 2, cols), lambda i, s: (0, s[0], i, 0)),
                     pl.BlockSpec((l, 3, hr // 2, cols), lambda i, s: (0, 0, i, 0))]
        out_specs.append(pl.BlockSpec((l, hr // 2, cols), lambda i, s: (0, i, 0)))
        out_shape.append(jax.ShapeDtypeStruct((l, hr, cols), F32))
    return pl.pallas_call(
        body, name=f"add_chips_{tag}",
        grid_spec=pltpu.PrefetchScalarGridSpec(num_scalar_prefetch=1, grid=(2,), in_specs=in_specs, out_specs=out_specs),
        out_shape=out_shape,
        compiler_params=_params(("parallel",)),
    )(mine.reshape(1), *[a for pair in zip(parts, received) for a in pair])


def _adam_update(w, g, m, v):
    m2 = ADAM_B1 * m + (1.0 - ADAM_B1) * g
    v2 = ADAM_B2 * v + (1.0 - ADAM_B2) * jnp.square(g)
    m_hat = m2 / (1.0 - ADAM_B1 ** ADAM_STEP)
    v_hat = v2 / (1.0 - ADAM_B2 ** ADAM_STEP)
    return -ADAM_LR * (m_hat / (jnp.sqrt(v_hat) + ADAM_EPS) + ADAM_WD * w), m2, v2


def _adamw_group(items, tag, after=()):
    n = len(items)

    def body(*refs):
        ins, outs = refs[:5 * n], refs[5 * n + len(after):]
        mine = pl.program_id(0) == lax.axis_index("c")
        for a in range(n):
            w_ref, own_ref, other_ref, m_ref, v_ref = ins[5 * a:5 * a + 5]
            g_ref, d_ref, nm_ref, nv_ref = outs[4 * a:4 * a + 4]
            gv = jnp.where(mine, own_ref[...], other_ref[...])
            g_ref[...] = gv
            d_ref[...], nm_ref[...], nv_ref[...] = _adam_update(w_ref[...], gv, m_ref[...], v_ref[...])

    in_specs, out_specs, out_shape, args = [], [], [], []
    for w, own, other, m, v in items:
        _, hr, cols = w.shape
        tr = hr // ADAM_BLOCKS
        full = pl.BlockSpec((None, tr, cols), lambda h, i: (h, i, 0))
        half = pl.BlockSpec((tr, cols), lambda h, i: (i, 0))
        in_specs += [full, half, half, full, full]
        out_specs += [full] * 4
        out_shape += [jax.ShapeDtypeStruct((2, hr, cols), F32)] * 4
        args += [w, own, other, m, v]
    outs = pl.pallas_call(body, name=f"adamw_{tag}", grid=(2, ADAM_BLOCKS), in_specs=in_specs + [ANY] * len(after),
                          out_specs=out_specs, out_shape=out_shape,
                          compiler_params=_params(("parallel", "parallel")))(*args, *after)
    return [outs[4 * a:4 * a + 4] for a in range(n)]


def _adamw_small(w, gathered, m, v, shapes):
    n_rows = w.shape[0]
    places = []
    for i, name in enumerate(VECTOR_PARAMS):
        places.append((name, i * TILE_ROWS, 1 if len(shapes[name]) == 1 else shapes[name][0], shapes[name][-1]))
    places.append(("pool_w", len(VECTOR_PARAMS) * TILE_ROWS, n_rows - len(VECTOR_PARAMS) * TILE_ROWS, D_MODEL))

    def body(w_ref, g_ref, m_ref, v_ref, loss_ref, *rest):
        outs, (sum_scr, d_scr, nm_scr, nv_scr) = rest[:-4], rest[-4:]
        total = g_ref[0]
        for i in range(1, g_ref.shape[0]):
            total += g_ref[i]
        sum_scr[...] = total
        gv = sum_scr[0:n_rows, :]
        d_scr[...], nm_scr[...], nv_scr[...] = _adam_update(w_ref[...], gv, m_ref[...], v_ref[...])
        loss_ref[...] = sum_scr[n_rows:n_rows + 1, 0:1]
        for k, (_, first, rows, cols) in enumerate(places):
            for j, scr in enumerate((sum_scr, d_scr, nm_scr, nv_scr)):
                outs[4 * k + j][...] = scr[first:first + rows, 0:cols]

    out_shape = [jax.ShapeDtypeStruct((1, 1), F32)]
    for _, _, rows, cols in places:
        out_shape += [jax.ShapeDtypeStruct((rows, cols), F32)] * 4
    res = pl.pallas_call(
        body, name="adamw_small", out_shape=out_shape,
        scratch_shapes=[pltpu.VMEM(gathered.shape[1:], F32)] + [pltpu.VMEM(w.shape, F32)] * 3,
        compiler_params=_params())(w, gathered, m, v)
    return res[0], {name: res[1 + 4 * k:5 + 4 * k] for k, (name, _, _, _) in enumerate(places)}


VECTOR_PARAMS = ("ffn1_norm", "mix_norm", "hgrn_lb", "hgrn_onorm", "ffn2_norm", "ple_norm", "ple_post_norm", "final_norm",
                 "pool_scale")
ALL_PARAMS = ("ffn1_norm", "ffn1_w1", "ffn1_w3", "ffn1_w2", "mix_norm", "w_in", "hgrn_lb", "hgrn_onorm", "w_branch_a",
              "pool_w", "pool_scale", "w_branch_b", "w_out", "ffn2_norm", "ffn2_w1", "ffn2_w3", "ffn2_w2", "ple_norm",
              "ple_w_gate", "ple_w_proj", "ple_post_norm", "final_norm")
TILE_ROWS = 8


def _pack_small(values, loss=None):
    tile = lambda a: jnp.pad(a, ((0, TILE_ROWS - a.shape[0]), (0, D_MODEL - a.shape[1])))
    parts = [tile(values[name].reshape(-1, values[name].shape[-1])) for name in VECTOR_PARAMS]
    parts.append(values["pool_w"].reshape(-1, D_MODEL))
    if loss is not None:
        parts.append(tile(loss))
    return jnp.concatenate(parts, axis=0)


def _halved(a, lead):
    return a.reshape(lead, 2, -1, a.shape[-1])


def _shard_halves(a, lead):
    return a.reshape(lead, N_CHIPS, 2, -1, a.shape[-1])


REDUCED_TRANSPOSED = ("ffn1_w1", "ffn1_w3", "ffn2_w1", "ffn2_w3")


def _entries(arrays):
    return [a[i] for a in arrays for i in range(a.shape[0])]


def _adam_items(names, own, other, w, m, v):
    items = []
    for name, g_own, g_other in zip(names, _entries(own), _entries(other)):
        view = (lambda a: _halved(a[0].T, 1)[0]) if name in REDUCED_TRANSPOSED else (lambda a: _halved(a, 1)[0])
        items.append((view(w[name]), g_own, g_other, view(m[name]), view(v[name])))
    return items


def _adam_store(names, results, w, out):
    for name, res in zip(names, results):
        shape = w[name].shape
        if name in REDUCED_TRANSPOSED:
            back = [a.reshape(shape[2], shape[1]).T.reshape(shape) for a in res]
        else:
            back = [a.reshape(shape) for a in res]
        out["grad"][name], out["delta"][name], out["new_m"][name], out["new_v"][name] = back


def kernel(x, p, ffn1_norm, ffn1_w1, ffn1_w3, ffn1_w2, mix_norm, w_in, hgrn_lb, hgrn_onorm, w_branch_a, pool_w, pool_scale, w_branch_b, w_out, ffn2_norm, ffn2_w1, ffn2_w3, ffn2_w2, ple_norm, ple_w_gate, ple_w_proj, ple_post_norm, final_norm, loss_target, m_ffn1_norm, m_ffn1_w1, m_ffn1_w3, m_ffn1_w2, m_mix_norm, m_w_in, m_hgrn_lb, m_hgrn_onorm, m_w_branch_a, m_pool_w, m_pool_scale, m_w_branch_b, m_w_out, m_ffn2_norm, m_ffn2_w1, m_ffn2_w3, m_ffn2_w2, m_ple_norm, m_ple_w_gate, m_ple_w_proj, m_ple_post_norm, m_final_norm, v_ffn1_norm, v_ffn1_w1, v_ffn1_w3, v_ffn1_w2, v_mix_norm, v_w_in, v_hgrn_lb, v_hgrn_onorm, v_w_branch_a, v_pool_w, v_pool_scale, v_w_branch_b, v_w_out, v_ffn2_norm, v_ffn2_w1, v_ffn2_w3, v_ffn2_w2, v_ple_norm, v_ple_w_gate, v_ple_w_proj, v_ple_post_norm, v_final_norm):
    args = dict(locals())
    w = {name: args[name] for name in ALL_PARAMS}
    m = {name: args["m_" + name] for name in ALL_PARAMS}
    v = {name: args["v_" + name] for name in ALL_PARAMS}
    cx, cy, cc = _position()
    chip = (2 * cx + cy).astype(jnp.int32)
    core = cc.astype(jnp.int32)
    xs, ps, target = x[0], p[0, 0], loss_target[0]
    t = xs.shape[0]
    tm = min(256, t)
    tm_ffn = min(512, t)
    tt = min(512, t)
    tk = min(2048, t)
    small = {name: w[name] for name in VECTOR_PARAMS}
    small["final_norm"] = w["final_norm"].reshape(1, D_MODEL)
    pool_w0 = w["pool_w"][0]

    ffn_shard = lambda i: _halved(jnp.stack([w[f"ffn{i}_w1"][0].T, w[f"ffn{i}_w3"][0].T, w[f"ffn{i}_w2"][0]]).astype(BF16), 3)
    sq_shard = _halved(jnp.stack([w["w_branch_a"][0], w["w_out"][0], w["ple_w_gate"][0]]).astype(BF16), 3)
    win_shard, wb_shard, wp_shard = (_halved(w[n].astype(BF16), 1) for n in ("w_in", "w_branch_b", "ple_w_proj"))

    ffn1_shard, ffn2_shard = ffn_shard(1), ffn_shard(2)
    (ffn1w,) = _alone(_relay_gather_rider([ffn1_shard]), "gather_ffn1")
    ffn1w = _with_own(ffn1w, ffn1_shard, chip).reshape(3, D_FF, D_MODEL)
    (x1, a1, b1), ((winw,),) = _ffn_fwd(xs, small["ffn1_norm"], ffn1w, 1, tm_ffn, [_relay_gather_rider([win_shard])])
    winw = _with_own(winw, win_shard, chip).reshape(N_CHIPS, D_MODEL, SHARD_IN_COLS)
    (main, pool_r, gates), ((sqw, wbw, wpw),) = _mix_fwd(x1, small["mix_norm"], winw, tm,
                                                          [_relay_gather_rider([sq_shard, wb_shard, wp_shard])])
    sqw = _with_own(sqw, sq_shard, chip).reshape(3, D_MODEL, D_MODEL)
    wbw = _with_own(wbw, wb_shard, chip).reshape(N_CHIPS, POOL_WIDTH, -1)
    wpw = _with_own(wpw, wp_shard, chip).reshape(N_CHIPS, ps.shape[1], -1)
    (o, states), ((ffn2w,),) = _hgrn_fwd(main, small["hgrn_lb"], tt, [_relay_gather_rider([ffn2_shard])])
    ffn2w = _with_own(ffn2w, ffn2_shard, chip).reshape(3, D_FF, D_MODEL)
    (x2, ya, yb, pooled), _ = _post_fwd(o, main, pool_r, gates, x1, small["hgrn_onorm"], pool_w0, small["pool_scale"], sqw,
                                       wbw, tm)
    (x3, a2, b2), _ = _ffn_fwd(x2, small["ffn2_norm"], ffn2w, 2, tm_ffn)
    (dx3, loss, d_wg, d_wp, d_ple, d_post, d_final), _ = _tail(
        x3, ps, target, small["ple_norm"], small["ple_post_norm"], small["final_norm"], sqw, wpw, tm_ffn)

    add_pairs = lambda parts, got, group: _add_pair(parts, got, core, group)
    add_chips = lambda sums, got, group: _add_chips(sums, got, chip, group)
    names1 = ("ffn2_w1", "ffn2_w3", "ffn2_w2", "ple_w_gate", "ple_w_proj")
    names2 = ("w_branch_a", "w_out", "w_branch_b")
    names3 = ("w_in",)
    names4 = ("ffn1_w1", "ffn1_w3")
    names5 = ("ffn1_w2",)
    tags1, tags2, tags3, tags4, tags5 = "ffn2", "branches", "w_in", "ffn1_in", "ffn1_out"

    (dx2, dab2, s2, h3, dxh2, d_ffn2_norm), _ = _ffn_bwd(dx3, x2, small["ffn2_norm"], a2, b2, ffn2w, 2, tm)
    (d_w13_2,), _ = _wgrad(dab2, h3, WGRAD_IN_BLOCKS, "wgrad_ffn2_in", tk)
    (d_w2_2,), _ = _wgrad(s2, dxh2, WGRAD_OUT_BLOCKS, "wgrad_ffn2_out", tk)
    part1 = [_shard_halves(d_w13_2, 2), _shard_halves(d_w2_2, 1), _shard_halves(d_wg, 1), _shard_halves(d_wp, 1)]
    (do, dog, du, dgates, d_wa, d_wout, d_wb, d_pool_w, d_pool_scale, d_onorm), (sib1,) = _post_bwd(
        dx2, o, main, gates, ya, yb, pooled, small["hgrn_onorm"], pool_w0, small["pool_scale"], sqw, wbw, tm,
        [_pair_rider(part1)])
    sums1 = add_pairs(part1, sib1, tags1)
    part2 = [_shard_halves(d_wa, 1), _shard_halves(d_wout, 1), _shard_halves(d_wb, 1)]
    (dqfi, d_lb), (got1, sib2) = _hgrn_bwd(main, small["hgrn_lb"], states, do, tt, [_chips_rider(sums1), _pair_rider(part2)])
    own1 = add_chips(sums1, got1, tags1)
    sums2 = add_pairs(part2, sib2, tags2)
    (dx1, dproj, h2, d_mix_norm), (other1, got2) = _mix_bwd(dqfi, dog, du, dgates, dx2, x1, small["mix_norm"], winw, tm,
                                                            [_share_rider(own1), _chips_rider(sums2)])
    own2 = add_chips(sums2, got2, tags2)
    (d_win,), (other2,) = _wgrad_cols(h2, dproj, N_CHIPS, "wgrad_in", tk, [_share_rider(own2)])
    part3 = [_shard_halves(d_win, 1)]
    (dx, dab1, s1, h1, dxh1, d_ffn1_norm), _ = _ffn_bwd(dx1, xs, small["ffn1_norm"], a1, b1, ffn1w, 1, tm)
    vecs = dict(ffn1_norm=d_ffn1_norm, mix_norm=d_mix_norm, hgrn_lb=d_lb, hgrn_onorm=d_onorm, ffn2_norm=d_ffn2_norm,
                ple_norm=d_ple, ple_post_norm=d_post, final_norm=d_final, pool_scale=d_pool_scale, pool_w=d_pool_w)
    (d_w2_1,), (sib3, (small_all,)) = _wgrad(s1, dxh1, WGRAD_OUT_BLOCKS, "wgrad_ffn1_out", tk,
                                             [_pair_rider(part3), _small_rider(_pack_small(vecs, loss))])
    sums3 = add_pairs(part3, sib3, tags3)
    part5 = [_shard_halves(d_w2_1, 1)]
    (d_w13_1,), (got3, sib5) = _wgrad(dab1, h1, WGRAD_IN_BLOCKS, "wgrad_ffn1_in", tk,
                                      [_chips_rider(sums3), _pair_rider(part5)])
    own3 = add_chips(sums3, got3, tags3)
    sums5 = add_pairs(part5, sib5, tags5)
    part4 = [_shard_halves(d_w13_1, 2)]
    landing = lambda a: jax.ShapeDtypeStruct((a.shape[0], 3) + a.shape[2:], a.dtype)

    def sibling_plan(srcs, lands):
        x, y, c = _position()
        return [(srcs[-1], lands[-1], (x, y, 1 - c))]

    plan_a = lambda srcs, lands: _chips_plan(1)(srcs[:1], lands[:1]) + sibling_plan(srcs, lands)
    started_a = _start_copies([sums5[0], own3[0]], [landing(sums5[0]), own3[0]], 4, plan_a, "start_a")
    sib4 = _alone(_pair_rider(part4), "pair_last", [started_a[3]])
    sums4 = add_pairs(part4, sib4, tags4)
    started_b = _start_copies([sums4[0]], [landing(sums4[0])], 3, _chips_plan(1), "start_b")

    out = dict(grad={}, delta={}, new_m={}, new_v={})
    results = _adamw_group(_adam_items(names1 + names2, own1 + own2, other1 + other2, w, m, v), "early",
                           [started_a[3], started_b[3]])
    _adam_store(names1 + names2, results, w, out)
    got5, other3 = _wait_copies(started_a, 4, plan_a, results[0][1], "wait_a")
    results = _adamw_group(_adam_items(names3, own3, [other3], w, m, v), "w_in")
    _adam_store(names3, results, w, out)
    (got4,) = _wait_copies(started_b, 3, _chips_plan(1), results[0][1], "wait_b")
    own4 = add_chips(sums4, [got4], tags4)
    own5 = add_chips(sums5, [got5], tags5)
    other4, other5 = _alone(_share_rider(own4 + own5), "share_last")
    results = _adamw_group(_adam_items(names4 + names5, own4 + own5, [other4, other5], w, m, v), "ffn1")
    _adam_store(names4 + names5, results, w, out)

    shapes = {name: w[name].shape for name in VECTOR_PARAMS + ("pool_w",)}
    loss, results = _adamw_small(_pack_small(w), small_all, _pack_small(m), _pack_small(v), shapes)
    for name, res in results.items():
        out["grad"][name], out["delta"][name], out["new_m"][name], out["new_v"][name] = (a.reshape(shapes[name]) for a in res)

    return (loss[0, 0], dx[None], *[out["grad"][n] for n in ALL_PARAMS], *[out["delta"][n] for n in ALL_PARAMS],
            *[out["new_m"][n] for n in ALL_PARAMS], *[out["new_v"][n] for n in ALL_PARAMS])
```

```python
import functools

import jax
import jax.numpy as jnp
from jax import lax
from jax.experimental import pallas as pl
from jax.experimental.pallas import tpu as pltpu

F32 = jnp.float32
BF16 = jnp.bfloat16
MESH = pl.DeviceIdType.MESH

D_MODEL = 1024
D_FF = 2816
HEADS = 8
HEAD_DIM = 128
POOL_WIDTH = 512
POOL_WINDOWS = (2, 4, 8, 16)
POOL_HALO = 16
N_CHIPS = 4
EPS = 1e-6
CHUNK = 64
MAIN_COLS = 4096
GATE_COLS = 2048
SHARD_IN_COLS = 1664

ADAM_LR = 0.001
ADAM_B1 = 0.9
ADAM_B2 = 0.999
ADAM_EPS = 1e-08
ADAM_WD = 0.01
ADAM_STEP = 10

VMEM_LIMIT = 56 * 1024 * 1024
WGRAD_IN_BLOCKS = 4
WGRAD_OUT_BLOCKS = 2
ADAM_BLOCKS = 4


def _params(semantics=None, vmem=VMEM_LIMIT):
    return pltpu.CompilerParams(dimension_semantics=semantics, vmem_limit_bytes=vmem)


def _dot(a, b):
    return jnp.dot(a, b, preferred_element_type=F32)


def _dot_nt(a, b):
    return lax.dot_general(a, b, (((1,), (1,)), ((), ())), preferred_element_type=F32)


def _dot_tn(a, b):
    return lax.dot_general(a, b, (((0,), (0,)), ((), ())), preferred_element_type=F32)


def _tri_sum(tri, x):
    hi = x.astype(BF16)
    lo = (x - hi.astype(F32)).astype(BF16)
    return _dot(tri, hi) + _dot(tri, lo)


def _sigmoid(x):
    return jax.nn.sigmoid(x)


def _resident(shape):
    zeros = (0,) * len(shape)
    return pl.BlockSpec(shape, lambda *_: zeros, pipeline_mode=pl.Buffered(1))


def _pick(shape, k):
    zeros = (0,) * (len(shape) - 1)
    return pl.BlockSpec((None,) + tuple(shape[1:]), lambda *_: (k,) + zeros, pipeline_mode=pl.Buffered(1))


def _rows(tm, cols, col_block=0):
    return pl.BlockSpec((tm, cols), lambda i: (i, col_block))


def _acc(shape):
    zeros = (0,) * len(shape)
    return pl.BlockSpec(shape, lambda *_: zeros)


def _rms(x):
    r = lax.rsqrt(jnp.mean(x * x, axis=-1, keepdims=True) + EPS)
    return r, x * r


def _rms_bwd(dn, n, r):
    return r * (dn - n * jnp.mean(dn * n, axis=-1, keepdims=True))


def _colsum(a):
    return jnp.sum(a, axis=0, keepdims=True)


ANY = pl.BlockSpec(memory_space=pl.ANY)


class _Rider:
    def __init__(self, inputs, out_shape, sems, phases):
        self.inputs, self.out_shape, self.sems, self.phases = list(inputs), list(out_shape), list(sems), list(phases)


def _hosted(riders, body, *, name, grid=(), in_specs, out_specs, out_shape, scratch_shapes=(), compiler_params=None):
    riders = [r for r in riders if r is not None]
    n_in, n_out, n_scr = len(in_specs), len(out_shape), len(scratch_shapes)
    n_steps = 1
    for g in grid:
        n_steps *= g

    def wrapped(*refs):
        pos = n_in
        ins = refs[:n_in]
        r_ins = []
        for r in riders:
            r_ins.append(refs[pos:pos + len(r.inputs)])
            pos += len(r.inputs)
        outs = refs[pos:pos + n_out]
        pos += n_out
        r_outs = []
        for r in riders:
            r_outs.append(refs[pos:pos + len(r.out_shape)])
            pos += len(r.out_shape)
        scr = refs[pos:pos + n_scr]
        pos += n_scr
        r_sems = []
        for r in riders:
            r_sems.append(refs[pos:pos + len(r.sems)])
            pos += len(r.sems)
        step = 0
        for axis in range(len(grid)):
            step = step * grid[axis] + pl.program_id(axis)

        def at_step(which, fn):
            if n_steps == 1:
                fn()
            else:
                pl.when(step == which)(fn)

        for r, ri, ro, rs in zip(riders, r_ins, r_outs, r_sems):
            for fraction, fn in r.phases:
                if fraction == 0:
                    at_step(0, functools.partial(fn, ri, ro, rs))
        body(*ins, *outs, *scr)
        for r, ri, ro, rs in zip(riders, r_ins, r_outs, r_sems):
            for fraction, fn in r.phases:
                if fraction > 0:
                    at_step(min(int(fraction * n_steps), n_steps - 1), functools.partial(fn, ri, ro, rs))

    call = pl.pallas_call(
        wrapped, name=name, grid=grid,
        in_specs=list(in_specs) + [ANY for r in riders for _ in r.inputs],
        out_specs=list(out_specs) + [ANY for r in riders for _ in r.out_shape],
        out_shape=list(out_shape) + [s for r in riders for s in r.out_shape],
        scratch_shapes=list(scratch_shapes) + [s for r in riders for s in r.sems],
        compiler_params=compiler_params)

    def run(*args):
        res = call(*args, *[a for r in riders for a in r.inputs])
        extras, pos = [], n_out
        for r in riders:
            extras.append(list(res[pos:pos + len(r.out_shape)]))
            pos += len(r.out_shape)
        return list(res[:n_out]), extras

    return run


def _ffn_fwd(x, g, ffnw, tag, tm, riders=()):
    t = x.shape[0]

    def body(x_ref, g_ref, w1_ref, w3_ref, w2_ref, xo_ref, a_ref, b_ref):
        xv = x_ref[...]
        _, n = _rms(xv)
        h = (n * g_ref[...]).astype(BF16)
        a = _dot_nt(h, w1_ref[...])
        b = _dot_nt(h, w3_ref[...])
        s = (a * _sigmoid(a) * b).astype(BF16)
        xo_ref[...] = xv + 0.5 * _dot(s, w2_ref[...])
        a_ref[...] = a.astype(BF16)
        b_ref[...] = b.astype(BF16)

    return _hosted(
        riders, body, name=f"ffn_fwd_{tag}", grid=(t // tm,),
        in_specs=[_rows(tm, D_MODEL), _resident((1, D_MODEL)), _pick(ffnw.shape, 0), _pick(ffnw.shape, 1),
                  _pick(ffnw.shape, 2)],
        out_specs=[_rows(tm, D_MODEL), _rows(tm, D_FF), _rows(tm, D_FF)],
        out_shape=[jax.ShapeDtypeStruct((t, D_MODEL), F32), jax.ShapeDtypeStruct((t, D_FF), BF16),
                   jax.ShapeDtypeStruct((t, D_FF), BF16)],
        compiler_params=_params(("arbitrary",)),
    )(x, g, ffnw, ffnw, ffnw)


def _ffn_bwd(dxo, x, g, a, b, ffnw, tag, tm, riders=()):
    t = x.shape[0]

    def body(dxo_ref, x_ref, g_ref, a_ref, b_ref, w1_ref, w3_ref, w2_ref, dx_ref, dab_ref, s_ref, h_ref, dxh_ref, dg_ref):
        @pl.when(pl.program_id(0) == 0)
        def _():
            dg_ref[...] = jnp.zeros_like(dg_ref)

        xv = x_ref[...]
        gv = g_ref[...]
        r, n = _rms(xv)
        h_ref[...] = (n * gv).astype(BF16)
        dxo_v = dxo_ref[...]
        dxh = (0.5 * dxo_v).astype(BF16)
        dxh_ref[...] = dxh
        ds = _dot_nt(dxh, w2_ref[...])
        av = a_ref[...].astype(F32)
        bv = b_ref[...].astype(F32)
        sg = _sigmoid(av)
        silu = av * sg
        s_ref[...] = (silu * bv).astype(BF16)
        da = (ds * bv * (sg * (1.0 + av * (1.0 - sg)))).astype(BF16)
        db = (ds * silu).astype(BF16)
        dab_ref[:, :D_FF] = da
        dab_ref[:, D_FF:] = db
        dh = _dot(da, w1_ref[...]) + _dot(db, w3_ref[...])
        dg_ref[...] += _colsum(dh * n)
        dx_ref[...] = dxo_v + _rms_bwd(dh * gv, n, r)

    return _hosted(
        riders, body, name=f"ffn_bwd_{tag}", grid=(t // tm,),
        in_specs=[_rows(tm, D_MODEL), _rows(tm, D_MODEL), _resident((1, D_MODEL)), _rows(tm, D_FF), _rows(tm, D_FF),
                  _pick(ffnw.shape, 0), _pick(ffnw.shape, 1), _pick(ffnw.shape, 2)],
        out_specs=[_rows(tm, D_MODEL), _rows(tm, 2 * D_FF), _rows(tm, D_FF), _rows(tm, D_MODEL), _rows(tm, D_MODEL),
                   _acc((1, D_MODEL))],
        out_shape=[jax.ShapeDtypeStruct((t, D_MODEL), F32), jax.ShapeDtypeStruct((t, 2 * D_FF), BF16),
                   jax.ShapeDtypeStruct((t, D_FF), BF16), jax.ShapeDtypeStruct((t, D_MODEL), BF16),
                   jax.ShapeDtypeStruct((t, D_MODEL), BF16), jax.ShapeDtypeStruct((1, D_MODEL), F32)],
        compiler_params=_params(("arbitrary",)),
    )(dxo, x, g, a, b, ffnw, ffnw, ffnw)


def _wgrad_body(n_token_tiles):
    def body(x_ref, dy_ref, o_ref, acc):
        k = pl.program_id(1)

        @pl.when(k == 0)
        def _():
            acc[...] = jnp.zeros_like(acc)

        acc[...] += _dot_tn(x_ref[...], dy_ref[...])

        @pl.when(k == n_token_tiles - 1)
        def _():
            o_ref[...] = acc[...].astype(BF16)

    return body


def _wgrad(xm, dy, out_blocks, name, tk, riders=()):
    t, m = xm.shape
    n = dy.shape[1]
    mb = m // out_blocks

    return _hosted(
        riders, _wgrad_body(t // tk), name=name, grid=(out_blocks, t // tk),
        in_specs=[pl.BlockSpec((tk, mb), lambda j, k: (k, j)), pl.BlockSpec((tk, n), lambda j, k: (k, 0))],
        out_specs=[pl.BlockSpec((None, mb, n), lambda j, k: (j, 0, 0))],
        out_shape=[jax.ShapeDtypeStruct((out_blocks, mb, n), BF16)],
        scratch_shapes=[pltpu.VMEM((mb, n), F32)],
        compiler_params=_params(("arbitrary", "arbitrary")),
    )(xm, dy)


def _wgrad_cols(xm, dy, out_blocks, name, tk, riders=()):
    t, m = xm.shape
    n = dy.shape[1]
    nb = n // out_blocks

    return _hosted(
        riders, _wgrad_body(t // tk), name=name, grid=(out_blocks, t // tk),
        in_specs=[pl.BlockSpec((tk, m), lambda j, k: (k, 0)), pl.BlockSpec((tk, nb), lambda j, k: (k, j))],
        out_specs=[pl.BlockSpec((None, m, nb), lambda j, k: (j, 0, 0))],
        out_shape=[jax.ShapeDtypeStruct((out_blocks, m, nb), BF16)],
        scratch_shapes=[pltpu.VMEM((m, nb), F32)],
        compiler_params=_params(("arbitrary", "arbitrary")),
    )(xm, dy)


def _mix_fwd(x1, g, winw, tm, riders=()):
    t = x1.shape[0]

    def body(x_ref, g_ref, w_ref, main_ref, pool_ref, gate_ref):
        _, n = _rms(x_ref[...])
        h = (n * g_ref[...]).astype(BF16)
        proj = jnp.concatenate([_dot(h, w_ref[j]) for j in range(N_CHIPS)], axis=1)
        main_ref[...] = proj[:, :MAIN_COLS]
        pool_ref[...] = proj[:, MAIN_COLS:MAIN_COLS + POOL_WIDTH]
        gate_ref[...] = proj[:, MAIN_COLS + POOL_WIDTH:]

    return _hosted(
        riders, body, name="mix_fwd", grid=(t // tm,),
        in_specs=[_rows(tm, D_MODEL), _resident((1, D_MODEL)), _resident(winw.shape)],
        out_specs=[_rows(tm, MAIN_COLS), _rows(tm, POOL_WIDTH), _rows(tm, GATE_COLS)],
        out_shape=[jax.ShapeDtypeStruct((t, MAIN_COLS), F32), jax.ShapeDtypeStruct((t, POOL_WIDTH), F32),
                   jax.ShapeDtypeStruct((t, GATE_COLS), F32)],
        compiler_params=_params(("arbitrary",)),
    )(x1, g, winw)


def _mix_bwd(dqfi, dog, du, dgates, dx2, x1, g, winw, tm, riders=()):
    t = x1.shape[0]
    cols = N_CHIPS * SHARD_IN_COLS

    def body(dqfi_ref, dog_ref, du_ref, dgt_ref, dx2_ref, x_ref, g_ref, w_ref, dx_ref, dproj_ref, h_ref, dg_ref):
        @pl.when(pl.program_id(0) == 0)
        def _():
            dg_ref[...] = jnp.zeros_like(dg_ref)

        dproj = jnp.concatenate([dqfi_ref[...], dog_ref[...], du_ref[...], dgt_ref[...]], axis=1)
        dproj_ref[...] = dproj
        dh = _dot_nt(dproj[:, :SHARD_IN_COLS], w_ref[0])
        for j in range(1, N_CHIPS):
            dh += _dot_nt(dproj[:, j * SHARD_IN_COLS:(j + 1) * SHARD_IN_COLS], w_ref[j])
        gv = g_ref[...]
        r, n = _rms(x_ref[...])
        h_ref[...] = (n * gv).astype(BF16)
        dg_ref[...] += _colsum(dh * n)
        dx_ref[...] = dx2_ref[...] + _rms_bwd(dh * gv, n, r)

    return _hosted(
        riders, body, name="mix_bwd", grid=(t // tm,),
        in_specs=[_rows(tm, 3 * D_MODEL), _rows(tm, D_MODEL), _rows(tm, POOL_WIDTH), _rows(tm, GATE_COLS),
                  _rows(tm, D_MODEL), _rows(tm, D_MODEL), _resident((1, D_MODEL)), _resident(winw.shape)],
        out_specs=[_rows(tm, D_MODEL), _rows(tm, cols), _rows(tm, D_MODEL), _acc((1, D_MODEL))],
        out_shape=[jax.ShapeDtypeStruct((t, D_MODEL), F32), jax.ShapeDtypeStruct((t, cols), BF16),
                   jax.ShapeDtypeStruct((t, D_MODEL), BF16), jax.ShapeDtypeStruct((1, D_MODEL), F32)],
        compiler_params=_params(("arbitrary",)),
    )(dqfi, dog, du, dgates, dx2, x1, g, winw)


def _lower_bound(lb_raw):
    l0 = lb_raw[0:1, :]
    l1 = lb_raw[1:2, :]
    m = jnp.maximum(l0, l1)
    e0 = jnp.exp(l0 - m)
    e1 = jnp.exp(l1 - m)
    return e0 / (e0 + e1)


def _head_slices():
    return [slice(h * HEAD_DIM, (h + 1) * HEAD_DIM) for h in range(HEADS)]


def _gates(qr, fr, lb, tril_b, first_half):
    sg = _sigmoid(fr)
    f = lb + (1.0 - lb) * sg
    k = 1.0 - f
    sq = _sigmoid(qr)
    q = qr * sq
    log_f = jnp.log(f)
    gc = _tri_sum(tril_b, log_f)
    gm = _colsum(jnp.where(first_half, log_f, 0.0))
    gl = _colsum(log_f)
    e_q = jnp.exp(gc - gm)
    e_k = jnp.exp(gm - gc)
    e_in = jnp.exp(gc)
    e_out = jnp.exp(gl - gc)
    return dict(sg=sg, f=f, k=k, sq=sq, q=q, e_q=e_q, e_k=e_k, e_in=e_in, e_out=e_out, e_last=jnp.exp(gl))


def _hgrn_fwd(main, lb_raw, tt, riders=()):
    t = main.shape[0]
    n_local = tt // CHUNK

    def body(q_ref, f_ref, i_ref, lb_ref, o_ref, st_ref, s_scr):
        @pl.when(pl.program_id(0) == 0)
        def _():
            s_scr[...] = jnp.zeros_like(s_scr)

        lb = _lower_bound(lb_ref[...])
        row = lax.broadcasted_iota(jnp.int32, (CHUNK, CHUNK), 0)
        col = lax.broadcasted_iota(jnp.int32, (CHUNK, CHUNK), 1)
        tril = row >= col
        tril_b = tril.astype(BF16)
        first_half = lax.broadcasted_iota(jnp.int32, (CHUNK, D_MODEL), 0) < CHUNK // 2
        heads = _head_slices()

        def chunk(c, carry):
            rows = pl.ds(pl.multiple_of(c * CHUNK, CHUNK), CHUNK)
            z = _gates(q_ref[rows, :], f_ref[rows, :], lb, tril_b, first_half)
            qt = (z["q"] * z["e_q"]).astype(BF16)
            kt = (z["k"] * z["e_k"]).astype(BF16)
            qg = (z["q"] * z["e_in"]).astype(BF16)
            kg = (z["k"] * z["e_out"]).astype(BF16)
            vb = i_ref[rows, :].astype(BF16)
            states = [s_scr[h] for h in range(HEADS)]
            for h in range(HEADS):
                st_ref[c, h] = states[h]
            raw = [_dot_nt(qt[:, sl], kt[:, sl]) for sl in heads]
            inter = [_dot_nt(qg[:, sl], states[h].astype(BF16)) for h, sl in enumerate(heads)]
            grown = [_dot_tn(vb[:, sl], kg[:, sl]) for sl in heads]
            scores = [jnp.where(tril, r, 0.0).astype(BF16) for r in raw]
            for h, sl in enumerate(heads):
                s_scr[h] = states[h] * z["e_last"][:, sl] + grown[h]
            o_ref[rows, :] = jnp.concatenate([_dot(scores[h], vb[:, sl]) + inter[h] for h, sl in enumerate(heads)], axis=1)
            return carry

        lax.fori_loop(0, n_local, chunk, 0, unroll=True)

    return _hosted(
        riders, body, name="hgrn_fwd", grid=(t // tt,),
        in_specs=[_rows(tt, D_MODEL, 0), _rows(tt, D_MODEL, 1), _rows(tt, D_MODEL, 2), _resident((2, D_MODEL))],
        out_specs=[_rows(tt, D_MODEL),
                   pl.BlockSpec((n_local, HEADS, HEAD_DIM, HEAD_DIM), lambda i: (i, 0, 0, 0))],
        out_shape=[jax.ShapeDtypeStruct((t, D_MODEL), F32),
                   jax.ShapeDtypeStruct((t // CHUNK, HEADS, HEAD_DIM, HEAD_DIM), F32)],
        scratch_shapes=[pltpu.VMEM((HEADS, HEAD_DIM, HEAD_DIM), F32)],
        compiler_params=_params(("arbitrary",)),
    )(main, main, main, lb_raw)


def _hgrn_bwd(main, lb_raw, states, do, tt, riders=()):
    t = main.shape[0]
    n_tiles = t // tt
    n_local = tt // CHUNK

    def rev(col_block):
        return pl.BlockSpec((tt, D_MODEL), lambda i: (n_tiles - 1 - i, col_block))

    def body(q_ref, f_ref, i_ref, lb_ref, st_ref, do_ref, dqfi_ref, dlb_ref, ds_scr, acc_scr):
        @pl.when(pl.program_id(0) == 0)
        def _():
            ds_scr[...] = jnp.zeros_like(ds_scr)
            acc_scr[...] = jnp.zeros_like(acc_scr)

        lb = _lower_bound(lb_ref[...])
        row = lax.broadcasted_iota(jnp.int32, (CHUNK, CHUNK), 0)
        col = lax.broadcasted_iota(jnp.int32, (CHUNK, CHUNK), 1)
        tril = row >= col
        tril_b = tril.astype(BF16)
        triu_b = (row <= col).astype(BF16)
        first_half = lax.broadcasted_iota(jnp.int32, (CHUNK, D_MODEL), 0) < CHUNK // 2
        heads = _head_slices()
        cat = functools.partial(jnp.concatenate, axis=1)

        def chunk(cc, carry):
            c = n_local - 1 - cc
            rows = pl.ds(pl.multiple_of(c * CHUNK, CHUNK), CHUNK)
            qr = q_ref[rows, :]
            z = _gates(qr, f_ref[rows, :], lb, tril_b, first_half)
            qt = (z["q"] * z["e_q"]).astype(BF16)
            kt = (z["k"] * z["e_k"]).astype(BF16)
            qg_f = z["q"] * z["e_in"]
            qg = qg_f.astype(BF16)
            kg_f = z["k"] * z["e_out"]
            kg = kg_f.astype(BF16)
            vb = i_ref[rows, :].astype(BF16)
            dob = do_ref[rows, :].astype(BF16)
            st = [st_ref[c, h] for h in range(HEADS)]
            dst = [ds_scr[h] for h in range(HEADS)]
            dst_b = [d.astype(BF16) for d in dst]
            raw = [_dot_nt(qt[:, sl], kt[:, sl]) for sl in heads]
            draw = [_dot_nt(dob[:, sl], vb[:, sl]) for sl in heads]
            dqg = [_dot(dob[:, sl], st[h].astype(BF16)) for h, sl in enumerate(heads)]
            dkg = [_dot(vb[:, sl], dst_b[h]) for h, sl in enumerate(heads)]
            dv_inter = [_dot_nt(kg[:, sl], dst_b[h]) for h, sl in enumerate(heads)]
            grown = [_dot_tn(dob[:, sl], qg[:, sl]) for sl in heads]
            scores = [jnp.where(tril, r, 0.0).astype(BF16) for r in raw]
            dscores = [jnp.where(tril, r, 0.0).astype(BF16) for r in draw]
            dqt = [_dot(dscores[h], kt[:, sl]) for h, sl in enumerate(heads)]
            dkt = [_dot_tn(dscores[h], qt[:, sl]) for h, sl in enumerate(heads)]
            dv = [_dot_tn(scores[h], dob[:, sl]) + dv_inter[h] for h, sl in enumerate(heads)]
            carry_in = cat([z["e_last"][:, sl] * _colsum(dst[h] * st[h]) for h, sl in enumerate(heads)])
            for h, sl in enumerate(heads):
                ds_scr[h] = dst[h] * z["e_last"][:, sl] + grown[h]
            dqt, dkt, dqg, dkg = cat(dqt), cat(dkt), cat(dqg), cat(dkg)
            carry_in += _colsum(dkg * kg_f)
            dq = dqt * z["e_q"] + dqg * z["e_in"]
            dk = dkt * z["e_k"] + dkg * z["e_out"]
            dgate = (qt.astype(F32) * dqt - kt.astype(F32) * dkt) + (qg_f * dqg - kg_f * dkg)
            dlogf = _tri_sum(triu_b, dgate) + carry_in
            df = dlogf / z["f"] - dk
            sg = z["sg"]
            sq = z["sq"]
            acc_scr[...] += _colsum(df * (1.0 - sg))
            dqfi_ref[rows, 0:D_MODEL] = (dq * (sq * (1.0 + qr * (1.0 - sq)))).astype(BF16)
            dqfi_ref[rows, D_MODEL:2 * D_MODEL] = (df * (1.0 - lb) * sg * (1.0 - sg)).astype(BF16)
            dqfi_ref[rows, 2 * D_MODEL:3 * D_MODEL] = cat(dv).astype(BF16)
            return carry

        lax.fori_loop(0, n_local, chunk, 0, unroll=True)
        d0 = acc_scr[...] * lb * (1.0 - lb)
        dlb_ref[0:1, :] = d0
        dlb_ref[1:2, :] = -d0

    return _hosted(
        riders, body, name="hgrn_bwd", grid=(n_tiles,),
        in_specs=[rev(0), rev(1), rev(2), _resident((2, D_MODEL)),
                  pl.BlockSpec((n_local, HEADS, HEAD_DIM, HEAD_DIM), lambda i: (n_tiles - 1 - i, 0, 0, 0)),
                  rev(0)],
        out_specs=[pl.BlockSpec((tt, 3 * D_MODEL), lambda i: (n_tiles - 1 - i, 0)), _acc((2, D_MODEL))],
        out_shape=[jax.ShapeDtypeStruct((t, 3 * D_MODEL), BF16), jax.ShapeDtypeStruct((2, D_MODEL), F32)],
        scratch_shapes=[pltpu.VMEM((HEADS, HEAD_DIM, HEAD_DIM), F32), pltpu.VMEM((1, D_MODEL), F32)],
        compiler_params=_params(("arbitrary",)),
    )(main, main, main, lb_raw, states, do)


def _head_norm(o):
    rs, ns = [], []
    for h in range(HEADS):
        oh = o[:, h * HEAD_DIM:(h + 1) * HEAD_DIM]
        r, n = _rms(oh)
        rs.append(jnp.broadcast_to(r, oh.shape))
        ns.append(n)
    return jnp.concatenate(rs, axis=1), jnp.concatenate(ns, axis=1)


def _head_norm_bwd(dn, n, r):
    outs = []
    for h in range(HEADS):
        sl = slice(h * HEAD_DIM, (h + 1) * HEAD_DIM)
        outs.append(_rms_bwd(dn[:, sl], n[:, sl], r[:, sl]))
    return jnp.concatenate(outs, axis=1)


def _window_counts(first_row, tm):
    pos = (first_row + 1 + lax.broadcasted_iota(jnp.int32, (tm, 1), 0)).astype(F32)
    return [jnp.minimum(pos, float(w)) for w in POOL_WINDOWS]


def _post_fwd(o, main, pool_r, gates, x1, onorm, pool_w, pool_scale, sqw, wbw, tm, riders=()):
    t = o.shape[0]
    ext_rows = tm + POOL_HALO

    def body(o_ref, og_ref, u_ref, gt_ref, x1_ref, on_ref, pw_ref, ps_ref, wa_ref, wout_ref, wb_ref,
             x2_ref, ya_ref, yb_ref, pooled_ref, ext):
        i = pl.program_id(0)

        @pl.when(i == 0)
        def _():
            ext[0:POOL_HALO, :] = jnp.zeros((POOL_HALO, POOL_WIDTH), F32)

        _, n = _head_norm(o_ref[...])
        og = og_ref[...]
        oa = (n * on_ref[...] * (og * _sigmoid(og))).astype(BF16)
        ya = _dot(oa, wa_ref[...])

        u = u_ref[...]
        ext[POOL_HALO:ext_rows, :] = u
        e = ext[...]
        counts = _window_counts(i * tm, tm)
        pooled = []
        for gidx, w in enumerate(POOL_WINDOWS):
            s = e[:, gidx * HEAD_DIM:(gidx + 1) * HEAD_DIM]
            shift = 1
            while shift < w:
                s = s + pltpu.roll(s, shift, axis=0)
                shift *= 2
            pooled.append(s[POOL_HALO:, :] / counts[gidx] - u[:, gidx * HEAD_DIM:(gidx + 1) * HEAD_DIM])
        ext[0:POOL_HALO, :] = ext[tm:ext_rows, :]
        pooled_b = [pg.astype(BF16) for pg in pooled]
        pooled_ref[...] = jnp.concatenate(pooled_b, axis=1)
        mixed = jnp.concatenate([_dot(pooled_b[gidx], pw_ref[gidx].astype(BF16)) for gidx in range(len(POOL_WINDOWS))],
                                axis=1) * ps_ref[...]
        mixed_b = mixed.astype(BF16)
        yb = jnp.concatenate([_dot(mixed_b, wb_ref[j]) for j in range(N_CHIPS)], axis=1)

        gt = gt_ref[...]
        y = _sigmoid(gt[:, :D_MODEL]) * ya + _sigmoid(gt[:, D_MODEL:]) * yb
        x2_ref[...] = x1_ref[...] + _dot(y.astype(BF16), wout_ref[...])
        ya_ref[...] = ya.astype(BF16)
        yb_ref[...] = yb.astype(BF16)

    return _hosted(
        riders, body, name="post_fwd", grid=(t // tm,),
        in_specs=[_rows(tm, D_MODEL), _rows(tm, D_MODEL, 3), _rows(tm, POOL_WIDTH), _rows(tm, GATE_COLS), _rows(tm, D_MODEL),
                  _resident((1, D_MODEL)), _resident(pool_w.shape), _resident((1, POOL_WIDTH)),
                  _pick(sqw.shape, 0), _pick(sqw.shape, 1), _resident(wbw.shape)],
        out_specs=[_rows(tm, D_MODEL), _rows(tm, D_MODEL), _rows(tm, D_MODEL), _rows(tm, POOL_WIDTH)],
        out_shape=[jax.ShapeDtypeStruct((t, D_MODEL), F32), jax.ShapeDtypeStruct((t, D_MODEL), BF16),
                   jax.ShapeDtypeStruct((t, D_MODEL), BF16), jax.ShapeDtypeStruct((t, POOL_WIDTH), BF16)],
        scratch_shapes=[pltpu.VMEM((ext_rows, POOL_WIDTH), F32)],
        compiler_params=_params(("arbitrary",)),
    )(o, main, pool_r, gates, x1, onorm, pool_w, pool_scale, sqw, sqw, wbw)


def _post_bwd(dx2, o, main, gates, ya, yb, pooled, onorm, pool_w, pool_scale, sqw, wbw, tm, riders=()):
    t = o.shape[0]
    n_tiles = t // tm
    ext_rows = tm + POOL_HALO
    n_groups = len(POOL_WINDOWS)

    def rev(cols, col_block=0):
        return pl.BlockSpec((tm, cols), lambda i: (n_tiles - 1 - i, col_block))

    def body(dx2_ref, o_ref, og_ref, gt_ref, ya_ref, yb_ref, pooled_ref, on_ref, pw_ref, ps_ref, wa_ref, wout_ref, wb_ref,
             do_ref, dog_ref, du_ref, dgt_ref, dwa_ref, dwout_ref, dwb_ref, dpw_ref, dps_ref, don_ref, ext):
        i = pl.program_id(0)

        @pl.when(i == 0)
        def _():
            ext[tm:ext_rows, :] = jnp.zeros((POOL_HALO, POOL_WIDTH), F32)
            for ref in (dwa_ref, dwout_ref, dwb_ref, dpw_ref, dps_ref, don_ref):
                ref[...] = jnp.zeros_like(ref)

        groups = [slice(gidx * HEAD_DIM, (gidx + 1) * HEAD_DIM) for gidx in range(n_groups)]
        shards = [slice(j * 256, (j + 1) * 256) for j in range(N_CHIPS)]
        dx2b = dx2_ref[...].astype(BF16)
        dy = _dot_nt(dx2b, wout_ref[...])
        pooled_b = pooled_ref[...]
        pm = jnp.concatenate([_dot(pooled_b[:, g], pw_ref[gidx].astype(BF16)) for gidx, g in enumerate(groups)], axis=1)
        gt = gt_ref[...]
        sga = _sigmoid(gt[:, :D_MODEL])
        sgb = _sigmoid(gt[:, D_MODEL:])
        ya = ya_ref[...].astype(F32)
        yb = yb_ref[...].astype(F32)
        y = (sga * ya + sgb * yb).astype(BF16)
        dya = (dy * sga).astype(BF16)
        dyb = (dy * sgb).astype(BF16)
        dgt_ref[:, :D_MODEL] = (dy * ya * sga * (1.0 - sga)).astype(BF16)
        dgt_ref[:, D_MODEL:] = (dy * yb * sgb * (1.0 - sgb)).astype(BF16)
        dwout_ref[...] += _dot_tn(y, dx2b)
        doa = _dot_nt(dya, wa_ref[...])
        dmixed = _dot_nt(dyb[:, shards[0]], wb_ref[0])
        for j in range(1, N_CHIPS):
            dmixed += _dot_nt(dyb[:, shards[j]], wb_ref[j])
        r, n = _head_norm(o_ref[...])
        onv = on_ref[...]
        og = og_ref[...]
        sog = _sigmoid(og)
        silu_og = og * sog
        normed = n * onv
        oa = (normed * silu_og).astype(BF16)
        dog_ref[...] = (doa * normed * (sog * (1.0 + og * (1.0 - sog)))).astype(BF16)
        dnormed = doa * silu_og
        don_ref[...] += _colsum(dnormed * n)
        do_ref[...] = _head_norm_bwd(dnormed * onv, n, r)
        psv = ps_ref[...]
        mixed_b = (pm * psv).astype(BF16)
        dps_ref[...] += _colsum(dmixed * pm)
        dpm = (dmixed * psv).astype(BF16)
        dwa_ref[...] += _dot_tn(oa, dya)
        for j in range(N_CHIPS):
            dwb_ref[j] += _dot_tn(mixed_b, dyb[:, shards[j]])
        counts = _window_counts((n_tiles - 1 - i) * tm, tm)
        dpooled = []
        for gidx, g in enumerate(groups):
            dpw_ref[gidx] += _dot_tn(pooled_b[:, g], dpm[:, g])
            dpooled.append(_dot_nt(dpm[:, g], pw_ref[gidx].astype(BF16)))
        ext[0:tm, :] = jnp.concatenate([dpooled[gidx] / counts[gidx] for gidx in range(n_groups)], axis=1)
        e = ext[...]
        du = []
        for gidx, w in enumerate(POOL_WINDOWS):
            s = e[:, gidx * HEAD_DIM:(gidx + 1) * HEAD_DIM]
            shift = 1
            while shift < w:
                s = s + pltpu.roll(s, ext_rows - shift, axis=0)
                shift *= 2
            du.append(s[:tm, :] - dpooled[gidx])
        ext[tm:ext_rows, :] = ext[0:POOL_HALO, :]
        du_ref[...] = jnp.concatenate(du, axis=1).astype(BF16)

    wa_shape = (D_MODEL, D_MODEL)
    return _hosted(
        riders, body, name="post_bwd", grid=(n_tiles,),
        in_specs=[rev(D_MODEL), rev(D_MODEL), rev(D_MODEL, 3), rev(GATE_COLS), rev(D_MODEL), rev(D_MODEL), rev(POOL_WIDTH),
                  _resident((1, D_MODEL)), _resident(pool_w.shape), _resident((1, POOL_WIDTH)),
                  _pick(sqw.shape, 0), _pick(sqw.shape, 1), _resident(wbw.shape)],
        out_specs=[rev(D_MODEL), rev(D_MODEL), rev(POOL_WIDTH), rev(GATE_COLS),
                   _acc(wa_shape), _acc(wa_shape), _acc(wbw.shape), _acc(pool_w.shape), _acc((1, POOL_WIDTH)),
                   _acc((1, D_MODEL))],
        out_shape=[jax.ShapeDtypeStruct((t, D_MODEL), F32), jax.ShapeDtypeStruct((t, D_MODEL), BF16),
                   jax.ShapeDtypeStruct((t, POOL_WIDTH), BF16), jax.ShapeDtypeStruct((t, GATE_COLS), BF16),
                   jax.ShapeDtypeStruct(wa_shape, F32), jax.ShapeDtypeStruct(wa_shape, F32),
                   jax.ShapeDtypeStruct(wbw.shape, F32), jax.ShapeDtypeStruct(pool_w.shape, F32),
                   jax.ShapeDtypeStruct((1, POOL_WIDTH), F32), jax.ShapeDtypeStruct((1, D_MODEL), F32)],
        scratch_shapes=[pltpu.VMEM((ext_rows, POOL_WIDTH), F32)],
        compiler_params=_params(("arbitrary",)),
    )(dx2, o, main, gates, ya, yb, pooled, onorm, pool_w, pool_scale, sqw, sqw, wbw)


def _tail(x3, p, target, g_ple, g_post, g_final, sqw, wpw, tm, riders=()):
    t = x3.shape[0]
    pd = p.shape[1]

    def body(x_ref, p_ref, tg_ref, g4_ref, g5_ref, g6_ref, wg_ref, wp_ref,
             dx_ref, loss_ref, dwg_ref, dwp_ref, dg4_ref, dg5_ref, dg6_ref):
        @pl.when(pl.program_id(0) == 0)
        def _():
            for ref in (loss_ref, dwg_ref, dwp_ref, dg4_ref, dg5_ref, dg6_ref):
                ref[...] = jnp.zeros_like(ref)

        x3v = x_ref[...]
        g4, g5, g6 = g4_ref[...], g5_ref[...], g6_ref[...]
        r4, n4 = _rms(x3v)
        h4 = (n4 * g4).astype(BF16)
        gate = _sigmoid(_dot(h4, wg_ref[...]))
        pb = p_ref[...].astype(BF16)
        r5, n5 = _rms(jnp.concatenate([_dot(pb, wp_ref[j]) for j in range(N_CHIPS)], axis=1))
        emb = n5 * g5
        r6, n6 = _rms(x3v + gate * emb)
        diff = n6 * g6 - tg_ref[...]
        loss_ref[...] += 0.5 * jnp.sum(jnp.mean(diff * diff, axis=-1, keepdims=True), axis=0, keepdims=True)
        dout = diff * (1.0 / D_MODEL)
        dg6_ref[...] += _colsum(dout * n6)
        dx4 = _rms_bwd(dout * g6, n6, r6)
        demb = dx4 * gate
        dg5_ref[...] += _colsum(demb * n5)
        dpre = _rms_bwd(demb * g5, n5, r5).astype(BF16)
        for j in range(N_CHIPS):
            dwp_ref[j] += _dot_tn(pb, dpre[:, j * pd:(j + 1) * pd])
        dz = (dx4 * emb * gate * (1.0 - gate)).astype(BF16)
        dwg_ref[...] += _dot_tn(h4, dz)
        dh4 = _dot_nt(dz, wg_ref[...])
        dg4_ref[...] += _colsum(dh4 * n4)
        dx_ref[...] = dx4 + _rms_bwd(dh4 * g4, n4, r4)

    sq_shape = (D_MODEL, D_MODEL)
    vec = (1, D_MODEL)
    return _hosted(
        riders, body, name="tail", grid=(t // tm,),
        in_specs=[_rows(tm, D_MODEL), _rows(tm, pd), _rows(tm, D_MODEL), _resident(vec), _resident(vec), _resident(vec),
                  _pick(sqw.shape, 2), _resident(wpw.shape)],
        out_specs=[_rows(tm, D_MODEL), _acc((1, 1)), _acc(sq_shape), _acc(wpw.shape), _acc(vec), _acc(vec), _acc(vec)],
        out_shape=[jax.ShapeDtypeStruct((t, D_MODEL), F32), jax.ShapeDtypeStruct((1, 1), F32),
                   jax.ShapeDtypeStruct(sq_shape, F32), jax.ShapeDtypeStruct(wpw.shape, F32),
                   jax.ShapeDtypeStruct(vec, F32), jax.ShapeDtypeStruct(vec, F32), jax.ShapeDtypeStruct(vec, F32)],
        compiler_params=_params(("arbitrary",)),
    )(x3, p, target, g_ple, g_post, g_final, sqw, wpw)


def _position():
    return lax.axis_index("x"), lax.axis_index("y"), lax.axis_index("c")


def _other_chips(x, y):
    return [(1 - x, y), (x, 1 - y), (1 - x, 1 - y)]


def _remote(src, dst, send_sems, recv_sems, k, device):
    return pltpu.make_async_remote_copy(src_ref=src, dst_ref=dst, send_sem=send_sems.at[k], recv_sem=recv_sems.at[k],
                                        device_id=device, device_id_type=MESH)


def _gather_rider(shards, forward_at):
    n = len(shards)

    def copies(ins, outs, sems):
        send_sems, recv_sems = sems
        x, y, c = _position()
        mine = 2 * x + y
        first, passed, arriving = [], [], []
        for k, (cx, cy) in enumerate(_other_chips(x, y)):
            theirs = 2 * cx + cy
            for a in range(n):
                first.append(_remote(ins[a].at[:, c], outs[a].at[:, mine, c], send_sems, recv_sems, k * n + a, (cx, cy, c)))
                block = outs[a].at[:, theirs, c]
                passed.append(_remote(block, block, send_sems, recv_sems, (3 + k) * n + a, (x, y, 1 - c)))
                other = outs[a].at[:, theirs, 1 - c]
                arriving.append(_remote(other, other, send_sems, recv_sems, (3 + k) * n + a, (x, y, 1 - c)))
        return first, passed, arriving

    return _Rider(shards, [jax.ShapeDtypeStruct((s.shape[0], N_CHIPS) + s.shape[1:], s.dtype) for s in shards],
                  [pltpu.SemaphoreType.DMA((6 * n,)), pltpu.SemaphoreType.DMA((6 * n,))], _gather_phases(copies, forward_at))


def _gather_phases(copies, forward_at):
    def begin(ins, outs, sems):
        for cp in copies(ins, outs, sems)[0]:
            cp.start()

    def forward(ins, outs, sems):
        first, passed, _ = copies(ins, outs, sems)
        for got, cp in zip(first, passed):
            got.wait_recv()
            cp.start()

    def finish(ins, outs, sems):
        first, passed, arriving = copies(ins, outs, sems)
        for cp in arriving:
            cp.wait_recv()
        for cp in first + passed:
            cp.wait_send()

    return [(0, begin), (forward_at, forward), (1, finish)]


def _relay_gather_rider(shards):
    n = len(shards)
    kinds = 8

    def copies(ins, outs, sems):
        send_sems, recv_sems = sems
        x, y, c = _position()
        mine, nx, ny, diag = 2 * x + y, 2 * (1 - x) + y, 2 * x + 1 - y, 2 * (1 - x) + 1 - y
        to_x, to_y, sibling = (1 - x, y, c), (x, 1 - y, c), (x, y, 1 - c)
        sent, relayed, passed, arriving = [], [], [], []
        for a in range(n):
            hq = ins[a].shape[2] // 2
            quarter = lambda chip, half, q: outs[a].at[:, chip, half, pl.ds(q * hq, hq)]
            rc = functools.partial(_remote, send_sems=send_sems, recv_sems=recv_sems)
            sent += [rc(ins[a].at[:, c], outs[a].at[:, mine, c], k=kinds * a, device=to_x),
                     rc(ins[a].at[:, c], outs[a].at[:, mine, c], k=kinds * a + 1, device=to_y)]
            got = [rc(outs[a].at[:, nx, c], outs[a].at[:, nx, c], k=kinds * a, device=to_x),
                   rc(outs[a].at[:, ny, c], outs[a].at[:, ny, c], k=kinds * a + 1, device=to_y),
                   rc(quarter(diag, c, 0), quarter(diag, c, 0), k=kinds * a + 2, device=to_y),
                   rc(quarter(diag, c, 1), quarter(diag, c, 1), k=kinds * a + 3, device=to_x)]
            relayed += [rc(quarter(nx, c, 0), quarter(nx, c, 0), k=kinds * a + 2, device=to_y),
                        rc(quarter(ny, c, 1), quarter(ny, c, 1), k=kinds * a + 3, device=to_x)]
            blocks = [lambda h: outs[a].at[:, nx, h], lambda h: outs[a].at[:, ny, h],
                      lambda h: quarter(diag, h, 0), lambda h: quarter(diag, h, 1)]
            passed += [(got[i], rc(b(c), b(c), k=kinds * a + 4 + i, device=sibling)) for i, b in enumerate(blocks)]
            arriving += [rc(b(1 - c), b(1 - c), k=kinds * a + 4 + i, device=sibling) for i, b in enumerate(blocks)]
        return sent, relayed, passed, arriving

    def begin(ins, outs, sems):
        for cp in copies(ins, outs, sems)[0]:
            cp.start()

    def forward(ins, outs, sems):
        _, relayed, passed, _ = copies(ins, outs, sems)
        for i, (got, onward) in enumerate(passed):
            got.wait_recv()
            if i % 4 < 2:
                relayed[2 * (i // 4) + i % 4].start()
            onward.start()

    def finish(ins, outs, sems):
        sent, relayed, passed, arriving = copies(ins, outs, sems)
        for cp in arriving:
            cp.wait_recv()
        for cp in sent + relayed + [onward for _, onward in passed]:
            cp.wait_send()

    return _Rider(shards, [jax.ShapeDtypeStruct((s.shape[0], N_CHIPS) + s.shape[1:], s.dtype) for s in shards],
                  [pltpu.SemaphoreType.DMA((kinds * n,)), pltpu.SemaphoreType.DMA((kinds * n,))],
                  [(0, begin), (0.5, forward), (1, finish)])


def _with_own(gathered, shard, slot):
    return lax.dynamic_update_slice(gathered, shard[:, None], (0, slot, 0, 0, 0))


def _exchange_rider(arrays, out_shape, n_copies, transfers, n_local=0):
    def copies(ins, outs, sems):
        send_sems, recv_sems, local_sems = sems
        remote, local = transfers(ins, outs)
        return ([_remote(src, dst, send_sems, recv_sems, i, dev) for i, (src, dst, dev) in enumerate(remote)],
                [pltpu.make_async_copy(src, dst, local_sems.at[i]) for i, (src, dst) in enumerate(local)])

    def begin(ins, outs, sems):
        remote, local = copies(ins, outs, sems)
        for cp in remote + local:
            cp.start()

    def finish(ins, outs, sems):
        remote, local = copies(ins, outs, sems)
        for cp in remote:
            cp.wait_recv()
        for cp in remote:
            cp.wait_send()
        for cp in local:
            cp.wait()

    return _Rider(arrays, out_shape,
                  [pltpu.SemaphoreType.DMA((n_copies,)), pltpu.SemaphoreType.DMA((n_copies,)),
                   pltpu.SemaphoreType.DMA((max(n_local, 1),))],
                  [(0, begin), (1, finish)])


def _pair_rider(partials):
    def transfers(ins, outs):
        x, y, c = _position()
        return [(ins[a].at[:, :, 1 - c], outs[a], (x, y, 1 - c)) for a in range(len(partials))], []

    shapes = [jax.ShapeDtypeStruct(g.shape[:2] + g.shape[3:], g.dtype) for g in partials]
    return _exchange_rider(partials, shapes, len(partials), transfers)


def _chips_rider(sums):
    n = len(sums)

    def transfers(ins, outs):
        x, y, c = _position()
        return [(ins[a].at[:, 2 * cx + cy], outs[a].at[:, k], (cx, cy, c))
                for k, (cx, cy) in enumerate(_other_chips(x, y)) for a in range(n)], []

    shapes = [jax.ShapeDtypeStruct((q.shape[0], 3) + q.shape[2:], q.dtype) for q in sums]
    return _exchange_rider(sums, shapes, 3 * n, transfers)


def _share_rider(halves):
    def transfers(ins, outs):
        x, y, c = _position()
        return [(ins[a], outs[a], (x, y, 1 - c)) for a in range(len(halves))], []

    return _exchange_rider(halves, [jax.ShapeDtypeStruct(h.shape, h.dtype) for h in halves], len(halves), transfers)


def _small_rider(pack):
    flips = [(fx, fy, fc) for fx in (0, 1) for fy in (0, 1) for fc in (0, 1)][1:]

    def transfers(ins, outs):
        x, y, c = _position()
        slot = outs[0].at[4 * x + 2 * y + c]
        flip = lambda v, f: v + f - 2 * v * f
        return [(ins[0], slot, (flip(x, fx), flip(y, fy), flip(c, fc))) for fx, fy, fc in flips], [(ins[0], slot)]

    return _exchange_rider([pack], [jax.ShapeDtypeStruct((8,) + pack.shape, pack.dtype)], len(flips), transfers, n_local=1)


IN_HBM = pl.BlockSpec(memory_space=pltpu.HBM)
IN_SEM = pl.BlockSpec(memory_space=pltpu.SEMAPHORE)
SPLIT_COPY = pltpu.CompilerParams(has_side_effects=pltpu.SideEffectType.DATAFLOW_SIDE_EFFECTING)


def _start_copies(sources, landing_shapes, n_copies, plan, name):
    n_src, n_land = len(sources), len(landing_shapes)

    def body(*refs):
        srcs, lands, outs = refs[:n_src], refs[n_src:n_src + n_land], refs[n_src + n_land:]
        send, recv, token = outs[:n_copies], outs[n_copies:2 * n_copies], outs[-1]
        for i, (src, dst, device) in enumerate(plan(srcs, lands)):
            pltpu.make_async_remote_copy(src_ref=src, dst_ref=dst, send_sem=send[i], recv_sem=recv[i], device_id=device,
                                         device_id_type=MESH).start()
        token[...] = jnp.zeros_like(token)

    arrays = [pltpu.with_memory_space_constraint(a, pltpu.HBM) for a in sources]
    arrays += [pltpu.with_memory_space_constraint(lax.empty(s.shape, s.dtype), pltpu.HBM) for s in landing_shapes]
    res = pl.pallas_call(
        body, name=name,
        out_shape=[pltpu.SemaphoreType.DMA(())] * (2 * n_copies) + [pltpu.HBM(a.shape, a.dtype) for a in arrays]
        + [jax.ShapeDtypeStruct((8, 128), F32)],
        in_specs=[IN_HBM] * len(arrays),
        out_specs=[IN_SEM] * (2 * n_copies) + [IN_HBM] * len(arrays) + [pl.BlockSpec(memory_space=pltpu.VMEM)],
        input_output_aliases={i: 2 * n_copies + i for i in range(len(arrays))},
        compiler_params=SPLIT_COPY,
    )(*arrays)
    sems, rest = res[:2 * n_copies], res[2 * n_copies:]
    return sems, rest[:n_src], rest[n_src:n_src + n_land], rest[-1]


def _wait_copies(started, n_copies, plan, after, name):
    sems, sources, landings, _ = started
    n_src, n_land = len(sources), len(landings)

    def body(*refs):
        srcs, lands = refs[:n_src], refs[n_src:n_src + n_land]
        sem_refs = refs[n_src + n_land:n_src + n_land + 2 * n_copies]
        send, recv = sem_refs[:n_copies], sem_refs[n_copies:]
        for i, (src, dst, device) in enumerate(plan(srcs, lands)):
            cp = pltpu.make_async_remote_copy(src_ref=src, dst_ref=dst, send_sem=send[i], recv_sem=recv[i], device_id=device,
                                              device_id_type=MESH)
            cp.wait_send()
            cp.wait_recv()

    arrays = list(sources) + list(landings)
    res = pl.pallas_call(
        body, name=name, out_shape=[pltpu.HBM(a.shape, a.dtype) for a in arrays],
        in_specs=[IN_HBM] * len(arrays) + [IN_SEM] * (2 * n_copies) + [ANY], out_specs=[IN_HBM] * len(arrays),
        input_output_aliases={i: i for i in range(len(arrays))},
        compiler_params=SPLIT_COPY,
    )(*arrays, *sems, after)
    return res[n_src:]


def _chips_plan(n):
    def plan(srcs, lands):
        x, y, c = _position()
        return [(srcs[a].at[:, 2 * cx + cy], lands[a].at[:, k], (cx, cy, c))
                for k, (cx, cy) in enumerate(_other_chips(x, y)) for a in range(n)]
    return plan


def _alone(rider, name, after=()):
    return _hosted([rider], lambda *refs: None, name=name, in_specs=[ANY] * len(after), out_specs=[], out_shape=[])(*after)[1][0]


def _add_pair(mine, theirs, c, tag):
    n = len(mine)

    def body(c_ref, *refs):
        for a in range(n):
            refs[2 * n + a][...] = (refs[2 * a][...].astype(F32) + refs[2 * a + 1][...].astype(F32)).astype(BF16)

    in_specs, out_specs = [], []
    for got in theirs:
        l, _, hr, cols = got.shape
        in_specs += [pl.BlockSpec((l, None, None, hr, cols), lambda j, s: (0, j, s[0], 0, 0)),
                     pl.BlockSpec((l, None, hr, cols), lambda j, s: (0, j, 0, 0))]
        out_specs.append(pl.BlockSpec((l, None, hr, cols), lambda j, s: (0, j, 0, 0)))
    return pl.pallas_call(
        body, name=f"add_pair_{tag}",
        grid_spec=pltpu.PrefetchScalarGridSpec(num_scalar_prefetch=1, grid=(N_CHIPS,), in_specs=in_specs, out_specs=out_specs),
        out_shape=[jax.ShapeDtypeStruct(got.shape, BF16) for got in theirs],
        compiler_params=_params(("parallel",)),
    )(c.reshape(1), *[a for pair in zip(mine, theirs) for a in pair])


def _add_chips(parts, received, mine, tag):
    n = len(parts)

    def body(j_ref, *refs):
        for a in range(n):
            acc = refs[2 * a][...].astype(F32)
            for k in range(3):
                acc += refs[2 * a + 1][:, k].astype(F32)
            refs[2 * n + a][...] = acc

    in_specs, out_specs, out_shape = [], [], []
    for got in received:
        l, _, hr, cols = got.shape
        in_specs += [pl.BlockSpec((l, None, hr // 2, cols), lambda i, s: (0, s[0], i, 0)),
                     pl.BlockSpec((l, 3, hr // 2, cols), lambda i, s: (0, 0, i, 0))]
        out_specs.append(pl.BlockSpec((l, hr // 2, cols), lambda i, s: (0, i, 0)))
        out_shape.append(jax.ShapeDtypeStruct((l, hr, cols), F32))
    return pl.pallas_call(
        body, name=f"add_chips_{tag}",
        grid_spec=pltpu.PrefetchScalarGridSpec(num_scalar_prefetch=1, grid=(2,), in_specs=in_specs, out_specs=out_specs),
        out_shape=out_shape,
        compiler_params=_params(("parallel",)),
    )(mine.reshape(1), *[a for pair in zip(parts, received) for a in pair])


def _adam_update(w, g, m, v):
    m2 = ADAM_B1 * m + (1.0 - ADAM_B1) * g
    v2 = ADAM_B2 * v + (1.0 - ADAM_B2) * jnp.square(g)
    m_hat = m2 / (1.0 - ADAM_B1 ** ADAM_STEP)
    v_hat = v2 / (1.0 - ADAM_B2 ** ADAM_STEP)
    return -ADAM_LR * (m_hat / (jnp.sqrt(v_hat) + ADAM_EPS) + ADAM_WD * w), m2, v2


def _adamw_group(items, tag, after=()):
    n = len(items)

    def body(*refs):
        ins, outs = refs[:5 * n], refs[5 * n + len(after):]
        mine = pl.program_id(0) == lax.axis_index("c")
        for a in range(n):
            w_ref, own_ref, other_ref, m_ref, v_ref = ins[5 * a:5 * a + 5]
            g_ref, d_ref, nm_ref, nv_ref = outs[4 * a:4 * a + 4]
            gv = jnp.where(mine, own_ref[...], other_ref[...])
            g_ref[...] = gv
            d_ref[...], nm_ref[...], nv_ref[...] = _adam_update(w_ref[...], gv, m_ref[...], v_ref[...])

    in_specs, out_specs, out_shape, args = [], [], [], []
    for w, own, other, m, v in items:
        _, hr, cols = w.shape
        tr = hr // ADAM_BLOCKS
        full = pl.BlockSpec((None, tr, cols), lambda h, i: (h, i, 0))
        half = pl.BlockSpec((tr, cols), lambda h, i: (i, 0))
        in_specs += [full, half, half, full, full]
        out_specs += [full] * 4
        out_shape += [jax.ShapeDtypeStruct((2, hr, cols), F32)] * 4
        args += [w, own, other, m, v]
    outs = pl.pallas_call(body, name=f"adamw_{tag}", grid=(2, ADAM_BLOCKS), in_specs=in_specs + [ANY] * len(after),
                          out_specs=out_specs, out_shape=out_shape,
                          compiler_params=_params(("parallel", "parallel")))(*args, *after)
    return [outs[4 * a:4 * a + 4] for a in range(n)]


def _adamw_small(w, gathered, m, v, shapes):
    n_rows = w.shape[0]
    places = []
    for i, name in enumerate(VECTOR_PARAMS):
        places.append((name, i * TILE_ROWS, 1 if len(shapes[name]) == 1 else shapes[name][0], shapes[name][-1]))
    places.append(("pool_w", len(VECTOR_PARAMS) * TILE_ROWS, n_rows - len(VECTOR_PARAMS) * TILE_ROWS, D_MODEL))

    def body(w_ref, g_ref, m_ref, v_ref, loss_ref, *rest):
        outs, (sum_scr, d_scr, nm_scr, nv_scr) = rest[:-4], rest[-4:]
        total = g_ref[0]
        for i in range(1, g_ref.shape[0]):
            total += g_ref[i]
        sum_scr[...] = total
        gv = sum_scr[0:n_rows, :]
        d_scr[...], nm_scr[...], nv_scr[...] = _adam_update(w_ref[...], gv, m_ref[...], v_ref[...])
        loss_ref[...] = sum_scr[n_rows:n_rows + 1, 0:1]
        for k, (_, first, rows, cols) in enumerate(places):
            for j, scr in enumerate((sum_scr, d_scr, nm_scr, nv_scr)):
                outs[4 * k + j][...] = scr[first:first + rows, 0:cols]

    out_shape = [jax.ShapeDtypeStruct((1, 1), F32)]
    for _, _, rows, cols in places:
        out_shape += [jax.ShapeDtypeStruct((rows, cols), F32)] * 4
    res = pl.pallas_call(
        body, name="adamw_small", out_shape=out_shape,
        scratch_shapes=[pltpu.VMEM(gathered.shape[1:], F32)] + [pltpu.VMEM(w.shape, F32)] * 3,
        compiler_params=_params())(w, gathered, m, v)
    return res[0], {name: res[1 + 4 * k:5 + 4 * k] for k, (name, _, _, _) in enumerate(places)}


VECTOR_PARAMS = ("ffn1_norm", "mix_norm", "hgrn_lb", "hgrn_onorm", "ffn2_norm", "ple_norm", "ple_post_norm", "final_norm",
                 "pool_scale")
ALL_PARAMS = ("ffn1_norm", "ffn1_w1", "ffn1_w3", "ffn1_w2", "mix_norm", "w_in", "hgrn_lb", "hgrn_onorm", "w_branch_a",
              "pool_w", "pool_scale", "w_branch_b", "w_out", "ffn2_norm", "ffn2_w1", "ffn2_w3", "ffn2_w2", "ple_norm",
              "ple_w_gate", "ple_w_proj", "ple_post_norm", "final_norm")
TILE_ROWS = 8


def _pack_small(values, loss=None):
    tile = lambda a: jnp.pad(a, ((0, TILE_ROWS - a.shape[0]), (0, D_MODEL - a.shape[1])))
    parts = [tile(values[name].reshape(-1, values[name].shape[-1])) for name in VECTOR_PARAMS]
    parts.append(values["pool_w"].reshape(-1, D_MODEL))
    if loss is not None:
        parts.append(tile(loss))
    return jnp.concatenate(parts, axis=0)


def _halved(a, lead):
    return a.reshape(lead, 2, -1, a.shape[-1])


def _shard_halves(a, lead):
    return a.reshape(lead, N_CHIPS, 2, -1, a.shape[-1])


REDUCED_TRANSPOSED = ("ffn1_w1", "ffn1_w3", "ffn2_w1", "ffn2_w3")


def _entries(arrays):
    return [a[i] for a in arrays for i in range(a.shape[0])]


def _adam_items(names, own, other, w, m, v):
    items = []
    for name, g_own, g_other in zip(names, _entries(own), _entries(other)):
        view = (lambda a: _halved(a[0].T, 1)[0]) if name in REDUCED_TRANSPOSED else (lambda a: _halved(a, 1)[0])
        items.append((view(w[name]), g_own, g_other, view(m[name]), view(v[name])))
    return items


def _adam_store(names, results, w, out):
    for name, res in zip(names, results):
        shape = w[name].shape
        if name in REDUCED_TRANSPOSED:
            back = [a.reshape(shape[2], shape[1]).T.reshape(shape) for a in res]
        else:
            back = [a.reshape(shape) for a in res]
        out["grad"][name], out["delta"][name], out["new_m"][name], out["new_v"][name] = back


def kernel(x, p, ffn1_norm, ffn1_w1, ffn1_w3, ffn1_w2, mix_norm, w_in, hgrn_lb, hgrn_onorm, w_branch_a, pool_w, pool_scale, w_branch_b, w_out, ffn2_norm, ffn2_w1, ffn2_w3, ffn2_w2, ple_norm, ple_w_gate, ple_w_proj, ple_post_norm, final_norm, loss_target, m_ffn1_norm, m_ffn1_w1, m_ffn1_w3, m_ffn1_w2, m_mix_norm, m_w_in, m_hgrn_lb, m_hgrn_onorm, m_w_branch_a, m_pool_w, m_pool_scale, m_w_branch_b, m_w_out, m_ffn2_norm, m_ffn2_w1, m_ffn2_w3, m_ffn2_w2, m_ple_norm, m_ple_w_gate, m_ple_w_proj, m_ple_post_norm, m_final_norm, v_ffn1_norm, v_ffn1_w1, v_ffn1_w3, v_ffn1_w2, v_mix_norm, v_w_in, v_hgrn_lb, v_hgrn_onorm, v_w_branch_a, v_pool_w, v_pool_scale, v_w_branch_b, v_w_out, v_ffn2_norm, v_ffn2_w1, v_ffn2_w3, v_ffn2_w2, v_ple_norm, v_ple_w_gate, v_ple_w_proj, v_ple_post_norm, v_final_norm):
    args = dict(locals())
    w = {name: args[name] for name in ALL_PARAMS}
    m = {name: args["m_" + name] for name in ALL_PARAMS}
    v = {name: args["v_" + name] for name in ALL_PARAMS}
    cx, cy, cc = _position()
    chip = (2 * cx + cy).astype(jnp.int32)
    core = cc.astype(jnp.int32)
    xs, ps, target = x[0], p[0, 0], loss_target[0]
    t = xs.shape[0]
    tm = min(256, t)
    tm_ffn = min(512, t)
    tt = min(512, t)
    tk = min(2048, t)
    small = {name: w[name] for name in VECTOR_PARAMS}
    small["final_norm"] = w["final_norm"].reshape(1, D_MODEL)
    pool_w0 = w["pool_w"][0]

    ffn_shard = lambda i: _halved(jnp.stack([w[f"ffn{i}_w1"][0].T, w[f"ffn{i}_w3"][0].T, w[f"ffn{i}_w2"][0]]).astype(BF16), 3)
    sq_shard = _halved(jnp.stack([w["w_branch_a"][0], w["w_out"][0], w["ple_w_gate"][0]]).astype(BF16), 3)
    win_shard, wb_shard, wp_shard = (_halved(w[n].astype(BF16), 1) for n in ("w_in", "w_branch_b", "ple_w_proj"))

    ffn1_shard, ffn2_shard = ffn_shard(1), ffn_shard(2)
    (ffn1w,) = _alone(_relay_gather_rider([ffn1_shard]), "gather_ffn1")
    ffn1w = _with_own(ffn1w, ffn1_shard, chip).reshape(3, D_FF, D_MODEL)
    (x1, a1, b1), ((winw,),) = _ffn_fwd(xs, small["ffn1_norm"], ffn1w, 1, tm_ffn, [_gather_rider([win_shard], 0.6)])
    winw = _with_own(winw, win_shard, chip).reshape(N_CHIPS, D_MODEL, SHARD_IN_COLS)
    (main, pool_r, gates), ((ffn2w,),) = _mix_fwd(x1, small["mix_norm"], winw, tm, [_gather_rider([ffn2_shard], 0.75)])
    ffn2w = _with_own(ffn2w, ffn2_shard, chip).reshape(3, D_FF, D_MODEL)
    (o, states), ((sqw, wbw, wpw),) = _hgrn_fwd(main, small["hgrn_lb"], tt,
                                                 [_gather_rider([sq_shard, wb_shard, wp_shard], 0.5)])
    sqw = _with_own(sqw, sq_shard, chip).reshape(3, D_MODEL, D_MODEL)
    wbw = _with_own(wbw, wb_shard, chip).reshape(N_CHIPS, POOL_WIDTH, -1)
    wpw = _with_own(wpw, wp_shard, chip).reshape(N_CHIPS, ps.shape[1], -1)
    (x2, ya, yb, pooled), _ = _post_fwd(o, main, pool_r, gates, x1, small["hgrn_onorm"], pool_w0, small["pool_scale"], sqw,
                                       wbw, tm)
    (x3, a2, b2), _ = _ffn_fwd(x2, small["ffn2_norm"], ffn2w, 2, tm_ffn)
    (dx3, loss, d_wg, d_wp, d_ple, d_post, d_final), _ = _tail(
        x3, ps, target, small["ple_norm"], small["ple_post_norm"], small["final_norm"], sqw, wpw, tm_ffn)

    add_pairs = lambda parts, got, group: _add_pair(parts, got, core, group)
    add_chips = lambda sums, got, group: _add_chips(sums, got, chip, group)
    names1 = ("ffn2_w1", "ffn2_w3", "ffn2_w2", "ple_w_gate", "ple_w_proj")
    names2 = ("w_branch_a", "w_out", "w_branch_b")
    names3 = ("w_in",)
    names4 = ("ffn1_w1", "ffn1_w3")
    names5 = ("ffn1_w2",)
    tags1, tags2, tags3, tags4, tags5 = "ffn2", "branches", "w_in", "ffn1_in", "ffn1_out"

    (dx2, dab2, s2, h3, dxh2, d_ffn2_norm), _ = _ffn_bwd(dx3, x2, small["ffn2_norm"], a2, b2, ffn2w, 2, tm)
    (d_w13_2,), _ = _wgrad(dab2, h3, WGRAD_IN_BLOCKS, "wgrad_ffn2_in", tk)
    (d_w2_2,), _ = _wgrad(s2, dxh2, WGRAD_OUT_BLOCKS, "wgrad_ffn2_out", tk)
    part1 = [_shard_halves(d_w13_2, 2), _shard_halves(d_w2_2, 1), _shard_halves(d_wg, 1), _shard_halves(d_wp, 1)]
    (do, dog, du, dgates, d_wa, d_wout, d_wb, d_pool_w, d_pool_scale, d_onorm), (sib1,) = _post_bwd(
        dx2, o, main, gates, ya, yb, pooled, small["hgrn_onorm"], pool_w0, small["pool_scale"], sqw, wbw, tm,
        [_pair_rider(part1)])
    sums1 = add_pairs(part1, sib1, tags1)
    part2 = [_shard_halves(d_wa, 1), _shard_halves(d_wout, 1), _shard_halves(d_wb, 1)]
    (dqfi, d_lb), (got1, sib2) = _hgrn_bwd(main, small["hgrn_lb"], states, do, tt, [_chips_rider(sums1), _pair_rider(part2)])
    own1 = add_chips(sums1, got1, tags1)
    sums2 = add_pairs(part2, sib2, tags2)
    (dx1, dproj, h2, d_mix_norm), (other1, got2) = _mix_bwd(dqfi, dog, du, dgates, dx2, x1, small["mix_norm"], winw, tm,
                                                            [_share_rider(own1), _chips_rider(sums2)])
    own2 = add_chips(sums2, got2, tags2)
    (d_win,), (other2,) = _wgrad_cols(h2, dproj, N_CHIPS, "wgrad_in", tk, [_share_rider(own2)])
    part3 = [_shard_halves(d_win, 1)]
    (dx, dab1, s1, h1, dxh1, d_ffn1_norm), _ = _ffn_bwd(dx1, xs, small["ffn1_norm"], a1, b1, ffn1w, 1, tm)
    vecs = dict(ffn1_norm=d_ffn1_norm, mix_norm=d_mix_norm, hgrn_lb=d_lb, hgrn_onorm=d_onorm, ffn2_norm=d_ffn2_norm,
                ple_norm=d_ple, ple_post_norm=d_post, final_norm=d_final, pool_scale=d_pool_scale, pool_w=d_pool_w)
    (d_w2_1,), (sib3, (small_all,)) = _wgrad(s1, dxh1, WGRAD_OUT_BLOCKS, "wgrad_ffn1_out", tk,
                                             [_pair_rider(part3), _small_rider(_pack_small(vecs, loss))])
    sums3 = add_pairs(part3, sib3, tags3)
    part5 = [_shard_halves(d_w2_1, 1)]
    (d_w13_1,), (got3, sib5) = _wgrad(dab1, h1, WGRAD_IN_BLOCKS, "wgrad_ffn1_in", tk,
                                      [_chips_rider(sums3), _pair_rider(part5)])
    own3 = add_chips(sums3, got3, tags3)
    sums5 = add_pairs(part5, sib5, tags5)
    part4 = [_shard_halves(d_w13_1, 2)]
    landing = lambda a: jax.ShapeDtypeStruct((a.shape[0], 3) + a.shape[2:], a.dtype)

    def sibling_plan(srcs, lands):
        x, y, c = _position()
        return [(srcs[-1], lands[-1], (x, y, 1 - c))]

    plan_a = lambda srcs, lands: _chips_plan(1)(srcs[:1], lands[:1]) + sibling_plan(srcs, lands)
    started_a = _start_copies([sums5[0], own3[0]], [landing(sums5[0]), own3[0]], 4, plan_a, "start_a")
    sib4 = _alone(_pair_rider(part4), "pair_last", [started_a[3]])
    sums4 = add_pairs(part4, sib4, tags4)
    started_b = _start_copies([sums4[0]], [landing(sums4[0])], 3, _chips_plan(1), "start_b")

    out = dict(grad={}, delta={}, new_m={}, new_v={})
    results = _adamw_group(_adam_items(names1 + names2, own1 + own2, other1 + other2, w, m, v), "early",
                           [started_a[3], started_b[3]])
    _adam_store(names1 + names2, results, w, out)
    got5, other3 = _wait_copies(started_a, 4, plan_a, results[0][1], "wait_a")
    results = _adamw_group(_adam_items(names3, own3, [other3], w, m, v), "w_in")
    _adam_store(names3, results, w, out)
    (got4,) = _wait_copies(started_b, 3, _chips_plan(1), results[0][1], "wait_b")
    own4 = add_chips(sums4, [got4], tags4)
    own5 = add_chips(sums5, [got5], tags5)
    other4, other5 = _alone(_share_rider(own4 + own5), "share_last")
    results = _adamw_group(_adam_items(names4 + names5, own4 + own5, [other4, other5], w, m, v), "ffn1")
    _adam_store(names4 + names5, results, w, out)

    shapes = {name: w[name].shape for name in VECTOR_PARAMS + ("pool_w",)}
    loss, results = _adamw_small(_pack_small(w), small_all, _pack_small(m), _pack_small(v), shapes)
    for name, res in results.items():
        out["grad"][name], out["delta"][name], out["new_m"][name], out["new_v"][name] = (a.reshape(shapes[name]) for a in res)

    return (loss[0, 0], dx[None], *[out["grad"][n] for n in ALL_PARAMS], *[out["delta"][n] for n in ALL_PARAMS],
            *[out["new_m"][n] for n in ALL_PARAMS], *[out["new_v"][n] for n in ALL_PARAMS])
```

```python
import functools

import jax
import jax.numpy as jnp
from jax import lax
from jax.experimental import pallas as pl
from jax.experimental.pallas import tpu as pltpu

F32 = jnp.float32
BF16 = jnp.bfloat16
MESH = pl.DeviceIdType.MESH

D_MODEL = 1024
D_FF = 2816
HEADS = 8
HEAD_DIM = 128
POOL_WIDTH = 512
POOL_WINDOWS = (2, 4, 8, 16)
POOL_HALO = 16
N_CHIPS = 4
EPS = 1e-6
CHUNK = 64
MAIN_COLS = 4096
GATE_COLS = 2048
SHARD_IN_COLS = 1664

ADAM_LR = 0.001
ADAM_B1 = 0.9
ADAM_B2 = 0.999
ADAM_EPS = 1e-08
ADAM_WD = 0.01
ADAM_STEP = 10

VMEM_LIMIT = 56 * 1024 * 1024
WGRAD_IN_BLOCKS = 4
WGRAD_OUT_BLOCKS = 2
ADAM_BLOCKS = 4


def _params(semantics=None, vmem=VMEM_LIMIT):
    return pltpu.CompilerParams(dimension_semantics=semantics, vmem_limit_bytes=vmem)


def _dot(a, b):
    return jnp.dot(a, b, preferred_element_type=F32)


def _dot_nt(a, b):
    return lax.dot_general(a, b, (((1,), (1,)), ((), ())), preferred_element_type=F32)


def _dot_tn(a, b):
    return lax.dot_general(a, b, (((0,), (0,)), ((), ())), preferred_element_type=F32)


def _tri_sum(tri, x):
    hi = x.astype(BF16)
    lo = (x - hi.astype(F32)).astype(BF16)
    return _dot(tri, hi) + _dot(tri, lo)


def _sigmoid(x):
    return jax.nn.sigmoid(x)


def _resident(shape):
    zeros = (0,) * len(shape)
    return pl.BlockSpec(shape, lambda *_: zeros, pipeline_mode=pl.Buffered(1))


def _pick(shape, k):
    zeros = (0,) * (len(shape) - 1)
    return pl.BlockSpec((None,) + tuple(shape[1:]), lambda *_: (k,) + zeros, pipeline_mode=pl.Buffered(1))


def _rows(tm, cols, col_block=0):
    return pl.BlockSpec((tm, cols), lambda i: (i, col_block))


def _acc(shape):
    zeros = (0,) * len(shape)
    return pl.BlockSpec(shape, lambda *_: zeros)


def _rms(x):
    r = lax.rsqrt(jnp.mean(x * x, axis=-1, keepdims=True) + EPS)
    return r, x * r


def _rms_bwd(dn, n, r):
    return r * (dn - n * jnp.mean(dn * n, axis=-1, keepdims=True))


def _colsum(a):
    return jnp.sum(a, axis=0, keepdims=True)


ANY = pl.BlockSpec(memory_space=pl.ANY)


class _Rider:
    def __init__(self, inputs, out_shape, sems, phases):
        self.inputs, self.out_shape, self.sems, self.phases = list(inputs), list(out_shape), list(sems), list(phases)


def _hosted(riders, body, *, name, grid=(), in_specs, out_specs, out_shape, scratch_shapes=(), compiler_params=None):
    riders = [r for r in riders if r is not None]
    n_in, n_out, n_scr = len(in_specs), len(out_shape), len(scratch_shapes)
    n_steps = 1
    for g in grid:
        n_steps *= g

    def wrapped(*refs):
        pos = n_in
        ins = refs[:n_in]
        r_ins = []
        for r in riders:
            r_ins.append(refs[pos:pos + len(r.inputs)])
            pos += len(r.inputs)
        outs = refs[pos:pos + n_out]
        pos += n_out
        r_outs = []
        for r in riders:
            r_outs.append(refs[pos:pos + len(r.out_shape)])
            pos += len(r.out_shape)
        scr = refs[pos:pos + n_scr]
        pos += n_scr
        r_sems = []
        for r in riders:
            r_sems.append(refs[pos:pos + len(r.sems)])
            pos += len(r.sems)
        step = 0
        for axis in range(len(grid)):
            step = step * grid[axis] + pl.program_id(axis)

        def at_step(which, fn):
            if n_steps == 1:
                fn()
            else:
                pl.when(step == which)(fn)

        for r, ri, ro, rs in zip(riders, r_ins, r_outs, r_sems):
            for fraction, fn in r.phases:
                if fraction == 0:
                    at_step(0, functools.partial(fn, ri, ro, rs))
        body(*ins, *outs, *scr)
        for r, ri, ro, rs in zip(riders, r_ins, r_outs, r_sems):
            for fraction, fn in r.phases:
                if fraction > 0:
                    at_step(min(int(fraction * n_steps), n_steps - 1), functools.partial(fn, ri, ro, rs))

    call = pl.pallas_call(
        wrapped, name=name, grid=grid,
        in_specs=list(in_specs) + [ANY for r in riders for _ in r.inputs],
        out_specs=list(out_specs) + [ANY for r in riders for _ in r.out_shape],
        out_shape=list(out_shape) + [s for r in riders for s in r.out_shape],
        scratch_shapes=list(scratch_shapes) + [s for r in riders for s in r.sems],
        compiler_params=compiler_params)

    def run(*args):
        res = call(*args, *[a for r in riders for a in r.inputs])
        extras, pos = [], n_out
        for r in riders:
            extras.append(list(res[pos:pos + len(r.out_shape)]))
            pos += len(r.out_shape)
        return list(res[:n_out]), extras

    return run


def _ffn_fwd(x, g, ffnw, tag, tm, riders=()):
    t = x.shape[0]

    def body(x_ref, g_ref, w1_ref, w3_ref, w2_ref, xo_ref, a_ref, b_ref):
        xv = x_ref[...]
        _, n = _rms(xv)
        h = (n * g_ref[...]).astype(BF16)
        a = _dot_nt(h, w1_ref[...])
        b = _dot_nt(h, w3_ref[...])
        s = (a * _sigmoid(a) * b).astype(BF16)
        xo_ref[...] = xv + 0.5 * _dot(s, w2_ref[...])
        a_ref[...] = a.astype(BF16)
        b_ref[...] = b.astype(BF16)

    return _hosted(
        riders, body, name=f"ffn_fwd_{tag}", grid=(t // tm,),
        in_specs=[_rows(tm, D_MODEL), _resident((1, D_MODEL)), _pick(ffnw.shape, 0), _pick(ffnw.shape, 1),
                  _pick(ffnw.shape, 2)],
        out_specs=[_rows(tm, D_MODEL), _rows(tm, D_FF), _rows(tm, D_FF)],
        out_shape=[jax.ShapeDtypeStruct((t, D_MODEL), F32), jax.ShapeDtypeStruct((t, D_FF), BF16),
                   jax.ShapeDtypeStruct((t, D_FF), BF16)],
        compiler_params=_params(("arbitrary",)),
    )(x, g, ffnw, ffnw, ffnw)


def _ffn_bwd(dxo, x, g, a, b, ffnw, tag, tm, riders=()):
    t = x.shape[0]

    def body(dxo_ref, x_ref, g_ref, a_ref, b_ref, w1_ref, w3_ref, w2_ref, dx_ref, dab_ref, s_ref, h_ref, dxh_ref, dg_ref):
        @pl.when(pl.program_id(0) == 0)
        def _():
            dg_ref[...] = jnp.zeros_like(dg_ref)

        xv = x_ref[...]
        gv = g_ref[...]
        r, n = _rms(xv)
        h_ref[...] = (n * gv).astype(BF16)
        dxo_v = dxo_ref[...]
        dxh = (0.5 * dxo_v).astype(BF16)
        dxh_ref[...] = dxh
        ds = _dot_nt(dxh, w2_ref[...])
        av = a_ref[...].astype(F32)
        bv = b_ref[...].astype(F32)
        sg = _sigmoid(av)
        silu = av * sg
        s_ref[...] = (silu * bv).astype(BF16)
        da = (ds * bv * (sg * (1.0 + av * (1.0 - sg)))).astype(BF16)
        db = (ds * silu).astype(BF16)
        dab_ref[:, :D_FF] = da
        dab_ref[:, D_FF:] = db
        dh = _dot(da, w1_ref[...]) + _dot(db, w3_ref[...])
        dg_ref[...] += _colsum(dh * n)
        dx_ref[...] = dxo_v + _rms_bwd(dh * gv, n, r)

    return _hosted(
        riders, body, name=f"ffn_bwd_{tag}", grid=(t // tm,),
        in_specs=[_rows(tm, D_MODEL), _rows(tm, D_MODEL), _resident((1, D_MODEL)), _rows(tm, D_FF), _rows(tm, D_FF),
                  _pick(ffnw.shape, 0), _pick(ffnw.shape, 1), _pick(ffnw.shape, 2)],
        out_specs=[_rows(tm, D_MODEL), _rows(tm, 2 * D_FF), _rows(tm, D_FF), _rows(tm, D_MODEL), _rows(tm, D_MODEL),
                   _acc((1, D_MODEL))],
        out_shape=[jax.ShapeDtypeStruct((t, D_MODEL), F32), jax.ShapeDtypeStruct((t, 2 * D_FF), BF16),
                   jax.ShapeDtypeStruct((t, D_FF), BF16), jax.ShapeDtypeStruct((t, D_MODEL), BF16),
                   jax.ShapeDtypeStruct((t, D_MODEL), BF16), jax.ShapeDtypeStruct((1, D_MODEL), F32)],
        compiler_params=_params(("arbitrary",)),
    )(dxo, x, g, a, b, ffnw, ffnw, ffnw)


def _wgrad_body(n_token_tiles):
    def body(x_ref, dy_ref, o_ref, acc):
        k = pl.program_id(1)

        @pl.when(k == 0)
        def _():
            acc[...] = jnp.zeros_like(acc)

        acc[...] += _dot_tn(x_ref[...], dy_ref[...])

        @pl.when(k == n_token_tiles - 1)
        def _():
            o_ref[...] = acc[...].astype(BF16)

    return body


def _wgrad(xm, dy, out_blocks, name, tk, riders=()):
    t, m = xm.shape
    n = dy.shape[1]
    mb = m // out_blocks

    return _hosted(
        riders, _wgrad_body(t // tk), name=name, grid=(out_blocks, t // tk),
        in_specs=[pl.BlockSpec((tk, mb), lambda j, k: (k, j)), pl.BlockSpec((tk, n), lambda j, k: (k, 0))],
        out_specs=[pl.BlockSpec((None, mb, n), lambda j, k: (j, 0, 0))],
        out_shape=[jax.ShapeDtypeStruct((out_blocks, mb, n), BF16)],
        scratch_shapes=[pltpu.VMEM((mb, n), F32)],
        compiler_params=_params(("arbitrary", "arbitrary")),
    )(xm, dy)


def _wgrad_cols(xm, dy, out_blocks, name, tk, riders=()):
    t, m = xm.shape
    n = dy.shape[1]
    nb = n // out_blocks

    return _hosted(
        riders, _wgrad_body(t // tk), name=name, grid=(out_blocks, t // tk),
        in_specs=[pl.BlockSpec((tk, m), lambda j, k: (k, 0)), pl.BlockSpec((tk, nb), lambda j, k: (k, j))],
        out_specs=[pl.BlockSpec((None, m, nb), lambda j, k: (j, 0, 0))],
        out_shape=[jax.ShapeDtypeStruct((out_blocks, m, nb), BF16)],
        scratch_shapes=[pltpu.VMEM((m, nb), F32)],
        compiler_params=_params(("arbitrary", "arbitrary")),
    )(xm, dy)


def _mix_fwd(x1, g, winw, tm, riders=()):
    t = x1.shape[0]

    def body(x_ref, g_ref, w_ref, main_ref, pool_ref, gate_ref):
        _, n = _rms(x_ref[...])
        h = (n * g_ref[...]).astype(BF16)
        proj = jnp.concatenate([_dot(h, w_ref[j]) for j in range(N_CHIPS)], axis=1)
        main_ref[...] = proj[:, :MAIN_COLS]
        pool_ref[...] = proj[:, MAIN_COLS:MAIN_COLS + POOL_WIDTH]
        gate_ref[...] = proj[:, MAIN_COLS + POOL_WIDTH:]

    return _hosted(
        riders, body, name="mix_fwd", grid=(t // tm,),
        in_specs=[_rows(tm, D_MODEL), _resident((1, D_MODEL)), _resident(winw.shape)],
        out_specs=[_rows(tm, MAIN_COLS), _rows(tm, POOL_WIDTH), _rows(tm, GATE_COLS)],
        out_shape=[jax.ShapeDtypeStruct((t, MAIN_COLS), F32), jax.ShapeDtypeStruct((t, POOL_WIDTH), F32),
                   jax.ShapeDtypeStruct((t, GATE_COLS), F32)],
        compiler_params=_params(("arbitrary",)),
    )(x1, g, winw)


def _mix_bwd(dqfi, dog, du, dgates, dx2, x1, g, winw, tm, riders=()):
    t = x1.shape[0]
    cols = N_CHIPS * SHARD_IN_COLS

    def body(dqfi_ref, dog_ref, du_ref, dgt_ref, dx2_ref, x_ref, g_ref, w_ref, dx_ref, dproj_ref, h_ref, dg_ref):
        @pl.when(pl.program_id(0) == 0)
        def _():
            dg_ref[...] = jnp.zeros_like(dg_ref)

        dproj = jnp.concatenate([dqfi_ref[...], dog_ref[...], du_ref[...], dgt_ref[...]], axis=1)
        dproj_ref[...] = dproj
        dh = _dot_nt(dproj[:, :SHARD_IN_COLS], w_ref[0])
        for j in range(1, N_CHIPS):
            dh += _dot_nt(dproj[:, j * SHARD_IN_COLS:(j + 1) * SHARD_IN_COLS], w_ref[j])
        gv = g_ref[...]
        r, n = _rms(x_ref[...])
        h_ref[...] = (n * gv).astype(BF16)
        dg_ref[...] += _colsum(dh * n)
        dx_ref[...] = dx2_ref[...] + _rms_bwd(dh * gv, n, r)

    return _hosted(
        riders, body, name="mix_bwd", grid=(t // tm,),
        in_specs=[_rows(tm, 3 * D_MODEL), _rows(tm, D_MODEL), _rows(tm, POOL_WIDTH), _rows(tm, GATE_COLS),
                  _rows(tm, D_MODEL), _rows(tm, D_MODEL), _resident((1, D_MODEL)), _resident(winw.shape)],
        out_specs=[_rows(tm, D_MODEL), _rows(tm, cols), _rows(tm, D_MODEL), _acc((1, D_MODEL))],
        out_shape=[jax.ShapeDtypeStruct((t, D_MODEL), F32), jax.ShapeDtypeStruct((t, cols), BF16),
                   jax.ShapeDtypeStruct((t, D_MODEL), BF16), jax.ShapeDtypeStruct((1, D_MODEL), F32)],
        compiler_params=_params(("arbitrary",)),
    )(dqfi, dog, du, dgates, dx2, x1, g, winw)


def _lower_bound(lb_raw):
    l0 = lb_raw[0:1, :]
    l1 = lb_raw[1:2, :]
    m = jnp.maximum(l0, l1)
    e0 = jnp.exp(l0 - m)
    e1 = jnp.exp(l1 - m)
    return e0 / (e0 + e1)


def _head_slices():
    return [slice(h * HEAD_DIM, (h + 1) * HEAD_DIM) for h in range(HEADS)]


def _gates(qr, fr, lb, tril_b, first_half):
    sg = _sigmoid(fr)
    f = lb + (1.0 - lb) * sg
    k = 1.0 - f
    sq = _sigmoid(qr)
    q = qr * sq
    log_f = jnp.log(f)
    gc = _tri_sum(tril_b, log_f)
    gm = _colsum(jnp.where(first_half, log_f, 0.0))
    gl = _colsum(log_f)
    e_q = jnp.exp(gc - gm)
    e_k = jnp.exp(gm - gc)
    e_in = jnp.exp(gc)
    e_out = jnp.exp(gl - gc)
    return dict(sg=sg, f=f, k=k, sq=sq, q=q, e_q=e_q, e_k=e_k, e_in=e_in, e_out=e_out, e_last=jnp.exp(gl))


def _hgrn_fwd(main, lb_raw, tt, riders=()):
    t = main.shape[0]
    n_local = tt // CHUNK

    def body(q_ref, f_ref, i_ref, lb_ref, o_ref, st_ref, s_scr):
        @pl.when(pl.program_id(0) == 0)
        def _():
            s_scr[...] = jnp.zeros_like(s_scr)

        lb = _lower_bound(lb_ref[...])
        row = lax.broadcasted_iota(jnp.int32, (CHUNK, CHUNK), 0)
        col = lax.broadcasted_iota(jnp.int32, (CHUNK, CHUNK), 1)
        tril = row >= col
        tril_b = tril.astype(BF16)
        first_half = lax.broadcasted_iota(jnp.int32, (CHUNK, D_MODEL), 0) < CHUNK // 2
        heads = _head_slices()

        def chunk(c, carry):
            rows = pl.ds(pl.multiple_of(c * CHUNK, CHUNK), CHUNK)
            z = _gates(q_ref[rows, :], f_ref[rows, :], lb, tril_b, first_half)
            qt = (z["q"] * z["e_q"]).astype(BF16)
            kt = (z["k"] * z["e_k"]).astype(BF16)
            qg = (z["q"] * z["e_in"]).astype(BF16)
            kg = (z["k"] * z["e_out"]).astype(BF16)
            vb = i_ref[rows, :].astype(BF16)
            states = [s_scr[h] for h in range(HEADS)]
            for h in range(HEADS):
                st_ref[c, h] = states[h]
            raw = [_dot_nt(qt[:, sl], kt[:, sl]) for sl in heads]
            inter = [_dot_nt(qg[:, sl], states[h].astype(BF16)) for h, sl in enumerate(heads)]
            grown = [_dot_tn(vb[:, sl], kg[:, sl]) for sl in heads]
            scores = [jnp.where(tril, r, 0.0).astype(BF16) for r in raw]
            for h, sl in enumerate(heads):
                s_scr[h] = states[h] * z["e_last"][:, sl] + grown[h]
            o_ref[rows, :] = jnp.concatenate([_dot(scores[h], vb[:, sl]) + inter[h] for h, sl in enumerate(heads)], axis=1)
            return carry

        lax.fori_loop(0, n_local, chunk, 0, unroll=True)

    return _hosted(
        riders, body, name="hgrn_fwd", grid=(t // tt,),
        in_specs=[_rows(tt, D_MODEL, 0), _rows(tt, D_MODEL, 1), _rows(tt, D_MODEL, 2), _resident((2, D_MODEL))],
        out_specs=[_rows(tt, D_MODEL),
                   pl.BlockSpec((n_local, HEADS, HEAD_DIM, HEAD_DIM), lambda i: (i, 0, 0, 0))],
        out_shape=[jax.ShapeDtypeStruct((t, D_MODEL), F32),
                   jax.ShapeDtypeStruct((t // CHUNK, HEADS, HEAD_DIM, HEAD_DIM), F32)],
        scratch_shapes=[pltpu.VMEM((HEADS, HEAD_DIM, HEAD_DIM), F32)],
        compiler_params=_params(("arbitrary",)),
    )(main, main, main, lb_raw)


def _hgrn_bwd(main, lb_raw, states, do, tt, riders=()):
    t = main.shape[0]
    n_tiles = t // tt
    n_local = tt // CHUNK

    def rev(col_block):
        return pl.BlockSpec((tt, D_MODEL), lambda i: (n_tiles - 1 - i, col_block))

    def body(q_ref, f_ref, i_ref, lb_ref, st_ref, do_ref, dqfi_ref, dlb_ref, ds_scr, acc_scr):
        @pl.when(pl.program_id(0) == 0)
        def _():
            ds_scr[...] = jnp.zeros_like(ds_scr)
            acc_scr[...] = jnp.zeros_like(acc_scr)

        lb = _lower_bound(lb_ref[...])
        row = lax.broadcasted_iota(jnp.int32, (CHUNK, CHUNK), 0)
        col = lax.broadcasted_iota(jnp.int32, (CHUNK, CHUNK), 1)
        tril = row >= col
        tril_b = tril.astype(BF16)
        triu_b = (row <= col).astype(BF16)
        first_half = lax.broadcasted_iota(jnp.int32, (CHUNK, D_MODEL), 0) < CHUNK // 2
        heads = _head_slices()
        cat = functools.partial(jnp.concatenate, axis=1)

        def chunk(cc, carry):
            c = n_local - 1 - cc
            rows = pl.ds(pl.multiple_of(c * CHUNK, CHUNK), CHUNK)
            qr = q_ref[rows, :]
            z = _gates(qr, f_ref[rows, :], lb, tril_b, first_half)
            qt = (z["q"] * z["e_q"]).astype(BF16)
            kt = (z["k"] * z["e_k"]).astype(BF16)
            qg_f = z["q"] * z["e_in"]
            qg = qg_f.astype(BF16)
            kg_f = z["k"] * z["e_out"]
            kg = kg_f.astype(BF16)
            vb = i_ref[rows, :].astype(BF16)
            dob = do_ref[rows, :].astype(BF16)
            st = [st_ref[c, h] for h in range(HEADS)]
            dst = [ds_scr[h] for h in range(HEADS)]
            dst_b = [d.astype(BF16) for d in dst]
            raw = [_dot_nt(qt[:, sl], kt[:, sl]) for sl in heads]
            draw = [_dot_nt(dob[:, sl], vb[:, sl]) for sl in heads]
            dqg = [_dot(dob[:, sl], st[h].astype(BF16)) for h, sl in enumerate(heads)]
            dkg = [_dot(vb[:, sl], dst_b[h]) for h, sl in enumerate(heads)]
            dv_inter = [_dot_nt(kg[:, sl], dst_b[h]) for h, sl in enumerate(heads)]
            grown = [_dot_tn(dob[:, sl], qg[:, sl]) for sl in heads]
            scores = [jnp.where(tril, r, 0.0).astype(BF16) for r in raw]
            dscores = [jnp.where(tril, r, 0.0).astype(BF16) for r in draw]
            dqt = [_dot(dscores[h], kt[:, sl]) for h, sl in enumerate(heads)]
            dkt = [_dot_tn(dscores[h], qt[:, sl]) for h, sl in enumerate(heads)]
            dv = [_dot_tn(scores[h], dob[:, sl]) + dv_inter[h] for h, sl in enumerate(heads)]
            carry_in = cat([z["e_last"][:, sl] * _colsum(dst[h] * st[h]) for h, sl in enumerate(heads)])
            for h, sl in enumerate(heads):
                ds_scr[h] = dst[h] * z["e_last"][:, sl] + grown[h]
            dqt, dkt, dqg, dkg = cat(dqt), cat(dkt), cat(dqg), cat(dkg)
            carry_in += _colsum(dkg * kg_f)
            dq = dqt * z["e_q"] + dqg * z["e_in"]
            dk = dkt * z["e_k"] + dkg * z["e_out"]
            dgate = (qt.astype(F32) * dqt - kt.astype(F32) * dkt) + (qg_f * dqg - kg_f * dkg)
            dlogf = _tri_sum(triu_b, dgate) + carry_in
            df = dlogf / z["f"] - dk
            sg = z["sg"]
            sq = z["sq"]
            acc_scr[...] += _colsum(df * (1.0 - sg))
            dqfi_ref[rows, 0:D_MODEL] = (dq * (sq * (1.0 + qr * (1.0 - sq)))).astype(BF16)
            dqfi_ref[rows, D_MODEL:2 * D_MODEL] = (df * (1.0 - lb) * sg * (1.0 - sg)).astype(BF16)
            dqfi_ref[rows, 2 * D_MODEL:3 * D_MODEL] = cat(dv).astype(BF16)
            return carry

        lax.fori_loop(0, n_local, chunk, 0, unroll=True)
        d0 = acc_scr[...] * lb * (1.0 - lb)
        dlb_ref[0:1, :] = d0
        dlb_ref[1:2, :] = -d0

    return _hosted(
        riders, body, name="hgrn_bwd", grid=(n_tiles,),
        in_specs=[rev(0), rev(1), rev(2), _resident((2, D_MODEL)),
                  pl.BlockSpec((n_local, HEADS, HEAD_DIM, HEAD_DIM), lambda i: (n_tiles - 1 - i, 0, 0, 0)),
                  rev(0)],
        out_specs=[pl.BlockSpec((tt, 3 * D_MODEL), lambda i: (n_tiles - 1 - i, 0)), _acc((2, D_MODEL))],
        out_shape=[jax.ShapeDtypeStruct((t, 3 * D_MODEL), BF16), jax.ShapeDtypeStruct((2, D_MODEL), F32)],
        scratch_shapes=[pltpu.VMEM((HEADS, HEAD_DIM, HEAD_DIM), F32), pltpu.VMEM((1, D_MODEL), F32)],
        compiler_params=_params(("arbitrary",)),
    )(main, main, main, lb_raw, states, do)


def _head_norm(o):
    rs, ns = [], []
    for h in range(HEADS):
        oh = o[:, h * HEAD_DIM:(h + 1) * HEAD_DIM]
        r, n = _rms(oh)
        rs.append(jnp.broadcast_to(r, oh.shape))
        ns.append(n)
    return jnp.concatenate(rs, axis=1), jnp.concatenate(ns, axis=1)


def _head_norm_bwd(dn, n, r):
    outs = []
    for h in range(HEADS):
        sl = slice(h * HEAD_DIM, (h + 1) * HEAD_DIM)
        outs.append(_rms_bwd(dn[:, sl], n[:, sl], r[:, sl]))
    return jnp.concatenate(outs, axis=1)


def _window_counts(first_row, tm):
    pos = (first_row + 1 + lax.broadcasted_iota(jnp.int32, (tm, 1), 0)).astype(F32)
    return [jnp.minimum(pos, float(w)) for w in POOL_WINDOWS]


def _post_fwd(o, main, pool_r, gates, x1, onorm, pool_w, pool_scale, sqw, wbw, tm, riders=()):
    t = o.shape[0]
    ext_rows = tm + POOL_HALO

    def body(o_ref, og_ref, u_ref, gt_ref, x1_ref, on_ref, pw_ref, ps_ref, wa_ref, wout_ref, wb_ref,
             x2_ref, ya_ref, yb_ref, pooled_ref, ext):
        i = pl.program_id(0)

        @pl.when(i == 0)
        def _():
            ext[0:POOL_HALO, :] = jnp.zeros((POOL_HALO, POOL_WIDTH), F32)

        _, n = _head_norm(o_ref[...])
        og = og_ref[...]
        oa = (n * on_ref[...] * (og * _sigmoid(og))).astype(BF16)
        ya = _dot(oa, wa_ref[...])

        u = u_ref[...]
        ext[POOL_HALO:ext_rows, :] = u
        e = ext[...]
        counts = _window_counts(i * tm, tm)
        pooled = []
        for gidx, w in enumerate(POOL_WINDOWS):
            s = e[:, gidx * HEAD_DIM:(gidx + 1) * HEAD_DIM]
            shift = 1
            while shift < w:
                s = s + pltpu.roll(s, shift, axis=0)
                shift *= 2
            pooled.append(s[POOL_HALO:, :] / counts[gidx] - u[:, gidx * HEAD_DIM:(gidx + 1) * HEAD_DIM])
        ext[0:POOL_HALO, :] = ext[tm:ext_rows, :]
        pooled_b = [pg.astype(BF16) for pg in pooled]
        pooled_ref[...] = jnp.concatenate(pooled_b, axis=1)
        mixed = jnp.concatenate([_dot(pooled_b[gidx], pw_ref[gidx].astype(BF16)) for gidx in range(len(POOL_WINDOWS))],
                                axis=1) * ps_ref[...]
        mixed_b = mixed.astype(BF16)
        yb = jnp.concatenate([_dot(mixed_b, wb_ref[j]) for j in range(N_CHIPS)], axis=1)

        gt = gt_ref[...]
        y = _sigmoid(gt[:, :D_MODEL]) * ya + _sigmoid(gt[:, D_MODEL:]) * yb
        x2_ref[...] = x1_ref[...] + _dot(y.astype(BF16), wout_ref[...])
        ya_ref[...] = ya.astype(BF16)
        yb_ref[...] = yb.astype(BF16)

    return _hosted(
        riders, body, name="post_fwd", grid=(t // tm,),
        in_specs=[_rows(tm, D_MODEL), _rows(tm, D_MODEL, 3), _rows(tm, POOL_WIDTH), _rows(tm, GATE_COLS), _rows(tm, D_MODEL),
                  _resident((1, D_MODEL)), _resident(pool_w.shape), _resident((1, POOL_WIDTH)),
                  _pick(sqw.shape, 0), _pick(sqw.shape, 1), _resident(wbw.shape)],
        out_specs=[_rows(tm, D_MODEL), _rows(tm, D_MODEL), _rows(tm, D_MODEL), _rows(tm, POOL_WIDTH)],
        out_shape=[jax.ShapeDtypeStruct((t, D_MODEL), F32), jax.ShapeDtypeStruct((t, D_MODEL), BF16),
                   jax.ShapeDtypeStruct((t, D_MODEL), BF16), jax.ShapeDtypeStruct((t, POOL_WIDTH), BF16)],
        scratch_shapes=[pltpu.VMEM((ext_rows, POOL_WIDTH), F32)],
        compiler_params=_params(("arbitrary",)),
    )(o, main, pool_r, gates, x1, onorm, pool_w, pool_scale, sqw, sqw, wbw)


def _post_bwd(dx2, o, main, gates, ya, yb, pooled, onorm, pool_w, pool_scale, sqw, wbw, tm, riders=()):
    t = o.shape[0]
    n_tiles = t // tm
    ext_rows = tm + POOL_HALO
    n_groups = len(POOL_WINDOWS)

    def rev(cols, col_block=0):
        return pl.BlockSpec((tm, cols), lambda i: (n_tiles - 1 - i, col_block))

    def body(dx2_ref, o_ref, og_ref, gt_ref, ya_ref, yb_ref, pooled_ref, on_ref, pw_ref, ps_ref, wa_ref, wout_ref, wb_ref,
             do_ref, dog_ref, du_ref, dgt_ref, dwa_ref, dwout_ref, dwb_ref, dpw_ref, dps_ref, don_ref, ext):
        i = pl.program_id(0)

        @pl.when(i == 0)
        def _():
            ext[tm:ext_rows, :] = jnp.zeros((POOL_HALO, POOL_WIDTH), F32)
            for ref in (dwa_ref, dwout_ref, dwb_ref, dpw_ref, dps_ref, don_ref):
                ref[...] = jnp.zeros_like(ref)

        groups = [slice(gidx * HEAD_DIM, (gidx + 1) * HEAD_DIM) for gidx in range(n_groups)]
        shards = [slice(j * 256, (j + 1) * 256) for j in range(N_CHIPS)]
        dx2b = dx2_ref[...].astype(BF16)
        dy = _dot_nt(dx2b, wout_ref[...])
        pooled_b = pooled_ref[...]
        pm = jnp.concatenate([_dot(pooled_b[:, g], pw_ref[gidx].astype(BF16)) for gidx, g in enumerate(groups)], axis=1)
        gt = gt_ref[...]
        sga = _sigmoid(gt[:, :D_MODEL])
        sgb = _sigmoid(gt[:, D_MODEL:])
        ya = ya_ref[...].astype(F32)
        yb = yb_ref[...].astype(F32)
        y = (sga * ya + sgb * yb).astype(BF16)
        dya = (dy * sga).astype(BF16)
        dyb = (dy * sgb).astype(BF16)
        dgt_ref[:, :D_MODEL] = (dy * ya * sga * (1.0 - sga)).astype(BF16)
        dgt_ref[:, D_MODEL:] = (dy * yb * sgb * (1.0 - sgb)).astype(BF16)
        dwout_ref[...] += _dot_tn(y, dx2b)
        doa = _dot_nt(dya, wa_ref[...])
        dmixed = _dot_nt(dyb[:, shards[0]], wb_ref[0])
        for j in range(1, N_CHIPS):
            dmixed += _dot_nt(dyb[:, shards[j]], wb_ref[j])
        r, n = _head_norm(o_ref[...])
        onv = on_ref[...]
        og = og_ref[...]
        sog = _sigmoid(og)
        silu_og = og * sog
        normed = n * onv
        oa = (normed * silu_og).astype(BF16)
        dog_ref[...] = (doa * normed * (sog * (1.0 + og * (1.0 - sog)))).astype(BF16)
        dnormed = doa * silu_og
        don_ref[...] += _colsum(dnormed * n)
        do_ref[...] = _head_norm_bwd(dnormed * onv, n, r)
        psv = ps_ref[...]
        mixed_b = (pm * psv).astype(BF16)
        dps_ref[...] += _colsum(dmixed * pm)
        dpm = (dmixed * psv).astype(BF16)
        dwa_ref[...] += _dot_tn(oa, dya)
        for j in range(N_CHIPS):
            dwb_ref[j] += _dot_tn(mixed_b, dyb[:, shards[j]])
        counts = _window_counts((n_tiles - 1 - i) * tm, tm)
        dpooled = []
        for gidx, g in enumerate(groups):
            dpw_ref[gidx] += _dot_tn(pooled_b[:, g], dpm[:, g])
            dpooled.append(_dot_nt(dpm[:, g], pw_ref[gidx].astype(BF16)))
        ext[0:tm, :] = jnp.concatenate([dpooled[gidx] / counts[gidx] for gidx in range(n_groups)], axis=1)
        e = ext[...]
        du = []
        for gidx, w in enumerate(POOL_WINDOWS):
            s = e[:, gidx * HEAD_DIM:(gidx + 1) * HEAD_DIM]
            shift = 1
            while shift < w:
                s = s + pltpu.roll(s, ext_rows - shift, axis=0)
                shift *= 2
            du.append(s[:tm, :] - dpooled[gidx])
        ext[tm:ext_rows, :] = ext[0:POOL_HALO, :]
        du_ref[...] = jnp.concatenate(du, axis=1).astype(BF16)

    wa_shape = (D_MODEL, D_MODEL)
    return _hosted(
        riders, body, name="post_bwd", grid=(n_tiles,),
        in_specs=[rev(D_MODEL), rev(D_MODEL), rev(D_MODEL, 3), rev(GATE_COLS), rev(D_MODEL), rev(D_MODEL), rev(POOL_WIDTH),
                  _resident((1, D_MODEL)), _resident(pool_w.shape), _resident((1, POOL_WIDTH)),
                  _pick(sqw.shape, 0), _pick(sqw.shape, 1), _resident(wbw.shape)],
        out_specs=[rev(D_MODEL), rev(D_MODEL), rev(POOL_WIDTH), rev(GATE_COLS),
                   _acc(wa_shape), _acc(wa_shape), _acc(wbw.shape), _acc(pool_w.shape), _acc((1, POOL_WIDTH)),
                   _acc((1, D_MODEL))],
        out_shape=[jax.ShapeDtypeStruct((t, D_MODEL), F32), jax.ShapeDtypeStruct((t, D_MODEL), BF16),
                   jax.ShapeDtypeStruct((t, POOL_WIDTH), BF16), jax.ShapeDtypeStruct((t, GATE_COLS), BF16),
                   jax.ShapeDtypeStruct(wa_shape, F32), jax.ShapeDtypeStruct(wa_shape, F32),
                   jax.ShapeDtypeStruct(wbw.shape, F32), jax.ShapeDtypeStruct(pool_w.shape, F32),
                   jax.ShapeDtypeStruct((1, POOL_WIDTH), F32), jax.ShapeDtypeStruct((1, D_MODEL), F32)],
        scratch_shapes=[pltpu.VMEM((ext_rows, POOL_WIDTH), F32)],
        compiler_params=_params(("arbitrary",)),
    )(dx2, o, main, gates, ya, yb, pooled, onorm, pool_w, pool_scale, sqw, sqw, wbw)


def _tail(x3, p, target, g_ple, g_post, g_final, sqw, wpw, tm, riders=()):
    t = x3.shape[0]
    pd = p.shape[1]

    def body(x_ref, p_ref, tg_ref, g4_ref, g5_ref, g6_ref, wg_ref, wp_ref,
             dx_ref, loss_ref, dwg_ref, dwp_ref, dg4_ref, dg5_ref, dg6_ref):
        @pl.when(pl.program_id(0) == 0)
        def _():
            for ref in (loss_ref, dwg_ref, dwp_ref, dg4_ref, dg5_ref, dg6_ref):
                ref[...] = jnp.zeros_like(ref)

        x3v = x_ref[...]
        g4, g5, g6 = g4_ref[...], g5_ref[...], g6_ref[...]
        r4, n4 = _rms(x3v)
        h4 = (n4 * g4).astype(BF16)
        gate = _sigmoid(_dot(h4, wg_ref[...]))
        pb = p_ref[...].astype(BF16)
        r5, n5 = _rms(jnp.concatenate([_dot(pb, wp_ref[j]) for j in range(N_CHIPS)], axis=1))
        emb = n5 * g5
        r6, n6 = _rms(x3v + gate * emb)
        diff = n6 * g6 - tg_ref[...]
        loss_ref[...] += 0.5 * jnp.sum(jnp.mean(diff * diff, axis=-1, keepdims=True), axis=0, keepdims=True)
        dout = diff * (1.0 / D_MODEL)
        dg6_ref[...] += _colsum(dout * n6)
        dx4 = _rms_bwd(dout * g6, n6, r6)
        demb = dx4 * gate
        dg5_ref[...] += _colsum(demb * n5)
        dpre = _rms_bwd(demb * g5, n5, r5).astype(BF16)
        for j in range(N_CHIPS):
            dwp_ref[j] += _dot_tn(pb, dpre[:, j * pd:(j + 1) * pd])
        dz = (dx4 * emb * gate * (1.0 - gate)).astype(BF16)
        dwg_ref[...] += _dot_tn(h4, dz)
        dh4 = _dot_nt(dz, wg_ref[...])
        dg4_ref[...] += _colsum(dh4 * n4)
        dx_ref[...] = dx4 + _rms_bwd(dh4 * g4, n4, r4)

    sq_shape = (D_MODEL, D_MODEL)
    vec = (1, D_MODEL)
    return _hosted(
        riders, body, name="tail", grid=(t // tm,),
        in_specs=[_rows(tm, D_MODEL), _rows(tm, pd), _rows(tm, D_MODEL), _resident(vec), _resident(vec), _resident(vec),
                  _pick(sqw.shape, 2), _resident(wpw.shape)],
        out_specs=[_rows(tm, D_MODEL), _acc((1, 1)), _acc(sq_shape), _acc(wpw.shape), _acc(vec), _acc(vec), _acc(vec)],
        out_shape=[jax.ShapeDtypeStruct((t, D_MODEL), F32), jax.ShapeDtypeStruct((1, 1), F32),
                   jax.ShapeDtypeStruct(sq_shape, F32), jax.ShapeDtypeStruct(wpw.shape, F32),
                   jax.ShapeDtypeStruct(vec, F32), jax.ShapeDtypeStruct(vec, F32), jax.ShapeDtypeStruct(vec, F32)],
        compiler_params=_params(("arbitrary",)),
    )(x3, p, target, g_ple, g_post, g_final, sqw, wpw)


def _position():
    return lax.axis_index("x"), lax.axis_index("y"), lax.axis_index("c")


def _other_chips(x, y):
    return [(1 - x, y), (x, 1 - y), (1 - x, 1 - y)]


def _remote(src, dst, send_sems, recv_sems, k, device):
    return pltpu.make_async_remote_copy(src_ref=src, dst_ref=dst, send_sem=send_sems.at[k], recv_sem=recv_sems.at[k],
                                        device_id=device, device_id_type=MESH)


def _gather_rider(shards, forward_at):
    n = len(shards)

    def copies(ins, outs, sems):
        send_sems, recv_sems = sems
        x, y, c = _position()
        mine = 2 * x + y
        first, passed, arriving = [], [], []
        for k, (cx, cy) in enumerate(_other_chips(x, y)):
            theirs = 2 * cx + cy
            for a in range(n):
                first.append(_remote(ins[a].at[:, c], outs[a].at[:, mine, c], send_sems, recv_sems, k * n + a, (cx, cy, c)))
                block = outs[a].at[:, theirs, c]
                passed.append(_remote(block, block, send_sems, recv_sems, (3 + k) * n + a, (x, y, 1 - c)))
                other = outs[a].at[:, theirs, 1 - c]
                arriving.append(_remote(other, other, send_sems, recv_sems, (3 + k) * n + a, (x, y, 1 - c)))
        return first, passed, arriving

    return _Rider(shards, [jax.ShapeDtypeStruct((s.shape[0], N_CHIPS) + s.shape[1:], s.dtype) for s in shards],
                  [pltpu.SemaphoreType.DMA((6 * n,)), pltpu.SemaphoreType.DMA((6 * n,))], _gather_phases(copies, forward_at))


def _gather_phases(copies, forward_at):
    def begin(ins, outs, sems):
        for cp in copies(ins, outs, sems)[0]:
            cp.start()

    def forward(ins, outs, sems):
        first, passed, _ = copies(ins, outs, sems)
        for got, cp in zip(first, passed):
            got.wait_recv()
            cp.start()

    def finish(ins, outs, sems):
        first, passed, arriving = copies(ins, outs, sems)
        for cp in arriving:
            cp.wait_recv()
        for cp in first + passed:
            cp.wait_send()

    return [(0, begin), (forward_at, forward), (1, finish)]


def _relay_gather_rider(shards):
    n = len(shards)
    kinds = 8

    def copies(ins, outs, sems):
        send_sems, recv_sems = sems
        x, y, c = _position()
        mine, nx, ny, diag = 2 * x + y, 2 * (1 - x) + y, 2 * x + 1 - y, 2 * (1 - x) + 1 - y
        to_x, to_y, sibling = (1 - x, y, c), (x, 1 - y, c), (x, y, 1 - c)
        sent, relayed, passed, arriving = [], [], [], []
        for a in range(n):
            hq = ins[a].shape[2] // 2
            quarter = lambda chip, half, q: outs[a].at[:, chip, half, pl.ds(q * hq, hq)]
            rc = functools.partial(_remote, send_sems=send_sems, recv_sems=recv_sems)
            sent += [rc(ins[a].at[:, c], outs[a].at[:, mine, c], k=kinds * a, device=to_x),
                     rc(ins[a].at[:, c], outs[a].at[:, mine, c], k=kinds * a + 1, device=to_y)]
            got = [rc(outs[a].at[:, nx, c], outs[a].at[:, nx, c], k=kinds * a, device=to_x),
                   rc(outs[a].at[:, ny, c], outs[a].at[:, ny, c], k=kinds * a + 1, device=to_y),
                   rc(quarter(diag, c, 0), quarter(diag, c, 0), k=kinds * a + 2, device=to_y),
                   rc(quarter(diag, c, 1), quarter(diag, c, 1), k=kinds * a + 3, device=to_x)]
            relayed += [rc(quarter(nx, c, 0), quarter(nx, c, 0), k=kinds * a + 2, device=to_y),
                        rc(quarter(ny, c, 1), quarter(ny, c, 1), k=kinds * a + 3, device=to_x)]
            blocks = [lambda h: outs[a].at[:, nx, h], lambda h: outs[a].at[:, ny, h],
                      lambda h: quarter(diag, h, 0), lambda h: quarter(diag, h, 1)]
            passed += [(got[i], rc(b(c), b(c), k=kinds * a + 4 + i, device=sibling)) for i, b in enumerate(blocks)]
            arriving += [rc(b(1 - c), b(1 - c), k=kinds * a + 4 + i, device=sibling) for i, b in enumerate(blocks)]
        return sent, relayed, passed, arriving

    def begin(ins, outs, sems):
        for cp in copies(ins, outs, sems)[0]:
            cp.start()

    def forward(ins, outs, sems):
        _, relayed, passed, _ = copies(ins, outs, sems)
        for i, (got, onward) in enumerate(passed):
            got.wait_recv()
            if i % 4 < 2:
                relayed[2 * (i // 4) + i % 4].start()
            onward.start()

    def finish(ins, outs, sems):
        sent, relayed, passed, arriving = copies(ins, outs, sems)
        for cp in arriving:
            cp.wait_recv()
        for cp in sent + relayed + [onward for _, onward in passed]:
            cp.wait_send()

    return _Rider(shards, [jax.ShapeDtypeStruct((s.shape[0], N_CHIPS) + s.shape[1:], s.dtype) for s in shards],
                  [pltpu.SemaphoreType.DMA((kinds * n,)), pltpu.SemaphoreType.DMA((kinds * n,))],
                  [(0, begin), (0.5, forward), (1, finish)])


def _with_own(gathered, shard, slot):
    return lax.dynamic_update_slice(gathered, shard[:, None], (0, slot, 0, 0, 0))


def _exchange_rider(arrays, out_shape, n_copies, transfers, n_local=0):
    def copies(ins, outs, sems):
        send_sems, recv_sems, local_sems = sems
        remote, local = transfers(ins, outs)
        return ([_remote(src, dst, send_sems, recv_sems, i, dev) for i, (src, dst, dev) in enumerate(remote)],
                [pltpu.make_async_copy(src, dst, local_sems.at[i]) for i, (src, dst) in enumerate(local)])

    def begin(ins, outs, sems):
        remote, local = copies(ins, outs, sems)
        for cp in remote + local:
            cp.start()

    def finish(ins, outs, sems):
        remote, local = copies(ins, outs, sems)
        for cp in remote:
            cp.wait_recv()
        for cp in remote:
            cp.wait_send()
        for cp in local:
            cp.wait()

    return _Rider(arrays, out_shape,
                  [pltpu.SemaphoreType.DMA((n_copies,)), pltpu.SemaphoreType.DMA((n_copies,)),
                   pltpu.SemaphoreType.DMA((max(n_local, 1),))],
                  [(0, begin), (1, finish)])


def _pair_rider(partials):
    def transfers(ins, outs):
        x, y, c = _position()
        return [(ins[a].at[:, :, 1 - c], outs[a], (x, y, 1 - c)) for a in range(len(partials))], []

    shapes = [jax.ShapeDtypeStruct(g.shape[:2] + g.shape[3:], g.dtype) for g in partials]
    return _exchange_rider(partials, shapes, len(partials), transfers)


def _chips_rider(sums):
    n = len(sums)

    def transfers(ins, outs):
        x, y, c = _position()
        return [(ins[a].at[:, 2 * cx + cy], outs[a].at[:, k], (cx, cy, c))
                for k, (cx, cy) in enumerate(_other_chips(x, y)) for a in range(n)], []

    shapes = [jax.ShapeDtypeStruct((q.shape[0], 3) + q.shape[2:], q.dtype) for q in sums]
    return _exchange_rider(sums, shapes, 3 * n, transfers)


def _share_rider(halves):
    def transfers(ins, outs):
        x, y, c = _position()
        return [(ins[a], outs[a], (x, y, 1 - c)) for a in range(len(halves))], []

    return _exchange_rider(halves, [jax.ShapeDtypeStruct(h.shape, h.dtype) for h in halves], len(halves), transfers)


def _small_rider(pack):
    flips = [(fx, fy, fc) for fx in (0, 1) for fy in (0, 1) for fc in (0, 1)][1:]

    def transfers(ins, outs):
        x, y, c = _position()
        slot = outs[0].at[4 * x + 2 * y + c]
        flip = lambda v, f: v + f - 2 * v * f
        return [(ins[0], slot, (flip(x, fx), flip(y, fy), flip(c, fc))) for fx, fy, fc in flips], []

    return _exchange_rider([pack], [jax.ShapeDtypeStruct((8,) + pack.shape, pack.dtype)], len(flips), transfers)


IN_HBM = pl.BlockSpec(memory_space=pltpu.HBM)
IN_SEM = pl.BlockSpec(memory_space=pltpu.SEMAPHORE)
SPLIT_COPY = pltpu.CompilerParams(has_side_effects=pltpu.SideEffectType.DATAFLOW_SIDE_EFFECTING)


def _start_copies(sources, landing_shapes, n_copies, plan, name):
    n_src, n_land = len(sources), len(landing_shapes)

    def body(*refs):
        srcs, lands, outs = refs[:n_src], refs[n_src:n_src + n_land], refs[n_src + n_land:]
        send, recv, token = outs[:n_copies], outs[n_copies:2 * n_copies], outs[-1]
        for i, (src, dst, device) in enumerate(plan(srcs, lands)):
            pltpu.make_async_remote_copy(src_ref=src, dst_ref=dst, send_sem=send[i], recv_sem=recv[i], device_id=device,
                                         device_id_type=MESH).start()
        token[...] = jnp.zeros_like(token)

    arrays = [pltpu.with_memory_space_constraint(a, pltpu.HBM) for a in sources]
    arrays += [pltpu.with_memory_space_constraint(lax.empty(s.shape, s.dtype), pltpu.HBM) for s in landing_shapes]
    res = pl.pallas_call(
        body, name=name,
        out_shape=[pltpu.SemaphoreType.DMA(())] * (2 * n_copies) + [pltpu.HBM(a.shape, a.dtype) for a in arrays]
        + [jax.ShapeDtypeStruct((8, 128), F32)],
        in_specs=[IN_HBM] * len(arrays),
        out_specs=[IN_SEM] * (2 * n_copies) + [IN_HBM] * len(arrays) + [pl.BlockSpec(memory_space=pltpu.VMEM)],
        input_output_aliases={i: 2 * n_copies + i for i in range(len(arrays))},
        compiler_params=SPLIT_COPY,
    )(*arrays)
    sems, rest = res[:2 * n_copies], res[2 * n_copies:]
    return sems, rest[:n_src], rest[n_src:n_src + n_land], rest[-1]


def _wait_copies(started, n_copies, plan, after, name):
    sems, sources, landings, _ = started
    n_src, n_land = len(sources), len(landings)

    def body(*refs):
        srcs, lands = refs[:n_src], refs[n_src:n_src + n_land]
        sem_refs = refs[n_src + n_land:n_src + n_land + 2 * n_copies]
        send, recv = sem_refs[:n_copies], sem_refs[n_copies:]
        for i, (src, dst, device) in enumerate(plan(srcs, lands)):
            cp = pltpu.make_async_remote_copy(src_ref=src, dst_ref=dst, send_sem=send[i], recv_sem=recv[i], device_id=device,
                                              device_id_type=MESH)
            cp.wait_send()
            cp.wait_recv()

    arrays = list(sources) + list(landings)
    res = pl.pallas_call(
        body, name=name, out_shape=[pltpu.HBM(a.shape, a.dtype) for a in arrays],
        in_specs=[IN_HBM] * len(arrays) + [IN_SEM] * (2 * n_copies) + [ANY], out_specs=[IN_HBM] * len(arrays),
        input_output_aliases={i: i for i in range(len(arrays))},
        compiler_params=SPLIT_COPY,
    )(*arrays, *sems, after)
    return res[n_src:]


def _chips_plan(n):
    def plan(srcs, lands):
        x, y, c = _position()
        return [(srcs[a].at[:, 2 * cx + cy], lands[a].at[:, k], (cx, cy, c))
                for k, (cx, cy) in enumerate(_other_chips(x, y)) for a in range(n)]
    return plan


def _alone(rider, name, after=()):
    return _hosted([rider], lambda *refs: None, name=name, in_specs=[ANY] * len(after), out_specs=[], out_shape=[])(*after)[1][0]


def _add_pair(mine, theirs, c, tag):
    n = len(mine)

    def body(c_ref, *refs):
        for a in range(n):
            refs[2 * n + a][...] = (refs[2 * a][...].astype(F32) + refs[2 * a + 1][...].astype(F32)).astype(BF16)

    in_specs, out_specs = [], []
    for got in theirs:
        l, _, hr, cols = got.shape
        in_specs += [pl.BlockSpec((l, None, None, hr, cols), lambda j, s: (0, j, s[0], 0, 0)),
                     pl.BlockSpec((l, None, hr, cols), lambda j, s: (0, j, 0, 0))]
        out_specs.append(pl.BlockSpec((l, None, hr, cols), lambda j, s: (0, j, 0, 0)))
    return pl.pallas_call(
        body, name=f"add_pair_{tag}",
        grid_spec=pltpu.PrefetchScalarGridSpec(num_scalar_prefetch=1, grid=(N_CHIPS,), in_specs=in_specs, out_specs=out_specs),
        out_shape=[jax.ShapeDtypeStruct(got.shape, BF16) for got in theirs],
        compiler_params=_params(("parallel",)),
    )(c.reshape(1), *[a for pair in zip(mine, theirs) for a in pair])


def _add_chips(parts, received, mine, tag):
    n = len(parts)

    def body(j_ref, *refs):
        for a in range(n):
            acc = refs[2 * a][...].astype(F32)
            for k in range(3):
                acc += refs[2 * a + 1][:, k].astype(F32)
            refs[2 * n + a][...] = acc

    in_specs, out_specs, out_shape = [], [], []
    for got in received:
        l, _, hr, cols = got.shape
        in_specs += [pl.BlockSpec((l, None, hr // 2, cols), lambda i, s: (0, s[0], i, 0)),
                     pl.BlockSpec((l, 3, hr // 2, cols), lambda i, s: (0, 0, i, 0))]
        out_specs.append(pl.BlockSpec((l, hr // 2, cols), lambda i, s: (0, i, 0)))
        out_shape.append(jax.ShapeDtypeStruct((l, hr, cols), F32))
    return pl.pallas_call(
        body, name=f"add_chips_{tag}",
        grid_spec=pltpu.PrefetchScalarGridSpec(num_scalar_prefetch=1, grid=(2,), in_specs=in_specs, out_specs=out_specs),
        out_shape=out_shape,
        compiler_params=_params(("parallel",)),
    )(mine.reshape(1), *[a for pair in zip(parts, received) for a in pair])


def _adam_update(w, g, m, v):
    m2 = ADAM_B1 * m + (1.0 - ADAM_B1) * g
    v2 = ADAM_B2 * v + (1.0 - ADAM_B2) * jnp.square(g)
    m_hat = m2 / (1.0 - ADAM_B1 ** ADAM_STEP)
    v_hat = v2 / (1.0 - ADAM_B2 ** ADAM_STEP)
    return -ADAM_LR * (m_hat / (jnp.sqrt(v_hat) + ADAM_EPS) + ADAM_WD * w), m2, v2


def _adamw_group(items, tag, after=()):
    n = len(items)

    def body(*refs):
        ins, outs = refs[:5 * n], refs[5 * n + len(after):]
        mine = pl.program_id(0) == lax.axis_index("c")
        for a in range(n):
            w_ref, own_ref, other_ref, m_ref, v_ref = ins[5 * a:5 * a + 5]
            g_ref, d_ref, nm_ref, nv_ref = outs[4 * a:4 * a + 4]
            gv = jnp.where(mine, own_ref[...], other_ref[...])
            g_ref[...] = gv
            d_ref[...], nm_ref[...], nv_ref[...] = _adam_update(w_ref[...], gv, m_ref[...], v_ref[...])

    in_specs, out_specs, out_shape, args = [], [], [], []
    for w, own, other, m, v in items:
        _, hr, cols = w.shape
        tr = hr // ADAM_BLOCKS
        full = pl.BlockSpec((None, tr, cols), lambda h, i: (h, i, 0))
        half = pl.BlockSpec((tr, cols), lambda h, i: (i, 0))
        in_specs += [full, half, half, full, full]
        out_specs += [full] * 4
        out_shape += [jax.ShapeDtypeStruct((2, hr, cols), F32)] * 4
        args += [w, own, other, m, v]
    outs = pl.pallas_call(body, name=f"adamw_{tag}", grid=(2, ADAM_BLOCKS), in_specs=in_specs + [ANY] * len(after),
                          out_specs=out_specs, out_shape=out_shape,
                          compiler_params=_params(("parallel", "parallel")))(*args, *after)
    return [outs[4 * a:4 * a + 4] for a in range(n)]


def _adamw_small(w, gathered, m, v, shapes):
    n_rows = w.shape[0]
    places = []
    for i, name in enumerate(VECTOR_PARAMS):
        places.append((name, i * TILE_ROWS, 1 if len(shapes[name]) == 1 else shapes[name][0], shapes[name][-1]))
    places.append(("pool_w", len(VECTOR_PARAMS) * TILE_ROWS, n_rows - len(VECTOR_PARAMS) * TILE_ROWS, D_MODEL))

    def body(w_ref, g_ref, m_ref, v_ref, loss_ref, *rest):
        outs, (sum_scr, d_scr, nm_scr, nv_scr) = rest[:-4], rest[-4:]
        total = g_ref[0]
        for i in range(1, g_ref.shape[0]):
            total += g_ref[i]
        sum_scr[...] = total
        gv = sum_scr[0:n_rows, :]
        d_scr[...], nm_scr[...], nv_scr[...] = _adam_update(w_ref[...], gv, m_ref[...], v_ref[...])
        loss_ref[...] = sum_scr[n_rows:n_rows + 1, 0:1]
        for k, (_, first, rows, cols) in enumerate(places):
            for j, scr in enumerate((sum_scr, d_scr, nm_scr, nv_scr)):
                outs[4 * k + j][...] = scr[first:first + rows, 0:cols]

    out_shape = [jax.ShapeDtypeStruct((1, 1), F32)]
    for _, _, rows, cols in places:
        out_shape += [jax.ShapeDtypeStruct((rows, cols), F32)] * 4
    res = pl.pallas_call(
        body, name="adamw_small", out_shape=out_shape,
        scratch_shapes=[pltpu.VMEM(gathered.shape[1:], F32)] + [pltpu.VMEM(w.shape, F32)] * 3,
        compiler_params=_params())(w, gathered, m, v)
    return res[0], {name: res[1 + 4 * k:5 + 4 * k] for k, (name, _, _, _) in enumerate(places)}


VECTOR_PARAMS = ("ffn1_norm", "mix_norm", "hgrn_lb", "hgrn_onorm", "ffn2_norm", "ple_norm", "ple_post_norm", "final_norm",
                 "pool_scale")
ALL_PARAMS = ("ffn1_norm", "ffn1_w1", "ffn1_w3", "ffn1_w2", "mix_norm", "w_in", "hgrn_lb", "hgrn_onorm", "w_branch_a",
              "pool_w", "pool_scale", "w_branch_b", "w_out", "ffn2_norm", "ffn2_w1", "ffn2_w3", "ffn2_w2", "ple_norm",
              "ple_w_gate", "ple_w_proj", "ple_post_norm", "final_norm")
TILE_ROWS = 8


def _pack_small(values, loss=None):
    tile = lambda a: jnp.pad(a, ((0, TILE_ROWS - a.shape[0]), (0, D_MODEL - a.shape[1])))
    parts = [tile(values[name].reshape(-1, values[name].shape[-1])) for name in VECTOR_PARAMS]
    parts.append(values["pool_w"].reshape(-1, D_MODEL))
    if loss is not None:
        parts.append(tile(loss))
    return jnp.concatenate(parts, axis=0)


def _halved(a, lead):
    return a.reshape(lead, 2, -1, a.shape[-1])


def _shard_halves(a, lead):
    return a.reshape(lead, N_CHIPS, 2, -1, a.shape[-1])


REDUCED_TRANSPOSED = ("ffn1_w1", "ffn1_w3", "ffn2_w1", "ffn2_w3")


def _entries(arrays):
    return [a[i] for a in arrays for i in range(a.shape[0])]


def _adam_items(names, own, other, w, m, v):
    items = []
    for name, g_own, g_other in zip(names, _entries(own), _entries(other)):
        view = (lambda a: _halved(a[0].T, 1)[0]) if name in REDUCED_TRANSPOSED else (lambda a: _halved(a, 1)[0])
        items.append((view(w[name]), g_own, g_other, view(m[name]), view(v[name])))
    return items


def _adam_store(names, results, w, out):
    for name, res in zip(names, results):
        shape = w[name].shape
        if name in REDUCED_TRANSPOSED:
            back = [a.reshape(shape[2], shape[1]).T.reshape(shape) for a in res]
        else:
            back = [a.reshape(shape) for a in res]
        out["grad"][name], out["delta"][name], out["new_m"][name], out["new_v"][name] = back


def kernel(x, p, ffn1_norm, ffn1_w1, ffn1_w3, ffn1_w2, mix_norm, w_in, hgrn_lb, hgrn_onorm, w_branch_a, pool_w, pool_scale, w_branch_b, w_out, ffn2_norm, ffn2_w1, ffn2_w3, ffn2_w2, ple_norm, ple_w_gate, ple_w_proj, ple_post_norm, final_norm, loss_target, m_ffn1_norm, m_ffn1_w1, m_ffn1_w3, m_ffn1_w2, m_mix_norm, m_w_in, m_hgrn_lb, m_hgrn_onorm, m_w_branch_a, m_pool_w, m_pool_scale, m_w_branch_b, m_w_out, m_ffn2_norm, m_ffn2_w1, m_ffn2_w3, m_ffn2_w2, m_ple_norm, m_ple_w_gate, m_ple_w_proj, m_ple_post_norm, m_final_norm, v_ffn1_norm, v_ffn1_w1, v_ffn1_w3, v_ffn1_w2, v_mix_norm, v_w_in, v_hgrn_lb, v_hgrn_onorm, v_w_branch_a, v_pool_w, v_pool_scale, v_w_branch_b, v_w_out, v_ffn2_norm, v_ffn2_w1, v_ffn2_w3, v_ffn2_w2, v_ple_norm, v_ple_w_gate, v_ple_w_proj, v_ple_post_norm, v_final_norm):
    args = dict(locals())
    w = {name: args[name] for name in ALL_PARAMS}
    m = {name: args["m_" + name] for name in ALL_PARAMS}
    v = {name: args["v_" + name] for name in ALL_PARAMS}
    cx, cy, cc = _position()
    chip = (2 * cx + cy).astype(jnp.int32)
    core = cc.astype(jnp.int32)
    xs, ps, target = x[0], p[0, 0], loss_target[0]
    t = xs.shape[0]
    tm = min(256, t)
    tm_ffn = min(512, t)
    tt = min(512, t)
    tk = min(2048, t)
    small = {name: w[name] for name in VECTOR_PARAMS}
    small["final_norm"] = w["final_norm"].reshape(1, D_MODEL)
    pool_w0 = w["pool_w"][0]

    ffn_shard = lambda i: _halved(jnp.stack([w[f"ffn{i}_w1"][0].T, w[f"ffn{i}_w3"][0].T, w[f"ffn{i}_w2"][0]]).astype(BF16), 3)
    sq_shard = _halved(jnp.stack([w["w_branch_a"][0], w["w_out"][0], w["ple_w_gate"][0]]).astype(BF16), 3)
    win_shard, wb_shard, wp_shard = (_halved(w[n].astype(BF16), 1) for n in ("w_in", "w_branch_b", "ple_w_proj"))

    ffn1_shard, ffn2_shard = ffn_shard(1), ffn_shard(2)
    (ffn1w,) = _alone(_relay_gather_rider([ffn1_shard]), "gather_ffn1")
    ffn1w = _with_own(ffn1w, ffn1_shard, chip).reshape(3, D_FF, D_MODEL)
    (x1, a1, b1), ((winw,),) = _ffn_fwd(xs, small["ffn1_norm"], ffn1w, 1, tm_ffn, [_gather_rider([win_shard], 0.6)])
    winw = _with_own(winw, win_shard, chip).reshape(N_CHIPS, D_MODEL, SHARD_IN_COLS)
    (main, pool_r, gates), ((ffn2w,),) = _mix_fwd(x1, small["mix_norm"], winw, tm, [_gather_rider([ffn2_shard], 0.75)])
    ffn2w = _with_own(ffn2w, ffn2_shard, chip).reshape(3, D_FF, D_MODEL)
    (o, states), ((sqw, wbw, wpw),) = _hgrn_fwd(main, small["hgrn_lb"], tt,
                                                 [_gather_rider([sq_shard, wb_shard, wp_shard], 0.5)])
    sqw = _with_own(sqw, sq_shard, chip).reshape(3, D_MODEL, D_MODEL)
    wbw = _with_own(wbw, wb_shard, chip).reshape(N_CHIPS, POOL_WIDTH, -1)
    wpw = _with_own(wpw, wp_shard, chip).reshape(N_CHIPS, ps.shape[1], -1)
    (x2, ya, yb, pooled), _ = _post_fwd(o, main, pool_r, gates, x1, small["hgrn_onorm"], pool_w0, small["pool_scale"], sqw,
                                       wbw, tm)
    (x3, a2, b2), _ = _ffn_fwd(x2, small["ffn2_norm"], ffn2w, 2, tm_ffn)
    (dx3, loss, d_wg, d_wp, d_ple, d_post, d_final), _ = _tail(
        x3, ps, target, small["ple_norm"], small["ple_post_norm"], small["final_norm"], sqw, wpw, tm_ffn)

    add_pairs = lambda parts, got, group: _add_pair(parts, got, core, group)
    add_chips = lambda sums, got, group: _add_chips(sums, got, chip, group)
    names1 = ("ffn2_w1", "ffn2_w3", "ffn2_w2", "ple_w_gate", "ple_w_proj")
    names2 = ("w_branch_a", "w_out", "w_branch_b")
    names3 = ("w_in",)
    names4 = ("ffn1_w1", "ffn1_w3")
    names5 = ("ffn1_w2",)
    tags1, tags2, tags3, tags4, tags5 = "ffn2", "branches", "w_in", "ffn1_in", "ffn1_out"

    (dx2, dab2, s2, h3, dxh2, d_ffn2_norm), _ = _ffn_bwd(dx3, x2, small["ffn2_norm"], a2, b2, ffn2w, 2, tm)
    (d_w13_2,), _ = _wgrad(dab2, h3, WGRAD_IN_BLOCKS, "wgrad_ffn2_in", tk)
    (d_w2_2,), _ = _wgrad(s2, dxh2, WGRAD_OUT_BLOCKS, "wgrad_ffn2_out", tk)
    part1 = [_shard_halves(d_w13_2, 2), _shard_halves(d_w2_2, 1), _shard_halves(d_wg, 1), _shard_halves(d_wp, 1)]
    (do, dog, du, dgates, d_wa, d_wout, d_wb, d_pool_w, d_pool_scale, d_onorm), (sib1,) = _post_bwd(
        dx2, o, main, gates, ya, yb, pooled, small["hgrn_onorm"], pool_w0, small["pool_scale"], sqw, wbw, tm,
        [_pair_rider(part1)])
    sums1 = add_pairs(part1, sib1, tags1)
    part2 = [_shard_halves(d_wa, 1), _shard_halves(d_wout, 1), _shard_halves(d_wb, 1)]
    (dqfi, d_lb), (got1, sib2) = _hgrn_bwd(main, small["hgrn_lb"], states, do, tt, [_chips_rider(sums1), _pair_rider(part2)])
    own1 = add_chips(sums1, got1, tags1)
    sums2 = add_pairs(part2, sib2, tags2)
    (dx1, dproj, h2, d_mix_norm), (other1, got2) = _mix_bwd(dqfi, dog, du, dgates, dx2, x1, small["mix_norm"], winw, tm,
                                                            [_share_rider(own1), _chips_rider(sums2)])
    own2 = add_chips(sums2, got2, tags2)
    (d_win,), (other2,) = _wgrad_cols(h2, dproj, N_CHIPS, "wgrad_in", tk, [_share_rider(own2)])
    part3 = [_shard_halves(d_win, 1)]
    (dx, dab1, s1, h1, dxh1, d_ffn1_norm), _ = _ffn_bwd(dx1, xs, small["ffn1_norm"], a1, b1, ffn1w, 1, tm)
    vecs = dict(ffn1_norm=d_ffn1_norm, mix_norm=d_mix_norm, hgrn_lb=d_lb, hgrn_onorm=d_onorm, ffn2_norm=d_ffn2_norm,
                ple_norm=d_ple, ple_post_norm=d_post, final_norm=d_final, pool_scale=d_pool_scale, pool_w=d_pool_w)
    small_pack = _pack_small(vecs, loss)
    (d_w2_1,), (sib3, (small_all,)) = _wgrad(s1, dxh1, WGRAD_OUT_BLOCKS, "wgrad_ffn1_out", tk,
                                             [_pair_rider(part3), _small_rider(small_pack)])
    small_all = lax.dynamic_update_slice(small_all, small_pack[None], (2 * chip + core, 0, 0))
    sums3 = add_pairs(part3, sib3, tags3)
    part5 = [_shard_halves(d_w2_1, 1)]
    (d_w13_1,), (got3, sib5) = _wgrad(dab1, h1, WGRAD_IN_BLOCKS, "wgrad_ffn1_in", tk,
                                      [_chips_rider(sums3), _pair_rider(part5)])
    own3 = add_chips(sums3, got3, tags3)
    sums5 = add_pairs(part5, sib5, tags5)
    part4 = [_shard_halves(d_w13_1, 2)]
    landing = lambda a: jax.ShapeDtypeStruct((a.shape[0], 3) + a.shape[2:], a.dtype)

    def sibling_plan(srcs, lands):
        x, y, c = _position()
        return [(srcs[-1], lands[-1], (x, y, 1 - c))]

    plan_a = lambda srcs, lands: _chips_plan(1)(srcs[:1], lands[:1]) + sibling_plan(srcs, lands)
    started_a = _start_copies([sums5[0], own3[0]], [landing(sums5[0]), own3[0]], 4, plan_a, "start_a")
    sib4 = _alone(_pair_rider(part4), "pair_last", [started_a[3]])
    sums4 = add_pairs(part4, sib4, tags4)
    started_b = _start_copies([sums4[0]], [landing(sums4[0])], 3, _chips_plan(1), "start_b")

    out = dict(grad={}, delta={}, new_m={}, new_v={})
    results = _adamw_group(_adam_items(names1 + names2, own1 + own2, other1 + other2, w, m, v), "early",
                           [started_a[3], started_b[3]])
    _adam_store(names1 + names2, results, w, out)
    got5, other3 = _wait_copies(started_a, 4, plan_a, results[0][1], "wait_a")
    results = _adamw_group(_adam_items(names3, own3, [other3], w, m, v), "w_in")
    _adam_store(names3, results, w, out)
    (got4,) = _wait_copies(started_b, 3, _chips_plan(1), results[0][1], "wait_b")
    own4 = add_chips(sums4, [got4], tags4)
    own5 = add_chips(sums5, [got5], tags5)
    other4, other5 = _alone(_share_rider(own4 + own5), "share_last")
    results = _adamw_group(_adam_items(names4 + names5, own4 + own5, [other4, other5], w, m, v), "ffn1")
    _adam_store(names4 + names5, results, w, out)

    shapes = {name: w[name].shape for name in VECTOR_PARAMS + ("pool_w",)}
    loss, results = _adamw_small(_pack_small(w), small_all, _pack_small(m), _pack_small(v), shapes)
    for name, res in results.items():
        out["grad"][name], out["delta"][name], out["new_m"][name], out["new_v"][name] = (a.reshape(shapes[name]) for a in res)

    return (loss[0, 0], dx[None], *[out["grad"][n] for n in ALL_PARAMS], *[out["delta"][n] for n in ALL_PARAMS],
            *[out["new_m"][n] for n in ALL_PARAMS], *[out["new_v"][n] for n in ALL_PARAMS])
```

```python
import functools

import jax
import jax.numpy as jnp
from jax import lax
from jax.experimental import pallas as pl
from jax.experimental.pallas import tpu as pltpu

F32 = jnp.float32
BF16 = jnp.bfloat16
MESH = pl.DeviceIdType.MESH

D_MODEL = 1024
D_FF = 2816
HEADS = 8
HEAD_DIM = 128
POOL_WIDTH = 512
POOL_WINDOWS = (2, 4, 8, 16)
POOL_HALO = 16
N_CHIPS = 4
EPS = 1e-6
CHUNK = 64
MAIN_COLS = 4096
GATE_COLS = 2048
SHARD_IN_COLS = 1664

ADAM_LR = 0.001
ADAM_B1 = 0.9
ADAM_B2 = 0.999
ADAM_EPS = 1e-08
ADAM_WD = 0.01
ADAM_STEP = 10

VMEM_LIMIT = 56 * 1024 * 1024
WGRAD_IN_BLOCKS = 4
WGRAD_OUT_BLOCKS = 2
ADAM_BLOCKS = 4


def _params(semantics=None, vmem=VMEM_LIMIT):
    return pltpu.CompilerParams(dimension_semantics=semantics, vmem_limit_bytes=vmem)


def _dot(a, b):
    return jnp.dot(a, b, preferred_element_type=F32)


def _dot_nt(a, b):
    return lax.dot_general(a, b, (((1,), (1,)), ((), ())), preferred_element_type=F32)


def _dot_tn(a, b):
    return lax.dot_general(a, b, (((0,), (0,)), ((), ())), preferred_element_type=F32)


def _tri_sum(tri, x):
    hi = x.astype(BF16)
    lo = (x - hi.astype(F32)).astype(BF16)
    return _dot(tri, hi) + _dot(tri, lo)


def _sigmoid(x):
    return jax.nn.sigmoid(x)


def _resident(shape):
    zeros = (0,) * len(shape)
    return pl.BlockSpec(shape, lambda *_: zeros, pipeline_mode=pl.Buffered(1))


def _pick(shape, k):
    zeros = (0,) * (len(shape) - 1)
    return pl.BlockSpec((None,) + tuple(shape[1:]), lambda *_: (k,) + zeros, pipeline_mode=pl.Buffered(1))


def _rows(tm, cols, col_block=0):
    return pl.BlockSpec((tm, cols), lambda i: (i, col_block))


def _acc(shape):
    zeros = (0,) * len(shape)
    return pl.BlockSpec(shape, lambda *_: zeros)


def _rms(x):
    r = lax.rsqrt(jnp.mean(x * x, axis=-1, keepdims=True) + EPS)
    return r, x * r


def _rms_bwd(dn, n, r):
    return r * (dn - n * jnp.mean(dn * n, axis=-1, keepdims=True))


def _colsum(a):
    return jnp.sum(a, axis=0, keepdims=True)


ANY = pl.BlockSpec(memory_space=pl.ANY)


class _Rider:
    def __init__(self, inputs, out_shape, sems, phases):
        self.inputs, self.out_shape, self.sems, self.phases = list(inputs), list(out_shape), list(sems), list(phases)


def _hosted(riders, body, *, name, grid=(), in_specs, out_specs, out_shape, scratch_shapes=(), compiler_params=None):
    riders = [r for r in riders if r is not None]
    n_in, n_out, n_scr = len(in_specs), len(out_shape), len(scratch_shapes)
    n_steps = 1
    for g in grid:
        n_steps *= g

    def wrapped(*refs):
        pos = n_in
        ins = refs[:n_in]
        r_ins = []
        for r in riders:
            r_ins.append(refs[pos:pos + len(r.inputs)])
            pos += len(r.inputs)
        outs = refs[pos:pos + n_out]
        pos += n_out
        r_outs = []
        for r in riders:
            r_outs.append(refs[pos:pos + len(r.out_shape)])
            pos += len(r.out_shape)
        scr = refs[pos:pos + n_scr]
        pos += n_scr
        r_sems = []
        for r in riders:
            r_sems.append(refs[pos:pos + len(r.sems)])
            pos += len(r.sems)
        step = 0
        for axis in range(len(grid)):
            step = step * grid[axis] + pl.program_id(axis)

        def at_step(which, fn):
            if n_steps == 1:
                fn()
            else:
                pl.when(step == which)(fn)

        for r, ri, ro, rs in zip(riders, r_ins, r_outs, r_sems):
            for fraction, fn in r.phases:
                if fraction == 0:
                    at_step(0, functools.partial(fn, ri, ro, rs))
        body(*ins, *outs, *scr)
        for r, ri, ro, rs in zip(riders, r_ins, r_outs, r_sems):
            for fraction, fn in r.phases:
                if fraction > 0:
                    at_step(min(int(fraction * n_steps), n_steps - 1), functools.partial(fn, ri, ro, rs))

    call = pl.pallas_call(
        wrapped, name=name, grid=grid,
        in_specs=list(in_specs) + [ANY for r in riders for _ in r.inputs],
        out_specs=list(out_specs) + [ANY for r in riders for _ in r.out_shape],
        out_shape=list(out_shape) + [s for r in riders for s in r.out_shape],
        scratch_shapes=list(scratch_shapes) + [s for r in riders for s in r.sems],
        compiler_params=compiler_params)

    def run(*args):
        res = call(*args, *[a for r in riders for a in r.inputs])
        extras, pos = [], n_out
        for r in riders:
            extras.append(list(res[pos:pos + len(r.out_shape)]))
            pos += len(r.out_shape)
        return list(res[:n_out]), extras

    return run


def _ffn_fwd(x, g, ffnw, tag, tm, riders=()):
    t = x.shape[0]

    def body(x_ref, g_ref, w1_ref, w3_ref, w2_ref, xo_ref, a_ref, b_ref):
        xv = x_ref[...]
        _, n = _rms(xv)
        h = (n * g_ref[...]).astype(BF16)
        a = _dot_nt(h, w1_ref[...])
        b = _dot_nt(h, w3_ref[...])
        s = (a * _sigmoid(a) * b).astype(BF16)
        xo_ref[...] = xv + 0.5 * _dot(s, w2_ref[...])
        a_ref[...] = a.astype(BF16)
        b_ref[...] = b.astype(BF16)

    return _hosted(
        riders, body, name=f"ffn_fwd_{tag}", grid=(t // tm,),
        in_specs=[_rows(tm, D_MODEL), _resident((1, D_MODEL)), _pick(ffnw.shape, 0), _pick(ffnw.shape, 1),
                  _pick(ffnw.shape, 2)],
        out_specs=[_rows(tm, D_MODEL), _rows(tm, D_FF), _rows(tm, D_FF)],
        out_shape=[jax.ShapeDtypeStruct((t, D_MODEL), F32), jax.ShapeDtypeStruct((t, D_FF), BF16),
                   jax.ShapeDtypeStruct((t, D_FF), BF16)],
        compiler_params=_params(("arbitrary",)),
    )(x, g, ffnw, ffnw, ffnw)


def _ffn_bwd(dxo, x, g, a, b, ffnw, tag, tm, riders=()):
    t = x.shape[0]

    def body(dxo_ref, x_ref, g_ref, a_ref, b_ref, w1_ref, w3_ref, w2_ref, dx_ref, dab_ref, s_ref, h_ref, dxh_ref, dg_ref):
        @pl.when(pl.program_id(0) == 0)
        def _():
            dg_ref[...] = jnp.zeros_like(dg_ref)

        xv = x_ref[...]
        gv = g_ref[...]
        r, n = _rms(xv)
        h_ref[...] = (n * gv).astype(BF16)
        dxo_v = dxo_ref[...]
        dxh = (0.5 * dxo_v).astype(BF16)
        dxh_ref[...] = dxh
        ds = _dot_nt(dxh, w2_ref[...])
        av = a_ref[...].astype(F32)
        bv = b_ref[...].astype(F32)
        sg = _sigmoid(av)
        silu = av * sg
        s_ref[...] = (silu * bv).astype(BF16)
        da = (ds * bv * (sg * (1.0 + av * (1.0 - sg)))).astype(BF16)
        db = (ds * silu).astype(BF16)
        dab_ref[:, :D_FF] = da
        dab_ref[:, D_FF:] = db
        dh = _dot(da, w1_ref[...]) + _dot(db, w3_ref[...])
        dg_ref[...] += _colsum(dh * n)
        dx_ref[...] = dxo_v + _rms_bwd(dh * gv, n, r)

    return _hosted(
        riders, body, name=f"ffn_bwd_{tag}", grid=(t // tm,),
        in_specs=[_rows(tm, D_MODEL), _rows(tm, D_MODEL), _resident((1, D_MODEL)), _rows(tm, D_FF), _rows(tm, D_FF),
                  _pick(ffnw.shape, 0), _pick(ffnw.shape, 1), _pick(ffnw.shape, 2)],
        out_specs=[_rows(tm, D_MODEL), _rows(tm, 2 * D_FF), _rows(tm, D_FF), _rows(tm, D_MODEL), _rows(tm, D_MODEL),
                   _acc((1, D_MODEL))],
        out_shape=[jax.ShapeDtypeStruct((t, D_MODEL), F32), jax.ShapeDtypeStruct((t, 2 * D_FF), BF16),
                   jax.ShapeDtypeStruct((t, D_FF), BF16), jax.ShapeDtypeStruct((t, D_MODEL), BF16),
                   jax.ShapeDtypeStruct((t, D_MODEL), BF16), jax.ShapeDtypeStruct((1, D_MODEL), F32)],
        compiler_params=_params(("arbitrary",)),
    )(dxo, x, g, a, b, ffnw, ffnw, ffnw)


def _wgrad_body(n_token_tiles):
    def body(x_ref, dy_ref, o_ref, acc):
        k = pl.program_id(1)

        @pl.when(k == 0)
        def _():
            acc[...] = jnp.zeros_like(acc)

        acc[...] += _dot_tn(x_ref[...], dy_ref[...])

        @pl.when(k == n_token_tiles - 1)
        def _():
            o_ref[...] = acc[...].astype(BF16)

    return body


def _wgrad(xm, dy, out_blocks, name, tk, riders=()):
    t, m = xm.shape
    n = dy.shape[1]
    mb = m // out_blocks

    return _hosted(
        riders, _wgrad_body(t // tk), name=name, grid=(out_blocks, t // tk),
        in_specs=[pl.BlockSpec((tk, mb), lambda j, k: (k, j)), pl.BlockSpec((tk, n), lambda j, k: (k, 0))],
        out_specs=[pl.BlockSpec((None, mb, n), lambda j, k: (j, 0, 0))],
        out_shape=[jax.ShapeDtypeStruct((out_blocks, mb, n), BF16)],
        scratch_shapes=[pltpu.VMEM((mb, n), F32)],
        compiler_params=_params(("arbitrary", "arbitrary")),
    )(xm, dy)


def _wgrad_cols(xm, dy, out_blocks, name, tk, riders=()):
    t, m = xm.shape
    n = dy.shape[1]
    nb = n // out_blocks

    return _hosted(
        riders, _wgrad_body(t // tk), name=name, grid=(out_blocks, t // tk),
        in_specs=[pl.BlockSpec((tk, m), lambda j, k: (k, 0)), pl.BlockSpec((tk, nb), lambda j, k: (k, j))],
        out_specs=[pl.BlockSpec((None, m, nb), lambda j, k: (j, 0, 0))],
        out_shape=[jax.ShapeDtypeStruct((out_blocks, m, nb), BF16)],
        scratch_shapes=[pltpu.VMEM((m, nb), F32)],
        compiler_params=_params(("arbitrary", "arbitrary")),
    )(xm, dy)


def _mix_fwd(x1, g, winw, tm, riders=()):
    t = x1.shape[0]

    def body(x_ref, g_ref, w_ref, main_ref, pool_ref, gate_ref):
        _, n = _rms(x_ref[...])
        h = (n * g_ref[...]).astype(BF16)
        proj = _dot(h, w_ref[...])
        main_ref[...] = proj[:, :MAIN_COLS]
        pool_ref[...] = proj[:, MAIN_COLS:MAIN_COLS + POOL_WIDTH]
        gate_ref[...] = proj[:, MAIN_COLS + POOL_WIDTH:]

    return _hosted(
        riders, body, name="mix_fwd", grid=(t // tm,),
        in_specs=[_rows(tm, D_MODEL), _resident((1, D_MODEL)), _resident(winw.shape)],
        out_specs=[_rows(tm, MAIN_COLS), _rows(tm, POOL_WIDTH), _rows(tm, GATE_COLS)],
        out_shape=[jax.ShapeDtypeStruct((t, MAIN_COLS), F32), jax.ShapeDtypeStruct((t, POOL_WIDTH), F32),
                   jax.ShapeDtypeStruct((t, GATE_COLS), F32)],
        compiler_params=_params(("arbitrary",)),
    )(x1, g, winw)


def _mix_bwd(dqfi, dog, du, dgates, dx2, x1, g, winw, tm, riders=()):
    t = x1.shape[0]
    cols = N_CHIPS * SHARD_IN_COLS

    def body(dqfi_ref, dog_ref, du_ref, dgt_ref, dx2_ref, x_ref, g_ref, w_ref, dx_ref, dproj_ref, h_ref, dg_ref):
        @pl.when(pl.program_id(0) == 0)
        def _():
            dg_ref[...] = jnp.zeros_like(dg_ref)

        dproj = jnp.concatenate([dqfi_ref[...], dog_ref[...], du_ref[...], dgt_ref[...]], axis=1)
        dproj_ref[...] = dproj
        dh = _dot_nt(dproj, w_ref[...])
        gv = g_ref[...]
        r, n = _rms(x_ref[...])
        h_ref[...] = (n * gv).astype(BF16)
        dg_ref[...] += _colsum(dh * n)
        dx_ref[...] = dx2_ref[...] + _rms_bwd(dh * gv, n, r)

    return _hosted(
        riders, body, name="mix_bwd", grid=(t // tm,),
        in_specs=[_rows(tm, 3 * D_MODEL), _rows(tm, D_MODEL), _rows(tm, POOL_WIDTH), _rows(tm, GATE_COLS),
                  _rows(tm, D_MODEL), _rows(tm, D_MODEL), _resident((1, D_MODEL)), _resident(winw.shape)],
        out_specs=[_rows(tm, D_MODEL), _rows(tm, cols), _rows(tm, D_MODEL), _acc((1, D_MODEL))],
        out_shape=[jax.ShapeDtypeStruct((t, D_MODEL), F32), jax.ShapeDtypeStruct((t, cols), BF16),
                   jax.ShapeDtypeStruct((t, D_MODEL), BF16), jax.ShapeDtypeStruct((1, D_MODEL), F32)],
        compiler_params=_params(("arbitrary",)),
    )(dqfi, dog, du, dgates, dx2, x1, g, winw)


def _lower_bound(lb_raw):
    l0 = lb_raw[0:1, :]
    l1 = lb_raw[1:2, :]
    m = jnp.maximum(l0, l1)
    e0 = jnp.exp(l0 - m)
    e1 = jnp.exp(l1 - m)
    return e0 / (e0 + e1)


def _head_slices():
    return [slice(h * HEAD_DIM, (h + 1) * HEAD_DIM) for h in range(HEADS)]


def _gates(qr, fr, lb, tril_b, first_half):
    sg = _sigmoid(fr)
    f = lb + (1.0 - lb) * sg
    k = 1.0 - f
    sq = _sigmoid(qr)
    q = qr * sq
    log_f = jnp.log(f)
    gc = _tri_sum(tril_b, log_f)
    gm = _colsum(jnp.where(first_half, log_f, 0.0))
    gl = _colsum(log_f)
    e_q = jnp.exp(gc - gm)
    e_k = jnp.exp(gm - gc)
    e_in = jnp.exp(gc)
    e_out = jnp.exp(gl - gc)
    return dict(sg=sg, f=f, k=k, sq=sq, q=q, e_q=e_q, e_k=e_k, e_in=e_in, e_out=e_out, e_last=jnp.exp(gl))


def _hgrn_fwd(main, lb_raw, tt, riders=()):
    t = main.shape[0]
    n_local = tt // CHUNK

    def body(q_ref, f_ref, i_ref, lb_ref, o_ref, st_ref, s_scr):
        @pl.when(pl.program_id(0) == 0)
        def _():
            s_scr[...] = jnp.zeros_like(s_scr)

        lb = _lower_bound(lb_ref[...])
        row = lax.broadcasted_iota(jnp.int32, (CHUNK, CHUNK), 0)
        col = lax.broadcasted_iota(jnp.int32, (CHUNK, CHUNK), 1)
        tril = row >= col
        tril_b = tril.astype(BF16)
        first_half = lax.broadcasted_iota(jnp.int32, (CHUNK, D_MODEL), 0) < CHUNK // 2
        heads = _head_slices()

        def chunk(c, carry):
            rows = pl.ds(pl.multiple_of(c * CHUNK, CHUNK), CHUNK)
            z = _gates(q_ref[rows, :], f_ref[rows, :], lb, tril_b, first_half)
            qt = (z["q"] * z["e_q"]).astype(BF16)
            kt = (z["k"] * z["e_k"]).astype(BF16)
            qg = (z["q"] * z["e_in"]).astype(BF16)
            kg = (z["k"] * z["e_out"]).astype(BF16)
            vb = i_ref[rows, :].astype(BF16)
            states = [s_scr[h] for h in range(HEADS)]
            for h in range(HEADS):
                st_ref[c, h] = states[h]
            raw = [_dot_nt(qt[:, sl], kt[:, sl]) for sl in heads]
            inter = [_dot_nt(qg[:, sl], states[h].astype(BF16)) for h, sl in enumerate(heads)]
            grown = [_dot_tn(vb[:, sl], kg[:, sl]) for sl in heads]
            scores = [jnp.where(tril, r, 0.0).astype(BF16) for r in raw]
            for h, sl in enumerate(heads):
                s_scr[h] = states[h] * z["e_last"][:, sl] + grown[h]
            o_ref[rows, :] = jnp.concatenate([_dot(scores[h], vb[:, sl]) + inter[h] for h, sl in enumerate(heads)], axis=1)
            return carry

        lax.fori_loop(0, n_local, chunk, 0, unroll=True)

    return _hosted(
        riders, body, name="hgrn_fwd", grid=(t // tt,),
        in_specs=[_rows(tt, D_MODEL, 0), _rows(tt, D_MODEL, 1), _rows(tt, D_MODEL, 2), _resident((2, D_MODEL))],
        out_specs=[_rows(tt, D_MODEL),
                   pl.BlockSpec((n_local, HEADS, HEAD_DIM, HEAD_DIM), lambda i: (i, 0, 0, 0))],
        out_shape=[jax.ShapeDtypeStruct((t, D_MODEL), F32),
                   jax.ShapeDtypeStruct((t // CHUNK, HEADS, HEAD_DIM, HEAD_DIM), F32)],
        scratch_shapes=[pltpu.VMEM((HEADS, HEAD_DIM, HEAD_DIM), F32)],
        compiler_params=_params(("arbitrary",)),
    )(main, main, main, lb_raw)


def _hgrn_bwd(main, lb_raw, states, do, tt, riders=()):
    t = main.shape[0]
    n_tiles = t // tt
    n_local = tt // CHUNK

    def rev(col_block):
        return pl.BlockSpec((tt, D_MODEL), lambda i: (n_tiles - 1 - i, col_block))

    def body(q_ref, f_ref, i_ref, lb_ref, st_ref, do_ref, dqfi_ref, dlb_ref, ds_scr, acc_scr):
        @pl.when(pl.program_id(0) == 0)
        def _():
            ds_scr[...] = jnp.zeros_like(ds_scr)
            acc_scr[...] = jnp.zeros_like(acc_scr)

        lb = _lower_bound(lb_ref[...])
        row = lax.broadcasted_iota(jnp.int32, (CHUNK, CHUNK), 0)
        col = lax.broadcasted_iota(jnp.int32, (CHUNK, CHUNK), 1)
        tril = row >= col
        tril_b = tril.astype(BF16)
        triu_b = (row <= col).astype(BF16)
        first_half = lax.broadcasted_iota(jnp.int32, (CHUNK, D_MODEL), 0) < CHUNK // 2
        heads = _head_slices()
        cat = functools.partial(jnp.concatenate, axis=1)

        def chunk(cc, carry):
            c = n_local - 1 - cc
            rows = pl.ds(pl.multiple_of(c * CHUNK, CHUNK), CHUNK)
            qr = q_ref[rows, :]
            z = _gates(qr, f_ref[rows, :], lb, tril_b, first_half)
            qt = (z["q"] * z["e_q"]).astype(BF16)
            kt = (z["k"] * z["e_k"]).astype(BF16)
            qg_f = z["q"] * z["e_in"]
            qg = qg_f.astype(BF16)
            kg_f = z["k"] * z["e_out"]
            kg = kg_f.astype(BF16)
            vb = i_ref[rows, :].astype(BF16)
            dob = do_ref[rows, :].astype(BF16)
            st = [st_ref[c, h] for h in range(HEADS)]
            dst = [ds_scr[h] for h in range(HEADS)]
            dst_b = [d.astype(BF16) for d in dst]
            raw = [_dot_nt(qt[:, sl], kt[:, sl]) for sl in heads]
            draw = [_dot_nt(dob[:, sl], vb[:, sl]) for sl in heads]
            dqg = [_dot(dob[:, sl], st[h].astype(BF16)) for h, sl in enumerate(heads)]
            dkg = [_dot(vb[:, sl], dst_b[h]) for h, sl in enumerate(heads)]
            dv_inter = [_dot_nt(kg[:, sl], dst_b[h]) for h, sl in enumerate(heads)]
            grown = [_dot_tn(dob[:, sl], qg[:, sl]) for sl in heads]
            scores = [jnp.where(tril, r, 0.0).astype(BF16) for r in raw]
            dscores = [jnp.where(tril, r, 0.0).astype(BF16) for r in draw]
            dqt = [_dot(dscores[h], kt[:, sl]) for h, sl in enumerate(heads)]
            dkt = [_dot_tn(dscores[h], qt[:, sl]) for h, sl in enumerate(heads)]
            dv = [_dot_tn(scores[h], dob[:, sl]) + dv_inter[h] for h, sl in enumerate(heads)]
            carry_in = cat([z["e_last"][:, sl] * _colsum(dst[h] * st[h]) for h, sl in enumerate(heads)])
            for h, sl in enumerate(heads):
                ds_scr[h] = dst[h] * z["e_last"][:, sl] + grown[h]
            dqt, dkt, dqg, dkg = cat(dqt), cat(dkt), cat(dqg), cat(dkg)
            carry_in += _colsum(dkg * kg_f)
            dq = dqt * z["e_q"] + dqg * z["e_in"]
            dk = dkt * z["e_k"] + dkg * z["e_out"]
            dgate = (qt.astype(F32) * dqt - kt.astype(F32) * dkt) + (qg_f * dqg - kg_f * dkg)
            dlogf = _tri_sum(triu_b, dgate) + carry_in
            df = dlogf / z["f"] - dk
            sg = z["sg"]
            sq = z["sq"]
            acc_scr[...] += _colsum(df * (1.0 - sg))
            dqfi_ref[rows, 0:D_MODEL] = (dq * (sq * (1.0 + qr * (1.0 - sq)))).astype(BF16)
            dqfi_ref[rows, D_MODEL:2 * D_MODEL] = (df * (1.0 - lb) * sg * (1.0 - sg)).astype(BF16)
            dqfi_ref[rows, 2 * D_MODEL:3 * D_MODEL] = cat(dv).astype(BF16)
            return carry

        lax.fori_loop(0, n_local, chunk, 0, unroll=True)
        d0 = acc_scr[...] * lb * (1.0 - lb)
        dlb_ref[0:1, :] = d0
        dlb_ref[1:2, :] = -d0

    return _hosted(
        riders, body, name="hgrn_bwd", grid=(n_tiles,),
        in_specs=[rev(0), rev(1), rev(2), _resident((2, D_MODEL)),
                  pl.BlockSpec((n_local, HEADS, HEAD_DIM, HEAD_DIM), lambda i: (n_tiles - 1 - i, 0, 0, 0)),
                  rev(0)],
        out_specs=[pl.BlockSpec((tt, 3 * D_MODEL), lambda i: (n_tiles - 1 - i, 0)), _acc((2, D_MODEL))],
        out_shape=[jax.ShapeDtypeStruct((t, 3 * D_MODEL), BF16), jax.ShapeDtypeStruct((2, D_MODEL), F32)],
        scratch_shapes=[pltpu.VMEM((HEADS, HEAD_DIM, HEAD_DIM), F32), pltpu.VMEM((1, D_MODEL), F32)],
        compiler_params=_params(("arbitrary",)),
    )(main, main, main, lb_raw, states, do)


def _head_norm(o):
    rs, ns = [], []
    for h in range(HEADS):
        oh = o[:, h * HEAD_DIM:(h + 1) * HEAD_DIM]
        r, n = _rms(oh)
        rs.append(jnp.broadcast_to(r, oh.shape))
        ns.append(n)
    return jnp.concatenate(rs, axis=1), jnp.concatenate(ns, axis=1)


def _head_norm_bwd(dn, n, r):
    outs = []
    for h in range(HEADS):
        sl = slice(h * HEAD_DIM, (h + 1) * HEAD_DIM)
        outs.append(_rms_bwd(dn[:, sl], n[:, sl], r[:, sl]))
    return jnp.concatenate(outs, axis=1)


def _window_counts(first_row, tm):
    pos = (first_row + 1 + lax.broadcasted_iota(jnp.int32, (tm, 1), 0)).astype(F32)
    return [jnp.minimum(pos, float(w)) for w in POOL_WINDOWS]


def _post_fwd(o, main, pool_r, gates, x1, onorm, pool_w, pool_scale, sqw, wbw, tm, riders=()):
    t = o.shape[0]
    ext_rows = tm + POOL_HALO

    def body(o_ref, og_ref, u_ref, gt_ref, x1_ref, on_ref, pw_ref, ps_ref, wa_ref, wout_ref, wb_ref,
             x2_ref, ya_ref, yb_ref, pooled_ref, ext):
        i = pl.program_id(0)

        @pl.when(i == 0)
        def _():
            ext[0:POOL_HALO, :] = jnp.zeros((POOL_HALO, POOL_WIDTH), F32)

        _, n = _head_norm(o_ref[...])
        og = og_ref[...]
        oa = (n * on_ref[...] * (og * _sigmoid(og))).astype(BF16)
        ya = _dot(oa, wa_ref[...])

        u = u_ref[...]
        ext[POOL_HALO:ext_rows, :] = u
        e = ext[...]
        counts = _window_counts(i * tm, tm)
        pooled = []
        for gidx, w in enumerate(POOL_WINDOWS):
            s = e[:, gidx * HEAD_DIM:(gidx + 1) * HEAD_DIM]
            shift = 1
            while shift < w:
                s = s + pltpu.roll(s, shift, axis=0)
                shift *= 2
            pooled.append(s[POOL_HALO:, :] / counts[gidx] - u[:, gidx * HEAD_DIM:(gidx + 1) * HEAD_DIM])
        ext[0:POOL_HALO, :] = ext[tm:ext_rows, :]
        pooled_b = [pg.astype(BF16) for pg in pooled]
        pooled_ref[...] = jnp.concatenate(pooled_b, axis=1)
        mixed = jnp.concatenate([_dot(pooled_b[gidx], pw_ref[gidx].astype(BF16)) for gidx in range(len(POOL_WINDOWS))],
                                axis=1) * ps_ref[...]
        mixed_b = mixed.astype(BF16)
        yb = jnp.concatenate([_dot(mixed_b, wb_ref[j]) for j in range(N_CHIPS)], axis=1)

        gt = gt_ref[...]
        y = _sigmoid(gt[:, :D_MODEL]) * ya + _sigmoid(gt[:, D_MODEL:]) * yb
        x2_ref[...] = x1_ref[...] + _dot(y.astype(BF16), wout_ref[...])
        ya_ref[...] = ya.astype(BF16)
        yb_ref[...] = yb.astype(BF16)

    return _hosted(
        riders, body, name="post_fwd", grid=(t // tm,),
        in_specs=[_rows(tm, D_MODEL), _rows(tm, D_MODEL, 3), _rows(tm, POOL_WIDTH), _rows(tm, GATE_COLS), _rows(tm, D_MODEL),
                  _resident((1, D_MODEL)), _resident(pool_w.shape), _resident((1, POOL_WIDTH)),
                  _pick(sqw.shape, 0), _pick(sqw.shape, 1), _resident(wbw.shape)],
        out_specs=[_rows(tm, D_MODEL), _rows(tm, D_MODEL), _rows(tm, D_MODEL), _rows(tm, POOL_WIDTH)],
        out_shape=[jax.ShapeDtypeStruct((t, D_MODEL), F32), jax.ShapeDtypeStruct((t, D_MODEL), BF16),
                   jax.ShapeDtypeStruct((t, D_MODEL), BF16), jax.ShapeDtypeStruct((t, POOL_WIDTH), BF16)],
        scratch_shapes=[pltpu.VMEM((ext_rows, POOL_WIDTH), F32)],
        compiler_params=_params(("arbitrary",)),
    )(o, main, pool_r, gates, x1, onorm, pool_w, pool_scale, sqw, sqw, wbw)


def _post_bwd(dx2, o, main, gates, ya, yb, pooled, onorm, pool_w, pool_scale, sqw, wbw, tm, riders=()):
    t = o.shape[0]
    n_tiles = t // tm
    ext_rows = tm + POOL_HALO
    n_groups = len(POOL_WINDOWS)

    def rev(cols, col_block=0):
        return pl.BlockSpec((tm, cols), lambda i: (n_tiles - 1 - i, col_block))

    def body(dx2_ref, o_ref, og_ref, gt_ref, ya_ref, yb_ref, pooled_ref, on_ref, pw_ref, ps_ref, wa_ref, wout_ref, wb_ref,
             do_ref, dog_ref, du_ref, dgt_ref, dwa_ref, dwout_ref, dwb_ref, dpw_ref, dps_ref, don_ref, ext):
        i = pl.program_id(0)

        @pl.when(i == 0)
        def _():
            ext[tm:ext_rows, :] = jnp.zeros((POOL_HALO, POOL_WIDTH), F32)
            for ref in (dwa_ref, dwout_ref, dwb_ref, dpw_ref, dps_ref, don_ref):
                ref[...] = jnp.zeros_like(ref)

        groups = [slice(gidx * HEAD_DIM, (gidx + 1) * HEAD_DIM) for gidx in range(n_groups)]
        shards = [slice(j * 256, (j + 1) * 256) for j in range(N_CHIPS)]
        dx2b = dx2_ref[...].astype(BF16)
        dy = _dot_nt(dx2b, wout_ref[...])
        pooled_b = pooled_ref[...]
        pm = jnp.concatenate([_dot(pooled_b[:, g], pw_ref[gidx].astype(BF16)) for gidx, g in enumerate(groups)], axis=1)
        gt = gt_ref[...]
        sga = _sigmoid(gt[:, :D_MODEL])
        sgb = _sigmoid(gt[:, D_MODEL:])
        ya = ya_ref[...].astype(F32)
        yb = yb_ref[...].astype(F32)
        y = (sga * ya + sgb * yb).astype(BF16)
        dya = (dy * sga).astype(BF16)
        dyb = (dy * sgb).astype(BF16)
        dgt_ref[:, :D_MODEL] = (dy * ya * sga * (1.0 - sga)).astype(BF16)
        dgt_ref[:, D_MODEL:] = (dy * yb * sgb * (1.0 - sgb)).astype(BF16)
        dwout_ref[...] += _dot_tn(y, dx2b)
        doa = _dot_nt(dya, wa_ref[...])
        dmixed = _dot_nt(dyb[:, shards[0]], wb_ref[0])
        for j in range(1, N_CHIPS):
            dmixed += _dot_nt(dyb[:, shards[j]], wb_ref[j])
        r, n = _head_norm(o_ref[...])
        onv = on_ref[...]
        og = og_ref[...]
        sog = _sigmoid(og)
        silu_og = og * sog
        normed = n * onv
        oa = (normed * silu_og).astype(BF16)
        dog_ref[...] = (doa * normed * (sog * (1.0 + og * (1.0 - sog)))).astype(BF16)
        dnormed = doa * silu_og
        don_ref[...] += _colsum(dnormed * n)
        do_ref[...] = _head_norm_bwd(dnormed * onv, n, r)
        psv = ps_ref[...]
        mixed_b = (pm * psv).astype(BF16)
        dps_ref[...] += _colsum(dmixed * pm)
        dpm = (dmixed * psv).astype(BF16)
        dwa_ref[...] += _dot_tn(oa, dya)
        for j in range(N_CHIPS):
            dwb_ref[j] += _dot_tn(mixed_b, dyb[:, shards[j]])
        counts = _window_counts((n_tiles - 1 - i) * tm, tm)
        dpooled = []
        for gidx, g in enumerate(groups):
            dpw_ref[gidx] += _dot_tn(pooled_b[:, g], dpm[:, g])
            dpooled.append(_dot_nt(dpm[:, g], pw_ref[gidx].astype(BF16)))
        ext[0:tm, :] = jnp.concatenate([dpooled[gidx] / counts[gidx] for gidx in range(n_groups)], axis=1)
        e = ext[...]
        du = []
        for gidx, w in enumerate(POOL_WINDOWS):
            s = e[:, gidx * HEAD_DIM:(gidx + 1) * HEAD_DIM]
            shift = 1
            while shift < w:
                s = s + pltpu.roll(s, ext_rows - shift, axis=0)
                shift *= 2
            du.append(s[:tm, :] - dpooled[gidx])
        ext[tm:ext_rows, :] = ext[0:POOL_HALO, :]
        du_ref[...] = jnp.concatenate(du, axis=1).astype(BF16)

    wa_shape = (D_MODEL, D_MODEL)
    return _hosted(
        riders, body, name="post_bwd", grid=(n_tiles,),
        in_specs=[rev(D_MODEL), rev(D_MODEL), rev(D_MODEL, 3), rev(GATE_COLS), rev(D_MODEL), rev(D_MODEL), rev(POOL_WIDTH),
                  _resident((1, D_MODEL)), _resident(pool_w.shape), _resident((1, POOL_WIDTH)),
                  _pick(sqw.shape, 0), _pick(sqw.shape, 1), _resident(wbw.shape)],
        out_specs=[rev(D_MODEL), rev(D_MODEL), rev(POOL_WIDTH), rev(GATE_COLS),
                   _acc(wa_shape), _acc(wa_shape), _acc(wbw.shape), _acc(pool_w.shape), _acc((1, POOL_WIDTH)),
                   _acc((1, D_MODEL))],
        out_shape=[jax.ShapeDtypeStruct((t, D_MODEL), F32), jax.ShapeDtypeStruct((t, D_MODEL), BF16),
                   jax.ShapeDtypeStruct((t, POOL_WIDTH), BF16), jax.ShapeDtypeStruct((t, GATE_COLS), BF16),
                   jax.ShapeDtypeStruct(wa_shape, F32), jax.ShapeDtypeStruct(wa_shape, F32),
                   jax.ShapeDtypeStruct(wbw.shape, F32), jax.ShapeDtypeStruct(pool_w.shape, F32),
                   jax.ShapeDtypeStruct((1, POOL_WIDTH), F32), jax.ShapeDtypeStruct((1, D_MODEL), F32)],
        scratch_shapes=[pltpu.VMEM((ext_rows, POOL_WIDTH), F32)],
        compiler_params=_params(("arbitrary",)),
    )(dx2, o, main, gates, ya, yb, pooled, onorm, pool_w, pool_scale, sqw, sqw, wbw)


def _tail(x3, p, target, g_ple, g_post, g_final, sqw, wpw, tm, riders=()):
    t = x3.shape[0]
    pd = p.shape[1]

    def body(x_ref, p_ref, tg_ref, g4_ref, g5_ref, g6_ref, wg_ref, wp_ref,
             dx_ref, loss_ref, dwg_ref, dwp_ref, dg4_ref, dg5_ref, dg6_ref):
        @pl.when(pl.program_id(0) == 0)
        def _():
            for ref in (loss_ref, dwg_ref, dwp_ref, dg4_ref, dg5_ref, dg6_ref):
                ref[...] = jnp.zeros_like(ref)

        x3v = x_ref[...]
        g4, g5, g6 = g4_ref[...], g5_ref[...], g6_ref[...]
        r4, n4 = _rms(x3v)
        h4 = (n4 * g4).astype(BF16)
        gate = _sigmoid(_dot(h4, wg_ref[...]))
        pb = p_ref[...].astype(BF16)
        r5, n5 = _rms(jnp.concatenate([_dot(pb, wp_ref[j]) for j in range(N_CHIPS)], axis=1))
        emb = n5 * g5
        r6, n6 = _rms(x3v + gate * emb)
        diff = n6 * g6 - tg_ref[...]
        loss_ref[...] += 0.5 * jnp.sum(jnp.mean(diff * diff, axis=-1, keepdims=True), axis=0, keepdims=True)
        dout = diff * (1.0 / D_MODEL)
        dg6_ref[...] += _colsum(dout * n6)
        dx4 = _rms_bwd(dout * g6, n6, r6)
        demb = dx4 * gate
        dg5_ref[...] += _colsum(demb * n5)
        dpre = _rms_bwd(demb * g5, n5, r5).astype(BF16)
        for j in range(N_CHIPS):
            dwp_ref[j] += _dot_tn(pb, dpre[:, j * pd:(j + 1) * pd])
        dz = (dx4 * emb * gate * (1.0 - gate)).astype(BF16)
        dwg_ref[...] += _dot_tn(h4, dz)
        dh4 = _dot_nt(dz, wg_ref[...])
        dg4_ref[...] += _colsum(dh4 * n4)
        dx_ref[...] = dx4 + _rms_bwd(dh4 * g4, n4, r4)

    sq_shape = (D_MODEL, D_MODEL)
    vec = (1, D_MODEL)
    return _hosted(
        riders, body, name="tail", grid=(t // tm,),
        in_specs=[_rows(tm, D_MODEL), _rows(tm, pd), _rows(tm, D_MODEL), _resident(vec), _resident(vec), _resident(vec),
                  _pick(sqw.shape, 2), _resident(wpw.shape)],
        out_specs=[_rows(tm, D_MODEL), _acc((1, 1)), _acc(sq_shape), _acc(wpw.shape), _acc(vec), _acc(vec), _acc(vec)],
        out_shape=[jax.ShapeDtypeStruct((t, D_MODEL), F32), jax.ShapeDtypeStruct((1, 1), F32),
                   jax.ShapeDtypeStruct(sq_shape, F32), jax.ShapeDtypeStruct(wpw.shape, F32),
                   jax.ShapeDtypeStruct(vec, F32), jax.ShapeDtypeStruct(vec, F32), jax.ShapeDtypeStruct(vec, F32)],
        compiler_params=_params(("arbitrary",)),
    )(x3, p, target, g_ple, g_post, g_final, sqw, wpw)


def _position():
    return lax.axis_index("x"), lax.axis_index("y"), lax.axis_index("c")


def _other_chips(x, y):
    return [(1 - x, y), (x, 1 - y), (1 - x, 1 - y)]


def _remote(src, dst, send_sems, recv_sems, k, device):
    return pltpu.make_async_remote_copy(src_ref=src, dst_ref=dst, send_sem=send_sems.at[k], recv_sem=recv_sems.at[k],
                                        device_id=device, device_id_type=MESH)


def _gather_rider(shards, forward_at):
    n = len(shards)

    def copies(ins, outs, sems):
        send_sems, recv_sems = sems
        x, y, c = _position()
        mine = 2 * x + y
        first, passed, arriving = [], [], []
        for k, (cx, cy) in enumerate(_other_chips(x, y)):
            theirs = 2 * cx + cy
            for a in range(n):
                first.append(_remote(ins[a].at[:, c], outs[a].at[:, mine, c], send_sems, recv_sems, k * n + a, (cx, cy, c)))
                block = outs[a].at[:, theirs, c]
                passed.append(_remote(block, block, send_sems, recv_sems, (3 + k) * n + a, (x, y, 1 - c)))
                other = outs[a].at[:, theirs, 1 - c]
                arriving.append(_remote(other, other, send_sems, recv_sems, (3 + k) * n + a, (x, y, 1 - c)))
        return first, passed, arriving

    return _Rider(shards, [jax.ShapeDtypeStruct((s.shape[0], N_CHIPS) + s.shape[1:], s.dtype) for s in shards],
                  [pltpu.SemaphoreType.DMA((6 * n,)), pltpu.SemaphoreType.DMA((6 * n,))], _gather_phases(copies, forward_at))


def _gather_phases(copies, forward_at):
    def begin(ins, outs, sems):
        for cp in copies(ins, outs, sems)[0]:
            cp.start()

    def forward(ins, outs, sems):
        first, passed, _ = copies(ins, outs, sems)
        for got, cp in zip(first, passed):
            got.wait_recv()
            cp.start()

    def finish(ins, outs, sems):
        first, passed, arriving = copies(ins, outs, sems)
        for cp in arriving:
            cp.wait_recv()
        for cp in first + passed:
            cp.wait_send()

    return [(0, begin), (forward_at, forward), (1, finish)]


def _relay_gather_rider(shards):
    n = len(shards)
    kinds = 8

    def copies(ins, outs, sems):
        send_sems, recv_sems = sems
        x, y, c = _position()
        mine, nx, ny, diag = 2 * x + y, 2 * (1 - x) + y, 2 * x + 1 - y, 2 * (1 - x) + 1 - y
        to_x, to_y, sibling = (1 - x, y, c), (x, 1 - y, c), (x, y, 1 - c)
        sent, relayed, passed, arriving = [], [], [], []
        for a in range(n):
            hq = ins[a].shape[2] // 2
            quarter = lambda chip, half, q: outs[a].at[:, chip, half, pl.ds(q * hq, hq)]
            rc = functools.partial(_remote, send_sems=send_sems, recv_sems=recv_sems)
            sent += [rc(ins[a].at[:, c], outs[a].at[:, mine, c], k=kinds * a, device=to_x),
                     rc(ins[a].at[:, c], outs[a].at[:, mine, c], k=kinds * a + 1, device=to_y)]
            got = [rc(outs[a].at[:, nx, c], outs[a].at[:, nx, c], k=kinds * a, device=to_x),
                   rc(outs[a].at[:, ny, c], outs[a].at[:, ny, c], k=kinds * a + 1, device=to_y),
                   rc(quarter(diag, c, 0), quarter(diag, c, 0), k=kinds * a + 2, device=to_y),
                   rc(quarter(diag, c, 1), quarter(diag, c, 1), k=kinds * a + 3, device=to_x)]
            relayed += [rc(quarter(nx, c, 0), quarter(nx, c, 0), k=kinds * a + 2, device=to_y),
                        rc(quarter(ny, c, 1), quarter(ny, c, 1), k=kinds * a + 3, device=to_x)]
            blocks = [lambda h: outs[a].at[:, nx, h], lambda h: outs[a].at[:, ny, h],
                      lambda h: quarter(diag, h, 0), lambda h: quarter(diag, h, 1)]
            passed += [(got[i], rc(b(c), b(c), k=kinds * a + 4 + i, device=sibling)) for i, b in enumerate(blocks)]
            arriving += [rc(b(1 - c), b(1 - c), k=kinds * a + 4 + i, device=sibling) for i, b in enumerate(blocks)]
        return sent, relayed, passed, arriving

    def begin(ins, outs, sems):
        for cp in copies(ins, outs, sems)[0]:
            cp.start()

    def forward(ins, outs, sems):
        _, relayed, passed, _ = copies(ins, outs, sems)
        for i, (got, onward) in enumerate(passed):
            got.wait_recv()
            if i % 4 < 2:
                relayed[2 * (i // 4) + i % 4].start()
            onward.start()

    def finish(ins, outs, sems):
        sent, relayed, passed, arriving = copies(ins, outs, sems)
        for cp in arriving:
            cp.wait_recv()
        for cp in sent + relayed + [onward for _, onward in passed]:
            cp.wait_send()

    return _Rider(shards, [jax.ShapeDtypeStruct((s.shape[0], N_CHIPS) + s.shape[1:], s.dtype) for s in shards],
                  [pltpu.SemaphoreType.DMA((kinds * n,)), pltpu.SemaphoreType.DMA((kinds * n,))],
                  [(0, begin), (0.5, forward), (1, finish)])


def _with_own(gathered, shard, slot):
    return lax.dynamic_update_slice(gathered, shard[:, None], (0, slot, 0, 0, 0))


def _exchange_rider(arrays, out_shape, n_copies, transfers, n_local=0):
    def copies(ins, outs, sems):
        send_sems, recv_sems, local_sems = sems
        remote, local = transfers(ins, outs)
        return ([_remote(src, dst, send_sems, recv_sems, i, dev) for i, (src, dst, dev) in enumerate(remote)],
                [pltpu.make_async_copy(src, dst, local_sems.at[i]) for i, (src, dst) in enumerate(local)])

    def begin(ins, outs, sems):
        remote, local = copies(ins, outs, sems)
        for cp in remote + local:
            cp.start()

    def finish(ins, outs, sems):
        remote, local = copies(ins, outs, sems)
        for cp in remote:
            cp.wait_recv()
        for cp in remote:
            cp.wait_send()
        for cp in local:
            cp.wait()

    return _Rider(arrays, out_shape,
                  [pltpu.SemaphoreType.DMA((n_copies,)), pltpu.SemaphoreType.DMA((n_copies,)),
                   pltpu.SemaphoreType.DMA((max(n_local, 1),))],
                  [(0, begin), (1, finish)])


def _pair_rider(partials):
    def transfers(ins, outs):
        x, y, c = _position()
        return [(ins[a].at[:, :, 1 - c], outs[a], (x, y, 1 - c)) for a in range(len(partials))], []

    shapes = [jax.ShapeDtypeStruct(g.shape[:2] + g.shape[3:], g.dtype) for g in partials]
    return _exchange_rider(partials, shapes, len(partials), transfers)


def _chips_rider(sums):
    n = len(sums)

    def transfers(ins, outs):
        x, y, c = _position()
        return [(ins[a].at[:, 2 * cx + cy], outs[a].at[:, k], (cx, cy, c))
                for k, (cx, cy) in enumerate(_other_chips(x, y)) for a in range(n)], []

    shapes = [jax.ShapeDtypeStruct((q.shape[0], 3) + q.shape[2:], q.dtype) for q in sums]
    return _exchange_rider(sums, shapes, 3 * n, transfers)


def _share_rider(halves):
    def transfers(ins, outs):
        x, y, c = _position()
        return [(ins[a], outs[a], (x, y, 1 - c)) for a in range(len(halves))], []

    return _exchange_rider(halves, [jax.ShapeDtypeStruct(h.shape, h.dtype) for h in halves], len(halves), transfers)


def _small_rider(pack):
    flips = [(fx, fy, fc) for fx in (0, 1) for fy in (0, 1) for fc in (0, 1)][1:]

    def transfers(ins, outs):
        x, y, c = _position()
        slot = outs[0].at[4 * x + 2 * y + c]
        flip = lambda v, f: v + f - 2 * v * f
        return [(ins[0], slot, (flip(x, fx), flip(y, fy), flip(c, fc))) for fx, fy, fc in flips], [(ins[0], slot)]

    return _exchange_rider([pack], [jax.ShapeDtypeStruct((8,) + pack.shape, pack.dtype)], len(flips), transfers, n_local=1)


IN_HBM = pl.BlockSpec(memory_space=pltpu.HBM)
IN_SEM = pl.BlockSpec(memory_space=pltpu.SEMAPHORE)
SPLIT_COPY = pltpu.CompilerParams(has_side_effects=pltpu.SideEffectType.DATAFLOW_SIDE_EFFECTING)


def _start_copies(sources, landing_shapes, n_copies, plan, name):
    n_src, n_land = len(sources), len(landing_shapes)

    def body(*refs):
        srcs, lands, outs = refs[:n_src], refs[n_src:n_src + n_land], refs[n_src + n_land:]
        send, recv, token = outs[:n_copies], outs[n_copies:2 * n_copies], outs[-1]
        for i, (src, dst, device) in enumerate(plan(srcs, lands)):
            pltpu.make_async_remote_copy(src_ref=src, dst_ref=dst, send_sem=send[i], recv_sem=recv[i], device_id=device,
                                         device_id_type=MESH).start()
        token[...] = jnp.zeros_like(token)

    arrays = [pltpu.with_memory_space_constraint(a, pltpu.HBM) for a in sources]
    arrays += [pltpu.with_memory_space_constraint(lax.empty(s.shape, s.dtype), pltpu.HBM) for s in landing_shapes]
    res = pl.pallas_call(
        body, name=name,
        out_shape=[pltpu.SemaphoreType.DMA(())] * (2 * n_copies) + [pltpu.HBM(a.shape, a.dtype) for a in arrays]
        + [jax.ShapeDtypeStruct((8, 128), F32)],
        in_specs=[IN_HBM] * len(arrays),
        out_specs=[IN_SEM] * (2 * n_copies) + [IN_HBM] * len(arrays) + [pl.BlockSpec(memory_space=pltpu.VMEM)],
        input_output_aliases={i: 2 * n_copies + i for i in range(len(arrays))},
        compiler_params=SPLIT_COPY,
    )(*arrays)
    sems, rest = res[:2 * n_copies], res[2 * n_copies:]
    return sems, rest[:n_src], rest[n_src:n_src + n_land], rest[-1]


def _wait_copies(started, n_copies, plan, after, name):
    sems, sources, landings, _ = started
    n_src, n_land = len(sources), len(landings)

    def body(*refs):
        srcs, lands = refs[:n_src], refs[n_src:n_src + n_land]
        sem_refs = refs[n_src + n_land:n_src + n_land + 2 * n_copies]
        send, recv = sem_refs[:n_copies], sem_refs[n_copies:]
        for i, (src, dst, device) in enumerate(plan(srcs, lands)):
            cp = pltpu.make_async_remote_copy(src_ref=src, dst_ref=dst, send_sem=send[i], recv_sem=recv[i], device_id=device,
                                              device_id_type=MESH)
            cp.wait_send()
            cp.wait_recv()

    arrays = list(sources) + list(landings)
    res = pl.pallas_call(
        body, name=name, out_shape=[pltpu.HBM(a.shape, a.dtype) for a in arrays],
        in_specs=[IN_HBM] * len(arrays) + [IN_SEM] * (2 * n_copies) + [ANY], out_specs=[IN_HBM] * len(arrays),
        input_output_aliases={i: i for i in range(len(arrays))},
        compiler_params=SPLIT_COPY,
    )(*arrays, *sems, after)
    return res[n_src:]


def _chips_plan(n):
    def plan(srcs, lands):
        x, y, c = _position()
        return [(srcs[a].at[:, 2 * cx + cy], lands[a].at[:, k], (cx, cy, c))
                for k, (cx, cy) in enumerate(_other_chips(x, y)) for a in range(n)]
    return plan


def _alone(rider, name, after=()):
    return _hosted([rider], lambda *refs: None, name=name, in_specs=[ANY] * len(after), out_specs=[], out_shape=[])(*after)[1][0]


def _add_pair(mine, theirs, c, tag):
    n = len(mine)

    def body(c_ref, *refs):
        for a in range(n):
            refs[2 * n + a][...] = (refs[2 * a][...].astype(F32) + refs[2 * a + 1][...].astype(F32)).astype(BF16)

    in_specs, out_specs = [], []
    for got in theirs:
        l, _, hr, cols = got.shape
        in_specs += [pl.BlockSpec((l, None, None, hr, cols), lambda j, s: (0, j, s[0], 0, 0)),
                     pl.BlockSpec((l, None, hr, cols), lambda j, s: (0, j, 0, 0))]
        out_specs.append(pl.BlockSpec((l, None, hr, cols), lambda j, s: (0, j, 0, 0)))
    return pl.pallas_call(
        body, name=f"add_pair_{tag}",
        grid_spec=pltpu.PrefetchScalarGridSpec(num_scalar_prefetch=1, grid=(N_CHIPS,), in_specs=in_specs, out_specs=out_specs),
        out_shape=[jax.ShapeDtypeStruct(got.shape, BF16) for got in theirs],
        compiler_params=_params(("parallel",)),
    )(c.reshape(1), *[a for pair in zip(mine, theirs) for a in pair])


def _add_chips(parts, received, mine, tag):
    n = len(parts)

    def body(j_ref, *refs):
        for a in range(n):
            acc = refs[2 * a][...].astype(F32)
            for k in range(3):
                acc += refs[2 * a + 1][:, k].astype(F32)
            refs[2 * n + a][...] = acc

    in_specs, out_specs, out_shape = [], [], []
    for got in received:
        l, _, hr, cols = got.shape
        in_specs += [pl.BlockSpec((l, None, hr // 2, cols), lambda i, s: (0, s[0], i, 0)),
                     pl.BlockSpec((l, 3, hr // 2, cols), lambda i, s: (0, 0, i, 0))]
        out_specs.append(pl.BlockSpec((l, hr // 2, cols), lambda i, s: (0, i, 0)))
        out_shape.append(jax.ShapeDtypeStruct((l, hr, cols), F32))
    return pl.pallas_call(
        body, name=f"add_chips_{tag}",
        grid_spec=pltpu.PrefetchScalarGridSpec(num_scalar_prefetch=1, grid=(2,), in_specs=in_specs, out_specs=out_specs),
        out_shape=out_shape,
        compiler_params=_params(("parallel",)),
    )(mine.reshape(1), *[a for pair in zip(parts, received) for a in pair])


def _adam_update(w, g, m, v):
    m2 = ADAM_B1 * m + (1.0 - ADAM_B1) * g
    v2 = ADAM_B2 * v + (1.0 - ADAM_B2) * jnp.square(g)
    m_hat = m2 / (1.0 - ADAM_B1 ** ADAM_STEP)
    v_hat = v2 / (1.0 - ADAM_B2 ** ADAM_STEP)
    return -ADAM_LR * (m_hat / (jnp.sqrt(v_hat) + ADAM_EPS) + ADAM_WD * w), m2, v2


def _adamw_group(items, tag, after=()):
    n = len(items)

    def body(*refs):
        ins, outs = refs[:5 * n], refs[5 * n + len(after):]
        mine = pl.program_id(0) == lax.axis_index("c")
        for a in range(n):
            w_ref, own_ref, other_ref, m_ref, v_ref = ins[5 * a:5 * a + 5]
            g_ref, d_ref, nm_ref, nv_ref = outs[4 * a:4 * a + 4]
            gv = jnp.where(mine, own_ref[...], other_ref[...])
            g_ref[...] = gv
            d_ref[...], nm_ref[...], nv_ref[...] = _adam_update(w_ref[...], gv, m_ref[...], v_ref[...])

    in_specs, out_specs, out_shape, args = [], [], [], []
    for w, own, other, m, v in items:
        _, hr, cols = w.shape
        tr = hr // ADAM_BLOCKS
        full = pl.BlockSpec((None, tr, cols), lambda h, i: (h, i, 0))
        half = pl.BlockSpec((tr, cols), lambda h, i: (i, 0))
        in_specs += [full, half, half, full, full]
        out_specs += [full] * 4
        out_shape += [jax.ShapeDtypeStruct((2, hr, cols), F32)] * 4
        args += [w, own, other, m, v]
    outs = pl.pallas_call(body, name=f"adamw_{tag}", grid=(2, ADAM_BLOCKS), in_specs=in_specs + [ANY] * len(after),
                          out_specs=out_specs, out_shape=out_shape,
                          compiler_params=_params(("parallel", "parallel")))(*args, *after)
    return [outs[4 * a:4 * a + 4] for a in range(n)]


def _adamw_small(w, gathered, m, v, shapes):
    n_rows = w.shape[0]
    places = []
    for i, name in enumerate(VECTOR_PARAMS):
        places.append((name, i * TILE_ROWS, 1 if len(shapes[name]) == 1 else shapes[name][0], shapes[name][-1]))
    places.append(("pool_w", len(VECTOR_PARAMS) * TILE_ROWS, n_rows - len(VECTOR_PARAMS) * TILE_ROWS, D_MODEL))

    def body(w_ref, g_ref, m_ref, v_ref, loss_ref, *rest):
        outs, (sum_scr, d_scr, nm_scr, nv_scr) = rest[:-4], rest[-4:]
        total = g_ref[0]
        for i in range(1, g_ref.shape[0]):
            total += g_ref[i]
        sum_scr[...] = total
        gv = sum_scr[0:n_rows, :]
        d_scr[...], nm_scr[...], nv_scr[...] = _adam_update(w_ref[...], gv, m_ref[...], v_ref[...])
        loss_ref[...] = sum_scr[n_rows:n_rows + 1, 0:1]
        for k, (_, first, rows, cols) in enumerate(places):
            for j, scr in enumerate((sum_scr, d_scr, nm_scr, nv_scr)):
                outs[4 * k + j][...] = scr[first:first + rows, 0:cols]

    out_shape = [jax.ShapeDtypeStruct((1, 1), F32)]
    for _, _, rows, cols in places:
        out_shape += [jax.ShapeDtypeStruct((rows, cols), F32)] * 4
    res = pl.pallas_call(
        body, name="adamw_small", out_shape=out_shape,
        scratch_shapes=[pltpu.VMEM(gathered.shape[1:], F32)] + [pltpu.VMEM(w.shape, F32)] * 3,
        compiler_params=_params())(w, gathered, m, v)
    return res[0], {name: res[1 + 4 * k:5 + 4 * k] for k, (name, _, _, _) in enumerate(places)}


VECTOR_PARAMS = ("ffn1_norm", "mix_norm", "hgrn_lb", "hgrn_onorm", "ffn2_norm", "ple_norm", "ple_post_norm", "final_norm",
                 "pool_scale")
ALL_PARAMS = ("ffn1_norm", "ffn1_w1", "ffn1_w3", "ffn1_w2", "mix_norm", "w_in", "hgrn_lb", "hgrn_onorm", "w_branch_a",
              "pool_w", "pool_scale", "w_branch_b", "w_out", "ffn2_norm", "ffn2_w1", "ffn2_w3", "ffn2_w2", "ple_norm",
              "ple_w_gate", "ple_w_proj", "ple_post_norm", "final_norm")
TILE_ROWS = 8


def _pack_small(values, loss=None):
    tile = lambda a: jnp.pad(a, ((0, TILE_ROWS - a.shape[0]), (0, D_MODEL - a.shape[1])))
    parts = [tile(values[name].reshape(-1, values[name].shape[-1])) for name in VECTOR_PARAMS]
    parts.append(values["pool_w"].reshape(-1, D_MODEL))
    if loss is not None:
        parts.append(tile(loss))
    return jnp.concatenate(parts, axis=0)


def _halved(a, lead):
    return a.reshape(lead, 2, -1, a.shape[-1])


def _shard_halves(a, lead):
    return a.reshape(lead, N_CHIPS, 2, -1, a.shape[-1])


REDUCED_TRANSPOSED = ("ffn1_w1", "ffn1_w3", "ffn2_w1", "ffn2_w3")


def _entries(arrays):
    return [a[i] for a in arrays for i in range(a.shape[0])]


def _adam_items(names, own, other, w, m, v):
    items = []
    for name, g_own, g_other in zip(names, _entries(own), _entries(other)):
        view = (lambda a: _halved(a[0].T, 1)[0]) if name in REDUCED_TRANSPOSED else (lambda a: _halved(a, 1)[0])
        items.append((view(w[name]), g_own, g_other, view(m[name]), view(v[name])))
    return items


def _adam_store(names, results, w, out):
    for name, res in zip(names, results):
        shape = w[name].shape
        if name in REDUCED_TRANSPOSED:
            back = [a.reshape(shape[2], shape[1]).T.reshape(shape) for a in res]
        else:
            back = [a.reshape(shape) for a in res]
        out["grad"][name], out["delta"][name], out["new_m"][name], out["new_v"][name] = back


def kernel(x, p, ffn1_norm, ffn1_w1, ffn1_w3, ffn1_w2, mix_norm, w_in, hgrn_lb, hgrn_onorm, w_branch_a, pool_w, pool_scale, w_branch_b, w_out, ffn2_norm, ffn2_w1, ffn2_w3, ffn2_w2, ple_norm, ple_w_gate, ple_w_proj, ple_post_norm, final_norm, loss_target, m_ffn1_norm, m_ffn1_w1, m_ffn1_w3, m_ffn1_w2, m_mix_norm, m_w_in, m_hgrn_lb, m_hgrn_onorm, m_w_branch_a, m_pool_w, m_pool_scale, m_w_branch_b, m_w_out, m_ffn2_norm, m_ffn2_w1, m_ffn2_w3, m_ffn2_w2, m_ple_norm, m_ple_w_gate, m_ple_w_proj, m_ple_post_norm, m_final_norm, v_ffn1_norm, v_ffn1_w1, v_ffn1_w3, v_ffn1_w2, v_mix_norm, v_w_in, v_hgrn_lb, v_hgrn_onorm, v_w_branch_a, v_pool_w, v_pool_scale, v_w_branch_b, v_w_out, v_ffn2_norm, v_ffn2_w1, v_ffn2_w3, v_ffn2_w2, v_ple_norm, v_ple_w_gate, v_ple_w_proj, v_ple_post_norm, v_final_norm):
    args = dict(locals())
    w = {name: args[name] for name in ALL_PARAMS}
    m = {name: args["m_" + name] for name in ALL_PARAMS}
    v = {name: args["v_" + name] for name in ALL_PARAMS}
    cx, cy, cc = _position()
    chip = (2 * cx + cy).astype(jnp.int32)
    core = cc.astype(jnp.int32)
    xs, ps, target = x[0], p[0, 0], loss_target[0]
    t = xs.shape[0]
    tm = min(256, t)
    tm_ffn = min(512, t)
    tt = min(512, t)
    tk = min(2048, t)
    small = {name: w[name] for name in VECTOR_PARAMS}
    small["final_norm"] = w["final_norm"].reshape(1, D_MODEL)
    pool_w0 = w["pool_w"][0]

    ffn_shard = lambda i: _halved(jnp.stack([w[f"ffn{i}_w1"][0].T, w[f"ffn{i}_w3"][0].T, w[f"ffn{i}_w2"][0]]).astype(BF16), 3)
    sq_shard = _halved(jnp.stack([w["w_branch_a"][0], w["w_out"][0], w["ple_w_gate"][0]]).astype(BF16), 3)
    win_shard, wb_shard, wp_shard = (_halved(w[n].astype(BF16), 1) for n in ("w_in", "w_branch_b", "ple_w_proj"))

    ffn1_shard, ffn2_shard = ffn_shard(1), ffn_shard(2)
    (ffn1w,) = _alone(_relay_gather_rider([ffn1_shard]), "gather_ffn1")
    ffn1w = _with_own(ffn1w, ffn1_shard, chip).reshape(3, D_FF, D_MODEL)
    (x1, a1, b1), ((winw,),) = _ffn_fwd(xs, small["ffn1_norm"], ffn1w, 1, tm_ffn, [_gather_rider([win_shard], 0.6)])
    winw = _with_own(winw, win_shard, chip).reshape(N_CHIPS, D_MODEL, SHARD_IN_COLS)
    winw = winw.transpose(1, 0, 2).reshape(D_MODEL, N_CHIPS * SHARD_IN_COLS)
    (main, pool_r, gates), ((ffn2w,),) = _mix_fwd(x1, small["mix_norm"], winw, tm, [_gather_rider([ffn2_shard], 0.75)])
    ffn2w = _with_own(ffn2w, ffn2_shard, chip).reshape(3, D_FF, D_MODEL)
    (o, states), ((sqw, wbw, wpw),) = _hgrn_fwd(main, small["hgrn_lb"], tt,
                                                 [_gather_rider([sq_shard, wb_shard, wp_shard], 0.5)])
    sqw = _with_own(sqw, sq_shard, chip).reshape(3, D_MODEL, D_MODEL)
    wbw = _with_own(wbw, wb_shard, chip).reshape(N_CHIPS, POOL_WIDTH, -1)
    wpw = _with_own(wpw, wp_shard, chip).reshape(N_CHIPS, ps.shape[1], -1)
    (x2, ya, yb, pooled), _ = _post_fwd(o, main, pool_r, gates, x1, small["hgrn_onorm"], pool_w0, small["pool_scale"], sqw,
                                       wbw, tm)
    (x3, a2, b2), _ = _ffn_fwd(x2, small["ffn2_norm"], ffn2w, 2, tm_ffn)
    (dx3, loss, d_wg, d_wp, d_ple, d_post, d_final), _ = _tail(
        x3, ps, target, small["ple_norm"], small["ple_post_norm"], small["final_norm"], sqw, wpw, tm_ffn)

    add_pairs = lambda parts, got, group: _add_pair(parts, got, core, group)
    add_chips = lambda sums, got, group: _add_chips(sums, got, chip, group)
    names1 = ("ffn2_w1", "ffn2_w3", "ffn2_w2", "ple_w_gate", "ple_w_proj")
    names2 = ("w_branch_a", "w_out", "w_branch_b")
    names3 = ("w_in",)
    names4 = ("ffn1_w1", "ffn1_w3")
    names5 = ("ffn1_w2",)
    tags1, tags2, tags3, tags4, tags5 = "ffn2", "branches", "w_in", "ffn1_in", "ffn1_out"

    (dx2, dab2, s2, h3, dxh2, d_ffn2_norm), _ = _ffn_bwd(dx3, x2, small["ffn2_norm"], a2, b2, ffn2w, 2, tm)
    (d_w13_2,), _ = _wgrad(dab2, h3, WGRAD_IN_BLOCKS, "wgrad_ffn2_in", tk)
    (d_w2_2,), _ = _wgrad(s2, dxh2, WGRAD_OUT_BLOCKS, "wgrad_ffn2_out", tk)
    part1 = [_shard_halves(d_w13_2, 2), _shard_halves(d_w2_2, 1), _shard_halves(d_wg, 1), _shard_halves(d_wp, 1)]
    (do, dog, du, dgates, d_wa, d_wout, d_wb, d_pool_w, d_pool_scale, d_onorm), (sib1,) = _post_bwd(
        dx2, o, main, gates, ya, yb, pooled, small["hgrn_onorm"], pool_w0, small["pool_scale"], sqw, wbw, tm,
        [_pair_rider(part1)])
    sums1 = add_pairs(part1, sib1, tags1)
    part2 = [_shard_halves(d_wa, 1), _shard_halves(d_wout, 1), _shard_halves(d_wb, 1)]
    (dqfi, d_lb), (got1, sib2) = _hgrn_bwd(main, small["hgrn_lb"], states, do, tt, [_chips_rider(sums1), _pair_rider(part2)])
    own1 = add_chips(sums1, got1, tags1)
    sums2 = add_pairs(part2, sib2, tags2)
    (dx1, dproj, h2, d_mix_norm), (other1, got2) = _mix_bwd(dqfi, dog, du, dgates, dx2, x1, small["mix_norm"], winw, tm,
                                                            [_share_rider(own1), _chips_rider(sums2)])
    own2 = add_chips(sums2, got2, tags2)
    (d_win,), (other2,) = _wgrad_cols(h2, dproj, N_CHIPS, "wgrad_in", tk, [_share_rider(own2)])
    part3 = [_shard_halves(d_win, 1)]
    (dx, dab1, s1, h1, dxh1, d_ffn1_norm), _ = _ffn_bwd(dx1, xs, small["ffn1_norm"], a1, b1, ffn1w, 1, tm)
    vecs = dict(ffn1_norm=d_ffn1_norm, mix_norm=d_mix_norm, hgrn_lb=d_lb, hgrn_onorm=d_onorm, ffn2_norm=d_ffn2_norm,
                ple_norm=d_ple, ple_post_norm=d_post, final_norm=d_final, pool_scale=d_pool_scale, pool_w=d_pool_w)
    (d_w2_1,), (sib3, (small_all,)) = _wgrad(s1, dxh1, WGRAD_OUT_BLOCKS, "wgrad_ffn1_out", tk,
                                             [_pair_rider(part3), _small_rider(_pack_small(vecs, loss))])
    sums3 = add_pairs(part3, sib3, tags3)
    part5 = [_shard_halves(d_w2_1, 1)]
    (d_w13_1,), (got3, sib5) = _wgrad(dab1, h1, WGRAD_IN_BLOCKS, "wgrad_ffn1_in", tk,
                                      [_chips_rider(sums3), _pair_rider(part5)])
    own3 = add_chips(sums3, got3, tags3)
    sums5 = add_pairs(part5, sib5, tags5)
    part4 = [_shard_halves(d_w13_1, 2)]
    landing = lambda a: jax.ShapeDtypeStruct((a.shape[0], 3) + a.shape[2:], a.dtype)

    def sibling_plan(srcs, lands):
        x, y, c = _position()
        return [(srcs[-1], lands[-1], (x, y, 1 - c))]

    plan_a = lambda srcs, lands: _chips_plan(1)(srcs[:1], lands[:1]) + sibling_plan(srcs, lands)
    started_a = _start_copies([sums5[0], own3[0]], [landing(sums5[0]), own3[0]], 4, plan_a, "start_a")
    sib4 = _alone(_pair_rider(part4), "pair_last", [started_a[3]])
    sums4 = add_pairs(part4, sib4, tags4)
    started_b = _start_copies([sums4[0]], [landing(sums4[0])], 3, _chips_plan(1), "start_b")

    out = dict(grad={}, delta={}, new_m={}, new_v={})
    results = _adamw_group(_adam_items(names1 + names2, own1 + own2, other1 + other2, w, m, v), "early",
                           [started_a[3], started_b[3]])
    _adam_store(names1 + names2, results, w, out)
    got5, other3 = _wait_copies(started_a, 4, plan_a, results[0][1], "wait_a")
    results = _adamw_group(_adam_items(names3, own3, [other3], w, m, v), "w_in")
    _adam_store(names3, results, w, out)
    (got4,) = _wait_copies(started_b, 3, _chips_plan(1), results[0][1], "wait_b")
    own4 = add_chips(sums4, [got4], tags4)
    own5 = add_chips(sums5, [got5], tags5)
    other4, other5 = _alone(_share_rider(own4 + own5), "share_last")
    results = _adamw_group(_adam_items(names4 + names5, own4 + own5, [other4, other5], w, m, v), "ffn1")
    _adam_store(names4 + names5, results, w, out)

    shapes = {name: w[name].shape for name in VECTOR_PARAMS + ("pool_w",)}
    loss, results = _adamw_small(_pack_small(w), small_all, _pack_small(m), _pack_small(v), shapes)
    for name, res in results.items():
        out["grad"][name], out["delta"][name], out["new_m"][name], out["new_v"][name] = (a.reshape(shapes[name]) for a in res)

    return (loss[0, 0], dx[None], *[out["grad"][n] for n in ALL_PARAMS], *[out["delta"][n] for n in ALL_PARAMS],
            *[out["new_m"][n] for n in ALL_PARAMS], *[out["new_v"][n] for n in ALL_PARAMS])
```
